```python
import jax, jax.numpy as jnp
from jax import lax
import numpy as np

D_MODEL = 2048
BATCH = 8
SEQ = 8192
DEPTH = 1

CHUNK = 64
LEFT_CHUNKS = 8
BAND = (LEFT_CHUNKS + 1) * CHUNK
ATT_HEADS = 16
ATT_HEAD_DIM = 64
ATT_WIDTH = ATT_HEADS * ATT_HEAD_DIM
REL_CLIP = 256
REL_FUTURE = CHUNK - 1
N_REL = REL_CLIP + REL_FUTURE + 1
SSD_HEADS = 16
SSD_HEAD_DIM = 64
SSD_WIDTH = SSD_HEADS * SSD_HEAD_DIM
SSD_GROUPS = 2
SSD_STATE = 128
SSD_CONV = 4
SSD_CHUNK = CHUNK
XBC_WIDTH = SSD_WIDTH + 2 * SSD_GROUPS * SSD_STATE
MIX_WIDTH = ATT_WIDTH + SSD_WIDTH
IN_COLS = 3 * ATT_WIDTH + SSD_WIDTH + XBC_WIDTH + SSD_HEADS
FFN_HIDDEN = -(-8 * D_MODEL // (3 * 256)) * 256
N_MOD = 6
EPS = 1e-6

kernel_name = "hymba_chunk_attn_ssd_adaln"


def rmsnorm(x, g):
    xf = x.astype(jnp.float32)
    y = xf * lax.rsqrt(jnp.mean(xf * xf, axis=-1, keepdims=True) + EPS)
    return (y * g.astype(jnp.float32)).astype(x.dtype)


def chunk_attention(q, k, v, rel_bias):
    b, s, h, dh = q.shape
    n_chunks = s // CHUNK
    pad = LEFT_CHUNKS * CHUNK
    k_pad = jnp.pad(k, ((0, 0), (pad, 0), (0, 0), (0, 0)))
    v_pad = jnp.pad(v, ((0, 0), (pad, 0), (0, 0), (0, 0)))
    q_loc = jnp.arange(CHUNK)[:, None] + pad
    k_loc = jnp.arange(BAND)[None, :]
    rel_idx = jnp.clip(q_loc - k_loc, -REL_FUTURE, REL_CLIP) + REL_FUTURE
    bias = rel_bias[:, rel_idx].astype(jnp.float32)
    scale = ATT_HEAD_DIM ** -0.5

    def one_chunk(i):
        start = i * CHUNK
        qc = lax.dynamic_slice_in_dim(q, start, CHUNK, axis=1)
        kc = lax.dynamic_slice_in_dim(k_pad, start, BAND, axis=1)
        vc = lax.dynamic_slice_in_dim(v_pad, start, BAND, axis=1)
        scores = jnp.einsum("bqhd,bkhd->bhqk", qc, kc).astype(jnp.float32) * scale + bias
        valid = (start - pad + jnp.arange(BAND)) >= 0
        scores = jnp.where(valid, scores, -jnp.inf)
        probs = jax.nn.softmax(scores, axis=-1).astype(vc.dtype)
        return jnp.einsum("bhqk,bkhd->bqhd", probs, vc)

    out = lax.map(one_chunk, jnp.arange(n_chunks))
    return jnp.moveaxis(out, 0, 1).reshape(b, s, h * dh)


def ssd_mixer(xbc_raw, z, dt_raw, conv_w, conv_b, dt_bias, a_log, d_skip, norm_g):
    b, s, ch = xbc_raw.shape
    xbc = lax.conv_general_dilated(
        xbc_raw, conv_w[:, None, :].astype(xbc_raw.dtype), window_strides=(1,),
        padding=[(SSD_CONV - 1, 0)], dimension_numbers=("NWC", "WIO", "NWC"),
        feature_group_count=ch) + conv_b
    xbc = jax.nn.silu(xbc)
    xs, bm, cm = jnp.split(xbc, [SSD_WIDTH, SSD_WIDTH + SSD_GROUPS * SSD_STATE], axis=-1)
    nc = s // SSD_CHUNK
    r = SSD_HEADS // SSD_GROUPS
    dt = jax.nn.softplus((dt_raw + dt_bias).astype(jnp.float32))
    a = -jnp.exp(a_log.astype(jnp.float32))
    x = xs.reshape(b, nc, SSD_CHUNK, SSD_GROUPS, r, SSD_HEAD_DIM)
    bm = bm.reshape(b, nc, SSD_CHUNK, SSD_GROUPS, SSD_STATE)
    cm = cm.reshape(b, nc, SSD_CHUNK, SSD_GROUPS, SSD_STATE)
    dt = dt.reshape(b, nc, SSD_CHUNK, SSD_GROUPS, r)
    xdt = x * dt[..., None].astype(x.dtype)
    a_dt = jnp.moveaxis(dt * a.reshape(SSD_GROUPS, r), 2, -1)
    cs = jnp.cumsum(a_dt, axis=-1)
    causal = jnp.tril(jnp.ones((SSD_CHUNK, SSD_CHUNK), dtype=bool))
    seg = jnp.exp(jnp.where(causal, cs[..., :, None] - cs[..., None, :], -jnp.inf))
    cb = jnp.einsum("bclgn,bcsgn->bcgls", cm, bm)
    y_diag = jnp.einsum("bcgls,bcgrls,bcsgrp->bclgrp", cb, seg, xdt)
    decay = jnp.exp(cs[..., -1:] - cs)
    states = jnp.einsum("bclgn,bcgrl,bclgrp->bcgrpn", bm, decay, xdt).astype(jnp.float32)
    chunk_decay = jnp.exp(cs[..., -1])

    def step(h, inp):
        st, dec = inp
        return dec[..., None, None] * h + st, h

    h0 = jnp.zeros((b, SSD_GROUPS, r, SSD_HEAD_DIM, SSD_STATE), jnp.float32)
    _, prev = lax.scan(step, h0, (jnp.moveaxis(states, 1, 0), jnp.moveaxis(chunk_decay, 1, 0)))
    prev = jnp.moveaxis(prev, 0, 1)
    y_off = jnp.einsum("bclgn,bcgrpn,bcgrl->bclgrp", cm, prev, jnp.exp(cs))
    y = y_diag + y_off + x * d_skip.reshape(SSD_GROUPS, r)[:, :, None]
    y = y.reshape(b, s, SSD_WIDTH).astype(xs.dtype)
    return rmsnorm(y * jax.nn.silu(z), norm_g)


def _fwd_setup_inputs(seed: int = 0) -> dict:
    key = jax.random.key(seed)
    ks = jax.random.split(key, 24)
    f32 = jnp.float32
    nrm = lambda k, shape, s: jax.random.normal(k, shape, f32) * s
    gain = lambda k, shape: 1.0 + 0.01 * jax.random.normal(k, shape, f32)
    dt0 = jnp.exp(jax.random.uniform(ks[10], (DEPTH, SSD_HEADS), f32,
                                     jnp.log(1e-3), jnp.log(1e-1)))
    return {
        "x": nrm(ks[0], (BATCH, SEQ, D_MODEL), 1.0),
        "c": nrm(ks[1], (BATCH, D_MODEL), 1.0),
        "w_ada": nrm(ks[2], (DEPTH, D_MODEL, N_MOD * D_MODEL), D_MODEL ** -0.5),
        "b_ada": nrm(ks[3], (DEPTH, N_MOD * D_MODEL), 0.01),
        "g_mix": gain(ks[4], (DEPTH, D_MODEL)),
        "w_in": nrm(ks[5], (DEPTH, D_MODEL, IN_COLS), D_MODEL ** -0.5),
        "rel_bias": nrm(ks[6], (DEPTH, ATT_HEADS, N_REL), 0.5),
        "conv_w": nrm(ks[7], (DEPTH, SSD_CONV, XBC_WIDTH), SSD_CONV ** -0.5),
        "conv_b": nrm(ks[8], (DEPTH, XBC_WIDTH), 0.01),
        "dt_bias": dt0 + jnp.log(-jnp.expm1(-dt0)),
        "a_log": jnp.log(jax.random.uniform(ks[11], (DEPTH, SSD_HEADS), f32, 1.0, 16.0)),
        "d_skip": gain(ks[12], (DEPTH, SSD_HEADS)),
        "g_att_out": gain(ks[13], (DEPTH, ATT_WIDTH)),
        "g_ssd_out": gain(ks[14], (DEPTH, SSD_WIDTH)),
        "w_out": nrm(ks[15], (DEPTH, MIX_WIDTH, D_MODEL), MIX_WIDTH ** -0.5),
        "g_ffn": gain(ks[16], (DEPTH, D_MODEL)),
        "w_gate": nrm(ks[17], (DEPTH, D_MODEL, FFN_HIDDEN), D_MODEL ** -0.5),
        "w_up": nrm(ks[18], (DEPTH, D_MODEL, FFN_HIDDEN), D_MODEL ** -0.5),
        "w_down": nrm(ks[19], (DEPTH, FFN_HIDDEN, D_MODEL), FFN_HIDDEN ** -0.5),
        "g_final": gain(ks[20], (D_MODEL,)),
    }


def _fwd_reference(x, c, w_ada, b_ada, g_mix, w_in, rel_bias, conv_w, conv_b, dt_bias, a_log,
              d_skip, g_att_out, g_ssd_out, w_out, g_ffn, w_gate, w_up, w_down, g_final):
    b, s, _ = x.shape
    cond = jax.nn.silu(c)
    splits = [ATT_WIDTH, 2 * ATT_WIDTH, 3 * ATT_WIDTH, 3 * ATT_WIDTH + SSD_WIDTH,
              3 * ATT_WIDTH + SSD_WIDTH + XBC_WIDTH]
    for l in range(DEPTH):
        mods = cond @ w_ada[l] + b_ada[l]
        sh1, sc1, gt1, sh2, sc2, gt2 = [m[:, None, :] for m in jnp.split(mods, N_MOD, axis=-1)]
        h = rmsnorm(x, g_mix[l]) * (1.0 + sc1) + sh1
        proj = h @ w_in[l]
        q, k, v, z, xbc, dt_raw = jnp.split(proj, splits, axis=-1)
        att = chunk_attention(q.reshape(b, s, ATT_HEADS, ATT_HEAD_DIM),
                              k.reshape(b, s, ATT_HEADS, ATT_HEAD_DIM),
                              v.reshape(b, s, ATT_HEADS, ATT_HEAD_DIM), rel_bias[l])
        att = rmsnorm(att, g_att_out[l])
        ssd = ssd_mixer(xbc, z, dt_raw, conv_w[l], conv_b[l], dt_bias[l], a_log[l],
                        d_skip[l], g_ssd_out[l])
        mix = jnp.concatenate([att, ssd], axis=-1) @ w_out[l]
        x = x + gt1 * mix
        h = rmsnorm(x, g_ffn[l]) * (1.0 + sc2) + sh2
        ffn = (jax.nn.silu(h @ w_gate[l]) * (h @ w_up[l])) @ w_down[l]
        x = x + gt2 * ffn
    return rmsnorm(x, g_final)


import jax as _jax
import jax.numpy as _jnp

TWIN_FORMAT = 'train_step'
FWD_PARAMS = ['x', 'c', 'w_ada', 'b_ada', 'g_mix', 'w_in', 'rel_bias', 'conv_w', 'conv_b', 'dt_bias', 'a_log', 'd_skip', 'g_att_out', 'g_ssd_out', 'w_out', 'g_ffn', 'w_gate', 'w_up', 'w_down', 'g_final']
TWIN_WEIGHTS = ['w_ada', 'b_ada', 'g_mix', 'w_in', 'rel_bias', 'conv_w', 'conv_b', 'dt_bias', 'a_log', 'd_skip', 'g_att_out', 'g_ssd_out', 'w_out', 'g_ffn', 'w_gate', 'w_up', 'w_down', 'g_final']
TWIN_DIFF_INPUT = 'x'
TWIN_INPUTS = ['x', 'c', 'w_ada', 'b_ada', 'g_mix', 'w_in', 'rel_bias', 'conv_w', 'conv_b', 'dt_bias', 'a_log', 'd_skip', 'g_att_out', 'g_ssd_out', 'w_out', 'g_ffn', 'w_gate', 'w_up', 'w_down', 'g_final', 'loss_target', 'm_w_ada', 'm_b_ada', 'm_g_mix', 'm_w_in', 'm_rel_bias', 'm_conv_w', 'm_conv_b', 'm_dt_bias', 'm_a_log', 'm_d_skip', 'm_g_att_out', 'm_g_ssd_out', 'm_w_out', 'm_g_ffn', 'm_w_gate', 'm_w_up', 'm_w_down', 'm_g_final', 'v_w_ada', 'v_b_ada', 'v_g_mix', 'v_w_in', 'v_rel_bias', 'v_conv_w', 'v_conv_b', 'v_dt_bias', 'v_a_log', 'v_d_skip', 'v_g_att_out', 'v_g_ssd_out', 'v_w_out', 'v_g_ffn', 'v_w_gate', 'v_w_up', 'v_w_down', 'v_g_final']
TWIN_OUTPUTS = ['loss', 'grad_x', 'grad_w_ada', 'grad_b_ada', 'grad_g_mix', 'grad_w_in', 'grad_rel_bias', 'grad_conv_w', 'grad_conv_b', 'grad_dt_bias', 'grad_a_log', 'grad_d_skip', 'grad_g_att_out', 'grad_g_ssd_out', 'grad_w_out', 'grad_g_ffn', 'grad_w_gate', 'grad_w_up', 'grad_w_down', 'grad_g_final', 'delta_w_ada', 'delta_b_ada', 'delta_g_mix', 'delta_w_in', 'delta_rel_bias', 'delta_conv_w', 'delta_conv_b', 'delta_dt_bias', 'delta_a_log', 'delta_d_skip', 'delta_g_att_out', 'delta_g_ssd_out', 'delta_w_out', 'delta_g_ffn', 'delta_w_gate', 'delta_w_up', 'delta_w_down', 'delta_g_final', 'new_m_w_ada', 'new_m_b_ada', 'new_m_g_mix', 'new_m_w_in', 'new_m_rel_bias', 'new_m_conv_w', 'new_m_conv_b', 'new_m_dt_bias', 'new_m_a_log', 'new_m_d_skip', 'new_m_g_att_out', 'new_m_g_ssd_out', 'new_m_w_out', 'new_m_g_ffn', 'new_m_w_gate', 'new_m_w_up', 'new_m_w_down', 'new_m_g_final', 'new_v_w_ada', 'new_v_b_ada', 'new_v_g_mix', 'new_v_w_in', 'new_v_rel_bias', 'new_v_conv_w', 'new_v_conv_b', 'new_v_dt_bias', 'new_v_a_log', 'new_v_d_skip', 'new_v_g_att_out', 'new_v_g_ssd_out', 'new_v_w_out', 'new_v_g_ffn', 'new_v_w_gate', 'new_v_w_up', 'new_v_w_down', 'new_v_g_final']
TWIN_LEAF_KINDS = {'loss': 'loss', 'grad_x': 'grad_x', 'grad_w_ada': 'grad_w', 'grad_b_ada': 'grad_w', 'grad_g_mix': 'grad_w', 'grad_w_in': 'grad_w', 'grad_rel_bias': 'grad_w', 'grad_conv_w': 'grad_w', 'grad_conv_b': 'grad_w', 'grad_dt_bias': 'grad_w', 'grad_a_log': 'grad_w', 'grad_d_skip': 'grad_w', 'grad_g_att_out': 'grad_w', 'grad_g_ssd_out': 'grad_w', 'grad_w_out': 'grad_w', 'grad_g_ffn': 'grad_w', 'grad_w_gate': 'grad_w', 'grad_w_up': 'grad_w', 'grad_w_down': 'grad_w', 'grad_g_final': 'grad_w', 'delta_w_ada': 'delta_w', 'delta_b_ada': 'delta_w', 'delta_g_mix': 'delta_w', 'delta_w_in': 'delta_w', 'delta_rel_bias': 'delta_w', 'delta_conv_w': 'delta_w', 'delta_conv_b': 'delta_w', 'delta_dt_bias': 'delta_w', 'delta_a_log': 'delta_w', 'delta_d_skip': 'delta_w', 'delta_g_att_out': 'delta_w', 'delta_g_ssd_out': 'delta_w', 'delta_w_out': 'delta_w', 'delta_g_ffn': 'delta_w', 'delta_w_gate': 'delta_w', 'delta_w_up': 'delta_w', 'delta_w_down': 'delta_w', 'delta_g_final': 'delta_w', 'new_m_w_ada': 'new_m', 'new_m_b_ada': 'new_m', 'new_m_g_mix': 'new_m', 'new_m_w_in': 'new_m', 'new_m_rel_bias': 'new_m', 'new_m_conv_w': 'new_m', 'new_m_conv_b': 'new_m', 'new_m_dt_bias': 'new_m', 'new_m_a_log': 'new_m', 'new_m_d_skip': 'new_m', 'new_m_g_att_out': 'new_m', 'new_m_g_ssd_out': 'new_m', 'new_m_w_out': 'new_m', 'new_m_g_ffn': 'new_m', 'new_m_w_gate': 'new_m', 'new_m_w_up': 'new_m', 'new_m_w_down': 'new_m', 'new_m_g_final': 'new_m', 'new_v_w_ada': 'new_v', 'new_v_b_ada': 'new_v', 'new_v_g_mix': 'new_v', 'new_v_w_in': 'new_v', 'new_v_rel_bias': 'new_v', 'new_v_conv_w': 'new_v', 'new_v_conv_b': 'new_v', 'new_v_dt_bias': 'new_v', 'new_v_a_log': 'new_v', 'new_v_d_skip': 'new_v', 'new_v_g_att_out': 'new_v', 'new_v_g_ssd_out': 'new_v', 'new_v_w_out': 'new_v', 'new_v_g_ffn': 'new_v', 'new_v_w_gate': 'new_v', 'new_v_w_up': 'new_v', 'new_v_w_down': 'new_v', 'new_v_g_final': 'new_v'}


def _forward(args):
    return _fwd_reference(*[args[k] for k in FWD_PARAMS])


def _output_shape():
    def fwd():
        inp = _fwd_setup_inputs(0)
        return _fwd_reference(*[inp[k] for k in FWD_PARAMS])
    out = _jax.eval_shape(fwd)
    return out.shape, out.dtype

N_MICROBATCH = 1
ADAM_LR = 0.001
ADAM_B1 = 0.9
ADAM_B2 = 0.999
ADAM_EPS = 1e-08
ADAM_WD = 0.01
ADAM_STEP = 10
PER_EXAMPLE_BATCH_AXIS = {'x': 0, 'c': 0, 'loss_target': 0}
SHARED_INPUTS = []
_WEIGHT_DTYPES = {'w_ada': _jnp.float32, 'b_ada': _jnp.float32, 'g_mix': _jnp.float32, 'w_in': _jnp.float32, 'rel_bias': _jnp.float32, 'conv_w': _jnp.float32, 'conv_b': _jnp.float32, 'dt_bias': _jnp.float32, 'a_log': _jnp.float32, 'd_skip': _jnp.float32, 'g_att_out': _jnp.float32, 'g_ssd_out': _jnp.float32, 'w_out': _jnp.float32, 'g_ffn': _jnp.float32, 'w_gate': _jnp.float32, 'w_up': _jnp.float32, 'w_down': _jnp.float32, 'g_final': _jnp.float32}
MOMENT_SCALE = {'w_ada': 5.651424e-02, 'b_ada': 1.007301e-01, 'g_mix': 7.636828e-02, 'w_in': 5.976872e-02, 'rel_bias': 1.098524e-02, 'conv_w': 6.199534e-02, 'conv_b': 5.910164e-02, 'dt_bias': 6.810803e-01, 'a_log': 2.954316e-01, 'd_skip': 2.495542e-01, 'g_att_out': 7.683306e-02, 'g_ssd_out': 6.716551e-02, 'w_out': 7.324455e-02, 'g_ffn': 8.389051e-02, 'w_gate': 3.816462e-02, 'w_up': 3.726673e-02, 'w_down': 6.184079e-02, 'g_final': 3.263486e+01}


def _to_microbatches(a, axis):
    t = _jnp.moveaxis(a, axis, 0)
    t = t.reshape((N_MICROBATCH, t.shape[0] // N_MICROBATCH) + t.shape[1:])
    return _jnp.moveaxis(t, 1, axis + 1)


def setup_inputs(seed: int = 0) -> dict:
    inp = _fwd_setup_inputs(seed)
    key = _jax.random.fold_in(_jax.random.key(seed), 7919)
    shape, _ = _output_shape()
    out = dict(inp)
    out["loss_target"] = _jax.random.normal(_jax.random.fold_in(key, 0), shape, _jnp.float32)
    for i, name in enumerate(TWIN_WEIGHTS):
        w = inp[name].astype(_jnp.float32)
        if MOMENT_SCALE is None:
            s = _jnp.sqrt(_jnp.mean(_jnp.square(w)) + 1e-30)
        else:
            s = MOMENT_SCALE[name]
        km, kv = _jax.random.split(_jax.random.fold_in(key, i + 1))
        out[name] = w
        out["m_" + name] = s * _jax.random.normal(km, w.shape, _jnp.float32)
        out["v_" + name] = (s * s) * _jax.random.uniform(kv, w.shape, _jnp.float32, 0.5, 1.5)
    if N_MICROBATCH > 1:
        for name, axis in PER_EXAMPLE_BATCH_AXIS.items():
            out[name] = _to_microbatches(out[name], axis)
    return {'x': out['x'], 'c': out['c'], 'w_ada': out['w_ada'], 'b_ada': out['b_ada'], 'g_mix': out['g_mix'], 'w_in': out['w_in'], 'rel_bias': out['rel_bias'], 'conv_w': out['conv_w'], 'conv_b': out['conv_b'], 'dt_bias': out['dt_bias'], 'a_log': out['a_log'], 'd_skip': out['d_skip'], 'g_att_out': out['g_att_out'], 'g_ssd_out': out['g_ssd_out'], 'w_out': out['w_out'], 'g_ffn': out['g_ffn'], 'w_gate': out['w_gate'], 'w_up': out['w_up'], 'w_down': out['w_down'], 'g_final': out['g_final'], 'loss_target': out['loss_target'], 'm_w_ada': out['m_w_ada'], 'm_b_ada': out['m_b_ada'], 'm_g_mix': out['m_g_mix'], 'm_w_in': out['m_w_in'], 'm_rel_bias': out['m_rel_bias'], 'm_conv_w': out['m_conv_w'], 'm_conv_b': out['m_conv_b'], 'm_dt_bias': out['m_dt_bias'], 'm_a_log': out['m_a_log'], 'm_d_skip': out['m_d_skip'], 'm_g_att_out': out['m_g_att_out'], 'm_g_ssd_out': out['m_g_ssd_out'], 'm_w_out': out['m_w_out'], 'm_g_ffn': out['m_g_ffn'], 'm_w_gate': out['m_w_gate'], 'm_w_up': out['m_w_up'], 'm_w_down': out['m_w_down'], 'm_g_final': out['m_g_final'], 'v_w_ada': out['v_w_ada'], 'v_b_ada': out['v_b_ada'], 'v_g_mix': out['v_g_mix'], 'v_w_in': out['v_w_in'], 'v_rel_bias': out['v_rel_bias'], 'v_conv_w': out['v_conv_w'], 'v_conv_b': out['v_conv_b'], 'v_dt_bias': out['v_dt_bias'], 'v_a_log': out['v_a_log'], 'v_d_skip': out['v_d_skip'], 'v_g_att_out': out['v_g_att_out'], 'v_g_ssd_out': out['v_g_ssd_out'], 'v_w_out': out['v_w_out'], 'v_g_ffn': out['v_g_ffn'], 'v_w_gate': out['v_w_gate'], 'v_w_up': out['v_w_up'], 'v_w_down': out['v_w_down'], 'v_g_final': out['v_g_final']}


def _loss(weights, diff, rest, loss_target):
    with _jax.named_scope("forward"):
        args = {**rest, TWIN_DIFF_INPUT: diff, **{k: w.astype(_WEIGHT_DTYPES[k]) for k, w in weights.items()}}
        y = _forward(args)
    with _jax.named_scope("loss_head"):
        err = _jnp.square(y.astype(_jnp.float32) - loss_target)
        return 0.5 * _jnp.sum(_jnp.mean(err, axis=-1)) if err.ndim else 0.5 * err


def _adamw(w, g, m, v):
    m = ADAM_B1 * m + (1.0 - ADAM_B1) * g
    v = ADAM_B2 * v + (1.0 - ADAM_B2) * _jnp.square(g)
    m_hat = m / (1.0 - ADAM_B1 ** ADAM_STEP)
    v_hat = v / (1.0 - ADAM_B2 ** ADAM_STEP)
    delta = -ADAM_LR * (m_hat / (_jnp.sqrt(v_hat) + ADAM_EPS) + ADAM_WD * w)
    return delta, m, v


def reference(x, c, w_ada, b_ada, g_mix, w_in, rel_bias, conv_w, conv_b, dt_bias, a_log, d_skip, g_att_out, g_ssd_out, w_out, g_ffn, w_gate, w_up, w_down, g_final, loss_target, m_w_ada, m_b_ada, m_g_mix, m_w_in, m_rel_bias, m_conv_w, m_conv_b, m_dt_bias, m_a_log, m_d_skip, m_g_att_out, m_g_ssd_out, m_w_out, m_g_ffn, m_w_gate, m_w_up, m_w_down, m_g_final, v_w_ada, v_b_ada, v_g_mix, v_w_in, v_rel_bias, v_conv_w, v_conv_b, v_dt_bias, v_a_log, v_d_skip, v_g_att_out, v_g_ssd_out, v_w_out, v_g_ffn, v_w_gate, v_w_up, v_w_down, v_g_final):
    given = dict(x=x, c=c, w_ada=w_ada, b_ada=b_ada, g_mix=g_mix, w_in=w_in, rel_bias=rel_bias, conv_w=conv_w, conv_b=conv_b, dt_bias=dt_bias, a_log=a_log, d_skip=d_skip, g_att_out=g_att_out, g_ssd_out=g_ssd_out, w_out=w_out, g_ffn=g_ffn, w_gate=w_gate, w_up=w_up, w_down=w_down, g_final=g_final, loss_target=loss_target, m_w_ada=m_w_ada, m_b_ada=m_b_ada, m_g_mix=m_g_mix, m_w_in=m_w_in, m_rel_bias=m_rel_bias, m_conv_w=m_conv_w, m_conv_b=m_conv_b, m_dt_bias=m_dt_bias, m_a_log=m_a_log, m_d_skip=m_d_skip, m_g_att_out=m_g_att_out, m_g_ssd_out=m_g_ssd_out, m_w_out=m_w_out, m_g_ffn=m_g_ffn, m_w_gate=m_w_gate, m_w_up=m_w_up, m_w_down=m_w_down, m_g_final=m_g_final, v_w_ada=v_w_ada, v_b_ada=v_b_ada, v_g_mix=v_g_mix, v_w_in=v_w_in, v_rel_bias=v_rel_bias, v_conv_w=v_conv_w, v_conv_b=v_conv_b, v_dt_bias=v_dt_bias, v_a_log=v_a_log, v_d_skip=v_d_skip, v_g_att_out=v_g_att_out, v_g_ssd_out=v_g_ssd_out, v_w_out=v_w_out, v_g_ffn=v_g_ffn, v_w_gate=v_w_gate, v_w_up=v_w_up, v_w_down=v_w_down, v_g_final=v_g_final)
    weights = {n: given[n] for n in TWIN_WEIGHTS}
    shared = {n: given[n] for n in SHARED_INPUTS}
    per_example = {n: given[n] for n in ['x', 'c']}
    grad_fn = _jax.value_and_grad(_loss, argnums=(0, 1))

    def one_microbatch(ex, loss_target):
        ex = dict(ex)
        diff = ex.pop(TWIN_DIFF_INPUT)
        return grad_fn(weights, diff, {**shared, **ex}, loss_target)

    if N_MICROBATCH == 1:
        loss, (grad_w, grad_x) = one_microbatch(per_example, given["loss_target"])
    else:
        def body(carry, xs):
            loss_sum, grad_sum = carry
            l_k, (gw_k, gx_k) = one_microbatch(xs[0], xs[1])
            with _jax.named_scope("update"):
                return (loss_sum + l_k, _jax.tree.map(_jnp.add, grad_sum, gw_k)), gx_k

        init = (_jnp.zeros((), _jnp.float32), _jax.tree.map(_jnp.zeros_like, weights))
        (loss, grad_w), grad_x = _jax.lax.scan(body, init, (per_example, given["loss_target"]))
    with _jax.named_scope("update"):
        delta_w, new_m, new_v = {}, {}, {}
        for n in TWIN_WEIGHTS:
            delta_w[n], new_m[n], new_v[n] = _adamw(weights[n], grad_w[n], given["m_" + n], given["v_" + n])
    return (loss, grad_x, *[grad_w[n] for n in TWIN_WEIGHTS], *[delta_w[n] for n in TWIN_WEIGHTS],
            *[new_m[n] for n in TWIN_WEIGHTS], *[new_v[n] for n in TWIN_WEIGHTS])
```

```python
import functools

import jax
import jax.numpy as jnp
from jax import lax
from jax.experimental import pallas as pl
from jax.experimental.pallas import tpu as pltpu

F32 = jnp.float32
BF = jnp.bfloat16
HI = lax.Precision.HIGHEST

D = 2048
NH = 16
HD = 64
AW = 1024
SW = 1024
NG = 2
NS = 128
XBC = 1536
CH = 64
BAND = 576
NREL = 320
REL_CLIP = 256
FF = 5632
INC = 5648
INP = 5760
EPS = 1e-6
NDEV = 8
LANE = 128
VMEM_LIMIT = 56 * 1024 * 1024
LR, B1, B2, AEPS, WD, STEP = 0.001, 0.9, 0.999, 1e-08, 0.01, 10
MESH = pl.DeviceIdType.MESH


def _cp(*sem):
    return pltpu.CompilerParams(dimension_semantics=sem, vmem_limit_bytes=VMEM_LIMIT)


def _row(ts, w, col=0):
    return pl.BlockSpec((ts, w), lambda i, _c=col: (i, _c))


def _fix(shape):
    nd = len(shape)
    return pl.BlockSpec(shape, lambda i, _n=nd: (0,) * _n)


def _silu_grad(x, s):
    return s * (1.0 + x * (1.0 - s))


def _softplus(x):
    return jnp.maximum(x, 0.0) + jnp.log1p(jnp.exp(-jnp.abs(x)))


def _mm(a, b, out_dtype, name, tm, tn, tk, precision=None):
    m, k = a.shape
    _, n = b.shape
    nk = k // tk
    assert m % tm == 0 and n % tn == 0 and k % tk == 0, (name, a.shape, b.shape)

    def body(a_ref, b_ref, o_ref, acc):
        kk = pl.program_id(2)

        @pl.when(kk == 0)
        def _():
            acc[...] = jnp.zeros_like(acc)

        acc[...] += jnp.dot(a_ref[...], b_ref[...], preferred_element_type=F32, precision=precision)

        @pl.when(kk == nk - 1)
        def _():
            o_ref[...] = acc[...].astype(o_ref.dtype)

    return pl.pallas_call(
        body, name=name, grid=(m // tm, n // tn, nk),
        in_specs=[pl.BlockSpec((tm, tk), lambda i, j, kk: (i, kk)),
                  pl.BlockSpec((tk, tn), lambda i, j, kk: (kk, j))],
        out_specs=pl.BlockSpec((tm, tn), lambda i, j, kk: (i, j)),
        out_shape=jax.ShapeDtypeStruct((m, n), out_dtype),
        scratch_shapes=[pltpu.VMEM((tm, tn), F32)],
        compiler_params=_cp("parallel", "parallel", "arbitrary"),
    )(a, b)


def _norm_fwd(x, g, sc, sh, mix=None, gt=None, *, name, ts=256):
    s, d = x.shape
    has = mix is not None

    def body(*refs):
        if has:
            x_ref, g_ref, sc_ref, sh_ref, mix_ref, gt_ref, h_ref, x1_ref = refs
        else:
            x_ref, g_ref, sc_ref, sh_ref, h_ref = refs
        xv = x_ref[...]
        if has:
            xv = xv + gt_ref[...] * mix_ref[...]
            x1_ref[...] = xv
        r = lax.rsqrt(jnp.mean(xv * xv, axis=-1, keepdims=True) + EPS)
        h_ref[...] = ((xv * r) * g_ref[...] * (1.0 + sc_ref[...]) + sh_ref[...]).astype(BF)

    ins = [x, g, sc, sh] + ([mix, gt] if has else [])
    in_specs = [_row(ts, d), _fix((1, d)), _fix((1, d)), _fix((1, d))] + ([_row(ts, d), _fix((1, d))] if has else [])
    out_shape = [jax.ShapeDtypeStruct((s, d), BF)] + ([jax.ShapeDtypeStruct((s, d), F32)] if has else [])
    out_specs = [_row(ts, d)] + ([_row(ts, d)] if has else [])
    out = pl.pallas_call(body, name=name, grid=(s // ts,), in_specs=in_specs, out_specs=out_specs,
                         out_shape=out_shape, compiler_params=_cp("parallel"))(*ins)
    return out if has else out[0]


def _norm_bwd(x, g, sc, sh, dh, dres, mix=None, gt=None, *, name, ts=256):
    s, d = x.shape
    has = mix is not None

    def body(*refs):
        if has:
            x_ref, g_ref, sc_ref, sh_ref, dh_ref, dres_ref, mix_ref, gt_ref, dx_ref, dmix_ref, part_ref = refs
        else:
            x_ref, g_ref, sc_ref, sh_ref, dh_ref, dres_ref, dx_ref, part_ref = refs
        i = pl.program_id(0)
        xv = x_ref[...]
        gv = g_ref[...]
        dhv = dh_ref[...]
        r = lax.rsqrt(jnp.mean(xv * xv, axis=-1, keepdims=True) + EPS)
        xh = xv * r
        dyn = dhv * (1.0 + sc_ref[...])
        gy = dyn * gv
        dx = dres_ref[...] + r * (gy - xh * jnp.mean(gy * xh, axis=-1, keepdims=True))
        dx_ref[...] = dx

        @pl.when(i == 0)
        def _():
            part_ref[...] = jnp.zeros_like(part_ref)

        part_ref[0:1, :] += jnp.sum(dhv * (xh * gv), axis=0, keepdims=True)
        part_ref[1:2, :] += jnp.sum(dhv, axis=0, keepdims=True)
        part_ref[2:3, :] += jnp.sum(dyn * xh, axis=0, keepdims=True)
        if has:
            part_ref[3:4, :] += jnp.sum(dx * mix_ref[...], axis=0, keepdims=True)
            dmix_ref[...] = (gt_ref[...] * dx).astype(BF)

    ins = [x, g, sc, sh, dh, dres] + ([mix, gt] if has else [])
    in_specs = [_row(ts, d), _fix((1, d)), _fix((1, d)), _fix((1, d)), _row(ts, d), _row(ts, d)]
    in_specs += [_row(ts, d), _fix((1, d))] if has else []
    out_shape = [jax.ShapeDtypeStruct((s, d), F32)] + ([jax.ShapeDtypeStruct((s, d), BF)] if has else [])
    out_shape += [jax.ShapeDtypeStruct((8, d), F32)]
    out_specs = [_row(ts, d)] + ([_row(ts, d)] if has else []) + [_fix((8, d))]
    return pl.pallas_call(body, name=name, grid=(s // ts,), in_specs=in_specs, out_specs=out_specs,
                          out_shape=out_shape, compiler_params=_cp("arbitrary"))(*ins)


def _final(x1, ffn, gt2, gf, tgt, *, ts=256):
    s, d = x1.shape

    def body(x1_ref, ffn_ref, gt_ref, g_ref, t_ref, dx_ref, dffn_ref, part_ref):
        i = pl.program_id(0)
        fv = ffn_ref[...]
        gv = g_ref[...]
        xv = x1_ref[...] + gt_ref[...] * fv
        r = lax.rsqrt(jnp.mean(xv * xv, axis=-1, keepdims=True) + EPS)
        xh = xv * r
        e = xh * gv - t_ref[...]
        dy = e * (1.0 / d)
        gy = dy * gv
        dx = r * (gy - xh * jnp.mean(gy * xh, axis=-1, keepdims=True))
        dx_ref[...] = dx
        dffn_ref[...] = (gt_ref[...] * dx).astype(BF)

        @pl.when(i == 0)
        def _():
            part_ref[...] = jnp.zeros_like(part_ref)

        part_ref[0:1, :] += jnp.sum(dy * xh, axis=0, keepdims=True)
        part_ref[1:2, :] += jnp.sum(dx * fv, axis=0, keepdims=True)
        part_ref[2:3, :] += jnp.sum(e * e, axis=0, keepdims=True) * (0.5 / d)

    return pl.pallas_call(
        body, name="final_loss", grid=(s // ts,),
        in_specs=[_row(ts, d), _row(ts, d), _fix((1, d)), _fix((1, d)), _row(ts, d)],
        out_specs=[_row(ts, d), _row(ts, d), _fix((8, d))],
        out_shape=[jax.ShapeDtypeStruct((s, d), F32), jax.ShapeDtypeStruct((s, d), BF),
                   jax.ShapeDtypeStruct((8, d), F32)],
        compiler_params=_cp("arbitrary"))(x1, ffn, gt2, gf, tgt)


def _swiglu_fwd(gu, *, ts=256):
    s = gu.shape[0]

    def body(g_ref, u_ref, o_ref):
        gv = g_ref[...].astype(F32)
        o_ref[...] = (gv * jax.nn.sigmoid(gv) * u_ref[...].astype(F32)).astype(BF)

    return pl.pallas_call(body, name="swiglu_fwd", grid=(s // ts,),
                          in_specs=[_row(ts, FF, 0), _row(ts, FF, 1)], out_specs=_row(ts, FF),
                          out_shape=jax.ShapeDtypeStruct((s, FF), BF), compiler_params=_cp("parallel"))(gu, gu)


def _swiglu_bwd(gu, dact, *, ts=256):
    s = gu.shape[0]

    def body(g_ref, u_ref, da_ref, o_ref):
        gv = g_ref[...].astype(F32)
        uv = u_ref[...].astype(F32)
        dav = da_ref[...].astype(F32)
        sg = jax.nn.sigmoid(gv)
        o_ref[:, 0:FF] = (dav * uv * _silu_grad(gv, sg)).astype(BF)
        o_ref[:, FF:2 * FF] = (dav * gv * sg).astype(BF)

    return pl.pallas_call(body, name="swiglu_bwd", grid=(s // ts,),
                          in_specs=[_row(ts, FF, 0), _row(ts, FF, 1), _row(ts, FF)], out_specs=_row(ts, 2 * FF),
                          out_shape=jax.ShapeDtypeStruct((s, 2 * FF), BF),
                          compiler_params=_cp("parallel"))(gu, gu, dact)


def _mixout_fwd(att, y, proj, g_att, g_ssd, *, ts=256):
    s = att.shape[0]

    def body(a_ref, y_ref, z_ref, ga_ref, gs_ref, o_ref):
        av = a_ref[...]
        ra = lax.rsqrt(jnp.mean(av * av, axis=-1, keepdims=True) + EPS)
        o_ref[:, 0:AW] = (av * ra * ga_ref[...]).astype(BF)
        zv = z_ref[...]
        yz = y_ref[...] * (zv * jax.nn.sigmoid(zv))
        rs = lax.rsqrt(jnp.mean(yz * yz, axis=-1, keepdims=True) + EPS)
        o_ref[:, AW:AW + SW] = (yz * rs * gs_ref[...]).astype(BF)

    return pl.pallas_call(body, name="mixout_fwd", grid=(s // ts,),
                          in_specs=[_row(ts, AW), _row(ts, SW), _row(ts, SW, 3), _fix((1, AW)), _fix((1, SW))],
                          out_specs=_row(ts, AW + SW), out_shape=jax.ShapeDtypeStruct((s, AW + SW), BF),
                          compiler_params=_cp("parallel"))(att, y, proj, g_att, g_ssd)


def _mixout_bwd(dcat, att, y, proj, g_att, g_ssd, *, ts=256):
    s = att.shape[0]

    def body(dc_ref, a_ref, y_ref, z_ref, ga_ref, gs_ref, da_ref, dy_ref, dz_ref, part_ref):
        i = pl.program_id(0)
        av = a_ref[...]
        dca = dc_ref[:, 0:AW]
        ra = lax.rsqrt(jnp.mean(av * av, axis=-1, keepdims=True) + EPS)
        ah = av * ra
        gy = dca * ga_ref[...]
        da_ref[...] = ra * (gy - ah * jnp.mean(gy * ah, axis=-1, keepdims=True))
        zv = z_ref[...]
        yv = y_ref[...]
        sg = jax.nn.sigmoid(zv)
        sz = zv * sg
        yz = yv * sz
        dcs = dc_ref[:, AW:AW + SW]
        rs = lax.rsqrt(jnp.mean(yz * yz, axis=-1, keepdims=True) + EPS)
        yh = yz * rs
        gys = dcs * gs_ref[...]
        dyz = rs * (gys - yh * jnp.mean(gys * yh, axis=-1, keepdims=True))
        dy_ref[...] = dyz * sz
        dz_ref[...] = (dyz * yv * _silu_grad(zv, sg)).astype(BF)

        @pl.when(i == 0)
        def _():
            part_ref[...] = jnp.zeros_like(part_ref)

        part_ref[0:1, :] += jnp.sum(dca * ah, axis=0, keepdims=True)
        part_ref[1:2, :] += jnp.sum(dcs * yh, axis=0, keepdims=True)

    return pl.pallas_call(
        body, name="mixout_bwd", grid=(s // ts,),
        in_specs=[_row(ts, AW + SW), _row(ts, AW), _row(ts, SW), _row(ts, SW, 3), _fix((1, AW)), _fix((1, SW))],
        out_specs=[_row(ts, AW), _row(ts, SW), _row(ts, SW), _fix((8, AW))],
        out_shape=[jax.ShapeDtypeStruct((s, AW), F32), jax.ShapeDtypeStruct((s, SW), F32),
                   jax.ShapeDtypeStruct((s, SW), BF), jax.ShapeDtypeStruct((8, AW), F32)],
        compiler_params=_cp("arbitrary"))(dcat, att, y, proj, g_att, g_ssd)


TQ = 512
SCALE = HD ** -0.5


def _attn_fwd(qkv, bias):
    s = qkv.shape[0]
    nb = s // TQ
    prev = lambda i: jnp.maximum(i - 1, 0)

    def body(q_ref, kp_ref, kc_ref, vp_ref, vc_ref, b_ref, o_ref, kbuf, vbuf):
        i = pl.program_id(0)
        kbuf[0:TQ, :] = kp_ref[...]
        kbuf[TQ:2 * TQ, :] = kc_ref[...]
        vbuf[0:TQ, :] = vp_ref[...]
        vbuf[TQ:2 * TQ, :] = vc_ref[...]

        def chunk(c, carry):
            r0 = pl.multiple_of(c * CH, CH)
            kpos = i * TQ + r0 - TQ + lax.broadcasted_iota(jnp.int32, (1, BAND), 1)
            valid = kpos >= 0
            for h in range(NH):
                sl = slice(HD * h, HD * h + HD)
                q = q_ref[pl.ds(r0, CH), sl]
                k = kbuf[pl.ds(r0, BAND), sl]
                v = vbuf[pl.ds(r0, BAND), sl]
                sc = lax.dot_general(q, k, (((1,), (1,)), ((), ())), preferred_element_type=F32) * SCALE + b_ref[h]
                sc = jnp.where(valid, sc, -jnp.inf)
                e = jnp.exp(sc - jnp.max(sc, axis=-1, keepdims=True))
                p = e / jnp.sum(e, axis=-1, keepdims=True)
                o_ref[pl.ds(r0, CH), sl] = jnp.dot(p.astype(BF), v, preferred_element_type=F32)
            return carry

        lax.fori_loop(0, TQ // CH, chunk, 0)

    blk = lambda col, im: pl.BlockSpec((TQ, AW), lambda i, _c=col, _f=im: (_f(i), _c))
    cur = lambda i: i
    return pl.pallas_call(
        body, name="attn_fwd", grid=(nb,),
        in_specs=[blk(0, cur), blk(1, prev), blk(1, cur), blk(2, prev), blk(2, cur), _fix((NH, CH, BAND))],
        out_specs=_row(TQ, AW), out_shape=jax.ShapeDtypeStruct((s, AW), F32),
        scratch_shapes=[pltpu.VMEM((2 * TQ, AW), BF), pltpu.VMEM((2 * TQ, AW), BF)],
        compiler_params=_cp("parallel"))(qkv, qkv, qkv, qkv, qkv, bias)


def _attn_bwd(qkv, datt, bias_t):
    s = qkv.shape[0]
    nb = s // TQ
    cur = lambda i: jnp.minimum(i, nb - 1)
    prev = lambda i: jnp.maximum(jnp.minimum(i, nb - 1) - 1, 0)
    late = lambda i: jnp.maximum(i - 1, 0)

    def body(q_ref, kp_ref, kc_ref, vp_ref, vc_ref, do_ref, b_ref, dq_ref, dk_ref, dv_ref, db_ref,
             kbuf, vbuf, dkacc, dvacc):
        i = pl.program_id(0)

        @pl.when(i == 0)
        def _():
            dkacc[...] = jnp.zeros_like(dkacc)
            dvacc[...] = jnp.zeros_like(dvacc)
            db_ref[...] = jnp.zeros_like(db_ref)

        @pl.when(i < nb)
        def _():
            kbuf[0:TQ, :] = kp_ref[...]
            kbuf[TQ:2 * TQ, :] = kc_ref[...]
            vbuf[0:TQ, :] = vp_ref[...]
            vbuf[TQ:2 * TQ, :] = vc_ref[...]

            def chunk(c, carry):
                r0 = pl.multiple_of(c * CH, CH)
                kpos = i * TQ + r0 - TQ + lax.broadcasted_iota(jnp.int32, (BAND, 1), 0)
                valid = kpos >= 0
                for h in range(NH):
                    sl = slice(HD * h, HD * h + HD)
                    q = q_ref[pl.ds(r0, CH), sl]
                    k = kbuf[pl.ds(r0, BAND), sl]
                    v = vbuf[pl.ds(r0, BAND), sl]
                    do = do_ref[pl.ds(r0, CH), sl].astype(BF)
                    st = lax.dot_general(k, q, (((1,), (1,)), ((), ())), preferred_element_type=F32) * SCALE + b_ref[h]
                    st = jnp.where(valid, st, -jnp.inf)
                    e = jnp.exp(st - jnp.max(st, axis=0, keepdims=True))
                    pt = e / jnp.sum(e, axis=0, keepdims=True)
                    dvacc[pl.ds(r0, BAND), sl] += jnp.dot(pt.astype(BF), do, preferred_element_type=F32)
                    dpt = lax.dot_general(v, do, (((1,), (1,)), ((), ())), preferred_element_type=F32)
                    dst = pt * (dpt - jnp.sum(dpt * pt, axis=0, keepdims=True))
                    db_ref[h] += dst
                    dsb = dst.astype(BF)
                    dkacc[pl.ds(r0, BAND), sl] += jnp.dot(dsb, q, preferred_element_type=F32) * SCALE
                    dq = lax.dot_general(dsb, k, (((0,), (0,)), ((), ())), preferred_element_type=F32) * SCALE
                    dq_ref[pl.ds(r0, CH), sl] = dq.astype(BF)
                return carry

            lax.fori_loop(0, TQ // CH, chunk, 0)

        dk_ref[...] = dkacc[0:TQ, :].astype(BF)
        dv_ref[...] = dvacc[0:TQ, :].astype(BF)
        dkacc[0:TQ, :] = dkacc[TQ:2 * TQ, :]
        dvacc[0:TQ, :] = dvacc[TQ:2 * TQ, :]
        dkacc[TQ:2 * TQ, :] = jnp.zeros((TQ, AW), F32)
        dvacc[TQ:2 * TQ, :] = jnp.zeros((TQ, AW), F32)

    blk = lambda col, im: pl.BlockSpec((TQ, AW), lambda i, _c=col, _f=im: (_f(i), _c))
    return pl.pallas_call(
        body, name="attn_bwd", grid=(nb + 1,),
        in_specs=[blk(0, cur), blk(1, prev), blk(1, cur), blk(2, prev), blk(2, cur), blk(0, cur),
                  _fix((NH, BAND, CH))],
        out_specs=[blk(0, cur), blk(0, late), blk(0, late), _fix((NH, BAND, CH))],
        out_shape=[jax.ShapeDtypeStruct((s, AW), BF)] * 3 + [jax.ShapeDtypeStruct((NH, BAND, CH), F32)],
        scratch_shapes=[pltpu.VMEM((2 * TQ, AW), BF), pltpu.VMEM((2 * TQ, AW), BF),
                        pltpu.VMEM((2 * TQ, AW), F32), pltpu.VMEM((2 * TQ, AW), F32)],
        compiler_params=_cp("arbitrary"))(qkv, qkv, qkv, qkv, qkv, datt, bias_t)


TB = 512
NCB = TB // CH
GW = 8 * HD


def _tri(lower):
    r = lax.broadcasted_iota(jnp.int32, (CH, CH), 0)
    c = lax.broadcasted_iota(jnp.int32, (CH, CH), 1)
    return r >= c if lower else r <= c


def _ssd_fwd(proj, conv_w, conv_b, dt_bias, a_log, d_skip):
    s = proj.shape[0]
    nb = s // TB

    def body(xs_ref, bc_ref, dt_ref, cw_ref, cb_ref, dtb_ref, al_ref, dsk_ref, y_ref, pre_ref, prev_ref,
             xpad, ubuf, dtbuf, csb, cstb, hst, xdec, yoffb, drow):
        i = pl.program_id(0)

        @pl.when(i == 0)
        def _():
            xpad[0:8, :] = jnp.zeros((8, XBC), F32)
            hst[...] = jnp.zeros_like(hst)

        xpad[8:8 + TB, 0:SW] = xs_ref[...]
        xpad[8:8 + TB, SW:XBC] = bc_ref[...]
        pre = cb_ref[...]
        for kk in range(4):
            pre = pre + cw_ref[kk:kk + 1, :] * xpad[5 + kk:5 + kk + TB, :]
        pre_ref[...] = pre
        ubuf[...] = pre * jax.nn.sigmoid(pre)
        xpad[0:8, :] = xpad[TB:TB + 8, :]
        dtbuf[...] = _softplus(dt_ref[...] + dtb_ref[...])
        a = -jnp.exp(al_ref[...])
        tri = _tri(True).astype(F32)
        causal = _tri(True)

        def chunk(c, carry):
            r0 = pl.multiple_of(c * CH, CH)
            rows = pl.ds(r0, CH)
            cs = jnp.dot(tri, dtbuf[rows, :] * a, precision=HI, preferred_element_type=F32)
            csb[...] = cs
            cstb[...] = cs.T
            for g in range(NG):
                bg = ubuf[rows, SW + NS * g:SW + NS * g + NS]
                cg = ubuf[rows, SW + NG * NS + NS * g:SW + NG * NS + NS * g + NS].astype(BF)
                cb = lax.dot_general(cg, bg.astype(BF), (((1,), (1,)), ((), ())), preferred_element_type=F32)
                hg = hst[g]
                prev_ref[c, g] = hg
                yoffb[...] = jnp.dot(cg, hg.astype(BF), preferred_element_type=F32)
                for r in range(8):
                    h = 8 * g + r
                    sl = slice(HD * h, HD * h + HD)
                    rs = slice(HD * r, HD * r + HD)
                    cs_h = csb[:, h:h + 1]
                    cl = csb[CH - 1:CH, h:h + 1]
                    seg = jnp.exp(jnp.where(causal, cs_h - cstb[h:h + 1, :], -jnp.inf))
                    xh = ubuf[rows, sl]
                    xdt = xh * dtbuf[rows, h:h + 1]
                    yd = jnp.dot((cb * seg).astype(BF), xdt.astype(BF), preferred_element_type=F32)
                    y_ref[rows, sl] = yd + jnp.exp(cs_h) * yoffb[:, rs] + xh * dsk_ref[:, h:h + 1]
                    xdec[:, rs] = xdt * jnp.exp(cl - cs_h)
                    drow[:, rs] = jnp.broadcast_to(jnp.exp(cl), (1, HD))
                st = jnp.dot(bg.T.astype(BF), xdec[...].astype(BF), preferred_element_type=F32)
                hst[g] = hg * drow[...] + st
            return carry

        lax.fori_loop(0, NCB, chunk, 0)

    return pl.pallas_call(
        body, name="ssd_fwd", grid=(nb,),
        in_specs=[_row(TB, SW, 4), _row(TB, 512, 10), _row(TB, LANE, 44), _fix((4, XBC)), _fix((1, XBC)),
                  _fix((1, LANE)), _fix((1, LANE)), _fix((1, LANE))],
        out_specs=[_row(TB, SW), _row(TB, XBC), pl.BlockSpec((NCB, NG, NS, GW), lambda i: (i, 0, 0, 0))],
        out_shape=[jax.ShapeDtypeStruct((s, SW), F32), jax.ShapeDtypeStruct((s, XBC), F32),
                   jax.ShapeDtypeStruct((s // CH, NG, NS, GW), F32)],
        scratch_shapes=[pltpu.VMEM((TB + 8, XBC), F32), pltpu.VMEM((TB, XBC), F32), pltpu.VMEM((TB, LANE), F32),
                        pltpu.VMEM((CH, LANE), F32), pltpu.VMEM((LANE, CH), F32), pltpu.VMEM((NG, NS, GW), F32),
                        pltpu.VMEM((CH, GW), F32), pltpu.VMEM((CH, GW), F32), pltpu.VMEM((1, GW), F32)],
        compiler_params=_cp("arbitrary"))(proj, proj, proj, conv_w, conv_b, dt_bias, a_log, d_skip)


def _ssd_bwd(dy, pre, proj, prev, conv_w, dt_bias, a_log, d_skip):
    s = dy.shape[0]
    nb = s // TB
    rev = lambda i: nb - 1 - i
    halo = lambda i: jnp.maximum((nb - 1 - i) * (TB // 8) - 1, 0)

    def body(dy_ref, pre_ref, dt_ref, prev_ref, xs_ref, bc_ref, xsh_ref, bch_ref, cw_ref, dtb_ref, al_ref, dsk_ref,
             dx_ref, ddt_ref, pw_ref, ph_ref,
             ubuf, dtbuf, dpad, xpad, csb, cstb, dhs, yoffb, ebuf, xdecb, wbuf, drow, dub):
        i = pl.program_id(0)

        @pl.when(i == 0)
        def _():
            dhs[...] = jnp.zeros_like(dhs)
            dpad[TB:TB + 8, :] = jnp.zeros((8, XBC), F32)
            pw_ref[...] = jnp.zeros_like(pw_ref)
            ph_ref[...] = jnp.zeros_like(ph_ref)

        pre = pre_ref[...]
        ubuf[...] = pre * jax.nn.sigmoid(pre)
        dtbuf[...] = _softplus(dt_ref[...] + dtb_ref[...])
        a = -jnp.exp(al_ref[...])
        tri = _tri(True).astype(F32)
        trit = _tri(False).astype(F32)
        causal = _tri(True)
        upper = _tri(False)
        lane = lax.broadcasted_iota(jnp.int32, (CH, LANE), 1)
        lane1 = lax.broadcasted_iota(jnp.int32, (1, LANE), 1)
        lastrow = lax.broadcasted_iota(jnp.int32, (CH, LANE), 0) == CH - 1

        def chunk(cc, carry):
            c = NCB - 1 - cc
            r0 = pl.multiple_of(c * CH, CH)
            rows = pl.ds(r0, CH)
            dtc = dtbuf[rows, :]
            cs = jnp.dot(tri, dtc * a, precision=HI, preferred_element_type=F32)
            csb[...] = cs
            cstb[...] = cs.T
            dcs = jnp.zeros((CH, LANE), F32)
            xr = jnp.zeros((CH, LANE), F32)
            dlast = jnp.zeros((1, LANE), F32)
            dsk = jnp.zeros((1, LANE), F32)
            for g in range(NG):
                bg = ubuf[rows, SW + NS * g:SW + NS * g + NS]
                cg = ubuf[rows, SW + NG * NS + NS * g:SW + NG * NS + NS * g + NS]
                bb = bg.astype(BF)
                cbf = cg.astype(BF)
                cb = lax.dot_general(cbf, bb, (((1,), (1,)), ((), ())), preferred_element_type=F32)
                cbt = lax.dot_general(bb, cbf, (((1,), (1,)), ((), ())), preferred_element_type=F32)
                hp = prev_ref[c, g]
                hpb = hp.astype(BF)
                dhg = dhs[g]
                dhb = dhg.astype(BF)
                yoffb[...] = jnp.dot(cbf, hpb, preferred_element_type=F32)
                for r in range(8):
                    h = 8 * g + r
                    sl = slice(HD * h, HD * h + HD)
                    rs = slice(HD * r, HD * r + HD)
                    cs_h = csb[:, h:h + 1]
                    cl = csb[CH - 1:CH, h:h + 1]
                    dyh = dy_ref[rows, sl]
                    ecs = jnp.exp(cs_h)
                    ebuf[:, rs] = dyh * ecs
                    xdecb[:, rs] = ubuf[rows, sl] * dtbuf[rows, h:h + 1] * jnp.exp(cl - cs_h)
                    t1 = jnp.sum(dyh * ecs * yoffb[:, rs], axis=1, keepdims=True)
                    dcs = dcs + jnp.where(lane == h, t1, 0.0)
                    drow[:, rs] = jnp.broadcast_to(jnp.exp(cl), (1, HD))
                eb = ebuf[...].astype(BF)
                dc = lax.dot_general(eb, hpb, (((1,), (1,)), ((), ())), preferred_element_type=F32)
                dprev = jnp.dot(cg.T.astype(BF), eb, preferred_element_type=F32)
                wbuf[...] = jnp.dot(bb, dhb, preferred_element_type=F32)
                db = lax.dot_general(xdecb[...].astype(BF), dhb, (((1,), (1,)), ((), ())), preferred_element_type=F32)
                dcbs = jnp.zeros((CH, CH), F32)
                dcbts = jnp.zeros((CH, CH), F32)
                for r in range(8):
                    h = 8 * g + r
                    sl = slice(HD * h, HD * h + HD)
                    rs = slice(HD * r, HD * r + HD)
                    cs_h = csb[:, h:h + 1]
                    cst_h = cstb[h:h + 1, :]
                    cl = csb[CH - 1:CH, h:h + 1]
                    seg = jnp.exp(jnp.where(causal, cs_h - cst_h, -jnp.inf))
                    segt = jnp.exp(jnp.where(upper, cst_h - cs_h, -jnp.inf))
                    dyh = dy_ref[rows, sl]
                    xh = ubuf[rows, sl]
                    dt_h = dtbuf[rows, h:h + 1]
                    xdt = xh * dt_h
                    dyb = dyh.astype(BF)
                    xdb = xdt.astype(BF)
                    dm = lax.dot_general(dyb, xdb, (((1,), (1,)), ((), ())), preferred_element_type=F32)
                    dmt = lax.dot_general(xdb, dyb, (((1,), (1,)), ((), ())), preferred_element_type=F32)
                    mt = cbt * segt
                    dxdt = jnp.dot(mt.astype(BF), dyb, preferred_element_type=F32)
                    dcs_h = (jnp.sum(dm * (cb * seg), axis=1, keepdims=True)
                             - jnp.sum(dmt * mt, axis=1, keepdims=True))
                    dcbs = dcbs + dm * seg
                    dcbts = dcbts + dmt * segt
                    dec = jnp.exp(cl - cs_h)
                    w_h = wbuf[:, rs]
                    dxdt = dxdt + dec * w_h
                    t = jnp.sum(w_h * xdt, axis=1, keepdims=True) * dec
                    dcs_h = dcs_h - t
                    dch = jnp.sum(jnp.sum(dhs[g, :, rs] * prev_ref[c, g, :, rs], axis=1, keepdims=True),
                                  axis=0, keepdims=True)
                    dl = jnp.sum(t, axis=0, keepdims=True) + dch * jnp.exp(cl)
                    dcs = dcs + jnp.where(lane == h, dcs_h, 0.0)
                    dlast = dlast + jnp.where(lane1 == h, dl, 0.0)
                    xr = xr + jnp.where(lane == h, jnp.sum(dxdt * xh, axis=1, keepdims=True), 0.0)
                    dskh = jnp.sum(jnp.sum(dyh * xh, axis=1, keepdims=True), axis=0, keepdims=True)
                    dsk = dsk + jnp.where(lane1 == h, dskh, 0.0)
                    dub[:, sl] = dyh * dsk_ref[:, h:h + 1] + dxdt * dt_h
                dc = dc + jnp.dot(dcbs.astype(BF), bb, preferred_element_type=F32)
                db = db + jnp.dot(dcbts.astype(BF), cbf, preferred_element_type=F32)
                dub[:, SW + NS * g:SW + NS * g + NS] = db
                dub[:, SW + NG * NS + NS * g:SW + NG * NS + NS * g + NS] = dc
                dhs[g] = dprev + drow[...] * dhg
            dcs = dcs + jnp.where(lastrow, dlast, 0.0)
            dadt = jnp.dot(trit, dcs, precision=HI, preferred_element_type=F32)
            ddt = dadt * a + xr
            ddtr = ddt * jax.nn.sigmoid(dt_ref[rows, :] + dtb_ref[...])
            ddt_ref[rows, :] = ddtr.astype(BF)
            ph_ref[0:1, :] += jnp.sum(ddtr, axis=0, keepdims=True)
            ph_ref[1:2, :] += jnp.sum(dadt * dtc, axis=0, keepdims=True) * a
            ph_ref[2:3, :] += dsk
            pc = pre_ref[rows, :]
            sg = jax.nn.sigmoid(pc)
            dpad[rows, :] = dub[...] * _silu_grad(pc, sg)
            return carry

        lax.fori_loop(0, NCB, chunk, 0)

        dxb = cw_ref[0:1, :] * dpad[3:3 + TB, :]
        for kk in range(1, 4):
            dxb = dxb + cw_ref[kk:kk + 1, :] * dpad[3 - kk:3 - kk + TB, :]
        dx_ref[...] = dxb.astype(BF)
        keep = jnp.where(i < nb - 1, 1.0, 0.0)
        xpad[0:8, 0:SW] = xsh_ref[...] * keep
        xpad[0:8, SW:XBC] = bch_ref[...] * keep
        xpad[8:8 + TB, 0:SW] = xs_ref[...]
        xpad[8:8 + TB, SW:XBC] = bc_ref[...]
        dp = dpad[0:TB, :]
        for kk in range(4):
            pw_ref[kk:kk + 1, :] += jnp.sum(dp * xpad[5 + kk:5 + kk + TB, :], axis=0, keepdims=True)
        pw_ref[4:5, :] += jnp.sum(dp, axis=0, keepdims=True)
        dpad[TB:TB + 8, :] = dpad[0:8, :]

    rblk = lambda w, col: pl.BlockSpec((TB, w), lambda i, _c=col: (rev(i), _c))
    return pl.pallas_call(
        body, name="ssd_bwd", grid=(nb,),
        in_specs=[rblk(SW, 0), rblk(XBC, 0), rblk(LANE, 44),
                  pl.BlockSpec((NCB, NG, NS, GW), lambda i: (rev(i), 0, 0, 0)),
                  rblk(SW, 4), rblk(512, 10),
                  pl.BlockSpec((8, SW), lambda i: (halo(i), 4)), pl.BlockSpec((8, 512), lambda i: (halo(i), 10)),
                  _fix((4, XBC)), _fix((1, LANE)), _fix((1, LANE)), _fix((1, LANE))],
        out_specs=[rblk(XBC, 0), rblk(LANE, 0), _fix((8, XBC)), _fix((8, LANE))],
        out_shape=[jax.ShapeDtypeStruct((s, XBC), BF), jax.ShapeDtypeStruct((s, LANE), BF),
                   jax.ShapeDtypeStruct((8, XBC), F32), jax.ShapeDtypeStruct((8, LANE), F32)],
        scratch_shapes=[pltpu.VMEM((TB, XBC), F32), pltpu.VMEM((TB, LANE), F32), pltpu.VMEM((TB + 8, XBC), F32),
                        pltpu.VMEM((TB + 8, XBC), F32), pltpu.VMEM((CH, LANE), F32), pltpu.VMEM((LANE, CH), F32),
                        pltpu.VMEM((NG, NS, GW), F32), pltpu.VMEM((CH, GW), F32), pltpu.VMEM((CH, GW), F32),
                        pltpu.VMEM((CH, GW), F32), pltpu.VMEM((CH, GW), F32), pltpu.VMEM((1, GW), F32),
                        pltpu.VMEM((CH, XBC), F32)],
        compiler_params=_cp("arbitrary"))(dy, pre, proj, prev, proj, proj, proj, proj, conv_w, dt_bias, a_log, d_skip)


def _rel_index():
    q = jnp.arange(CH)[:, None] + (BAND - CH)
    k = jnp.arange(BAND)[None, :]
    return jnp.clip(q - k, -(CH - 1), REL_CLIP) + (CH - 1)


def _local_step(x, tgt, mods, w_in_p, w_in_t, w_out, w_out_t, w_gu, w_gu_t, w_down, w_down_t,
                g_mix, rel_bias, conv_w, conv_b, dt_bias, a_log, d_skip, g_att, g_ssd, g_ffn, g_final):
    sh1, sc1, gt1, sh2, sc2, gt2 = mods
    pad16 = lambda v: jnp.pad(v, ((0, 0), (0, LANE - NH)))
    dtb_p, al_p, dsk_p = pad16(dt_bias), pad16(a_log), pad16(d_skip)
    ridx = _rel_index()
    bias = rel_bias[:, ridx]
    bias_t = jnp.swapaxes(bias, 1, 2)

    h1 = _norm_fwd(x, g_mix, sc1, sh1, name="norm1_fwd")
    proj = _mm(h1, w_in_p, F32, "in_proj", 512, 1152, 2048)
    qkv = proj[:, :3 * AW].astype(BF)
    att = _attn_fwd(qkv, bias)
    y, pre, prev = _ssd_fwd(proj, conv_w, conv_b, dtb_p, al_p, dsk_p)
    cat = _mixout_fwd(att, y, proj, g_att, g_ssd)
    mix = _mm(cat, w_out, F32, "out_proj", 512, 1024, 2048)
    h2, x1 = _norm_fwd(x, g_ffn, sc2, sh2, mix, gt1, name="norm2_fwd")
    gu = _mm(h2, w_gu, BF, "gate_up", 512, 1024, 2048)
    act = _swiglu_fwd(gu)
    ffn = _mm(act, w_down, F32, "down_proj", 512, 1024, 1408)
    dx2, dffn, pf = _final(x1, ffn, gt2, g_final, tgt)

    dact = _mm(dffn, w_down_t, BF, "d_act", 512, 1408, 2048)
    gw_down = _mm(act.T, dffn, F32, "gw_down", 704, 1024, 1024)
    dgu = _swiglu_bwd(gu, dact)
    dh2 = _mm(dgu, w_gu_t, F32, "d_h2", 512, 1024, 1408)
    gw_gu = _mm(h2.T, dgu, F32, "gw_gate_up", 512, 1024, 1024)
    dx1, dmix, p2 = _norm_bwd(x1, g_ffn, sc2, sh2, dh2, dx2, mix, gt1, name="norm2_bwd")
    dcat = _mm(dmix, w_out_t, F32, "d_cat", 512, 1024, 2048)
    gw_out = _mm(cat.T, dmix, F32, "gw_out", 512, 1024, 1024)
    datt, dy, dz, pm = _mixout_bwd(dcat, att, y, proj, g_att, g_ssd)
    dq, dk, dv, dbias_t = _attn_bwd(qkv, datt, bias_t)
    dxbc, ddt, pw, ph = _ssd_bwd(dy, pre, proj, prev, conv_w, dtb_p, al_p, dsk_p)
    dproj = jnp.concatenate([dq, dk, dv, dz, dxbc, ddt], axis=1)
    dh1 = _mm(dproj, w_in_t, F32, "d_h1", 512, 1024, 1152)
    gw_in = _mm(h1.T, dproj, F32, "gw_in", 512, 1152, 1024)
    gx, p1 = _norm_bwd(x, g_mix, sc1, sh1, dh1, dx1, name="norm1_bwd")

    onehot = (ridx.T.reshape(BAND * CH, 1) == jnp.arange(NREL)[None, :]).astype(F32)
    g_rel = _mm(dbias_t.reshape(NH, BAND * CH), onehot, F32, "g_rel_bias", NH, NREL, 4608, precision=HI)

    small = dict(
        dmods=jnp.concatenate([p1[1], p1[0], p2[3], p2[1], p2[0], pf[1]]),
        g_mix=p1[2], conv_b=pw[4], dt_bias=ph[0], a_log=ph[1], d_skip=ph[2], g_att=pm[0], g_ssd=pm[1],
        g_ffn=p2[2], g_final=pf[0], rel_bias=g_rel.reshape(-1), conv_w=pw[0:4].reshape(-1), loss=pf[2])
    return gx, gw_in, gw_out, gw_gu, gw_down, small


def _exchange(srcs, gather, name):
    n = len(srcs)
    outs = [jax.ShapeDtypeStruct((NDEV,) + (s.shape if g else s.shape[1:]), s.dtype) for s, g in zip(srcs, gather)]

    def body(*refs):
        src, dst = refs[:n], refs[n:2 * n]
        ssem, rsem, lsem = refs[2 * n:]
        x, y, c = lax.axis_index("x"), lax.axis_index("y"), lax.axis_index("c")
        me = 4 * x + 2 * y + c
        local = []
        for a in range(n):
            own = src[a] if gather[a] else src[a].at[me]
            cp = pltpu.make_async_copy(own, dst[a].at[me], lsem.at[a])
            cp.start()
            local.append(cp)
        remote = []
        for k in range(1, NDEV):
            px = 1 - x if k & 4 else x
            py = 1 - y if k & 2 else y
            pc = 1 - c if k & 1 else c
            pid = 4 * px + 2 * py + pc
            for a in range(n):
                s_ref = src[a] if gather[a] else src[a].at[pid]
                cp = pltpu.make_async_remote_copy(
                    src_ref=s_ref, dst_ref=dst[a].at[me], send_sem=ssem.at[a * (NDEV - 1) + k - 1],
                    recv_sem=rsem.at[a * (NDEV - 1) + k - 1], device_id=(px, py, pc), device_id_type=MESH)
                cp.start()
                remote.append(cp)
        for cp in remote:
            cp.wait()
        for cp in local:
            cp.wait()

    anyspec = pl.BlockSpec(memory_space=pl.ANY)
    return pl.pallas_call(
        body, name=name, out_shape=outs, in_specs=[anyspec] * n, out_specs=[anyspec] * n,
        scratch_shapes=[pltpu.SemaphoreType.DMA((n * (NDEV - 1),)), pltpu.SemaphoreType.DMA((n * (NDEV - 1),)),
                        pltpu.SemaphoreType.DMA((n,))],
    )(*srcs)


def _mods(c16, w_shard, b_shard, *, tn=512):
    n = w_shard.shape[1]

    def body(c_ref, w_ref, b_ref, o_ref):
        cv = c_ref[...]
        cond = (cv * jax.nn.sigmoid(cv)).astype(BF)
        o_ref[...] = jnp.dot(cond, w_ref[...].astype(BF), preferred_element_type=F32) + b_ref[...]

    return pl.pallas_call(body, name="ada_mods", grid=(n // tn,),
                          in_specs=[_fix((16, D)), pl.BlockSpec((D, tn), lambda j: (0, j)),
                                    pl.BlockSpec((1, tn), lambda j: (0, j))],
                          out_specs=pl.BlockSpec((16, tn), lambda j: (0, j)),
                          out_shape=jax.ShapeDtypeStruct((16, n), F32), compiler_params=_cp("parallel"))(c16, w_shard, b_shard)


def _gw_ada(c_t, dm, *, tr=256):
    n = dm.shape[1]

    def body(ct_ref, dm_ref, o_ref):
        acc = jnp.zeros((tr, n), F32)
        for b in range(NDEV):
            cv = ct_ref[:, b:b + 1]
            cond = (cv * jax.nn.sigmoid(cv)).astype(BF).astype(F32)
            acc = acc + cond * dm_ref[b:b + 1, :].astype(BF).astype(F32)
        o_ref[...] = acc

    return pl.pallas_call(body, name="gw_ada", grid=(D // tr,),
                          in_specs=[_row(tr, NDEV), _fix((NDEV, n))], out_specs=_row(tr, n),
                          out_shape=jax.ShapeDtypeStruct((D, n), F32), compiler_params=_cp("parallel"))(c_t, dm)


def _sum8(parts, *, name):
    _, r, c = parts.shape

    def body(p_ref, o_ref):
        acc = p_ref[0].astype(F32)
        for d in range(1, NDEV):
            acc = acc + p_ref[d].astype(F32)
        o_ref[...] = acc

    return pl.pallas_call(body, name=name, grid=(1,), in_specs=[_fix((NDEV, r, c))], out_specs=_fix((r, c)),
                          out_shape=jax.ShapeDtypeStruct((r, c), F32), compiler_params=_cp("arbitrary"))(parts)


def _adam(w, m, v, *, recv=None, grad=None, tr, name):
    r, c = w.shape
    summed = recv is not None
    assert r % tr == 0

    def body(g_ref, w_ref, m_ref, v_ref, go_ref, do_ref, mo_ref, vo_ref):
        if summed:
            g = g_ref[0].astype(F32)
            for d in range(1, NDEV):
                g = g + g_ref[d].astype(F32)
        else:
            g = g_ref[...]
        go_ref[...] = g
        mn = B1 * m_ref[...] + (1.0 - B1) * g
        vn = B2 * v_ref[...] + (1.0 - B2) * (g * g)
        mo_ref[...] = mn
        vo_ref[...] = vn
        mh = mn / (1.0 - B1 ** STEP)
        vh = vn / (1.0 - B2 ** STEP)
        do_ref[...] = -LR * (mh / (jnp.sqrt(vh) + AEPS) + WD * w_ref[...])

    gspec = pl.BlockSpec((NDEV, tr, c), lambda i: (0, i, 0)) if summed else _row(tr, c)
    return pl.pallas_call(body, name=name, grid=(r // tr,),
                          in_specs=[gspec, _row(tr, c), _row(tr, c), _row(tr, c)], out_specs=[_row(tr, c)] * 4,
                          out_shape=[jax.ShapeDtypeStruct((r, c), F32)] * 4,
                          compiler_params=_cp("parallel"))(recv if summed else grad, w, m, v)


def _cols_to_blocks(g, width):
    r = g.shape[0]
    return jnp.transpose(g.reshape(r, NDEV, width), (1, 0, 2))


def _blocks_to_cols(b):
    _, r, width = b.shape
    return jnp.transpose(b, (1, 0, 2)).reshape(r, NDEV * width)


_SMALL = ("b_ada", "g_mix", "rel_bias", "conv_w", "conv_b", "dt_bias", "a_log", "d_skip",
          "g_att_out", "g_ssd_out", "g_ffn", "g_final")


def _pad_rows(v):
    v = v.reshape(-1)
    return jnp.pad(v, (0, (-v.shape[0]) % LANE))


def kernel(x, c, w_ada, b_ada, g_mix, w_in, rel_bias, conv_w, conv_b, dt_bias, a_log, d_skip, g_att_out, g_ssd_out, w_out, g_ffn, w_gate, w_up, w_down, g_final, loss_target, m_w_ada, m_b_ada, m_g_mix, m_w_in, m_rel_bias, m_conv_w, m_conv_b, m_dt_bias, m_a_log, m_d_skip, m_g_att_out, m_g_ssd_out, m_w_out, m_g_ffn, m_w_gate, m_w_up, m_w_down, m_g_final, v_w_ada, v_b_ada, v_g_mix, v_w_in, v_rel_bias, v_conv_w, v_conv_b, v_dt_bias, v_a_log, v_d_skip, v_g_att_out, v_g_ssd_out, v_w_out, v_g_ffn, v_w_gate, v_w_up, v_w_down, v_g_final):
    me = 4 * lax.axis_index("x") + 2 * lax.axis_index("y") + lax.axis_index("c")
    wts = dict(w_ada=w_ada, b_ada=b_ada, g_mix=g_mix, w_in=w_in, rel_bias=rel_bias, conv_w=conv_w, conv_b=conv_b,
               dt_bias=dt_bias, a_log=a_log, d_skip=d_skip, g_att_out=g_att_out, g_ssd_out=g_ssd_out, w_out=w_out,
               g_ffn=g_ffn, w_gate=w_gate, w_up=w_up, w_down=w_down, g_final=g_final)
    mom = dict(w_ada=m_w_ada, b_ada=m_b_ada, g_mix=m_g_mix, w_in=m_w_in, rel_bias=m_rel_bias, conv_w=m_conv_w,
               conv_b=m_conv_b, dt_bias=m_dt_bias, a_log=m_a_log, d_skip=m_d_skip, g_att_out=m_g_att_out,
               g_ssd_out=m_g_ssd_out, w_out=m_w_out, g_ffn=m_g_ffn, w_gate=m_w_gate, w_up=m_w_up, w_down=m_w_down,
               g_final=m_g_final)
    var = dict(w_ada=v_w_ada, b_ada=v_b_ada, g_mix=v_g_mix, w_in=v_w_in, rel_bias=v_rel_bias, conv_w=v_conv_w,
               conv_b=v_conv_b, dt_bias=v_dt_bias, a_log=v_a_log, d_skip=v_d_skip, g_att_out=v_g_att_out,
               g_ssd_out=v_g_ssd_out, w_out=v_w_out, g_ffn=v_g_ffn, w_gate=v_w_gate, w_up=v_w_up, w_down=v_w_down,
               g_final=v_g_final)
    order = ("w_ada", "b_ada", "g_mix", "w_in", "rel_bias", "conv_w", "conv_b", "dt_bias", "a_log", "d_skip",
             "g_att_out", "g_ssd_out", "w_out", "g_ffn", "w_gate", "w_up", "w_down", "g_final")

    c_all, gi, gg, gu_, gd, go = _exchange(
        [c, w_in[0].astype(BF), w_gate[0].astype(BF), w_up[0].astype(BF), w_down[0].astype(BF), w_out[0].astype(BF)],
        [True] * 6, "gather_weights")
    c_all = c_all.reshape(NDEV, D)
    w_in_p = jnp.pad(_blocks_to_cols(gi), ((0, 0), (0, INP - INC)))
    w_gu = jnp.concatenate([_blocks_to_cols(gg), _blocks_to_cols(gu_)], axis=1)
    w_dn = gd.reshape(FF, D)
    w_o = go.reshape(D, D)

    ncol = w_ada.shape[2]
    b_sh = lax.dynamic_slice(b_ada, (0, me * ncol), (1, ncol))
    mods_part = _mods(jnp.pad(c_all, ((0, 8), (0, 0))), w_ada[0], b_sh)[:NDEV]
    (mods_all,) = _exchange([mods_part], [True], "gather_mods")
    mods_me = lax.dynamic_index_in_dim(mods_all, me, axis=1, keepdims=False).reshape(6, 1, D)
    mods = [mods_me[i] for i in range(6)]

    rel_all, cw_all = _exchange([rel_bias[0], conv_w[0]], [True, True], "gather_small_weights")
    rel_full = _blocks_to_cols(rel_all)
    cw_full = _blocks_to_cols(cw_all)

    gx, gw_in, gw_out, gw_gu, gw_down, small = _local_step(
        x[0], loss_target[0], mods, w_in_p, w_in_p.T, w_o, w_o.T, w_gu, w_gu.T, w_dn, w_dn.T,
        g_mix, rel_full, cw_full, conv_b, dt_bias, a_log, d_skip, g_att_out, g_ssd_out, g_ffn,
        g_final.reshape(1, D))

    loss_row = jnp.pad(jnp.sum(small["loss"]).reshape(1), (0, LANE - 1))
    pack = jnp.concatenate([small[k] for k in ("dmods", "g_mix", "conv_b", "dt_bias", "a_log", "d_skip", "g_att",
                                               "g_ssd", "g_ffn", "g_final", "rel_bias", "conv_w")] + [loss_row])
    pack = pack.reshape(-1, LANE)
    packs, ri, rg, ru, rd, ro = _exchange(
        [pack,
         _cols_to_blocks(gw_in[:, :INC], INC // NDEV).astype(BF),
         _cols_to_blocks(gw_gu[:, :FF], FF // NDEV).astype(BF),
         _cols_to_blocks(gw_gu[:, FF:], FF // NDEV).astype(BF),
         gw_down.reshape(NDEV, FF // NDEV, D).astype(BF),
         gw_out.reshape(NDEV, D // NDEV, D).astype(BF)],
        [True, False, False, False, False, False], "exchange_grads")
    tot = _sum8(packs, name="sum_small").reshape(-1)

    sizes = dict(dmods=6 * D, g_mix=D, conv_b=XBC, dt_bias=LANE, a_log=LANE, d_skip=LANE, g_att_out=AW, g_ssd_out=SW,
                 g_ffn=D, g_final=D, rel_bias=NH * NREL, conv_w=4 * XBC, loss=LANE)
    off, o = {}, 0
    for k_, n_ in sizes.items():
        off[k_] = o
        o += n_
    take = lambda k_, n_=None: tot[off[k_]:off[k_] + (n_ or sizes[k_])]
    loss = tot[off["loss"]]
    rel_w = NREL // NDEV
    cw_w = XBC // NDEV
    gsm = dict(
        b_ada=take("dmods"), g_mix=take("g_mix"),
        rel_bias=lax.dynamic_slice(take("rel_bias").reshape(NH, NREL), (0, me * rel_w), (NH, rel_w)),
        conv_w=lax.dynamic_slice(take("conv_w").reshape(4, XBC), (0, me * cw_w), (4, cw_w)),
        conv_b=take("conv_b"), dt_bias=take("dt_bias", NH), a_log=take("a_log", NH), d_skip=take("d_skip", NH),
        g_att_out=take("g_att_out"), g_ssd_out=take("g_ssd_out"), g_ffn=take("g_ffn"), g_final=take("g_final"))

    flat = lambda d_: jnp.concatenate([_pad_rows(d_[k_]) for k_ in _SMALL])
    gflat = flat(gsm)
    nrow = -(-gflat.shape[0] // (8 * LANE)) * 8
    to2d = lambda v_: jnp.pad(v_, (0, nrow * LANE - v_.shape[0])).reshape(nrow, LANE)
    res = _adam(to2d(flat(wts)), to2d(flat(mom)), to2d(flat(var)), grad=to2d(gflat), tr=nrow, name="adam_small")
    outs = {k_: {} for k_ in ("grad", "delta", "m", "v")}
    o = 0
    for k_ in _SMALL:
        n_ = wts[k_].size
        for kind, arr in zip(("grad", "delta", "m", "v"), res):
            outs[kind][k_] = arr.reshape(-1)[o:o + n_].reshape(wts[k_].shape)
        o += n_ + (-n_) % LANE

    dm_all = packs[:, :6 * D // LANE, :].reshape(NDEV, 6 * D)
    dm_sh = lax.dynamic_slice(dm_all, (0, me * ncol), (NDEV, ncol))
    gwa = _gw_ada(c_all.T, dm_sh)
    big = dict(
        w_ada=_adam(w_ada[0], m_w_ada[0], v_w_ada[0], grad=gwa, tr=256, name="adam_w_ada"),
        w_in=_adam(w_in[0], m_w_in[0], v_w_in[0], recv=ri, tr=256, name="adam_w_in"),
        w_gate=_adam(w_gate[0], m_w_gate[0], v_w_gate[0], recv=rg, tr=256, name="adam_w_gate"),
        w_up=_adam(w_up[0], m_w_up[0], v_w_up[0], recv=ru, tr=256, name="adam_w_up"),
        w_down=_adam(w_down[0], m_w_down[0], v_w_down[0], recv=rd, tr=176, name="adam_w_down"),
        w_out=_adam(w_out[0], m_w_out[0], v_w_out[0], recv=ro, tr=128, name="adam_w_out"))
    for k_, r_ in big.items():
        for kind, arr in zip(("grad", "delta", "m", "v"), r_):
            outs[kind][k_] = arr.reshape(wts[k_].shape)

    return (loss, gx.reshape(x.shape), *[outs["grad"][k_] for k_ in order], *[outs["delta"][k_] for k_ in order],
            *[outs["m"][k_] for k_ in order], *[outs["v"][k_] for k_ in order])
```

```python
import functools

import jax
import jax.numpy as jnp
from jax import lax
from jax.experimental import pallas as pl
from jax.experimental.pallas import tpu as pltpu

F32 = jnp.float32
BF = jnp.bfloat16
HI = lax.Precision.HIGHEST

D = 2048
NH = 16
HD = 64
AW = 1024
SW = 1024
NG = 2
NS = 128
XBC = 1536
CH = 64
BAND = 576
NREL = 320
REL_CLIP = 256
FF = 5632
INC = 5648
INP = 5760
EPS = 1e-6
NDEV = 8
LANE = 128
VMEM_LIMIT = 56 * 1024 * 1024
LR, B1, B2, AEPS, WD, STEP = 0.001, 0.9, 0.999, 1e-08, 0.01, 10
MESH = pl.DeviceIdType.MESH


def _cp(*sem):
    return pltpu.CompilerParams(dimension_semantics=sem, vmem_limit_bytes=VMEM_LIMIT)


def _row(ts, w, col=0):
    return pl.BlockSpec((ts, w), lambda i, _c=col: (i, _c))


def _fix(shape, single=False):
    nd = len(shape)
    if single:
        return pl.BlockSpec(shape, lambda i, _n=nd: (0,) * _n, pipeline_mode=pl.Buffered(1))
    return pl.BlockSpec(shape, lambda i, _n=nd: (0,) * _n)


def _silu_grad(x, s):
    return s * (1.0 + x * (1.0 - s))


def _softplus(x):
    return jnp.maximum(x, 0.0) + jnp.log1p(jnp.exp(-jnp.abs(x)))


def _mm(a, b, out_dtype, name, tm, tn, tk, precision=None):
    m, k = a.shape
    _, n = b.shape
    nk = k // tk
    assert m % tm == 0 and n % tn == 0 and k % tk == 0, (name, a.shape, b.shape)

    def body(a_ref, b_ref, o_ref, acc):
        kk = pl.program_id(2)

        @pl.when(kk == 0)
        def _():
            acc[...] = jnp.zeros_like(acc)

        acc[...] += jnp.dot(a_ref[...], b_ref[...], preferred_element_type=F32, precision=precision)

        @pl.when(kk == nk - 1)
        def _():
            o_ref[...] = acc[...].astype(o_ref.dtype)

    return pl.pallas_call(
        body, name=name, grid=(m // tm, n // tn, nk),
        in_specs=[pl.BlockSpec((tm, tk), lambda i, j, kk: (i, kk)),
                  pl.BlockSpec((tk, tn), lambda i, j, kk: (kk, j))],
        out_specs=pl.BlockSpec((tm, tn), lambda i, j, kk: (i, j)),
        out_shape=jax.ShapeDtypeStruct((m, n), out_dtype),
        scratch_shapes=[pltpu.VMEM((tm, tn), F32)],
        compiler_params=_cp("parallel", "parallel", "arbitrary"),
    )(a, b)


def _norm_fwd(x, g, sc, sh, mix=None, gt=None, *, name, ts=256):
    s, d = x.shape
    has = mix is not None

    def body(*refs):
        if has:
            x_ref, g_ref, sc_ref, sh_ref, mix_ref, gt_ref, h_ref, x1_ref = refs
        else:
            x_ref, g_ref, sc_ref, sh_ref, h_ref = refs
        xv = x_ref[...]
        if has:
            xv = xv + gt_ref[...] * mix_ref[...]
            x1_ref[...] = xv
        r = lax.rsqrt(jnp.mean(xv * xv, axis=-1, keepdims=True) + EPS)
        h_ref[...] = ((xv * r) * g_ref[...] * (1.0 + sc_ref[...]) + sh_ref[...]).astype(BF)

    ins = [x, g, sc, sh] + ([mix, gt] if has else [])
    in_specs = [_row(ts, d), _fix((1, d)), _fix((1, d)), _fix((1, d))] + ([_row(ts, d), _fix((1, d))] if has else [])
    out_shape = [jax.ShapeDtypeStruct((s, d), BF)] + ([jax.ShapeDtypeStruct((s, d), F32)] if has else [])
    out_specs = [_row(ts, d)] + ([_row(ts, d)] if has else [])
    out = pl.pallas_call(body, name=name, grid=(s // ts,), in_specs=in_specs, out_specs=out_specs,
                         out_shape=out_shape, compiler_params=_cp("parallel"))(*ins)
    return out if has else out[0]


def _norm_bwd(x, g, sc, sh, dh, dres, mix=None, gt=None, *, name, ts=256):
    s, d = x.shape
    has = mix is not None

    def body(*refs):
        if has:
            x_ref, g_ref, sc_ref, sh_ref, dh_ref, dres_ref, mix_ref, gt_ref, dx_ref, dmix_ref, part_ref = refs
        else:
            x_ref, g_ref, sc_ref, sh_ref, dh_ref, dres_ref, dx_ref, part_ref = refs
        i = pl.program_id(0)
        xv = x_ref[...]
        gv = g_ref[...]
        dhv = dh_ref[...]
        r = lax.rsqrt(jnp.mean(xv * xv, axis=-1, keepdims=True) + EPS)
        xh = xv * r
        dyn = dhv * (1.0 + sc_ref[...])
        gy = dyn * gv
        dx = dres_ref[...] + r * (gy - xh * jnp.mean(gy * xh, axis=-1, keepdims=True))
        dx_ref[...] = dx

        @pl.when(i == 0)
        def _():
            part_ref[...] = jnp.zeros_like(part_ref)

        part_ref[0:1, :] += jnp.sum(dhv * (xh * gv), axis=0, keepdims=True)
        part_ref[1:2, :] += jnp.sum(dhv, axis=0, keepdims=True)
        part_ref[2:3, :] += jnp.sum(dyn * xh, axis=0, keepdims=True)
        if has:
            part_ref[3:4, :] += jnp.sum(dx * mix_ref[...], axis=0, keepdims=True)
            dmix_ref[...] = (gt_ref[...] * dx).astype(BF)

    ins = [x, g, sc, sh, dh, dres] + ([mix, gt] if has else [])
    in_specs = [_row(ts, d), _fix((1, d)), _fix((1, d)), _fix((1, d)), _row(ts, d), _row(ts, d)]
    in_specs += [_row(ts, d), _fix((1, d))] if has else []
    out_shape = [jax.ShapeDtypeStruct((s, d), F32)] + ([jax.ShapeDtypeStruct((s, d), BF)] if has else [])
    out_shape += [jax.ShapeDtypeStruct((8, d), F32)]
    out_specs = [_row(ts, d)] + ([_row(ts, d)] if has else []) + [_fix((8, d))]
    return pl.pallas_call(body, name=name, grid=(s // ts,), in_specs=in_specs, out_specs=out_specs,
                          out_shape=out_shape, compiler_params=_cp("arbitrary"))(*ins)


def _final(x1, ffn, gt2, gf, tgt, *, ts=256):
    s, d = x1.shape

    def body(x1_ref, ffn_ref, gt_ref, g_ref, t_ref, dx_ref, dffn_ref, part_ref):
        i = pl.program_id(0)
        fv = ffn_ref[...]
        gv = g_ref[...]
        xv = x1_ref[...] + gt_ref[...] * fv
        r = lax.rsqrt(jnp.mean(xv * xv, axis=-1, keepdims=True) + EPS)
        xh = xv * r
        e = xh * gv - t_ref[...]
        dy = e * (1.0 / d)
        gy = dy * gv
        dx = r * (gy - xh * jnp.mean(gy * xh, axis=-1, keepdims=True))
        dx_ref[...] = dx
        dffn_ref[...] = (gt_ref[...] * dx).astype(BF)

        @pl.when(i == 0)
        def _():
            part_ref[...] = jnp.zeros_like(part_ref)

        part_ref[0:1, :] += jnp.sum(dy * xh, axis=0, keepdims=True)
        part_ref[1:2, :] += jnp.sum(dx * fv, axis=0, keepdims=True)
        part_ref[2:3, :] += jnp.sum(e * e, axis=0, keepdims=True) * (0.5 / d)

    return pl.pallas_call(
        body, name="final_loss", grid=(s // ts,),
        in_specs=[_row(ts, d), _row(ts, d), _fix((1, d)), _fix((1, d)), _row(ts, d)],
        out_specs=[_row(ts, d), _row(ts, d), _fix((8, d))],
        out_shape=[jax.ShapeDtypeStruct((s, d), F32), jax.ShapeDtypeStruct((s, d), BF),
                   jax.ShapeDtypeStruct((8, d), F32)],
        compiler_params=_cp("arbitrary"))(x1, ffn, gt2, gf, tgt)


def _swiglu_fwd(gu, *, ts=256):
    s = gu.shape[0]

    def body(g_ref, u_ref, o_ref):
        gv = g_ref[...].astype(F32)
        o_ref[...] = (gv * jax.nn.sigmoid(gv) * u_ref[...].astype(F32)).astype(BF)

    return pl.pallas_call(body, name="swiglu_fwd", grid=(s // ts,),
                          in_specs=[_row(ts, FF, 0), _row(ts, FF, 1)], out_specs=_row(ts, FF),
                          out_shape=jax.ShapeDtypeStruct((s, FF), BF), compiler_params=_cp("parallel"))(gu, gu)


def _swiglu_bwd(gu, dact, *, ts=256):
    s = gu.shape[0]

    def body(g_ref, u_ref, da_ref, o_ref):
        gv = g_ref[...].astype(F32)
        uv = u_ref[...].astype(F32)
        dav = da_ref[...].astype(F32)
        sg = jax.nn.sigmoid(gv)
        o_ref[:, 0:FF] = (dav * uv * _silu_grad(gv, sg)).astype(BF)
        o_ref[:, FF:2 * FF] = (dav * gv * sg).astype(BF)

    return pl.pallas_call(body, name="swiglu_bwd", grid=(s // ts,),
                          in_specs=[_row(ts, FF, 0), _row(ts, FF, 1), _row(ts, FF)], out_specs=_row(ts, 2 * FF),
                          out_shape=jax.ShapeDtypeStruct((s, 2 * FF), BF),
                          compiler_params=_cp("parallel"))(gu, gu, dact)


def _mixout_fwd(att, y, proj, g_att, g_ssd, *, ts=256):
    s = att.shape[0]

    def body(a_ref, y_ref, z_ref, ga_ref, gs_ref, o_ref):
        av = a_ref[...]
        ra = lax.rsqrt(jnp.mean(av * av, axis=-1, keepdims=True) + EPS)
        o_ref[:, 0:AW] = (av * ra * ga_ref[...]).astype(BF)
        zv = z_ref[...]
        yz = y_ref[...] * (zv * jax.nn.sigmoid(zv))
        rs = lax.rsqrt(jnp.mean(yz * yz, axis=-1, keepdims=True) + EPS)
        o_ref[:, AW:AW + SW] = (yz * rs * gs_ref[...]).astype(BF)

    return pl.pallas_call(body, name="mixout_fwd", grid=(s // ts,),
                          in_specs=[_row(ts, AW), _row(ts, SW), _row(ts, SW, 3), _fix((1, AW)), _fix((1, SW))],
                          out_specs=_row(ts, AW + SW), out_shape=jax.ShapeDtypeStruct((s, AW + SW), BF),
                          compiler_params=_cp("parallel"))(att, y, proj, g_att, g_ssd)


def _mixout_bwd(dcat, att, y, proj, g_att, g_ssd, *, ts=256):
    s = att.shape[0]

    def body(dc_ref, a_ref, y_ref, z_ref, ga_ref, gs_ref, da_ref, dy_ref, dz_ref, part_ref):
        i = pl.program_id(0)
        av = a_ref[...]
        dca = dc_ref[:, 0:AW]
        ra = lax.rsqrt(jnp.mean(av * av, axis=-1, keepdims=True) + EPS)
        ah = av * ra
        gy = dca * ga_ref[...]
        da_ref[...] = ra * (gy - ah * jnp.mean(gy * ah, axis=-1, keepdims=True))
        zv = z_ref[...]
        yv = y_ref[...]
        sg = jax.nn.sigmoid(zv)
        sz = zv * sg
        yz = yv * sz
        dcs = dc_ref[:, AW:AW + SW]
        rs = lax.rsqrt(jnp.mean(yz * yz, axis=-1, keepdims=True) + EPS)
        yh = yz * rs
        gys = dcs * gs_ref[...]
        dyz = rs * (gys - yh * jnp.mean(gys * yh, axis=-1, keepdims=True))
        dy_ref[...] = dyz * sz
        dz_ref[...] = (dyz * yv * _silu_grad(zv, sg)).astype(BF)

        @pl.when(i == 0)
        def _():
            part_ref[...] = jnp.zeros_like(part_ref)

        part_ref[0:1, :] += jnp.sum(dca * ah, axis=0, keepdims=True)
        part_ref[1:2, :] += jnp.sum(dcs * yh, axis=0, keepdims=True)

    return pl.pallas_call(
        body, name="mixout_bwd", grid=(s // ts,),
        in_specs=[_row(ts, AW + SW), _row(ts, AW), _row(ts, SW), _row(ts, SW, 3), _fix((1, AW)), _fix((1, SW))],
        out_specs=[_row(ts, AW), _row(ts, SW), _row(ts, SW), _fix((8, AW))],
        out_shape=[jax.ShapeDtypeStruct((s, AW), F32), jax.ShapeDtypeStruct((s, SW), F32),
                   jax.ShapeDtypeStruct((s, SW), BF), jax.ShapeDtypeStruct((8, AW), F32)],
        compiler_params=_cp("arbitrary"))(dcat, att, y, proj, g_att, g_ssd)


TQ = 512
SCALE = HD ** -0.5


def _attn_fwd(qkv, bias):
    s = qkv.shape[0]
    nb = s // TQ
    prev = lambda i: jnp.maximum(i - 1, 0)

    def body(q_ref, kp_ref, kc_ref, vp_ref, vc_ref, b_ref, o_ref, kbuf, vbuf):
        i = pl.program_id(0)
        kbuf[0:TQ, :] = kp_ref[...]
        kbuf[TQ:2 * TQ, :] = kc_ref[...]
        vbuf[0:TQ, :] = vp_ref[...]
        vbuf[TQ:2 * TQ, :] = vc_ref[...]

        def chunk(c, carry):
            r0 = pl.multiple_of(c * CH, CH)
            kpos = i * TQ + r0 - TQ + lax.broadcasted_iota(jnp.int32, (1, BAND), 1)
            valid = kpos >= 0
            for h in range(NH):
                sl = slice(HD * h, HD * h + HD)
                q = q_ref[pl.ds(r0, CH), sl]
                k = kbuf[pl.ds(r0, BAND), sl]
                v = vbuf[pl.ds(r0, BAND), sl]
                sc = lax.dot_general(q, k, (((1,), (1,)), ((), ())), preferred_element_type=F32) * SCALE + b_ref[h]
                sc = jnp.where(valid, sc, -jnp.inf)
                e = jnp.exp(sc - jnp.max(sc, axis=-1, keepdims=True))
                p = e / jnp.sum(e, axis=-1, keepdims=True)
                o_ref[pl.ds(r0, CH), sl] = jnp.dot(p.astype(BF), v, preferred_element_type=F32)
            return carry

        lax.fori_loop(0, TQ // CH, chunk, 0)

    blk = lambda col, im: pl.BlockSpec((TQ, AW), lambda i, _c=col, _f=im: (_f(i), _c))
    cur = lambda i: i
    return pl.pallas_call(
        body, name="attn_fwd", grid=(nb,),
        in_specs=[blk(0, cur), blk(1, prev), blk(1, cur), blk(2, prev), blk(2, cur), _fix((NH, CH, BAND))],
        out_specs=_row(TQ, AW), out_shape=jax.ShapeDtypeStruct((s, AW), F32),
        scratch_shapes=[pltpu.VMEM((2 * TQ, AW), BF), pltpu.VMEM((2 * TQ, AW), BF)],
        compiler_params=_cp("parallel"))(qkv, qkv, qkv, qkv, qkv, bias)


def _attn_bwd(qkv, datt, bias_t):
    s = qkv.shape[0]
    nb = s // TQ
    cur = lambda i: jnp.minimum(i, nb - 1)
    prev = lambda i: jnp.maximum(jnp.minimum(i, nb - 1) - 1, 0)
    late = lambda i: jnp.maximum(i - 1, 0)

    def body(q_ref, kp_ref, kc_ref, vp_ref, vc_ref, do_ref, b_ref, dq_ref, dk_ref, dv_ref, db_ref,
             kbuf, vbuf, dkacc, dvacc):
        i = pl.program_id(0)

        @pl.when(i == 0)
        def _():
            dkacc[...] = jnp.zeros_like(dkacc)
            dvacc[...] = jnp.zeros_like(dvacc)
            db_ref[...] = jnp.zeros_like(db_ref)

        @pl.when(i < nb)
        def _():
            kbuf[0:TQ, :] = kp_ref[...]
            kbuf[TQ:2 * TQ, :] = kc_ref[...]
            vbuf[0:TQ, :] = vp_ref[...]
            vbuf[TQ:2 * TQ, :] = vc_ref[...]

            def chunk(c, carry):
                r0 = pl.multiple_of(c * CH, CH)
                kpos = i * TQ + r0 - TQ + lax.broadcasted_iota(jnp.int32, (BAND, 1), 0)
                valid = kpos >= 0
                for h in range(NH):
                    sl = slice(HD * h, HD * h + HD)
                    q = q_ref[pl.ds(r0, CH), sl]
                    k = kbuf[pl.ds(r0, BAND), sl]
                    v = vbuf[pl.ds(r0, BAND), sl]
                    do = do_ref[pl.ds(r0, CH), sl].astype(BF)
                    st = lax.dot_general(k, q, (((1,), (1,)), ((), ())), preferred_element_type=F32) * SCALE + b_ref[h]
                    st = jnp.where(valid, st, -jnp.inf)
                    e = jnp.exp(st - jnp.max(st, axis=0, keepdims=True))
                    pt = e / jnp.sum(e, axis=0, keepdims=True)
                    dvacc[pl.ds(r0, BAND), sl] += jnp.dot(pt.astype(BF), do, preferred_element_type=F32)
                    dpt = lax.dot_general(v, do, (((1,), (1,)), ((), ())), preferred_element_type=F32)
                    dst = pt * (dpt - jnp.sum(dpt * pt, axis=0, keepdims=True))
                    db_ref[h] += dst
                    dsb = dst.astype(BF)
                    dkacc[pl.ds(r0, BAND), sl] += jnp.dot(dsb, q, preferred_element_type=F32) * SCALE
                    dq = lax.dot_general(dsb, k, (((0,), (0,)), ((), ())), preferred_element_type=F32) * SCALE
                    dq_ref[pl.ds(r0, CH), sl] = dq.astype(BF)
                return carry

            lax.fori_loop(0, TQ // CH, chunk, 0)

        dk_ref[...] = dkacc[0:TQ, :].astype(BF)
        dv_ref[...] = dvacc[0:TQ, :].astype(BF)
        dkacc[0:TQ, :] = dkacc[TQ:2 * TQ, :]
        dvacc[0:TQ, :] = dvacc[TQ:2 * TQ, :]
        dkacc[TQ:2 * TQ, :] = jnp.zeros((TQ, AW), F32)
        dvacc[TQ:2 * TQ, :] = jnp.zeros((TQ, AW), F32)

    blk = lambda col, im: pl.BlockSpec((TQ, AW), lambda i, _c=col, _f=im: (_f(i), _c))
    return pl.pallas_call(
        body, name="attn_bwd", grid=(nb + 1,),
        in_specs=[blk(0, cur), blk(1, prev), blk(1, cur), blk(2, prev), blk(2, cur), blk(0, cur),
                  _fix((NH, BAND, CH))],
        out_specs=[blk(0, cur), blk(0, late), blk(0, late), _fix((NH, BAND, CH))],
        out_shape=[jax.ShapeDtypeStruct((s, AW), BF)] * 3 + [jax.ShapeDtypeStruct((NH, BAND, CH), F32)],
        scratch_shapes=[pltpu.VMEM((2 * TQ, AW), BF), pltpu.VMEM((2 * TQ, AW), BF),
                        pltpu.VMEM((2 * TQ, AW), F32), pltpu.VMEM((2 * TQ, AW), F32)],
        compiler_params=_cp("arbitrary"))(qkv, qkv, qkv, qkv, qkv, datt, bias_t)


QB = 256
NQC = QB // CH
WIN = 3 * QB


def _bias_window(bias):
    rows = [jnp.pad(bias, ((0, 0), (0, 0), (CH * a, WIN - BAND - CH * a)), constant_values=-jnp.inf)
            for a in range(NQC)]
    return jnp.concatenate(rows, axis=1)


def _win_specs(width, nb):
    def mk(col, back):
        return pl.BlockSpec((QB, width), lambda i, _c=col, _k=back: (jnp.maximum(jnp.minimum(i, nb - 1) - _k, 0), _c))
    return [mk(1, 2), mk(1, 1), mk(1, 0), mk(2, 2), mk(2, 1), mk(2, 0)]


def _attn2_fwd(qkv, bias_w):
    s = qkv.shape[0]
    nb = s // QB

    def body(q_ref, k0, k1, k2, v0, v1, v2, b_ref, o_ref, kbuf, vbuf):
        i = pl.program_id(0)
        for j, (kr, vr) in enumerate(((k0, v0), (k1, v1), (k2, v2))):
            kbuf[QB * j:QB * j + QB, :] = kr[...]
            vbuf[QB * j:QB * j + QB, :] = vr[...]
        valid = (i * QB - 2 * QB + lax.broadcasted_iota(jnp.int32, (1, WIN), 1)) >= 0
        for h in range(NH):
            sl = slice(HD * h, HD * h + HD)
            sc = lax.dot_general(q_ref[:, sl], kbuf[:, sl], (((1,), (1,)), ((), ())), preferred_element_type=F32)
            sc = jnp.where(valid, sc * SCALE + b_ref[h], -jnp.inf)
            e = jnp.exp(sc - jnp.max(sc, axis=-1, keepdims=True))
            p = e * (1.0 / jnp.sum(e, axis=-1, keepdims=True))
            o_ref[:, sl] = jnp.dot(p.astype(BF), vbuf[:, sl], preferred_element_type=F32)

    return pl.pallas_call(
        body, name="attn_fwd", grid=(nb,),
        in_specs=[_row(QB, AW)] + _win_specs(AW, nb) + [_fix((NH, QB, WIN), single=True)],
        out_specs=_row(QB, AW), out_shape=jax.ShapeDtypeStruct((s, AW), F32),
        scratch_shapes=[pltpu.VMEM((WIN, AW), BF), pltpu.VMEM((WIN, AW), BF)],
        compiler_params=_cp("parallel"))(qkv, qkv, qkv, qkv, qkv, qkv, qkv, bias_w)


def _attn2_bwd(qkv, datt, bias_wt):
    s = qkv.shape[0]
    nb = s // QB
    cur = lambda i: jnp.minimum(i, nb - 1)
    late = lambda i: jnp.maximum(i - 2, 0)

    def body(q_ref, k0, k1, k2, v0, v1, v2, do_ref, b_ref, dq_ref, dk_ref, dv_ref, db_ref,
             kbuf, vbuf, dkacc, dvacc, dsbuf):
        i = pl.program_id(0)

        @pl.when(i == 0)
        def _():
            dkacc[...] = jnp.zeros_like(dkacc)
            dvacc[...] = jnp.zeros_like(dvacc)
            db_ref[...] = jnp.zeros_like(db_ref)

        @pl.when(i < nb)
        def _():
            for j, (kr, vr) in enumerate(((k0, v0), (k1, v1), (k2, v2))):
                kbuf[QB * j:QB * j + QB, :] = kr[...]
                vbuf[QB * j:QB * j + QB, :] = vr[...]
            valid = (i * QB - 2 * QB + lax.broadcasted_iota(jnp.int32, (WIN, 1), 0)) >= 0
            for h in range(NH):
                sl = slice(HD * h, HD * h + HD)
                q = q_ref[:, sl]
                k = kbuf[:, sl]
                do = do_ref[:, sl].astype(BF)
                st = lax.dot_general(k, q, (((1,), (1,)), ((), ())), preferred_element_type=F32)
                st = jnp.where(valid, st * SCALE + b_ref[h], -jnp.inf)
                e = jnp.exp(st - jnp.max(st, axis=0, keepdims=True))
                pt = e * (1.0 / jnp.sum(e, axis=0, keepdims=True))
                dvacc[:, sl] += jnp.dot(pt.astype(BF), do, preferred_element_type=F32)
                dpt = lax.dot_general(vbuf[:, sl], do, (((1,), (1,)), ((), ())), preferred_element_type=F32)
                dst = pt * (dpt - jnp.sum(dpt * pt, axis=0, keepdims=True))
                dsbuf[...] = dst
                fold = dsbuf[0:BAND, 0:CH]
                for a in range(1, NQC):
                    fold = fold + dsbuf[CH * a:CH * a + BAND, CH * a:CH * a + CH]
                db_ref[h] += fold
                dsb = dst.astype(BF)
                dkacc[:, sl] += jnp.dot(dsb, q, preferred_element_type=F32) * SCALE
                dq = lax.dot_general(dsb, k, (((0,), (0,)), ((), ())), preferred_element_type=F32) * SCALE
                dq_ref[:, sl] = dq.astype(BF)

        dk_ref[...] = dkacc[0:QB, :].astype(BF)
        dv_ref[...] = dvacc[0:QB, :].astype(BF)
        for acc in (dkacc, dvacc):
            acc[0:QB, :] = acc[QB:2 * QB, :]
            acc[QB:2 * QB, :] = acc[2 * QB:WIN, :]
            acc[2 * QB:WIN, :] = jnp.zeros((QB, AW), F32)

    qspec = pl.BlockSpec((QB, AW), lambda i: (cur(i), 0))
    lspec = pl.BlockSpec((QB, AW), lambda i: (late(i), 0))
    return pl.pallas_call(
        body, name="attn_bwd", grid=(nb + 2,),
        in_specs=[qspec] + _win_specs(AW, nb) + [qspec, _fix((NH, WIN, QB), single=True)],
        out_specs=[qspec, lspec, lspec, _fix((NH, BAND, CH))],
        out_shape=[jax.ShapeDtypeStruct((s, AW), BF)] * 3 + [jax.ShapeDtypeStruct((NH, BAND, CH), F32)],
        scratch_shapes=[pltpu.VMEM((WIN, AW), BF), pltpu.VMEM((WIN, AW), BF),
                        pltpu.VMEM((WIN, AW), F32), pltpu.VMEM((WIN, AW), F32), pltpu.VMEM((WIN, QB), F32)],
        compiler_params=_cp("arbitrary"))(qkv, qkv, qkv, qkv, qkv, qkv, qkv, datt, bias_wt)


TB = 512
SCH = 256
NCB = TB // SCH
GW = 8 * HD


def _tri(lower):
    r = lax.broadcasted_iota(jnp.int32, (SCH, SCH), 0)
    c = lax.broadcasted_iota(jnp.int32, (SCH, SCH), 1)
    return r >= c if lower else r <= c


def _ssd_fwd(proj, conv_w, conv_b, dt_bias, a_log, d_skip):
    s = proj.shape[0]
    nb = s // TB

    def body(xs_ref, bc_ref, dt_ref, cw_ref, cb_ref, dtb_ref, al_ref, dsk_ref, y_ref, pre_ref, prev_ref,
             xpad, ubuf, dtbuf, csb, cstb, hst, xdec, yoffb, drow):
        i = pl.program_id(0)

        @pl.when(i == 0)
        def _():
            xpad[0:8, :] = jnp.zeros((8, XBC), F32)
            hst[...] = jnp.zeros_like(hst)

        xpad[8:8 + TB, 0:SW] = xs_ref[...]
        xpad[8:8 + TB, SW:XBC] = bc_ref[...]
        pre = cb_ref[...]
        for kk in range(4):
            pre = pre + cw_ref[kk:kk + 1, :] * xpad[5 + kk:5 + kk + TB, :]
        pre_ref[...] = pre
        ubuf[...] = pre * jax.nn.sigmoid(pre)
        xpad[0:8, :] = xpad[TB:TB + 8, :]
        dtbuf[...] = _softplus(dt_ref[...] + dtb_ref[...])
        a = -jnp.exp(al_ref[...])
        tri = _tri(True).astype(F32)
        causal = _tri(True)

        def chunk(c, carry):
            r0 = pl.multiple_of(c * SCH, SCH)
            rows = pl.ds(r0, SCH)
            cs = jnp.dot(tri, dtbuf[rows, :] * a, precision=HI, preferred_element_type=F32)
            csb[...] = cs
            cstb[...] = cs.T
            for g in range(NG):
                bg = ubuf[rows, SW + NS * g:SW + NS * g + NS]
                cg = ubuf[rows, SW + NG * NS + NS * g:SW + NG * NS + NS * g + NS].astype(BF)
                cb = lax.dot_general(cg, bg.astype(BF), (((1,), (1,)), ((), ())), preferred_element_type=F32)
                hg = hst[g]
                prev_ref[c, g] = hg
                yoffb[...] = jnp.dot(cg, hg.astype(BF), preferred_element_type=F32)
                for r in range(8):
                    h = 8 * g + r
                    sl = slice(HD * h, HD * h + HD)
                    rs = slice(HD * r, HD * r + HD)
                    cs_h = csb[:, h:h + 1]
                    cl = csb[SCH - 1:SCH, h:h + 1]
                    seg = jnp.exp(jnp.where(causal, cs_h - cstb[h:h + 1, :], -jnp.inf))
                    xh = ubuf[rows, sl]
                    xdt = xh * dtbuf[rows, h:h + 1]
                    yd = jnp.dot((cb * seg).astype(BF), xdt.astype(BF), preferred_element_type=F32)
                    y_ref[rows, sl] = yd + jnp.exp(cs_h) * yoffb[:, rs] + xh * dsk_ref[:, h:h + 1]
                    xdec[:, rs] = xdt * jnp.exp(cl - cs_h)
                    drow[:, rs] = jnp.broadcast_to(jnp.exp(cl), (1, HD))
                st = jnp.dot(bg.T.astype(BF), xdec[...].astype(BF), preferred_element_type=F32)
                hst[g] = hg * drow[...] + st
            return carry

        lax.fori_loop(0, NCB, chunk, 0)

    return pl.pallas_call(
        body, name="ssd_fwd", grid=(nb,),
        in_specs=[_row(TB, SW, 4), _row(TB, 512, 10), _row(TB, LANE, 44), _fix((4, XBC)), _fix((1, XBC)),
                  _fix((1, LANE)), _fix((1, LANE)), _fix((1, LANE))],
        out_specs=[_row(TB, SW), _row(TB, XBC), pl.BlockSpec((NCB, NG, NS, GW), lambda i: (i, 0, 0, 0))],
        out_shape=[jax.ShapeDtypeStruct((s, SW), F32), jax.ShapeDtypeStruct((s, XBC), F32),
                   jax.ShapeDtypeStruct((s // SCH, NG, NS, GW), F32)],
        scratch_shapes=[pltpu.VMEM((TB + 8, XBC), F32), pltpu.VMEM((TB, XBC), F32), pltpu.VMEM((TB, LANE), F32),
                        pltpu.VMEM((SCH, LANE), F32), pltpu.VMEM((LANE, SCH), F32), pltpu.VMEM((NG, NS, GW), F32),
                        pltpu.VMEM((SCH, GW), F32), pltpu.VMEM((SCH, GW), F32), pltpu.VMEM((1, GW), F32)],
        compiler_params=_cp("arbitrary"))(proj, proj, proj, conv_w, conv_b, dt_bias, a_log, d_skip)


def _ssd_bwd(dy, pre, proj, prev, conv_w, dt_bias, a_log, d_skip):
    s = dy.shape[0]
    nb = s // TB
    rev = lambda i: nb - 1 - i
    halo = lambda i: jnp.maximum((nb - 1 - i) * (TB // 8) - 1, 0)

    def body(dy_ref, pre_ref, dt_ref, prev_ref, xs_ref, bc_ref, xsh_ref, bch_ref, cw_ref, dtb_ref, al_ref, dsk_ref,
             dx_ref, ddt_ref, pw_ref, ph_ref,
             ubuf, dtbuf, dpad, xpad, csb, cstb, dhs, yoffb, ebuf, xdecb, wbuf, drow, dub):
        i = pl.program_id(0)

        @pl.when(i == 0)
        def _():
            dhs[...] = jnp.zeros_like(dhs)
            dpad[TB:TB + 8, :] = jnp.zeros((8, XBC), F32)
            pw_ref[...] = jnp.zeros_like(pw_ref)
            ph_ref[...] = jnp.zeros_like(ph_ref)

        pre = pre_ref[...]
        ubuf[...] = pre * jax.nn.sigmoid(pre)
        dtbuf[...] = _softplus(dt_ref[...] + dtb_ref[...])
        a = -jnp.exp(al_ref[...])
        tri = _tri(True).astype(F32)
        trit = _tri(False).astype(F32)
        causal = _tri(True)
        upper = _tri(False)
        lane = lax.broadcasted_iota(jnp.int32, (SCH, LANE), 1)
        lane1 = lax.broadcasted_iota(jnp.int32, (1, LANE), 1)
        lastrow = lax.broadcasted_iota(jnp.int32, (SCH, LANE), 0) == SCH - 1

        def chunk(cc, carry):
            c = NCB - 1 - cc
            r0 = pl.multiple_of(c * SCH, SCH)
            rows = pl.ds(r0, SCH)
            dtc = dtbuf[rows, :]
            cs = jnp.dot(tri, dtc * a, precision=HI, preferred_element_type=F32)
            csb[...] = cs
            cstb[...] = cs.T
            dcs = jnp.zeros((SCH, LANE), F32)
            xr = jnp.zeros((SCH, LANE), F32)
            dlast = jnp.zeros((1, LANE), F32)
            dsk = jnp.zeros((1, LANE), F32)
            for g in range(NG):
                bg = ubuf[rows, SW + NS * g:SW + NS * g + NS]
                cg = ubuf[rows, SW + NG * NS + NS * g:SW + NG * NS + NS * g + NS]
                bb = bg.astype(BF)
                cbf = cg.astype(BF)
                cb = lax.dot_general(cbf, bb, (((1,), (1,)), ((), ())), preferred_element_type=F32)
                cbt = lax.dot_general(bb, cbf, (((1,), (1,)), ((), ())), preferred_element_type=F32)
                hp = prev_ref[c, g]
                hpb = hp.astype(BF)
                dhg = dhs[g]
                dhb = dhg.astype(BF)
                yoffb[...] = jnp.dot(cbf, hpb, preferred_element_type=F32)
                for r in range(8):
                    h = 8 * g + r
                    sl = slice(HD * h, HD * h + HD)
                    rs = slice(HD * r, HD * r + HD)
                    cs_h = csb[:, h:h + 1]
                    cl = csb[SCH - 1:SCH, h:h + 1]
                    dyh = dy_ref[rows, sl]
                    ecs = jnp.exp(cs_h)
                    ebuf[:, rs] = dyh * ecs
                    xdecb[:, rs] = ubuf[rows, sl] * dtbuf[rows, h:h + 1] * jnp.exp(cl - cs_h)
                    t1 = jnp.sum(dyh * ecs * yoffb[:, rs], axis=1, keepdims=True)
                    dcs = dcs + jnp.where(lane == h, t1, 0.0)
                    drow[:, rs] = jnp.broadcast_to(jnp.exp(cl), (1, HD))
                eb = ebuf[...].astype(BF)
                dc = lax.dot_general(eb, hpb, (((1,), (1,)), ((), ())), preferred_element_type=F32)
                dprev = jnp.dot(cg.T.astype(BF), eb, preferred_element_type=F32)
                wbuf[...] = jnp.dot(bb, dhb, preferred_element_type=F32)
                db = lax.dot_general(xdecb[...].astype(BF), dhb, (((1,), (1,)), ((), ())), preferred_element_type=F32)
                dcbs = jnp.zeros((SCH, SCH), F32)
                dcbts = jnp.zeros((SCH, SCH), F32)
                for r in range(8):
                    h = 8 * g + r
                    sl = slice(HD * h, HD * h + HD)
                    rs = slice(HD * r, HD * r + HD)
                    cs_h = csb[:, h:h + 1]
                    cst_h = cstb[h:h + 1, :]
                    cl = csb[SCH - 1:SCH, h:h + 1]
                    seg = jnp.exp(jnp.where(causal, cs_h - cst_h, -jnp.inf))
                    segt = jnp.exp(jnp.where(upper, cst_h - cs_h, -jnp.inf))
                    dyh = dy_ref[rows, sl]
                    xh = ubuf[rows, sl]
                    dt_h = dtbuf[rows, h:h + 1]
                    xdt = xh * dt_h
                    dyb = dyh.astype(BF)
                    xdb = xdt.astype(BF)
                    dm = lax.dot_general(dyb, xdb, (((1,), (1,)), ((), ())), preferred_element_type=F32)
                    dmt = lax.dot_general(xdb, dyb, (((1,), (1,)), ((), ())), preferred_element_type=F32)
                    mt = cbt * segt
                    dxdt = jnp.dot(mt.astype(BF), dyb, preferred_element_type=F32)
                    dcs_h = (jnp.sum(dm * (cb * seg), axis=1, keepdims=True)
                             - jnp.sum(dmt * mt, axis=1, keepdims=True))
                    dcbs = dcbs + dm * seg
                    dcbts = dcbts + dmt * segt
                    dec = jnp.exp(cl - cs_h)
                    w_h = wbuf[:, rs]
                    dxdt = dxdt + dec * w_h
                    t = jnp.sum(w_h * xdt, axis=1, keepdims=True) * dec
                    dcs_h = dcs_h - t
                    dch = jnp.sum(jnp.sum(dhs[g, :, rs] * prev_ref[c, g, :, rs], axis=1, keepdims=True),
                                  axis=0, keepdims=True)
                    dl = jnp.sum(t, axis=0, keepdims=True) + dch * jnp.exp(cl)
                    dcs = dcs + jnp.where(lane == h, dcs_h, 0.0)
                    dlast = dlast + jnp.where(lane1 == h, dl, 0.0)
                    xr = xr + jnp.where(lane == h, jnp.sum(dxdt * xh, axis=1, keepdims=True), 0.0)
                    dskh = jnp.sum(jnp.sum(dyh * xh, axis=1, keepdims=True), axis=0, keepdims=True)
                    dsk = dsk + jnp.where(lane1 == h, dskh, 0.0)
                    dub[:, sl] = dyh * dsk_ref[:, h:h + 1] + dxdt * dt_h
                dc = dc + jnp.dot(dcbs.astype(BF), bb, preferred_element_type=F32)
                db = db + jnp.dot(dcbts.astype(BF), cbf, preferred_element_type=F32)
                dub[:, SW + NS * g:SW + NS * g + NS] = db
                dub[:, SW + NG * NS + NS * g:SW + NG * NS + NS * g + NS] = dc
                dhs[g] = dprev + drow[...] * dhg
            dcs = dcs + jnp.where(lastrow, dlast, 0.0)
            dadt = jnp.dot(trit, dcs, precision=HI, preferred_element_type=F32)
            ddt = dadt * a + xr
            ddtr = ddt * jax.nn.sigmoid(dt_ref[rows, :] + dtb_ref[...])
            ddt_ref[rows, :] = ddtr.astype(BF)
            ph_ref[0:1, :] += jnp.sum(ddtr, axis=0, keepdims=True)
            ph_ref[1:2, :] += jnp.sum(dadt * dtc, axis=0, keepdims=True) * a
            ph_ref[2:3, :] += dsk
            pc = pre_ref[rows, :]
            sg = jax.nn.sigmoid(pc)
            dpad[rows, :] = dub[...] * _silu_grad(pc, sg)
            return carry

        lax.fori_loop(0, NCB, chunk, 0)

        dxb = cw_ref[0:1, :] * dpad[3:3 + TB, :]
        for kk in range(1, 4):
            dxb = dxb + cw_ref[kk:kk + 1, :] * dpad[3 - kk:3 - kk + TB, :]
        dx_ref[...] = dxb.astype(BF)
        keep = jnp.where(i < nb - 1, 1.0, 0.0)
        xpad[0:8, 0:SW] = xsh_ref[...] * keep
        xpad[0:8, SW:XBC] = bch_ref[...] * keep
        xpad[8:8 + TB, 0:SW] = xs_ref[...]
        xpad[8:8 + TB, SW:XBC] = bc_ref[...]
        dp = dpad[0:TB, :]
        for kk in range(4):
            pw_ref[kk:kk + 1, :] += jnp.sum(dp * xpad[5 + kk:5 + kk + TB, :], axis=0, keepdims=True)
        pw_ref[4:5, :] += jnp.sum(dp, axis=0, keepdims=True)
        dpad[TB:TB + 8, :] = dpad[0:8, :]

    rblk = lambda w, col: pl.BlockSpec((TB, w), lambda i, _c=col: (rev(i), _c))
    return pl.pallas_call(
        body, name="ssd_bwd", grid=(nb,),
        in_specs=[rblk(SW, 0), rblk(XBC, 0), rblk(LANE, 44),
                  pl.BlockSpec((NCB, NG, NS, GW), lambda i: (rev(i), 0, 0, 0)),
                  rblk(SW, 4), rblk(512, 10),
                  pl.BlockSpec((8, SW), lambda i: (halo(i), 4)), pl.BlockSpec((8, 512), lambda i: (halo(i), 10)),
                  _fix((4, XBC)), _fix((1, LANE)), _fix((1, LANE)), _fix((1, LANE))],
        out_specs=[rblk(XBC, 0), rblk(LANE, 0), _fix((8, XBC)), _fix((8, LANE))],
        out_shape=[jax.ShapeDtypeStruct((s, XBC), BF), jax.ShapeDtypeStruct((s, LANE), BF),
                   jax.ShapeDtypeStruct((8, XBC), F32), jax.ShapeDtypeStruct((8, LANE), F32)],
        scratch_shapes=[pltpu.VMEM((TB, XBC), F32), pltpu.VMEM((TB, LANE), F32), pltpu.VMEM((TB + 8, XBC), F32),
                        pltpu.VMEM((TB + 8, XBC), F32), pltpu.VMEM((SCH, LANE), F32), pltpu.VMEM((LANE, SCH), F32),
                        pltpu.VMEM((NG, NS, GW), F32), pltpu.VMEM((SCH, GW), F32), pltpu.VMEM((SCH, GW), F32),
                        pltpu.VMEM((SCH, GW), F32), pltpu.VMEM((SCH, GW), F32), pltpu.VMEM((1, GW), F32),
                        pltpu.VMEM((SCH, XBC), F32)],
        compiler_params=_cp("arbitrary"))(dy, pre, proj, prev, proj, proj, proj, proj, conv_w, dt_bias, a_log, d_skip)


def _rel_index():
    q = jnp.arange(CH)[:, None] + (BAND - CH)
    k = jnp.arange(BAND)[None, :]
    return jnp.clip(q - k, -(CH - 1), REL_CLIP) + (CH - 1)


def _local_step(x, tgt, mods, w_in_p, w_in_t, w_out, w_out_t, w_gu, w_gu_t, w_down, w_down_t,
                g_mix, rel_bias, conv_w, conv_b, dt_bias, a_log, d_skip, g_att, g_ssd, g_ffn, g_final):
    sh1, sc1, gt1, sh2, sc2, gt2 = mods
    pad16 = lambda v: jnp.pad(v, ((0, 0), (0, LANE - NH)))
    dtb_p, al_p, dsk_p = pad16(dt_bias), pad16(a_log), pad16(d_skip)
    ridx = _rel_index()
    bias = rel_bias[:, ridx]
    bias_w = _bias_window(bias)
    bias_wt = jnp.swapaxes(bias_w, 1, 2)

    h1 = _norm_fwd(x, g_mix, sc1, sh1, name="norm1_fwd")
    proj = _mm(h1, w_in_p, F32, "in_proj", 512, 1152, 2048)
    qkv = proj[:, :3 * AW].astype(BF)
    att = _attn2_fwd(qkv, bias_w)
    y, pre, prev = _ssd_fwd(proj, conv_w, conv_b, dtb_p, al_p, dsk_p)
    cat = _mixout_fwd(att, y, proj, g_att, g_ssd)
    mix = _mm(cat, w_out, F32, "out_proj", 512, 1024, 2048)
    h2, x1 = _norm_fwd(x, g_ffn, sc2, sh2, mix, gt1, name="norm2_fwd")
    gu = _mm(h2, w_gu, BF, "gate_up", 512, 1024, 2048)
    act = _swiglu_fwd(gu)
    ffn = _mm(act, w_down, F32, "down_proj", 512, 1024, 1408)
    dx2, dffn, pf = _final(x1, ffn, gt2, g_final, tgt)

    dact = _mm(dffn, w_down_t, BF, "d_act", 512, 1408, 2048)
    gw_down = _mm(act.T, dffn, F32, "gw_down", 704, 1024, 1024)
    dgu = _swiglu_bwd(gu, dact)
    dh2 = _mm(dgu, w_gu_t, F32, "d_h2", 512, 1024, 1408)
    gw_gu = _mm(h2.T, dgu, F32, "gw_gate_up", 512, 1024, 1024)
    dx1, dmix, p2 = _norm_bwd(x1, g_ffn, sc2, sh2, dh2, dx2, mix, gt1, name="norm2_bwd")
    dcat = _mm(dmix, w_out_t, F32, "d_cat", 512, 1024, 2048)
    gw_out = _mm(cat.T, dmix, F32, "gw_out", 512, 1024, 1024)
    datt, dy, dz, pm = _mixout_bwd(dcat, att, y, proj, g_att, g_ssd)
    dq, dk, dv, dbias_t = _attn2_bwd(qkv, datt, bias_wt)
    dxbc, ddt, pw, ph = _ssd_bwd(dy, pre, proj, prev, conv_w, dtb_p, al_p, dsk_p)
    dproj = jnp.concatenate([dq, dk, dv, dz, dxbc, ddt], axis=1)
    dh1 = _mm(dproj, w_in_t, F32, "d_h1", 512, 1024, 1152)
    gw_in = _mm(h1.T, dproj, F32, "gw_in", 512, 1152, 1024)
    gx, p1 = _norm_bwd(x, g_mix, sc1, sh1, dh1, dx1, name="norm1_bwd")

    onehot = (ridx.T.reshape(BAND * CH, 1) == jnp.arange(NREL)[None, :]).astype(F32)
    g_rel = _mm(dbias_t.reshape(NH, BAND * CH), onehot, F32, "g_rel_bias", NH, NREL, 4608, precision=HI)

    small = dict(
        dmods=jnp.concatenate([p1[1], p1[0], p2[3], p2[1], p2[0], pf[1]]),
        g_mix=p1[2], conv_b=pw[4], dt_bias=ph[0], a_log=ph[1], d_skip=ph[2], g_att=pm[0], g_ssd=pm[1],
        g_ffn=p2[2], g_final=pf[0], rel_bias=g_rel.reshape(-1), conv_w=pw[0:4].reshape(-1), loss=pf[2])
    return gx, gw_in, gw_out, gw_gu, gw_down, small


def _exchange(srcs, gather, name):
    n = len(srcs)
    outs = [jax.ShapeDtypeStruct((NDEV,) + (s.shape if g else s.shape[1:]), s.dtype) for s, g in zip(srcs, gather)]

    def body(*refs):
        src, dst = refs[:n], refs[n:2 * n]
        ssem, rsem, lsem = refs[2 * n:]
        x, y, c = lax.axis_index("x"), lax.axis_index("y"), lax.axis_index("c")
        me = 4 * x + 2 * y + c
        local = []
        for a in range(n):
            own = src[a] if gather[a] else src[a].at[me]
            cp = pltpu.make_async_copy(own, dst[a].at[me], lsem.at[a])
            cp.start()
            local.append(cp)
        remote = []
        for k in range(1, NDEV):
            px = 1 - x if k & 4 else x
            py = 1 - y if k & 2 else y
            pc = 1 - c if k & 1 else c
            pid = 4 * px + 2 * py + pc
            for a in range(n):
                s_ref = src[a] if gather[a] else src[a].at[pid]
                cp = pltpu.make_async_remote_copy(
                    src_ref=s_ref, dst_ref=dst[a].at[me], send_sem=ssem.at[a * (NDEV - 1) + k - 1],
                    recv_sem=rsem.at[a * (NDEV - 1) + k - 1], device_id=(px, py, pc), device_id_type=MESH)
                cp.start()
                remote.append(cp)
        for cp in remote:
            cp.wait()
        for cp in local:
            cp.wait()

    anyspec = pl.BlockSpec(memory_space=pl.ANY)
    return pl.pallas_call(
        body, name=name, out_shape=outs, in_specs=[anyspec] * n, out_specs=[anyspec] * n,
        scratch_shapes=[pltpu.SemaphoreType.DMA((n * (NDEV - 1),)), pltpu.SemaphoreType.DMA((n * (NDEV - 1),)),
                        pltpu.SemaphoreType.DMA((n,))],
    )(*srcs)


def _mods(c16, w_shard, b_shard, *, tn=512):
    n = w_shard.shape[1]

    def body(c_ref, w_ref, b_ref, o_ref):
        cv = c_ref[...]
        cond = (cv * jax.nn.sigmoid(cv)).astype(BF)
        o_ref[...] = jnp.dot(cond, w_ref[...].astype(BF), preferred_element_type=F32) + b_ref[...]

    return pl.pallas_call(body, name="ada_mods", grid=(n // tn,),
                          in_specs=[_fix((16, D)), pl.BlockSpec((D, tn), lambda j: (0, j)),
                                    pl.BlockSpec((1, tn), lambda j: (0, j))],
                          out_specs=pl.BlockSpec((16, tn), lambda j: (0, j)),
                          out_shape=jax.ShapeDtypeStruct((16, n), F32), compiler_params=_cp("parallel"))(c16, w_shard, b_shard)


def _gw_ada(c_t, dm, *, tr=256):
    n = dm.shape[1]

    def body(ct_ref, dm_ref, o_ref):
        acc = jnp.zeros((tr, n), F32)
        for b in range(NDEV):
            cv = ct_ref[:, b:b + 1]
            cond = (cv * jax.nn.sigmoid(cv)).astype(BF).astype(F32)
            acc = acc + cond * dm_ref[b:b + 1, :].astype(BF).astype(F32)
        o_ref[...] = acc

    return pl.pallas_call(body, name="gw_ada", grid=(D // tr,),
                          in_specs=[_row(tr, NDEV), _fix((NDEV, n))], out_specs=_row(tr, n),
                          out_shape=jax.ShapeDtypeStruct((D, n), F32), compiler_params=_cp("parallel"))(c_t, dm)


def _sum8(parts, *, name):
    _, r, c = parts.shape

    def body(p_ref, o_ref):
        acc = p_ref[0].astype(F32)
        for d in range(1, NDEV):
            acc = acc + p_ref[d].astype(F32)
        o_ref[...] = acc

    return pl.pallas_call(body, name=name, grid=(1,), in_specs=[_fix((NDEV, r, c))], out_specs=_fix((r, c)),
                          out_shape=jax.ShapeDtypeStruct((r, c), F32), compiler_params=_cp("arbitrary"))(parts)


def _adam(w, m, v, *, recv=None, grad=None, tr, name):
    r, c = w.shape
    summed = recv is not None
    assert r % tr == 0

    def body(g_ref, w_ref, m_ref, v_ref, go_ref, do_ref, mo_ref, vo_ref):
        if summed:
            g = g_ref[0].astype(F32)
            for d in range(1, NDEV):
                g = g + g_ref[d].astype(F32)
        else:
            g = g_ref[...]
        go_ref[...] = g
        mn = B1 * m_ref[...] + (1.0 - B1) * g
        vn = B2 * v_ref[...] + (1.0 - B2) * (g * g)
        mo_ref[...] = mn
        vo_ref[...] = vn
        mh = mn / (1.0 - B1 ** STEP)
        vh = vn / (1.0 - B2 ** STEP)
        do_ref[...] = -LR * (mh / (jnp.sqrt(vh) + AEPS) + WD * w_ref[...])

    gspec = pl.BlockSpec((NDEV, tr, c), lambda i: (0, i, 0)) if summed else _row(tr, c)
    return pl.pallas_call(body, name=name, grid=(r // tr,),
                          in_specs=[gspec, _row(tr, c), _row(tr, c), _row(tr, c)], out_specs=[_row(tr, c)] * 4,
                          out_shape=[jax.ShapeDtypeStruct((r, c), F32)] * 4,
                          compiler_params=_cp("parallel"))(recv if summed else grad, w, m, v)


def _cols_to_blocks(g, width):
    r = g.shape[0]
    return jnp.transpose(g.reshape(r, NDEV, width), (1, 0, 2))


def _blocks_to_cols(b):
    _, r, width = b.shape
    return jnp.transpose(b, (1, 0, 2)).reshape(r, NDEV * width)


_SMALL = ("b_ada", "g_mix", "rel_bias", "conv_w", "conv_b", "dt_bias", "a_log", "d_skip",
          "g_att_out", "g_ssd_out", "g_ffn", "g_final")


def _pad_rows(v):
    v = v.reshape(-1)
    return jnp.pad(v, (0, (-v.shape[0]) % LANE))


def kernel(x, c, w_ada, b_ada, g_mix, w_in, rel_bias, conv_w, conv_b, dt_bias, a_log, d_skip, g_att_out, g_ssd_out, w_out, g_ffn, w_gate, w_up, w_down, g_final, loss_target, m_w_ada, m_b_ada, m_g_mix, m_w_in, m_rel_bias, m_conv_w, m_conv_b, m_dt_bias, m_a_log, m_d_skip, m_g_att_out, m_g_ssd_out, m_w_out, m_g_ffn, m_w_gate, m_w_up, m_w_down, m_g_final, v_w_ada, v_b_ada, v_g_mix, v_w_in, v_rel_bias, v_conv_w, v_conv_b, v_dt_bias, v_a_log, v_d_skip, v_g_att_out, v_g_ssd_out, v_w_out, v_g_ffn, v_w_gate, v_w_up, v_w_down, v_g_final):
    me = 4 * lax.axis_index("x") + 2 * lax.axis_index("y") + lax.axis_index("c")
    wts = dict(w_ada=w_ada, b_ada=b_ada, g_mix=g_mix, w_in=w_in, rel_bias=rel_bias, conv_w=conv_w, conv_b=conv_b,
               dt_bias=dt_bias, a_log=a_log, d_skip=d_skip, g_att_out=g_att_out, g_ssd_out=g_ssd_out, w_out=w_out,
               g_ffn=g_ffn, w_gate=w_gate, w_up=w_up, w_down=w_down, g_final=g_final)
    mom = dict(w_ada=m_w_ada, b_ada=m_b_ada, g_mix=m_g_mix, w_in=m_w_in, rel_bias=m_rel_bias, conv_w=m_conv_w,
               conv_b=m_conv_b, dt_bias=m_dt_bias, a_log=m_a_log, d_skip=m_d_skip, g_att_out=m_g_att_out,
               g_ssd_out=m_g_ssd_out, w_out=m_w_out, g_ffn=m_g_ffn, w_gate=m_w_gate, w_up=m_w_up, w_down=m_w_down,
               g_final=m_g_final)
    var = dict(w_ada=v_w_ada, b_ada=v_b_ada, g_mix=v_g_mix, w_in=v_w_in, rel_bias=v_rel_bias, conv_w=v_conv_w,
               conv_b=v_conv_b, dt_bias=v_dt_bias, a_log=v_a_log, d_skip=v_d_skip, g_att_out=v_g_att_out,
               g_ssd_out=v_g_ssd_out, w_out=v_w_out, g_ffn=v_g_ffn, w_gate=v_w_gate, w_up=v_w_up, w_down=v_w_down,
               g_final=v_g_final)
    order = ("w_ada", "b_ada", "g_mix", "w_in", "rel_bias", "conv_w", "conv_b", "dt_bias", "a_log", "d_skip",
             "g_att_out", "g_ssd_out", "w_out", "g_ffn", "w_gate", "w_up", "w_down", "g_final")

    c_all, gi, gg, gu_, gd, go = _exchange(
        [c, w_in[0].astype(BF), w_gate[0].astype(BF), w_up[0].astype(BF), w_down[0].astype(BF), w_out[0].astype(BF)],
        [True] * 6, "gather_weights")
    c_all = c_all.reshape(NDEV, D)
    w_in_p = jnp.pad(_blocks_to_cols(gi), ((0, 0), (0, INP - INC)))
    w_gu = jnp.concatenate([_blocks_to_cols(gg), _blocks_to_cols(gu_)], axis=1)
    w_dn = gd.reshape(FF, D)
    w_o = go.reshape(D, D)

    ncol = w_ada.shape[2]
    b_sh = lax.dynamic_slice(b_ada, (0, me * ncol), (1, ncol))
    mods_part = _mods(jnp.pad(c_all, ((0, 8), (0, 0))), w_ada[0], b_sh)[:NDEV]
    (mods_all,) = _exchange([mods_part], [True], "gather_mods")
    mods_me = lax.dynamic_index_in_dim(mods_all, me, axis=1, keepdims=False).reshape(6, 1, D)
    mods = [mods_me[i] for i in range(6)]

    rel_all, cw_all = _exchange([rel_bias[0], conv_w[0]], [True, True], "gather_small_weights")
    rel_full = _blocks_to_cols(rel_all)
    cw_full = _blocks_to_cols(cw_all)

    gx, gw_in, gw_out, gw_gu, gw_down, small = _local_step(
        x[0], loss_target[0], mods, w_in_p, w_in_p.T, w_o, w_o.T, w_gu, w_gu.T, w_dn, w_dn.T,
        g_mix, rel_full, cw_full, conv_b, dt_bias, a_log, d_skip, g_att_out, g_ssd_out, g_ffn,
        g_final.reshape(1, D))

    loss_row = jnp.pad(jnp.sum(small["loss"]).reshape(1), (0, LANE - 1))
    pack = jnp.concatenate([small[k] for k in ("dmods", "g_mix", "conv_b", "dt_bias", "a_log", "d_skip", "g_att",
                                               "g_ssd", "g_ffn", "g_final", "rel_bias", "conv_w")] + [loss_row])
    pack = pack.reshape(-1, LANE)
    packs, ri, rg, ru, rd, ro = _exchange(
        [pack,
         _cols_to_blocks(gw_in[:, :INC], INC // NDEV).astype(BF),
         _cols_to_blocks(gw_gu[:, :FF], FF // NDEV).astype(BF),
         _cols_to_blocks(gw_gu[:, FF:], FF // NDEV).astype(BF),
         gw_down.reshape(NDEV, FF // NDEV, D).astype(BF),
         gw_out.reshape(NDEV, D // NDEV, D).astype(BF)],
        [True, False, False, False, False, False], "exchange_grads")
    tot = _sum8(packs, name="sum_small").reshape(-1)

    sizes = dict(dmods=6 * D, g_mix=D, conv_b=XBC, dt_bias=LANE, a_log=LANE, d_skip=LANE, g_att_out=AW, g_ssd_out=SW,
                 g_ffn=D, g_final=D, rel_bias=NH * NREL, conv_w=4 * XBC, loss=LANE)
    off, o = {}, 0
    for k_, n_ in sizes.items():
        off[k_] = o
        o += n_
    take = lambda k_, n_=None: tot[off[k_]:off[k_] + (n_ or sizes[k_])]
    loss = tot[off["loss"]]
    rel_w = NREL // NDEV
    cw_w = XBC // NDEV
    gsm = dict(
        b_ada=take("dmods"), g_mix=take("g_mix"),
        rel_bias=lax.dynamic_slice(take("rel_bias").reshape(NH, NREL), (0, me * rel_w), (NH, rel_w)),
        conv_w=lax.dynamic_slice(take("conv_w").reshape(4, XBC), (0, me * cw_w), (4, cw_w)),
        conv_b=take("conv_b"), dt_bias=take("dt_bias", NH), a_log=take("a_log", NH), d_skip=take("d_skip", NH),
        g_att_out=take("g_att_out"), g_ssd_out=take("g_ssd_out"), g_ffn=take("g_ffn"), g_final=take("g_final"))

    flat = lambda d_: jnp.concatenate([_pad_rows(d_[k_]) for k_ in _SMALL])
    gflat = flat(gsm)
    nrow = -(-gflat.shape[0] // (8 * LANE)) * 8
    to2d = lambda v_: jnp.pad(v_, (0, nrow * LANE - v_.shape[0])).reshape(nrow, LANE)
    res = _adam(to2d(flat(wts)), to2d(flat(mom)), to2d(flat(var)), grad=to2d(gflat), tr=nrow, name="adam_small")
    outs = {k_: {} for k_ in ("grad", "delta", "m", "v")}
    o = 0
    for k_ in _SMALL:
        n_ = wts[k_].size
        for kind, arr in zip(("grad", "delta", "m", "v"), res):
            outs[kind][k_] = arr.reshape(-1)[o:o + n_].reshape(wts[k_].shape)
        o += n_ + (-n_) % LANE

    dm_all = packs[:, :6 * D // LANE, :].reshape(NDEV, 6 * D)
    dm_sh = lax.dynamic_slice(dm_all, (0, me * ncol), (NDEV, ncol))
    gwa = _gw_ada(c_all.T, dm_sh)
    big = dict(
        w_ada=_adam(w_ada[0], m_w_ada[0], v_w_ada[0], grad=gwa, tr=256, name="adam_w_ada"),
        w_in=_adam(w_in[0], m_w_in[0], v_w_in[0], recv=ri, tr=256, name="adam_w_in"),
        w_gate=_adam(w_gate[0], m_w_gate[0], v_w_gate[0], recv=rg, tr=256, name="adam_w_gate"),
        w_up=_adam(w_up[0], m_w_up[0], v_w_up[0], recv=ru, tr=256, name="adam_w_up"),
        w_down=_adam(w_down[0], m_w_down[0], v_w_down[0], recv=rd, tr=176, name="adam_w_down"),
        w_out=_adam(w_out[0], m_w_out[0], v_w_out[0], recv=ro, tr=128, name="adam_w_out"))
    for k_, r_ in big.items():
        for kind, arr in zip(("grad", "delta", "m", "v"), r_):
            outs[kind][k_] = arr.reshape(wts[k_].shape)

    return (loss, gx.reshape(x.shape), *[outs["grad"][k_] for k_ in order], *[outs["delta"][k_] for k_ in order],
            *[outs["m"][k_] for k_ in order], *[outs["v"][k_] for k_ in order])
```

```python
import functools

import jax
import jax.numpy as jnp
from jax import lax
from jax.experimental import pallas as pl
from jax.experimental.pallas import tpu as pltpu

F32 = jnp.float32
BF = jnp.bfloat16
HI = lax.Precision.HIGHEST

D = 2048
NH = 16
HD = 64
AW = 1024
SW = 1024
NG = 2
NS = 128
XBC = 1536
CH = 64
BAND = 576
NREL = 320
REL_CLIP = 256
FF = 5632
INC = 5648
INP = 5760
EPS = 1e-6
NDEV = 8
LANE = 128
VMEM_LIMIT = 56 * 1024 * 1024
LR, B1, B2, AEPS, WD, STEP = 0.001, 0.9, 0.999, 1e-08, 0.01, 10
MESH = pl.DeviceIdType.MESH


def _cp(*sem):
    return pltpu.CompilerParams(dimension_semantics=sem, vmem_limit_bytes=VMEM_LIMIT)


def _row(ts, w, col=0):
    return pl.BlockSpec((ts, w), lambda i, _c=col: (i, _c))


def _fix(shape, single=False):
    nd = len(shape)
    if single:
        return pl.BlockSpec(shape, lambda i, _n=nd: (0,) * _n, pipeline_mode=pl.Buffered(1))
    return pl.BlockSpec(shape, lambda i, _n=nd: (0,) * _n)


def _silu_grad(x, s):
    return s * (1.0 + x * (1.0 - s))


def _softplus(x):
    return jnp.maximum(x, 0.0) + jnp.log1p(jnp.exp(-jnp.abs(x)))


def _mm(a, b, out_dtype, name, tm, tn, tk, precision=None):
    m, k = a.shape
    _, n = b.shape
    nk = k // tk
    assert m % tm == 0 and n % tn == 0 and k % tk == 0, (name, a.shape, b.shape)

    def body(a_ref, b_ref, o_ref, acc):
        kk = pl.program_id(2)

        @pl.when(kk == 0)
        def _():
            acc[...] = jnp.zeros_like(acc)

        acc[...] += jnp.dot(a_ref[...], b_ref[...], preferred_element_type=F32, precision=precision)

        @pl.when(kk == nk - 1)
        def _():
            o_ref[...] = acc[...].astype(o_ref.dtype)

    return pl.pallas_call(
        body, name=name, grid=(m // tm, n // tn, nk),
        in_specs=[pl.BlockSpec((tm, tk), lambda i, j, kk: (i, kk)),
                  pl.BlockSpec((tk, tn), lambda i, j, kk: (kk, j))],
        out_specs=pl.BlockSpec((tm, tn), lambda i, j, kk: (i, j)),
        out_shape=jax.ShapeDtypeStruct((m, n), out_dtype),
        scratch_shapes=[pltpu.VMEM((tm, tn), F32)],
        compiler_params=_cp("parallel", "parallel", "arbitrary"),
    )(a, b)


def _norm_fwd(x, g, sc, sh, mix=None, gt=None, *, name, ts=256):
    s, d = x.shape
    has = mix is not None

    def body(*refs):
        if has:
            x_ref, g_ref, sc_ref, sh_ref, mix_ref, gt_ref, h_ref, x1_ref = refs
        else:
            x_ref, g_ref, sc_ref, sh_ref, h_ref = refs
        xv = x_ref[...]
        if has:
            xv = xv + gt_ref[...] * mix_ref[...]
            x1_ref[...] = xv
        r = lax.rsqrt(jnp.mean(xv * xv, axis=-1, keepdims=True) + EPS)
        h_ref[...] = ((xv * r) * g_ref[...] * (1.0 + sc_ref[...]) + sh_ref[...]).astype(BF)

    ins = [x, g, sc, sh] + ([mix, gt] if has else [])
    in_specs = [_row(ts, d), _fix((1, d)), _fix((1, d)), _fix((1, d))] + ([_row(ts, d), _fix((1, d))] if has else [])
    out_shape = [jax.ShapeDtypeStruct((s, d), BF)] + ([jax.ShapeDtypeStruct((s, d), F32)] if has else [])
    out_specs = [_row(ts, d)] + ([_row(ts, d)] if has else [])
    out = pl.pallas_call(body, name=name, grid=(s // ts,), in_specs=in_specs, out_specs=out_specs,
                         out_shape=out_shape, compiler_params=_cp("parallel"))(*ins)
    return out if has else out[0]


def _norm_bwd(x, g, sc, sh, dh, dres, mix=None, gt=None, *, name, ts=256):
    s, d = x.shape
    has = mix is not None

    def body(*refs):
        if has:
            x_ref, g_ref, sc_ref, sh_ref, dh_ref, dres_ref, mix_ref, gt_ref, dx_ref, dmix_ref, part_ref = refs
        else:
            x_ref, g_ref, sc_ref, sh_ref, dh_ref, dres_ref, dx_ref, part_ref = refs
        i = pl.program_id(0)
        xv = x_ref[...]
        gv = g_ref[...]
        dhv = dh_ref[...]
        r = lax.rsqrt(jnp.mean(xv * xv, axis=-1, keepdims=True) + EPS)
        xh = xv * r
        dyn = dhv * (1.0 + sc_ref[...])
        gy = dyn * gv
        dx = dres_ref[...] + r * (gy - xh * jnp.mean(gy * xh, axis=-1, keepdims=True))
        dx_ref[...] = dx

        @pl.when(i == 0)
        def _():
            part_ref[...] = jnp.zeros_like(part_ref)

        part_ref[0:1, :] += jnp.sum(dhv * (xh * gv), axis=0, keepdims=True)
        part_ref[1:2, :] += jnp.sum(dhv, axis=0, keepdims=True)
        part_ref[2:3, :] += jnp.sum(dyn * xh, axis=0, keepdims=True)
        if has:
            part_ref[3:4, :] += jnp.sum(dx * mix_ref[...], axis=0, keepdims=True)
            dmix_ref[...] = (gt_ref[...] * dx).astype(BF)

    ins = [x, g, sc, sh, dh, dres] + ([mix, gt] if has else [])
    in_specs = [_row(ts, d), _fix((1, d)), _fix((1, d)), _fix((1, d)), _row(ts, d), _row(ts, d)]
    in_specs += [_row(ts, d), _fix((1, d))] if has else []
    out_shape = [jax.ShapeDtypeStruct((s, d), F32)] + ([jax.ShapeDtypeStruct((s, d), BF)] if has else [])
    out_shape += [jax.ShapeDtypeStruct((8, d), F32)]
    out_specs = [_row(ts, d)] + ([_row(ts, d)] if has else []) + [_fix((8, d))]
    return pl.pallas_call(body, name=name, grid=(s // ts,), in_specs=in_specs, out_specs=out_specs,
                          out_shape=out_shape, compiler_params=_cp("arbitrary"))(*ins)


def _final(x1, ffn, gt2, gf, tgt, *, ts=256):
    s, d = x1.shape

    def body(x1_ref, ffn_ref, gt_ref, g_ref, t_ref, dx_ref, dffn_ref, part_ref):
        i = pl.program_id(0)
        fv = ffn_ref[...]
        gv = g_ref[...]
        xv = x1_ref[...] + gt_ref[...] * fv
        r = lax.rsqrt(jnp.mean(xv * xv, axis=-1, keepdims=True) + EPS)
        xh = xv * r
        e = xh * gv - t_ref[...]
        dy = e * (1.0 / d)
        gy = dy * gv
        dx = r * (gy - xh * jnp.mean(gy * xh, axis=-1, keepdims=True))
        dx_ref[...] = dx
        dffn_ref[...] = (gt_ref[...] * dx).astype(BF)

        @pl.when(i == 0)
        def _():
            part_ref[...] = jnp.zeros_like(part_ref)

        part_ref[0:1, :] += jnp.sum(dy * xh, axis=0, keepdims=True)
        part_ref[1:2, :] += jnp.sum(dx * fv, axis=0, keepdims=True)
        part_ref[2:3, :] += jnp.sum(e * e, axis=0, keepdims=True) * (0.5 / d)

    return pl.pallas_call(
        body, name="final_loss", grid=(s // ts,),
        in_specs=[_row(ts, d), _row(ts, d), _fix((1, d)), _fix((1, d)), _row(ts, d)],
        out_specs=[_row(ts, d), _row(ts, d), _fix((8, d))],
        out_shape=[jax.ShapeDtypeStruct((s, d), F32), jax.ShapeDtypeStruct((s, d), BF),
                   jax.ShapeDtypeStruct((8, d), F32)],
        compiler_params=_cp("arbitrary"))(x1, ffn, gt2, gf, tgt)


def _swiglu_fwd(gu, *, ts=256):
    s = gu.shape[0]

    def body(g_ref, u_ref, o_ref):
        gv = g_ref[...].astype(F32)
        o_ref[...] = (gv * jax.nn.sigmoid(gv) * u_ref[...].astype(F32)).astype(BF)

    return pl.pallas_call(body, name="swiglu_fwd", grid=(s // ts,),
                          in_specs=[_row(ts, FF, 0), _row(ts, FF, 1)], out_specs=_row(ts, FF),
                          out_shape=jax.ShapeDtypeStruct((s, FF), BF), compiler_params=_cp("parallel"))(gu, gu)


def _swiglu_bwd(gu, dact, *, ts=256):
    s = gu.shape[0]

    def body(g_ref, u_ref, da_ref, o_ref):
        gv = g_ref[...].astype(F32)
        uv = u_ref[...].astype(F32)
        dav = da_ref[...].astype(F32)
        sg = jax.nn.sigmoid(gv)
        o_ref[:, 0:FF] = (dav * uv * _silu_grad(gv, sg)).astype(BF)
        o_ref[:, FF:2 * FF] = (dav * gv * sg).astype(BF)

    return pl.pallas_call(body, name="swiglu_bwd", grid=(s // ts,),
                          in_specs=[_row(ts, FF, 0), _row(ts, FF, 1), _row(ts, FF)], out_specs=_row(ts, 2 * FF),
                          out_shape=jax.ShapeDtypeStruct((s, 2 * FF), BF),
                          compiler_params=_cp("parallel"))(gu, gu, dact)


def _mixout_fwd(att, y, proj, g_att, g_ssd, *, ts=256):
    s = att.shape[0]

    def body(a_ref, y_ref, z_ref, ga_ref, gs_ref, o_ref):
        av = a_ref[...]
        ra = lax.rsqrt(jnp.mean(av * av, axis=-1, keepdims=True) + EPS)
        o_ref[:, 0:AW] = (av * ra * ga_ref[...]).astype(BF)
        zv = z_ref[...]
        yz = y_ref[...] * (zv * jax.nn.sigmoid(zv))
        rs = lax.rsqrt(jnp.mean(yz * yz, axis=-1, keepdims=True) + EPS)
        o_ref[:, AW:AW + SW] = (yz * rs * gs_ref[...]).astype(BF)

    return pl.pallas_call(body, name="mixout_fwd", grid=(s // ts,),
                          in_specs=[_row(ts, AW), _row(ts, SW), _row(ts, SW, 3), _fix((1, AW)), _fix((1, SW))],
                          out_specs=_row(ts, AW + SW), out_shape=jax.ShapeDtypeStruct((s, AW + SW), BF),
                          compiler_params=_cp("parallel"))(att, y, proj, g_att, g_ssd)


def _mixout_bwd(dcat, att, y, proj, g_att, g_ssd, *, ts=256):
    s = att.shape[0]

    def body(dc_ref, a_ref, y_ref, z_ref, ga_ref, gs_ref, da_ref, dy_ref, dz_ref, part_ref):
        i = pl.program_id(0)
        av = a_ref[...]
        dca = dc_ref[:, 0:AW]
        ra = lax.rsqrt(jnp.mean(av * av, axis=-1, keepdims=True) + EPS)
        ah = av * ra
        gy = dca * ga_ref[...]
        da_ref[...] = ra * (gy - ah * jnp.mean(gy * ah, axis=-1, keepdims=True))
        zv = z_ref[...]
        yv = y_ref[...]
        sg = jax.nn.sigmoid(zv)
        sz = zv * sg
        yz = yv * sz
        dcs = dc_ref[:, AW:AW + SW]
        rs = lax.rsqrt(jnp.mean(yz * yz, axis=-1, keepdims=True) + EPS)
        yh = yz * rs
        gys = dcs * gs_ref[...]
        dyz = rs * (gys - yh * jnp.mean(gys * yh, axis=-1, keepdims=True))
        dy_ref[...] = dyz * sz
        dz_ref[...] = (dyz * yv * _silu_grad(zv, sg)).astype(BF)

        @pl.when(i == 0)
        def _():
            part_ref[...] = jnp.zeros_like(part_ref)

        part_ref[0:1, :] += jnp.sum(dca * ah, axis=0, keepdims=True)
        part_ref[1:2, :] += jnp.sum(dcs * yh, axis=0, keepdims=True)

    return pl.pallas_call(
        body, name="mixout_bwd", grid=(s // ts,),
        in_specs=[_row(ts, AW + SW), _row(ts, AW), _row(ts, SW), _row(ts, SW, 3), _fix((1, AW)), _fix((1, SW))],
        out_specs=[_row(ts, AW), _row(ts, SW), _row(ts, SW), _fix((8, AW))],
        out_shape=[jax.ShapeDtypeStruct((s, AW), F32), jax.ShapeDtypeStruct((s, SW), F32),
                   jax.ShapeDtypeStruct((s, SW), BF), jax.ShapeDtypeStruct((8, AW), F32)],
        compiler_params=_cp("arbitrary"))(dcat, att, y, proj, g_att, g_ssd)


TQ = 512
SCALE = HD ** -0.5


def _attn_fwd(qkv, bias):
    s = qkv.shape[0]
    nb = s // TQ
    prev = lambda i: jnp.maximum(i - 1, 0)

    def body(q_ref, kp_ref, kc_ref, vp_ref, vc_ref, b_ref, o_ref, kbuf, vbuf):
        i = pl.program_id(0)
        kbuf[0:TQ, :] = kp_ref[...]
        kbuf[TQ:2 * TQ, :] = kc_ref[...]
        vbuf[0:TQ, :] = vp_ref[...]
        vbuf[TQ:2 * TQ, :] = vc_ref[...]

        def chunk(c, carry):
            r0 = pl.multiple_of(c * CH, CH)
            kpos = i * TQ + r0 - TQ + lax.broadcasted_iota(jnp.int32, (1, BAND), 1)
            valid = kpos >= 0
            for h in range(NH):
                sl = slice(HD * h, HD * h + HD)
                q = q_ref[pl.ds(r0, CH), sl]
                k = kbuf[pl.ds(r0, BAND), sl]
                v = vbuf[pl.ds(r0, BAND), sl]
                sc = lax.dot_general(q, k, (((1,), (1,)), ((), ())), preferred_element_type=F32) * SCALE + b_ref[h]
                sc = jnp.where(valid, sc, -jnp.inf)
                e = jnp.exp(sc - jnp.max(sc, axis=-1, keepdims=True))
                p = e / jnp.sum(e, axis=-1, keepdims=True)
                o_ref[pl.ds(r0, CH), sl] = jnp.dot(p.astype(BF), v, preferred_element_type=F32)
            return carry

        lax.fori_loop(0, TQ // CH, chunk, 0)

    blk = lambda col, im: pl.BlockSpec((TQ, AW), lambda i, _c=col, _f=im: (_f(i), _c))
    cur = lambda i: i
    return pl.pallas_call(
        body, name="attn_fwd", grid=(nb,),
        in_specs=[blk(0, cur), blk(1, prev), blk(1, cur), blk(2, prev), blk(2, cur), _fix((NH, CH, BAND))],
        out_specs=_row(TQ, AW), out_shape=jax.ShapeDtypeStruct((s, AW), F32),
        scratch_shapes=[pltpu.VMEM((2 * TQ, AW), BF), pltpu.VMEM((2 * TQ, AW), BF)],
        compiler_params=_cp("parallel"))(qkv, qkv, qkv, qkv, qkv, bias)


def _attn_bwd(qkv, datt, bias_t):
    s = qkv.shape[0]
    nb = s // TQ
    cur = lambda i: jnp.minimum(i, nb - 1)
    prev = lambda i: jnp.maximum(jnp.minimum(i, nb - 1) - 1, 0)
    late = lambda i: jnp.maximum(i - 1, 0)

    def body(q_ref, kp_ref, kc_ref, vp_ref, vc_ref, do_ref, b_ref, dq_ref, dk_ref, dv_ref, db_ref,
             kbuf, vbuf, dkacc, dvacc):
        i = pl.program_id(0)

        @pl.when(i == 0)
        def _():
            dkacc[...] = jnp.zeros_like(dkacc)
            dvacc[...] = jnp.zeros_like(dvacc)
            db_ref[...] = jnp.zeros_like(db_ref)

        @pl.when(i < nb)
        def _():
            kbuf[0:TQ, :] = kp_ref[...]
            kbuf[TQ:2 * TQ, :] = kc_ref[...]
            vbuf[0:TQ, :] = vp_ref[...]
            vbuf[TQ:2 * TQ, :] = vc_ref[...]

            def chunk(c, carry):
                r0 = pl.multiple_of(c * CH, CH)
                kpos = i * TQ + r0 - TQ + lax.broadcasted_iota(jnp.int32, (BAND, 1), 0)
                valid = kpos >= 0
                for h in range(NH):
                    sl = slice(HD * h, HD * h + HD)
                    q = q_ref[pl.ds(r0, CH), sl]
                    k = kbuf[pl.ds(r0, BAND), sl]
                    v = vbuf[pl.ds(r0, BAND), sl]
                    do = do_ref[pl.ds(r0, CH), sl].astype(BF)
                    st = lax.dot_general(k, q, (((1,), (1,)), ((), ())), preferred_element_type=F32) * SCALE + b_ref[h]
                    st = jnp.where(valid, st, -jnp.inf)
                    e = jnp.exp(st - jnp.max(st, axis=0, keepdims=True))
                    pt = e / jnp.sum(e, axis=0, keepdims=True)
                    dvacc[pl.ds(r0, BAND), sl] += jnp.dot(pt.astype(BF), do, preferred_element_type=F32)
                    dpt = lax.dot_general(v, do, (((1,), (1,)), ((), ())), preferred_element_type=F32)
                    dst = pt * (dpt - jnp.sum(dpt * pt, axis=0, keepdims=True))
                    db_ref[h] += dst
                    dsb = dst.astype(BF)
                    dkacc[pl.ds(r0, BAND), sl] += jnp.dot(dsb, q, preferred_element_type=F32) * SCALE
                    dq = lax.dot_general(dsb, k, (((0,), (0,)), ((), ())), preferred_element_type=F32) * SCALE
                    dq_ref[pl.ds(r0, CH), sl] = dq.astype(BF)
                return carry

            lax.fori_loop(0, TQ // CH, chunk, 0)

        dk_ref[...] = dkacc[0:TQ, :].astype(BF)
        dv_ref[...] = dvacc[0:TQ, :].astype(BF)
        dkacc[0:TQ, :] = dkacc[TQ:2 * TQ, :]
        dvacc[0:TQ, :] = dvacc[TQ:2 * TQ, :]
        dkacc[TQ:2 * TQ, :] = jnp.zeros((TQ, AW), F32)
        dvacc[TQ:2 * TQ, :] = jnp.zeros((TQ, AW), F32)

    blk = lambda col, im: pl.BlockSpec((TQ, AW), lambda i, _c=col, _f=im: (_f(i), _c))
    return pl.pallas_call(
        body, name="attn_bwd", grid=(nb + 1,),
        in_specs=[blk(0, cur), blk(1, prev), blk(1, cur), blk(2, prev), blk(2, cur), blk(0, cur),
                  _fix((NH, BAND, CH))],
        out_specs=[blk(0, cur), blk(0, late), blk(0, late), _fix((NH, BAND, CH))],
        out_shape=[jax.ShapeDtypeStruct((s, AW), BF)] * 3 + [jax.ShapeDtypeStruct((NH, BAND, CH), F32)],
        scratch_shapes=[pltpu.VMEM((2 * TQ, AW), BF), pltpu.VMEM((2 * TQ, AW), BF),
                        pltpu.VMEM((2 * TQ, AW), F32), pltpu.VMEM((2 * TQ, AW), F32)],
        compiler_params=_cp("arbitrary"))(qkv, qkv, qkv, qkv, qkv, datt, bias_t)


QB = 256
NQC = QB // CH
WIN = 3 * QB


def _bias_window(bias):
    rows = [jnp.pad(bias, ((0, 0), (0, 0), (CH * a, WIN - BAND - CH * a)), constant_values=-jnp.inf)
            for a in range(NQC)]
    return jnp.concatenate(rows, axis=1)


def _win_specs(width, nb):
    def mk(col, back):
        return pl.BlockSpec((QB, width), lambda i, _c=col, _k=back: (jnp.maximum(jnp.minimum(i, nb - 1) - _k, 0), _c))
    return [mk(1, 2), mk(1, 1), mk(1, 0), mk(2, 2), mk(2, 1), mk(2, 0)]


def _attn2_fwd(qkv, bias_w):
    s = qkv.shape[0]
    nb = s // QB

    def body(q_ref, k0, k1, k2, v0, v1, v2, b_ref, o_ref, kbuf, vbuf):
        i = pl.program_id(0)
        for j, (kr, vr) in enumerate(((k0, v0), (k1, v1), (k2, v2))):
            kbuf[QB * j:QB * j + QB, :] = kr[...]
            vbuf[QB * j:QB * j + QB, :] = vr[...]
        valid = (i * QB - 2 * QB + lax.broadcasted_iota(jnp.int32, (1, WIN), 1)) >= 0
        for h in range(NH):
            sl = slice(HD * h, HD * h + HD)
            sc = lax.dot_general(q_ref[:, sl], kbuf[:, sl], (((1,), (1,)), ((), ())), preferred_element_type=F32)
            sc = jnp.where(valid, sc * SCALE + b_ref[h], -jnp.inf)
            e = jnp.exp(sc - jnp.max(sc, axis=-1, keepdims=True))
            p = e * (1.0 / jnp.sum(e, axis=-1, keepdims=True))
            o_ref[:, sl] = jnp.dot(p.astype(BF), vbuf[:, sl], preferred_element_type=F32)

    return pl.pallas_call(
        body, name="attn_fwd", grid=(nb,),
        in_specs=[_row(QB, AW)] + _win_specs(AW, nb) + [_fix((NH, QB, WIN), single=True)],
        out_specs=_row(QB, AW), out_shape=jax.ShapeDtypeStruct((s, AW), F32),
        scratch_shapes=[pltpu.VMEM((WIN, AW), BF), pltpu.VMEM((WIN, AW), BF)],
        compiler_params=_cp("parallel"))(qkv, qkv, qkv, qkv, qkv, qkv, qkv, bias_w)


def _attn2_bwd(qkv, datt, bias_wt):
    s = qkv.shape[0]
    nb = s // QB
    cur = lambda i: jnp.minimum(i, nb - 1)
    late = lambda i: jnp.maximum(i - 2, 0)

    def body(q_ref, k0, k1, k2, v0, v1, v2, do_ref, b_ref, dq_ref, dk_ref, dv_ref, db_ref,
             kbuf, vbuf, dkacc, dvacc, dsbuf):
        i = pl.program_id(0)

        @pl.when(i == 0)
        def _():
            dkacc[...] = jnp.zeros_like(dkacc)
            dvacc[...] = jnp.zeros_like(dvacc)
            db_ref[...] = jnp.zeros_like(db_ref)

        @pl.when(i < nb)
        def _():
            for j, (kr, vr) in enumerate(((k0, v0), (k1, v1), (k2, v2))):
                kbuf[QB * j:QB * j + QB, :] = kr[...]
                vbuf[QB * j:QB * j + QB, :] = vr[...]
            valid = (i * QB - 2 * QB + lax.broadcasted_iota(jnp.int32, (WIN, 1), 0)) >= 0
            for h in range(NH):
                sl = slice(HD * h, HD * h + HD)
                q = q_ref[:, sl]
                k = kbuf[:, sl]
                do = do_ref[:, sl].astype(BF)
                st = lax.dot_general(k, q, (((1,), (1,)), ((), ())), preferred_element_type=F32)
                st = jnp.where(valid, st * SCALE + b_ref[h], -jnp.inf)
                e = jnp.exp(st - jnp.max(st, axis=0, keepdims=True))
                pt = e * (1.0 / jnp.sum(e, axis=0, keepdims=True))
                dvacc[:, sl] += jnp.dot(pt.astype(BF), do, preferred_element_type=F32)
                dpt = lax.dot_general(vbuf[:, sl], do, (((1,), (1,)), ((), ())), preferred_element_type=F32)
                dst = pt * (dpt - jnp.sum(dpt * pt, axis=0, keepdims=True))
                dsbuf[...] = dst
                fold = dsbuf[0:BAND, 0:CH]
                for a in range(1, NQC):
                    fold = fold + dsbuf[CH * a:CH * a + BAND, CH * a:CH * a + CH]
                db_ref[h] += fold
                dsb = dst.astype(BF)
                dkacc[:, sl] += jnp.dot(dsb, q, preferred_element_type=F32) * SCALE
                dq = lax.dot_general(dsb, k, (((0,), (0,)), ((), ())), preferred_element_type=F32) * SCALE
                dq_ref[:, sl] = dq.astype(BF)

        dk_ref[...] = dkacc[0:QB, :].astype(BF)
        dv_ref[...] = dvacc[0:QB, :].astype(BF)
        for acc in (dkacc, dvacc):
            acc[0:QB, :] = acc[QB:2 * QB, :]
            acc[QB:2 * QB, :] = acc[2 * QB:WIN, :]
            acc[2 * QB:WIN, :] = jnp.zeros((QB, AW), F32)

    qspec = pl.BlockSpec((QB, AW), lambda i: (cur(i), 0))
    lspec = pl.BlockSpec((QB, AW), lambda i: (late(i), 0))
    return pl.pallas_call(
        body, name="attn_bwd", grid=(nb + 2,),
        in_specs=[qspec] + _win_specs(AW, nb) + [qspec, _fix((NH, WIN, QB), single=True)],
        out_specs=[qspec, lspec, lspec, _fix((NH, BAND, CH))],
        out_shape=[jax.ShapeDtypeStruct((s, AW), BF)] * 3 + [jax.ShapeDtypeStruct((NH, BAND, CH), F32)],
        scratch_shapes=[pltpu.VMEM((WIN, AW), BF), pltpu.VMEM((WIN, AW), BF),
                        pltpu.VMEM((WIN, AW), F32), pltpu.VMEM((WIN, AW), F32), pltpu.VMEM((WIN, QB), F32)],
        compiler_params=_cp("arbitrary"))(qkv, qkv, qkv, qkv, qkv, qkv, qkv, datt, bias_wt)


TB = 512
SCH = 256
NCB = TB // SCH
GW = 8 * HD


def _tri(lower):
    r = lax.broadcasted_iota(jnp.int32, (SCH, SCH), 0)
    c = lax.broadcasted_iota(jnp.int32, (SCH, SCH), 1)
    return r >= c if lower else r <= c


def _ssd_fwd(proj, conv_w, conv_b, dt_bias, a_log, d_skip):
    s = proj.shape[0]
    nb = s // TB

    def body(xs_ref, bc_ref, dt_ref, cw_ref, cb_ref, dtb_ref, al_ref, dsk_ref, y_ref, pre_ref, prev_ref,
             xpad, ubuf, dtbuf, csb, cstb, hst, xdec, yoffb, drow):
        i = pl.program_id(0)

        @pl.when(i == 0)
        def _():
            xpad[0:8, :] = jnp.zeros((8, XBC), F32)
            hst[...] = jnp.zeros_like(hst)

        xpad[8:8 + TB, 0:SW] = xs_ref[...]
        xpad[8:8 + TB, SW:XBC] = bc_ref[...]
        pre = cb_ref[...]
        for kk in range(4):
            pre = pre + cw_ref[kk:kk + 1, :] * xpad[5 + kk:5 + kk + TB, :]
        pre_ref[...] = pre
        ubuf[...] = pre * jax.nn.sigmoid(pre)
        xpad[0:8, :] = xpad[TB:TB + 8, :]
        dtbuf[...] = _softplus(dt_ref[...] + dtb_ref[...])
        a = -jnp.exp(al_ref[...])
        tri = _tri(True).astype(F32)
        causal = _tri(True)

        def chunk(c, carry):
            r0 = pl.multiple_of(c * SCH, SCH)
            rows = pl.ds(r0, SCH)
            cs = jnp.dot(tri, dtbuf[rows, :] * a, precision=HI, preferred_element_type=F32)
            csb[...] = cs
            cstb[...] = cs.T
            for g in range(NG):
                bg = ubuf[rows, SW + NS * g:SW + NS * g + NS]
                cg = ubuf[rows, SW + NG * NS + NS * g:SW + NG * NS + NS * g + NS].astype(BF)
                cb = lax.dot_general(cg, bg.astype(BF), (((1,), (1,)), ((), ())), preferred_element_type=F32)
                hg = hst[g]
                prev_ref[c, g] = hg
                yoffb[...] = jnp.dot(cg, hg.astype(BF), preferred_element_type=F32)
                for r in range(8):
                    h = 8 * g + r
                    sl = slice(HD * h, HD * h + HD)
                    rs = slice(HD * r, HD * r + HD)
                    cs_h = csb[:, h:h + 1]
                    cl = csb[SCH - 1:SCH, h:h + 1]
                    seg = jnp.exp(jnp.where(causal, cs_h - cstb[h:h + 1, :], -jnp.inf))
                    xh = ubuf[rows, sl]
                    xdt = xh * dtbuf[rows, h:h + 1]
                    yd = jnp.dot((cb * seg).astype(BF), xdt.astype(BF), preferred_element_type=F32)
                    y_ref[rows, sl] = yd + jnp.exp(cs_h) * yoffb[:, rs] + xh * dsk_ref[:, h:h + 1]
                    xdec[:, rs] = xdt * jnp.exp(cl - cs_h)
                    drow[:, rs] = jnp.broadcast_to(jnp.exp(cl), (1, HD))
                st = jnp.dot(bg.T.astype(BF), xdec[...].astype(BF), preferred_element_type=F32)
                hst[g] = hg * drow[...] + st
            return carry

        lax.fori_loop(0, NCB, chunk, 0)

    return pl.pallas_call(
        body, name="ssd_fwd", grid=(nb,),
        in_specs=[_row(TB, SW, 4), _row(TB, 512, 10), _row(TB, LANE, 44), _fix((4, XBC)), _fix((1, XBC)),
                  _fix((1, LANE)), _fix((1, LANE)), _fix((1, LANE))],
        out_specs=[_row(TB, SW), _row(TB, XBC), pl.BlockSpec((NCB, NG, NS, GW), lambda i: (i, 0, 0, 0))],
        out_shape=[jax.ShapeDtypeStruct((s, SW), F32), jax.ShapeDtypeStruct((s, XBC), F32),
                   jax.ShapeDtypeStruct((s // SCH, NG, NS, GW), F32)],
        scratch_shapes=[pltpu.VMEM((TB + 8, XBC), F32), pltpu.VMEM((TB, XBC), F32), pltpu.VMEM((TB, LANE), F32),
                        pltpu.VMEM((SCH, LANE), F32), pltpu.VMEM((LANE, SCH), F32), pltpu.VMEM((NG, NS, GW), F32),
                        pltpu.VMEM((SCH, GW), F32), pltpu.VMEM((SCH, GW), F32), pltpu.VMEM((1, GW), F32)],
        compiler_params=_cp("arbitrary"))(proj, proj, proj, conv_w, conv_b, dt_bias, a_log, d_skip)


def _ssd_bwd(dy, pre, proj, prev, conv_w, dt_bias, a_log, d_skip):
    s = dy.shape[0]
    nb = s // TB
    rev = lambda i: nb - 1 - i
    halo = lambda i: jnp.maximum((nb - 1 - i) * (TB // 8) - 1, 0)

    def body(dy_ref, pre_ref, dt_ref, prev_ref, xs_ref, bc_ref, xsh_ref, bch_ref, cw_ref, dtb_ref, al_ref, dsk_ref,
             dx_ref, ddt_ref, pw_ref, ph_ref,
             ubuf, dtbuf, dpad, xpad, csb, cstb, dhs, yoffb, ebuf, xdecb, wbuf, drow, dub):
        i = pl.program_id(0)

        @pl.when(i == 0)
        def _():
            dhs[...] = jnp.zeros_like(dhs)
            dpad[TB:TB + 8, :] = jnp.zeros((8, XBC), F32)
            pw_ref[...] = jnp.zeros_like(pw_ref)
            ph_ref[...] = jnp.zeros_like(ph_ref)

        pre = pre_ref[...]
        ubuf[...] = pre * jax.nn.sigmoid(pre)
        dtbuf[...] = _softplus(dt_ref[...] + dtb_ref[...])
        a = -jnp.exp(al_ref[...])
        tri = _tri(True).astype(F32)
        trit = _tri(False).astype(F32)
        causal = _tri(True)
        upper = _tri(False)
        lane = lax.broadcasted_iota(jnp.int32, (SCH, LANE), 1)
        lane1 = lax.broadcasted_iota(jnp.int32, (1, LANE), 1)
        lastrow = lax.broadcasted_iota(jnp.int32, (SCH, LANE), 0) == SCH - 1

        def chunk(cc, carry):
            c = NCB - 1 - cc
            r0 = pl.multiple_of(c * SCH, SCH)
            rows = pl.ds(r0, SCH)
            dtc = dtbuf[rows, :]
            cs = jnp.dot(tri, dtc * a, precision=HI, preferred_element_type=F32)
            csb[...] = cs
            cstb[...] = cs.T
            dcs = jnp.zeros((SCH, LANE), F32)
            xr = jnp.zeros((SCH, LANE), F32)
            dlast = jnp.zeros((1, LANE), F32)
            dsk = jnp.zeros((1, LANE), F32)
            for g in range(NG):
                bg = ubuf[rows, SW + NS * g:SW + NS * g + NS]
                cg = ubuf[rows, SW + NG * NS + NS * g:SW + NG * NS + NS * g + NS]
                bb = bg.astype(BF)
                cbf = cg.astype(BF)
                cb = lax.dot_general(cbf, bb, (((1,), (1,)), ((), ())), preferred_element_type=F32)
                cbt = lax.dot_general(bb, cbf, (((1,), (1,)), ((), ())), preferred_element_type=F32)
                hp = prev_ref[c, g]
                hpb = hp.astype(BF)
                dhg = dhs[g]
                dhb = dhg.astype(BF)
                yoffb[...] = jnp.dot(cbf, hpb, preferred_element_type=F32)
                for r in range(8):
                    h = 8 * g + r
                    sl = slice(HD * h, HD * h + HD)
                    rs = slice(HD * r, HD * r + HD)
                    cs_h = csb[:, h:h + 1]
                    cl = csb[SCH - 1:SCH, h:h + 1]
                    dyh = dy_ref[rows, sl]
                    ecs = jnp.exp(cs_h)
                    ebuf[:, rs] = dyh * ecs
                    xdecb[:, rs] = ubuf[rows, sl] * dtbuf[rows, h:h + 1] * jnp.exp(cl - cs_h)
                    t1 = jnp.sum(dyh * ecs * yoffb[:, rs], axis=1, keepdims=True)
                    dcs = dcs + jnp.where(lane == h, t1, 0.0)
                    drow[:, rs] = jnp.broadcast_to(jnp.exp(cl), (1, HD))
                eb = ebuf[...].astype(BF)
                dc = lax.dot_general(eb, hpb, (((1,), (1,)), ((), ())), preferred_element_type=F32)
                dprev = jnp.dot(cg.T.astype(BF), eb, preferred_element_type=F32)
                wbuf[...] = jnp.dot(bb, dhb, preferred_element_type=F32)
                db = lax.dot_general(xdecb[...].astype(BF), dhb, (((1,), (1,)), ((), ())), preferred_element_type=F32)
                dcbs = jnp.zeros((SCH, SCH), F32)
                dcbts = jnp.zeros((SCH, SCH), F32)
                for r in range(8):
                    h = 8 * g + r
                    sl = slice(HD * h, HD * h + HD)
                    rs = slice(HD * r, HD * r + HD)
                    cs_h = csb[:, h:h + 1]
                    cst_h = cstb[h:h + 1, :]
                    cl = csb[SCH - 1:SCH, h:h + 1]
                    seg = jnp.exp(jnp.where(causal, cs_h - cst_h, -jnp.inf))
                    segt = jnp.exp(jnp.where(upper, cst_h - cs_h, -jnp.inf))
                    dyh = dy_ref[rows, sl]
                    xh = ubuf[rows, sl]
                    dt_h = dtbuf[rows, h:h + 1]
                    xdt = xh * dt_h
                    dyb = dyh.astype(BF)
                    xdb = xdt.astype(BF)
                    dm = lax.dot_general(dyb, xdb, (((1,), (1,)), ((), ())), preferred_element_type=F32)
                    dmt = lax.dot_general(xdb, dyb, (((1,), (1,)), ((), ())), preferred_element_type=F32)
                    mt = cbt * segt
                    dxdt = jnp.dot(mt.astype(BF), dyb, preferred_element_type=F32)
                    dcs_h = (jnp.sum(dm * (cb * seg), axis=1, keepdims=True)
                             - jnp.sum(dmt * mt, axis=1, keepdims=True))
                    dcbs = dcbs + dm * seg
                    dcbts = dcbts + dmt * segt
                    dec = jnp.exp(cl - cs_h)
                    w_h = wbuf[:, rs]
                    dxdt = dxdt + dec * w_h
                    t = jnp.sum(w_h * xdt, axis=1, keepdims=True) * dec
                    dcs_h = dcs_h - t
                    dch = jnp.sum(jnp.sum(dhs[g, :, rs] * prev_ref[c, g, :, rs], axis=1, keepdims=True),
                                  axis=0, keepdims=True)
                    dl = jnp.sum(t, axis=0, keepdims=True) + dch * jnp.exp(cl)
                    dcs = dcs + jnp.where(lane == h, dcs_h, 0.0)
                    dlast = dlast + jnp.where(lane1 == h, dl, 0.0)
                    xr = xr + jnp.where(lane == h, jnp.sum(dxdt * xh, axis=1, keepdims=True), 0.0)
                    dskh = jnp.sum(jnp.sum(dyh * xh, axis=1, keepdims=True), axis=0, keepdims=True)
                    dsk = dsk + jnp.where(lane1 == h, dskh, 0.0)
                    dub[:, sl] = dyh * dsk_ref[:, h:h + 1] + dxdt * dt_h
                dc = dc + jnp.dot(dcbs.astype(BF), bb, preferred_element_type=F32)
                db = db + jnp.dot(dcbts.astype(BF), cbf, preferred_element_type=F32)
                dub[:, SW + NS * g:SW + NS * g + NS] = db
                dub[:, SW + NG * NS + NS * g:SW + NG * NS + NS * g + NS] = dc
                dhs[g] = dprev + drow[...] * dhg
            dcs = dcs + jnp.where(lastrow, dlast, 0.0)
            dadt = jnp.dot(trit, dcs, precision=HI, preferred_element_type=F32)
            ddt = dadt * a + xr
            ddtr = ddt * jax.nn.sigmoid(dt_ref[rows, :] + dtb_ref[...])
            ddt_ref[rows, :] = ddtr.astype(BF)
            ph_ref[0:1, :] += jnp.sum(ddtr, axis=0, keepdims=True)
            ph_ref[1:2, :] += jnp.sum(dadt * dtc, axis=0, keepdims=True) * a
            ph_ref[2:3, :] += dsk
            pc = pre_ref[rows, :]
            sg = jax.nn.sigmoid(pc)
            dpad[rows, :] = dub[...] * _silu_grad(pc, sg)
            return carry

        lax.fori_loop(0, NCB, chunk, 0)

        dxb = cw_ref[0:1, :] * dpad[3:3 + TB, :]
        for kk in range(1, 4):
            dxb = dxb + cw_ref[kk:kk + 1, :] * dpad[3 - kk:3 - kk + TB, :]
        dx_ref[...] = dxb.astype(BF)
        keep = jnp.where(i < nb - 1, 1.0, 0.0)
        xpad[0:8, 0:SW] = xsh_ref[...] * keep
        xpad[0:8, SW:XBC] = bch_ref[...] * keep
        xpad[8:8 + TB, 0:SW] = xs_ref[...]
        xpad[8:8 + TB, SW:XBC] = bc_ref[...]
        dp = dpad[0:TB, :]
        for kk in range(4):
            pw_ref[kk:kk + 1, :] += jnp.sum(dp * xpad[5 + kk:5 + kk + TB, :], axis=0, keepdims=True)
        pw_ref[4:5, :] += jnp.sum(dp, axis=0, keepdims=True)
        dpad[TB:TB + 8, :] = dpad[0:8, :]

    rblk = lambda w, col: pl.BlockSpec((TB, w), lambda i, _c=col: (rev(i), _c))
    return pl.pallas_call(
        body, name="ssd_bwd", grid=(nb,),
        in_specs=[rblk(SW, 0), rblk(XBC, 0), rblk(LANE, 44),
                  pl.BlockSpec((NCB, NG, NS, GW), lambda i: (rev(i), 0, 0, 0)),
                  rblk(SW, 4), rblk(512, 10),
                  pl.BlockSpec((8, SW), lambda i: (halo(i), 4)), pl.BlockSpec((8, 512), lambda i: (halo(i), 10)),
                  _fix((4, XBC)), _fix((1, LANE)), _fix((1, LANE)), _fix((1, LANE))],
        out_specs=[rblk(XBC, 0), rblk(LANE, 0), _fix((8, XBC)), _fix((8, LANE))],
        out_shape=[jax.ShapeDtypeStruct((s, XBC), BF), jax.ShapeDtypeStruct((s, LANE), BF),
                   jax.ShapeDtypeStruct((8, XBC), F32), jax.ShapeDtypeStruct((8, LANE), F32)],
        scratch_shapes=[pltpu.VMEM((TB, XBC), F32), pltpu.VMEM((TB, LANE), F32), pltpu.VMEM((TB + 8, XBC), F32),
                        pltpu.VMEM((TB + 8, XBC), F32), pltpu.VMEM((SCH, LANE), F32), pltpu.VMEM((LANE, SCH), F32),
                        pltpu.VMEM((NG, NS, GW), F32), pltpu.VMEM((SCH, GW), F32), pltpu.VMEM((SCH, GW), F32),
                        pltpu.VMEM((SCH, GW), F32), pltpu.VMEM((SCH, GW), F32), pltpu.VMEM((1, GW), F32),
                        pltpu.VMEM((SCH, XBC), F32)],
        compiler_params=_cp("arbitrary"))(dy, pre, proj, prev, proj, proj, proj, proj, conv_w, dt_bias, a_log, d_skip)


def _rel_index():
    q = jnp.arange(CH)[:, None] + (BAND - CH)
    k = jnp.arange(BAND)[None, :]
    return jnp.clip(q - k, -(CH - 1), REL_CLIP) + (CH - 1)


def _local_step(x, tgt, mods, w_in_p, w_in_t, late_weights, ffn_grads_ready,
                g_mix, rel_bias, conv_w, conv_b, dt_bias, a_log, d_skip, g_att, g_ssd, g_ffn, g_final):
    sh1, sc1, gt1, sh2, sc2, gt2 = mods
    pad16 = lambda v: jnp.pad(v, ((0, 0), (0, LANE - NH)))
    dtb_p, al_p, dsk_p = pad16(dt_bias), pad16(a_log), pad16(d_skip)
    ridx = _rel_index()
    bias = rel_bias[:, ridx]
    bias_w = _bias_window(bias)
    bias_wt = jnp.swapaxes(bias_w, 1, 2)

    h1 = _norm_fwd(x, g_mix, sc1, sh1, name="norm1_fwd")
    proj = _mm(h1, w_in_p, F32, "in_proj", 512, 1152, 2048)
    qkv = proj[:, :3 * AW].astype(BF)
    att = _attn2_fwd(qkv, bias_w)
    y, pre, prev = _ssd_fwd(proj, conv_w, conv_b, dtb_p, al_p, dsk_p)
    cat = _mixout_fwd(att, y, proj, g_att, g_ssd)
    w_out, w_out_t, w_gu, w_gu_t, w_down, w_down_t = late_weights(cat)
    mix = _mm(cat, w_out, F32, "out_proj", 512, 1024, 2048)
    h2, x1 = _norm_fwd(x, g_ffn, sc2, sh2, mix, gt1, name="norm2_fwd")
    gu = _mm(h2, w_gu, BF, "gate_up", 512, 1024, 2048)
    act = _swiglu_fwd(gu)
    ffn = _mm(act, w_down, F32, "down_proj", 512, 1024, 1408)
    dx2, dffn, pf = _final(x1, ffn, gt2, g_final, tgt)

    dact = _mm(dffn, w_down_t, BF, "d_act", 512, 1408, 2048)
    gw_down = _mm(act.T, dffn, F32, "gw_down", 704, 1024, 1024)
    dgu = _swiglu_bwd(gu, dact)
    dh2 = _mm(dgu, w_gu_t, F32, "d_h2", 512, 1024, 1408)
    gw_gu = _mm(h2.T, dgu, F32, "gw_gate_up", 512, 1024, 1024)
    zero = ffn_grads_ready(gw_down, gw_gu)
    dx1, dmix, p2 = _norm_bwd(x1, g_ffn + zero, sc2, sh2, dh2, dx2, mix, gt1, name="norm2_bwd")
    dcat = _mm(dmix, w_out_t, F32, "d_cat", 512, 1024, 2048)
    gw_out = _mm(cat.T, dmix, F32, "gw_out", 512, 1024, 1024)
    datt, dy, dz, pm = _mixout_bwd(dcat, att, y, proj, g_att, g_ssd)
    dq, dk, dv, dbias_t = _attn2_bwd(qkv, datt, bias_wt)
    dxbc, ddt, pw, ph = _ssd_bwd(dy, pre, proj, prev, conv_w, dtb_p, al_p, dsk_p)
    dproj = jnp.concatenate([dq, dk, dv, dz, dxbc, ddt], axis=1)
    dh1 = _mm(dproj, w_in_t, F32, "d_h1", 512, 1024, 1152)
    gw_in = _mm(h1.T, dproj, F32, "gw_in", 512, 1152, 1024)
    gx, p1 = _norm_bwd(x, g_mix, sc1, sh1, dh1, dx1, name="norm1_bwd")

    onehot = (ridx.T.reshape(BAND * CH, 1) == jnp.arange(NREL)[None, :]).astype(F32)
    g_rel = _mm(dbias_t.reshape(NH, BAND * CH), onehot, F32, "g_rel_bias", NH, NREL, 4608, precision=HI)

    small = dict(
        dmods=jnp.concatenate([p1[1], p1[0], p2[3], p2[1], p2[0], pf[1]]),
        g_mix=p1[2], conv_b=pw[4], dt_bias=ph[0], a_log=ph[1], d_skip=ph[2], g_att=pm[0], g_ssd=pm[1],
        g_ffn=p2[2], g_final=pf[0], rel_bias=g_rel.reshape(-1), conv_w=pw[0:4].reshape(-1), loss=pf[2])
    return gx, gw_in, gw_out, small


def _exchange(srcs, gather, name):
    n = len(srcs)
    outs = [jax.ShapeDtypeStruct((NDEV,) + (s.shape if g else s.shape[1:]), s.dtype) for s, g in zip(srcs, gather)]

    def body(*refs):
        src, dst = refs[:n], refs[n:2 * n]
        ssem, rsem, lsem = refs[2 * n:]
        x, y, c = lax.axis_index("x"), lax.axis_index("y"), lax.axis_index("c")
        me = 4 * x + 2 * y + c
        local = []
        for a in range(n):
            own = src[a] if gather[a] else src[a].at[me]
            cp = pltpu.make_async_copy(own, dst[a].at[me], lsem.at[a])
            cp.start()
            local.append(cp)
        remote = []
        for k in range(1, NDEV):
            px = 1 - x if k & 4 else x
            py = 1 - y if k & 2 else y
            pc = 1 - c if k & 1 else c
            pid = 4 * px + 2 * py + pc
            for a in range(n):
                s_ref = src[a] if gather[a] else src[a].at[pid]
                cp = pltpu.make_async_remote_copy(
                    src_ref=s_ref, dst_ref=dst[a].at[me], send_sem=ssem.at[a * (NDEV - 1) + k - 1],
                    recv_sem=rsem.at[a * (NDEV - 1) + k - 1], device_id=(px, py, pc), device_id_type=MESH)
                cp.start()
                remote.append(cp)
        for cp in remote:
            cp.wait()
        for cp in local:
            cp.wait()

    anyspec = pl.BlockSpec(memory_space=pl.ANY)
    return pl.pallas_call(
        body, name=name, out_shape=outs, in_specs=[anyspec] * n, out_specs=[anyspec] * n,
        scratch_shapes=[pltpu.SemaphoreType.DMA((n * (NDEV - 1),)), pltpu.SemaphoreType.DMA((n * (NDEV - 1),)),
                        pltpu.SemaphoreType.DMA((n,))],
    )(*srcs)


_HBM = pl.BlockSpec(memory_space=pltpu.HBM)
_SEM = pl.BlockSpec(memory_space=pltpu.SEMAPHORE)
_EFFECT = pltpu.SideEffectType.DATAFLOW_SIDE_EFFECTING


def _peer_copies(src, land, ssem, rsem, gather):
    x, y, c = lax.axis_index("x"), lax.axis_index("y"), lax.axis_index("c")
    me = 4 * x + 2 * y + c
    copies = []
    for a in range(len(src)):
        for k in range(1, NDEV):
            px = 1 - x if k & 4 else x
            py = 1 - y if k & 2 else y
            pc = 1 - c if k & 1 else c
            s_ref = src[a] if gather[a] else src[a].at[4 * px + 2 * py + pc]
            idx = a * (NDEV - 1) + k - 1
            copies.append(pltpu.make_async_remote_copy(
                src_ref=s_ref, dst_ref=land[a].at[me], send_sem=ssem.at[idx], recv_sem=rsem.at[idx],
                device_id=(px, py, pc), device_id_type=MESH))
    return copies


def _xstart(srcs, gather, name):
    n = len(srcs)
    me = 4 * lax.axis_index("x") + 2 * lax.axis_index("y") + lax.axis_index("c")
    lands = []
    for s_, g_ in zip(srcs, gather):
        own = s_ if g_ else lax.dynamic_index_in_dim(s_, me, 0, keepdims=False)
        empty = lax.empty((NDEV,) + own.shape, own.dtype)
        lands.append(lax.dynamic_update_slice(empty, own[None], (me,) + (0,) * own.ndim))
    nsem = n * (NDEV - 1)

    def body(*refs):
        src, land, ssem, rsem, token = refs[:n], refs[n:2 * n], refs[2 * n], refs[2 * n + 1], refs[-1]
        for cp in _peer_copies(src, land, ssem, rsem, gather):
            cp.start()
        token[...] = jnp.zeros_like(token)

    ops = [pltpu.with_memory_space_constraint(a_, pltpu.HBM) for a_ in list(srcs) + lands]
    return pl.pallas_call(
        body, name=name,
        out_shape=(pltpu.SemaphoreType.DMA((nsem,)), pltpu.SemaphoreType.DMA((nsem,)),
                   *[pltpu.HBM(a_.shape, a_.dtype) for a_ in ops], jax.ShapeDtypeStruct((8, LANE), F32)),
        in_specs=[_HBM] * (2 * n),
        out_specs=(_SEM, _SEM, *[_HBM] * (2 * n), pl.BlockSpec(memory_space=pltpu.VMEM)),
        input_output_aliases={i: 2 + i for i in range(2 * n)},
        compiler_params=pltpu.CompilerParams(has_side_effects=_EFFECT),
    )(*ops)


def _xwait(started, gather, after, name):
    n = len(gather)
    ssem, rsem = started[0], started[1]
    bufs = started[2:2 + 2 * n]

    def body(*refs):
        src, land, ssem_, rsem_ = refs[:n], refs[n:2 * n], refs[2 * n], refs[2 * n + 1]
        for cp in _peer_copies(src, land, ssem_, rsem_, gather):
            cp.wait_send()
            cp.wait_recv()

    out = pl.pallas_call(
        body, name=name, out_shape=tuple(pltpu.HBM(a_.shape, a_.dtype) for a_ in bufs),
        in_specs=[_HBM] * (2 * n) + [_SEM, _SEM, pl.BlockSpec(memory_space=pl.ANY)],
        out_specs=[_HBM] * (2 * n), input_output_aliases={i: i for i in range(2 * n)},
        compiler_params=pltpu.CompilerParams(has_side_effects=_EFFECT),
    )(*bufs, ssem, rsem, after)
    return out[n:]


def _mods(c16, w_shard, b_shard, *, tn=512):
    n = w_shard.shape[1]

    def body(c_ref, w_ref, b_ref, o_ref):
        cv = c_ref[...]
        cond = (cv * jax.nn.sigmoid(cv)).astype(BF)
        o_ref[...] = jnp.dot(cond, w_ref[...].astype(BF), preferred_element_type=F32) + b_ref[...]

    return pl.pallas_call(body, name="ada_mods", grid=(n // tn,),
                          in_specs=[_fix((16, D)), pl.BlockSpec((D, tn), lambda j: (0, j)),
                                    pl.BlockSpec((1, tn), lambda j: (0, j))],
                          out_specs=pl.BlockSpec((16, tn), lambda j: (0, j)),
                          out_shape=jax.ShapeDtypeStruct((16, n), F32), compiler_params=_cp("parallel"))(c16, w_shard, b_shard)


def _gw_ada(c_t, dm, *, tr=256):
    n = dm.shape[1]

    def body(ct_ref, dm_ref, o_ref):
        acc = jnp.zeros((tr, n), F32)
        for b in range(NDEV):
            cv = ct_ref[:, b:b + 1]
            cond = (cv * jax.nn.sigmoid(cv)).astype(BF).astype(F32)
            acc = acc + cond * dm_ref[b:b + 1, :].astype(BF).astype(F32)
        o_ref[...] = acc

    return pl.pallas_call(body, name="gw_ada", grid=(D // tr,),
                          in_specs=[_row(tr, NDEV), _fix((NDEV, n))], out_specs=_row(tr, n),
                          out_shape=jax.ShapeDtypeStruct((D, n), F32), compiler_params=_cp("parallel"))(c_t, dm)


def _sum8(parts, *, name):
    _, r, c = parts.shape

    def body(p_ref, o_ref):
        acc = p_ref[0].astype(F32)
        for d in range(1, NDEV):
            acc = acc + p_ref[d].astype(F32)
        o_ref[...] = acc

    return pl.pallas_call(body, name=name, grid=(1,), in_specs=[_fix((NDEV, r, c))], out_specs=_fix((r, c)),
                          out_shape=jax.ShapeDtypeStruct((r, c), F32), compiler_params=_cp("arbitrary"))(parts)


def _adam(w, m, v, *, recv=None, grad=None, tr, name):
    r, c = w.shape
    summed = recv is not None
    assert r % tr == 0

    def body(g_ref, w_ref, m_ref, v_ref, go_ref, do_ref, mo_ref, vo_ref):
        if summed:
            g = g_ref[0].astype(F32)
            for d in range(1, NDEV):
                g = g + g_ref[d].astype(F32)
        else:
            g = g_ref[...]
        go_ref[...] = g
        mn = B1 * m_ref[...] + (1.0 - B1) * g
        vn = B2 * v_ref[...] + (1.0 - B2) * (g * g)
        mo_ref[...] = mn
        vo_ref[...] = vn
        mh = mn / (1.0 - B1 ** STEP)
        vh = vn / (1.0 - B2 ** STEP)
        do_ref[...] = -LR * (mh / (jnp.sqrt(vh) + AEPS) + WD * w_ref[...])

    gspec = pl.BlockSpec((NDEV, tr, c), lambda i: (0, i, 0)) if summed else _row(tr, c)
    return pl.pallas_call(body, name=name, grid=(r // tr,),
                          in_specs=[gspec, _row(tr, c), _row(tr, c), _row(tr, c)], out_specs=[_row(tr, c)] * 4,
                          out_shape=[jax.ShapeDtypeStruct((r, c), F32)] * 4,
                          compiler_params=_cp("parallel"))(recv if summed else grad, w, m, v)


def _cols_to_blocks(g, width):
    r = g.shape[0]
    return jnp.transpose(g.reshape(r, NDEV, width), (1, 0, 2))


def _blocks_to_cols(b):
    _, r, width = b.shape
    return jnp.transpose(b, (1, 0, 2)).reshape(r, NDEV * width)


_SMALL = ("b_ada", "g_mix", "rel_bias", "conv_w", "conv_b", "dt_bias", "a_log", "d_skip",
          "g_att_out", "g_ssd_out", "g_ffn", "g_final")


def _pad_rows(v):
    v = v.reshape(-1)
    return jnp.pad(v, (0, (-v.shape[0]) % LANE))


def kernel(x, c, w_ada, b_ada, g_mix, w_in, rel_bias, conv_w, conv_b, dt_bias, a_log, d_skip, g_att_out, g_ssd_out, w_out, g_ffn, w_gate, w_up, w_down, g_final, loss_target, m_w_ada, m_b_ada, m_g_mix, m_w_in, m_rel_bias, m_conv_w, m_conv_b, m_dt_bias, m_a_log, m_d_skip, m_g_att_out, m_g_ssd_out, m_w_out, m_g_ffn, m_w_gate, m_w_up, m_w_down, m_g_final, v_w_ada, v_b_ada, v_g_mix, v_w_in, v_rel_bias, v_conv_w, v_conv_b, v_dt_bias, v_a_log, v_d_skip, v_g_att_out, v_g_ssd_out, v_w_out, v_g_ffn, v_w_gate, v_w_up, v_w_down, v_g_final):
    me = 4 * lax.axis_index("x") + 2 * lax.axis_index("y") + lax.axis_index("c")
    wts = dict(w_ada=w_ada, b_ada=b_ada, g_mix=g_mix, w_in=w_in, rel_bias=rel_bias, conv_w=conv_w, conv_b=conv_b,
               dt_bias=dt_bias, a_log=a_log, d_skip=d_skip, g_att_out=g_att_out, g_ssd_out=g_ssd_out, w_out=w_out,
               g_ffn=g_ffn, w_gate=w_gate, w_up=w_up, w_down=w_down, g_final=g_final)
    mom = dict(w_ada=m_w_ada, b_ada=m_b_ada, g_mix=m_g_mix, w_in=m_w_in, rel_bias=m_rel_bias, conv_w=m_conv_w,
               conv_b=m_conv_b, dt_bias=m_dt_bias, a_log=m_a_log, d_skip=m_d_skip, g_att_out=m_g_att_out,
               g_ssd_out=m_g_ssd_out, w_out=m_w_out, g_ffn=m_g_ffn, w_gate=m_w_gate, w_up=m_w_up, w_down=m_w_down,
               g_final=m_g_final)
    var = dict(w_ada=v_w_ada, b_ada=v_b_ada, g_mix=v_g_mix, w_in=v_w_in, rel_bias=v_rel_bias, conv_w=v_conv_w,
               conv_b=v_conv_b, dt_bias=v_dt_bias, a_log=v_a_log, d_skip=v_d_skip, g_att_out=v_g_att_out,
               g_ssd_out=v_g_ssd_out, w_out=v_w_out, g_ffn=v_g_ffn, w_gate=v_w_gate, w_up=v_w_up, w_down=v_w_down,
               g_final=v_g_final)
    order = ("w_ada", "b_ada", "g_mix", "w_in", "rel_bias", "conv_w", "conv_b", "dt_bias", "a_log", "d_skip",
             "g_att_out", "g_ssd_out", "w_out", "g_ffn", "w_gate", "w_up", "w_down", "g_final")

    st_in = _xstart([w_in[0].astype(BF)], [True], "gather_w_in_start")
    tok_in = st_in[-1][0, 0]
    st_rest = _xstart([(w_gate[0] + tok_in).astype(BF), w_up[0].astype(BF), w_down[0].astype(BF),
                       w_out[0].astype(BF)], [True] * 4, "gather_w_rest_start")
    tok = tok_in + st_rest[-1][0, 0]

    c_all, rel_all, cw_all = _exchange([c + tok, rel_bias[0], conv_w[0]], [True] * 3, "gather_small")
    c_all = c_all.reshape(NDEV, D)
    rel_full = _blocks_to_cols(rel_all)
    cw_full = _blocks_to_cols(cw_all)

    ncol = w_ada.shape[2]
    b_sh = lax.dynamic_slice(b_ada, (0, me * ncol), (1, ncol))
    mods_part = _mods(jnp.pad(c_all, ((0, 8), (0, 0))), w_ada[0], b_sh)[:NDEV]
    (mods_all,) = _exchange([mods_part], [True], "gather_mods")
    mods_me = lax.dynamic_index_in_dim(mods_all, me, axis=1, keepdims=False).reshape(6, 1, D)
    mods = [mods_me[i] for i in range(6)]

    (gi,) = _xwait(st_in, [True], mods_all, "gather_w_in_wait")
    w_in_p = jnp.pad(_blocks_to_cols(gi), ((0, 0), (0, INP - INC)))

    def late_weights(after):
        gg, gu_, gd, go = _xwait(st_rest, [True] * 4, after, "gather_w_rest_wait")
        w_gu = jnp.concatenate([_blocks_to_cols(gg), _blocks_to_cols(gu_)], axis=1)
        w_dn = gd.reshape(FF, D)
        w_o = go.reshape(D, D)
        return w_o, w_o.T, w_gu, w_gu.T, w_dn, w_dn.T

    started = {}

    def ffn_grads_ready(gw_down, gw_gu):
        started["ffn"] = _xstart(
            [_cols_to_blocks(gw_gu[:, :FF], FF // NDEV).astype(BF),
             _cols_to_blocks(gw_gu[:, FF:], FF // NDEV).astype(BF),
             gw_down.reshape(NDEV, FF // NDEV, D).astype(BF)], [False] * 3, "exchange_ffn_grads_start")
        return started["ffn"][-1][0, 0]

    gx, gw_in, gw_out, small = _local_step(
        x[0], loss_target[0], mods, w_in_p, w_in_p.T, late_weights, ffn_grads_ready,
        g_mix, rel_full, cw_full, conv_b, dt_bias, a_log, d_skip, g_att_out, g_ssd_out, g_ffn,
        g_final.reshape(1, D))

    loss_row = jnp.pad(jnp.sum(small["loss"]).reshape(1), (0, LANE - 1))
    pack = jnp.concatenate([small[k] for k in ("dmods", "g_mix", "conv_b", "dt_bias", "a_log", "d_skip", "g_att",
                                               "g_ssd", "g_ffn", "g_final", "rel_bias", "conv_w")] + [loss_row])
    pack = pack.reshape(-1, LANE)
    st_last = _xstart([pack, _cols_to_blocks(gw_in[:, :INC], INC // NDEV).astype(BF),
                       gw_out.reshape(NDEV, D // NDEV, D).astype(BF)], [True, False, False],
                      "exchange_rest_grads_start")
    rg, ru, rd = _xwait(started["ffn"], [False] * 3, st_last[-1], "exchange_ffn_grads_wait")
    big = dict(
        w_gate=_adam(w_gate[0], m_w_gate[0], v_w_gate[0], recv=rg, tr=256, name="adam_w_gate"),
        w_up=_adam(w_up[0], m_w_up[0], v_w_up[0], recv=ru, tr=256, name="adam_w_up"),
        w_down=_adam(w_down[0], m_w_down[0], v_w_down[0], recv=rd, tr=176, name="adam_w_down"))
    packs, ri, ro = _xwait(st_last, [True, False, False], big["w_down"][1], "exchange_rest_grads_wait")
    tot = _sum8(packs, name="sum_small").reshape(-1)

    sizes = dict(dmods=6 * D, g_mix=D, conv_b=XBC, dt_bias=LANE, a_log=LANE, d_skip=LANE, g_att_out=AW, g_ssd_out=SW,
                 g_ffn=D, g_final=D, rel_bias=NH * NREL, conv_w=4 * XBC, loss=LANE)
    off, o = {}, 0
    for k_, n_ in sizes.items():
        off[k_] = o
        o += n_
    take = lambda k_, n_=None: tot[off[k_]:off[k_] + (n_ or sizes[k_])]
    loss = tot[off["loss"]]
    rel_w = NREL // NDEV
    cw_w = XBC // NDEV
    gsm = dict(
        b_ada=take("dmods"), g_mix=take("g_mix"),
        rel_bias=lax.dynamic_slice(take("rel_bias").reshape(NH, NREL), (0, me * rel_w), (NH, rel_w)),
        conv_w=lax.dynamic_slice(take("conv_w").reshape(4, XBC), (0, me * cw_w), (4, cw_w)),
        conv_b=take("conv_b"), dt_bias=take("dt_bias", NH), a_log=take("a_log", NH), d_skip=take("d_skip", NH),
        g_att_out=take("g_att_out"), g_ssd_out=take("g_ssd_out"), g_ffn=take("g_ffn"), g_final=take("g_final"))

    flat = lambda d_: jnp.concatenate([_pad_rows(d_[k_]) for k_ in _SMALL])
    gflat = flat(gsm)
    nrow = -(-gflat.shape[0] // (8 * LANE)) * 8
    to2d = lambda v_: jnp.pad(v_, (0, nrow * LANE - v_.shape[0])).reshape(nrow, LANE)
    res = _adam(to2d(flat(wts)), to2d(flat(mom)), to2d(flat(var)), grad=to2d(gflat), tr=nrow, name="adam_small")
    outs = {k_: {} for k_ in ("grad", "delta", "m", "v")}
    o = 0
    for k_ in _SMALL:
        n_ = wts[k_].size
        for kind, arr in zip(("grad", "delta", "m", "v"), res):
            outs[kind][k_] = arr.reshape(-1)[o:o + n_].reshape(wts[k_].shape)
        o += n_ + (-n_) % LANE

    dm_all = packs[:, :6 * D // LANE, :].reshape(NDEV, 6 * D)
    dm_sh = lax.dynamic_slice(dm_all, (0, me * ncol), (NDEV, ncol))
    gwa = _gw_ada(c_all.T, dm_sh)
    big.update(
        w_ada=_adam(w_ada[0], m_w_ada[0], v_w_ada[0], grad=gwa, tr=256, name="adam_w_ada"),
        w_in=_adam(w_in[0], m_w_in[0], v_w_in[0], recv=ri, tr=256, name="adam_w_in"),
        w_out=_adam(w_out[0], m_w_out[0], v_w_out[0], recv=ro, tr=128, name="adam_w_out"))
    for k_, r_ in big.items():
        for kind, arr in zip(("grad", "delta", "m", "v"), r_):
            outs[kind][k_] = arr.reshape(wts[k_].shape)

    return (loss, gx.reshape(x.shape), *[outs["grad"][k_] for k_ in order], *[outs["delta"][k_] for k_ in order],
            *[outs["m"][k_] for k_ in order], *[outs["v"][k_] for k_ in order])
```

```python
import functools

import jax
import jax.numpy as jnp
from jax import lax
from jax.experimental import pallas as pl
from jax.experimental.pallas import tpu as pltpu

F32 = jnp.float32
BF = jnp.bfloat16
HI = lax.Precision.HIGHEST

D = 2048
NH = 16
HD = 64
AW = 1024
SW = 1024
NG = 2
NS = 128
XBC = 1536
CH = 64
BAND = 576
NREL = 320
REL_CLIP = 256
FF = 5632
INC = 5648
INP = 5760
EPS = 1e-6
NDEV = 8
LANE = 128
VMEM_LIMIT = 56 * 1024 * 1024
LR, B1, B2, AEPS, WD, STEP = 0.001, 0.9, 0.999, 1e-08, 0.01, 10
MESH = pl.DeviceIdType.MESH


def _cp(*sem):
    return pltpu.CompilerParams(dimension_semantics=sem, vmem_limit_bytes=VMEM_LIMIT)


def _row(ts, w, col=0):
    return pl.BlockSpec((ts, w), lambda i, _c=col: (i, _c))


def _fix(shape, single=False):
    nd = len(shape)
    if single:
        return pl.BlockSpec(shape, lambda i, _n=nd: (0,) * _n, pipeline_mode=pl.Buffered(1))
    return pl.BlockSpec(shape, lambda i, _n=nd: (0,) * _n)


def _silu_grad(x, s):
    return s * (1.0 + x * (1.0 - s))


def _softplus(x):
    return jnp.maximum(x, 0.0) + jnp.log1p(jnp.exp(-jnp.abs(x)))


def _mm(a, b, out_dtype, name, tm, tn, tk, precision=None):
    m, k = a.shape
    _, n = b.shape
    tm, tn, tk = min(tm, m), min(tn, n), min(tk, k)
    nk = k // tk
    assert m % tm == 0 and n % tn == 0 and k % tk == 0, (name, a.shape, b.shape)

    if nk == 1:
        def whole_k(a_ref, b_ref, o_ref):
            o_ref[...] = jnp.dot(a_ref[...], b_ref[...], preferred_element_type=F32,
                                 precision=precision).astype(o_ref.dtype)

        return pl.pallas_call(
            whole_k, name=name, grid=(m // tm, n // tn),
            in_specs=[pl.BlockSpec((tm, k), lambda i, j: (i, 0)), pl.BlockSpec((k, tn), lambda i, j: (0, j))],
            out_specs=pl.BlockSpec((tm, tn), lambda i, j: (i, j)),
            out_shape=jax.ShapeDtypeStruct((m, n), out_dtype),
            compiler_params=_cp("parallel", "parallel"),
        )(a, b)

    def body(a_ref, b_ref, o_ref, acc):
        kk = pl.program_id(2)

        @pl.when(kk == 0)
        def _():
            acc[...] = jnp.zeros_like(acc)

        acc[...] += jnp.dot(a_ref[...], b_ref[...], preferred_element_type=F32, precision=precision)

        @pl.when(kk == nk - 1)
        def _():
            o_ref[...] = acc[...].astype(o_ref.dtype)

    return pl.pallas_call(
        body, name=name, grid=(m // tm, n // tn, nk),
        in_specs=[pl.BlockSpec((tm, tk), lambda i, j, kk: (i, kk)),
                  pl.BlockSpec((tk, tn), lambda i, j, kk: (kk, j))],
        out_specs=pl.BlockSpec((tm, tn), lambda i, j, kk: (i, j)),
        out_shape=jax.ShapeDtypeStruct((m, n), out_dtype),
        scratch_shapes=[pltpu.VMEM((tm, tn), F32)],
        compiler_params=_cp("parallel", "parallel", "arbitrary"),
    )(a, b)


def _norm_fwd(x, g, sc, sh, mix=None, gt=None, *, name, ts=256):
    s, d = x.shape
    has = mix is not None

    def body(*refs):
        if has:
            x_ref, g_ref, sc_ref, sh_ref, mix_ref, gt_ref, h_ref, x1_ref = refs
        else:
            x_ref, g_ref, sc_ref, sh_ref, h_ref = refs
        xv = x_ref[...]
        if has:
            xv = xv + gt_ref[...] * mix_ref[...]
            x1_ref[...] = xv
        r = lax.rsqrt(jnp.mean(xv * xv, axis=-1, keepdims=True) + EPS)
        h_ref[...] = ((xv * r) * g_ref[...] * (1.0 + sc_ref[...]) + sh_ref[...]).astype(BF)

    ins = [x, g, sc, sh] + ([mix, gt] if has else [])
    in_specs = [_row(ts, d), _fix((1, d)), _fix((1, d)), _fix((1, d))] + ([_row(ts, d), _fix((1, d))] if has else [])
    out_shape = [jax.ShapeDtypeStruct((s, d), BF)] + ([jax.ShapeDtypeStruct((s, d), F32)] if has else [])
    out_specs = [_row(ts, d)] + ([_row(ts, d)] if has else [])
    out = pl.pallas_call(body, name=name, grid=(s // ts,), in_specs=in_specs, out_specs=out_specs,
                         out_shape=out_shape, compiler_params=_cp("parallel"))(*ins)
    return out if has else out[0]


def _norm_bwd(x, g, sc, sh, dh, dres, mix=None, gt=None, *, name, ts=256):
    s, d = x.shape
    has = mix is not None

    def body(*refs):
        if has:
            x_ref, g_ref, sc_ref, sh_ref, dh_ref, dres_ref, mix_ref, gt_ref, dx_ref, dmix_ref, part_ref = refs
        else:
            x_ref, g_ref, sc_ref, sh_ref, dh_ref, dres_ref, dx_ref, part_ref = refs
        i = pl.program_id(0)
        xv = x_ref[...]
        gv = g_ref[...]
        dhv = dh_ref[...]
        r = lax.rsqrt(jnp.mean(xv * xv, axis=-1, keepdims=True) + EPS)
        xh = xv * r
        dyn = dhv * (1.0 + sc_ref[...])
        gy = dyn * gv
        dx = dres_ref[...] + r * (gy - xh * jnp.mean(gy * xh, axis=-1, keepdims=True))
        dx_ref[...] = dx

        @pl.when(i == 0)
        def _():
            part_ref[...] = jnp.zeros_like(part_ref)

        part_ref[0:1, :] += jnp.sum(dhv * (xh * gv), axis=0, keepdims=True)
        part_ref[1:2, :] += jnp.sum(dhv, axis=0, keepdims=True)
        part_ref[2:3, :] += jnp.sum(dyn * xh, axis=0, keepdims=True)
        if has:
            part_ref[3:4, :] += jnp.sum(dx * mix_ref[...], axis=0, keepdims=True)
            dmix_ref[...] = (gt_ref[...] * dx).astype(BF)

    ins = [x, g, sc, sh, dh, dres] + ([mix, gt] if has else [])
    in_specs = [_row(ts, d), _fix((1, d)), _fix((1, d)), _fix((1, d)), _row(ts, d), _row(ts, d)]
    in_specs += [_row(ts, d), _fix((1, d))] if has else []
    out_shape = [jax.ShapeDtypeStruct((s, d), F32)] + ([jax.ShapeDtypeStruct((s, d), BF)] if has else [])
    out_shape += [jax.ShapeDtypeStruct((8, d), F32)]
    out_specs = [_row(ts, d)] + ([_row(ts, d)] if has else []) + [_fix((8, d))]
    return pl.pallas_call(body, name=name, grid=(s // ts,), in_specs=in_specs, out_specs=out_specs,
                          out_shape=out_shape, compiler_params=_cp("arbitrary"))(*ins)


def _final(x1, ffn, gt2, gf, tgt, *, ts=256):
    s, d = x1.shape

    def body(x1_ref, ffn_ref, gt_ref, g_ref, t_ref, dx_ref, dffn_ref, part_ref):
        i = pl.program_id(0)
        fv = ffn_ref[...]
        gv = g_ref[...]
        xv = x1_ref[...] + gt_ref[...] * fv
        r = lax.rsqrt(jnp.mean(xv * xv, axis=-1, keepdims=True) + EPS)
        xh = xv * r
        e = xh * gv - t_ref[...]
        dy = e * (1.0 / d)
        gy = dy * gv
        dx = r * (gy - xh * jnp.mean(gy * xh, axis=-1, keepdims=True))
        dx_ref[...] = dx
        dffn_ref[...] = (gt_ref[...] * dx).astype(BF)

        @pl.when(i == 0)
        def _():
            part_ref[...] = jnp.zeros_like(part_ref)

        part_ref[0:1, :] += jnp.sum(dy * xh, axis=0, keepdims=True)
        part_ref[1:2, :] += jnp.sum(dx * fv, axis=0, keepdims=True)
        part_ref[2:3, :] += jnp.sum(e * e, axis=0, keepdims=True) * (0.5 / d)

    return pl.pallas_call(
        body, name="final_loss", grid=(s // ts,),
        in_specs=[_row(ts, d), _row(ts, d), _fix((1, d)), _fix((1, d)), _row(ts, d)],
        out_specs=[_row(ts, d), _row(ts, d), _fix((8, d))],
        out_shape=[jax.ShapeDtypeStruct((s, d), F32), jax.ShapeDtypeStruct((s, d), BF),
                   jax.ShapeDtypeStruct((8, d), F32)],
        compiler_params=_cp("arbitrary"))(x1, ffn, gt2, gf, tgt)


def _swiglu_fwd(gu, *, ts=256):
    s = gu.shape[0]

    def body(g_ref, u_ref, o_ref):
        gv = g_ref[...].astype(F32)
        o_ref[...] = (gv * jax.nn.sigmoid(gv) * u_ref[...].astype(F32)).astype(BF)

    return pl.pallas_call(body, name="swiglu_fwd", grid=(s // ts,),
                          in_specs=[_row(ts, FF, 0), _row(ts, FF, 1)], out_specs=_row(ts, FF),
                          out_shape=jax.ShapeDtypeStruct((s, FF), BF), compiler_params=_cp("parallel"))(gu, gu)


def _swiglu_bwd(gu, dact, *, ts=256):
    s = gu.shape[0]

    def body(g_ref, u_ref, da_ref, o_ref):
        gv = g_ref[...].astype(F32)
        uv = u_ref[...].astype(F32)
        dav = da_ref[...].astype(F32)
        sg = jax.nn.sigmoid(gv)
        o_ref[:, 0:FF] = (dav * uv * _silu_grad(gv, sg)).astype(BF)
        o_ref[:, FF:2 * FF] = (dav * gv * sg).astype(BF)

    return pl.pallas_call(body, name="swiglu_bwd", grid=(s // ts,),
                          in_specs=[_row(ts, FF, 0), _row(ts, FF, 1), _row(ts, FF)], out_specs=_row(ts, 2 * FF),
                          out_shape=jax.ShapeDtypeStruct((s, 2 * FF), BF),
                          compiler_params=_cp("parallel"))(gu, gu, dact)


def _mixout_fwd(att, y, proj, g_att, g_ssd, *, ts=256):
    s = att.shape[0]

    def body(a_ref, y_ref, z_ref, ga_ref, gs_ref, o_ref):
        av = a_ref[...]
        ra = lax.rsqrt(jnp.mean(av * av, axis=-1, keepdims=True) + EPS)
        o_ref[:, 0:AW] = (av * ra * ga_ref[...]).astype(BF)
        zv = z_ref[...]
        yz = y_ref[...] * (zv * jax.nn.sigmoid(zv))
        rs = lax.rsqrt(jnp.mean(yz * yz, axis=-1, keepdims=True) + EPS)
        o_ref[:, AW:AW + SW] = (yz * rs * gs_ref[...]).astype(BF)

    return pl.pallas_call(body, name="mixout_fwd", grid=(s // ts,),
                          in_specs=[_row(ts, AW), _row(ts, SW), _row(ts, SW, 3), _fix((1, AW)), _fix((1, SW))],
                          out_specs=_row(ts, AW + SW), out_shape=jax.ShapeDtypeStruct((s, AW + SW), BF),
                          compiler_params=_cp("parallel"))(att, y, proj, g_att, g_ssd)


def _mixout_bwd(dcat, att, y, proj, g_att, g_ssd, *, ts=256):
    s = att.shape[0]

    def body(dc_ref, a_ref, y_ref, z_ref, ga_ref, gs_ref, da_ref, dy_ref, dz_ref, part_ref):
        i = pl.program_id(0)
        av = a_ref[...]
        dca = dc_ref[:, 0:AW]
        ra = lax.rsqrt(jnp.mean(av * av, axis=-1, keepdims=True) + EPS)
        ah = av * ra
        gy = dca * ga_ref[...]
        da_ref[...] = ra * (gy - ah * jnp.mean(gy * ah, axis=-1, keepdims=True))
        zv = z_ref[...]
        yv = y_ref[...]
        sg = jax.nn.sigmoid(zv)
        sz = zv * sg
        yz = yv * sz
        dcs = dc_ref[:, AW:AW + SW]
        rs = lax.rsqrt(jnp.mean(yz * yz, axis=-1, keepdims=True) + EPS)
        yh = yz * rs
        gys = dcs * gs_ref[...]
        dyz = rs * (gys - yh * jnp.mean(gys * yh, axis=-1, keepdims=True))
        dy_ref[...] = dyz * sz
        dz_ref[...] = (dyz * yv * _silu_grad(zv, sg)).astype(BF)

        @pl.when(i == 0)
        def _():
            part_ref[...] = jnp.zeros_like(part_ref)

        part_ref[0:1, :] += jnp.sum(dca * ah, axis=0, keepdims=True)
        part_ref[1:2, :] += jnp.sum(dcs * yh, axis=0, keepdims=True)

    return pl.pallas_call(
        body, name="mixout_bwd", grid=(s // ts,),
        in_specs=[_row(ts, AW + SW), _row(ts, AW), _row(ts, SW), _row(ts, SW, 3), _fix((1, AW)), _fix((1, SW))],
        out_specs=[_row(ts, AW), _row(ts, SW), _row(ts, SW), _fix((8, AW))],
        out_shape=[jax.ShapeDtypeStruct((s, AW), F32), jax.ShapeDtypeStruct((s, SW), F32),
                   jax.ShapeDtypeStruct((s, SW), BF), jax.ShapeDtypeStruct((8, AW), F32)],
        compiler_params=_cp("arbitrary"))(dcat, att, y, proj, g_att, g_ssd)


TQ = 512
SCALE = HD ** -0.5


def _attn_fwd(qkv, bias):
    s = qkv.shape[0]
    nb = s // TQ
    prev = lambda i: jnp.maximum(i - 1, 0)

    def body(q_ref, kp_ref, kc_ref, vp_ref, vc_ref, b_ref, o_ref, kbuf, vbuf):
        i = pl.program_id(0)
        kbuf[0:TQ, :] = kp_ref[...]
        kbuf[TQ:2 * TQ, :] = kc_ref[...]
        vbuf[0:TQ, :] = vp_ref[...]
        vbuf[TQ:2 * TQ, :] = vc_ref[...]

        def chunk(c, carry):
            r0 = pl.multiple_of(c * CH, CH)
            kpos = i * TQ + r0 - TQ + lax.broadcasted_iota(jnp.int32, (1, BAND), 1)
            valid = kpos >= 0
            for h in range(NH):
                sl = slice(HD * h, HD * h + HD)
                q = q_ref[pl.ds(r0, CH), sl]
                k = kbuf[pl.ds(r0, BAND), sl]
                v = vbuf[pl.ds(r0, BAND), sl]
                sc = lax.dot_general(q, k, (((1,), (1,)), ((), ())), preferred_element_type=F32) * SCALE + b_ref[h]
                sc = jnp.where(valid, sc, -jnp.inf)
                e = jnp.exp(sc - jnp.max(sc, axis=-1, keepdims=True))
                p = e / jnp.sum(e, axis=-1, keepdims=True)
                o_ref[pl.ds(r0, CH), sl] = jnp.dot(p.astype(BF), v, preferred_element_type=F32)
            return carry

        lax.fori_loop(0, TQ // CH, chunk, 0)

    blk = lambda col, im: pl.BlockSpec((TQ, AW), lambda i, _c=col, _f=im: (_f(i), _c))
    cur = lambda i: i
    return pl.pallas_call(
        body, name="attn_fwd", grid=(nb,),
        in_specs=[blk(0, cur), blk(1, prev), blk(1, cur), blk(2, prev), blk(2, cur), _fix((NH, CH, BAND))],
        out_specs=_row(TQ, AW), out_shape=jax.ShapeDtypeStruct((s, AW), F32),
        scratch_shapes=[pltpu.VMEM((2 * TQ, AW), BF), pltpu.VMEM((2 * TQ, AW), BF)],
        compiler_params=_cp("parallel"))(qkv, qkv, qkv, qkv, qkv, bias)


def _attn_bwd(qkv, datt, bias_t):
    s = qkv.shape[0]
    nb = s // TQ
    cur = lambda i: jnp.minimum(i, nb - 1)
    prev = lambda i: jnp.maximum(jnp.minimum(i, nb - 1) - 1, 0)
    late = lambda i: jnp.maximum(i - 1, 0)

    def body(q_ref, kp_ref, kc_ref, vp_ref, vc_ref, do_ref, b_ref, dq_ref, dk_ref, dv_ref, db_ref,
             kbuf, vbuf, dkacc, dvacc):
        i = pl.program_id(0)

        @pl.when(i == 0)
        def _():
            dkacc[...] = jnp.zeros_like(dkacc)
            dvacc[...] = jnp.zeros_like(dvacc)
            db_ref[...] = jnp.zeros_like(db_ref)

        @pl.when(i < nb)
        def _():
            kbuf[0:TQ, :] = kp_ref[...]
            kbuf[TQ:2 * TQ, :] = kc_ref[...]
            vbuf[0:TQ, :] = vp_ref[...]
            vbuf[TQ:2 * TQ, :] = vc_ref[...]

            def chunk(c, carry):
                r0 = pl.multiple_of(c * CH, CH)
                kpos = i * TQ + r0 - TQ + lax.broadcasted_iota(jnp.int32, (BAND, 1), 0)
                valid = kpos >= 0
                for h in range(NH):
                    sl = slice(HD * h, HD * h + HD)
                    q = q_ref[pl.ds(r0, CH), sl]
                    k = kbuf[pl.ds(r0, BAND), sl]
                    v = vbuf[pl.ds(r0, BAND), sl]
                    do = do_ref[pl.ds(r0, CH), sl].astype(BF)
                    st = lax.dot_general(k, q, (((1,), (1,)), ((), ())), preferred_element_type=F32) * SCALE + b_ref[h]
                    st = jnp.where(valid, st, -jnp.inf)
                    e = jnp.exp(st - jnp.max(st, axis=0, keepdims=True))
                    pt = e / jnp.sum(e, axis=0, keepdims=True)
                    dvacc[pl.ds(r0, BAND), sl] += jnp.dot(pt.astype(BF), do, preferred_element_type=F32)
                    dpt = lax.dot_general(v, do, (((1,), (1,)), ((), ())), preferred_element_type=F32)
                    dst = pt * (dpt - jnp.sum(dpt * pt, axis=0, keepdims=True))
                    db_ref[h] += dst
                    dsb = dst.astype(BF)
                    dkacc[pl.ds(r0, BAND), sl] += jnp.dot(dsb, q, preferred_element_type=F32) * SCALE
                    dq = lax.dot_general(dsb, k, (((0,), (0,)), ((), ())), preferred_element_type=F32) * SCALE
                    dq_ref[pl.ds(r0, CH), sl] = dq.astype(BF)
                return carry

            lax.fori_loop(0, TQ // CH, chunk, 0)

        dk_ref[...] = dkacc[0:TQ, :].astype(BF)
        dv_ref[...] = dvacc[0:TQ, :].astype(BF)
        dkacc[0:TQ, :] = dkacc[TQ:2 * TQ, :]
        dvacc[0:TQ, :] = dvacc[TQ:2 * TQ, :]
        dkacc[TQ:2 * TQ, :] = jnp.zeros((TQ, AW), F32)
        dvacc[TQ:2 * TQ, :] = jnp.zeros((TQ, AW), F32)

    blk = lambda col, im: pl.BlockSpec((TQ, AW), lambda i, _c=col, _f=im: (_f(i), _c))
    return pl.pallas_call(
        body, name="attn_bwd", grid=(nb + 1,),
        in_specs=[blk(0, cur), blk(1, prev), blk(1, cur), blk(2, prev), blk(2, cur), blk(0, cur),
                  _fix((NH, BAND, CH))],
        out_specs=[blk(0, cur), blk(0, late), blk(0, late), _fix((NH, BAND, CH))],
        out_shape=[jax.ShapeDtypeStruct((s, AW), BF)] * 3 + [jax.ShapeDtypeStruct((NH, BAND, CH), F32)],
        scratch_shapes=[pltpu.VMEM((2 * TQ, AW), BF), pltpu.VMEM((2 * TQ, AW), BF),
                        pltpu.VMEM((2 * TQ, AW), F32), pltpu.VMEM((2 * TQ, AW), F32)],
        compiler_params=_cp("arbitrary"))(qkv, qkv, qkv, qkv, qkv, datt, bias_t)


QB = 256
NQC = QB // CH
WIN = 3 * QB


def _bias_window(bias):
    rows = [jnp.pad(bias, ((0, 0), (0, 0), (CH * a, WIN - BAND - CH * a)), constant_values=-jnp.inf)
            for a in range(NQC)]
    return jnp.concatenate(rows, axis=1)


def _win_specs(width, nb):
    def mk(col, back):
        return pl.BlockSpec((QB, width), lambda i, _c=col, _k=back: (jnp.maximum(jnp.minimum(i, nb - 1) - _k, 0), _c))
    return [mk(1, 2), mk(1, 1), mk(1, 0), mk(2, 2), mk(2, 1), mk(2, 0)]


def _attn2_fwd(qkv, bias_w):
    s = qkv.shape[0]
    nb = s // QB

    def body(q_ref, k0, k1, k2, v0, v1, v2, b_ref, o_ref, kbuf, vbuf):
        i = pl.program_id(0)
        for j, (kr, vr) in enumerate(((k0, v0), (k1, v1), (k2, v2))):
            kbuf[QB * j:QB * j + QB, :] = kr[...]
            vbuf[QB * j:QB * j + QB, :] = vr[...]
        valid = (i * QB - 2 * QB + lax.broadcasted_iota(jnp.int32, (1, WIN), 1)) >= 0
        for h in range(NH):
            sl = slice(HD * h, HD * h + HD)
            sc = lax.dot_general(q_ref[:, sl], kbuf[:, sl], (((1,), (1,)), ((), ())), preferred_element_type=F32)
            sc = jnp.where(valid, sc * SCALE + b_ref[h], -jnp.inf)
            e = jnp.exp(sc - jnp.max(sc, axis=-1, keepdims=True))
            p = e * (1.0 / jnp.sum(e, axis=-1, keepdims=True))
            o_ref[:, sl] = jnp.dot(p.astype(BF), vbuf[:, sl], preferred_element_type=F32)

    return pl.pallas_call(
        body, name="attn_fwd", grid=(nb,),
        in_specs=[_row(QB, AW)] + _win_specs(AW, nb) + [_fix((NH, QB, WIN), single=True)],
        out_specs=_row(QB, AW), out_shape=jax.ShapeDtypeStruct((s, AW), F32),
        scratch_shapes=[pltpu.VMEM((WIN, AW), BF), pltpu.VMEM((WIN, AW), BF)],
        compiler_params=_cp("parallel"))(qkv, qkv, qkv, qkv, qkv, qkv, qkv, bias_w)


def _attn2_bwd(qkv, datt, bias_wt):
    s = qkv.shape[0]
    nb = s // QB
    cur = lambda i: jnp.minimum(i, nb - 1)
    late = lambda i: jnp.maximum(i - 2, 0)

    def body(q_ref, k0, k1, k2, v0, v1, v2, do_ref, b_ref, dq_ref, dk_ref, dv_ref, db_ref,
             kbuf, vbuf, dkacc, dvacc, dsbuf):
        i = pl.program_id(0)

        @pl.when(i == 0)
        def _():
            dkacc[...] = jnp.zeros_like(dkacc)
            dvacc[...] = jnp.zeros_like(dvacc)
            db_ref[...] = jnp.zeros_like(db_ref)

        @pl.when(i < nb)
        def _():
            for j, (kr, vr) in enumerate(((k0, v0), (k1, v1), (k2, v2))):
                kbuf[QB * j:QB * j + QB, :] = kr[...]
                vbuf[QB * j:QB * j + QB, :] = vr[...]
            valid = (i * QB - 2 * QB + lax.broadcasted_iota(jnp.int32, (WIN, 1), 0)) >= 0
            for h in range(NH):
                sl = slice(HD * h, HD * h + HD)
                q = q_ref[:, sl]
                k = kbuf[:, sl]
                do = do_ref[:, sl].astype(BF)
                st = lax.dot_general(k, q, (((1,), (1,)), ((), ())), preferred_element_type=F32)
                st = jnp.where(valid, st * SCALE + b_ref[h], -jnp.inf)
                e = jnp.exp(st - jnp.max(st, axis=0, keepdims=True))
                pt = e * (1.0 / jnp.sum(e, axis=0, keepdims=True))
                dvacc[:, sl] += jnp.dot(pt.astype(BF), do, preferred_element_type=F32)
                dpt = lax.dot_general(vbuf[:, sl], do, (((1,), (1,)), ((), ())), preferred_element_type=F32)
                dst = pt * (dpt - jnp.sum(dpt * pt, axis=0, keepdims=True))
                dsbuf[...] = dst
                fold = dsbuf[0:BAND, 0:CH]
                for a in range(1, NQC):
                    fold = fold + dsbuf[CH * a:CH * a + BAND, CH * a:CH * a + CH]
                db_ref[h] += fold
                dsb = dst.astype(BF)
                dkacc[:, sl] += jnp.dot(dsb, q, preferred_element_type=F32) * SCALE
                dq = lax.dot_general(dsb, k, (((0,), (0,)), ((), ())), preferred_element_type=F32) * SCALE
                dq_ref[:, sl] = dq.astype(BF)

        dk_ref[...] = dkacc[0:QB, :].astype(BF)
        dv_ref[...] = dvacc[0:QB, :].astype(BF)
        for acc in (dkacc, dvacc):
            acc[0:QB, :] = acc[QB:2 * QB, :]
            acc[QB:2 * QB, :] = acc[2 * QB:WIN, :]
            acc[2 * QB:WIN, :] = jnp.zeros((QB, AW), F32)

    qspec = pl.BlockSpec((QB, AW), lambda i: (cur(i), 0))
    lspec = pl.BlockSpec((QB, AW), lambda i: (late(i), 0))
    return pl.pallas_call(
        body, name="attn_bwd", grid=(nb + 2,),
        in_specs=[qspec] + _win_specs(AW, nb) + [qspec, _fix((NH, WIN, QB), single=True)],
        out_specs=[qspec, lspec, lspec, _fix((NH, BAND, CH))],
        out_shape=[jax.ShapeDtypeStruct((s, AW), BF)] * 3 + [jax.ShapeDtypeStruct((NH, BAND, CH), F32)],
        scratch_shapes=[pltpu.VMEM((WIN, AW), BF), pltpu.VMEM((WIN, AW), BF),
                        pltpu.VMEM((WIN, AW), F32), pltpu.VMEM((WIN, AW), F32), pltpu.VMEM((WIN, QB), F32)],
        compiler_params=_cp("arbitrary"))(qkv, qkv, qkv, qkv, qkv, qkv, qkv, datt, bias_wt)


TB = 512
SCH = 256
NCB = TB // SCH
GW = 8 * HD


def _tri(lower):
    r = lax.broadcasted_iota(jnp.int32, (SCH, SCH), 0)
    c = lax.broadcasted_iota(jnp.int32, (SCH, SCH), 1)
    return r >= c if lower else r <= c


def _ssd_fwd(proj, conv_w, conv_b, dt_bias, a_log, d_skip):
    s = proj.shape[0]
    nb = s // TB

    def body(xs_ref, bc_ref, dt_ref, cw_ref, cb_ref, dtb_ref, al_ref, dsk_ref, y_ref, pre_ref, prev_ref,
             xpad, ubuf, dtbuf, csb, cstb, hst, xdec, yoffb, drow):
        i = pl.program_id(0)

        @pl.when(i == 0)
        def _():
            xpad[0:8, :] = jnp.zeros((8, XBC), F32)
            hst[...] = jnp.zeros_like(hst)

        xpad[8:8 + TB, 0:SW] = xs_ref[...]
        xpad[8:8 + TB, SW:XBC] = bc_ref[...]
        pre = cb_ref[...]
        for kk in range(4):
            pre = pre + cw_ref[kk:kk + 1, :] * xpad[5 + kk:5 + kk + TB, :]
        pre_ref[...] = pre
        ubuf[...] = pre * jax.nn.sigmoid(pre)
        xpad[0:8, :] = xpad[TB:TB + 8, :]
        dtbuf[...] = _softplus(dt_ref[...] + dtb_ref[...])
        a = -jnp.exp(al_ref[...])
        tri = _tri(True).astype(F32)
        causal = _tri(True)

        def chunk(c, carry):
            r0 = pl.multiple_of(c * SCH, SCH)
            rows = pl.ds(r0, SCH)
            cs = jnp.dot(tri, dtbuf[rows, :] * a, precision=HI, preferred_element_type=F32)
            csb[...] = cs
            cstb[...] = cs.T
            for g in range(NG):
                bg = ubuf[rows, SW + NS * g:SW + NS * g + NS]
                cg = ubuf[rows, SW + NG * NS + NS * g:SW + NG * NS + NS * g + NS].astype(BF)
                cb = lax.dot_general(cg, bg.astype(BF), (((1,), (1,)), ((), ())), preferred_element_type=F32)
                hg = hst[g]
                prev_ref[c, g] = hg
                yoffb[...] = jnp.dot(cg, hg.astype(BF), preferred_element_type=F32)
                for r in range(8):
                    h = 8 * g + r
                    sl = slice(HD * h, HD * h + HD)
                    rs = slice(HD * r, HD * r + HD)
                    cs_h = csb[:, h:h + 1]
                    cl = csb[SCH - 1:SCH, h:h + 1]
                    seg = jnp.exp(jnp.where(causal, cs_h - cstb[h:h + 1, :], -jnp.inf))
                    xh = ubuf[rows, sl]
                    xdt = xh * dtbuf[rows, h:h + 1]
                    yd = jnp.dot((cb * seg).astype(BF), xdt.astype(BF), preferred_element_type=F32)
                    y_ref[rows, sl] = yd + jnp.exp(cs_h) * yoffb[:, rs] + xh * dsk_ref[:, h:h + 1]
                    xdec[:, rs] = xdt * jnp.exp(cl - cs_h)
                    drow[:, rs] = jnp.broadcast_to(jnp.exp(cl), (1, HD))
                st = jnp.dot(bg.T.astype(BF), xdec[...].astype(BF), preferred_element_type=F32)
                hst[g] = hg * drow[...] + st
            return carry

        lax.fori_loop(0, NCB, chunk, 0)

    return pl.pallas_call(
        body, name="ssd_fwd", grid=(nb,),
        in_specs=[_row(TB, SW, 4), _row(TB, 512, 10), _row(TB, LANE, 44), _fix((4, XBC)), _fix((1, XBC)),
                  _fix((1, LANE)), _fix((1, LANE)), _fix((1, LANE))],
        out_specs=[_row(TB, SW), _row(TB, XBC), pl.BlockSpec((NCB, NG, NS, GW), lambda i: (i, 0, 0, 0))],
        out_shape=[jax.ShapeDtypeStruct((s, SW), F32), jax.ShapeDtypeStruct((s, XBC), F32),
                   jax.ShapeDtypeStruct((s // SCH, NG, NS, GW), F32)],
        scratch_shapes=[pltpu.VMEM((TB + 8, XBC), F32), pltpu.VMEM((TB, XBC), F32), pltpu.VMEM((TB, LANE), F32),
                        pltpu.VMEM((SCH, LANE), F32), pltpu.VMEM((LANE, SCH), F32), pltpu.VMEM((NG, NS, GW), F32),
                        pltpu.VMEM((SCH, GW), F32), pltpu.VMEM((SCH, GW), F32), pltpu.VMEM((1, GW), F32)],
        compiler_params=_cp("arbitrary"))(proj, proj, proj, conv_w, conv_b, dt_bias, a_log, d_skip)


def _ssd_bwd(dy, pre, proj, prev, conv_w, dt_bias, a_log, d_skip):
    s = dy.shape[0]
    nb = s // TB
    rev = lambda i: nb - 1 - i
    halo = lambda i: jnp.maximum((nb - 1 - i) * (TB // 8) - 1, 0)

    def body(dy_ref, pre_ref, dt_ref, prev_ref, xs_ref, bc_ref, xsh_ref, bch_ref, cw_ref, dtb_ref, al_ref, dsk_ref,
             dx_ref, ddt_ref, pw_ref, ph_ref,
             ubuf, dtbuf, dpad, xpad, csb, cstb, dhs, yoffb, ebuf, xdecb, wbuf, drow, dub):
        i = pl.program_id(0)

        @pl.when(i == 0)
        def _():
            dhs[...] = jnp.zeros_like(dhs)
            dpad[TB:TB + 8, :] = jnp.zeros((8, XBC), F32)
            pw_ref[...] = jnp.zeros_like(pw_ref)
            ph_ref[...] = jnp.zeros_like(ph_ref)

        pre = pre_ref[...]
        ubuf[...] = pre * jax.nn.sigmoid(pre)
        dtbuf[...] = _softplus(dt_ref[...] + dtb_ref[...])
        a = -jnp.exp(al_ref[...])
        tri = _tri(True).astype(F32)
        trit = _tri(False).astype(F32)
        causal = _tri(True)
        upper = _tri(False)
        lane = lax.broadcasted_iota(jnp.int32, (SCH, LANE), 1)
        lane1 = lax.broadcasted_iota(jnp.int32, (1, LANE), 1)
        lastrow = lax.broadcasted_iota(jnp.int32, (SCH, LANE), 0) == SCH - 1

        def chunk(cc, carry):
            c = NCB - 1 - cc
            r0 = pl.multiple_of(c * SCH, SCH)
            rows = pl.ds(r0, SCH)
            dtc = dtbuf[rows, :]
            cs = jnp.dot(tri, dtc * a, precision=HI, preferred_element_type=F32)
            csb[...] = cs
            cstb[...] = cs.T
            dcs = jnp.zeros((SCH, LANE), F32)
            xr = jnp.zeros((SCH, LANE), F32)
            dlast = jnp.zeros((1, LANE), F32)
            dsk = jnp.zeros((1, LANE), F32)
            for g in range(NG):
                bg = ubuf[rows, SW + NS * g:SW + NS * g + NS]
                cg = ubuf[rows, SW + NG * NS + NS * g:SW + NG * NS + NS * g + NS]
                bb = bg.astype(BF)
                cbf = cg.astype(BF)
                cb = lax.dot_general(cbf, bb, (((1,), (1,)), ((), ())), preferred_element_type=F32)
                cbt = lax.dot_general(bb, cbf, (((1,), (1,)), ((), ())), preferred_element_type=F32)
                hp = prev_ref[c, g]
                hpb = hp.astype(BF)
                dhg = dhs[g]
                dhb = dhg.astype(BF)
                yoffb[...] = jnp.dot(cbf, hpb, preferred_element_type=F32)
                for r in range(8):
                    h = 8 * g + r
                    sl = slice(HD * h, HD * h + HD)
                    rs = slice(HD * r, HD * r + HD)
                    cs_h = csb[:, h:h + 1]
                    cl = csb[SCH - 1:SCH, h:h + 1]
                    dyh = dy_ref[rows, sl]
                    ecs = jnp.exp(cs_h)
                    ebuf[:, rs] = dyh * ecs
                    xdecb[:, rs] = ubuf[rows, sl] * dtbuf[rows, h:h + 1] * jnp.exp(cl - cs_h)
                    t1 = jnp.sum(dyh * ecs * yoffb[:, rs], axis=1, keepdims=True)
                    dcs = dcs + jnp.where(lane == h, t1, 0.0)
                    drow[:, rs] = jnp.broadcast_to(jnp.exp(cl), (1, HD))
                eb = ebuf[...].astype(BF)
                dc = lax.dot_general(eb, hpb, (((1,), (1,)), ((), ())), preferred_element_type=F32)
                dprev = jnp.dot(cg.T.astype(BF), eb, preferred_element_type=F32)
                wbuf[...] = jnp.dot(bb, dhb, preferred_element_type=F32)
                db = lax.dot_general(xdecb[...].astype(BF), dhb, (((1,), (1,)), ((), ())), preferred_element_type=F32)
                dcbs = jnp.zeros((SCH, SCH), F32)
                dcbts = jnp.zeros((SCH, SCH), F32)
                for r in range(8):
                    h = 8 * g + r
                    sl = slice(HD * h, HD * h + HD)
                    rs = slice(HD * r, HD * r + HD)
                    cs_h = csb[:, h:h + 1]
                    cst_h = cstb[h:h + 1, :]
                    cl = csb[SCH - 1:SCH, h:h + 1]
                    seg = jnp.exp(jnp.where(causal, cs_h - cst_h, -jnp.inf))
                    segt = jnp.exp(jnp.where(upper, cst_h - cs_h, -jnp.inf))
                    dyh = dy_ref[rows, sl]
                    xh = ubuf[rows, sl]
                    dt_h = dtbuf[rows, h:h + 1]
                    xdt = xh * dt_h
                    dyb = dyh.astype(BF)
                    xdb = xdt.astype(BF)
                    dm = lax.dot_general(dyb, xdb, (((1,), (1,)), ((), ())), preferred_element_type=F32)
                    dmt = lax.dot_general(xdb, dyb, (((1,), (1,)), ((), ())), preferred_element_type=F32)
                    mt = cbt * segt
                    dxdt = jnp.dot(mt.astype(BF), dyb, preferred_element_type=F32)
                    dcs_h = (jnp.sum(dm * (cb * seg), axis=1, keepdims=True)
                             - jnp.sum(dmt * mt, axis=1, keepdims=True))
                    dcbs = dcbs + dm * seg
                    dcbts = dcbts + dmt * segt
                    dec = jnp.exp(cl - cs_h)
                    w_h = wbuf[:, rs]
                    dxdt = dxdt + dec * w_h
                    t = jnp.sum(w_h * xdt, axis=1, keepdims=True) * dec
                    dcs_h = dcs_h - t
                    dch = jnp.sum(jnp.sum(dhs[g, :, rs] * prev_ref[c, g, :, rs], axis=1, keepdims=True),
                                  axis=0, keepdims=True)
                    dl = jnp.sum(t, axis=0, keepdims=True) + dch * jnp.exp(cl)
                    dcs = dcs + jnp.where(lane == h, dcs_h, 0.0)
                    dlast = dlast + jnp.where(lane1 == h, dl, 0.0)
                    xr = xr + jnp.where(lane == h, jnp.sum(dxdt * xh, axis=1, keepdims=True), 0.0)
                    dskh = jnp.sum(jnp.sum(dyh * xh, axis=1, keepdims=True), axis=0, keepdims=True)
                    dsk = dsk + jnp.where(lane1 == h, dskh, 0.0)
                    dub[:, sl] = dyh * dsk_ref[:, h:h + 1] + dxdt * dt_h
                dc = dc + jnp.dot(dcbs.astype(BF), bb, preferred_element_type=F32)
                db = db + jnp.dot(dcbts.astype(BF), cbf, preferred_element_type=F32)
                dub[:, SW + NS * g:SW + NS * g + NS] = db
                dub[:, SW + NG * NS + NS * g:SW + NG * NS + NS * g + NS] = dc
                dhs[g] = dprev + drow[...] * dhg
            dcs = dcs + jnp.where(lastrow, dlast, 0.0)
            dadt = jnp.dot(trit, dcs, precision=HI, preferred_element_type=F32)
            ddt = dadt * a + xr
            ddtr = ddt * jax.nn.sigmoid(dt_ref[rows, :] + dtb_ref[...])
            ddt_ref[rows, :] = ddtr.astype(BF)
            ph_ref[0:1, :] += jnp.sum(ddtr, axis=0, keepdims=True)
            ph_ref[1:2, :] += jnp.sum(dadt * dtc, axis=0, keepdims=True) * a
            ph_ref[2:3, :] += dsk
            pc = pre_ref[rows, :]
            sg = jax.nn.sigmoid(pc)
            dpad[rows, :] = dub[...] * _silu_grad(pc, sg)
            return carry

        lax.fori_loop(0, NCB, chunk, 0)

        dxb = cw_ref[0:1, :] * dpad[3:3 + TB, :]
        for kk in range(1, 4):
            dxb = dxb + cw_ref[kk:kk + 1, :] * dpad[3 - kk:3 - kk + TB, :]
        dx_ref[...] = dxb.astype(BF)
        keep = jnp.where(i < nb - 1, 1.0, 0.0)
        xpad[0:8, 0:SW] = xsh_ref[...] * keep
        xpad[0:8, SW:XBC] = bch_ref[...] * keep
        xpad[8:8 + TB, 0:SW] = xs_ref[...]
        xpad[8:8 + TB, SW:XBC] = bc_ref[...]
        dp = dpad[0:TB, :]
        for kk in range(4):
            pw_ref[kk:kk + 1, :] += jnp.sum(dp * xpad[5 + kk:5 + kk + TB, :], axis=0, keepdims=True)
        pw_ref[4:5, :] += jnp.sum(dp, axis=0, keepdims=True)
        dpad[TB:TB + 8, :] = dpad[0:8, :]

    rblk = lambda w, col: pl.BlockSpec((TB, w), lambda i, _c=col: (rev(i), _c))
    return pl.pallas_call(
        body, name="ssd_bwd", grid=(nb,),
        in_specs=[rblk(SW, 0), rblk(XBC, 0), rblk(LANE, 44),
                  pl.BlockSpec((NCB, NG, NS, GW), lambda i: (rev(i), 0, 0, 0)),
                  rblk(SW, 4), rblk(512, 10),
                  pl.BlockSpec((8, SW), lambda i: (halo(i), 4)), pl.BlockSpec((8, 512), lambda i: (halo(i), 10)),
                  _fix((4, XBC)), _fix((1, LANE)), _fix((1, LANE)), _fix((1, LANE))],
        out_specs=[rblk(XBC, 0), rblk(LANE, 0), _fix((8, XBC)), _fix((8, LANE))],
        out_shape=[jax.ShapeDtypeStruct((s, XBC), BF), jax.ShapeDtypeStruct((s, LANE), BF),
                   jax.ShapeDtypeStruct((8, XBC), F32), jax.ShapeDtypeStruct((8, LANE), F32)],
        scratch_shapes=[pltpu.VMEM((TB, XBC), F32), pltpu.VMEM((TB, LANE), F32), pltpu.VMEM((TB + 8, XBC), F32),
                        pltpu.VMEM((TB + 8, XBC), F32), pltpu.VMEM((SCH, LANE), F32), pltpu.VMEM((LANE, SCH), F32),
                        pltpu.VMEM((NG, NS, GW), F32), pltpu.VMEM((SCH, GW), F32), pltpu.VMEM((SCH, GW), F32),
                        pltpu.VMEM((SCH, GW), F32), pltpu.VMEM((SCH, GW), F32), pltpu.VMEM((1, GW), F32),
                        pltpu.VMEM((SCH, XBC), F32)],
        compiler_params=_cp("arbitrary"))(dy, pre, proj, prev, proj, proj, proj, proj, conv_w, dt_bias, a_log, d_skip)


def _rel_index():
    q = jnp.arange(CH)[:, None] + (BAND - CH)
    k = jnp.arange(BAND)[None, :]
    return jnp.clip(q - k, -(CH - 1), REL_CLIP) + (CH - 1)


def _local_step(x, tgt, mods, first_weights, late_weights, grads_ready,
                g_mix, rel_bias, conv_w, conv_b, dt_bias, a_log, d_skip, g_att, g_ssd, g_ffn, g_final):
    sh1, sc1, gt1, sh2, sc2, gt2 = mods
    pad16 = lambda v: jnp.pad(v, ((0, 0), (0, LANE - NH)))
    dtb_p, al_p, dsk_p = pad16(dt_bias), pad16(a_log), pad16(d_skip)
    ridx = _rel_index()
    bias = rel_bias[:, ridx]
    bias_w = _bias_window(bias)
    bias_wt = jnp.swapaxes(bias_w, 1, 2)

    h1 = _norm_fwd(x, g_mix, sc1, sh1, name="norm1_fwd")
    w_in_p, w_in_t = first_weights(h1)
    proj = _mm(h1, w_in_p, F32, "in_proj", 1024, 1152, 2048)
    qkv = proj[:, :3 * AW].astype(BF)
    att = _attn2_fwd(qkv, bias_w)
    y, pre, prev = _ssd_fwd(proj, conv_w, conv_b, dtb_p, al_p, dsk_p)
    cat = _mixout_fwd(att, y, proj, g_att, g_ssd)
    w_out, w_out_t, w_gu, w_gu_t, w_down, w_down_t = late_weights(cat)
    mix = _mm(cat, w_out, F32, "out_proj", 1024, 1024, 2048)
    h2, x1 = _norm_fwd(x, g_ffn, sc2, sh2, mix, gt1, name="norm2_fwd")
    gu = _mm(h2, w_gu, BF, "gate_up", 1024, 1408, 2048)
    act = _swiglu_fwd(gu)
    ffn = _mm(act, w_down, F32, "down_proj", 1024, 1024, 1408)
    dx2, dffn, pf = _final(x1, ffn, gt2, g_final, tgt)

    dact = _mm(dffn, w_down_t, BF, "d_act", 1024, 1408, 2048)
    gw_down = _mm(act.T, dffn, F32, "gw_down", 1408, 1024, 1024)
    dgu = _swiglu_bwd(gu, dact)
    dh2 = _mm(dgu, w_gu_t, F32, "d_h2", 1024, 1024, 1408)
    gw_gu = _mm(h2.T, dgu, F32, "gw_gate_up", 1024, 1408, 1024)
    zero = grads_ready("ffn", gw_down, gw_gu)
    dx1, dmix, p2 = _norm_bwd(x1, g_ffn + zero, sc2, sh2, dh2, dx2, mix, gt1, name="norm2_bwd")
    gw_out = _mm(cat.T, dmix, F32, "gw_out", 1024, 1024, 2048)
    zero = grads_ready("out", gw_out)
    dcat = _mm(dmix, w_out_t, F32, "d_cat", 1024, 1024, 2048)
    datt, dy, dz, pm = _mixout_bwd(dcat, att, y, proj, g_att + zero, g_ssd)
    dq, dk, dv, dbias_t = _attn2_bwd(qkv, datt, bias_wt)
    dxbc, ddt, pw, ph = _ssd_bwd(dy, pre, proj, prev, conv_w, dtb_p, al_p, dsk_p)
    dproj = jnp.concatenate([dq, dk, dv, dz, dxbc, ddt], axis=1)
    gw_in = _mm(h1.T, dproj, F32, "gw_in", 1024, 1152, 1024)
    zero = grads_ready("in", gw_in)
    dh1 = _mm(dproj, w_in_t, F32, "d_h1", 1024, 1024, 1152)
    gx, p1 = _norm_bwd(x, g_mix + zero, sc1, sh1, dh1, dx1, name="norm1_bwd")

    onehot = (ridx.T.reshape(BAND * CH, 1) == jnp.arange(NREL)[None, :]).astype(F32)
    g_rel = _mm(dbias_t.reshape(NH, BAND * CH), onehot, F32, "g_rel_bias", NH, NREL, 4608, precision=HI)

    small = dict(
        dmods=jnp.concatenate([p1[1], p1[0], p2[3], p2[1], p2[0], pf[1]]),
        g_mix=p1[2], conv_b=pw[4], dt_bias=ph[0], a_log=ph[1], d_skip=ph[2], g_att=pm[0], g_ssd=pm[1],
        g_ffn=p2[2], g_final=pf[0], rel_bias=g_rel.reshape(-1), conv_w=pw[0:4].reshape(-1), loss=pf[2])
    return gx, small


def _exchange(srcs, gather, name):
    n = len(srcs)
    outs = [jax.ShapeDtypeStruct((NDEV,) + (s.shape if g else s.shape[1:]), s.dtype) for s, g in zip(srcs, gather)]
    outs.append(jax.ShapeDtypeStruct((8, LANE), F32))

    def body(*refs):
        src, dst, token = refs[:n], refs[n:2 * n], refs[2 * n]
        ssem, rsem, lsem = refs[2 * n + 1:]
        token[...] = jnp.zeros_like(token)
        x, y, c = lax.axis_index("x"), lax.axis_index("y"), lax.axis_index("c")
        me = 4 * x + 2 * y + c
        local = []
        for a in range(n):
            own = src[a] if gather[a] else src[a].at[me]
            cp = pltpu.make_async_copy(own, dst[a].at[me], lsem.at[a])
            cp.start()
            local.append(cp)
        remote = []
        for k in range(1, NDEV):
            px = 1 - x if k & 4 else x
            py = 1 - y if k & 2 else y
            pc = 1 - c if k & 1 else c
            pid = 4 * px + 2 * py + pc
            for a in range(n):
                s_ref = src[a] if gather[a] else src[a].at[pid]
                cp = pltpu.make_async_remote_copy(
                    src_ref=s_ref, dst_ref=dst[a].at[me], send_sem=ssem.at[a * (NDEV - 1) + k - 1],
                    recv_sem=rsem.at[a * (NDEV - 1) + k - 1], device_id=(px, py, pc), device_id_type=MESH)
                cp.start()
                remote.append(cp)
        for cp in remote:
            cp.wait()
        for cp in local:
            cp.wait()

    anyspec = pl.BlockSpec(memory_space=pl.ANY)
    return pl.pallas_call(
        body, name=name, out_shape=outs, in_specs=[anyspec] * n,
        out_specs=[anyspec] * n + [pl.BlockSpec(memory_space=pltpu.VMEM)],
        scratch_shapes=[pltpu.SemaphoreType.DMA((n * (NDEV - 1),)), pltpu.SemaphoreType.DMA((n * (NDEV - 1),)),
                        pltpu.SemaphoreType.DMA((n,))],
    )(*srcs)


_HBM = pl.BlockSpec(memory_space=pltpu.HBM)
_SEM = pl.BlockSpec(memory_space=pltpu.SEMAPHORE)
_EFFECT = pltpu.SideEffectType.DATAFLOW_SIDE_EFFECTING


def _peer_copies(src, land, ssem, rsem, gather):
    x, y, c = lax.axis_index("x"), lax.axis_index("y"), lax.axis_index("c")
    me = 4 * x + 2 * y + c
    copies = []
    for a in range(len(src)):
        for k in range(1, NDEV):
            px = 1 - x if k & 4 else x
            py = 1 - y if k & 2 else y
            pc = 1 - c if k & 1 else c
            s_ref = src[a] if gather[a] else src[a].at[4 * px + 2 * py + pc]
            idx = a * (NDEV - 1) + k - 1
            copies.append(pltpu.make_async_remote_copy(
                src_ref=s_ref, dst_ref=land[a].at[me], send_sem=ssem.at[idx], recv_sem=rsem.at[idx],
                device_id=(px, py, pc), device_id_type=MESH))
    return copies


def _xstart(srcs, gather, name):
    n = len(srcs)
    me = 4 * lax.axis_index("x") + 2 * lax.axis_index("y") + lax.axis_index("c")
    lands = []
    for s_, g_ in zip(srcs, gather):
        own = s_ if g_ else lax.dynamic_index_in_dim(s_, me, 0, keepdims=False)
        empty = lax.empty((NDEV,) + own.shape, own.dtype)
        lands.append(lax.dynamic_update_slice(empty, own[None], (me,) + (0,) * own.ndim))
    nsem = n * (NDEV - 1)

    def body(*refs):
        src, land, ssem, rsem, token = refs[:n], refs[n:2 * n], refs[2 * n], refs[2 * n + 1], refs[-1]
        for cp in _peer_copies(src, land, ssem, rsem, gather):
            cp.start()
        token[...] = jnp.zeros_like(token)

    ops = [pltpu.with_memory_space_constraint(a_, pltpu.HBM) for a_ in list(srcs) + lands]
    return pl.pallas_call(
        body, name=name,
        out_shape=(pltpu.SemaphoreType.DMA((nsem,)), pltpu.SemaphoreType.DMA((nsem,)),
                   *[pltpu.HBM(a_.shape, a_.dtype) for a_ in ops], jax.ShapeDtypeStruct((8, LANE), F32)),
        in_specs=[_HBM] * (2 * n),
        out_specs=(_SEM, _SEM, *[_HBM] * (2 * n), pl.BlockSpec(memory_space=pltpu.VMEM)),
        input_output_aliases={i: 2 + i for i in range(2 * n)},
        compiler_params=pltpu.CompilerParams(has_side_effects=_EFFECT),
    )(*ops)


def _xwait(started, gather, after, name):
    n = len(gather)
    ssem, rsem = started[0], started[1]
    bufs = started[2:2 + 2 * n]

    def body(*refs):
        src, land, ssem_, rsem_ = refs[:n], refs[n:2 * n], refs[2 * n], refs[2 * n + 1]
        for cp in _peer_copies(src, land, ssem_, rsem_, gather):
            cp.wait_send()
            cp.wait_recv()

    out = pl.pallas_call(
        body, name=name, out_shape=tuple(pltpu.HBM(a_.shape, a_.dtype) for a_ in bufs),
        in_specs=[_HBM] * (2 * n) + [_SEM, _SEM, pl.BlockSpec(memory_space=pl.ANY)],
        out_specs=[_HBM] * (2 * n), input_output_aliases={i: i for i in range(2 * n)},
        compiler_params=pltpu.CompilerParams(has_side_effects=_EFFECT),
    )(*bufs, ssem, rsem, after)
    return out[n:]


def _mods(c16, w_shard, b_shard, *, tn=512):
    n = w_shard.shape[1]

    def body(c_ref, w_ref, b_ref, o_ref):
        cv = c_ref[...]
        cond = (cv * jax.nn.sigmoid(cv)).astype(BF)
        o_ref[...] = jnp.dot(cond, w_ref[...].astype(BF), preferred_element_type=F32) + b_ref[...]

    return pl.pallas_call(body, name="ada_mods", grid=(n // tn,),
                          in_specs=[_fix((16, D)), pl.BlockSpec((D, tn), lambda j: (0, j)),
                                    pl.BlockSpec((1, tn), lambda j: (0, j))],
                          out_specs=pl.BlockSpec((16, tn), lambda j: (0, j)),
                          out_shape=jax.ShapeDtypeStruct((16, n), F32), compiler_params=_cp("parallel"))(c16, w_shard, b_shard)


def _gw_ada(c_t, dm, *, tr=256):
    n = dm.shape[1]

    def body(ct_ref, dm_ref, o_ref):
        acc = jnp.zeros((tr, n), F32)
        for b in range(NDEV):
            cv = ct_ref[:, b:b + 1]
            cond = (cv * jax.nn.sigmoid(cv)).astype(BF).astype(F32)
            acc = acc + cond * dm_ref[b:b + 1, :].astype(BF).astype(F32)
        o_ref[...] = acc

    return pl.pallas_call(body, name="gw_ada", grid=(D // tr,),
                          in_specs=[_row(tr, NDEV), _fix((NDEV, n))], out_specs=_row(tr, n),
                          out_shape=jax.ShapeDtypeStruct((D, n), F32), compiler_params=_cp("parallel"))(c_t, dm)


def _sum8(parts, *, name):
    _, r, c = parts.shape

    def body(p_ref, o_ref):
        acc = p_ref[0].astype(F32)
        for d in range(1, NDEV):
            acc = acc + p_ref[d].astype(F32)
        o_ref[...] = acc

    return pl.pallas_call(body, name=name, grid=(1,), in_specs=[_fix((NDEV, r, c))], out_specs=_fix((r, c)),
                          out_shape=jax.ShapeDtypeStruct((r, c), F32), compiler_params=_cp("arbitrary"))(parts)


def _adam(w, m, v, *, recv=None, grad=None, tr, name):
    r, c = w.shape
    summed = recv is not None
    assert r % tr == 0

    def body(g_ref, w_ref, m_ref, v_ref, go_ref, do_ref, mo_ref, vo_ref):
        if summed:
            g = g_ref[0].astype(F32)
            for d in range(1, NDEV):
                g = g + g_ref[d].astype(F32)
        else:
            g = g_ref[...]
        go_ref[...] = g
        mn = B1 * m_ref[...] + (1.0 - B1) * g
        vn = B2 * v_ref[...] + (1.0 - B2) * (g * g)
        mo_ref[...] = mn
        vo_ref[...] = vn
        mh = mn / (1.0 - B1 ** STEP)
        vh = vn / (1.0 - B2 ** STEP)
        do_ref[...] = -LR * (mh / (jnp.sqrt(vh) + AEPS) + WD * w_ref[...])

    gspec = pl.BlockSpec((NDEV, tr, c), lambda i: (0, i, 0)) if summed else _row(tr, c)
    return pl.pallas_call(body, name=name, grid=(r // tr,),
                          in_specs=[gspec, _row(tr, c), _row(tr, c), _row(tr, c)], out_specs=[_row(tr, c)] * 4,
                          out_shape=[jax.ShapeDtypeStruct((r, c), F32)] * 4,
                          compiler_params=_cp("parallel"))(recv if summed else grad, w, m, v)


def _cols_to_blocks(g, width):
    r = g.shape[0]
    return jnp.transpose(g.reshape(r, NDEV, width), (1, 0, 2))


def _blocks_to_cols(b):
    _, r, width = b.shape
    return jnp.transpose(b, (1, 0, 2)).reshape(r, NDEV * width)


_SMALL = ("b_ada", "g_mix", "rel_bias", "conv_w", "conv_b", "dt_bias", "a_log", "d_skip",
          "g_att_out", "g_ssd_out", "g_ffn", "g_final")


def _pad_rows(v):
    v = v.reshape(-1)
    return jnp.pad(v, (0, (-v.shape[0]) % LANE))


def kernel(x, c, w_ada, b_ada, g_mix, w_in, rel_bias, conv_w, conv_b, dt_bias, a_log, d_skip, g_att_out, g_ssd_out, w_out, g_ffn, w_gate, w_up, w_down, g_final, loss_target, m_w_ada, m_b_ada, m_g_mix, m_w_in, m_rel_bias, m_conv_w, m_conv_b, m_dt_bias, m_a_log, m_d_skip, m_g_att_out, m_g_ssd_out, m_w_out, m_g_ffn, m_w_gate, m_w_up, m_w_down, m_g_final, v_w_ada, v_b_ada, v_g_mix, v_w_in, v_rel_bias, v_conv_w, v_conv_b, v_dt_bias, v_a_log, v_d_skip, v_g_att_out, v_g_ssd_out, v_w_out, v_g_ffn, v_w_gate, v_w_up, v_w_down, v_g_final):
    me = 4 * lax.axis_index("x") + 2 * lax.axis_index("y") + lax.axis_index("c")
    wts = dict(w_ada=w_ada, b_ada=b_ada, g_mix=g_mix, w_in=w_in, rel_bias=rel_bias, conv_w=conv_w, conv_b=conv_b,
               dt_bias=dt_bias, a_log=a_log, d_skip=d_skip, g_att_out=g_att_out, g_ssd_out=g_ssd_out, w_out=w_out,
               g_ffn=g_ffn, w_gate=w_gate, w_up=w_up, w_down=w_down, g_final=g_final)
    mom = dict(w_ada=m_w_ada, b_ada=m_b_ada, g_mix=m_g_mix, w_in=m_w_in, rel_bias=m_rel_bias, conv_w=m_conv_w,
               conv_b=m_conv_b, dt_bias=m_dt_bias, a_log=m_a_log, d_skip=m_d_skip, g_att_out=m_g_att_out,
               g_ssd_out=m_g_ssd_out, w_out=m_w_out, g_ffn=m_g_ffn, w_gate=m_w_gate, w_up=m_w_up, w_down=m_w_down,
               g_final=m_g_final)
    var = dict(w_ada=v_w_ada, b_ada=v_b_ada, g_mix=v_g_mix, w_in=v_w_in, rel_bias=v_rel_bias, conv_w=v_conv_w,
               conv_b=v_conv_b, dt_bias=v_dt_bias, a_log=v_a_log, d_skip=v_d_skip, g_att_out=v_g_att_out,
               g_ssd_out=v_g_ssd_out, w_out=v_w_out, g_ffn=v_g_ffn, w_gate=v_w_gate, w_up=v_w_up, w_down=v_w_down,
               g_final=v_g_final)
    order = ("w_ada", "b_ada", "g_mix", "w_in", "rel_bias", "conv_w", "conv_b", "dt_bias", "a_log", "d_skip",
             "g_att_out", "g_ssd_out", "w_out", "g_ffn", "w_gate", "w_up", "w_down", "g_final")

    c_all, rel_all, cw_all, _ = _exchange([c, rel_bias[0], conv_w[0]], [True] * 3, "gather_small")
    c_all = c_all.reshape(NDEV, D)
    rel_full = _blocks_to_cols(rel_all)
    cw_full = _blocks_to_cols(cw_all)

    ncol = w_ada.shape[2]
    b_sh = lax.dynamic_slice(b_ada, (0, me * ncol), (1, ncol))
    mods_part = _mods(jnp.pad(c_all, ((0, 8), (0, 0))), w_ada[0], b_sh)[:NDEV]
    mods_all, tok_m = _exchange([mods_part], [True], "gather_mods")
    mods_me = lax.dynamic_index_in_dim(mods_all, me, axis=1, keepdims=False).reshape(6, 1, D)
    mods = [mods_me[i] for i in range(6)]

    st_in = _xstart([(w_in[0] + tok_m[0, 0]).astype(BF)], [True], "gather_w_in_start")
    st_rest = _xstart([(w_out[0] + st_in[-1][0, 0]).astype(BF), w_gate[0].astype(BF), w_up[0].astype(BF),
                       w_down[0].astype(BF)], [True] * 4, "gather_w_rest_start")

    def first_weights(after):
        (gi,) = _xwait(st_in, [True], after, "gather_w_in_wait")
        w_in_p = jnp.pad(_blocks_to_cols(gi), ((0, 0), (0, INP - INC)))
        return w_in_p, w_in_p.T

    def late_weights(after):
        go, gg, gu_, gd = _xwait(st_rest, [True] * 4, after, "gather_w_rest_wait")
        w_gu = jnp.concatenate([_blocks_to_cols(gg), _blocks_to_cols(gu_)], axis=1)
        w_dn = gd.reshape(FF, D)
        w_o = go.reshape(D, D)
        return w_o, w_o.T, w_gu, w_gu.T, w_dn, w_dn.T

    started = {}

    def grads_ready(which, *g):
        if which == "ffn":
            gw_down, gw_gu = g
            srcs = [_cols_to_blocks(gw_gu[:, :FF], FF // NDEV), _cols_to_blocks(gw_gu[:, FF:], FF // NDEV),
                    gw_down.reshape(NDEV, FF // NDEV, D)]
        elif which == "out":
            srcs = [g[0].reshape(NDEV, D // NDEV, D)]
        else:
            srcs = [_cols_to_blocks(g[0][:, :INC], INC // NDEV)]
        started[which] = _xstart([s_.astype(BF) for s_ in srcs], [False] * len(srcs), f"exchange_{which}_grads_start")
        return started[which][-1][0, 0]

    gx, small = _local_step(
        x[0], loss_target[0], mods, first_weights, late_weights, grads_ready,
        g_mix, rel_full, cw_full, conv_b, dt_bias, a_log, d_skip, g_att_out, g_ssd_out, g_ffn,
        g_final.reshape(1, D))

    loss_row = jnp.pad(jnp.sum(small["loss"]).reshape(1), (0, LANE - 1))
    pack = jnp.concatenate([small[k] for k in ("dmods", "g_mix", "conv_b", "dt_bias", "a_log", "d_skip", "g_att",
                                               "g_ssd", "g_ffn", "g_final", "rel_bias", "conv_w")] + [loss_row])
    packs, _ = _exchange([pack.reshape(-1, LANE)], [True], "gather_small_grads")
    rg, ru, rd = _xwait(started["ffn"], [False] * 3, packs, "exchange_ffn_grads_wait")
    (ro,) = _xwait(started["out"], [False], rg, "exchange_out_grads_wait")
    (ri,) = _xwait(started["in"], [False], ro, "exchange_in_grads_wait")
    big = dict(
        w_gate=_adam(w_gate[0], m_w_gate[0], v_w_gate[0], recv=rg, tr=256, name="adam_w_gate"),
        w_up=_adam(w_up[0], m_w_up[0], v_w_up[0], recv=ru, tr=256, name="adam_w_up"),
        w_down=_adam(w_down[0], m_w_down[0], v_w_down[0], recv=rd, tr=176, name="adam_w_down"))
    tot = _sum8(packs, name="sum_small").reshape(-1)

    sizes = dict(dmods=6 * D, g_mix=D, conv_b=XBC, dt_bias=LANE, a_log=LANE, d_skip=LANE, g_att_out=AW, g_ssd_out=SW,
                 g_ffn=D, g_final=D, rel_bias=NH * NREL, conv_w=4 * XBC, loss=LANE)
    off, o = {}, 0
    for k_, n_ in sizes.items():
        off[k_] = o
        o += n_
    take = lambda k_, n_=None: tot[off[k_]:off[k_] + (n_ or sizes[k_])]
    loss = tot[off["loss"]]
    rel_w = NREL // NDEV
    cw_w = XBC // NDEV
    gsm = dict(
        b_ada=take("dmods"), g_mix=take("g_mix"),
        rel_bias=lax.dynamic_slice(take("rel_bias").reshape(NH, NREL), (0, me * rel_w), (NH, rel_w)),
        conv_w=lax.dynamic_slice(take("conv_w").reshape(4, XBC), (0, me * cw_w), (4, cw_w)),
        conv_b=take("conv_b"), dt_bias=take("dt_bias", NH), a_log=take("a_log", NH), d_skip=take("d_skip", NH),
        g_att_out=take("g_att_out"), g_ssd_out=take("g_ssd_out"), g_ffn=take("g_ffn"), g_final=take("g_final"))

    flat = lambda d_: jnp.concatenate([_pad_rows(d_[k_]) for k_ in _SMALL])
    gflat = flat(gsm)
    nrow = -(-gflat.shape[0] // (8 * LANE)) * 8
    to2d = lambda v_: jnp.pad(v_, (0, nrow * LANE - v_.shape[0])).reshape(nrow, LANE)
    res = _adam(to2d(flat(wts)), to2d(flat(mom)), to2d(flat(var)), grad=to2d(gflat), tr=nrow, name="adam_small")
    outs = {k_: {} for k_ in ("grad", "delta", "m", "v")}
    o = 0
    for k_ in _SMALL:
        n_ = wts[k_].size
        for kind, arr in zip(("grad", "delta", "m", "v"), res):
            outs[kind][k_] = arr.reshape(-1)[o:o + n_].reshape(wts[k_].shape)
        o += n_ + (-n_) % LANE

    dm_all = packs[:, :6 * D // LANE, :].reshape(NDEV, 6 * D)
    dm_sh = lax.dynamic_slice(dm_all, (0, me * ncol), (NDEV, ncol))
    gwa = _gw_ada(c_all.T, dm_sh)
    big.update(
        w_ada=_adam(w_ada[0], m_w_ada[0], v_w_ada[0], grad=gwa, tr=256, name="adam_w_ada"),
        w_in=_adam(w_in[0], m_w_in[0], v_w_in[0], recv=ri, tr=256, name="adam_w_in"),
        w_out=_adam(w_out[0], m_w_out[0], v_w_out[0], recv=ro, tr=128, name="adam_w_out"))
    for k_, r_ in big.items():
        for kind, arr in zip(("grad", "delta", "m", "v"), r_):
            outs[kind][k_] = arr.reshape(wts[k_].shape)

    return (loss, gx.reshape(x.shape), *[outs["grad"][k_] for k_ in order], *[outs["delta"][k_] for k_ in order],
            *[outs["m"][k_] for k_ in order], *[outs["v"][k_] for k_ in order])
```

```python
import functools

import jax
import jax.numpy as jnp
from jax import lax
from jax.experimental import pallas as pl
from jax.experimental.pallas import tpu as pltpu

F32 = jnp.float32
BF = jnp.bfloat16
HI = lax.Precision.HIGHEST

D = 2048
NH = 16
HD = 64
AW = 1024
SW = 1024
NG = 2
NS = 128
XBC = 1536
CH = 64
BAND = 576
NREL = 320
REL_CLIP = 256
FF = 5632
INC = 5648
INP = 5760
EPS = 1e-6
NDEV = 8
LANE = 128
VMEM_LIMIT = 56 * 1024 * 1024
LR, B1, B2, AEPS, WD, STEP = 0.001, 0.9, 0.999, 1e-08, 0.01, 10
MESH = pl.DeviceIdType.MESH


def _cp(*sem):
    return pltpu.CompilerParams(dimension_semantics=sem, vmem_limit_bytes=VMEM_LIMIT)


def _row(ts, w, col=0):
    return pl.BlockSpec((ts, w), lambda i, _c=col: (i, _c))


def _fix(shape, single=False):
    nd = len(shape)
    if single:
        return pl.BlockSpec(shape, lambda i, _n=nd: (0,) * _n, pipeline_mode=pl.Buffered(1))
    return pl.BlockSpec(shape, lambda i, _n=nd: (0,) * _n)


def _silu_grad(x, s):
    return s * (1.0 + x * (1.0 - s))


def _softplus(x):
    return jnp.maximum(x, 0.0) + jnp.log1p(jnp.exp(-jnp.abs(x)))


_DIMS = {"NN": (((1,), (0,)), ((), ())), "TN": (((0,), (0,)), ((), ())), "NT": (((1,), (1,)), ((), ()))}


def _mm(pairs, mode, out_dtype, name, tm, tn, tk, precision=None):
    a0, b0 = pairs[0]
    m, k = a0.shape[::-1] if mode == "TN" else a0.shape
    n = b0.shape[0] if mode == "NT" else b0.shape[1]
    tm, tn, tk = min(tm, m), min(tn, n), min(tk, k)
    nk1 = k // tk
    npair = len(pairs)
    nk = nk1 * npair
    assert m % tm == 0 and n % tn == 0 and k % tk == 0, (name, a0.shape, b0.shape)
    dims = _DIMS[mode]

    def kloc(kk, p):
        return jnp.clip(kk - p * nk1, 0, nk1 - 1)

    def aspec(p):
        if mode == "TN":
            return pl.BlockSpec((tk, tm), lambda i, j, kk, _p=p: (kloc(kk, _p), i))
        return pl.BlockSpec((tm, tk), lambda i, j, kk, _p=p: (i, kloc(kk, _p)))

    def bspec(p):
        if mode == "NT":
            return pl.BlockSpec((tn, tk), lambda i, j, kk, _p=p: (j, kloc(kk, _p)))
        return pl.BlockSpec((tk, tn), lambda i, j, kk, _p=p: (kloc(kk, _p), j))

    def body(*refs):
        ab, o_ref = refs[:2 * npair], refs[2 * npair]
        if nk == 1:
            o_ref[...] = lax.dot_general(ab[0][...], ab[1][...], dims, preferred_element_type=F32,
                                         precision=precision).astype(o_ref.dtype)
            return
        acc = refs[2 * npair + 1]
        kk = pl.program_id(2)

        @pl.when(kk == 0)
        def _():
            acc[...] = jnp.zeros_like(acc)

        for p in range(npair):
            @pl.when((kk >= p * nk1) & (kk < (p + 1) * nk1))
            def _(p=p):
                acc[...] += lax.dot_general(ab[2 * p][...], ab[2 * p + 1][...], dims, preferred_element_type=F32,
                                            precision=precision)

        @pl.when(kk == nk - 1)
        def _():
            o_ref[...] = acc[...].astype(o_ref.dtype)

    in_specs = []
    for p in range(npair):
        in_specs += [aspec(p), bspec(p)]
    return pl.pallas_call(
        body, name=name, grid=(m // tm, n // tn, nk), in_specs=in_specs,
        out_specs=pl.BlockSpec((tm, tn), lambda i, j, kk: (i, j)),
        out_shape=jax.ShapeDtypeStruct((m, n), out_dtype),
        scratch_shapes=[] if nk == 1 else [pltpu.VMEM((tm, tn), F32)],
        compiler_params=_cp("parallel", "parallel", "arbitrary"),
    )(*[t for ab_ in pairs for t in ab_])


def _norm_fwd(x, g, sc, sh, mix=None, gt=None, *, name, ts=256):
    s, d = x.shape
    has = mix is not None

    def body(*refs):
        if has:
            x_ref, g_ref, sc_ref, sh_ref, mix_ref, gt_ref, h_ref, x1_ref = refs
        else:
            x_ref, g_ref, sc_ref, sh_ref, h_ref = refs
        xv = x_ref[...]
        if has:
            xv = xv + gt_ref[...] * mix_ref[...]
            x1_ref[...] = xv
        r = lax.rsqrt(jnp.mean(xv * xv, axis=-1, keepdims=True) + EPS)
        h_ref[...] = ((xv * r) * g_ref[...] * (1.0 + sc_ref[...]) + sh_ref[...]).astype(BF)

    ins = [x, g, sc, sh] + ([mix, gt] if has else [])
    in_specs = [_row(ts, d), _fix((1, d)), _fix((1, d)), _fix((1, d))] + ([_row(ts, d), _fix((1, d))] if has else [])
    out_shape = [jax.ShapeDtypeStruct((s, d), BF)] + ([jax.ShapeDtypeStruct((s, d), F32)] if has else [])
    out_specs = [_row(ts, d)] + ([_row(ts, d)] if has else [])
    out = pl.pallas_call(body, name=name, grid=(s // ts,), in_specs=in_specs, out_specs=out_specs,
                         out_shape=out_shape, compiler_params=_cp("parallel"))(*ins)
    return out if has else out[0]


def _norm_bwd(x, g, sc, sh, dh, dres, mix=None, gt=None, *, name, ts=256):
    s, d = x.shape
    has = mix is not None

    def body(*refs):
        if has:
            x_ref, g_ref, sc_ref, sh_ref, dh_ref, dres_ref, mix_ref, gt_ref, dx_ref, dmix_ref, part_ref = refs
        else:
            x_ref, g_ref, sc_ref, sh_ref, dh_ref, dres_ref, dx_ref, part_ref = refs
        i = pl.program_id(0)
        xv = x_ref[...]
        gv = g_ref[...]
        dhv = dh_ref[...]
        r = lax.rsqrt(jnp.mean(xv * xv, axis=-1, keepdims=True) + EPS)
        xh = xv * r
        dyn = dhv * (1.0 + sc_ref[...])
        gy = dyn * gv
        dx = dres_ref[...] + r * (gy - xh * jnp.mean(gy * xh, axis=-1, keepdims=True))
        dx_ref[...] = dx

        @pl.when(i == 0)
        def _():
            part_ref[...] = jnp.zeros_like(part_ref)

        part_ref[0:1, :] += jnp.sum(dhv * (xh * gv), axis=0, keepdims=True)
        part_ref[1:2, :] += jnp.sum(dhv, axis=0, keepdims=True)
        part_ref[2:3, :] += jnp.sum(dyn * xh, axis=0, keepdims=True)
        if has:
            part_ref[3:4, :] += jnp.sum(dx * mix_ref[...], axis=0, keepdims=True)
            dmix_ref[...] = (gt_ref[...] * dx).astype(BF)

    ins = [x, g, sc, sh, dh, dres] + ([mix, gt] if has else [])
    in_specs = [_row(ts, d), _fix((1, d)), _fix((1, d)), _fix((1, d)), _row(ts, d), _row(ts, d)]
    in_specs += [_row(ts, d), _fix((1, d))] if has else []
    out_shape = [jax.ShapeDtypeStruct((s, d), F32)] + ([jax.ShapeDtypeStruct((s, d), BF)] if has else [])
    out_shape += [jax.ShapeDtypeStruct((8, d), F32)]
    out_specs = [_row(ts, d)] + ([_row(ts, d)] if has else []) + [_fix((8, d))]
    return pl.pallas_call(body, name=name, grid=(s // ts,), in_specs=in_specs, out_specs=out_specs,
                          out_shape=out_shape, compiler_params=_cp("arbitrary"))(*ins)


def _final(x1, ffn, gt2, gf, tgt, *, ts=256):
    s, d = x1.shape

    def body(x1_ref, ffn_ref, gt_ref, g_ref, t_ref, dx_ref, dffn_ref, part_ref):
        i = pl.program_id(0)
        fv = ffn_ref[...]
        gv = g_ref[...]
        xv = x1_ref[...] + gt_ref[...] * fv
        r = lax.rsqrt(jnp.mean(xv * xv, axis=-1, keepdims=True) + EPS)
        xh = xv * r
        e = xh * gv - t_ref[...]
        dy = e * (1.0 / d)
        gy = dy * gv
        dx = r * (gy - xh * jnp.mean(gy * xh, axis=-1, keepdims=True))
        dx_ref[...] = dx
        dffn_ref[...] = (gt_ref[...] * dx).astype(BF)

        @pl.when(i == 0)
        def _():
            part_ref[...] = jnp.zeros_like(part_ref)

        part_ref[0:1, :] += jnp.sum(dy * xh, axis=0, keepdims=True)
        part_ref[1:2, :] += jnp.sum(dx * fv, axis=0, keepdims=True)
        part_ref[2:3, :] += jnp.sum(e * e, axis=0, keepdims=True) * (0.5 / d)

    return pl.pallas_call(
        body, name="final_loss", grid=(s // ts,),
        in_specs=[_row(ts, d), _row(ts, d), _fix((1, d)), _fix((1, d)), _row(ts, d)],
        out_specs=[_row(ts, d), _row(ts, d), _fix((8, d))],
        out_shape=[jax.ShapeDtypeStruct((s, d), F32), jax.ShapeDtypeStruct((s, d), BF),
                   jax.ShapeDtypeStruct((8, d), F32)],
        compiler_params=_cp("arbitrary"))(x1, ffn, gt2, gf, tgt)


def _gate_up(h, wg_t, wu_t, *, tm=1024, tn=512):
    s, k = h.shape
    n = wg_t.shape[0]
    tm = min(tm, s)

    def body(h_ref, wg_ref, wu_ref, g_ref, u_ref, a_ref):
        hv = h_ref[...]
        gv = lax.dot_general(hv, wg_ref[...], _DIMS["NT"], preferred_element_type=F32)
        uv = lax.dot_general(hv, wu_ref[...], _DIMS["NT"], preferred_element_type=F32)
        g_ref[...] = gv.astype(BF)
        u_ref[...] = uv.astype(BF)
        a_ref[...] = (gv * jax.nn.sigmoid(gv) * uv).astype(BF)

    wspec = pl.BlockSpec((tn, k), lambda i, j: (j, 0))
    ospec = pl.BlockSpec((tm, tn), lambda i, j: (i, j))
    return pl.pallas_call(body, name="gate_up", grid=(s // tm, n // tn),
                          in_specs=[pl.BlockSpec((tm, k), lambda i, j: (i, 0)), wspec, wspec],
                          out_specs=[ospec] * 3, out_shape=[jax.ShapeDtypeStruct((s, n), BF)] * 3,
                          compiler_params=_cp("parallel", "parallel"))(h, wg_t, wu_t)


def _swiglu_bwd(gate, up, dact, *, ts=256):
    s = gate.shape[0]

    def body(g_ref, u_ref, da_ref, dg_ref, du_ref):
        gv = g_ref[...].astype(F32)
        uv = u_ref[...].astype(F32)
        dav = da_ref[...].astype(F32)
        sg = jax.nn.sigmoid(gv)
        dg_ref[...] = (dav * uv * _silu_grad(gv, sg)).astype(BF)
        du_ref[...] = (dav * gv * sg).astype(BF)

    return pl.pallas_call(body, name="swiglu_bwd", grid=(s // ts,),
                          in_specs=[_row(ts, FF)] * 3, out_specs=[_row(ts, FF)] * 2,
                          out_shape=[jax.ShapeDtypeStruct((s, FF), BF)] * 2,
                          compiler_params=_cp("parallel"))(gate, up, dact)


def _mixout_fwd(att, y, proj, g_att, g_ssd, *, ts=256):
    s = att.shape[0]

    def body(a_ref, y_ref, z_ref, ga_ref, gs_ref, o_ref):
        av = a_ref[...]
        ra = lax.rsqrt(jnp.mean(av * av, axis=-1, keepdims=True) + EPS)
        o_ref[:, 0:AW] = (av * ra * ga_ref[...]).astype(BF)
        zv = z_ref[...]
        yz = y_ref[...] * (zv * jax.nn.sigmoid(zv))
        rs = lax.rsqrt(jnp.mean(yz * yz, axis=-1, keepdims=True) + EPS)
        o_ref[:, AW:AW + SW] = (yz * rs * gs_ref[...]).astype(BF)

    return pl.pallas_call(body, name="mixout_fwd", grid=(s // ts,),
                          in_specs=[_row(ts, AW), _row(ts, SW), _row(ts, SW, 3), _fix((1, AW)), _fix((1, SW))],
                          out_specs=_row(ts, AW + SW), out_shape=jax.ShapeDtypeStruct((s, AW + SW), BF),
                          compiler_params=_cp("parallel"))(att, y, proj, g_att, g_ssd)


def _mixout_bwd(dcat, att, y, proj, g_att, g_ssd, *, ts=256):
    s = att.shape[0]

    def body(dc_ref, a_ref, y_ref, z_ref, ga_ref, gs_ref, da_ref, dy_ref, dz_ref, part_ref):
        i = pl.program_id(0)
        av = a_ref[...]
        dca = dc_ref[:, 0:AW]
        ra = lax.rsqrt(jnp.mean(av * av, axis=-1, keepdims=True) + EPS)
        ah = av * ra
        gy = dca * ga_ref[...]
        da_ref[...] = ra * (gy - ah * jnp.mean(gy * ah, axis=-1, keepdims=True))
        zv = z_ref[...]
        yv = y_ref[...]
        sg = jax.nn.sigmoid(zv)
        sz = zv * sg
        yz = yv * sz
        dcs = dc_ref[:, AW:AW + SW]
        rs = lax.rsqrt(jnp.mean(yz * yz, axis=-1, keepdims=True) + EPS)
        yh = yz * rs
        gys = dcs * gs_ref[...]
        dyz = rs * (gys - yh * jnp.mean(gys * yh, axis=-1, keepdims=True))
        dy_ref[...] = dyz * sz
        dz_ref[...] = (dyz * yv * _silu_grad(zv, sg)).astype(BF)

        @pl.when(i == 0)
        def _():
            part_ref[...] = jnp.zeros_like(part_ref)

        part_ref[0:1, :] += jnp.sum(dca * ah, axis=0, keepdims=True)
        part_ref[1:2, :] += jnp.sum(dcs * yh, axis=0, keepdims=True)

    return pl.pallas_call(
        body, name="mixout_bwd", grid=(s // ts,),
        in_specs=[_row(ts, AW + SW), _row(ts, AW), _row(ts, SW), _row(ts, SW, 3), _fix((1, AW)), _fix((1, SW))],
        out_specs=[_row(ts, AW), _row(ts, SW), _row(ts, SW), _fix((8, AW))],
        out_shape=[jax.ShapeDtypeStruct((s, AW), F32), jax.ShapeDtypeStruct((s, SW), F32),
                   jax.ShapeDtypeStruct((s, SW), BF), jax.ShapeDtypeStruct((8, AW), F32)],
        compiler_params=_cp("arbitrary"))(dcat, att, y, proj, g_att, g_ssd)


TQ = 512
SCALE = HD ** -0.5


def _attn_fwd(qkv, bias):
    s = qkv.shape[0]
    nb = s // TQ
    prev = lambda i: jnp.maximum(i - 1, 0)

    def body(q_ref, kp_ref, kc_ref, vp_ref, vc_ref, b_ref, o_ref, kbuf, vbuf):
        i = pl.program_id(0)
        kbuf[0:TQ, :] = kp_ref[...]
        kbuf[TQ:2 * TQ, :] = kc_ref[...]
        vbuf[0:TQ, :] = vp_ref[...]
        vbuf[TQ:2 * TQ, :] = vc_ref[...]

        def chunk(c, carry):
            r0 = pl.multiple_of(c * CH, CH)
            kpos = i * TQ + r0 - TQ + lax.broadcasted_iota(jnp.int32, (1, BAND), 1)
            valid = kpos >= 0
            for h in range(NH):
                sl = slice(HD * h, HD * h + HD)
                q = q_ref[pl.ds(r0, CH), sl]
                k = kbuf[pl.ds(r0, BAND), sl]
                v = vbuf[pl.ds(r0, BAND), sl]
                sc = lax.dot_general(q, k, (((1,), (1,)), ((), ())), preferred_element_type=F32) * SCALE + b_ref[h]
                sc = jnp.where(valid, sc, -jnp.inf)
                e = jnp.exp(sc - jnp.max(sc, axis=-1, keepdims=True))
                p = e / jnp.sum(e, axis=-1, keepdims=True)
                o_ref[pl.ds(r0, CH), sl] = jnp.dot(p.astype(BF), v, preferred_element_type=F32)
            return carry

        lax.fori_loop(0, TQ // CH, chunk, 0)

    blk = lambda col, im: pl.BlockSpec((TQ, AW), lambda i, _c=col, _f=im: (_f(i), _c))
    cur = lambda i: i
    return pl.pallas_call(
        body, name="attn_fwd", grid=(nb,),
        in_specs=[blk(0, cur), blk(1, prev), blk(1, cur), blk(2, prev), blk(2, cur), _fix((NH, CH, BAND))],
        out_specs=_row(TQ, AW), out_shape=jax.ShapeDtypeStruct((s, AW), F32),
        scratch_shapes=[pltpu.VMEM((2 * TQ, AW), BF), pltpu.VMEM((2 * TQ, AW), BF)],
        compiler_params=_cp("parallel"))(qkv, qkv, qkv, qkv, qkv, bias)


def _attn_bwd(qkv, datt, bias_t):
    s = qkv.shape[0]
    nb = s // TQ
    cur = lambda i: jnp.minimum(i, nb - 1)
    prev = lambda i: jnp.maximum(jnp.minimum(i, nb - 1) - 1, 0)
    late = lambda i: jnp.maximum(i - 1, 0)

    def body(q_ref, kp_ref, kc_ref, vp_ref, vc_ref, do_ref, b_ref, dq_ref, dk_ref, dv_ref, db_ref,
             kbuf, vbuf, dkacc, dvacc):
        i = pl.program_id(0)

        @pl.when(i == 0)
        def _():
            dkacc[...] = jnp.zeros_like(dkacc)
            dvacc[...] = jnp.zeros_like(dvacc)
            db_ref[...] = jnp.zeros_like(db_ref)

        @pl.when(i < nb)
        def _():
            kbuf[0:TQ, :] = kp_ref[...]
            kbuf[TQ:2 * TQ, :] = kc_ref[...]
            vbuf[0:TQ, :] = vp_ref[...]
            vbuf[TQ:2 * TQ, :] = vc_ref[...]

            def chunk(c, carry):
                r0 = pl.multiple_of(c * CH, CH)
                kpos = i * TQ + r0 - TQ + lax.broadcasted_iota(jnp.int32, (BAND, 1), 0)
                valid = kpos >= 0
                for h in range(NH):
                    sl = slice(HD * h, HD * h + HD)
                    q = q_ref[pl.ds(r0, CH), sl]
                    k = kbuf[pl.ds(r0, BAND), sl]
                    v = vbuf[pl.ds(r0, BAND), sl]
                    do = do_ref[pl.ds(r0, CH), sl].astype(BF)
                    st = lax.dot_general(k, q, (((1,), (1,)), ((), ())), preferred_element_type=F32) * SCALE + b_ref[h]
                    st = jnp.where(valid, st, -jnp.inf)
                    e = jnp.exp(st - jnp.max(st, axis=0, keepdims=True))
                    pt = e / jnp.sum(e, axis=0, keepdims=True)
                    dvacc[pl.ds(r0, BAND), sl] += jnp.dot(pt.astype(BF), do, preferred_element_type=F32)
                    dpt = lax.dot_general(v, do, (((1,), (1,)), ((), ())), preferred_element_type=F32)
                    dst = pt * (dpt - jnp.sum(dpt * pt, axis=0, keepdims=True))
                    db_ref[h] += dst
                    dsb = dst.astype(BF)
                    dkacc[pl.ds(r0, BAND), sl] += jnp.dot(dsb, q, preferred_element_type=F32) * SCALE
                    dq = lax.dot_general(dsb, k, (((0,), (0,)), ((), ())), preferred_element_type=F32) * SCALE
                    dq_ref[pl.ds(r0, CH), sl] = dq.astype(BF)
                return carry

            lax.fori_loop(0, TQ // CH, chunk, 0)

        dk_ref[...] = dkacc[0:TQ, :].astype(BF)
        dv_ref[...] = dvacc[0:TQ, :].astype(BF)
        dkacc[0:TQ, :] = dkacc[TQ:2 * TQ, :]
        dvacc[0:TQ, :] = dvacc[TQ:2 * TQ, :]
        dkacc[TQ:2 * TQ, :] = jnp.zeros((TQ, AW), F32)
        dvacc[TQ:2 * TQ, :] = jnp.zeros((TQ, AW), F32)

    blk = lambda col, im: pl.BlockSpec((TQ, AW), lambda i, _c=col, _f=im: (_f(i), _c))
    return pl.pallas_call(
        body, name="attn_bwd", grid=(nb + 1,),
        in_specs=[blk(0, cur), blk(1, prev), blk(1, cur), blk(2, prev), blk(2, cur), blk(0, cur),
                  _fix((NH, BAND, CH))],
        out_specs=[blk(0, cur), blk(0, late), blk(0, late), _fix((NH, BAND, CH))],
        out_shape=[jax.ShapeDtypeStruct((s, AW), BF)] * 3 + [jax.ShapeDtypeStruct((NH, BAND, CH), F32)],
        scratch_shapes=[pltpu.VMEM((2 * TQ, AW), BF), pltpu.VMEM((2 * TQ, AW), BF),
                        pltpu.VMEM((2 * TQ, AW), F32), pltpu.VMEM((2 * TQ, AW), F32)],
        compiler_params=_cp("arbitrary"))(qkv, qkv, qkv, qkv, qkv, datt, bias_t)


QB = 256
NQC = QB // CH
WIN = 3 * QB


def _bias_window(bias):
    rows = [jnp.pad(bias, ((0, 0), (0, 0), (CH * a, WIN - BAND - CH * a)), constant_values=-jnp.inf)
            for a in range(NQC)]
    return jnp.concatenate(rows, axis=1)


def _win_specs(width, nb):
    def mk(col, back):
        return pl.BlockSpec((QB, width), lambda i, _c=col, _k=back: (jnp.maximum(jnp.minimum(i, nb - 1) - _k, 0), _c))
    return [mk(1, 2), mk(1, 1), mk(1, 0), mk(2, 2), mk(2, 1), mk(2, 0)]


def _attn2_fwd(qkv, bias_w):
    s = qkv.shape[0]
    nb = s // QB

    def body(q_ref, k0, k1, k2, v0, v1, v2, b_ref, o_ref, kbuf, vbuf):
        i = pl.program_id(0)
        for j, (kr, vr) in enumerate(((k0, v0), (k1, v1), (k2, v2))):
            kbuf[QB * j:QB * j + QB, :] = kr[...].astype(BF)
            vbuf[QB * j:QB * j + QB, :] = vr[...].astype(BF)
        valid = (i * QB - 2 * QB + lax.broadcasted_iota(jnp.int32, (1, WIN), 1)) >= 0
        for h in range(NH):
            sl = slice(HD * h, HD * h + HD)
            sc = lax.dot_general(q_ref[:, sl].astype(BF), kbuf[:, sl], _DIMS["NT"], preferred_element_type=F32)
            sc = jnp.where(valid, sc * SCALE + b_ref[h], -jnp.inf)
            e = jnp.exp(sc - jnp.max(sc, axis=-1, keepdims=True))
            p = e * (1.0 / jnp.sum(e, axis=-1, keepdims=True))
            o_ref[:, sl] = jnp.dot(p.astype(BF), vbuf[:, sl], preferred_element_type=F32)

    return pl.pallas_call(
        body, name="attn_fwd", grid=(nb,),
        in_specs=[_row(QB, AW)] + _win_specs(AW, nb) + [_fix((NH, QB, WIN), single=True)],
        out_specs=_row(QB, AW), out_shape=jax.ShapeDtypeStruct((s, AW), F32),
        scratch_shapes=[pltpu.VMEM((WIN, AW), BF), pltpu.VMEM((WIN, AW), BF)],
        compiler_params=_cp("parallel"))(qkv, qkv, qkv, qkv, qkv, qkv, qkv, bias_w)


def _attn2_bwd(qkv, datt, bias_wt):
    s = qkv.shape[0]
    nb = s // QB
    cur = lambda i: jnp.minimum(i, nb - 1)
    late = lambda i: jnp.maximum(i - 2, 0)

    def body(q_ref, k0, k1, k2, v0, v1, v2, do_ref, b_ref, dq_ref, dk_ref, dv_ref, db_ref,
             kbuf, vbuf, dkacc, dvacc, dsbuf):
        i = pl.program_id(0)

        @pl.when(i == 0)
        def _():
            dkacc[...] = jnp.zeros_like(dkacc)
            dvacc[...] = jnp.zeros_like(dvacc)
            db_ref[...] = jnp.zeros_like(db_ref)

        @pl.when(i < nb)
        def _():
            for j, (kr, vr) in enumerate(((k0, v0), (k1, v1), (k2, v2))):
                kbuf[QB * j:QB * j + QB, :] = kr[...].astype(BF)
                vbuf[QB * j:QB * j + QB, :] = vr[...].astype(BF)
            valid = (i * QB - 2 * QB + lax.broadcasted_iota(jnp.int32, (WIN, 1), 0)) >= 0
            for h in range(NH):
                sl = slice(HD * h, HD * h + HD)
                q = q_ref[:, sl].astype(BF)
                k = kbuf[:, sl]
                do = do_ref[:, sl].astype(BF)
                st = lax.dot_general(k, q, (((1,), (1,)), ((), ())), preferred_element_type=F32)
                st = jnp.where(valid, st * SCALE + b_ref[h], -jnp.inf)
                e = jnp.exp(st - jnp.max(st, axis=0, keepdims=True))
                pt = e * (1.0 / jnp.sum(e, axis=0, keepdims=True))
                dvacc[:, sl] += jnp.dot(pt.astype(BF), do, preferred_element_type=F32)
                dpt = lax.dot_general(vbuf[:, sl], do, (((1,), (1,)), ((), ())), preferred_element_type=F32)
                dst = pt * (dpt - jnp.sum(dpt * pt, axis=0, keepdims=True))
                dsbuf[...] = dst
                fold = dsbuf[0:BAND, 0:CH]
                for a in range(1, NQC):
                    fold = fold + dsbuf[CH * a:CH * a + BAND, CH * a:CH * a + CH]
                db_ref[h] += fold
                dsb = dst.astype(BF)
                dkacc[:, sl] += jnp.dot(dsb, q, preferred_element_type=F32) * SCALE
                dq = lax.dot_general(dsb, k, (((0,), (0,)), ((), ())), preferred_element_type=F32) * SCALE
                dq_ref[:, sl] = dq.astype(BF)

        dk_ref[...] = dkacc[0:QB, :].astype(BF)
        dv_ref[...] = dvacc[0:QB, :].astype(BF)
        for acc in (dkacc, dvacc):
            acc[0:QB, :] = acc[QB:2 * QB, :]
            acc[QB:2 * QB, :] = acc[2 * QB:WIN, :]
            acc[2 * QB:WIN, :] = jnp.zeros((QB, AW), F32)

    qspec = pl.BlockSpec((QB, AW), lambda i: (cur(i), 0))
    lspec = pl.BlockSpec((QB, AW), lambda i: (late(i), 0))
    return pl.pallas_call(
        body, name="attn_bwd", grid=(nb + 2,),
        in_specs=[qspec] + _win_specs(AW, nb) + [qspec, _fix((NH, WIN, QB), single=True)],
        out_specs=[qspec, lspec, lspec, _fix((NH, BAND, CH))],
        out_shape=[jax.ShapeDtypeStruct((s, AW), BF)] * 3 + [jax.ShapeDtypeStruct((NH, BAND, CH), F32)],
        scratch_shapes=[pltpu.VMEM((WIN, AW), BF), pltpu.VMEM((WIN, AW), BF),
                        pltpu.VMEM((WIN, AW), F32), pltpu.VMEM((WIN, AW), F32), pltpu.VMEM((WIN, QB), F32)],
        compiler_params=_cp("arbitrary"))(qkv, qkv, qkv, qkv, qkv, qkv, qkv, datt, bias_wt)


TB = 512
SCH = 256
NCB = TB // SCH
GW = 8 * HD


def _tri(lower):
    r = lax.broadcasted_iota(jnp.int32, (SCH, SCH), 0)
    c = lax.broadcasted_iota(jnp.int32, (SCH, SCH), 1)
    return r >= c if lower else r <= c


def _ssd_fwd(proj, conv_w, conv_b, dt_bias, a_log, d_skip):
    s = proj.shape[0]
    nb = s // TB

    def body(xs_ref, bc_ref, dt_ref, cw_ref, cb_ref, dtb_ref, al_ref, dsk_ref, y_ref, pre_ref, prev_ref,
             xpad, ubuf, dtbuf, csb, cstb, hst, xdec, yoffb, drow):
        i = pl.program_id(0)

        @pl.when(i == 0)
        def _():
            xpad[0:8, :] = jnp.zeros((8, XBC), F32)
            hst[...] = jnp.zeros_like(hst)

        xpad[8:8 + TB, 0:SW] = xs_ref[...]
        xpad[8:8 + TB, SW:XBC] = bc_ref[...]
        pre = cb_ref[...]
        for kk in range(4):
            pre = pre + cw_ref[kk:kk + 1, :] * xpad[5 + kk:5 + kk + TB, :]
        pre_ref[...] = pre
        ubuf[...] = pre * jax.nn.sigmoid(pre)
        xpad[0:8, :] = xpad[TB:TB + 8, :]
        dtbuf[...] = _softplus(dt_ref[...] + dtb_ref[...])
        a = -jnp.exp(al_ref[...])
        tri = _tri(True).astype(F32)
        causal = _tri(True)

        def chunk(c, carry):
            r0 = pl.multiple_of(c * SCH, SCH)
            rows = pl.ds(r0, SCH)
            cs = jnp.dot(tri, dtbuf[rows, :] * a, precision=HI, preferred_element_type=F32)
            csb[...] = cs
            cstb[...] = cs.T
            for g in range(NG):
                bg = ubuf[rows, SW + NS * g:SW + NS * g + NS]
                cg = ubuf[rows, SW + NG * NS + NS * g:SW + NG * NS + NS * g + NS].astype(BF)
                cb = lax.dot_general(cg, bg.astype(BF), (((1,), (1,)), ((), ())), preferred_element_type=F32)
                hg = hst[g]
                prev_ref[c, g] = hg
                yoffb[...] = jnp.dot(cg, hg.astype(BF), preferred_element_type=F32)
                for r in range(8):
                    h = 8 * g + r
                    sl = slice(HD * h, HD * h + HD)
                    rs = slice(HD * r, HD * r + HD)
                    cs_h = csb[:, h:h + 1]
                    cl = csb[SCH - 1:SCH, h:h + 1]
                    seg = jnp.exp(jnp.where(causal, cs_h - cstb[h:h + 1, :], -jnp.inf))
                    xh = ubuf[rows, sl]
                    xdt = xh * dtbuf[rows, h:h + 1]
                    yd = jnp.dot((cb * seg).astype(BF), xdt.astype(BF), preferred_element_type=F32)
                    y_ref[rows, sl] = yd + jnp.exp(cs_h) * yoffb[:, rs] + xh * dsk_ref[:, h:h + 1]
                    xdec[:, rs] = xdt * jnp.exp(cl - cs_h)
                    drow[:, rs] = jnp.broadcast_to(jnp.exp(cl), (1, HD))
                st = jnp.dot(bg.T.astype(BF), xdec[...].astype(BF), preferred_element_type=F32)
                hst[g] = hg * drow[...] + st
            return carry

        lax.fori_loop(0, NCB, chunk, 0)

    return pl.pallas_call(
        body, name="ssd_fwd", grid=(nb,),
        in_specs=[_row(TB, SW, 4), _row(TB, 512, 10), _row(TB, LANE, 44), _fix((4, XBC)), _fix((1, XBC)),
                  _fix((1, LANE)), _fix((1, LANE)), _fix((1, LANE))],
        out_specs=[_row(TB, SW), _row(TB, XBC), pl.BlockSpec((NCB, NG, NS, GW), lambda i: (i, 0, 0, 0))],
        out_shape=[jax.ShapeDtypeStruct((s, SW), F32), jax.ShapeDtypeStruct((s, XBC), F32),
                   jax.ShapeDtypeStruct((s // SCH, NG, NS, GW), F32)],
        scratch_shapes=[pltpu.VMEM((TB + 8, XBC), F32), pltpu.VMEM((TB, XBC), F32), pltpu.VMEM((TB, LANE), F32),
                        pltpu.VMEM((SCH, LANE), F32), pltpu.VMEM((LANE, SCH), F32), pltpu.VMEM((NG, NS, GW), F32),
                        pltpu.VMEM((SCH, GW), F32), pltpu.VMEM((SCH, GW), F32), pltpu.VMEM((1, GW), F32)],
        compiler_params=_cp("arbitrary"))(proj, proj, proj, conv_w, conv_b, dt_bias, a_log, d_skip)


def _ssd_bwd(dy, pre, proj, prev, conv_w, dt_bias, a_log, d_skip):
    s = dy.shape[0]
    nb = s // TB
    rev = lambda i: nb - 1 - i
    halo = lambda i: jnp.maximum((nb - 1 - i) * (TB // 8) - 1, 0)

    def body(dy_ref, pre_ref, dt_ref, prev_ref, xs_ref, bc_ref, xsh_ref, bch_ref, cw_ref, dtb_ref, al_ref, dsk_ref,
             dx_ref, ddt_ref, pw_ref, ph_ref,
             ubuf, dtbuf, dpad, xpad, csb, cstb, dhs, yoffb, ebuf, xdecb, wbuf, drow, dub):
        i = pl.program_id(0)

        @pl.when(i == 0)
        def _():
            dhs[...] = jnp.zeros_like(dhs)
            dpad[TB:TB + 8, :] = jnp.zeros((8, XBC), F32)
            pw_ref[...] = jnp.zeros_like(pw_ref)
            ph_ref[...] = jnp.zeros_like(ph_ref)

        pre = pre_ref[...]
        ubuf[...] = pre * jax.nn.sigmoid(pre)
        dtbuf[...] = _softplus(dt_ref[...] + dtb_ref[...])
        a = -jnp.exp(al_ref[...])
        tri = _tri(True).astype(F32)
        trit = _tri(False).astype(F32)
        causal = _tri(True)
        upper = _tri(False)
        lane = lax.broadcasted_iota(jnp.int32, (SCH, LANE), 1)
        lane1 = lax.broadcasted_iota(jnp.int32, (1, LANE), 1)
        lastrow = lax.broadcasted_iota(jnp.int32, (SCH, LANE), 0) == SCH - 1

        def chunk(cc, carry):
            c = NCB - 1 - cc
            r0 = pl.multiple_of(c * SCH, SCH)
            rows = pl.ds(r0, SCH)
            dtc = dtbuf[rows, :]
            cs = jnp.dot(tri, dtc * a, precision=HI, preferred_element_type=F32)
            csb[...] = cs
            cstb[...] = cs.T
            dcs = jnp.zeros((SCH, LANE), F32)
            xr = jnp.zeros((SCH, LANE), F32)
            dlast = jnp.zeros((1, LANE), F32)
            dsk = jnp.zeros((1, LANE), F32)
            for g in range(NG):
                bg = ubuf[rows, SW + NS * g:SW + NS * g + NS]
                cg = ubuf[rows, SW + NG * NS + NS * g:SW + NG * NS + NS * g + NS]
                bb = bg.astype(BF)
                cbf = cg.astype(BF)
                cb = lax.dot_general(cbf, bb, (((1,), (1,)), ((), ())), preferred_element_type=F32)
                cbt = lax.dot_general(bb, cbf, (((1,), (1,)), ((), ())), preferred_element_type=F32)
                hp = prev_ref[c, g]
                hpb = hp.astype(BF)
                dhg = dhs[g]
                dhb = dhg.astype(BF)
                yoffb[...] = jnp.dot(cbf, hpb, preferred_element_type=F32)
                for r in range(8):
                    h = 8 * g + r
                    sl = slice(HD * h, HD * h + HD)
                    rs = slice(HD * r, HD * r + HD)
                    cs_h = csb[:, h:h + 1]
                    cl = csb[SCH - 1:SCH, h:h + 1]
                    dyh = dy_ref[rows, sl]
                    ecs = jnp.exp(cs_h)
                    ebuf[:, rs] = dyh * ecs
                    xdecb[:, rs] = ubuf[rows, sl] * dtbuf[rows, h:h + 1] * jnp.exp(cl - cs_h)
                    t1 = jnp.sum(dyh * ecs * yoffb[:, rs], axis=1, keepdims=True)
                    dcs = dcs + jnp.where(lane == h, t1, 0.0)
                    drow[:, rs] = jnp.broadcast_to(jnp.exp(cl), (1, HD))
                eb = ebuf[...].astype(BF)
                dc = lax.dot_general(eb, hpb, (((1,), (1,)), ((), ())), preferred_element_type=F32)
                dprev = jnp.dot(cg.T.astype(BF), eb, preferred_element_type=F32)
                wbuf[...] = jnp.dot(bb, dhb, preferred_element_type=F32)
                db = lax.dot_general(xdecb[...].astype(BF), dhb, (((1,), (1,)), ((), ())), preferred_element_type=F32)
                dcbs = jnp.zeros((SCH, SCH), F32)
                dcbts = jnp.zeros((SCH, SCH), F32)
                for r in range(8):
                    h = 8 * g + r
                    sl = slice(HD * h, HD * h + HD)
                    rs = slice(HD * r, HD * r + HD)
                    cs_h = csb[:, h:h + 1]
                    cst_h = cstb[h:h + 1, :]
                    cl = csb[SCH - 1:SCH, h:h + 1]
                    seg = jnp.exp(jnp.where(causal, cs_h - cst_h, -jnp.inf))
                    segt = jnp.exp(jnp.where(upper, cst_h - cs_h, -jnp.inf))
                    dyh = dy_ref[rows, sl]
                    xh = ubuf[rows, sl]
                    dt_h = dtbuf[rows, h:h + 1]
                    xdt = xh * dt_h
                    dyb = dyh.astype(BF)
                    xdb = xdt.astype(BF)
                    dm = lax.dot_general(dyb, xdb, (((1,), (1,)), ((), ())), preferred_element_type=F32)
                    dmt = lax.dot_general(xdb, dyb, (((1,), (1,)), ((), ())), preferred_element_type=F32)
                    mt = cbt * segt
                    dxdt = jnp.dot(mt.astype(BF), dyb, preferred_element_type=F32)
                    dcs_h = (jnp.sum(dm * (cb * seg), axis=1, keepdims=True)
                             - jnp.sum(dmt * mt, axis=1, keepdims=True))
                    dcbs = dcbs + dm * seg
                    dcbts = dcbts + dmt * segt
                    dec = jnp.exp(cl - cs_h)
                    w_h = wbuf[:, rs]
                    dxdt = dxdt + dec * w_h
                    t = jnp.sum(w_h * xdt, axis=1, keepdims=True) * dec
                    dcs_h = dcs_h - t
                    dch = jnp.sum(jnp.sum(dhs[g, :, rs] * prev_ref[c, g, :, rs], axis=1, keepdims=True),
                                  axis=0, keepdims=True)
                    dl = jnp.sum(t, axis=0, keepdims=True) + dch * jnp.exp(cl)
                    dcs = dcs + jnp.where(lane == h, dcs_h, 0.0)
                    dlast = dlast + jnp.where(lane1 == h, dl, 0.0)
                    xr = xr + jnp.where(lane == h, jnp.sum(dxdt * xh, axis=1, keepdims=True), 0.0)
                    dskh = jnp.sum(jnp.sum(dyh * xh, axis=1, keepdims=True), axis=0, keepdims=True)
                    dsk = dsk + jnp.where(lane1 == h, dskh, 0.0)
                    dub[:, sl] = dyh * dsk_ref[:, h:h + 1] + dxdt * dt_h
                dc = dc + jnp.dot(dcbs.astype(BF), bb, preferred_element_type=F32)
                db = db + jnp.dot(dcbts.astype(BF), cbf, preferred_element_type=F32)
                dub[:, SW + NS * g:SW + NS * g + NS] = db
                dub[:, SW + NG * NS + NS * g:SW + NG * NS + NS * g + NS] = dc
                dhs[g] = dprev + drow[...] * dhg
            dcs = dcs + jnp.where(lastrow, dlast, 0.0)
            dadt = jnp.dot(trit, dcs, precision=HI, preferred_element_type=F32)
            ddt = dadt * a + xr
            ddtr = ddt * jax.nn.sigmoid(dt_ref[rows, :] + dtb_ref[...])
            ddt_ref[rows, :] = ddtr.astype(BF)
            ph_ref[0:1, :] += jnp.sum(ddtr, axis=0, keepdims=True)
            ph_ref[1:2, :] += jnp.sum(dadt * dtc, axis=0, keepdims=True) * a
            ph_ref[2:3, :] += dsk
            pc = pre_ref[rows, :]
            sg = jax.nn.sigmoid(pc)
            dpad[rows, :] = dub[...] * _silu_grad(pc, sg)
            return carry

        lax.fori_loop(0, NCB, chunk, 0)

        dxb = cw_ref[0:1, :] * dpad[3:3 + TB, :]
        for kk in range(1, 4):
            dxb = dxb + cw_ref[kk:kk + 1, :] * dpad[3 - kk:3 - kk + TB, :]
        dx_ref[...] = dxb.astype(BF)
        keep = jnp.where(i < nb - 1, 1.0, 0.0)
        xpad[0:8, 0:SW] = xsh_ref[...] * keep
        xpad[0:8, SW:XBC] = bch_ref[...] * keep
        xpad[8:8 + TB, 0:SW] = xs_ref[...]
        xpad[8:8 + TB, SW:XBC] = bc_ref[...]
        dp = dpad[0:TB, :]
        for kk in range(4):
            pw_ref[kk:kk + 1, :] += jnp.sum(dp * xpad[5 + kk:5 + kk + TB, :], axis=0, keepdims=True)
        pw_ref[4:5, :] += jnp.sum(dp, axis=0, keepdims=True)
        dpad[TB:TB + 8, :] = dpad[0:8, :]

    rblk = lambda w, col: pl.BlockSpec((TB, w), lambda i, _c=col: (rev(i), _c))
    return pl.pallas_call(
        body, name="ssd_bwd", grid=(nb,),
        in_specs=[rblk(SW, 0), rblk(XBC, 0), rblk(LANE, 44),
                  pl.BlockSpec((NCB, NG, NS, GW), lambda i: (rev(i), 0, 0, 0)),
                  rblk(SW, 4), rblk(512, 10),
                  pl.BlockSpec((8, SW), lambda i: (halo(i), 4)), pl.BlockSpec((8, 512), lambda i: (halo(i), 10)),
                  _fix((4, XBC)), _fix((1, LANE)), _fix((1, LANE)), _fix((1, LANE))],
        out_specs=[rblk(XBC, 0), rblk(LANE, 0), _fix((8, XBC)), _fix((8, LANE))],
        out_shape=[jax.ShapeDtypeStruct((s, XBC), BF), jax.ShapeDtypeStruct((s, LANE), BF),
                   jax.ShapeDtypeStruct((8, XBC), F32), jax.ShapeDtypeStruct((8, LANE), F32)],
        scratch_shapes=[pltpu.VMEM((TB, XBC), F32), pltpu.VMEM((TB, LANE), F32), pltpu.VMEM((TB + 8, XBC), F32),
                        pltpu.VMEM((TB + 8, XBC), F32), pltpu.VMEM((SCH, LANE), F32), pltpu.VMEM((LANE, SCH), F32),
                        pltpu.VMEM((NG, NS, GW), F32), pltpu.VMEM((SCH, GW), F32), pltpu.VMEM((SCH, GW), F32),
                        pltpu.VMEM((SCH, GW), F32), pltpu.VMEM((SCH, GW), F32), pltpu.VMEM((1, GW), F32),
                        pltpu.VMEM((SCH, XBC), F32)],
        compiler_params=_cp("arbitrary"))(dy, pre, proj, prev, proj, proj, proj, proj, conv_w, dt_bias, a_log, d_skip)


def _rel_index():
    q = jnp.arange(CH)[:, None] + (BAND - CH)
    k = jnp.arange(BAND)[None, :]
    return jnp.clip(q - k, -(CH - 1), REL_CLIP) + (CH - 1)


def _local_step(x, tgt, mods, first_weights, late_weights, grads_ready,
                g_mix, rel_bias, conv_w, conv_b, dt_bias, a_log, d_skip, g_att, g_ssd, g_ffn, g_final):
    sh1, sc1, gt1, sh2, sc2, gt2 = mods
    pad16 = lambda v: jnp.pad(v, ((0, 0), (0, LANE - NH)))
    dtb_p, al_p, dsk_p = pad16(dt_bias), pad16(a_log), pad16(d_skip)
    ridx = _rel_index()
    bias = rel_bias[:, ridx]
    bias_w = _bias_window(bias)
    bias_wt = jnp.swapaxes(bias_w, 1, 2)

    h1 = _norm_fwd(x, g_mix, sc1, sh1, name="norm1_fwd")
    w_in_t = first_weights(h1)
    proj = _mm([(h1, w_in_t)], "NT", F32, "in_proj", 1024, 1152, 2048)
    att = _attn2_fwd(proj, bias_w)
    y, pre, prev = _ssd_fwd(proj, conv_w, conv_b, dtb_p, al_p, dsk_p)
    cat = _mixout_fwd(att, y, proj, g_att, g_ssd)
    w_out, w_gate_t, w_up_t, w_down = late_weights(cat)
    mix = _mm([(cat, w_out)], "NN", F32, "out_proj", 1024, 1024, 2048)
    h2, x1 = _norm_fwd(x, g_ffn, sc2, sh2, mix, gt1, name="norm2_fwd")
    gate, up, act = _gate_up(h2, w_gate_t, w_up_t)
    ffn = _mm([(act, w_down)], "NN", F32, "down_proj", 1024, 1024, 1408)
    dx2, dffn, pf = _final(x1, ffn, gt2, g_final, tgt)

    dact = _mm([(dffn, w_down)], "NT", BF, "d_act", 1024, 1408, 2048)
    gw_down = _mm([(act, dffn)], "TN", BF, "gw_down", 1408, 2048, 512)
    dgate, dup = _swiglu_bwd(gate, up, dact)
    dh2 = _mm([(dgate, w_gate_t), (dup, w_up_t)], "NN", F32, "d_h2", 1024, 1024, 1408)
    gw_gate_t = _mm([(dgate, h2)], "TN", BF, "gw_gate", 1408, 2048, 512)
    gw_up_t = _mm([(dup, h2)], "TN", BF, "gw_up", 1408, 2048, 512)
    zero = grads_ready("ffn", gw_gate_t, gw_up_t, gw_down)
    dx1, dmix, p2 = _norm_bwd(x1, g_ffn + zero, sc2, sh2, dh2, dx2, mix, gt1, name="norm2_bwd")
    gw_out = _mm([(cat, dmix)], "TN", BF, "gw_out", 1024, 2048, 1024)
    zero = grads_ready("out", gw_out)
    dcat = _mm([(dmix, w_out)], "NT", F32, "d_cat", 1024, 1024, 2048)
    datt, dy, dz, pm = _mixout_bwd(dcat, att, y, proj, g_att + zero, g_ssd)
    dq, dk, dv, dbias_t = _attn2_bwd(proj, datt, bias_wt)
    dxbc, ddt, pw, ph = _ssd_bwd(dy, pre, proj, prev, conv_w, dtb_p, al_p, dsk_p)
    dproj = jnp.concatenate([dq, dk, dv, dz, dxbc, ddt], axis=1)
    gw_in_t = _mm([(dproj, h1)], "TN", BF, "gw_in", 1152, 2048, 512)
    zero = grads_ready("in", gw_in_t)
    dh1 = _mm([(dproj, w_in_t)], "NN", F32, "d_h1", 1024, 1024, 1152)
    gx, p1 = _norm_bwd(x, g_mix + zero, sc1, sh1, dh1, dx1, name="norm1_bwd")

    onehot = (ridx.T.reshape(BAND * CH, 1) == jnp.arange(NREL)[None, :]).astype(F32)
    g_rel = _mm([(dbias_t.reshape(NH, BAND * CH), onehot)], "NN", F32, "g_rel_bias", NH, NREL, 4608, precision=HI)

    small = dict(
        dmods=jnp.concatenate([p1[1], p1[0], p2[3], p2[1], p2[0], pf[1]]),
        g_mix=p1[2], conv_b=pw[4], dt_bias=ph[0], a_log=ph[1], d_skip=ph[2], g_att=pm[0], g_ssd=pm[1],
        g_ffn=p2[2], g_final=pf[0], rel_bias=g_rel.reshape(-1), conv_w=pw[0:4].reshape(-1), loss=pf[2])
    return gx, small


def _exchange(srcs, gather, name):
    n = len(srcs)
    outs = [jax.ShapeDtypeStruct((NDEV,) + (s.shape if g else s.shape[1:]), s.dtype) for s, g in zip(srcs, gather)]
    outs.append(jax.ShapeDtypeStruct((8, LANE), F32))

    def body(*refs):
        src, dst, token = refs[:n], refs[n:2 * n], refs[2 * n]
        ssem, rsem, lsem = refs[2 * n + 1:]
        token[...] = jnp.zeros_like(token)
        x, y, c = lax.axis_index("x"), lax.axis_index("y"), lax.axis_index("c")
        me = 4 * x + 2 * y + c
        local = []
        for a in range(n):
            own = src[a] if gather[a] else src[a].at[me]
            cp = pltpu.make_async_copy(own, dst[a].at[me], lsem.at[a])
            cp.start()
            local.append(cp)
        remote = []
        for k in range(1, NDEV):
            px = 1 - x if k & 4 else x
            py = 1 - y if k & 2 else y
            pc = 1 - c if k & 1 else c
            pid = 4 * px + 2 * py + pc
            for a in range(n):
                s_ref = src[a] if gather[a] else src[a].at[pid]
                cp = pltpu.make_async_remote_copy(
                    src_ref=s_ref, dst_ref=dst[a].at[me], send_sem=ssem.at[a * (NDEV - 1) + k - 1],
                    recv_sem=rsem.at[a * (NDEV - 1) + k - 1], device_id=(px, py, pc), device_id_type=MESH)
                cp.start()
                remote.append(cp)
        for cp in remote:
            cp.wait()
        for cp in local:
            cp.wait()

    anyspec = pl.BlockSpec(memory_space=pl.ANY)
    return pl.pallas_call(
        body, name=name, out_shape=outs, in_specs=[anyspec] * n,
        out_specs=[anyspec] * n + [pl.BlockSpec(memory_space=pltpu.VMEM)],
        scratch_shapes=[pltpu.SemaphoreType.DMA((n * (NDEV - 1),)), pltpu.SemaphoreType.DMA((n * (NDEV - 1),)),
                        pltpu.SemaphoreType.DMA((n,))],
    )(*srcs)


_HBM = pl.BlockSpec(memory_space=pltpu.HBM)
_SEM = pl.BlockSpec(memory_space=pltpu.SEMAPHORE)
_EFFECT = pltpu.SideEffectType.DATAFLOW_SIDE_EFFECTING


def _peer_copies(src, land, ssem, rsem, gather):
    x, y, c = lax.axis_index("x"), lax.axis_index("y"), lax.axis_index("c")
    me = 4 * x + 2 * y + c
    copies = []
    for a in range(len(src)):
        for k in range(1, NDEV):
            px = 1 - x if k & 4 else x
            py = 1 - y if k & 2 else y
            pc = 1 - c if k & 1 else c
            s_ref = src[a] if gather[a] else src[a].at[4 * px + 2 * py + pc]
            idx = a * (NDEV - 1) + k - 1
            copies.append(pltpu.make_async_remote_copy(
                src_ref=s_ref, dst_ref=land[a].at[me], send_sem=ssem.at[idx], recv_sem=rsem.at[idx],
                device_id=(px, py, pc), device_id_type=MESH))
    return copies


def _xstart(srcs, gather, name):
    n = len(srcs)
    me = 4 * lax.axis_index("x") + 2 * lax.axis_index("y") + lax.axis_index("c")
    lands = []
    for s_, g_ in zip(srcs, gather):
        own = s_ if g_ else lax.dynamic_index_in_dim(s_, me, 0, keepdims=False)
        empty = lax.empty((NDEV,) + own.shape, own.dtype)
        lands.append(lax.dynamic_update_slice(empty, own[None], (me,) + (0,) * own.ndim))
    nsem = n * (NDEV - 1)

    def body(*refs):
        src, land, ssem, rsem, token = refs[:n], refs[n:2 * n], refs[2 * n], refs[2 * n + 1], refs[-1]
        for cp in _peer_copies(src, land, ssem, rsem, gather):
            cp.start()
        token[...] = jnp.zeros_like(token)

    ops = [pltpu.with_memory_space_constraint(a_, pltpu.HBM) for a_ in list(srcs) + lands]
    return pl.pallas_call(
        body, name=name,
        out_shape=(pltpu.SemaphoreType.DMA((nsem,)), pltpu.SemaphoreType.DMA((nsem,)),
                   *[pltpu.HBM(a_.shape, a_.dtype) for a_ in ops], jax.ShapeDtypeStruct((8, LANE), F32)),
        in_specs=[_HBM] * (2 * n),
        out_specs=(_SEM, _SEM, *[_HBM] * (2 * n), pl.BlockSpec(memory_space=pltpu.VMEM)),
        input_output_aliases={i: 2 + i for i in range(2 * n)},
        compiler_params=pltpu.CompilerParams(has_side_effects=_EFFECT),
    )(*ops)


def _xwait(started, gather, after, name):
    n = len(gather)
    ssem, rsem = started[0], started[1]
    bufs = started[2:2 + 2 * n]

    def body(*refs):
        src, land, ssem_, rsem_ = refs[:n], refs[n:2 * n], refs[2 * n], refs[2 * n + 1]
        for cp in _peer_copies(src, land, ssem_, rsem_, gather):
            cp.wait_send()
            cp.wait_recv()

    out = pl.pallas_call(
        body, name=name, out_shape=tuple(pltpu.HBM(a_.shape, a_.dtype) for a_ in bufs),
        in_specs=[_HBM] * (2 * n) + [_SEM, _SEM, pl.BlockSpec(memory_space=pl.ANY)],
        out_specs=[_HBM] * (2 * n), input_output_aliases={i: i for i in range(2 * n)},
        compiler_params=pltpu.CompilerParams(has_side_effects=_EFFECT),
    )(*bufs, ssem, rsem, after)
    return out[n:]


def _mods(c16, w_shard, b_shard, *, tn=512):
    n = w_shard.shape[1]

    def body(c_ref, w_ref, b_ref, o_ref):
        cv = c_ref[...]
        cond = (cv * jax.nn.sigmoid(cv)).astype(BF)
        o_ref[...] = jnp.dot(cond, w_ref[...].astype(BF), preferred_element_type=F32) + b_ref[...]

    return pl.pallas_call(body, name="ada_mods", grid=(n // tn,),
                          in_specs=[_fix((16, D)), pl.BlockSpec((D, tn), lambda j: (0, j)),
                                    pl.BlockSpec((1, tn), lambda j: (0, j))],
                          out_specs=pl.BlockSpec((16, tn), lambda j: (0, j)),
                          out_shape=jax.ShapeDtypeStruct((16, n), F32), compiler_params=_cp("parallel"))(c16, w_shard, b_shard)


def _gw_ada(c_t, dm, *, tr=256):
    n = dm.shape[1]

    def body(ct_ref, dm_ref, o_ref):
        acc = jnp.zeros((tr, n), F32)
        for b in range(NDEV):
            cv = ct_ref[:, b:b + 1]
            cond = (cv * jax.nn.sigmoid(cv)).astype(BF).astype(F32)
            acc = acc + cond * dm_ref[b:b + 1, :].astype(BF).astype(F32)
        o_ref[...] = acc

    return pl.pallas_call(body, name="gw_ada", grid=(D // tr,),
                          in_specs=[_row(tr, NDEV), _fix((NDEV, n))], out_specs=_row(tr, n),
                          out_shape=jax.ShapeDtypeStruct((D, n), F32), compiler_params=_cp("parallel"))(c_t, dm)


def _sum8(parts, *, name, tc=512):
    _, r, c = parts.shape
    tc = min(tc, c)

    def body(p_ref, o_ref):
        acc = p_ref[0].astype(F32)
        for d in range(1, NDEV):
            acc = acc + p_ref[d].astype(F32)
        o_ref[...] = acc

    return pl.pallas_call(body, name=name, grid=(c // tc,),
                          in_specs=[pl.BlockSpec((NDEV, r, tc), lambda j: (0, 0, j))],
                          out_specs=pl.BlockSpec((r, tc), lambda j: (0, j)),
                          out_shape=jax.ShapeDtypeStruct((r, c), F32), compiler_params=_cp("parallel"))(parts)


def _adam(w, m, v, *, recv=None, grad=None, tr, name):
    r, c = w.shape
    summed = recv is not None
    assert r % tr == 0

    def body(g_ref, w_ref, m_ref, v_ref, go_ref, do_ref, mo_ref, vo_ref):
        if summed:
            g = g_ref[0].astype(F32)
            for d in range(1, NDEV):
                g = g + g_ref[d].astype(F32)
        else:
            g = g_ref[...]
        go_ref[...] = g
        mn = B1 * m_ref[...] + (1.0 - B1) * g
        vn = B2 * v_ref[...] + (1.0 - B2) * (g * g)
        mo_ref[...] = mn
        vo_ref[...] = vn
        mh = mn / (1.0 - B1 ** STEP)
        vh = vn / (1.0 - B2 ** STEP)
        do_ref[...] = -LR * (mh / (jnp.sqrt(vh) + AEPS) + WD * w_ref[...])

    gspec = pl.BlockSpec((NDEV, tr, c), lambda i: (0, i, 0)) if summed else _row(tr, c)
    return pl.pallas_call(body, name=name, grid=(r // tr,),
                          in_specs=[gspec, _row(tr, c), _row(tr, c), _row(tr, c)], out_specs=[_row(tr, c)] * 4,
                          out_shape=[jax.ShapeDtypeStruct((r, c), F32)] * 4,
                          compiler_params=_cp("parallel"))(recv if summed else grad, w, m, v)


def _cols_to_blocks(g, width):
    r = g.shape[0]
    return jnp.transpose(g.reshape(r, NDEV, width), (1, 0, 2))


def _blocks_to_cols(b):
    _, r, width = b.shape
    return jnp.transpose(b, (1, 0, 2)).reshape(r, NDEV * width)


_SMALL = ("b_ada", "g_mix", "rel_bias", "conv_w", "conv_b", "dt_bias", "a_log", "d_skip",
          "g_att_out", "g_ssd_out", "g_ffn", "g_final")


def _pad_rows(v):
    v = v.reshape(-1)
    return jnp.pad(v, (0, (-v.shape[0]) % LANE))


def kernel(x, c, w_ada, b_ada, g_mix, w_in, rel_bias, conv_w, conv_b, dt_bias, a_log, d_skip, g_att_out, g_ssd_out, w_out, g_ffn, w_gate, w_up, w_down, g_final, loss_target, m_w_ada, m_b_ada, m_g_mix, m_w_in, m_rel_bias, m_conv_w, m_conv_b, m_dt_bias, m_a_log, m_d_skip, m_g_att_out, m_g_ssd_out, m_w_out, m_g_ffn, m_w_gate, m_w_up, m_w_down, m_g_final, v_w_ada, v_b_ada, v_g_mix, v_w_in, v_rel_bias, v_conv_w, v_conv_b, v_dt_bias, v_a_log, v_d_skip, v_g_att_out, v_g_ssd_out, v_w_out, v_g_ffn, v_w_gate, v_w_up, v_w_down, v_g_final):
    me = 4 * lax.axis_index("x") + 2 * lax.axis_index("y") + lax.axis_index("c")
    wts = dict(w_ada=w_ada, b_ada=b_ada, g_mix=g_mix, w_in=w_in, rel_bias=rel_bias, conv_w=conv_w, conv_b=conv_b,
               dt_bias=dt_bias, a_log=a_log, d_skip=d_skip, g_att_out=g_att_out, g_ssd_out=g_ssd_out, w_out=w_out,
               g_ffn=g_ffn, w_gate=w_gate, w_up=w_up, w_down=w_down, g_final=g_final)
    mom = dict(w_ada=m_w_ada, b_ada=m_b_ada, g_mix=m_g_mix, w_in=m_w_in, rel_bias=m_rel_bias, conv_w=m_conv_w,
               conv_b=m_conv_b, dt_bias=m_dt_bias, a_log=m_a_log, d_skip=m_d_skip, g_att_out=m_g_att_out,
               g_ssd_out=m_g_ssd_out, w_out=m_w_out, g_ffn=m_g_ffn, w_gate=m_w_gate, w_up=m_w_up, w_down=m_w_down,
               g_final=m_g_final)
    var = dict(w_ada=v_w_ada, b_ada=v_b_ada, g_mix=v_g_mix, w_in=v_w_in, rel_bias=v_rel_bias, conv_w=v_conv_w,
               conv_b=v_conv_b, dt_bias=v_dt_bias, a_log=v_a_log, d_skip=v_d_skip, g_att_out=v_g_att_out,
               g_ssd_out=v_g_ssd_out, w_out=v_w_out, g_ffn=v_g_ffn, w_gate=v_w_gate, w_up=v_w_up, w_down=v_w_down,
               g_final=v_g_final)
    order = ("w_ada", "b_ada", "g_mix", "w_in", "rel_bias", "conv_w", "conv_b", "dt_bias", "a_log", "d_skip",
             "g_att_out", "g_ssd_out", "w_out", "g_ffn", "w_gate", "w_up", "w_down", "g_final")

    c_all, rel_all, cw_all, _ = _exchange([c, rel_bias[0], conv_w[0]], [True] * 3, "gather_small")
    c_all = c_all.reshape(NDEV, D)
    rel_full = _blocks_to_cols(rel_all)
    cw_full = _blocks_to_cols(cw_all)

    ncol = w_ada.shape[2]
    b_sh = lax.dynamic_slice(b_ada, (0, me * ncol), (1, ncol))
    mods_part = _mods(jnp.pad(c_all, ((0, 8), (0, 0))), w_ada[0], b_sh)[:NDEV]
    mods_all, tok_m = _exchange([mods_part], [True], "gather_mods")
    mods_me = lax.dynamic_index_in_dim(mods_all, me, axis=1, keepdims=False).reshape(6, 1, D)
    mods = [mods_me[i] for i in range(6)]

    st_in = _xstart([(w_in[0] + tok_m[0, 0]).T.astype(BF)], [True], "gather_w_in_start")
    st_rest = _xstart([(w_out[0] + st_in[-1][0, 0]).astype(BF), w_gate[0].T.astype(BF), w_up[0].T.astype(BF),
                       w_down[0].astype(BF)], [True] * 4, "gather_w_rest_start")

    def first_weights(after):
        (gi,) = _xwait(st_in, [True], after, "gather_w_in_wait")
        return jnp.pad(gi.reshape(INC, D), ((0, INP - INC), (0, 0)))

    def late_weights(after):
        go, gg, gu_, gd = _xwait(st_rest, [True] * 4, after, "gather_w_rest_wait")
        return go.reshape(D, D), gg.reshape(FF, D), gu_.reshape(FF, D), gd.reshape(FF, D)

    started = {}

    def grads_ready(which, *g):
        if which == "in":
            g = (g[0][:INC],)
        srcs = [g_.reshape(NDEV, g_.shape[0] // NDEV, D) for g_ in g]
        started[which] = _xstart(srcs, [False] * len(srcs), f"exchange_{which}_grads_start")
        return started[which][-1][0, 0]

    gx, small = _local_step(
        x[0], loss_target[0], mods, first_weights, late_weights, grads_ready,
        g_mix, rel_full, cw_full, conv_b, dt_bias, a_log, d_skip, g_att_out, g_ssd_out, g_ffn,
        g_final.reshape(1, D))

    loss_row = jnp.pad(jnp.sum(small["loss"]).reshape(1), (0, LANE - 1))
    pack = jnp.concatenate([small[k] for k in ("dmods", "g_mix", "conv_b", "dt_bias", "a_log", "d_skip", "g_att",
                                               "g_ssd", "g_ffn", "g_final", "rel_bias", "conv_w")] + [loss_row])
    packs, _ = _exchange([pack.reshape(-1, LANE)], [True], "gather_small_grads")
    rg, ru, rd = _xwait(started["ffn"], [False] * 3, packs, "exchange_ffn_grads_wait")
    (ro,) = _xwait(started["out"], [False], rg, "exchange_out_grads_wait")
    (ri,) = _xwait(started["in"], [False], ro, "exchange_in_grads_wait")
    big = dict(
        w_gate=_adam(w_gate[0], m_w_gate[0], v_w_gate[0], grad=_sum8(rg, name="sum_w_gate").T, tr=256,
                     name="adam_w_gate"),
        w_up=_adam(w_up[0], m_w_up[0], v_w_up[0], grad=_sum8(ru, name="sum_w_up").T, tr=256, name="adam_w_up"),
        w_down=_adam(w_down[0], m_w_down[0], v_w_down[0], recv=rd, tr=176, name="adam_w_down"))
    tot = _sum8(packs, name="sum_small").reshape(-1)

    sizes = dict(dmods=6 * D, g_mix=D, conv_b=XBC, dt_bias=LANE, a_log=LANE, d_skip=LANE, g_att_out=AW, g_ssd_out=SW,
                 g_ffn=D, g_final=D, rel_bias=NH * NREL, conv_w=4 * XBC, loss=LANE)
    off, o = {}, 0
    for k_, n_ in sizes.items():
        off[k_] = o
        o += n_
    take = lambda k_, n_=None: tot[off[k_]:off[k_] + (n_ or sizes[k_])]
    loss = tot[off["loss"]]
    rel_w = NREL // NDEV
    cw_w = XBC // NDEV
    gsm = dict(
        b_ada=take("dmods"), g_mix=take("g_mix"),
        rel_bias=lax.dynamic_slice(take("rel_bias").reshape(NH, NREL), (0, me * rel_w), (NH, rel_w)),
        conv_w=lax.dynamic_slice(take("conv_w").reshape(4, XBC), (0, me * cw_w), (4, cw_w)),
        conv_b=take("conv_b"), dt_bias=take("dt_bias", NH), a_log=take("a_log", NH), d_skip=take("d_skip", NH),
        g_att_out=take("g_att_out"), g_ssd_out=take("g_ssd_out"), g_ffn=take("g_ffn"), g_final=take("g_final"))

    flat = lambda d_: jnp.concatenate([_pad_rows(d_[k_]) for k_ in _SMALL])
    gflat = flat(gsm)
    nrow = -(-gflat.shape[0] // (8 * LANE)) * 8
    to2d = lambda v_: jnp.pad(v_, (0, nrow * LANE - v_.shape[0])).reshape(nrow, LANE)
    res = _adam(to2d(flat(wts)), to2d(flat(mom)), to2d(flat(var)), grad=to2d(gflat), tr=nrow, name="adam_small")
    outs = {k_: {} for k_ in ("grad", "delta", "m", "v")}
    o = 0
    for k_ in _SMALL:
        n_ = wts[k_].size
        for kind, arr in zip(("grad", "delta", "m", "v"), res):
            outs[kind][k_] = arr.reshape(-1)[o:o + n_].reshape(wts[k_].shape)
        o += n_ + (-n_) % LANE

    dm_all = packs[:, :6 * D // LANE, :].reshape(NDEV, 6 * D)
    dm_sh = lax.dynamic_slice(dm_all, (0, me * ncol), (NDEV, ncol))
    gwa = _gw_ada(c_all.T, dm_sh)
    big.update(
        w_ada=_adam(w_ada[0], m_w_ada[0], v_w_ada[0], grad=gwa, tr=256, name="adam_w_ada"),
        w_in=_adam(w_in[0], m_w_in[0], v_w_in[0], grad=_sum8(ri, name="sum_w_in").T, tr=256, name="adam_w_in"),
        w_out=_adam(w_out[0], m_w_out[0], v_w_out[0], recv=ro, tr=128, name="adam_w_out"))
    for k_, r_ in big.items():
        for kind, arr in zip(("grad", "delta", "m", "v"), r_):
            outs[kind][k_] = arr.reshape(wts[k_].shape)

    return (loss, gx.reshape(x.shape), *[outs["grad"][k_] for k_ in order], *[outs["delta"][k_] for k_ in order],
            *[outs["m"][k_] for k_ in order], *[outs["v"][k_] for k_ in order])
```

```python
import functools

import jax
import jax.numpy as jnp
from jax import lax
from jax.experimental import pallas as pl
from jax.experimental.pallas import tpu as pltpu

F32 = jnp.float32
BF = jnp.bfloat16
HI = lax.Precision.HIGHEST

D = 2048
NH = 16
HD = 64
AW = 1024
SW = 1024
NG = 2
NS = 128
XBC = 1536
CH = 64
BAND = 576
NREL = 320
REL_CLIP = 256
FF = 5632
INC = 5648
INP = 5760
EPS = 1e-6
NDEV = 8
LANE = 128
VMEM_LIMIT = 56 * 1024 * 1024
LR, B1, B2, AEPS, WD, STEP = 0.001, 0.9, 0.999, 1e-08, 0.01, 10
MESH = pl.DeviceIdType.MESH


def _cp(*sem):
    return pltpu.CompilerParams(dimension_semantics=sem, vmem_limit_bytes=VMEM_LIMIT)


def _row(ts, w, col=0):
    return pl.BlockSpec((ts, w), lambda i, _c=col: (i, _c))


def _fix(shape, single=False):
    nd = len(shape)
    if single:
        return pl.BlockSpec(shape, lambda i, _n=nd: (0,) * _n, pipeline_mode=pl.Buffered(1))
    return pl.BlockSpec(shape, lambda i, _n=nd: (0,) * _n)


def _silu_grad(x, s):
    return s * (1.0 + x * (1.0 - s))


def _softplus(x):
    return jnp.maximum(x, 0.0) + jnp.log1p(jnp.exp(-jnp.abs(x)))


_DIMS = {"NN": (((1,), (0,)), ((), ())), "TN": (((0,), (0,)), ((), ())), "NT": (((1,), (1,)), ((), ()))}


def _mm(pairs, mode, out_dtype, name, tm, tn, tk, precision=None, after=None):
    a0, b0 = pairs[0]
    m, k = a0.shape[::-1] if mode == "TN" else a0.shape
    n = b0.shape[0] if mode == "NT" else b0.shape[1]
    tm, tn, tk = min(tm, m), min(tn, n), min(tk, k)
    nk1 = k // tk
    npair = len(pairs)
    nk = nk1 * npair
    assert m % tm == 0 and n % tn == 0 and k % tk == 0, (name, a0.shape, b0.shape)
    dims = _DIMS[mode]

    def kloc(kk, p):
        return jnp.clip(kk - p * nk1, 0, nk1 - 1)

    def aspec(p):
        if mode == "TN":
            return pl.BlockSpec((tk, tm), lambda i, j, kk, _p=p: (kloc(kk, _p), i))
        return pl.BlockSpec((tm, tk), lambda i, j, kk, _p=p: (i, kloc(kk, _p)))

    def bspec(p):
        if mode == "NT":
            return pl.BlockSpec((tn, tk), lambda i, j, kk, _p=p: (j, kloc(kk, _p)))
        return pl.BlockSpec((tk, tn), lambda i, j, kk, _p=p: (kloc(kk, _p), j))

    nin = 2 * npair + (after is not None)

    def body(*refs):
        ab, o_ref = refs[:2 * npair], refs[nin]
        if nk == 1:
            o_ref[...] = lax.dot_general(ab[0][...], ab[1][...], dims, preferred_element_type=F32,
                                         precision=precision).astype(o_ref.dtype)
            return
        acc = refs[nin + 1]
        kk = pl.program_id(2)

        @pl.when(kk == 0)
        def _():
            acc[...] = jnp.zeros_like(acc)

        for p in range(npair):
            @pl.when((kk >= p * nk1) & (kk < (p + 1) * nk1))
            def _(p=p):
                acc[...] += lax.dot_general(ab[2 * p][...], ab[2 * p + 1][...], dims, preferred_element_type=F32,
                                            precision=precision)

        @pl.when(kk == nk - 1)
        def _():
            o_ref[...] = acc[...].astype(o_ref.dtype)

    in_specs = []
    for p in range(npair):
        in_specs += [aspec(p), bspec(p)]
    operands = [t for ab_ in pairs for t in ab_]
    if after is not None:
        in_specs.append(pl.BlockSpec(memory_space=pl.ANY))
        operands.append(after)
    return pl.pallas_call(
        body, name=name, grid=(m // tm, n // tn, nk), in_specs=in_specs,
        out_specs=pl.BlockSpec((tm, tn), lambda i, j, kk: (i, j)),
        out_shape=jax.ShapeDtypeStruct((m, n), out_dtype),
        scratch_shapes=[] if nk == 1 else [pltpu.VMEM((tm, tn), F32)],
        compiler_params=_cp("parallel", "parallel", "arbitrary"),
    )(*operands)


def _norm_fwd(x, g, sc, sh, mix=None, gt=None, *, name, ts=256):
    s, d = x.shape
    has = mix is not None

    def body(*refs):
        if has:
            x_ref, g_ref, sc_ref, sh_ref, mix_ref, gt_ref, h_ref, x1_ref = refs
        else:
            x_ref, g_ref, sc_ref, sh_ref, h_ref = refs
        xv = x_ref[...]
        if has:
            xv = xv + gt_ref[...] * mix_ref[...]
            x1_ref[...] = xv
        r = lax.rsqrt(jnp.mean(xv * xv, axis=-1, keepdims=True) + EPS)
        h_ref[...] = ((xv * r) * g_ref[...] * (1.0 + sc_ref[...]) + sh_ref[...]).astype(BF)

    ins = [x, g, sc, sh] + ([mix, gt] if has else [])
    in_specs = [_row(ts, d), _fix((1, d)), _fix((1, d)), _fix((1, d))] + ([_row(ts, d), _fix((1, d))] if has else [])
    out_shape = [jax.ShapeDtypeStruct((s, d), BF)] + ([jax.ShapeDtypeStruct((s, d), F32)] if has else [])
    out_specs = [_row(ts, d)] + ([_row(ts, d)] if has else [])
    out = pl.pallas_call(body, name=name, grid=(s // ts,), in_specs=in_specs, out_specs=out_specs,
                         out_shape=out_shape, compiler_params=_cp("parallel"))(*ins)
    return out if has else out[0]


def _norm_bwd(x, g, sc, sh, dh, dres, mix=None, gt=None, *, name, ts=256):
    s, d = x.shape
    has = mix is not None

    def body(*refs):
        if has:
            x_ref, g_ref, sc_ref, sh_ref, dh_ref, dres_ref, mix_ref, gt_ref, dx_ref, dmix_ref, part_ref = refs
        else:
            x_ref, g_ref, sc_ref, sh_ref, dh_ref, dres_ref, dx_ref, part_ref = refs
        i = pl.program_id(0)
        xv = x_ref[...]
        gv = g_ref[...]
        dhv = dh_ref[...]
        r = lax.rsqrt(jnp.mean(xv * xv, axis=-1, keepdims=True) + EPS)
        xh = xv * r
        dyn = dhv * (1.0 + sc_ref[...])
        gy = dyn * gv
        dx = dres_ref[...] + r * (gy - xh * jnp.mean(gy * xh, axis=-1, keepdims=True))
        dx_ref[...] = dx

        @pl.when(i == 0)
        def _():
            part_ref[...] = jnp.zeros_like(part_ref)

        part_ref[0:1, :] += jnp.sum(dhv * (xh * gv), axis=0, keepdims=True)
        part_ref[1:2, :] += jnp.sum(dhv, axis=0, keepdims=True)
        part_ref[2:3, :] += jnp.sum(dyn * xh, axis=0, keepdims=True)
        if has:
            part_ref[3:4, :] += jnp.sum(dx * mix_ref[...], axis=0, keepdims=True)
            dmix_ref[...] = (gt_ref[...] * dx).astype(BF)

    ins = [x, g, sc, sh, dh, dres] + ([mix, gt] if has else [])
    in_specs = [_row(ts, d), _fix((1, d)), _fix((1, d)), _fix((1, d)), _row(ts, d), _row(ts, d)]
    in_specs += [_row(ts, d), _fix((1, d))] if has else []
    out_shape = [jax.ShapeDtypeStruct((s, d), F32)] + ([jax.ShapeDtypeStruct((s, d), BF)] if has else [])
    out_shape += [jax.ShapeDtypeStruct((8, d), F32)]
    out_specs = [_row(ts, d)] + ([_row(ts, d)] if has else []) + [_fix((8, d))]
    return pl.pallas_call(body, name=name, grid=(s // ts,), in_specs=in_specs, out_specs=out_specs,
                          out_shape=out_shape, compiler_params=_cp("arbitrary"))(*ins)


def _final(x1, ffn, gt2, gf, tgt, *, ts=256):
    s, d = x1.shape

    def body(x1_ref, ffn_ref, gt_ref, g_ref, t_ref, dx_ref, dffn_ref, part_ref):
        i = pl.program_id(0)
        fv = ffn_ref[...]
        gv = g_ref[...]
        xv = x1_ref[...] + gt_ref[...] * fv
        r = lax.rsqrt(jnp.mean(xv * xv, axis=-1, keepdims=True) + EPS)
        xh = xv * r
        e = xh * gv - t_ref[...]
        dy = e * (1.0 / d)
        gy = dy * gv
        dx = r * (gy - xh * jnp.mean(gy * xh, axis=-1, keepdims=True))
        dx_ref[...] = dx
        dffn_ref[...] = (gt_ref[...] * dx).astype(BF)

        @pl.when(i == 0)
        def _():
            part_ref[...] = jnp.zeros_like(part_ref)

        part_ref[0:1, :] += jnp.sum(dy * xh, axis=0, keepdims=True)
        part_ref[1:2, :] += jnp.sum(dx * fv, axis=0, keepdims=True)
        part_ref[2:3, :] += jnp.sum(e * e, axis=0, keepdims=True) * (0.5 / d)

    return pl.pallas_call(
        body, name="final_loss", grid=(s // ts,),
        in_specs=[_row(ts, d), _row(ts, d), _fix((1, d)), _fix((1, d)), _row(ts, d)],
        out_specs=[_row(ts, d), _row(ts, d), _fix((8, d))],
        out_shape=[jax.ShapeDtypeStruct((s, d), F32), jax.ShapeDtypeStruct((s, d), BF),
                   jax.ShapeDtypeStruct((8, d), F32)],
        compiler_params=_cp("arbitrary"))(x1, ffn, gt2, gf, tgt)


def _gate_up(h, wg_t, wu_t, *, tm=1024, tn=512):
    s, k = h.shape
    n = wg_t.shape[0]
    tm = min(tm, s)

    def body(h_ref, wg_ref, wu_ref, g_ref, u_ref, a_ref):
        hv = h_ref[...]
        gv = lax.dot_general(hv, wg_ref[...], _DIMS["NT"], preferred_element_type=F32)
        uv = lax.dot_general(hv, wu_ref[...], _DIMS["NT"], preferred_element_type=F32)
        g_ref[...] = gv.astype(BF)
        u_ref[...] = uv.astype(BF)
        a_ref[...] = (gv * jax.nn.sigmoid(gv) * uv).astype(BF)

    wspec = pl.BlockSpec((tn, k), lambda i, j: (j, 0))
    ospec = pl.BlockSpec((tm, tn), lambda i, j: (i, j))
    return pl.pallas_call(body, name="gate_up", grid=(s // tm, n // tn),
                          in_specs=[pl.BlockSpec((tm, k), lambda i, j: (i, 0)), wspec, wspec],
                          out_specs=[ospec] * 3, out_shape=[jax.ShapeDtypeStruct((s, n), BF)] * 3,
                          compiler_params=_cp("parallel", "parallel"))(h, wg_t, wu_t)


def _swiglu_bwd(gate, up, dact, *, ts=256):
    s = gate.shape[0]

    def body(g_ref, u_ref, da_ref, dg_ref, du_ref):
        gv = g_ref[...].astype(F32)
        uv = u_ref[...].astype(F32)
        dav = da_ref[...].astype(F32)
        sg = jax.nn.sigmoid(gv)
        dg_ref[...] = (dav * uv * _silu_grad(gv, sg)).astype(BF)
        du_ref[...] = (dav * gv * sg).astype(BF)

    return pl.pallas_call(body, name="swiglu_bwd", grid=(s // ts,),
                          in_specs=[_row(ts, FF)] * 3, out_specs=[_row(ts, FF)] * 2,
                          out_shape=[jax.ShapeDtypeStruct((s, FF), BF)] * 2,
                          compiler_params=_cp("parallel"))(gate, up, dact)


def _mixout_fwd(att, y, proj, g_att, g_ssd, *, ts=256):
    s = att.shape[0]

    def body(a_ref, y_ref, z_ref, ga_ref, gs_ref, o_ref):
        av = a_ref[...]
        ra = lax.rsqrt(jnp.mean(av * av, axis=-1, keepdims=True) + EPS)
        o_ref[:, 0:AW] = (av * ra * ga_ref[...]).astype(BF)
        zv = z_ref[...]
        yz = y_ref[...] * (zv * jax.nn.sigmoid(zv))
        rs = lax.rsqrt(jnp.mean(yz * yz, axis=-1, keepdims=True) + EPS)
        o_ref[:, AW:AW + SW] = (yz * rs * gs_ref[...]).astype(BF)

    return pl.pallas_call(body, name="mixout_fwd", grid=(s // ts,),
                          in_specs=[_row(ts, AW), _row(ts, SW), _row(ts, SW, 3), _fix((1, AW)), _fix((1, SW))],
                          out_specs=_row(ts, AW + SW), out_shape=jax.ShapeDtypeStruct((s, AW + SW), BF),
                          compiler_params=_cp("parallel"))(att, y, proj, g_att, g_ssd)


def _mixout_bwd(dcat, att, y, proj, g_att, g_ssd, *, ts=256):
    s = att.shape[0]

    def body(dc_ref, a_ref, y_ref, z_ref, ga_ref, gs_ref, da_ref, dy_ref, dz_ref, part_ref):
        i = pl.program_id(0)
        av = a_ref[...]
        dca = dc_ref[:, 0:AW]
        ra = lax.rsqrt(jnp.mean(av * av, axis=-1, keepdims=True) + EPS)
        ah = av * ra
        gy = dca * ga_ref[...]
        da_ref[...] = ra * (gy - ah * jnp.mean(gy * ah, axis=-1, keepdims=True))
        zv = z_ref[...]
        yv = y_ref[...]
        sg = jax.nn.sigmoid(zv)
        sz = zv * sg
        yz = yv * sz
        dcs = dc_ref[:, AW:AW + SW]
        rs = lax.rsqrt(jnp.mean(yz * yz, axis=-1, keepdims=True) + EPS)
        yh = yz * rs
        gys = dcs * gs_ref[...]
        dyz = rs * (gys - yh * jnp.mean(gys * yh, axis=-1, keepdims=True))
        dy_ref[...] = dyz * sz
        dz_ref[...] = (dyz * yv * _silu_grad(zv, sg)).astype(BF)

        @pl.when(i == 0)
        def _():
            part_ref[...] = jnp.zeros_like(part_ref)

        part_ref[0:1, :] += jnp.sum(dca * ah, axis=0, keepdims=True)
        part_ref[1:2, :] += jnp.sum(dcs * yh, axis=0, keepdims=True)

    return pl.pallas_call(
        body, name="mixout_bwd", grid=(s // ts,),
        in_specs=[_row(ts, AW + SW), _row(ts, AW), _row(ts, SW), _row(ts, SW, 3), _fix((1, AW)), _fix((1, SW))],
        out_specs=[_row(ts, AW), _row(ts, SW), _row(ts, SW), _fix((8, AW))],
        out_shape=[jax.ShapeDtypeStruct((s, AW), F32), jax.ShapeDtypeStruct((s, SW), F32),
                   jax.ShapeDtypeStruct((s, SW), BF), jax.ShapeDtypeStruct((8, AW), F32)],
        compiler_params=_cp("arbitrary"))(dcat, att, y, proj, g_att, g_ssd)


TQ = 512
SCALE = HD ** -0.5


def _attn_fwd(qkv, bias):
    s = qkv.shape[0]
    nb = s // TQ
    prev = lambda i: jnp.maximum(i - 1, 0)

    def body(q_ref, kp_ref, kc_ref, vp_ref, vc_ref, b_ref, o_ref, kbuf, vbuf):
        i = pl.program_id(0)
        kbuf[0:TQ, :] = kp_ref[...]
        kbuf[TQ:2 * TQ, :] = kc_ref[...]
        vbuf[0:TQ, :] = vp_ref[...]
        vbuf[TQ:2 * TQ, :] = vc_ref[...]

        def chunk(c, carry):
            r0 = pl.multiple_of(c * CH, CH)
            kpos = i * TQ + r0 - TQ + lax.broadcasted_iota(jnp.int32, (1, BAND), 1)
            valid = kpos >= 0
            for h in range(NH):
                sl = slice(HD * h, HD * h + HD)
                q = q_ref[pl.ds(r0, CH), sl]
                k = kbuf[pl.ds(r0, BAND), sl]
                v = vbuf[pl.ds(r0, BAND), sl]
                sc = lax.dot_general(q, k, (((1,), (1,)), ((), ())), preferred_element_type=F32) * SCALE + b_ref[h]
                sc = jnp.where(valid, sc, -jnp.inf)
                e = jnp.exp(sc - jnp.max(sc, axis=-1, keepdims=True))
                p = e / jnp.sum(e, axis=-1, keepdims=True)
                o_ref[pl.ds(r0, CH), sl] = jnp.dot(p.astype(BF), v, preferred_element_type=F32)
            return carry

        lax.fori_loop(0, TQ // CH, chunk, 0)

    blk = lambda col, im: pl.BlockSpec((TQ, AW), lambda i, _c=col, _f=im: (_f(i), _c))
    cur = lambda i: i
    return pl.pallas_call(
        body, name="attn_fwd", grid=(nb,),
        in_specs=[blk(0, cur), blk(1, prev), blk(1, cur), blk(2, prev), blk(2, cur), _fix((NH, CH, BAND))],
        out_specs=_row(TQ, AW), out_shape=jax.ShapeDtypeStruct((s, AW), F32),
        scratch_shapes=[pltpu.VMEM((2 * TQ, AW), BF), pltpu.VMEM((2 * TQ, AW), BF)],
        compiler_params=_cp("parallel"))(qkv, qkv, qkv, qkv, qkv, bias)


def _attn_bwd(qkv, datt, bias_t):
    s = qkv.shape[0]
    nb = s // TQ
    cur = lambda i: jnp.minimum(i, nb - 1)
    prev = lambda i: jnp.maximum(jnp.minimum(i, nb - 1) - 1, 0)
    late = lambda i: jnp.maximum(i - 1, 0)

    def body(q_ref, kp_ref, kc_ref, vp_ref, vc_ref, do_ref, b_ref, dq_ref, dk_ref, dv_ref, db_ref,
             kbuf, vbuf, dkacc, dvacc):
        i = pl.program_id(0)

        @pl.when(i == 0)
        def _():
            dkacc[...] = jnp.zeros_like(dkacc)
            dvacc[...] = jnp.zeros_like(dvacc)
            db_ref[...] = jnp.zeros_like(db_ref)

        @pl.when(i < nb)
        def _():
            kbuf[0:TQ, :] = kp_ref[...]
            kbuf[TQ:2 * TQ, :] = kc_ref[...]
            vbuf[0:TQ, :] = vp_ref[...]
            vbuf[TQ:2 * TQ, :] = vc_ref[...]

            def chunk(c, carry):
                r0 = pl.multiple_of(c * CH, CH)
                kpos = i * TQ + r0 - TQ + lax.broadcasted_iota(jnp.int32, (BAND, 1), 0)
                valid = kpos >= 0
                for h in range(NH):
                    sl = slice(HD * h, HD * h + HD)
                    q = q_ref[pl.ds(r0, CH), sl]
                    k = kbuf[pl.ds(r0, BAND), sl]
                    v = vbuf[pl.ds(r0, BAND), sl]
                    do = do_ref[pl.ds(r0, CH), sl].astype(BF)
                    st = lax.dot_general(k, q, (((1,), (1,)), ((), ())), preferred_element_type=F32) * SCALE + b_ref[h]
                    st = jnp.where(valid, st, -jnp.inf)
                    e = jnp.exp(st - jnp.max(st, axis=0, keepdims=True))
                    pt = e / jnp.sum(e, axis=0, keepdims=True)
                    dvacc[pl.ds(r0, BAND), sl] += jnp.dot(pt.astype(BF), do, preferred_element_type=F32)
                    dpt = lax.dot_general(v, do, (((1,), (1,)), ((), ())), preferred_element_type=F32)
                    dst = pt * (dpt - jnp.sum(dpt * pt, axis=0, keepdims=True))
                    db_ref[h] += dst
                    dsb = dst.astype(BF)
                    dkacc[pl.ds(r0, BAND), sl] += jnp.dot(dsb, q, preferred_element_type=F32) * SCALE
                    dq = lax.dot_general(dsb, k, (((0,), (0,)), ((), ())), preferred_element_type=F32) * SCALE
                    dq_ref[pl.ds(r0, CH), sl] = dq.astype(BF)
                return carry

            lax.fori_loop(0, TQ // CH, chunk, 0)

        dk_ref[...] = dkacc[0:TQ, :].astype(BF)
        dv_ref[...] = dvacc[0:TQ, :].astype(BF)
        dkacc[0:TQ, :] = dkacc[TQ:2 * TQ, :]
        dvacc[0:TQ, :] = dvacc[TQ:2 * TQ, :]
        dkacc[TQ:2 * TQ, :] = jnp.zeros((TQ, AW), F32)
        dvacc[TQ:2 * TQ, :] = jnp.zeros((TQ, AW), F32)

    blk = lambda col, im: pl.BlockSpec((TQ, AW), lambda i, _c=col, _f=im: (_f(i), _c))
    return pl.pallas_call(
        body, name="attn_bwd", grid=(nb + 1,),
        in_specs=[blk(0, cur), blk(1, prev), blk(1, cur), blk(2, prev), blk(2, cur), blk(0, cur),
                  _fix((NH, BAND, CH))],
        out_specs=[blk(0, cur), blk(0, late), blk(0, late), _fix((NH, BAND, CH))],
        out_shape=[jax.ShapeDtypeStruct((s, AW), BF)] * 3 + [jax.ShapeDtypeStruct((NH, BAND, CH), F32)],
        scratch_shapes=[pltpu.VMEM((2 * TQ, AW), BF), pltpu.VMEM((2 * TQ, AW), BF),
                        pltpu.VMEM((2 * TQ, AW), F32), pltpu.VMEM((2 * TQ, AW), F32)],
        compiler_params=_cp("arbitrary"))(qkv, qkv, qkv, qkv, qkv, datt, bias_t)


QB = 256
NQC = QB // CH
WIN = 3 * QB


def _bias_window(bias):
    rows = [jnp.pad(bias, ((0, 0), (0, 0), (CH * a, WIN - BAND - CH * a)), constant_values=-jnp.inf)
            for a in range(NQC)]
    return jnp.concatenate(rows, axis=1)


def _win_specs(width, nb):
    def mk(col, back):
        return pl.BlockSpec((QB, width), lambda i, _c=col, _k=back: (jnp.maximum(jnp.minimum(i, nb - 1) - _k, 0), _c))
    return [mk(1, 2), mk(1, 1), mk(1, 0), mk(2, 2), mk(2, 1), mk(2, 0)]


def _attn2_fwd(qkv, bias_w):
    s = qkv.shape[0]
    nb = s // QB

    def body(q_ref, k0, k1, k2, v0, v1, v2, b_ref, o_ref, kbuf, vbuf):
        i = pl.program_id(0)
        for j, (kr, vr) in enumerate(((k0, v0), (k1, v1), (k2, v2))):
            kbuf[QB * j:QB * j + QB, :] = kr[...].astype(BF)
            vbuf[QB * j:QB * j + QB, :] = vr[...].astype(BF)
        valid = (i * QB - 2 * QB + lax.broadcasted_iota(jnp.int32, (1, WIN), 1)) >= 0
        for h in range(NH):
            sl = slice(HD * h, HD * h + HD)
            sc = lax.dot_general(q_ref[:, sl].astype(BF), kbuf[:, sl], _DIMS["NT"], preferred_element_type=F32)
            sc = jnp.where(valid, sc * SCALE + b_ref[h], -jnp.inf)
            e = jnp.exp(sc - jnp.max(sc, axis=-1, keepdims=True))
            p = e * (1.0 / jnp.sum(e, axis=-1, keepdims=True))
            o_ref[:, sl] = jnp.dot(p.astype(BF), vbuf[:, sl], preferred_element_type=F32)

    return pl.pallas_call(
        body, name="attn_fwd", grid=(nb,),
        in_specs=[_row(QB, AW)] + _win_specs(AW, nb) + [_fix((NH, QB, WIN), single=True)],
        out_specs=_row(QB, AW), out_shape=jax.ShapeDtypeStruct((s, AW), F32),
        scratch_shapes=[pltpu.VMEM((WIN, AW), BF), pltpu.VMEM((WIN, AW), BF)],
        compiler_params=_cp("parallel"))(qkv, qkv, qkv, qkv, qkv, qkv, qkv, bias_w)


def _attn2_bwd(qkv, datt, bias_wt):
    s = qkv.shape[0]
    nb = s // QB
    cur = lambda i: jnp.minimum(i, nb - 1)
    late = lambda i: jnp.maximum(i - 2, 0)

    def body(q_ref, k0, k1, k2, v0, v1, v2, do_ref, b_ref, dq_ref, dk_ref, dv_ref, db_ref,
             kbuf, vbuf, dkacc, dvacc, dsbuf):
        i = pl.program_id(0)

        @pl.when(i == 0)
        def _():
            dkacc[...] = jnp.zeros_like(dkacc)
            dvacc[...] = jnp.zeros_like(dvacc)
            db_ref[...] = jnp.zeros_like(db_ref)

        @pl.when(i < nb)
        def _():
            for j, (kr, vr) in enumerate(((k0, v0), (k1, v1), (k2, v2))):
                kbuf[QB * j:QB * j + QB, :] = kr[...].astype(BF)
                vbuf[QB * j:QB * j + QB, :] = vr[...].astype(BF)
            valid = (i * QB - 2 * QB + lax.broadcasted_iota(jnp.int32, (WIN, 1), 0)) >= 0
            for h in range(NH):
                sl = slice(HD * h, HD * h + HD)
                q = q_ref[:, sl].astype(BF)
                k = kbuf[:, sl]
                do = do_ref[:, sl].astype(BF)
                st = lax.dot_general(k, q, (((1,), (1,)), ((), ())), preferred_element_type=F32)
                st = jnp.where(valid, st * SCALE + b_ref[h], -jnp.inf)
                e = jnp.exp(st - jnp.max(st, axis=0, keepdims=True))
                pt = e * (1.0 / jnp.sum(e, axis=0, keepdims=True))
                dvacc[:, sl] += jnp.dot(pt.astype(BF), do, preferred_element_type=F32)
                dpt = lax.dot_general(vbuf[:, sl], do, (((1,), (1,)), ((), ())), preferred_element_type=F32)
                dst = pt * (dpt - jnp.sum(dpt * pt, axis=0, keepdims=True))
                dsbuf[...] = dst
                fold = dsbuf[0:BAND, 0:CH]
                for a in range(1, NQC):
                    fold = fold + dsbuf[CH * a:CH * a + BAND, CH * a:CH * a + CH]
                db_ref[h] += fold
                dsb = dst.astype(BF)
                dkacc[:, sl] += jnp.dot(dsb, q, preferred_element_type=F32) * SCALE
                dq = lax.dot_general(dsb, k, (((0,), (0,)), ((), ())), preferred_element_type=F32) * SCALE
                dq_ref[:, sl] = dq.astype(BF)

        dk_ref[...] = dkacc[0:QB, :].astype(BF)
        dv_ref[...] = dvacc[0:QB, :].astype(BF)
        for acc in (dkacc, dvacc):
            acc[0:QB, :] = acc[QB:2 * QB, :]
            acc[QB:2 * QB, :] = acc[2 * QB:WIN, :]
            acc[2 * QB:WIN, :] = jnp.zeros((QB, AW), F32)

    qspec = pl.BlockSpec((QB, AW), lambda i: (cur(i), 0))
    lspec = pl.BlockSpec((QB, AW), lambda i: (late(i), 0))
    return pl.pallas_call(
        body, name="attn_bwd", grid=(nb + 2,),
        in_specs=[qspec] + _win_specs(AW, nb) + [qspec, _fix((NH, WIN, QB), single=True)],
        out_specs=[qspec, lspec, lspec, _fix((NH, BAND, CH))],
        out_shape=[jax.ShapeDtypeStruct((s, AW), BF)] * 3 + [jax.ShapeDtypeStruct((NH, BAND, CH), F32)],
        scratch_shapes=[pltpu.VMEM((WIN, AW), BF), pltpu.VMEM((WIN, AW), BF),
                        pltpu.VMEM((WIN, AW), F32), pltpu.VMEM((WIN, AW), F32), pltpu.VMEM((WIN, QB), F32)],
        compiler_params=_cp("arbitrary"))(qkv, qkv, qkv, qkv, qkv, qkv, qkv, datt, bias_wt)


TB = 512
SCH = 256
NCB = TB // SCH
GW = 8 * HD


def _tri(lower):
    r = lax.broadcasted_iota(jnp.int32, (SCH, SCH), 0)
    c = lax.broadcasted_iota(jnp.int32, (SCH, SCH), 1)
    return r >= c if lower else r <= c


def _ssd_fwd(proj, conv_w, conv_b, dt_bias, a_log, d_skip):
    s = proj.shape[0]
    nb = s // TB

    def body(xs_ref, bc_ref, dt_ref, cw_ref, cb_ref, dtb_ref, al_ref, dsk_ref, y_ref, pre_ref, prev_ref,
             xpad, ubuf, dtbuf, csb, cstb, hst, xdec, yoffb, drow):
        i = pl.program_id(0)

        @pl.when(i == 0)
        def _():
            xpad[0:8, :] = jnp.zeros((8, XBC), F32)
            hst[...] = jnp.zeros_like(hst)

        xpad[8:8 + TB, 0:SW] = xs_ref[...]
        xpad[8:8 + TB, SW:XBC] = bc_ref[...]
        pre = cb_ref[...]
        for kk in range(4):
            pre = pre + cw_ref[kk:kk + 1, :] * xpad[5 + kk:5 + kk + TB, :]
        pre_ref[...] = pre
        ubuf[...] = pre * jax.nn.sigmoid(pre)
        xpad[0:8, :] = xpad[TB:TB + 8, :]
        dtbuf[...] = _softplus(dt_ref[...] + dtb_ref[...])
        a = -jnp.exp(al_ref[...])
        tri = _tri(True).astype(F32)
        causal = _tri(True)

        def chunk(c, carry):
            r0 = pl.multiple_of(c * SCH, SCH)
            rows = pl.ds(r0, SCH)
            cs = jnp.dot(tri, dtbuf[rows, :] * a, precision=HI, preferred_element_type=F32)
            csb[...] = cs
            cstb[...] = cs.T
            for g in range(NG):
                bg = ubuf[rows, SW + NS * g:SW + NS * g + NS]
                cg = ubuf[rows, SW + NG * NS + NS * g:SW + NG * NS + NS * g + NS].astype(BF)
                cb = lax.dot_general(cg, bg.astype(BF), (((1,), (1,)), ((), ())), preferred_element_type=F32)
                hg = hst[g]
                prev_ref[c, g] = hg
                yoffb[...] = jnp.dot(cg, hg.astype(BF), preferred_element_type=F32)
                for r in range(8):
                    h = 8 * g + r
                    sl = slice(HD * h, HD * h + HD)
                    rs = slice(HD * r, HD * r + HD)
                    cs_h = csb[:, h:h + 1]
                    cl = csb[SCH - 1:SCH, h:h + 1]
                    seg = jnp.exp(jnp.where(causal, cs_h - cstb[h:h + 1, :], -jnp.inf))
                    xh = ubuf[rows, sl]
                    xdt = xh * dtbuf[rows, h:h + 1]
                    yd = jnp.dot((cb * seg).astype(BF), xdt.astype(BF), preferred_element_type=F32)
                    y_ref[rows, sl] = yd + jnp.exp(cs_h) * yoffb[:, rs] + xh * dsk_ref[:, h:h + 1]
                    xdec[:, rs] = xdt * jnp.exp(cl - cs_h)
                    drow[:, rs] = jnp.broadcast_to(jnp.exp(cl), (1, HD))
                st = jnp.dot(bg.T.astype(BF), xdec[...].astype(BF), preferred_element_type=F32)
                hst[g] = hg * drow[...] + st
            return carry

        lax.fori_loop(0, NCB, chunk, 0)

    return pl.pallas_call(
        body, name="ssd_fwd", grid=(nb,),
        in_specs=[_row(TB, SW, 4), _row(TB, 512, 10), _row(TB, LANE, 44), _fix((4, XBC)), _fix((1, XBC)),
                  _fix((1, LANE)), _fix((1, LANE)), _fix((1, LANE))],
        out_specs=[_row(TB, SW), _row(TB, XBC), pl.BlockSpec((NCB, NG, NS, GW), lambda i: (i, 0, 0, 0))],
        out_shape=[jax.ShapeDtypeStruct((s, SW), F32), jax.ShapeDtypeStruct((s, XBC), F32),
                   jax.ShapeDtypeStruct((s // SCH, NG, NS, GW), F32)],
        scratch_shapes=[pltpu.VMEM((TB + 8, XBC), F32), pltpu.VMEM((TB, XBC), F32), pltpu.VMEM((TB, LANE), F32),
                        pltpu.VMEM((SCH, LANE), F32), pltpu.VMEM((LANE, SCH), F32), pltpu.VMEM((NG, NS, GW), F32),
                        pltpu.VMEM((SCH, GW), F32), pltpu.VMEM((SCH, GW), F32), pltpu.VMEM((1, GW), F32)],
        compiler_params=_cp("arbitrary"))(proj, proj, proj, conv_w, conv_b, dt_bias, a_log, d_skip)


def _ssd_bwd(dy, pre, proj, prev, conv_w, dt_bias, a_log, d_skip):
    s = dy.shape[0]
    nb = s // TB
    rev = lambda i: nb - 1 - i
    halo = lambda i: jnp.maximum((nb - 1 - i) * (TB // 8) - 1, 0)

    def body(dy_ref, pre_ref, dt_ref, prev_ref, xs_ref, bc_ref, xsh_ref, bch_ref, cw_ref, dtb_ref, al_ref, dsk_ref,
             dx_ref, ddt_ref, pw_ref, ph_ref,
             ubuf, dtbuf, dpad, xpad, csb, cstb, dhs, yoffb, ebuf, xdecb, wbuf, drow, dub):
        i = pl.program_id(0)

        @pl.when(i == 0)
        def _():
            dhs[...] = jnp.zeros_like(dhs)
            dpad[TB:TB + 8, :] = jnp.zeros((8, XBC), F32)
            pw_ref[...] = jnp.zeros_like(pw_ref)
            ph_ref[...] = jnp.zeros_like(ph_ref)

        pre = pre_ref[...]
        ubuf[...] = pre * jax.nn.sigmoid(pre)
        dtbuf[...] = _softplus(dt_ref[...] + dtb_ref[...])
        a = -jnp.exp(al_ref[...])
        tri = _tri(True).astype(F32)
        trit = _tri(False).astype(F32)
        causal = _tri(True)
        upper = _tri(False)
        lane = lax.broadcasted_iota(jnp.int32, (SCH, LANE), 1)
        lane1 = lax.broadcasted_iota(jnp.int32, (1, LANE), 1)
        lastrow = lax.broadcasted_iota(jnp.int32, (SCH, LANE), 0) == SCH - 1

        def chunk(cc, carry):
            c = NCB - 1 - cc
            r0 = pl.multiple_of(c * SCH, SCH)
            rows = pl.ds(r0, SCH)
            dtc = dtbuf[rows, :]
            cs = jnp.dot(tri, dtc * a, precision=HI, preferred_element_type=F32)
            csb[...] = cs
            cstb[...] = cs.T
            dcs = jnp.zeros((SCH, LANE), F32)
            xr = jnp.zeros((SCH, LANE), F32)
            dlast = jnp.zeros((1, LANE), F32)
            dsk = jnp.zeros((1, LANE), F32)
            for g in range(NG):
                bg = ubuf[rows, SW + NS * g:SW + NS * g + NS]
                cg = ubuf[rows, SW + NG * NS + NS * g:SW + NG * NS + NS * g + NS]
                bb = bg.astype(BF)
                cbf = cg.astype(BF)
                cb = lax.dot_general(cbf, bb, (((1,), (1,)), ((), ())), preferred_element_type=F32)
                cbt = lax.dot_general(bb, cbf, (((1,), (1,)), ((), ())), preferred_element_type=F32)
                hp = prev_ref[c, g]
                hpb = hp.astype(BF)
                dhg = dhs[g]
                dhb = dhg.astype(BF)
                yoffb[...] = jnp.dot(cbf, hpb, preferred_element_type=F32)
                for r in range(8):
                    h = 8 * g + r
                    sl = slice(HD * h, HD * h + HD)
                    rs = slice(HD * r, HD * r + HD)
                    cs_h = csb[:, h:h + 1]
                    cl = csb[SCH - 1:SCH, h:h + 1]
                    dyh = dy_ref[rows, sl]
                    ecs = jnp.exp(cs_h)
                    ebuf[:, rs] = dyh * ecs
                    xdecb[:, rs] = ubuf[rows, sl] * dtbuf[rows, h:h + 1] * jnp.exp(cl - cs_h)
                    t1 = jnp.sum(dyh * ecs * yoffb[:, rs], axis=1, keepdims=True)
                    dcs = dcs + jnp.where(lane == h, t1, 0.0)
                    drow[:, rs] = jnp.broadcast_to(jnp.exp(cl), (1, HD))
                eb = ebuf[...].astype(BF)
                dc = lax.dot_general(eb, hpb, (((1,), (1,)), ((), ())), preferred_element_type=F32)
                dprev = jnp.dot(cg.T.astype(BF), eb, preferred_element_type=F32)
                wbuf[...] = jnp.dot(bb, dhb, preferred_element_type=F32)
                db = lax.dot_general(xdecb[...].astype(BF), dhb, (((1,), (1,)), ((), ())), preferred_element_type=F32)
                dcbs = jnp.zeros((SCH, SCH), F32)
                dcbts = jnp.zeros((SCH, SCH), F32)
                for r in range(8):
                    h = 8 * g + r
                    sl = slice(HD * h, HD * h + HD)
                    rs = slice(HD * r, HD * r + HD)
                    cs_h = csb[:, h:h + 1]
                    cst_h = cstb[h:h + 1, :]
                    cl = csb[SCH - 1:SCH, h:h + 1]
                    seg = jnp.exp(jnp.where(causal, cs_h - cst_h, -jnp.inf))
                    segt = jnp.exp(jnp.where(upper, cst_h - cs_h, -jnp.inf))
                    dyh = dy_ref[rows, sl]
                    xh = ubuf[rows, sl]
                    dt_h = dtbuf[rows, h:h + 1]
                    xdt = xh * dt_h
                    dyb = dyh.astype(BF)
                    xdb = xdt.astype(BF)
                    dm = lax.dot_general(dyb, xdb, (((1,), (1,)), ((), ())), preferred_element_type=F32)
                    dmt = lax.dot_general(xdb, dyb, (((1,), (1,)), ((), ())), preferred_element_type=F32)
                    mt = cbt * segt
                    dxdt = jnp.dot(mt.astype(BF), dyb, preferred_element_type=F32)
                    dcs_h = (jnp.sum(dm * (cb * seg), axis=1, keepdims=True)
                             - jnp.sum(dmt * mt, axis=1, keepdims=True))
                    dcbs = dcbs + dm * seg
                    dcbts = dcbts + dmt * segt
                    dec = jnp.exp(cl - cs_h)
                    w_h = wbuf[:, rs]
                    dxdt = dxdt + dec * w_h
                    t = jnp.sum(w_h * xdt, axis=1, keepdims=True) * dec
                    dcs_h = dcs_h - t
                    dch = jnp.sum(jnp.sum(dhs[g, :, rs] * prev_ref[c, g, :, rs], axis=1, keepdims=True),
                                  axis=0, keepdims=True)
                    dl = jnp.sum(t, axis=0, keepdims=True) + dch * jnp.exp(cl)
                    dcs = dcs + jnp.where(lane == h, dcs_h, 0.0)
                    dlast = dlast + jnp.where(lane1 == h, dl, 0.0)
                    xr = xr + jnp.where(lane == h, jnp.sum(dxdt * xh, axis=1, keepdims=True), 0.0)
                    dskh = jnp.sum(jnp.sum(dyh * xh, axis=1, keepdims=True), axis=0, keepdims=True)
                    dsk = dsk + jnp.where(lane1 == h, dskh, 0.0)
                    dub[:, sl] = dyh * dsk_ref[:, h:h + 1] + dxdt * dt_h
                dc = dc + jnp.dot(dcbs.astype(BF), bb, preferred_element_type=F32)
                db = db + jnp.dot(dcbts.astype(BF), cbf, preferred_element_type=F32)
                dub[:, SW + NS * g:SW + NS * g + NS] = db
                dub[:, SW + NG * NS + NS * g:SW + NG * NS + NS * g + NS] = dc
                dhs[g] = dprev + drow[...] * dhg
            dcs = dcs + jnp.where(lastrow, dlast, 0.0)
            dadt = jnp.dot(trit, dcs, precision=HI, preferred_element_type=F32)
            ddt = dadt * a + xr
            ddtr = ddt * jax.nn.sigmoid(dt_ref[rows, :] + dtb_ref[...])
            ddt_ref[rows, :] = ddtr.astype(BF)
            ph_ref[0:1, :] += jnp.sum(ddtr, axis=0, keepdims=True)
            ph_ref[1:2, :] += jnp.sum(dadt * dtc, axis=0, keepdims=True) * a
            ph_ref[2:3, :] += dsk
            pc = pre_ref[rows, :]
            sg = jax.nn.sigmoid(pc)
            dpad[rows, :] = dub[...] * _silu_grad(pc, sg)
            return carry

        lax.fori_loop(0, NCB, chunk, 0)

        dxb = cw_ref[0:1, :] * dpad[3:3 + TB, :]
        for kk in range(1, 4):
            dxb = dxb + cw_ref[kk:kk + 1, :] * dpad[3 - kk:3 - kk + TB, :]
        dx_ref[...] = dxb.astype(BF)
        keep = jnp.where(i < nb - 1, 1.0, 0.0)
        xpad[0:8, 0:SW] = xsh_ref[...] * keep
        xpad[0:8, SW:XBC] = bch_ref[...] * keep
        xpad[8:8 + TB, 0:SW] = xs_ref[...]
        xpad[8:8 + TB, SW:XBC] = bc_ref[...]
        dp = dpad[0:TB, :]
        for kk in range(4):
            pw_ref[kk:kk + 1, :] += jnp.sum(dp * xpad[5 + kk:5 + kk + TB, :], axis=0, keepdims=True)
        pw_ref[4:5, :] += jnp.sum(dp, axis=0, keepdims=True)
        dpad[TB:TB + 8, :] = dpad[0:8, :]

    rblk = lambda w, col: pl.BlockSpec((TB, w), lambda i, _c=col: (rev(i), _c))
    return pl.pallas_call(
        body, name="ssd_bwd", grid=(nb,),
        in_specs=[rblk(SW, 0), rblk(XBC, 0), rblk(LANE, 44),
                  pl.BlockSpec((NCB, NG, NS, GW), lambda i: (rev(i), 0, 0, 0)),
                  rblk(SW, 4), rblk(512, 10),
                  pl.BlockSpec((8, SW), lambda i: (halo(i), 4)), pl.BlockSpec((8, 512), lambda i: (halo(i), 10)),
                  _fix((4, XBC)), _fix((1, LANE)), _fix((1, LANE)), _fix((1, LANE))],
        out_specs=[rblk(XBC, 0), rblk(LANE, 0), _fix((8, XBC)), _fix((8, LANE))],
        out_shape=[jax.ShapeDtypeStruct((s, XBC), BF), jax.ShapeDtypeStruct((s, LANE), BF),
                   jax.ShapeDtypeStruct((8, XBC), F32), jax.ShapeDtypeStruct((8, LANE), F32)],
        scratch_shapes=[pltpu.VMEM((TB, XBC), F32), pltpu.VMEM((TB, LANE), F32), pltpu.VMEM((TB + 8, XBC), F32),
                        pltpu.VMEM((TB + 8, XBC), F32), pltpu.VMEM((SCH, LANE), F32), pltpu.VMEM((LANE, SCH), F32),
                        pltpu.VMEM((NG, NS, GW), F32), pltpu.VMEM((SCH, GW), F32), pltpu.VMEM((SCH, GW), F32),
                        pltpu.VMEM((SCH, GW), F32), pltpu.VMEM((SCH, GW), F32), pltpu.VMEM((1, GW), F32),
                        pltpu.VMEM((SCH, XBC), F32)],
        compiler_params=_cp("arbitrary"))(dy, pre, proj, prev, proj, proj, proj, proj, conv_w, dt_bias, a_log, d_skip)


def _rel_index():
    q = jnp.arange(CH)[:, None] + (BAND - CH)
    k = jnp.arange(BAND)[None, :]
    return jnp.clip(q - k, -(CH - 1), REL_CLIP) + (CH - 1)


def _local_step(x, tgt, mods, first_weights, late_weights, grads_ready,
                g_mix, rel_bias, conv_w, conv_b, dt_bias, a_log, d_skip, g_att, g_ssd, g_ffn, g_final):
    sh1, sc1, gt1, sh2, sc2, gt2 = mods
    pad16 = lambda v: jnp.pad(v, ((0, 0), (0, LANE - NH)))
    dtb_p, al_p, dsk_p = pad16(dt_bias), pad16(a_log), pad16(d_skip)
    ridx = _rel_index()
    ext = jnp.concatenate([jnp.broadcast_to(rel_bias[:, NREL - 1:], (NH, BAND + CH - 1 - NREL)),
                           rel_bias[:, ::-1]], axis=1)
    bias = jnp.stack([ext[:, CH - 1 - q:CH - 1 - q + BAND] for q in range(CH)], axis=1)
    bias_w = _bias_window(bias)
    bias_wt = jnp.swapaxes(bias_w, 1, 2)

    h1 = _norm_fwd(x, g_mix, sc1, sh1, name="norm1_fwd")
    w_in_t = first_weights(h1)
    proj = _mm([(h1, w_in_t)], "NT", F32, "in_proj", 1024, 1152, 2048)
    att = _attn2_fwd(proj, bias_w)
    y, pre, prev = _ssd_fwd(proj, conv_w, conv_b, dtb_p, al_p, dsk_p)
    cat = _mixout_fwd(att, y, proj, g_att, g_ssd)
    w_out, w_gate_t, w_up_t, w_down = late_weights(cat)
    mix = _mm([(cat, w_out)], "NN", F32, "out_proj", 1024, 1024, 2048)
    h2, x1 = _norm_fwd(x, g_ffn, sc2, sh2, mix, gt1, name="norm2_fwd")
    gate, up, act = _gate_up(h2, w_gate_t, w_up_t)
    ffn = _mm([(act, w_down)], "NN", F32, "down_proj", 1024, 1024, 1408)
    dx2, dffn, pf = _final(x1, ffn, gt2, g_final, tgt)

    dact = _mm([(dffn, w_down)], "NT", BF, "d_act", 1024, 1408, 2048)
    gw_down = _mm([(act, dffn)], "TN", BF, "gw_down", 1408, 2048, 512)
    dgate, dup = _swiglu_bwd(gate, up, dact)
    dh2 = _mm([(dgate, w_gate_t), (dup, w_up_t)], "NN", F32, "d_h2", 1024, 1024, 1408)
    gw_gate_t = _mm([(dgate, h2)], "TN", BF, "gw_gate", 1408, 2048, 512)
    gw_up_t = _mm([(dup, h2)], "TN", BF, "gw_up", 1408, 2048, 512)
    tok = grads_ready("ffn", gw_gate_t, gw_up_t, gw_down)
    dx1, dmix, p2 = _norm_bwd(x1, g_ffn + tok[0, 0], sc2, sh2, dh2, dx2, mix, gt1, name="norm2_bwd")
    gw_out = _mm([(cat, dmix)], "TN", BF, "gw_out", 1024, 2048, 1024)
    tok = grads_ready("out", gw_out)
    dcat = _mm([(dmix, w_out)], "NT", F32, "d_cat", 1024, 1024, 2048, after=tok)
    datt, dy, dz, pm = _mixout_bwd(dcat, att, y, proj, g_att, g_ssd)
    dq, dk, dv, dbias_t = _attn2_bwd(proj, datt, bias_wt)
    dxbc, ddt, pw, ph = _ssd_bwd(dy, pre, proj, prev, conv_w, dtb_p, al_p, dsk_p)
    dproj = jnp.concatenate([dq, dk, dv, dz, dxbc, ddt], axis=1)
    gw_in_t = _mm([(dproj, h1)], "TN", BF, "gw_in", 1152, 2048, 512)
    tok = grads_ready("in", gw_in_t)
    dh1 = _mm([(dproj, w_in_t)], "NN", F32, "d_h1", 1024, 1024, 1152, after=tok)
    gx, p1 = _norm_bwd(x, g_mix, sc1, sh1, dh1, dx1, name="norm1_bwd")

    onehot = (ridx.T.reshape(BAND * CH, 1) == jnp.arange(NREL)[None, :]).astype(F32)
    g_rel = _mm([(dbias_t.reshape(NH, BAND * CH), onehot)], "NN", F32, "g_rel_bias", NH, NREL, 4608, precision=HI)

    small = dict(
        dmods=jnp.concatenate([p1[1], p1[0], p2[3], p2[1], p2[0], pf[1]]),
        g_mix=p1[2], conv_b=pw[4], dt_bias=ph[0], a_log=ph[1], d_skip=ph[2], g_att=pm[0], g_ssd=pm[1],
        g_ffn=p2[2], g_final=pf[0], rel_bias=g_rel.reshape(-1), conv_w=pw[0:4].reshape(-1), loss=pf[2])
    return gx, small


def _exchange(srcs, gather, name):
    n = len(srcs)
    outs = [jax.ShapeDtypeStruct((NDEV,) + (s.shape if g else s.shape[1:]), s.dtype) for s, g in zip(srcs, gather)]
    outs.append(jax.ShapeDtypeStruct((8, LANE), F32))

    def body(*refs):
        src, dst, token = refs[:n], refs[n:2 * n], refs[2 * n]
        ssem, rsem, lsem = refs[2 * n + 1:]
        token[...] = jnp.zeros_like(token)
        x, y, c = lax.axis_index("x"), lax.axis_index("y"), lax.axis_index("c")
        me = 4 * x + 2 * y + c
        local = []
        for a in range(n):
            own = src[a] if gather[a] else src[a].at[me]
            cp = pltpu.make_async_copy(own, dst[a].at[me], lsem.at[a])
            cp.start()
            local.append(cp)
        remote = []
        for k in range(1, NDEV):
            px = 1 - x if k & 4 else x
            py = 1 - y if k & 2 else y
            pc = 1 - c if k & 1 else c
            pid = 4 * px + 2 * py + pc
            for a in range(n):
                s_ref = src[a] if gather[a] else src[a].at[pid]
                cp = pltpu.make_async_remote_copy(
                    src_ref=s_ref, dst_ref=dst[a].at[me], send_sem=ssem.at[a * (NDEV - 1) + k - 1],
                    recv_sem=rsem.at[a * (NDEV - 1) + k - 1], device_id=(px, py, pc), device_id_type=MESH)
                cp.start()
                remote.append(cp)
        for cp in remote:
            cp.wait()
        for cp in local:
            cp.wait()

    anyspec = pl.BlockSpec(memory_space=pl.ANY)
    return pl.pallas_call(
        body, name=name, out_shape=outs, in_specs=[anyspec] * n,
        out_specs=[anyspec] * n + [pl.BlockSpec(memory_space=pltpu.VMEM)],
        scratch_shapes=[pltpu.SemaphoreType.DMA((n * (NDEV - 1),)), pltpu.SemaphoreType.DMA((n * (NDEV - 1),)),
                        pltpu.SemaphoreType.DMA((n,))],
    )(*srcs)


_HBM = pl.BlockSpec(memory_space=pltpu.HBM)
_SEM = pl.BlockSpec(memory_space=pltpu.SEMAPHORE)
_EFFECT = pltpu.SideEffectType.DATAFLOW_SIDE_EFFECTING


def _peer_copies(src, land, ssem, rsem, gather):
    x, y, c = lax.axis_index("x"), lax.axis_index("y"), lax.axis_index("c")
    me = 4 * x + 2 * y + c
    copies = []
    for a in range(len(src)):
        for k in range(1, NDEV):
            px = 1 - x if k & 4 else x
            py = 1 - y if k & 2 else y
            pc = 1 - c if k & 1 else c
            s_ref = src[a] if gather[a] else src[a].at[4 * px + 2 * py + pc]
            idx = a * (NDEV - 1) + k - 1
            copies.append(pltpu.make_async_remote_copy(
                src_ref=s_ref, dst_ref=land[a].at[me], send_sem=ssem.at[idx], recv_sem=rsem.at[idx],
                device_id=(px, py, pc), device_id_type=MESH))
    return copies


def _xstart(srcs, gather, name):
    n = len(srcs)
    me = 4 * lax.axis_index("x") + 2 * lax.axis_index("y") + lax.axis_index("c")
    lands = []
    for s_, g_ in zip(srcs, gather):
        own = s_ if g_ else lax.dynamic_index_in_dim(s_, me, 0, keepdims=False)
        empty = lax.empty((NDEV,) + own.shape, own.dtype)
        lands.append(lax.dynamic_update_slice(empty, own[None], (me,) + (0,) * own.ndim))
    nsem = n * (NDEV - 1)

    def body(*refs):
        src, land, ssem, rsem, token = refs[:n], refs[n:2 * n], refs[2 * n], refs[2 * n + 1], refs[-1]
        for cp in _peer_copies(src, land, ssem, rsem, gather):
            cp.start()
        token[...] = jnp.zeros_like(token)

    ops = [pltpu.with_memory_space_constraint(a_, pltpu.HBM) for a_ in list(srcs) + lands]
    return pl.pallas_call(
        body, name=name,
        out_shape=(pltpu.SemaphoreType.DMA((nsem,)), pltpu.SemaphoreType.DMA((nsem,)),
                   *[pltpu.HBM(a_.shape, a_.dtype) for a_ in ops], jax.ShapeDtypeStruct((8, LANE), F32)),
        in_specs=[_HBM] * (2 * n),
        out_specs=(_SEM, _SEM, *[_HBM] * (2 * n), pl.BlockSpec(memory_space=pltpu.VMEM)),
        input_output_aliases={i: 2 + i for i in range(2 * n)},
        compiler_params=pltpu.CompilerParams(has_side_effects=_EFFECT),
    )(*ops)


def _xwait(started, gather, after, name):
    n = len(gather)
    ssem, rsem = started[0], started[1]
    bufs = started[2:2 + 2 * n]

    def body(*refs):
        src, land, ssem_, rsem_ = refs[:n], refs[n:2 * n], refs[2 * n], refs[2 * n + 1]
        for cp in _peer_copies(src, land, ssem_, rsem_, gather):
            cp.wait_send()
            cp.wait_recv()

    out = pl.pallas_call(
        body, name=name, out_shape=tuple(pltpu.HBM(a_.shape, a_.dtype) for a_ in bufs),
        in_specs=[_HBM] * (2 * n) + [_SEM, _SEM, pl.BlockSpec(memory_space=pl.ANY)],
        out_specs=[_HBM] * (2 * n), input_output_aliases={i: i for i in range(2 * n)},
        compiler_params=pltpu.CompilerParams(has_side_effects=_EFFECT),
    )(*bufs, ssem, rsem, after)
    return out[n:]


def _mods(c16, w_shard, b_shard, *, tn=512):
    n = w_shard.shape[1]

    def body(c_ref, w_ref, b_ref, o_ref):
        cv = c_ref[...]
        cond = (cv * jax.nn.sigmoid(cv)).astype(BF)
        o_ref[...] = jnp.dot(cond, w_ref[...].astype(BF), preferred_element_type=F32) + b_ref[...]

    return pl.pallas_call(body, name="ada_mods", grid=(n // tn,),
                          in_specs=[_fix((16, D)), pl.BlockSpec((D, tn), lambda j: (0, j)),
                                    pl.BlockSpec((1, tn), lambda j: (0, j))],
                          out_specs=pl.BlockSpec((16, tn), lambda j: (0, j)),
                          out_shape=jax.ShapeDtypeStruct((16, n), F32), compiler_params=_cp("parallel"))(c16, w_shard, b_shard)


def _gw_ada(c_t, dm, *, tr=256):
    n = dm.shape[1]

    def body(ct_ref, dm_ref, o_ref):
        acc = jnp.zeros((tr, n), F32)
        for b in range(NDEV):
            cv = ct_ref[:, b:b + 1]
            cond = (cv * jax.nn.sigmoid(cv)).astype(BF).astype(F32)
            acc = acc + cond * dm_ref[b:b + 1, :].astype(BF).astype(F32)
        o_ref[...] = acc

    return pl.pallas_call(body, name="gw_ada", grid=(D // tr,),
                          in_specs=[_row(tr, NDEV), _fix((NDEV, n))], out_specs=_row(tr, n),
                          out_shape=jax.ShapeDtypeStruct((D, n), F32), compiler_params=_cp("parallel"))(c_t, dm)


def _sum8(parts, *, name, tc=512):
    _, r, c = parts.shape
    tc = min(tc, c)

    def body(p_ref, o_ref):
        acc = p_ref[0].astype(F32)
        for d in range(1, NDEV):
            acc = acc + p_ref[d].astype(F32)
        o_ref[...] = acc

    return pl.pallas_call(body, name=name, grid=(c // tc,),
                          in_specs=[pl.BlockSpec((NDEV, r, tc), lambda j: (0, 0, j))],
                          out_specs=pl.BlockSpec((r, tc), lambda j: (0, j)),
                          out_shape=jax.ShapeDtypeStruct((r, c), F32), compiler_params=_cp("parallel"))(parts)


def _adam(w, m, v, *, recv=None, grad=None, tr, name):
    r, c = w.shape
    summed = recv is not None
    assert r % tr == 0

    def body(g_ref, w_ref, m_ref, v_ref, go_ref, do_ref, mo_ref, vo_ref):
        if summed:
            g = g_ref[0].astype(F32)
            for d in range(1, NDEV):
                g = g + g_ref[d].astype(F32)
        else:
            g = g_ref[...]
        go_ref[...] = g
        mn = B1 * m_ref[...] + (1.0 - B1) * g
        vn = B2 * v_ref[...] + (1.0 - B2) * (g * g)
        mo_ref[...] = mn
        vo_ref[...] = vn
        mh = mn / (1.0 - B1 ** STEP)
        vh = vn / (1.0 - B2 ** STEP)
        do_ref[...] = -LR * (mh / (jnp.sqrt(vh) + AEPS) + WD * w_ref[...])

    gspec = pl.BlockSpec((NDEV, tr, c), lambda i: (0, i, 0)) if summed else _row(tr, c)
    return pl.pallas_call(body, name=name, grid=(r // tr,),
                          in_specs=[gspec, _row(tr, c), _row(tr, c), _row(tr, c)], out_specs=[_row(tr, c)] * 4,
                          out_shape=[jax.ShapeDtypeStruct((r, c), F32)] * 4,
                          compiler_params=_cp("parallel"))(recv if summed else grad, w, m, v)


def _cols_to_blocks(g, width):
    r = g.shape[0]
    return jnp.transpose(g.reshape(r, NDEV, width), (1, 0, 2))


def _blocks_to_cols(b):
    _, r, width = b.shape
    return jnp.transpose(b, (1, 0, 2)).reshape(r, NDEV * width)


_SMALL = ("b_ada", "g_mix", "rel_bias", "conv_w", "conv_b", "dt_bias", "a_log", "d_skip",
          "g_att_out", "g_ssd_out", "g_ffn", "g_final")


def _pad_rows(v):
    v = v.reshape(-1)
    return jnp.pad(v, (0, (-v.shape[0]) % LANE))


def kernel(x, c, w_ada, b_ada, g_mix, w_in, rel_bias, conv_w, conv_b, dt_bias, a_log, d_skip, g_att_out, g_ssd_out, w_out, g_ffn, w_gate, w_up, w_down, g_final, loss_target, m_w_ada, m_b_ada, m_g_mix, m_w_in, m_rel_bias, m_conv_w, m_conv_b, m_dt_bias, m_a_log, m_d_skip, m_g_att_out, m_g_ssd_out, m_w_out, m_g_ffn, m_w_gate, m_w_up, m_w_down, m_g_final, v_w_ada, v_b_ada, v_g_mix, v_w_in, v_rel_bias, v_conv_w, v_conv_b, v_dt_bias, v_a_log, v_d_skip, v_g_att_out, v_g_ssd_out, v_w_out, v_g_ffn, v_w_gate, v_w_up, v_w_down, v_g_final):
    me = 4 * lax.axis_index("x") + 2 * lax.axis_index("y") + lax.axis_index("c")
    wts = dict(w_ada=w_ada, b_ada=b_ada, g_mix=g_mix, w_in=w_in, rel_bias=rel_bias, conv_w=conv_w, conv_b=conv_b,
               dt_bias=dt_bias, a_log=a_log, d_skip=d_skip, g_att_out=g_att_out, g_ssd_out=g_ssd_out, w_out=w_out,
               g_ffn=g_ffn, w_gate=w_gate, w_up=w_up, w_down=w_down, g_final=g_final)
    mom = dict(w_ada=m_w_ada, b_ada=m_b_ada, g_mix=m_g_mix, w_in=m_w_in, rel_bias=m_rel_bias, conv_w=m_conv_w,
               conv_b=m_conv_b, dt_bias=m_dt_bias, a_log=m_a_log, d_skip=m_d_skip, g_att_out=m_g_att_out,
               g_ssd_out=m_g_ssd_out, w_out=m_w_out, g_ffn=m_g_ffn, w_gate=m_w_gate, w_up=m_w_up, w_down=m_w_down,
               g_final=m_g_final)
    var = dict(w_ada=v_w_ada, b_ada=v_b_ada, g_mix=v_g_mix, w_in=v_w_in, rel_bias=v_rel_bias, conv_w=v_conv_w,
               conv_b=v_conv_b, dt_bias=v_dt_bias, a_log=v_a_log, d_skip=v_d_skip, g_att_out=v_g_att_out,
               g_ssd_out=v_g_ssd_out, w_out=v_w_out, g_ffn=v_g_ffn, w_gate=v_w_gate, w_up=v_w_up, w_down=v_w_down,
               g_final=v_g_final)
    order = ("w_ada", "b_ada", "g_mix", "w_in", "rel_bias", "conv_w", "conv_b", "dt_bias", "a_log", "d_skip",
             "g_att_out", "g_ssd_out", "w_out", "g_ffn", "w_gate", "w_up", "w_down", "g_final")

    c_all, rel_all, cw_all, _ = _exchange([c, rel_bias[0], conv_w[0]], [True] * 3, "gather_small")
    c_all = c_all.reshape(NDEV, D)
    rel_full = _blocks_to_cols(rel_all)
    cw_full = _blocks_to_cols(cw_all)

    ncol = w_ada.shape[2]
    b_sh = lax.dynamic_slice(b_ada, (0, me * ncol), (1, ncol))
    mods_part = _mods(jnp.pad(c_all, ((0, 8), (0, 0))), w_ada[0], b_sh)[:NDEV]
    mods_all, tok_m = _exchange([mods_part], [True], "gather_mods")
    mods_me = lax.dynamic_index_in_dim(mods_all, me, axis=1, keepdims=False).reshape(6, 1, D)
    mods = [mods_me[i] for i in range(6)]

    st_in = _xstart([(w_in[0] + tok_m[0, 0]).T.astype(BF)], [True], "gather_w_in_start")
    st_rest = _xstart([(w_out[0] + st_in[-1][0, 0]).astype(BF), w_gate[0].T.astype(BF), w_up[0].T.astype(BF),
                       w_down[0].astype(BF)], [True] * 4, "gather_w_rest_start")

    def first_weights(after):
        (gi,) = _xwait(st_in, [True], after, "gather_w_in_wait")
        return jnp.pad(gi.reshape(INC, D), ((0, INP - INC), (0, 0)))

    def late_weights(after):
        go, gg, gu_, gd = _xwait(st_rest, [True] * 4, after, "gather_w_rest_wait")
        return go.reshape(D, D), gg.reshape(FF, D), gu_.reshape(FF, D), gd.reshape(FF, D)

    started = {}

    def grads_ready(which, *g):
        if which == "in":
            g = (g[0][:INC],)
        srcs = [g_.reshape(NDEV, g_.shape[0] // NDEV, D) for g_ in g]
        started[which] = _xstart(srcs, [False] * len(srcs), f"exchange_{which}_grads_start")
        return started[which][-1]

    gx, small = _local_step(
        x[0], loss_target[0], mods, first_weights, late_weights, grads_ready,
        g_mix + st_rest[-1][0, 0], rel_full, cw_full, conv_b, dt_bias, a_log, d_skip, g_att_out, g_ssd_out, g_ffn,
        g_final.reshape(1, D))

    loss_row = jnp.pad(jnp.sum(small["loss"]).reshape(1), (0, LANE - 1))
    pack = jnp.concatenate([small[k] for k in ("dmods", "g_mix", "conv_b", "dt_bias", "a_log", "d_skip", "g_att",
                                               "g_ssd", "g_ffn", "g_final", "rel_bias", "conv_w")] + [loss_row])
    packs, _ = _exchange([pack.reshape(-1, LANE)], [True], "gather_small_grads")
    rg, ru, rd = _xwait(started["ffn"], [False] * 3, packs, "exchange_ffn_grads_wait")
    (ro,) = _xwait(started["out"], [False], rg, "exchange_out_grads_wait")
    (ri,) = _xwait(started["in"], [False], ro, "exchange_in_grads_wait")
    big = dict(
        w_gate=_adam(w_gate[0], m_w_gate[0], v_w_gate[0], grad=_sum8(rg, name="sum_w_gate").T, tr=256,
                     name="adam_w_gate"),
        w_up=_adam(w_up[0], m_w_up[0], v_w_up[0], grad=_sum8(ru, name="sum_w_up").T, tr=256, name="adam_w_up"),
        w_down=_adam(w_down[0], m_w_down[0], v_w_down[0], recv=rd, tr=176, name="adam_w_down"))
    tot = _sum8(packs, name="sum_small").reshape(-1)

    sizes = dict(dmods=6 * D, g_mix=D, conv_b=XBC, dt_bias=LANE, a_log=LANE, d_skip=LANE, g_att_out=AW, g_ssd_out=SW,
                 g_ffn=D, g_final=D, rel_bias=NH * NREL, conv_w=4 * XBC, loss=LANE)
    off, o = {}, 0
    for k_, n_ in sizes.items():
        off[k_] = o
        o += n_
    take = lambda k_, n_=None: tot[off[k_]:off[k_] + (n_ or sizes[k_])]
    loss = tot[off["loss"]]
    rel_w = NREL // NDEV
    cw_w = XBC // NDEV
    gsm = dict(
        b_ada=take("dmods"), g_mix=take("g_mix"),
        rel_bias=lax.dynamic_slice(take("rel_bias").reshape(NH, NREL), (0, me * rel_w), (NH, rel_w)),
        conv_w=lax.dynamic_slice(take("conv_w").reshape(4, XBC), (0, me * cw_w), (4, cw_w)),
        conv_b=take("conv_b"), dt_bias=take("dt_bias", NH), a_log=take("a_log", NH), d_skip=take("d_skip", NH),
        g_att_out=take("g_att_out"), g_ssd_out=take("g_ssd_out"), g_ffn=take("g_ffn"), g_final=take("g_final"))

    flat = lambda d_: jnp.concatenate([_pad_rows(d_[k_]) for k_ in _SMALL])
    gflat = flat(gsm)
    nrow = -(-gflat.shape[0] // (8 * LANE)) * 8
    to2d = lambda v_: jnp.pad(v_, (0, nrow * LANE - v_.shape[0])).reshape(nrow, LANE)
    res = _adam(to2d(flat(wts)), to2d(flat(mom)), to2d(flat(var)), grad=to2d(gflat), tr=nrow, name="adam_small")
    outs = {k_: {} for k_ in ("grad", "delta", "m", "v")}
    o = 0
    for k_ in _SMALL:
        n_ = wts[k_].size
        for kind, arr in zip(("grad", "delta", "m", "v"), res):
            outs[kind][k_] = arr.reshape(-1)[o:o + n_].reshape(wts[k_].shape)
        o += n_ + (-n_) % LANE

    dm_all = packs[:, :6 * D // LANE, :].reshape(NDEV, 6 * D)
    dm_sh = lax.dynamic_slice(dm_all, (0, me * ncol), (NDEV, ncol))
    gwa = _gw_ada(c_all.T, dm_sh)
    big.update(
        w_ada=_adam(w_ada[0], m_w_ada[0], v_w_ada[0], grad=gwa, tr=256, name="adam_w_ada"),
        w_in=_adam(w_in[0], m_w_in[0], v_w_in[0], grad=_sum8(ri, name="sum_w_in").T, tr=256, name="adam_w_in"),
        w_out=_adam(w_out[0], m_w_out[0], v_w_out[0], recv=ro, tr=128, name="adam_w_out"))
    for k_, r_ in big.items():
        for kind, arr in zip(("grad", "delta", "m", "v"), r_):
            outs[kind][k_] = arr.reshape(wts[k_].shape)

    return (loss, gx.reshape(x.shape), *[outs["grad"][k_] for k_ in order], *[outs["delta"][k_] for k_ in order],
            *[outs["m"][k_] for k_ in order], *[outs["v"][k_] for k_ in order])
```

```python
import functools

import jax
import jax.numpy as jnp
from jax import lax
from jax.experimental import pallas as pl
from jax.experimental.pallas import tpu as pltpu

F32 = jnp.float32
BF = jnp.bfloat16
HI = lax.Precision.HIGHEST

D = 2048
NH = 16
HD = 64
AW = 1024
SW = 1024
NG = 2
NS = 128
XBC = 1536
CH = 64
BAND = 576
NREL = 320
REL_CLIP = 256
FF = 5632
INC = 5648
INP = 5760
EPS = 1e-6
NDEV = 8
LANE = 128
VMEM_LIMIT = 56 * 1024 * 1024
LR, B1, B2, AEPS, WD, STEP = 0.001, 0.9, 0.999, 1e-08, 0.01, 10
MESH = pl.DeviceIdType.MESH


def _cp(*sem):
    return pltpu.CompilerParams(dimension_semantics=sem, vmem_limit_bytes=VMEM_LIMIT)


def _row(ts, w, col=0):
    return pl.BlockSpec((ts, w), lambda i, _c=col: (i, _c))


def _fix(shape, single=False):
    nd = len(shape)
    if single:
        return pl.BlockSpec(shape, lambda i, _n=nd: (0,) * _n, pipeline_mode=pl.Buffered(1))
    return pl.BlockSpec(shape, lambda i, _n=nd: (0,) * _n)


def _silu_grad(x, s):
    return s * (1.0 + x * (1.0 - s))


def _softplus(x):
    return jnp.maximum(x, 0.0) + jnp.log1p(jnp.exp(-jnp.abs(x)))


_DIMS = {"NN": (((1,), (0,)), ((), ())), "TN": (((0,), (0,)), ((), ())), "NT": (((1,), (1,)), ((), ()))}


def _mm(pairs, mode, out_dtype, name, tm, tn, tk, precision=None, after=None):
    a0, b0 = pairs[0]
    m, k = a0.shape[::-1] if mode == "TN" else a0.shape
    n = b0.shape[0] if mode == "NT" else b0.shape[1]
    tm, tn, tk = min(tm, m), min(tn, n), min(tk, k)
    nk1 = k // tk
    npair = len(pairs)
    nk = nk1 * npair
    assert m % tm == 0 and n % tn == 0 and k % tk == 0, (name, a0.shape, b0.shape)
    dims = _DIMS[mode]

    def kloc(kk, p):
        return jnp.clip(kk - p * nk1, 0, nk1 - 1)

    def aspec(p):
        if mode == "TN":
            return pl.BlockSpec((tk, tm), lambda i, j, kk, _p=p: (kloc(kk, _p), i))
        return pl.BlockSpec((tm, tk), lambda i, j, kk, _p=p: (i, kloc(kk, _p)))

    def bspec(p):
        if mode == "NT":
            return pl.BlockSpec((tn, tk), lambda i, j, kk, _p=p: (j, kloc(kk, _p)))
        return pl.BlockSpec((tk, tn), lambda i, j, kk, _p=p: (kloc(kk, _p), j))

    nin = 2 * npair + (after is not None)

    def body(*refs):
        ab, o_ref = refs[:2 * npair], refs[nin]
        if nk == 1:
            o_ref[...] = lax.dot_general(ab[0][...], ab[1][...], dims, preferred_element_type=F32,
                                         precision=precision).astype(o_ref.dtype)
            return
        acc = refs[nin + 1]
        kk = pl.program_id(2)

        @pl.when(kk == 0)
        def _():
            acc[...] = jnp.zeros_like(acc)

        for p in range(npair):
            @pl.when((kk >= p * nk1) & (kk < (p + 1) * nk1))
            def _(p=p):
                acc[...] += lax.dot_general(ab[2 * p][...], ab[2 * p + 1][...], dims, preferred_element_type=F32,
                                            precision=precision)

        @pl.when(kk == nk - 1)
        def _():
            o_ref[...] = acc[...].astype(o_ref.dtype)

    in_specs = []
    for p in range(npair):
        in_specs += [aspec(p), bspec(p)]
    operands = [t for ab_ in pairs for t in ab_]
    if after is not None:
        in_specs.append(pl.BlockSpec(memory_space=pl.ANY))
        operands.append(after)
    return pl.pallas_call(
        body, name=name, grid=(m // tm, n // tn, nk), in_specs=in_specs,
        out_specs=pl.BlockSpec((tm, tn), lambda i, j, kk: (i, j)),
        out_shape=jax.ShapeDtypeStruct((m, n), out_dtype),
        scratch_shapes=[] if nk == 1 else [pltpu.VMEM((tm, tn), F32)],
        compiler_params=_cp("parallel", "parallel", "arbitrary"),
    )(*operands)


def _norm_fwd(x, g, sc, sh, mix=None, gt=None, *, name, ts=256):
    s, d = x.shape
    has = mix is not None

    def body(*refs):
        if has:
            x_ref, g_ref, sc_ref, sh_ref, mix_ref, gt_ref, h_ref, x1_ref = refs
        else:
            x_ref, g_ref, sc_ref, sh_ref, h_ref = refs
        xv = x_ref[...]
        if has:
            xv = xv + gt_ref[...] * mix_ref[...]
            x1_ref[...] = xv
        r = lax.rsqrt(jnp.mean(xv * xv, axis=-1, keepdims=True) + EPS)
        h_ref[...] = ((xv * r) * g_ref[...] * (1.0 + sc_ref[...]) + sh_ref[...]).astype(BF)

    ins = [x, g, sc, sh] + ([mix, gt] if has else [])
    in_specs = [_row(ts, d), _fix((1, d)), _fix((1, d)), _fix((1, d))] + ([_row(ts, d), _fix((1, d))] if has else [])
    out_shape = [jax.ShapeDtypeStruct((s, d), BF)] + ([jax.ShapeDtypeStruct((s, d), F32)] if has else [])
    out_specs = [_row(ts, d)] + ([_row(ts, d)] if has else [])
    out = pl.pallas_call(body, name=name, grid=(s // ts,), in_specs=in_specs, out_specs=out_specs,
                         out_shape=out_shape, compiler_params=_cp("parallel"))(*ins)
    return out if has else out[0]


def _norm_bwd(x, g, sc, sh, dh, dres, mix=None, gt=None, *, name, ts=256):
    s, d = x.shape
    has = mix is not None

    def body(*refs):
        if has:
            x_ref, g_ref, sc_ref, sh_ref, dh_ref, dres_ref, mix_ref, gt_ref, dx_ref, dmix_ref, part_ref = refs
        else:
            x_ref, g_ref, sc_ref, sh_ref, dh_ref, dres_ref, dx_ref, part_ref = refs
        i = pl.program_id(0)
        xv = x_ref[...]
        gv = g_ref[...]
        dhv = dh_ref[...]
        r = lax.rsqrt(jnp.mean(xv * xv, axis=-1, keepdims=True) + EPS)
        xh = xv * r
        dyn = dhv * (1.0 + sc_ref[...])
        gy = dyn * gv
        dx = dres_ref[...] + r * (gy - xh * jnp.mean(gy * xh, axis=-1, keepdims=True))
        dx_ref[...] = dx

        @pl.when(i == 0)
        def _():
            part_ref[...] = jnp.zeros_like(part_ref)

        part_ref[0:1, :] += jnp.sum(dhv * (xh * gv), axis=0, keepdims=True)
        part_ref[1:2, :] += jnp.sum(dhv, axis=0, keepdims=True)
        part_ref[2:3, :] += jnp.sum(dyn * xh, axis=0, keepdims=True)
        if has:
            part_ref[3:4, :] += jnp.sum(dx * mix_ref[...], axis=0, keepdims=True)
            dmix_ref[...] = (gt_ref[...] * dx).astype(BF)

    ins = [x, g, sc, sh, dh, dres] + ([mix, gt] if has else [])
    in_specs = [_row(ts, d), _fix((1, d)), _fix((1, d)), _fix((1, d)), _row(ts, d), _row(ts, d)]
    in_specs += [_row(ts, d), _fix((1, d))] if has else []
    out_shape = [jax.ShapeDtypeStruct((s, d), F32)] + ([jax.ShapeDtypeStruct((s, d), BF)] if has else [])
    out_shape += [jax.ShapeDtypeStruct((8, d), F32)]
    out_specs = [_row(ts, d)] + ([_row(ts, d)] if has else []) + [_fix((8, d))]
    return pl.pallas_call(body, name=name, grid=(s // ts,), in_specs=in_specs, out_specs=out_specs,
                          out_shape=out_shape, compiler_params=_cp("arbitrary"))(*ins)


def _final(x1, ffn, gt2, gf, tgt, *, ts=256):
    s, d = x1.shape

    def body(x1_ref, ffn_ref, gt_ref, g_ref, t_ref, dx_ref, dffn_ref, part_ref):
        i = pl.program_id(0)
        fv = ffn_ref[...]
        gv = g_ref[...]
        xv = x1_ref[...] + gt_ref[...] * fv
        r = lax.rsqrt(jnp.mean(xv * xv, axis=-1, keepdims=True) + EPS)
        xh = xv * r
        e = xh * gv - t_ref[...]
        dy = e * (1.0 / d)
        gy = dy * gv
        dx = r * (gy - xh * jnp.mean(gy * xh, axis=-1, keepdims=True))
        dx_ref[...] = dx
        dffn_ref[...] = (gt_ref[...] * dx).astype(BF)

        @pl.when(i == 0)
        def _():
            part_ref[...] = jnp.zeros_like(part_ref)

        part_ref[0:1, :] += jnp.sum(dy * xh, axis=0, keepdims=True)
        part_ref[1:2, :] += jnp.sum(dx * fv, axis=0, keepdims=True)
        part_ref[2:3, :] += jnp.sum(e * e, axis=0, keepdims=True) * (0.5 / d)

    return pl.pallas_call(
        body, name="final_loss", grid=(s // ts,),
        in_specs=[_row(ts, d), _row(ts, d), _fix((1, d)), _fix((1, d)), _row(ts, d)],
        out_specs=[_row(ts, d), _row(ts, d), _fix((8, d))],
        out_shape=[jax.ShapeDtypeStruct((s, d), F32), jax.ShapeDtypeStruct((s, d), BF),
                   jax.ShapeDtypeStruct((8, d), F32)],
        compiler_params=_cp("arbitrary"))(x1, ffn, gt2, gf, tgt)


def _gate_up(h, wg_t, wu_t, *, tm=1024, tn=512):
    s, k = h.shape
    n = wg_t.shape[0]
    tm = min(tm, s)

    def body(h_ref, wg_ref, wu_ref, g_ref, u_ref, a_ref):
        hv = h_ref[...]
        gv = lax.dot_general(hv, wg_ref[...], _DIMS["NT"], preferred_element_type=F32)
        uv = lax.dot_general(hv, wu_ref[...], _DIMS["NT"], preferred_element_type=F32)
        g_ref[...] = gv.astype(BF)
        u_ref[...] = uv.astype(BF)
        a_ref[...] = (gv * jax.nn.sigmoid(gv) * uv).astype(BF)

    wspec = pl.BlockSpec((tn, k), lambda i, j: (j, 0))
    ospec = pl.BlockSpec((tm, tn), lambda i, j: (i, j))
    return pl.pallas_call(body, name="gate_up", grid=(s // tm, n // tn),
                          in_specs=[pl.BlockSpec((tm, k), lambda i, j: (i, 0)), wspec, wspec],
                          out_specs=[ospec] * 3, out_shape=[jax.ShapeDtypeStruct((s, n), BF)] * 3,
                          compiler_params=_cp("parallel", "parallel"))(h, wg_t, wu_t)


def _gate_up_bwd(dffn, w_down, gate, up, *, tm=1024, tn=512):
    s, k = dffn.shape
    n = w_down.shape[0]
    tm = min(tm, s)

    def body(d_ref, w_ref, g_ref, u_ref, dg_ref, du_ref):
        dav = lax.dot_general(d_ref[...], w_ref[...], _DIMS["NT"], preferred_element_type=F32)
        gv = g_ref[...].astype(F32)
        uv = u_ref[...].astype(F32)
        sg = jax.nn.sigmoid(gv)
        dg_ref[...] = (dav * uv * _silu_grad(gv, sg)).astype(BF)
        du_ref[...] = (dav * gv * sg).astype(BF)

    tile = pl.BlockSpec((tm, tn), lambda i, j: (i, j))
    return pl.pallas_call(body, name="gate_up_bwd", grid=(s // tm, n // tn),
                          in_specs=[pl.BlockSpec((tm, k), lambda i, j: (i, 0)),
                                    pl.BlockSpec((tn, k), lambda i, j: (j, 0)), tile, tile],
                          out_specs=[tile] * 2, out_shape=[jax.ShapeDtypeStruct((s, n), BF)] * 2,
                          compiler_params=_cp("parallel", "parallel"))(dffn, w_down, gate, up)


def _mixout_fwd(att, y, proj, g_att, g_ssd, *, ts=256):
    s = att.shape[0]

    def body(a_ref, y_ref, z_ref, ga_ref, gs_ref, o_ref):
        av = a_ref[...]
        ra = lax.rsqrt(jnp.mean(av * av, axis=-1, keepdims=True) + EPS)
        o_ref[:, 0:AW] = (av * ra * ga_ref[...]).astype(BF)
        zv = z_ref[...]
        yz = y_ref[...] * (zv * jax.nn.sigmoid(zv))
        rs = lax.rsqrt(jnp.mean(yz * yz, axis=-1, keepdims=True) + EPS)
        o_ref[:, AW:AW + SW] = (yz * rs * gs_ref[...]).astype(BF)

    return pl.pallas_call(body, name="mixout_fwd", grid=(s // ts,),
                          in_specs=[_row(ts, AW), _row(ts, SW), _row(ts, SW, 3), _fix((1, AW)), _fix((1, SW))],
                          out_specs=_row(ts, AW + SW), out_shape=jax.ShapeDtypeStruct((s, AW + SW), BF),
                          compiler_params=_cp("parallel"))(att, y, proj, g_att, g_ssd)


def _mixout_bwd(dcat, att, y, proj, g_att, g_ssd, *, ts=256):
    s = att.shape[0]

    def body(dc_ref, a_ref, y_ref, z_ref, ga_ref, gs_ref, da_ref, dy_ref, dz_ref, part_ref):
        i = pl.program_id(0)
        av = a_ref[...]
        dca = dc_ref[:, 0:AW]
        ra = lax.rsqrt(jnp.mean(av * av, axis=-1, keepdims=True) + EPS)
        ah = av * ra
        gy = dca * ga_ref[...]
        da_ref[...] = ra * (gy - ah * jnp.mean(gy * ah, axis=-1, keepdims=True))
        zv = z_ref[...]
        yv = y_ref[...]
        sg = jax.nn.sigmoid(zv)
        sz = zv * sg
        yz = yv * sz
        dcs = dc_ref[:, AW:AW + SW]
        rs = lax.rsqrt(jnp.mean(yz * yz, axis=-1, keepdims=True) + EPS)
        yh = yz * rs
        gys = dcs * gs_ref[...]
        dyz = rs * (gys - yh * jnp.mean(gys * yh, axis=-1, keepdims=True))
        dy_ref[...] = dyz * sz
        dz_ref[...] = (dyz * yv * _silu_grad(zv, sg)).astype(BF)

        @pl.when(i == 0)
        def _():
            part_ref[...] = jnp.zeros_like(part_ref)

        part_ref[0:1, :] += jnp.sum(dca * ah, axis=0, keepdims=True)
        part_ref[1:2, :] += jnp.sum(dcs * yh, axis=0, keepdims=True)

    return pl.pallas_call(
        body, name="mixout_bwd", grid=(s // ts,),
        in_specs=[_row(ts, AW + SW), _row(ts, AW), _row(ts, SW), _row(ts, SW, 3), _fix((1, AW)), _fix((1, SW))],
        out_specs=[_row(ts, AW), _row(ts, SW), _row(ts, SW), _fix((8, AW))],
        out_shape=[jax.ShapeDtypeStruct((s, AW), F32), jax.ShapeDtypeStruct((s, SW), F32),
                   jax.ShapeDtypeStruct((s, SW), BF), jax.ShapeDtypeStruct((8, AW), F32)],
        compiler_params=_cp("arbitrary"))(dcat, att, y, proj, g_att, g_ssd)


TQ = 512
SCALE = HD ** -0.5


def _attn_fwd(qkv, bias):
    s = qkv.shape[0]
    nb = s // TQ
    prev = lambda i: jnp.maximum(i - 1, 0)

    def body(q_ref, kp_ref, kc_ref, vp_ref, vc_ref, b_ref, o_ref, kbuf, vbuf):
        i = pl.program_id(0)
        kbuf[0:TQ, :] = kp_ref[...]
        kbuf[TQ:2 * TQ, :] = kc_ref[...]
        vbuf[0:TQ, :] = vp_ref[...]
        vbuf[TQ:2 * TQ, :] = vc_ref[...]

        def chunk(c, carry):
            r0 = pl.multiple_of(c * CH, CH)
            kpos = i * TQ + r0 - TQ + lax.broadcasted_iota(jnp.int32, (1, BAND), 1)
            valid = kpos >= 0
            for h in range(NH):
                sl = slice(HD * h, HD * h + HD)
                q = q_ref[pl.ds(r0, CH), sl]
                k = kbuf[pl.ds(r0, BAND), sl]
                v = vbuf[pl.ds(r0, BAND), sl]
                sc = lax.dot_general(q, k, (((1,), (1,)), ((), ())), preferred_element_type=F32) * SCALE + b_ref[h]
                sc = jnp.where(valid, sc, -jnp.inf)
                e = jnp.exp(sc - jnp.max(sc, axis=-1, keepdims=True))
                p = e / jnp.sum(e, axis=-1, keepdims=True)
                o_ref[pl.ds(r0, CH), sl] = jnp.dot(p.astype(BF), v, preferred_element_type=F32)
            return carry

        lax.fori_loop(0, TQ // CH, chunk, 0)

    blk = lambda col, im: pl.BlockSpec((TQ, AW), lambda i, _c=col, _f=im: (_f(i), _c))
    cur = lambda i: i
    return pl.pallas_call(
        body, name="attn_fwd", grid=(nb,),
        in_specs=[blk(0, cur), blk(1, prev), blk(1, cur), blk(2, prev), blk(2, cur), _fix((NH, CH, BAND))],
        out_specs=_row(TQ, AW), out_shape=jax.ShapeDtypeStruct((s, AW), F32),
        scratch_shapes=[pltpu.VMEM((2 * TQ, AW), BF), pltpu.VMEM((2 * TQ, AW), BF)],
        compiler_params=_cp("parallel"))(qkv, qkv, qkv, qkv, qkv, bias)


def _attn_bwd(qkv, datt, bias_t):
    s = qkv.shape[0]
    nb = s // TQ
    cur = lambda i: jnp.minimum(i, nb - 1)
    prev = lambda i: jnp.maximum(jnp.minimum(i, nb - 1) - 1, 0)
    late = lambda i: jnp.maximum(i - 1, 0)

    def body(q_ref, kp_ref, kc_ref, vp_ref, vc_ref, do_ref, b_ref, dq_ref, dk_ref, dv_ref, db_ref,
             kbuf, vbuf, dkacc, dvacc):
        i = pl.program_id(0)

        @pl.when(i == 0)
        def _():
            dkacc[...] = jnp.zeros_like(dkacc)
            dvacc[...] = jnp.zeros_like(dvacc)
            db_ref[...] = jnp.zeros_like(db_ref)

        @pl.when(i < nb)
        def _():
            kbuf[0:TQ, :] = kp_ref[...]
            kbuf[TQ:2 * TQ, :] = kc_ref[...]
            vbuf[0:TQ, :] = vp_ref[...]
            vbuf[TQ:2 * TQ, :] = vc_ref[...]

            def chunk(c, carry):
                r0 = pl.multiple_of(c * CH, CH)
                kpos = i * TQ + r0 - TQ + lax.broadcasted_iota(jnp.int32, (BAND, 1), 0)
                valid = kpos >= 0
                for h in range(NH):
                    sl = slice(HD * h, HD * h + HD)
                    q = q_ref[pl.ds(r0, CH), sl]
                    k = kbuf[pl.ds(r0, BAND), sl]
                    v = vbuf[pl.ds(r0, BAND), sl]
                    do = do_ref[pl.ds(r0, CH), sl].astype(BF)
                    st = lax.dot_general(k, q, (((1,), (1,)), ((), ())), preferred_element_type=F32) * SCALE + b_ref[h]
                    st = jnp.where(valid, st, -jnp.inf)
                    e = jnp.exp(st - jnp.max(st, axis=0, keepdims=True))
                    pt = e / jnp.sum(e, axis=0, keepdims=True)
                    dvacc[pl.ds(r0, BAND), sl] += jnp.dot(pt.astype(BF), do, preferred_element_type=F32)
                    dpt = lax.dot_general(v, do, (((1,), (1,)), ((), ())), preferred_element_type=F32)
                    dst = pt * (dpt - jnp.sum(dpt * pt, axis=0, keepdims=True))
                    db_ref[h] += dst
                    dsb = dst.astype(BF)
                    dkacc[pl.ds(r0, BAND), sl] += jnp.dot(dsb, q, preferred_element_type=F32) * SCALE
                    dq = lax.dot_general(dsb, k, (((0,), (0,)), ((), ())), preferred_element_type=F32) * SCALE
                    dq_ref[pl.ds(r0, CH), sl] = dq.astype(BF)
                return carry

            lax.fori_loop(0, TQ // CH, chunk, 0)

        dk_ref[...] = dkacc[0:TQ, :].astype(BF)
        dv_ref[...] = dvacc[0:TQ, :].astype(BF)
        dkacc[0:TQ, :] = dkacc[TQ:2 * TQ, :]
        dvacc[0:TQ, :] = dvacc[TQ:2 * TQ, :]
        dkacc[TQ:2 * TQ, :] = jnp.zeros((TQ, AW), F32)
        dvacc[TQ:2 * TQ, :] = jnp.zeros((TQ, AW), F32)

    blk = lambda col, im: pl.BlockSpec((TQ, AW), lambda i, _c=col, _f=im: (_f(i), _c))
    return pl.pallas_call(
        body, name="attn_bwd", grid=(nb + 1,),
        in_specs=[blk(0, cur), blk(1, prev), blk(1, cur), blk(2, prev), blk(2, cur), blk(0, cur),
                  _fix((NH, BAND, CH))],
        out_specs=[blk(0, cur), blk(0, late), blk(0, late), _fix((NH, BAND, CH))],
        out_shape=[jax.ShapeDtypeStruct((s, AW), BF)] * 3 + [jax.ShapeDtypeStruct((NH, BAND, CH), F32)],
        scratch_shapes=[pltpu.VMEM((2 * TQ, AW), BF), pltpu.VMEM((2 * TQ, AW), BF),
                        pltpu.VMEM((2 * TQ, AW), F32), pltpu.VMEM((2 * TQ, AW), F32)],
        compiler_params=_cp("arbitrary"))(qkv, qkv, qkv, qkv, qkv, datt, bias_t)


QB = 256
NQC = QB // CH
WIN = 3 * QB


def _bias_window(bias):
    rows = [jnp.pad(bias, ((0, 0), (0, 0), (CH * a, WIN - BAND - CH * a)), constant_values=-jnp.inf)
            for a in range(NQC)]
    return jnp.concatenate(rows, axis=1)


def _win_specs(width, nb):
    def mk(col, back):
        return pl.BlockSpec((QB, width), lambda i, _c=col, _k=back: (jnp.maximum(jnp.minimum(i, nb - 1) - _k, 0), _c))
    return [mk(1, 2), mk(1, 1), mk(1, 0), mk(2, 2), mk(2, 1), mk(2, 0)]


def _attn2_fwd(qkv, bias_w):
    s = qkv.shape[0]
    nb = s // QB

    def body(q_ref, k0, k1, k2, v0, v1, v2, b_ref, o_ref, kbuf, vbuf):
        i = pl.program_id(0)
        for j, (kr, vr) in enumerate(((k0, v0), (k1, v1), (k2, v2))):
            kbuf[QB * j:QB * j + QB, :] = kr[...].astype(BF)
            vbuf[QB * j:QB * j + QB, :] = vr[...].astype(BF)
        valid = (i * QB - 2 * QB + lax.broadcasted_iota(jnp.int32, (1, WIN), 1)) >= 0
        for h in range(NH):
            sl = slice(HD * h, HD * h + HD)
            sc = lax.dot_general(q_ref[:, sl].astype(BF), kbuf[:, sl], _DIMS["NT"], preferred_element_type=F32)
            sc = jnp.where(valid, sc * SCALE + b_ref[h], -jnp.inf)
            e = jnp.exp(sc - jnp.max(sc, axis=-1, keepdims=True))
            p = e * (1.0 / jnp.sum(e, axis=-1, keepdims=True))
            o_ref[:, sl] = jnp.dot(p.astype(BF), vbuf[:, sl], preferred_element_type=F32)

    return pl.pallas_call(
        body, name="attn_fwd", grid=(nb,),
        in_specs=[_row(QB, AW)] + _win_specs(AW, nb) + [_fix((NH, QB, WIN), single=True)],
        out_specs=_row(QB, AW), out_shape=jax.ShapeDtypeStruct((s, AW), F32),
        scratch_shapes=[pltpu.VMEM((WIN, AW), BF), pltpu.VMEM((WIN, AW), BF)],
        compiler_params=_cp("parallel"))(qkv, qkv, qkv, qkv, qkv, qkv, qkv, bias_w)


def _attn2_bwd(qkv, datt, bias_wt):
    s = qkv.shape[0]
    nb = s // QB
    cur = lambda i: jnp.minimum(i, nb - 1)
    late = lambda i: jnp.maximum(i - 2, 0)

    def body(q_ref, k0, k1, k2, v0, v1, v2, do_ref, b_ref, dq_ref, dk_ref, dv_ref, db_ref,
             kbuf, vbuf, dkacc, dvacc, dsbuf):
        i = pl.program_id(0)

        @pl.when(i == 0)
        def _():
            dkacc[...] = jnp.zeros_like(dkacc)
            dvacc[...] = jnp.zeros_like(dvacc)
            db_ref[...] = jnp.zeros_like(db_ref)

        @pl.when(i < nb)
        def _():
            for j, (kr, vr) in enumerate(((k0, v0), (k1, v1), (k2, v2))):
                kbuf[QB * j:QB * j + QB, :] = kr[...].astype(BF)
                vbuf[QB * j:QB * j + QB, :] = vr[...].astype(BF)
            valid = (i * QB - 2 * QB + lax.broadcasted_iota(jnp.int32, (WIN, 1), 0)) >= 0
            for h in range(NH):
                sl = slice(HD * h, HD * h + HD)
                q = q_ref[:, sl].astype(BF)
                k = kbuf[:, sl]
                do = do_ref[:, sl].astype(BF)
                st = lax.dot_general(k, q, (((1,), (1,)), ((), ())), preferred_element_type=F32)
                st = jnp.where(valid, st * SCALE + b_ref[h], -jnp.inf)
                e = jnp.exp(st - jnp.max(st, axis=0, keepdims=True))
                pt = e * (1.0 / jnp.sum(e, axis=0, keepdims=True))
                dvacc[:, sl] += jnp.dot(pt.astype(BF), do, preferred_element_type=F32)
                dpt = lax.dot_general(vbuf[:, sl], do, (((1,), (1,)), ((), ())), preferred_element_type=F32)
                dst = pt * (dpt - jnp.sum(dpt * pt, axis=0, keepdims=True))
                dsbuf[...] = dst
                fold = dsbuf[0:BAND, 0:CH]
                for a in range(1, NQC):
                    fold = fold + dsbuf[CH * a:CH * a + BAND, CH * a:CH * a + CH]
                db_ref[h] += fold
                dsb = dst.astype(BF)
                dkacc[:, sl] += jnp.dot(dsb, q, preferred_element_type=F32) * SCALE
                dq = lax.dot_general(dsb, k, (((0,), (0,)), ((), ())), preferred_element_type=F32) * SCALE
                dq_ref[:, sl] = dq.astype(BF)

        dk_ref[...] = dkacc[0:QB, :].astype(BF)
        dv_ref[...] = dvacc[0:QB, :].astype(BF)
        for acc in (dkacc, dvacc):
            acc[0:QB, :] = acc[QB:2 * QB, :]
            acc[QB:2 * QB, :] = acc[2 * QB:WIN, :]
            acc[2 * QB:WIN, :] = jnp.zeros((QB, AW), F32)

    qspec = pl.BlockSpec((QB, AW), lambda i: (cur(i), 0))
    lspec = pl.BlockSpec((QB, AW), lambda i: (late(i), 0))
    return pl.pallas_call(
        body, name="attn_bwd", grid=(nb + 2,),
        in_specs=[qspec] + _win_specs(AW, nb) + [qspec, _fix((NH, WIN, QB), single=True)],
        out_specs=[qspec, lspec, lspec, _fix((NH, BAND, CH))],
        out_shape=[jax.ShapeDtypeStruct((s, AW), BF)] * 3 + [jax.ShapeDtypeStruct((NH, BAND, CH), F32)],
        scratch_shapes=[pltpu.VMEM((WIN, AW), BF), pltpu.VMEM((WIN, AW), BF),
                        pltpu.VMEM((WIN, AW), F32), pltpu.VMEM((WIN, AW), F32), pltpu.VMEM((WIN, QB), F32)],
        compiler_params=_cp("arbitrary"))(qkv, qkv, qkv, qkv, qkv, qkv, qkv, datt, bias_wt)


TB = 512
SCH = 256
NCB = TB // SCH
GW = 8 * HD


def _tri(lower):
    r = lax.broadcasted_iota(jnp.int32, (SCH, SCH), 0)
    c = lax.broadcasted_iota(jnp.int32, (SCH, SCH), 1)
    return r >= c if lower else r <= c


def _ssd_fwd(proj, conv_w, conv_b, dt_bias, a_log, d_skip):
    s = proj.shape[0]
    nb = s // TB

    def body(xs_ref, bc_ref, dt_ref, cw_ref, cb_ref, dtb_ref, al_ref, dsk_ref, y_ref, pre_ref, prev_ref,
             xpad, ubuf, dtbuf, csb, cstb, hst, xdec, yoffb, drow):
        i = pl.program_id(0)

        @pl.when(i == 0)
        def _():
            xpad[0:8, :] = jnp.zeros((8, XBC), F32)
            hst[...] = jnp.zeros_like(hst)

        xpad[8:8 + TB, 0:SW] = xs_ref[...]
        xpad[8:8 + TB, SW:XBC] = bc_ref[...]
        pre = cb_ref[...]
        for kk in range(4):
            pre = pre + cw_ref[kk:kk + 1, :] * xpad[5 + kk:5 + kk + TB, :]
        pre_ref[...] = pre
        ubuf[...] = pre * jax.nn.sigmoid(pre)
        xpad[0:8, :] = xpad[TB:TB + 8, :]
        dtbuf[...] = _softplus(dt_ref[...] + dtb_ref[...])
        a = -jnp.exp(al_ref[...])
        tri = _tri(True).astype(F32)
        causal = _tri(True)

        def chunk(c, carry):
            r0 = pl.multiple_of(c * SCH, SCH)
            rows = pl.ds(r0, SCH)
            cs = jnp.dot(tri, dtbuf[rows, :] * a, precision=HI, preferred_element_type=F32)
            csb[...] = cs
            cstb[...] = cs.T
            for g in range(NG):
                bg = ubuf[rows, SW + NS * g:SW + NS * g + NS]
                cg = ubuf[rows, SW + NG * NS + NS * g:SW + NG * NS + NS * g + NS].astype(BF)
                cb = lax.dot_general(cg, bg.astype(BF), (((1,), (1,)), ((), ())), preferred_element_type=F32)
                hg = hst[g]
                prev_ref[c, g] = hg
                yoffb[...] = jnp.dot(cg, hg.astype(BF), preferred_element_type=F32)
                for r in range(8):
                    h = 8 * g + r
                    sl = slice(HD * h, HD * h + HD)
                    rs = slice(HD * r, HD * r + HD)
                    cs_h = csb[:, h:h + 1]
                    cl = csb[SCH - 1:SCH, h:h + 1]
                    seg = jnp.exp(jnp.where(causal, cs_h - cstb[h:h + 1, :], -jnp.inf))
                    xh = ubuf[rows, sl]
                    xdt = xh * dtbuf[rows, h:h + 1]
                    yd = jnp.dot((cb * seg).astype(BF), xdt.astype(BF), preferred_element_type=F32)
                    y_ref[rows, sl] = yd + jnp.exp(cs_h) * yoffb[:, rs] + xh * dsk_ref[:, h:h + 1]
                    xdec[:, rs] = xdt * jnp.exp(cl - cs_h)
                    drow[:, rs] = jnp.broadcast_to(jnp.exp(cl), (1, HD))
                st = jnp.dot(bg.T.astype(BF), xdec[...].astype(BF), preferred_element_type=F32)
                hst[g] = hg * drow[...] + st
            return carry

        lax.fori_loop(0, NCB, chunk, 0)

    return pl.pallas_call(
        body, name="ssd_fwd", grid=(nb,),
        in_specs=[_row(TB, SW, 4), _row(TB, 512, 10), _row(TB, LANE, 44), _fix((4, XBC)), _fix((1, XBC)),
                  _fix((1, LANE)), _fix((1, LANE)), _fix((1, LANE))],
        out_specs=[_row(TB, SW), _row(TB, XBC), pl.BlockSpec((NCB, NG, NS, GW), lambda i: (i, 0, 0, 0))],
        out_shape=[jax.ShapeDtypeStruct((s, SW), F32), jax.ShapeDtypeStruct((s, XBC), F32),
                   jax.ShapeDtypeStruct((s // SCH, NG, NS, GW), F32)],
        scratch_shapes=[pltpu.VMEM((TB + 8, XBC), F32), pltpu.VMEM((TB, XBC), F32), pltpu.VMEM((TB, LANE), F32),
                        pltpu.VMEM((SCH, LANE), F32), pltpu.VMEM((LANE, SCH), F32), pltpu.VMEM((NG, NS, GW), F32),
                        pltpu.VMEM((SCH, GW), F32), pltpu.VMEM((SCH, GW), F32), pltpu.VMEM((1, GW), F32)],
        compiler_params=_cp("arbitrary"))(proj, proj, proj, conv_w, conv_b, dt_bias, a_log, d_skip)


def _ssd_bwd(dy, pre, proj, prev, conv_w, dt_bias, a_log, d_skip):
    s = dy.shape[0]
    nb = s // TB
    rev = lambda i: nb - 1 - i
    halo = lambda i: jnp.maximum((nb - 1 - i) * (TB // 8) - 1, 0)

    def body(dy_ref, pre_ref, dt_ref, prev_ref, xs_ref, bc_ref, xsh_ref, bch_ref, cw_ref, dtb_ref, al_ref, dsk_ref,
             dx_ref, ddt_ref, pw_ref, ph_ref,
             ubuf, dtbuf, dpad, xpad, csb, cstb, dhs, yoffb, ebuf, xdecb, wbuf, drow, dub):
        i = pl.program_id(0)

        @pl.when(i == 0)
        def _():
            dhs[...] = jnp.zeros_like(dhs)
            dpad[TB:TB + 8, :] = jnp.zeros((8, XBC), F32)
            pw_ref[...] = jnp.zeros_like(pw_ref)
            ph_ref[...] = jnp.zeros_like(ph_ref)

        pre = pre_ref[...]
        ubuf[...] = pre * jax.nn.sigmoid(pre)
        dtbuf[...] = _softplus(dt_ref[...] + dtb_ref[...])
        a = -jnp.exp(al_ref[...])
        tri = _tri(True).astype(F32)
        trit = _tri(False).astype(F32)
        causal = _tri(True)
        upper = _tri(False)
        lane = lax.broadcasted_iota(jnp.int32, (SCH, LANE), 1)
        lane1 = lax.broadcasted_iota(jnp.int32, (1, LANE), 1)
        lastrow = lax.broadcasted_iota(jnp.int32, (SCH, LANE), 0) == SCH - 1

        def chunk(cc, carry):
            c = NCB - 1 - cc
            r0 = pl.multiple_of(c * SCH, SCH)
            rows = pl.ds(r0, SCH)
            dtc = dtbuf[rows, :]
            cs = jnp.dot(tri, dtc * a, precision=HI, preferred_element_type=F32)
            csb[...] = cs
            cstb[...] = cs.T
            dcs = jnp.zeros((SCH, LANE), F32)
            xr = jnp.zeros((SCH, LANE), F32)
            dlast = jnp.zeros((1, LANE), F32)
            dsk = jnp.zeros((1, LANE), F32)
            for g in range(NG):
                bg = ubuf[rows, SW + NS * g:SW + NS * g + NS]
                cg = ubuf[rows, SW + NG * NS + NS * g:SW + NG * NS + NS * g + NS]
                bb = bg.astype(BF)
                cbf = cg.astype(BF)
                cb = lax.dot_general(cbf, bb, (((1,), (1,)), ((), ())), preferred_element_type=F32)
                cbt = lax.dot_general(bb, cbf, (((1,), (1,)), ((), ())), preferred_element_type=F32)
                hp = prev_ref[c, g]
                hpb = hp.astype(BF)
                dhg = dhs[g]
                dhb = dhg.astype(BF)
                yoffb[...] = jnp.dot(cbf, hpb, preferred_element_type=F32)
                for r in range(8):
                    h = 8 * g + r
                    sl = slice(HD * h, HD * h + HD)
                    rs = slice(HD * r, HD * r + HD)
                    cs_h = csb[:, h:h + 1]
                    cl = csb[SCH - 1:SCH, h:h + 1]
                    dyh = dy_ref[rows, sl]
                    ecs = jnp.exp(cs_h)
                    ebuf[:, rs] = dyh * ecs
                    xdecb[:, rs] = ubuf[rows, sl] * dtbuf[rows, h:h + 1] * jnp.exp(cl - cs_h)
                    t1 = jnp.sum(dyh * ecs * yoffb[:, rs], axis=1, keepdims=True)
                    dcs = dcs + jnp.where(lane == h, t1, 0.0)
                    drow[:, rs] = jnp.broadcast_to(jnp.exp(cl), (1, HD))
                eb = ebuf[...].astype(BF)
                dc = lax.dot_general(eb, hpb, (((1,), (1,)), ((), ())), preferred_element_type=F32)
                dprev = jnp.dot(cg.T.astype(BF), eb, preferred_element_type=F32)
                wbuf[...] = jnp.dot(bb, dhb, preferred_element_type=F32)
                db = lax.dot_general(xdecb[...].astype(BF), dhb, (((1,), (1,)), ((), ())), preferred_element_type=F32)
                dcbs = jnp.zeros((SCH, SCH), F32)
                dcbts = jnp.zeros((SCH, SCH), F32)
                for r in range(8):
                    h = 8 * g + r
                    sl = slice(HD * h, HD * h + HD)
                    rs = slice(HD * r, HD * r + HD)
                    cs_h = csb[:, h:h + 1]
                    cst_h = cstb[h:h + 1, :]
                    cl = csb[SCH - 1:SCH, h:h + 1]
                    seg = jnp.exp(jnp.where(causal, cs_h - cst_h, -jnp.inf))
                    segt = jnp.exp(jnp.where(upper, cst_h - cs_h, -jnp.inf))
                    dyh = dy_ref[rows, sl]
                    xh = ubuf[rows, sl]
                    dt_h = dtbuf[rows, h:h + 1]
                    xdt = xh * dt_h
                    dyb = dyh.astype(BF)
                    xdb = xdt.astype(BF)
                    dm = lax.dot_general(dyb, xdb, (((1,), (1,)), ((), ())), preferred_element_type=F32)
                    dmt = lax.dot_general(xdb, dyb, (((1,), (1,)), ((), ())), preferred_element_type=F32)
                    mt = cbt * segt
                    dxdt = jnp.dot(mt.astype(BF), dyb, preferred_element_type=F32)
                    dcs_h = (jnp.sum(dm * (cb * seg), axis=1, keepdims=True)
                             - jnp.sum(dmt * mt, axis=1, keepdims=True))
                    dcbs = dcbs + dm * seg
                    dcbts = dcbts + dmt * segt
                    dec = jnp.exp(cl - cs_h)
                    w_h = wbuf[:, rs]
                    dxdt = dxdt + dec * w_h
                    t = jnp.sum(w_h * xdt, axis=1, keepdims=True) * dec
                    dcs_h = dcs_h - t
                    dch = jnp.sum(jnp.sum(dhs[g, :, rs] * prev_ref[c, g, :, rs], axis=1, keepdims=True),
                                  axis=0, keepdims=True)
                    dl = jnp.sum(t, axis=0, keepdims=True) + dch * jnp.exp(cl)
                    dcs = dcs + jnp.where(lane == h, dcs_h, 0.0)
                    dlast = dlast + jnp.where(lane1 == h, dl, 0.0)
                    xr = xr + jnp.where(lane == h, jnp.sum(dxdt * xh, axis=1, keepdims=True), 0.0)
                    dskh = jnp.sum(jnp.sum(dyh * xh, axis=1, keepdims=True), axis=0, keepdims=True)
                    dsk = dsk + jnp.where(lane1 == h, dskh, 0.0)
                    dub[:, sl] = dyh * dsk_ref[:, h:h + 1] + dxdt * dt_h
                dc = dc + jnp.dot(dcbs.astype(BF), bb, preferred_element_type=F32)
                db = db + jnp.dot(dcbts.astype(BF), cbf, preferred_element_type=F32)
                dub[:, SW + NS * g:SW + NS * g + NS] = db
                dub[:, SW + NG * NS + NS * g:SW + NG * NS + NS * g + NS] = dc
                dhs[g] = dprev + drow[...] * dhg
            dcs = dcs + jnp.where(lastrow, dlast, 0.0)
            dadt = jnp.dot(trit, dcs, precision=HI, preferred_element_type=F32)
            ddt = dadt * a + xr
            ddtr = ddt * jax.nn.sigmoid(dt_ref[rows, :] + dtb_ref[...])
            ddt_ref[rows, :] = ddtr.astype(BF)
            ph_ref[0:1, :] += jnp.sum(ddtr, axis=0, keepdims=True)
            ph_ref[1:2, :] += jnp.sum(dadt * dtc, axis=0, keepdims=True) * a
            ph_ref[2:3, :] += dsk
            pc = pre_ref[rows, :]
            sg = jax.nn.sigmoid(pc)
            dpad[rows, :] = dub[...] * _silu_grad(pc, sg)
            return carry

        lax.fori_loop(0, NCB, chunk, 0)

        dxb = cw_ref[0:1, :] * dpad[3:3 + TB, :]
        for kk in range(1, 4):
            dxb = dxb + cw_ref[kk:kk + 1, :] * dpad[3 - kk:3 - kk + TB, :]
        dx_ref[...] = dxb.astype(BF)
        keep = jnp.where(i < nb - 1, 1.0, 0.0)
        xpad[0:8, 0:SW] = xsh_ref[...] * keep
        xpad[0:8, SW:XBC] = bch_ref[...] * keep
        xpad[8:8 + TB, 0:SW] = xs_ref[...]
        xpad[8:8 + TB, SW:XBC] = bc_ref[...]
        dp = dpad[0:TB, :]
        for kk in range(4):
            pw_ref[kk:kk + 1, :] += jnp.sum(dp * xpad[5 + kk:5 + kk + TB, :], axis=0, keepdims=True)
        pw_ref[4:5, :] += jnp.sum(dp, axis=0, keepdims=True)
        dpad[TB:TB + 8, :] = dpad[0:8, :]

    rblk = lambda w, col: pl.BlockSpec((TB, w), lambda i, _c=col: (rev(i), _c))
    return pl.pallas_call(
        body, name="ssd_bwd", grid=(nb,),
        in_specs=[rblk(SW, 0), rblk(XBC, 0), rblk(LANE, 44),
                  pl.BlockSpec((NCB, NG, NS, GW), lambda i: (rev(i), 0, 0, 0)),
                  rblk(SW, 4), rblk(512, 10),
                  pl.BlockSpec((8, SW), lambda i: (halo(i), 4)), pl.BlockSpec((8, 512), lambda i: (halo(i), 10)),
                  _fix((4, XBC)), _fix((1, LANE)), _fix((1, LANE)), _fix((1, LANE))],
        out_specs=[rblk(XBC, 0), rblk(LANE, 0), _fix((8, XBC)), _fix((8, LANE))],
        out_shape=[jax.ShapeDtypeStruct((s, XBC), BF), jax.ShapeDtypeStruct((s, LANE), BF),
                   jax.ShapeDtypeStruct((8, XBC), F32), jax.ShapeDtypeStruct((8, LANE), F32)],
        scratch_shapes=[pltpu.VMEM((TB, XBC), F32), pltpu.VMEM((TB, LANE), F32), pltpu.VMEM((TB + 8, XBC), F32),
                        pltpu.VMEM((TB + 8, XBC), F32), pltpu.VMEM((SCH, LANE), F32), pltpu.VMEM((LANE, SCH), F32),
                        pltpu.VMEM((NG, NS, GW), F32), pltpu.VMEM((SCH, GW), F32), pltpu.VMEM((SCH, GW), F32),
                        pltpu.VMEM((SCH, GW), F32), pltpu.VMEM((SCH, GW), F32), pltpu.VMEM((1, GW), F32),
                        pltpu.VMEM((SCH, XBC), F32)],
        compiler_params=_cp("arbitrary"))(dy, pre, proj, prev, proj, proj, proj, proj, conv_w, dt_bias, a_log, d_skip)


def _rel_index():
    q = jnp.arange(CH)[:, None] + (BAND - CH)
    k = jnp.arange(BAND)[None, :]
    return jnp.clip(q - k, -(CH - 1), REL_CLIP) + (CH - 1)


def _local_step(x, tgt, mods, first_weights, late_weights, grads_ready,
                g_mix, rel_bias, conv_w, conv_b, dt_bias, a_log, d_skip, g_att, g_ssd, g_ffn, g_final):
    sh1, sc1, gt1, sh2, sc2, gt2 = mods
    pad16 = lambda v: jnp.pad(v, ((0, 0), (0, LANE - NH)))
    dtb_p, al_p, dsk_p = pad16(dt_bias), pad16(a_log), pad16(d_skip)
    ridx = _rel_index()
    ext = jnp.concatenate([jnp.broadcast_to(rel_bias[:, NREL - 1:], (NH, BAND + CH - 1 - NREL)),
                           rel_bias[:, ::-1]], axis=1)
    skew = jnp.tile(jnp.pad(ext, ((0, 0), (0, 1))), (1, CH + 1))[:, :CH * (BAND + CH + 1)]
    bias = skew.reshape(NH, CH, BAND + CH + 1)[:, ::-1, :BAND]
    bias_w = _bias_window(bias)
    bias_wt = jnp.swapaxes(bias_w, 1, 2)

    h1 = _norm_fwd(x, g_mix, sc1, sh1, name="norm1_fwd")
    w_in_t = first_weights(h1)
    proj = _mm([(h1, w_in_t)], "NT", F32, "in_proj", 1024, 1152, 2048)
    att = _attn2_fwd(proj, bias_w)
    y, pre, prev = _ssd_fwd(proj, conv_w, conv_b, dtb_p, al_p, dsk_p)
    cat = _mixout_fwd(att, y, proj, g_att, g_ssd)
    w_out, w_gate_t, w_up_t, w_down = late_weights(cat)
    mix = _mm([(cat, w_out)], "NN", F32, "out_proj", 1024, 1024, 2048)
    h2, x1 = _norm_fwd(x, g_ffn, sc2, sh2, mix, gt1, name="norm2_fwd")
    gate, up, act = _gate_up(h2, w_gate_t, w_up_t)
    ffn = _mm([(act, w_down)], "NN", F32, "down_proj", 512, 1024, FF)
    dx2, dffn, pf = _final(x1, ffn, gt2, g_final, tgt)

    gw_down = _mm([(act, dffn)], "TN", BF, "gw_down", 1408, 2048, 1024)
    dgate, dup = _gate_up_bwd(dffn, w_down, gate, up)
    dh2 = _mm([(dgate, w_gate_t), (dup, w_up_t)], "NN", F32, "d_h2", 1024, 1024, 1408)
    gw_gate_t = _mm([(dgate, h2)], "TN", BF, "gw_gate", 1408, 2048, 1024)
    gw_up_t = _mm([(dup, h2)], "TN", BF, "gw_up", 1408, 2048, 1024)
    tok = grads_ready("ffn", gw_gate_t, gw_up_t, gw_down)
    dx1, dmix, p2 = _norm_bwd(x1, g_ffn + tok[0, 0], sc2, sh2, dh2, dx2, mix, gt1, name="norm2_bwd")
    gw_out = _mm([(cat, dmix)], "TN", BF, "gw_out", 1024, 2048, 1024)
    tok = grads_ready("out", gw_out)
    dcat = _mm([(dmix, w_out)], "NT", F32, "d_cat", 1024, 1024, 2048, after=tok)
    datt, dy, dz, pm = _mixout_bwd(dcat, att, y, proj, g_att, g_ssd)
    dq, dk, dv, dbias_t = _attn2_bwd(proj, datt, bias_wt)
    dxbc, ddt, pw, ph = _ssd_bwd(dy, pre, proj, prev, conv_w, dtb_p, al_p, dsk_p)
    dproj = jnp.concatenate([dq, dk, dv, dz, dxbc, ddt], axis=1)
    gw_in_t = _mm([(dproj, h1)], "TN", BF, "gw_in", 1152, 2048, 1024)
    tok = grads_ready("in", gw_in_t)
    dh1 = _mm([(dproj, w_in_t)], "NN", F32, "d_h1", 512, 1024, INP, after=tok)
    gx, p1 = _norm_bwd(x, g_mix, sc1, sh1, dh1, dx1, name="norm1_bwd")

    onehot = (ridx.T.reshape(BAND * CH, 1) == jnp.arange(NREL)[None, :]).astype(F32)
    g_rel = _mm([(dbias_t.reshape(NH, BAND * CH), onehot)], "NN", F32, "g_rel_bias", NH, NREL, 4608, precision=HI)

    small = dict(
        dmods=jnp.concatenate([p1[1], p1[0], p2[3], p2[1], p2[0], pf[1]]),
        g_mix=p1[2], conv_b=pw[4], dt_bias=ph[0], a_log=ph[1], d_skip=ph[2], g_att=pm[0], g_ssd=pm[1],
        g_ffn=p2[2], g_final=pf[0], rel_bias=g_rel.reshape(-1), conv_w=pw[0:4].reshape(-1), loss=pf[2])
    return gx, small


def _exchange(srcs, gather, name):
    n = len(srcs)
    outs = [jax.ShapeDtypeStruct((NDEV,) + (s.shape if g else s.shape[1:]), s.dtype) for s, g in zip(srcs, gather)]
    outs.append(jax.ShapeDtypeStruct((8, LANE), F32))

    def body(*refs):
        src, dst, token = refs[:n], refs[n:2 * n], refs[2 * n]
        ssem, rsem, lsem = refs[2 * n + 1:]
        token[...] = jnp.zeros_like(token)
        x, y, c = lax.axis_index("x"), lax.axis_index("y"), lax.axis_index("c")
        me = 4 * x + 2 * y + c
        local = []
        for a in range(n):
            own = src[a] if gather[a] else src[a].at[me]
            cp = pltpu.make_async_copy(own, dst[a].at[me], lsem.at[a])
            cp.start()
            local.append(cp)
        remote = []
        for k in range(1, NDEV):
            px = 1 - x if k & 4 else x
            py = 1 - y if k & 2 else y
            pc = 1 - c if k & 1 else c
            pid = 4 * px + 2 * py + pc
            for a in range(n):
                s_ref = src[a] if gather[a] else src[a].at[pid]
                cp = pltpu.make_async_remote_copy(
                    src_ref=s_ref, dst_ref=dst[a].at[me], send_sem=ssem.at[a * (NDEV - 1) + k - 1],
                    recv_sem=rsem.at[a * (NDEV - 1) + k - 1], device_id=(px, py, pc), device_id_type=MESH)
                cp.start()
                remote.append(cp)
        for cp in remote:
            cp.wait()
        for cp in local:
            cp.wait()

    anyspec = pl.BlockSpec(memory_space=pl.ANY)
    return pl.pallas_call(
        body, name=name, out_shape=outs, in_specs=[anyspec] * n,
        out_specs=[anyspec] * n + [pl.BlockSpec(memory_space=pltpu.VMEM)],
        scratch_shapes=[pltpu.SemaphoreType.DMA((n * (NDEV - 1),)), pltpu.SemaphoreType.DMA((n * (NDEV - 1),)),
                        pltpu.SemaphoreType.DMA((n,))],
    )(*srcs)


_HBM = pl.BlockSpec(memory_space=pltpu.HBM)
_SEM = pl.BlockSpec(memory_space=pltpu.SEMAPHORE)
_EFFECT = pltpu.SideEffectType.DATAFLOW_SIDE_EFFECTING


def _peer_copies(src, land, ssem, rsem, gather):
    x, y, c = lax.axis_index("x"), lax.axis_index("y"), lax.axis_index("c")
    me = 4 * x + 2 * y + c
    copies = []
    for a in range(len(src)):
        for k in range(1, NDEV):
            px = 1 - x if k & 4 else x
            py = 1 - y if k & 2 else y
            pc = 1 - c if k & 1 else c
            s_ref = src[a] if gather[a] else src[a].at[4 * px + 2 * py + pc]
            idx = a * (NDEV - 1) + k - 1
            copies.append(pltpu.make_async_remote_copy(
                src_ref=s_ref, dst_ref=land[a].at[me], send_sem=ssem.at[idx], recv_sem=rsem.at[idx],
                device_id=(px, py, pc), device_id_type=MESH))
    return copies


def _xstart(srcs, gather, name):
    n = len(srcs)
    me = 4 * lax.axis_index("x") + 2 * lax.axis_index("y") + lax.axis_index("c")
    lands = []
    for s_, g_ in zip(srcs, gather):
        own = s_ if g_ else lax.dynamic_index_in_dim(s_, me, 0, keepdims=False)
        empty = lax.empty((NDEV,) + own.shape, own.dtype)
        lands.append(lax.dynamic_update_slice(empty, own[None], (me,) + (0,) * own.ndim))
    nsem = n * (NDEV - 1)

    def body(*refs):
        src, land, ssem, rsem, token = refs[:n], refs[n:2 * n], refs[2 * n], refs[2 * n + 1], refs[-1]
        for cp in _peer_copies(src, land, ssem, rsem, gather):
            cp.start()
        token[...] = jnp.zeros_like(token)

    ops = [pltpu.with_memory_space_constraint(a_, pltpu.HBM) for a_ in list(srcs) + lands]
    return pl.pallas_call(
        body, name=name,
        out_shape=(pltpu.SemaphoreType.DMA((nsem,)), pltpu.SemaphoreType.DMA((nsem,)),
                   *[pltpu.HBM(a_.shape, a_.dtype) for a_ in ops], jax.ShapeDtypeStruct((8, LANE), F32)),
        in_specs=[_HBM] * (2 * n),
        out_specs=(_SEM, _SEM, *[_HBM] * (2 * n), pl.BlockSpec(memory_space=pltpu.VMEM)),
        input_output_aliases={i: 2 + i for i in range(2 * n)},
        compiler_params=pltpu.CompilerParams(has_side_effects=_EFFECT),
    )(*ops)


def _xwait(started, gather, after, name):
    n = len(gather)
    ssem, rsem = started[0], started[1]
    bufs = started[2:2 + 2 * n]

    def body(*refs):
        src, land, ssem_, rsem_ = refs[:n], refs[n:2 * n], refs[2 * n], refs[2 * n + 1]
        for cp in _peer_copies(src, land, ssem_, rsem_, gather):
            cp.wait_send()
            cp.wait_recv()

    out = pl.pallas_call(
        body, name=name, out_shape=tuple(pltpu.HBM(a_.shape, a_.dtype) for a_ in bufs),
        in_specs=[_HBM] * (2 * n) + [_SEM, _SEM, pl.BlockSpec(memory_space=pl.ANY)],
        out_specs=[_HBM] * (2 * n), input_output_aliases={i: i for i in range(2 * n)},
        compiler_params=pltpu.CompilerParams(has_side_effects=_EFFECT),
    )(*bufs, ssem, rsem, after)
    return out[n:]


def _mods(c16, w_shard, b_shard, *, tn=512):
    n = w_shard.shape[1]

    def body(c_ref, w_ref, b_ref, o_ref):
        cv = c_ref[...]
        cond = (cv * jax.nn.sigmoid(cv)).astype(BF)
        o_ref[...] = jnp.dot(cond, w_ref[...].astype(BF), preferred_element_type=F32) + b_ref[...]

    return pl.pallas_call(body, name="ada_mods", grid=(n // tn,),
                          in_specs=[_fix((16, D)), pl.BlockSpec((D, tn), lambda j: (0, j)),
                                    pl.BlockSpec((1, tn), lambda j: (0, j))],
                          out_specs=pl.BlockSpec((16, tn), lambda j: (0, j)),
                          out_shape=jax.ShapeDtypeStruct((16, n), F32), compiler_params=_cp("parallel"))(c16, w_shard, b_shard)


def _gw_ada(c_t, dm, *, tr=256):
    n = dm.shape[1]

    def body(ct_ref, dm_ref, o_ref):
        acc = jnp.zeros((tr, n), F32)
        for b in range(NDEV):
            cv = ct_ref[:, b:b + 1]
            cond = (cv * jax.nn.sigmoid(cv)).astype(BF).astype(F32)
            acc = acc + cond * dm_ref[b:b + 1, :].astype(BF).astype(F32)
        o_ref[...] = acc

    return pl.pallas_call(body, name="gw_ada", grid=(D // tr,),
                          in_specs=[_row(tr, NDEV), _fix((NDEV, n))], out_specs=_row(tr, n),
                          out_shape=jax.ShapeDtypeStruct((D, n), F32), compiler_params=_cp("parallel"))(c_t, dm)


def _sum8(parts, *, name, tc=512):
    _, r, c = parts.shape
    tc = min(tc, c)

    def body(p_ref, o_ref):
        acc = p_ref[0].astype(F32)
        for d in range(1, NDEV):
            acc = acc + p_ref[d].astype(F32)
        o_ref[...] = acc

    return pl.pallas_call(body, name=name, grid=(c // tc,),
                          in_specs=[pl.BlockSpec((NDEV, r, tc), lambda j: (0, 0, j))],
                          out_specs=pl.BlockSpec((r, tc), lambda j: (0, j)),
                          out_shape=jax.ShapeDtypeStruct((r, c), F32), compiler_params=_cp("parallel"))(parts)


def _adam(w, m, v, *, recv=None, grad=None, tr, name):
    r, c = w.shape
    summed = recv is not None
    assert r % tr == 0

    def body(g_ref, w_ref, m_ref, v_ref, go_ref, do_ref, mo_ref, vo_ref):
        if summed:
            g = g_ref[0].astype(F32)
            for d in range(1, NDEV):
                g = g + g_ref[d].astype(F32)
        else:
            g = g_ref[...]
        go_ref[...] = g
        mn = B1 * m_ref[...] + (1.0 - B1) * g
        vn = B2 * v_ref[...] + (1.0 - B2) * (g * g)
        mo_ref[...] = mn
        vo_ref[...] = vn
        mh = mn / (1.0 - B1 ** STEP)
        vh = vn / (1.0 - B2 ** STEP)
        do_ref[...] = -LR * (mh / (jnp.sqrt(vh) + AEPS) + WD * w_ref[...])

    gspec = pl.BlockSpec((NDEV, tr, c), lambda i: (0, i, 0)) if summed else _row(tr, c)
    return pl.pallas_call(body, name=name, grid=(r // tr,),
                          in_specs=[gspec, _row(tr, c), _row(tr, c), _row(tr, c)], out_specs=[_row(tr, c)] * 4,
                          out_shape=[jax.ShapeDtypeStruct((r, c), F32)] * 4,
                          compiler_params=_cp("parallel"))(recv if summed else grad, w, m, v)


def _cols_to_blocks(g, width):
    r = g.shape[0]
    return jnp.transpose(g.reshape(r, NDEV, width), (1, 0, 2))


def _blocks_to_cols(b):
    _, r, width = b.shape
    return jnp.transpose(b, (1, 0, 2)).reshape(r, NDEV * width)


_SMALL = ("b_ada", "g_mix", "rel_bias", "conv_w", "conv_b", "dt_bias", "a_log", "d_skip",
          "g_att_out", "g_ssd_out", "g_ffn", "g_final")


def _pad_rows(v):
    v = v.reshape(-1)
    return jnp.pad(v, (0, (-v.shape[0]) % LANE))


def kernel(x, c, w_ada, b_ada, g_mix, w_in, rel_bias, conv_w, conv_b, dt_bias, a_log, d_skip, g_att_out, g_ssd_out, w_out, g_ffn, w_gate, w_up, w_down, g_final, loss_target, m_w_ada, m_b_ada, m_g_mix, m_w_in, m_rel_bias, m_conv_w, m_conv_b, m_dt_bias, m_a_log, m_d_skip, m_g_att_out, m_g_ssd_out, m_w_out, m_g_ffn, m_w_gate, m_w_up, m_w_down, m_g_final, v_w_ada, v_b_ada, v_g_mix, v_w_in, v_rel_bias, v_conv_w, v_conv_b, v_dt_bias, v_a_log, v_d_skip, v_g_att_out, v_g_ssd_out, v_w_out, v_g_ffn, v_w_gate, v_w_up, v_w_down, v_g_final):
    me = 4 * lax.axis_index("x") + 2 * lax.axis_index("y") + lax.axis_index("c")
    wts = dict(w_ada=w_ada, b_ada=b_ada, g_mix=g_mix, w_in=w_in, rel_bias=rel_bias, conv_w=conv_w, conv_b=conv_b,
               dt_bias=dt_bias, a_log=a_log, d_skip=d_skip, g_att_out=g_att_out, g_ssd_out=g_ssd_out, w_out=w_out,
               g_ffn=g_ffn, w_gate=w_gate, w_up=w_up, w_down=w_down, g_final=g_final)
    mom = dict(w_ada=m_w_ada, b_ada=m_b_ada, g_mix=m_g_mix, w_in=m_w_in, rel_bias=m_rel_bias, conv_w=m_conv_w,
               conv_b=m_conv_b, dt_bias=m_dt_bias, a_log=m_a_log, d_skip=m_d_skip, g_att_out=m_g_att_out,
               g_ssd_out=m_g_ssd_out, w_out=m_w_out, g_ffn=m_g_ffn, w_gate=m_w_gate, w_up=m_w_up, w_down=m_w_down,
               g_final=m_g_final)
    var = dict(w_ada=v_w_ada, b_ada=v_b_ada, g_mix=v_g_mix, w_in=v_w_in, rel_bias=v_rel_bias, conv_w=v_conv_w,
               conv_b=v_conv_b, dt_bias=v_dt_bias, a_log=v_a_log, d_skip=v_d_skip, g_att_out=v_g_att_out,
               g_ssd_out=v_g_ssd_out, w_out=v_w_out, g_ffn=v_g_ffn, w_gate=v_w_gate, w_up=v_w_up, w_down=v_w_down,
               g_final=v_g_final)
    order = ("w_ada", "b_ada", "g_mix", "w_in", "rel_bias", "conv_w", "conv_b", "dt_bias", "a_log", "d_skip",
             "g_att_out", "g_ssd_out", "w_out", "g_ffn", "w_gate", "w_up", "w_down", "g_final")

    c_all, rel_all, cw_all, _ = _exchange([c, rel_bias[0], conv_w[0]], [True] * 3, "gather_small")
    c_all = c_all.reshape(NDEV, D)
    rel_full = _blocks_to_cols(rel_all)
    cw_full = _blocks_to_cols(cw_all)

    ncol = w_ada.shape[2]
    b_sh = lax.dynamic_slice(b_ada, (0, me * ncol), (1, ncol))
    mods_part = _mods(jnp.pad(c_all, ((0, 8), (0, 0))), w_ada[0], b_sh)[:NDEV]
    mods_all, tok_m = _exchange([mods_part], [True], "gather_mods")
    mods_me = lax.dynamic_index_in_dim(mods_all, me, axis=1, keepdims=False).reshape(6, 1, D)
    mods = [mods_me[i] for i in range(6)]

    st_in = _xstart([(w_in[0] + tok_m[0, 0]).T.astype(BF)], [True], "gather_w_in_start")
    st_rest = _xstart([(w_out[0] + st_in[-1][0, 0]).astype(BF), w_gate[0].T.astype(BF), w_up[0].T.astype(BF),
                       w_down[0].astype(BF)], [True] * 4, "gather_w_rest_start")

    def first_weights(after):
        (gi,) = _xwait(st_in, [True], after, "gather_w_in_wait")
        return jnp.pad(gi.reshape(INC, D), ((0, INP - INC), (0, 0)))

    def late_weights(after):
        go, gg, gu_, gd = _xwait(st_rest, [True] * 4, after, "gather_w_rest_wait")
        return go.reshape(D, D), gg.reshape(FF, D), gu_.reshape(FF, D), gd.reshape(FF, D)

    started = {}

    def grads_ready(which, *g):
        if which == "in":
            g = (g[0][:INC],)
        srcs = [g_.reshape(NDEV, g_.shape[0] // NDEV, D) for g_ in g]
        started[which] = _xstart(srcs, [False] * len(srcs), f"exchange_{which}_grads_start")
        return started[which][-1]

    gx, small = _local_step(
        x[0], loss_target[0], mods, first_weights, late_weights, grads_ready,
        g_mix + st_rest[-1][0, 0], rel_full, cw_full, conv_b, dt_bias, a_log, d_skip, g_att_out, g_ssd_out, g_ffn,
        g_final.reshape(1, D))

    loss_row = jnp.pad(jnp.sum(small["loss"]).reshape(1), (0, LANE - 1))
    pack = jnp.concatenate([small[k] for k in ("dmods", "g_mix", "conv_b", "dt_bias", "a_log", "d_skip", "g_att",
                                               "g_ssd", "g_ffn", "g_final", "rel_bias", "conv_w")] + [loss_row])
    packs, _ = _exchange([pack.reshape(-1, LANE)], [True], "gather_small_grads")
    rg, ru, rd = _xwait(started["ffn"], [False] * 3, packs, "exchange_ffn_grads_wait")
    (ro,) = _xwait(started["out"], [False], rg, "exchange_out_grads_wait")
    (ri,) = _xwait(started["in"], [False], ro, "exchange_in_grads_wait")
    big = dict(
        w_gate=_adam(w_gate[0], m_w_gate[0], v_w_gate[0], grad=_sum8(rg, name="sum_w_gate").T, tr=256,
                     name="adam_w_gate"),
        w_up=_adam(w_up[0], m_w_up[0], v_w_up[0], grad=_sum8(ru, name="sum_w_up").T, tr=256, name="adam_w_up"),
        w_down=_adam(w_down[0], m_w_down[0], v_w_down[0], recv=rd, tr=176, name="adam_w_down"))
    tot = _sum8(packs, name="sum_small").reshape(-1)

    sizes = dict(dmods=6 * D, g_mix=D, conv_b=XBC, dt_bias=LANE, a_log=LANE, d_skip=LANE, g_att_out=AW, g_ssd_out=SW,
                 g_ffn=D, g_final=D, rel_bias=NH * NREL, conv_w=4 * XBC, loss=LANE)
    off, o = {}, 0
    for k_, n_ in sizes.items():
        off[k_] = o
        o += n_
    take = lambda k_, n_=None: tot[off[k_]:off[k_] + (n_ or sizes[k_])]
    loss = tot[off["loss"]]
    rel_w = NREL // NDEV
    cw_w = XBC // NDEV
    gsm = dict(
        b_ada=take("dmods"), g_mix=take("g_mix"),
        rel_bias=lax.dynamic_slice(take("rel_bias").reshape(NH, NREL), (0, me * rel_w), (NH, rel_w)),
        conv_w=lax.dynamic_slice(take("conv_w").reshape(4, XBC), (0, me * cw_w), (4, cw_w)),
        conv_b=take("conv_b"), dt_bias=take("dt_bias", NH), a_log=take("a_log", NH), d_skip=take("d_skip", NH),
        g_att_out=take("g_att_out"), g_ssd_out=take("g_ssd_out"), g_ffn=take("g_ffn"), g_final=take("g_final"))

    flat = lambda d_: jnp.concatenate([_pad_rows(d_[k_]) for k_ in _SMALL])
    gflat = flat(gsm)
    nrow = -(-gflat.shape[0] // (8 * LANE)) * 8
    to2d = lambda v_: jnp.pad(v_, (0, nrow * LANE - v_.shape[0])).reshape(nrow, LANE)
    res = _adam(to2d(flat(wts)), to2d(flat(mom)), to2d(flat(var)), grad=to2d(gflat), tr=nrow, name="adam_small")
    outs = {k_: {} for k_ in ("grad", "delta", "m", "v")}
    o = 0
    for k_ in _SMALL:
        n_ = wts[k_].size
        for kind, arr in zip(("grad", "delta", "m", "v"), res):
            outs[kind][k_] = arr.reshape(-1)[o:o + n_].reshape(wts[k_].shape)
        o += n_ + (-n_) % LANE

    dm_all = packs[:, :6 * D // LANE, :].reshape(NDEV, 6 * D)
    dm_sh = lax.dynamic_slice(dm_all, (0, me * ncol), (NDEV, ncol))
    gwa = _gw_ada(c_all.T, dm_sh)
    big.update(
        w_ada=_adam(w_ada[0], m_w_ada[0], v_w_ada[0], grad=gwa, tr=256, name="adam_w_ada"),
        w_in=_adam(w_in[0], m_w_in[0], v_w_in[0], grad=_sum8(ri, name="sum_w_in").T, tr=256, name="adam_w_in"),
        w_out=_adam(w_out[0], m_w_out[0], v_w_out[0], recv=ro, tr=128, name="adam_w_out"))
    for k_, r_ in big.items():
        for kind, arr in zip(("grad", "delta", "m", "v"), r_):
            outs[kind][k_] = arr.reshape(wts[k_].shape)

    return (loss, gx.reshape(x.shape), *[outs["grad"][k_] for k_ in order], *[outs["delta"][k_] for k_ in order],
            *[outs["m"][k_] for k_ in order], *[outs["v"][k_] for k_ in order])
```

```python
import functools

import jax
import jax.numpy as jnp
from jax import lax
from jax.experimental import pallas as pl
from jax.experimental.pallas import tpu as pltpu

F32 = jnp.float32
BF = jnp.bfloat16
HI = lax.Precision.HIGHEST

D = 2048
NH = 16
HD = 64
AW = 1024
SW = 1024
NG = 2
NS = 128
XBC = 1536
CH = 64
BAND = 576
NREL = 320
REL_CLIP = 256
FF = 5632
INC = 5648
INP = 5760
EPS = 1e-6
NDEV = 8
LANE = 128
VMEM_LIMIT = 56 * 1024 * 1024
LR, B1, B2, AEPS, WD, STEP = 0.001, 0.9, 0.999, 1e-08, 0.01, 10
MESH = pl.DeviceIdType.MESH


def _cp(*sem):
    return pltpu.CompilerParams(dimension_semantics=sem, vmem_limit_bytes=VMEM_LIMIT)


def _row(ts, w, col=0):
    return pl.BlockSpec((ts, w), lambda i, _c=col: (i, _c))


def _fix(shape, single=False):
    nd = len(shape)
    if single:
        return pl.BlockSpec(shape, lambda i, _n=nd: (0,) * _n, pipeline_mode=pl.Buffered(1))
    return pl.BlockSpec(shape, lambda i, _n=nd: (0,) * _n)


def _silu_grad(x, s):
    return s * (1.0 + x * (1.0 - s))


def _softplus(x):
    return jnp.maximum(x, 0.0) + jnp.log1p(jnp.exp(-jnp.abs(x)))


_DIMS = {"NN": (((1,), (0,)), ((), ())), "TN": (((0,), (0,)), ((), ())), "NT": (((1,), (1,)), ((), ()))}


def _mm(pairs, mode, out_dtype, name, tm, tn, tk, precision=None, after=None):
    a0, b0 = pairs[0]
    m, k = a0.shape[::-1] if mode == "TN" else a0.shape
    n = b0.shape[0] if mode == "NT" else b0.shape[1]
    tm, tn, tk = min(tm, m), min(tn, n), min(tk, k)
    nk1 = k // tk
    npair = len(pairs)
    nk = 1 if nk1 == 1 else nk1 * npair
    assert m % tm == 0 and n % tn == 0 and k % tk == 0, (name, a0.shape, b0.shape)
    dims = _DIMS[mode]

    def kloc(kk, p):
        return jnp.clip(kk - p * nk1, 0, nk1 - 1)

    def aspec(p):
        if mode == "TN":
            return pl.BlockSpec((tk, tm), lambda i, j, kk, _p=p: (kloc(kk, _p), i))
        return pl.BlockSpec((tm, tk), lambda i, j, kk, _p=p: (i, kloc(kk, _p)))

    def bspec(p):
        if mode == "NT":
            return pl.BlockSpec((tn, tk), lambda i, j, kk, _p=p: (j, kloc(kk, _p)))
        return pl.BlockSpec((tk, tn), lambda i, j, kk, _p=p: (kloc(kk, _p), j))

    nin = 2 * npair + (after is not None)

    def body(*refs):
        ab, o_ref = refs[:2 * npair], refs[nin]
        if nk == 1:
            tot = lax.dot_general(ab[0][...], ab[1][...], dims, preferred_element_type=F32, precision=precision)
            for p in range(1, npair):
                tot = tot + lax.dot_general(ab[2 * p][...], ab[2 * p + 1][...], dims, preferred_element_type=F32,
                                            precision=precision)
            o_ref[...] = tot.astype(o_ref.dtype)
            return
        acc = refs[nin + 1]
        kk = pl.program_id(2)

        @pl.when(kk == 0)
        def _():
            acc[...] = jnp.zeros_like(acc)

        for p in range(npair):
            @pl.when((kk >= p * nk1) & (kk < (p + 1) * nk1))
            def _(p=p):
                acc[...] += lax.dot_general(ab[2 * p][...], ab[2 * p + 1][...], dims, preferred_element_type=F32,
                                            precision=precision)

        @pl.when(kk == nk - 1)
        def _():
            o_ref[...] = acc[...].astype(o_ref.dtype)

    in_specs = []
    for p in range(npair):
        in_specs += [aspec(p), bspec(p)]
    operands = [t for ab_ in pairs for t in ab_]
    if after is not None:
        in_specs.append(pl.BlockSpec(memory_space=pl.ANY))
        operands.append(after)
    return pl.pallas_call(
        body, name=name, grid=(m // tm, n // tn, nk), in_specs=in_specs,
        out_specs=pl.BlockSpec((tm, tn), lambda i, j, kk: (i, j)),
        out_shape=jax.ShapeDtypeStruct((m, n), out_dtype),
        scratch_shapes=[] if nk == 1 else [pltpu.VMEM((tm, tn), F32)],
        compiler_params=_cp("parallel", "parallel", "arbitrary"),
    )(*operands)


def _norm_fwd(x, g, sc, sh, mix=None, gt=None, *, name, ts=256):
    s, d = x.shape
    has = mix is not None

    def body(*refs):
        if has:
            x_ref, g_ref, sc_ref, sh_ref, mix_ref, gt_ref, h_ref, x1_ref = refs
        else:
            x_ref, g_ref, sc_ref, sh_ref, h_ref = refs
        xv = x_ref[...]
        if has:
            xv = xv + gt_ref[...] * mix_ref[...]
            x1_ref[...] = xv
        r = lax.rsqrt(jnp.mean(xv * xv, axis=-1, keepdims=True) + EPS)
        h_ref[...] = ((xv * r) * g_ref[...] * (1.0 + sc_ref[...]) + sh_ref[...]).astype(BF)

    ins = [x, g, sc, sh] + ([mix, gt] if has else [])
    in_specs = [_row(ts, d), _fix((1, d)), _fix((1, d)), _fix((1, d))] + ([_row(ts, d), _fix((1, d))] if has else [])
    out_shape = [jax.ShapeDtypeStruct((s, d), BF)] + ([jax.ShapeDtypeStruct((s, d), F32)] if has else [])
    out_specs = [_row(ts, d)] + ([_row(ts, d)] if has else [])
    out = pl.pallas_call(body, name=name, grid=(s // ts,), in_specs=in_specs, out_specs=out_specs,
                         out_shape=out_shape, compiler_params=_cp("parallel"))(*ins)
    return out if has else out[0]


def _norm_bwd(x, g, sc, sh, dh, dres, mix=None, gt=None, *, name, ts=256):
    s, d = x.shape
    has = mix is not None

    def body(*refs):
        if has:
            x_ref, g_ref, sc_ref, sh_ref, dh_ref, dres_ref, mix_ref, gt_ref, dx_ref, dmix_ref, part_ref = refs
        else:
            x_ref, g_ref, sc_ref, sh_ref, dh_ref, dres_ref, dx_ref, part_ref = refs
        i = pl.program_id(0)
        xv = x_ref[...]
        gv = g_ref[...]
        dhv = dh_ref[...]
        r = lax.rsqrt(jnp.mean(xv * xv, axis=-1, keepdims=True) + EPS)
        xh = xv * r
        dyn = dhv * (1.0 + sc_ref[...])
        gy = dyn * gv
        dx = dres_ref[...] + r * (gy - xh * jnp.mean(gy * xh, axis=-1, keepdims=True))
        dx_ref[...] = dx

        @pl.when(i == 0)
        def _():
            part_ref[...] = jnp.zeros_like(part_ref)

        part_ref[0:1, :] += jnp.sum(dhv * (xh * gv), axis=0, keepdims=True)
        part_ref[1:2, :] += jnp.sum(dhv, axis=0, keepdims=True)
        part_ref[2:3, :] += jnp.sum(dyn * xh, axis=0, keepdims=True)
        if has:
            part_ref[3:4, :] += jnp.sum(dx * mix_ref[...], axis=0, keepdims=True)
            dmix_ref[...] = (gt_ref[...] * dx).astype(BF)

    ins = [x, g, sc, sh, dh, dres] + ([mix, gt] if has else [])
    in_specs = [_row(ts, d), _fix((1, d)), _fix((1, d)), _fix((1, d)), _row(ts, d), _row(ts, d)]
    in_specs += [_row(ts, d), _fix((1, d))] if has else []
    out_shape = [jax.ShapeDtypeStruct((s, d), F32)] + ([jax.ShapeDtypeStruct((s, d), BF)] if has else [])
    out_shape += [jax.ShapeDtypeStruct((8, d), F32)]
    out_specs = [_row(ts, d)] + ([_row(ts, d)] if has else []) + [_fix((8, d))]
    return pl.pallas_call(body, name=name, grid=(s // ts,), in_specs=in_specs, out_specs=out_specs,
                          out_shape=out_shape, compiler_params=_cp("arbitrary"))(*ins)


def _final(x1, ffn, gt2, gf, tgt, *, ts=256):
    s, d = x1.shape

    def body(x1_ref, ffn_ref, gt_ref, g_ref, t_ref, dx_ref, dffn_ref, part_ref):
        i = pl.program_id(0)
        fv = ffn_ref[...]
        gv = g_ref[...]
        xv = x1_ref[...] + gt_ref[...] * fv
        r = lax.rsqrt(jnp.mean(xv * xv, axis=-1, keepdims=True) + EPS)
        xh = xv * r
        e = xh * gv - t_ref[...]
        dy = e * (1.0 / d)
        gy = dy * gv
        dx = r * (gy - xh * jnp.mean(gy * xh, axis=-1, keepdims=True))
        dx_ref[...] = dx
        dffn_ref[...] = (gt_ref[...] * dx).astype(BF)

        @pl.when(i == 0)
        def _():
            part_ref[...] = jnp.zeros_like(part_ref)

        part_ref[0:1, :] += jnp.sum(dy * xh, axis=0, keepdims=True)
        part_ref[1:2, :] += jnp.sum(dx * fv, axis=0, keepdims=True)
        part_ref[2:3, :] += jnp.sum(e * e, axis=0, keepdims=True) * (0.5 / d)

    return pl.pallas_call(
        body, name="final_loss", grid=(s // ts,),
        in_specs=[_row(ts, d), _row(ts, d), _fix((1, d)), _fix((1, d)), _row(ts, d)],
        out_specs=[_row(ts, d), _row(ts, d), _fix((8, d))],
        out_shape=[jax.ShapeDtypeStruct((s, d), F32), jax.ShapeDtypeStruct((s, d), BF),
                   jax.ShapeDtypeStruct((8, d), F32)],
        compiler_params=_cp("arbitrary"))(x1, ffn, gt2, gf, tgt)


def _gate_up(h, wg_t, wu_t, *, tm=1024, tn=512):
    s, k = h.shape
    n = wg_t.shape[0]
    tm = min(tm, s)

    def body(h_ref, wg_ref, wu_ref, g_ref, u_ref, a_ref):
        hv = h_ref[...]
        gv = lax.dot_general(hv, wg_ref[...], _DIMS["NT"], preferred_element_type=F32)
        uv = lax.dot_general(hv, wu_ref[...], _DIMS["NT"], preferred_element_type=F32)
        g_ref[...] = gv.astype(BF)
        u_ref[...] = uv.astype(BF)
        a_ref[...] = (gv * jax.nn.sigmoid(gv) * uv).astype(BF)

    wspec = pl.BlockSpec((tn, k), lambda i, j: (j, 0))
    ospec = pl.BlockSpec((tm, tn), lambda i, j: (i, j))
    return pl.pallas_call(body, name="gate_up", grid=(s // tm, n // tn),
                          in_specs=[pl.BlockSpec((tm, k), lambda i, j: (i, 0)), wspec, wspec],
                          out_specs=[ospec] * 3, out_shape=[jax.ShapeDtypeStruct((s, n), BF)] * 3,
                          compiler_params=_cp("parallel", "parallel"))(h, wg_t, wu_t)


def _gate_up_bwd(dffn, w_down, gate, up, *, tm=1024, tn=512):
    s, k = dffn.shape
    n = w_down.shape[0]
    tm = min(tm, s)

    def body(d_ref, w_ref, g_ref, u_ref, dg_ref, du_ref):
        dav = lax.dot_general(d_ref[...], w_ref[...], _DIMS["NT"], preferred_element_type=F32)
        gv = g_ref[...].astype(F32)
        uv = u_ref[...].astype(F32)
        sg = jax.nn.sigmoid(gv)
        dg_ref[...] = (dav * uv * _silu_grad(gv, sg)).astype(BF)
        du_ref[...] = (dav * gv * sg).astype(BF)

    tile = pl.BlockSpec((tm, tn), lambda i, j: (i, j))
    return pl.pallas_call(body, name="gate_up_bwd", grid=(s // tm, n // tn),
                          in_specs=[pl.BlockSpec((tm, k), lambda i, j: (i, 0)),
                                    pl.BlockSpec((tn, k), lambda i, j: (j, 0)), tile, tile],
                          out_specs=[tile] * 2, out_shape=[jax.ShapeDtypeStruct((s, n), BF)] * 2,
                          compiler_params=_cp("parallel", "parallel"))(dffn, w_down, gate, up)


def _mixout_fwd(att, y, proj, g_att, g_ssd, *, ts=256):
    s = att.shape[0]

    def body(a_ref, y_ref, z_ref, ga_ref, gs_ref, o_ref):
        av = a_ref[...]
        ra = lax.rsqrt(jnp.mean(av * av, axis=-1, keepdims=True) + EPS)
        o_ref[:, 0:AW] = (av * ra * ga_ref[...]).astype(BF)
        zv = z_ref[...]
        yz = y_ref[...] * (zv * jax.nn.sigmoid(zv))
        rs = lax.rsqrt(jnp.mean(yz * yz, axis=-1, keepdims=True) + EPS)
        o_ref[:, AW:AW + SW] = (yz * rs * gs_ref[...]).astype(BF)

    return pl.pallas_call(body, name="mixout_fwd", grid=(s // ts,),
                          in_specs=[_row(ts, AW), _row(ts, SW), _row(ts, SW, 3), _fix((1, AW)), _fix((1, SW))],
                          out_specs=_row(ts, AW + SW), out_shape=jax.ShapeDtypeStruct((s, AW + SW), BF),
                          compiler_params=_cp("parallel"))(att, y, proj, g_att, g_ssd)


def _mixout_bwd(dcat, att, y, proj, g_att, g_ssd, *, ts=256):
    s = att.shape[0]

    def body(dc_ref, a_ref, y_ref, z_ref, ga_ref, gs_ref, da_ref, dy_ref, dz_ref, part_ref):
        i = pl.program_id(0)
        av = a_ref[...]
        dca = dc_ref[:, 0:AW]
        ra = lax.rsqrt(jnp.mean(av * av, axis=-1, keepdims=True) + EPS)
        ah = av * ra
        gy = dca * ga_ref[...]
        da_ref[...] = ra * (gy - ah * jnp.mean(gy * ah, axis=-1, keepdims=True))
        zv = z_ref[...]
        yv = y_ref[...]
        sg = jax.nn.sigmoid(zv)
        sz = zv * sg
        yz = yv * sz
        dcs = dc_ref[:, AW:AW + SW]
        rs = lax.rsqrt(jnp.mean(yz * yz, axis=-1, keepdims=True) + EPS)
        yh = yz * rs
        gys = dcs * gs_ref[...]
        dyz = rs * (gys - yh * jnp.mean(gys * yh, axis=-1, keepdims=True))
        dy_ref[...] = dyz * sz
        dz_ref[...] = (dyz * yv * _silu_grad(zv, sg)).astype(BF)

        @pl.when(i == 0)
        def _():
            part_ref[...] = jnp.zeros_like(part_ref)

        part_ref[0:1, :] += jnp.sum(dca * ah, axis=0, keepdims=True)
        part_ref[1:2, :] += jnp.sum(dcs * yh, axis=0, keepdims=True)

    return pl.pallas_call(
        body, name="mixout_bwd", grid=(s // ts,),
        in_specs=[_row(ts, AW + SW), _row(ts, AW), _row(ts, SW), _row(ts, SW, 3), _fix((1, AW)), _fix((1, SW))],
        out_specs=[_row(ts, AW), _row(ts, SW), _row(ts, SW), _fix((8, AW))],
        out_shape=[jax.ShapeDtypeStruct((s, AW), F32), jax.ShapeDtypeStruct((s, SW), F32),
                   jax.ShapeDtypeStruct((s, SW), BF), jax.ShapeDtypeStruct((8, AW), F32)],
        compiler_params=_cp("arbitrary"))(dcat, att, y, proj, g_att, g_ssd)


TQ = 512
SCALE = HD ** -0.5


def _attn_fwd(qkv, bias):
    s = qkv.shape[0]
    nb = s // TQ
    prev = lambda i: jnp.maximum(i - 1, 0)

    def body(q_ref, kp_ref, kc_ref, vp_ref, vc_ref, b_ref, o_ref, kbuf, vbuf):
        i = pl.program_id(0)
        kbuf[0:TQ, :] = kp_ref[...]
        kbuf[TQ:2 * TQ, :] = kc_ref[...]
        vbuf[0:TQ, :] = vp_ref[...]
        vbuf[TQ:2 * TQ, :] = vc_ref[...]

        def chunk(c, carry):
            r0 = pl.multiple_of(c * CH, CH)
            kpos = i * TQ + r0 - TQ + lax.broadcasted_iota(jnp.int32, (1, BAND), 1)
            valid = kpos >= 0
            for h in range(NH):
                sl = slice(HD * h, HD * h + HD)
                q = q_ref[pl.ds(r0, CH), sl]
                k = kbuf[pl.ds(r0, BAND), sl]
                v = vbuf[pl.ds(r0, BAND), sl]
                sc = lax.dot_general(q, k, (((1,), (1,)), ((), ())), preferred_element_type=F32) * SCALE + b_ref[h]
                sc = jnp.where(valid, sc, -jnp.inf)
                e = jnp.exp(sc - jnp.max(sc, axis=-1, keepdims=True))
                p = e / jnp.sum(e, axis=-1, keepdims=True)
                o_ref[pl.ds(r0, CH), sl] = jnp.dot(p.astype(BF), v, preferred_element_type=F32)
            return carry

        lax.fori_loop(0, TQ // CH, chunk, 0)

    blk = lambda col, im: pl.BlockSpec((TQ, AW), lambda i, _c=col, _f=im: (_f(i), _c))
    cur = lambda i: i
    return pl.pallas_call(
        body, name="attn_fwd", grid=(nb,),
        in_specs=[blk(0, cur), blk(1, prev), blk(1, cur), blk(2, prev), blk(2, cur), _fix((NH, CH, BAND))],
        out_specs=_row(TQ, AW), out_shape=jax.ShapeDtypeStruct((s, AW), F32),
        scratch_shapes=[pltpu.VMEM((2 * TQ, AW), BF), pltpu.VMEM((2 * TQ, AW), BF)],
        compiler_params=_cp("parallel"))(qkv, qkv, qkv, qkv, qkv, bias)


def _attn_bwd(qkv, datt, bias_t):
    s = qkv.shape[0]
    nb = s // TQ
    cur = lambda i: jnp.minimum(i, nb - 1)
    prev = lambda i: jnp.maximum(jnp.minimum(i, nb - 1) - 1, 0)
    late = lambda i: jnp.maximum(i - 1, 0)

    def body(q_ref, kp_ref, kc_ref, vp_ref, vc_ref, do_ref, b_ref, dq_ref, dk_ref, dv_ref, db_ref,
             kbuf, vbuf, dkacc, dvacc):
        i = pl.program_id(0)

        @pl.when(i == 0)
        def _():
            dkacc[...] = jnp.zeros_like(dkacc)
            dvacc[...] = jnp.zeros_like(dvacc)
            db_ref[...] = jnp.zeros_like(db_ref)

        @pl.when(i < nb)
        def _():
            kbuf[0:TQ, :] = kp_ref[...]
            kbuf[TQ:2 * TQ, :] = kc_ref[...]
            vbuf[0:TQ, :] = vp_ref[...]
            vbuf[TQ:2 * TQ, :] = vc_ref[...]

            def chunk(c, carry):
                r0 = pl.multiple_of(c * CH, CH)
                kpos = i * TQ + r0 - TQ + lax.broadcasted_iota(jnp.int32, (BAND, 1), 0)
                valid = kpos >= 0
                for h in range(NH):
                    sl = slice(HD * h, HD * h + HD)
                    q = q_ref[pl.ds(r0, CH), sl]
                    k = kbuf[pl.ds(r0, BAND), sl]
                    v = vbuf[pl.ds(r0, BAND), sl]
                    do = do_ref[pl.ds(r0, CH), sl].astype(BF)
                    st = lax.dot_general(k, q, (((1,), (1,)), ((), ())), preferred_element_type=F32) * SCALE + b_ref[h]
                    st = jnp.where(valid, st, -jnp.inf)
                    e = jnp.exp(st - jnp.max(st, axis=0, keepdims=True))
                    pt = e / jnp.sum(e, axis=0, keepdims=True)
                    dvacc[pl.ds(r0, BAND), sl] += jnp.dot(pt.astype(BF), do, preferred_element_type=F32)
                    dpt = lax.dot_general(v, do, (((1,), (1,)), ((), ())), preferred_element_type=F32)
                    dst = pt * (dpt - jnp.sum(dpt * pt, axis=0, keepdims=True))
                    db_ref[h] += dst
                    dsb = dst.astype(BF)
                    dkacc[pl.ds(r0, BAND), sl] += jnp.dot(dsb, q, preferred_element_type=F32) * SCALE
                    dq = lax.dot_general(dsb, k, (((0,), (0,)), ((), ())), preferred_element_type=F32) * SCALE
                    dq_ref[pl.ds(r0, CH), sl] = dq.astype(BF)
                return carry

            lax.fori_loop(0, TQ // CH, chunk, 0)

        dk_ref[...] = dkacc[0:TQ, :].astype(BF)
        dv_ref[...] = dvacc[0:TQ, :].astype(BF)
        dkacc[0:TQ, :] = dkacc[TQ:2 * TQ, :]
        dvacc[0:TQ, :] = dvacc[TQ:2 * TQ, :]
        dkacc[TQ:2 * TQ, :] = jnp.zeros((TQ, AW), F32)
        dvacc[TQ:2 * TQ, :] = jnp.zeros((TQ, AW), F32)

    blk = lambda col, im: pl.BlockSpec((TQ, AW), lambda i, _c=col, _f=im: (_f(i), _c))
    return pl.pallas_call(
        body, name="attn_bwd", grid=(nb + 1,),
        in_specs=[blk(0, cur), blk(1, prev), blk(1, cur), blk(2, prev), blk(2, cur), blk(0, cur),
                  _fix((NH, BAND, CH))],
        out_specs=[blk(0, cur), blk(0, late), blk(0, late), _fix((NH, BAND, CH))],
        out_shape=[jax.ShapeDtypeStruct((s, AW), BF)] * 3 + [jax.ShapeDtypeStruct((NH, BAND, CH), F32)],
        scratch_shapes=[pltpu.VMEM((2 * TQ, AW), BF), pltpu.VMEM((2 * TQ, AW), BF),
                        pltpu.VMEM((2 * TQ, AW), F32), pltpu.VMEM((2 * TQ, AW), F32)],
        compiler_params=_cp("arbitrary"))(qkv, qkv, qkv, qkv, qkv, datt, bias_t)


QB = 256
NQC = QB // CH
WIN = 3 * QB


def _bias_window(bias):
    rows = [jnp.pad(bias, ((0, 0), (0, 0), (CH * a, WIN - BAND - CH * a)), constant_values=-jnp.inf)
            for a in range(NQC)]
    return jnp.concatenate(rows, axis=1)


def _win_specs(width, nb):
    def mk(col, back):
        return pl.BlockSpec((QB, width), lambda i, _c=col, _k=back: (jnp.maximum(jnp.minimum(i, nb - 1) - _k, 0), _c))
    return [mk(1, 2), mk(1, 1), mk(1, 0), mk(2, 2), mk(2, 1), mk(2, 0)]


def _attn2_fwd(qkv, bias_w):
    s = qkv.shape[0]
    nb = s // QB

    def body(q_ref, k0, k1, k2, v0, v1, v2, b_ref, o_ref, kbuf, vbuf):
        i = pl.program_id(0)
        for j, (kr, vr) in enumerate(((k0, v0), (k1, v1), (k2, v2))):
            kbuf[QB * j:QB * j + QB, :] = kr[...].astype(BF)
            vbuf[QB * j:QB * j + QB, :] = vr[...].astype(BF)
        valid = (i * QB - 2 * QB + lax.broadcasted_iota(jnp.int32, (1, WIN), 1)) >= 0
        for h in range(NH):
            sl = slice(HD * h, HD * h + HD)
            sc = lax.dot_general(q_ref[:, sl].astype(BF), kbuf[:, sl], _DIMS["NT"], preferred_element_type=F32)
            sc = jnp.where(valid, sc * SCALE + b_ref[h], -jnp.inf)
            e = jnp.exp(sc - jnp.max(sc, axis=-1, keepdims=True))
            p = e * (1.0 / jnp.sum(e, axis=-1, keepdims=True))
            o_ref[:, sl] = jnp.dot(p.astype(BF), vbuf[:, sl], preferred_element_type=F32)

    return pl.pallas_call(
        body, name="attn_fwd", grid=(nb,),
        in_specs=[_row(QB, AW)] + _win_specs(AW, nb) + [_fix((NH, QB, WIN), single=True)],
        out_specs=_row(QB, AW), out_shape=jax.ShapeDtypeStruct((s, AW), F32),
        scratch_shapes=[pltpu.VMEM((WIN, AW), BF), pltpu.VMEM((WIN, AW), BF)],
        compiler_params=_cp("parallel"))(qkv, qkv, qkv, qkv, qkv, qkv, qkv, bias_w)


def _attn2_bwd(qkv, datt, bias_wt):
    s = qkv.shape[0]
    nb = s // QB
    cur = lambda i: jnp.minimum(i, nb - 1)
    late = lambda i: jnp.maximum(i - 2, 0)

    def body(q_ref, k0, k1, k2, v0, v1, v2, do_ref, b_ref, dq_ref, dk_ref, dv_ref, db_ref,
             kbuf, vbuf, dkacc, dvacc, dsbuf):
        i = pl.program_id(0)

        @pl.when(i == 0)
        def _():
            dkacc[...] = jnp.zeros_like(dkacc)
            dvacc[...] = jnp.zeros_like(dvacc)
            db_ref[...] = jnp.zeros_like(db_ref)

        @pl.when(i < nb)
        def _():
            for j, (kr, vr) in enumerate(((k0, v0), (k1, v1), (k2, v2))):
                kbuf[QB * j:QB * j + QB, :] = kr[...].astype(BF)
                vbuf[QB * j:QB * j + QB, :] = vr[...].astype(BF)
            valid = (i * QB - 2 * QB + lax.broadcasted_iota(jnp.int32, (WIN, 1), 0)) >= 0
            for h in range(NH):
                sl = slice(HD * h, HD * h + HD)
                q = q_ref[:, sl].astype(BF)
                k = kbuf[:, sl]
                do = do_ref[:, sl].astype(BF)
                st = lax.dot_general(k, q, (((1,), (1,)), ((), ())), preferred_element_type=F32)
                st = jnp.where(valid, st * SCALE + b_ref[h], -jnp.inf)
                e = jnp.exp(st - jnp.max(st, axis=0, keepdims=True))
                pt = e * (1.0 / jnp.sum(e, axis=0, keepdims=True))
                dvacc[:, sl] += jnp.dot(pt.astype(BF), do, preferred_element_type=F32)
                dpt = lax.dot_general(vbuf[:, sl], do, (((1,), (1,)), ((), ())), preferred_element_type=F32)
                dst = pt * (dpt - jnp.sum(dpt * pt, axis=0, keepdims=True))
                dsbuf[...] = dst
                fold = dsbuf[0:BAND, 0:CH]
                for a in range(1, NQC):
                    fold = fold + dsbuf[CH * a:CH * a + BAND, CH * a:CH * a + CH]
                db_ref[h] += fold
                dsb = dst.astype(BF)
                dkacc[:, sl] += jnp.dot(dsb, q, preferred_element_type=F32) * SCALE
                dq = lax.dot_general(dsb, k, (((0,), (0,)), ((), ())), preferred_element_type=F32) * SCALE
                dq_ref[:, sl] = dq.astype(BF)

        dk_ref[...] = dkacc[0:QB, :].astype(BF)
        dv_ref[...] = dvacc[0:QB, :].astype(BF)
        for acc in (dkacc, dvacc):
            acc[0:QB, :] = acc[QB:2 * QB, :]
            acc[QB:2 * QB, :] = acc[2 * QB:WIN, :]
            acc[2 * QB:WIN, :] = jnp.zeros((QB, AW), F32)

    qspec = pl.BlockSpec((QB, AW), lambda i: (cur(i), 0))
    lspec = pl.BlockSpec((QB, AW), lambda i: (late(i), 0))
    return pl.pallas_call(
        body, name="attn_bwd", grid=(nb + 2,),
        in_specs=[qspec] + _win_specs(AW, nb) + [qspec, _fix((NH, WIN, QB), single=True)],
        out_specs=[qspec, lspec, lspec, _fix((NH, BAND, CH))],
        out_shape=[jax.ShapeDtypeStruct((s, AW), BF)] * 3 + [jax.ShapeDtypeStruct((NH, BAND, CH), F32)],
        scratch_shapes=[pltpu.VMEM((WIN, AW), BF), pltpu.VMEM((WIN, AW), BF),
                        pltpu.VMEM((WIN, AW), F32), pltpu.VMEM((WIN, AW), F32), pltpu.VMEM((WIN, QB), F32)],
        compiler_params=_cp("arbitrary"))(qkv, qkv, qkv, qkv, qkv, qkv, qkv, datt, bias_wt)


TB = 512
SCH = 512
NCB = TB // SCH
GW = 8 * HD


def _tri(lower):
    r = lax.broadcasted_iota(jnp.int32, (SCH, SCH), 0)
    c = lax.broadcasted_iota(jnp.int32, (SCH, SCH), 1)
    return r >= c if lower else r <= c


def _ssd_fwd(proj, conv_w, conv_b, dt_bias, a_log, d_skip):
    s = proj.shape[0]
    nb = s // TB

    def body(xs_ref, bc_ref, dt_ref, cw_ref, cb_ref, dtb_ref, al_ref, dsk_ref, y_ref, pre_ref, prev_ref,
             xpad, ubuf, dtbuf, csb, cstb, hst, xdec, yoffb, drow):
        i = pl.program_id(0)

        @pl.when(i == 0)
        def _():
            xpad[0:8, :] = jnp.zeros((8, XBC), F32)
            hst[...] = jnp.zeros_like(hst)

        xpad[8:8 + TB, 0:SW] = xs_ref[...]
        xpad[8:8 + TB, SW:XBC] = bc_ref[...]
        pre = cb_ref[...]
        for kk in range(4):
            pre = pre + cw_ref[kk:kk + 1, :] * xpad[5 + kk:5 + kk + TB, :]
        pre_ref[...] = pre
        ubuf[...] = pre * jax.nn.sigmoid(pre)
        xpad[0:8, :] = xpad[TB:TB + 8, :]
        dtbuf[...] = _softplus(dt_ref[...] + dtb_ref[...])
        a = -jnp.exp(al_ref[...])
        tri = _tri(True).astype(F32)
        causal = _tri(True)

        def chunk(c, carry):
            r0 = pl.multiple_of(c * SCH, SCH)
            rows = pl.ds(r0, SCH)
            cs = jnp.dot(tri, dtbuf[rows, :] * a, precision=HI, preferred_element_type=F32)
            csb[...] = cs
            cstb[...] = cs.T
            for g in range(NG):
                bg = ubuf[rows, SW + NS * g:SW + NS * g + NS]
                cg = ubuf[rows, SW + NG * NS + NS * g:SW + NG * NS + NS * g + NS].astype(BF)
                cb = lax.dot_general(cg, bg.astype(BF), (((1,), (1,)), ((), ())), preferred_element_type=F32)
                hg = hst[g]
                prev_ref[c, g] = hg
                yoffb[...] = jnp.dot(cg, hg.astype(BF), preferred_element_type=F32)
                for r in range(8):
                    h = 8 * g + r
                    sl = slice(HD * h, HD * h + HD)
                    rs = slice(HD * r, HD * r + HD)
                    cs_h = csb[:, h:h + 1]
                    cl = csb[SCH - 1:SCH, h:h + 1]
                    seg = jnp.exp(jnp.where(causal, cs_h - cstb[h:h + 1, :], -jnp.inf))
                    xh = ubuf[rows, sl]
                    xdt = xh * dtbuf[rows, h:h + 1]
                    yd = jnp.dot((cb * seg).astype(BF), xdt.astype(BF), preferred_element_type=F32)
                    y_ref[rows, sl] = yd + jnp.exp(cs_h) * yoffb[:, rs] + xh * dsk_ref[:, h:h + 1]
                    xdec[:, rs] = xdt * jnp.exp(cl - cs_h)
                    drow[:, rs] = jnp.broadcast_to(jnp.exp(cl), (1, HD))
                st = jnp.dot(bg.T.astype(BF), xdec[...].astype(BF), preferred_element_type=F32)
                hst[g] = hg * drow[...] + st
            return carry

        lax.fori_loop(0, NCB, chunk, 0)

    return pl.pallas_call(
        body, name="ssd_fwd", grid=(nb,),
        in_specs=[_row(TB, SW, 4), _row(TB, 512, 10), _row(TB, LANE, 44), _fix((4, XBC)), _fix((1, XBC)),
                  _fix((1, LANE)), _fix((1, LANE)), _fix((1, LANE))],
        out_specs=[_row(TB, SW), _row(TB, XBC), pl.BlockSpec((NCB, NG, NS, GW), lambda i: (i, 0, 0, 0))],
        out_shape=[jax.ShapeDtypeStruct((s, SW), F32), jax.ShapeDtypeStruct((s, XBC), F32),
                   jax.ShapeDtypeStruct((s // SCH, NG, NS, GW), F32)],
        scratch_shapes=[pltpu.VMEM((TB + 8, XBC), F32), pltpu.VMEM((TB, XBC), F32), pltpu.VMEM((TB, LANE), F32),
                        pltpu.VMEM((SCH, LANE), F32), pltpu.VMEM((LANE, SCH), F32), pltpu.VMEM((NG, NS, GW), F32),
                        pltpu.VMEM((SCH, GW), F32), pltpu.VMEM((SCH, GW), F32), pltpu.VMEM((1, GW), F32)],
        compiler_params=_cp("arbitrary"))(proj, proj, proj, conv_w, conv_b, dt_bias, a_log, d_skip)


def _ssd_bwd(dy, pre, proj, prev, conv_w, dt_bias, a_log, d_skip):
    s = dy.shape[0]
    nb = s // TB
    rev = lambda i: nb - 1 - i
    halo = lambda i: jnp.maximum((nb - 1 - i) * (TB // 8) - 1, 0)

    def body(dy_ref, pre_ref, dt_ref, prev_ref, xs_ref, bc_ref, xsh_ref, bch_ref, cw_ref, dtb_ref, al_ref, dsk_ref,
             dx_ref, ddt_ref, pw_ref, ph_ref,
             ubuf, dtbuf, dpad, xpad, csb, cstb, dhs, yoffb, ebuf, xdecb, wbuf, drow, dub):
        i = pl.program_id(0)

        @pl.when(i == 0)
        def _():
            dhs[...] = jnp.zeros_like(dhs)
            dpad[TB:TB + 8, :] = jnp.zeros((8, XBC), F32)
            pw_ref[...] = jnp.zeros_like(pw_ref)
            ph_ref[...] = jnp.zeros_like(ph_ref)

        pre = pre_ref[...]
        ubuf[...] = pre * jax.nn.sigmoid(pre)
        dtbuf[...] = _softplus(dt_ref[...] + dtb_ref[...])
        a = -jnp.exp(al_ref[...])
        tri = _tri(True).astype(F32)
        trit = _tri(False).astype(F32)
        causal = _tri(True)
        rowid = lax.broadcasted_iota(jnp.int32, (LANE, SCH), 0)
        lane = lax.broadcasted_iota(jnp.int32, (SCH, LANE), 1)
        lane1 = lax.broadcasted_iota(jnp.int32, (1, LANE), 1)
        lastrow = lax.broadcasted_iota(jnp.int32, (SCH, LANE), 0) == SCH - 1

        def chunk(cc, carry):
            c = NCB - 1 - cc
            r0 = pl.multiple_of(c * SCH, SCH)
            rows = pl.ds(r0, SCH)
            dtc = dtbuf[rows, :]
            cs = jnp.dot(tri, dtc * a, precision=HI, preferred_element_type=F32)
            csb[...] = cs
            cstb[...] = cs.T
            dcs = jnp.zeros((SCH, LANE), F32)
            dcst = jnp.zeros((LANE, SCH), F32)
            xr = jnp.zeros((SCH, LANE), F32)
            dlast = jnp.zeros((1, LANE), F32)
            dsk = jnp.zeros((1, LANE), F32)
            for g in range(NG):
                bg = ubuf[rows, SW + NS * g:SW + NS * g + NS]
                cg = ubuf[rows, SW + NG * NS + NS * g:SW + NG * NS + NS * g + NS]
                bb = bg.astype(BF)
                cbf = cg.astype(BF)
                cb = lax.dot_general(cbf, bb, (((1,), (1,)), ((), ())), preferred_element_type=F32)
                hp = prev_ref[c, g]
                hpb = hp.astype(BF)
                dhg = dhs[g]
                dhb = dhg.astype(BF)
                yoffb[...] = jnp.dot(cbf, hpb, preferred_element_type=F32)
                for r in range(8):
                    h = 8 * g + r
                    sl = slice(HD * h, HD * h + HD)
                    rs = slice(HD * r, HD * r + HD)
                    cs_h = csb[:, h:h + 1]
                    cl = csb[SCH - 1:SCH, h:h + 1]
                    dyh = dy_ref[rows, sl]
                    ecs = jnp.exp(cs_h)
                    ebuf[:, rs] = dyh * ecs
                    xdecb[:, rs] = ubuf[rows, sl] * dtbuf[rows, h:h + 1] * jnp.exp(cl - cs_h)
                    t1 = jnp.sum(dyh * ecs * yoffb[:, rs], axis=1, keepdims=True)
                    dcs = dcs + jnp.where(lane == h, t1, 0.0)
                    drow[:, rs] = jnp.broadcast_to(jnp.exp(cl), (1, HD))
                eb = ebuf[...].astype(BF)
                dc = lax.dot_general(eb, hpb, (((1,), (1,)), ((), ())), preferred_element_type=F32)
                dprev = jnp.dot(cg.T.astype(BF), eb, preferred_element_type=F32)
                wbuf[...] = jnp.dot(bb, dhb, preferred_element_type=F32)
                db = lax.dot_general(xdecb[...].astype(BF), dhb, (((1,), (1,)), ((), ())), preferred_element_type=F32)
                dcbs = jnp.zeros((SCH, SCH), F32)
                for r in range(8):
                    h = 8 * g + r
                    sl = slice(HD * h, HD * h + HD)
                    rs = slice(HD * r, HD * r + HD)
                    cs_h = csb[:, h:h + 1]
                    cst_h = cstb[h:h + 1, :]
                    cl = csb[SCH - 1:SCH, h:h + 1]
                    seg = jnp.exp(jnp.where(causal, cs_h - cst_h, -jnp.inf))
                    dyh = dy_ref[rows, sl]
                    xh = ubuf[rows, sl]
                    dt_h = dtbuf[rows, h:h + 1]
                    xdt = xh * dt_h
                    dyb = dyh.astype(BF)
                    dm = lax.dot_general(dyb, xdt.astype(BF), _DIMS["NT"], preferred_element_type=F32)
                    mm_ = cb * seg
                    dxdt = lax.dot_general(mm_.astype(BF), dyb, _DIMS["TN"], preferred_element_type=F32)
                    gg = dm * mm_
                    dcs_h = jnp.sum(gg, axis=1, keepdims=True)
                    dcst = dcst + jnp.where(rowid == h, jnp.sum(gg, axis=0, keepdims=True), 0.0)
                    dcbs = dcbs + dm * seg
                    dec = jnp.exp(cl - cs_h)
                    w_h = wbuf[:, rs]
                    dxdt = dxdt + dec * w_h
                    t = jnp.sum(w_h * xdt, axis=1, keepdims=True) * dec
                    dcs_h = dcs_h - t
                    dch = jnp.sum(jnp.sum(dhs[g, :, rs] * prev_ref[c, g, :, rs], axis=1, keepdims=True),
                                  axis=0, keepdims=True)
                    dl = jnp.sum(t, axis=0, keepdims=True) + dch * jnp.exp(cl)
                    dcs = dcs + jnp.where(lane == h, dcs_h, 0.0)
                    dlast = dlast + jnp.where(lane1 == h, dl, 0.0)
                    xr = xr + jnp.where(lane == h, jnp.sum(dxdt * xh, axis=1, keepdims=True), 0.0)
                    dskh = jnp.sum(jnp.sum(dyh * xh, axis=1, keepdims=True), axis=0, keepdims=True)
                    dsk = dsk + jnp.where(lane1 == h, dskh, 0.0)
                    dub[:, sl] = dyh * dsk_ref[:, h:h + 1] + dxdt * dt_h
                dc = dc + jnp.dot(dcbs.astype(BF), bb, preferred_element_type=F32)
                db = db + lax.dot_general(dcbs.astype(BF), cbf, _DIMS["TN"], preferred_element_type=F32)
                dub[:, SW + NS * g:SW + NS * g + NS] = db
                dub[:, SW + NG * NS + NS * g:SW + NG * NS + NS * g + NS] = dc
                dhs[g] = dprev + drow[...] * dhg
            dcs = dcs - dcst.T + jnp.where(lastrow, dlast, 0.0)
            dadt = jnp.dot(trit, dcs, precision=HI, preferred_element_type=F32)
            ddt = dadt * a + xr
            ddtr = ddt * jax.nn.sigmoid(dt_ref[rows, :] + dtb_ref[...])
            ddt_ref[rows, :] = ddtr.astype(BF)
            ph_ref[0:1, :] += jnp.sum(ddtr, axis=0, keepdims=True)
            ph_ref[1:2, :] += jnp.sum(dadt * dtc, axis=0, keepdims=True) * a
            ph_ref[2:3, :] += dsk
            pc = pre_ref[rows, :]
            sg = jax.nn.sigmoid(pc)
            dpad[rows, :] = dub[...] * _silu_grad(pc, sg)
            return carry

        lax.fori_loop(0, NCB, chunk, 0)

        dxb = cw_ref[0:1, :] * dpad[3:3 + TB, :]
        for kk in range(1, 4):
            dxb = dxb + cw_ref[kk:kk + 1, :] * dpad[3 - kk:3 - kk + TB, :]
        dx_ref[...] = dxb.astype(BF)
        keep = jnp.where(i < nb - 1, 1.0, 0.0)
        xpad[0:8, 0:SW] = xsh_ref[...] * keep
        xpad[0:8, SW:XBC] = bch_ref[...] * keep
        xpad[8:8 + TB, 0:SW] = xs_ref[...]
        xpad[8:8 + TB, SW:XBC] = bc_ref[...]
        dp = dpad[0:TB, :]
        for kk in range(4):
            pw_ref[kk:kk + 1, :] += jnp.sum(dp * xpad[5 + kk:5 + kk + TB, :], axis=0, keepdims=True)
        pw_ref[4:5, :] += jnp.sum(dp, axis=0, keepdims=True)
        dpad[TB:TB + 8, :] = dpad[0:8, :]

    rblk = lambda w, col: pl.BlockSpec((TB, w), lambda i, _c=col: (rev(i), _c))
    return pl.pallas_call(
        body, name="ssd_bwd", grid=(nb,),
        in_specs=[rblk(SW, 0), rblk(XBC, 0), rblk(LANE, 44),
                  pl.BlockSpec((NCB, NG, NS, GW), lambda i: (rev(i), 0, 0, 0)),
                  rblk(SW, 4), rblk(512, 10),
                  pl.BlockSpec((8, SW), lambda i: (halo(i), 4)), pl.BlockSpec((8, 512), lambda i: (halo(i), 10)),
                  _fix((4, XBC)), _fix((1, LANE)), _fix((1, LANE)), _fix((1, LANE))],
        out_specs=[rblk(XBC, 0), rblk(LANE, 0), _fix((8, XBC)), _fix((8, LANE))],
        out_shape=[jax.ShapeDtypeStruct((s, XBC), BF), jax.ShapeDtypeStruct((s, LANE), BF),
                   jax.ShapeDtypeStruct((8, XBC), F32), jax.ShapeDtypeStruct((8, LANE), F32)],
        scratch_shapes=[pltpu.VMEM((TB, XBC), F32), pltpu.VMEM((TB, LANE), F32), pltpu.VMEM((TB + 8, XBC), F32),
                        pltpu.VMEM((TB + 8, XBC), F32), pltpu.VMEM((SCH, LANE), F32), pltpu.VMEM((LANE, SCH), F32),
                        pltpu.VMEM((NG, NS, GW), F32), pltpu.VMEM((SCH, GW), F32), pltpu.VMEM((SCH, GW), F32),
                        pltpu.VMEM((SCH, GW), F32), pltpu.VMEM((SCH, GW), F32), pltpu.VMEM((1, GW), F32),
                        pltpu.VMEM((SCH, XBC), F32)],
        compiler_params=_cp("arbitrary"))(dy, pre, proj, prev, proj, proj, proj, proj, conv_w, dt_bias, a_log, d_skip)


def _rel_index():
    q = jnp.arange(CH)[:, None] + (BAND - CH)
    k = jnp.arange(BAND)[None, :]
    return jnp.clip(q - k, -(CH - 1), REL_CLIP) + (CH - 1)


def _local_step(x, tgt, mods, first_weights, late_weights, grads_ready,
                g_mix, rel_bias, conv_w, conv_b, dt_bias, a_log, d_skip, g_att, g_ssd, g_ffn, g_final):
    sh1, sc1, gt1, sh2, sc2, gt2 = mods
    pad16 = lambda v: jnp.pad(v, ((0, 0), (0, LANE - NH)))
    dtb_p, al_p, dsk_p = pad16(dt_bias), pad16(a_log), pad16(d_skip)
    ridx = _rel_index()
    ext = jnp.concatenate([jnp.broadcast_to(rel_bias[:, NREL - 1:], (NH, BAND + CH - 1 - NREL)),
                           rel_bias[:, ::-1]], axis=1)
    skew = jnp.tile(jnp.pad(ext, ((0, 0), (0, 1))), (1, CH + 1))[:, :CH * (BAND + CH + 1)]
    bias = skew.reshape(NH, CH, BAND + CH + 1)[:, ::-1, :BAND]
    bias_w = _bias_window(bias)
    bias_wt = jnp.swapaxes(bias_w, 1, 2)

    h1 = _norm_fwd(x, g_mix, sc1, sh1, name="norm1_fwd")
    w_in_t = first_weights(h1)
    proj = _mm([(h1, w_in_t)], "NT", F32, "in_proj", 1024, 1152, 2048)
    att = _attn2_fwd(proj, bias_w)
    y, pre, prev = _ssd_fwd(proj, conv_w, conv_b, dtb_p, al_p, dsk_p)
    cat = _mixout_fwd(att, y, proj, g_att, g_ssd)
    w_out, w_gate_t, w_up_t, w_down = late_weights(cat)
    mix = _mm([(cat, w_out)], "NN", F32, "out_proj", 1024, 1024, 2048)
    h2, x1 = _norm_fwd(x, g_ffn, sc2, sh2, mix, gt1, name="norm2_fwd")
    gate, up, act = _gate_up(h2, w_gate_t, w_up_t)
    ffn = _mm([(act, w_down)], "NN", F32, "down_proj", 512, 1024, FF)
    dx2, dffn, pf = _final(x1, ffn, gt2, g_final, tgt)

    gw_down = _mm([(act, dffn)], "TN", BF, "gw_down", 1408, 2048, 1024)
    dgate, dup = _gate_up_bwd(dffn, w_down, gate, up)
    dh2 = _mm([(dgate, w_gate_t), (dup, w_up_t)], "NN", F32, "d_h2", 512, 512, FF)
    gw_gate_t = _mm([(dgate, h2)], "TN", BF, "gw_gate", 1408, 2048, 1024)
    gw_up_t = _mm([(dup, h2)], "TN", BF, "gw_up", 1408, 2048, 1024)
    tok = grads_ready("ffn", gw_gate_t, gw_up_t, gw_down)
    dx1, dmix, p2 = _norm_bwd(x1, g_ffn + tok[0, 0], sc2, sh2, dh2, dx2, mix, gt1, name="norm2_bwd")
    gw_out = _mm([(cat, dmix)], "TN", BF, "gw_out", 1024, 2048, 1024)
    tok = grads_ready("out", gw_out)
    dcat = _mm([(dmix, w_out)], "NT", F32, "d_cat", 1024, 1024, 2048, after=tok)
    datt, dy, dz, pm = _mixout_bwd(dcat, att, y, proj, g_att, g_ssd)
    dq, dk, dv, dbias_t = _attn2_bwd(proj, datt, bias_wt)
    dxbc, ddt, pw, ph = _ssd_bwd(dy, pre, proj, prev, conv_w, dtb_p, al_p, dsk_p)
    dproj = jnp.concatenate([dq, dk, dv, dz, dxbc, ddt], axis=1)
    gw_in_t = _mm([(dproj, h1)], "TN", BF, "gw_in", 1152, 2048, 1024)
    tok = grads_ready("in", gw_in_t)
    dh1 = _mm([(dproj, w_in_t)], "NN", F32, "d_h1", 512, 1024, INP, after=tok)
    gx, p1 = _norm_bwd(x, g_mix, sc1, sh1, dh1, dx1, name="norm1_bwd")

    onehot = (ridx.T.reshape(BAND * CH, 1) == jnp.arange(NREL)[None, :]).astype(F32)
    g_rel = _mm([(dbias_t.reshape(NH, BAND * CH), onehot)], "NN", F32, "g_rel_bias", NH, NREL, 4608, precision=HI)

    small = dict(
        dmods=jnp.concatenate([p1[1], p1[0], p2[3], p2[1], p2[0], pf[1]]),
        g_mix=p1[2], conv_b=pw[4], dt_bias=ph[0], a_log=ph[1], d_skip=ph[2], g_att=pm[0], g_ssd=pm[1],
        g_ffn=p2[2], g_final=pf[0], rel_bias=g_rel.reshape(-1), conv_w=pw[0:4].reshape(-1), loss=pf[2])
    return gx, small


def _exchange(srcs, gather, name):
    n = len(srcs)
    outs = [jax.ShapeDtypeStruct((NDEV,) + (s.shape if g else s.shape[1:]), s.dtype) for s, g in zip(srcs, gather)]
    outs.append(jax.ShapeDtypeStruct((8, LANE), F32))

    def body(*refs):
        src, dst, token = refs[:n], refs[n:2 * n], refs[2 * n]
        ssem, rsem, lsem = refs[2 * n + 1:]
        token[...] = jnp.zeros_like(token)
        x, y, c = lax.axis_index("x"), lax.axis_index("y"), lax.axis_index("c")
        me = 4 * x + 2 * y + c
        local = []
        for a in range(n):
            own = src[a] if gather[a] else src[a].at[me]
            cp = pltpu.make_async_copy(own, dst[a].at[me], lsem.at[a])
            cp.start()
            local.append(cp)
        remote = []
        for k in range(1, NDEV):
            px = 1 - x if k & 4 else x
            py = 1 - y if k & 2 else y
            pc = 1 - c if k & 1 else c
            pid = 4 * px + 2 * py + pc
            for a in range(n):
                s_ref = src[a] if gather[a] else src[a].at[pid]
                cp = pltpu.make_async_remote_copy(
                    src_ref=s_ref, dst_ref=dst[a].at[me], send_sem=ssem.at[a * (NDEV - 1) + k - 1],
                    recv_sem=rsem.at[a * (NDEV - 1) + k - 1], device_id=(px, py, pc), device_id_type=MESH)
                cp.start()
                remote.append(cp)
        for cp in remote:
            cp.wait()
        for cp in local:
            cp.wait()

    anyspec = pl.BlockSpec(memory_space=pl.ANY)
    return pl.pallas_call(
        body, name=name, out_shape=outs, in_specs=[anyspec] * n,
        out_specs=[anyspec] * n + [pl.BlockSpec(memory_space=pltpu.VMEM)],
        scratch_shapes=[pltpu.SemaphoreType.DMA((n * (NDEV - 1),)), pltpu.SemaphoreType.DMA((n * (NDEV - 1),)),
                        pltpu.SemaphoreType.DMA((n,))],
    )(*srcs)


_HBM = pl.BlockSpec(memory_space=pltpu.HBM)
_SEM = pl.BlockSpec(memory_space=pltpu.SEMAPHORE)
_EFFECT = pltpu.SideEffectType.DATAFLOW_SIDE_EFFECTING


def _peer_copies(src, land, ssem, rsem, gather):
    x, y, c = lax.axis_index("x"), lax.axis_index("y"), lax.axis_index("c")
    me = 4 * x + 2 * y + c
    copies = []
    for a in range(len(src)):
        for k in range(1, NDEV):
            px = 1 - x if k & 4 else x
            py = 1 - y if k & 2 else y
            pc = 1 - c if k & 1 else c
            s_ref = src[a] if gather[a] else src[a].at[4 * px + 2 * py + pc]
            idx = a * (NDEV - 1) + k - 1
            copies.append(pltpu.make_async_remote_copy(
                src_ref=s_ref, dst_ref=land[a].at[me], send_sem=ssem.at[idx], recv_sem=rsem.at[idx],
                device_id=(px, py, pc), device_id_type=MESH))
    return copies


def _xstart(srcs, gather, name):
    n = len(srcs)
    me = 4 * lax.axis_index("x") + 2 * lax.axis_index("y") + lax.axis_index("c")
    lands = []
    for s_, g_ in zip(srcs, gather):
        own = s_ if g_ else lax.dynamic_index_in_dim(s_, me, 0, keepdims=False)
        empty = lax.empty((NDEV,) + own.shape, own.dtype)
        lands.append(lax.dynamic_update_slice(empty, own[None], (me,) + (0,) * own.ndim))
    nsem = n * (NDEV - 1)

    def body(*refs):
        src, land, ssem, rsem, token = refs[:n], refs[n:2 * n], refs[2 * n], refs[2 * n + 1], refs[-1]
        for cp in _peer_copies(src, land, ssem, rsem, gather):
            cp.start()
        token[...] = jnp.zeros_like(token)

    ops = [pltpu.with_memory_space_constraint(a_, pltpu.HBM) for a_ in list(srcs) + lands]
    return pl.pallas_call(
        body, name=name,
        out_shape=(pltpu.SemaphoreType.DMA((nsem,)), pltpu.SemaphoreType.DMA((nsem,)),
                   *[pltpu.HBM(a_.shape, a_.dtype) for a_ in ops], jax.ShapeDtypeStruct((8, LANE), F32)),
        in_specs=[_HBM] * (2 * n),
        out_specs=(_SEM, _SEM, *[_HBM] * (2 * n), pl.BlockSpec(memory_space=pltpu.VMEM)),
        input_output_aliases={i: 2 + i for i in range(2 * n)},
        compiler_params=pltpu.CompilerParams(has_side_effects=_EFFECT),
    )(*ops)


def _xwait(started, gather, after, name):
    n = len(gather)
    ssem, rsem = started[0], started[1]
    bufs = started[2:2 + 2 * n]

    def body(*refs):
        src, land, ssem_, rsem_ = refs[:n], refs[n:2 * n], refs[2 * n], refs[2 * n + 1]
        for cp in _peer_copies(src, land, ssem_, rsem_, gather):
            cp.wait_send()
            cp.wait_recv()

    out = pl.pallas_call(
        body, name=name, out_shape=tuple(pltpu.HBM(a_.shape, a_.dtype) for a_ in bufs),
        in_specs=[_HBM] * (2 * n) + [_SEM, _SEM, pl.BlockSpec(memory_space=pl.ANY)],
        out_specs=[_HBM] * (2 * n), input_output_aliases={i: i for i in range(2 * n)},
        compiler_params=pltpu.CompilerParams(has_side_effects=_EFFECT),
    )(*bufs, ssem, rsem, after)
    return out[n:]


def _mods(c16, w_shard, b_shard, *, tn=512):
    n = w_shard.shape[1]

    def body(c_ref, w_ref, b_ref, o_ref):
        cv = c_ref[...]
        cond = (cv * jax.nn.sigmoid(cv)).astype(BF)
        o_ref[...] = jnp.dot(cond, w_ref[...].astype(BF), preferred_element_type=F32) + b_ref[...]

    return pl.pallas_call(body, name="ada_mods", grid=(n // tn,),
                          in_specs=[_fix((16, D)), pl.BlockSpec((D, tn), lambda j: (0, j)),
                                    pl.BlockSpec((1, tn), lambda j: (0, j))],
                          out_specs=pl.BlockSpec((16, tn), lambda j: (0, j)),
                          out_shape=jax.ShapeDtypeStruct((16, n), F32), compiler_params=_cp("parallel"))(c16, w_shard, b_shard)


def _gw_ada(c_t, dm, *, tr=256):
    n = dm.shape[1]

    def body(ct_ref, dm_ref, o_ref):
        acc = jnp.zeros((tr, n), F32)
        for b in range(NDEV):
            cv = ct_ref[:, b:b + 1]
            cond = (cv * jax.nn.sigmoid(cv)).astype(BF).astype(F32)
            acc = acc + cond * dm_ref[b:b + 1, :].astype(BF).astype(F32)
        o_ref[...] = acc

    return pl.pallas_call(body, name="gw_ada", grid=(D // tr,),
                          in_specs=[_row(tr, NDEV), _fix((NDEV, n))], out_specs=_row(tr, n),
                          out_shape=jax.ShapeDtypeStruct((D, n), F32), compiler_params=_cp("parallel"))(c_t, dm)


def _sum8(parts, *, name, tc=512):
    _, r, c = parts.shape
    tc = min(tc, c)

    def body(p_ref, o_ref):
        acc = p_ref[0].astype(F32)
        for d in range(1, NDEV):
            acc = acc + p_ref[d].astype(F32)
        o_ref[...] = acc

    return pl.pallas_call(body, name=name, grid=(c // tc,),
                          in_specs=[pl.BlockSpec((NDEV, r, tc), lambda j: (0, 0, j))],
                          out_specs=pl.BlockSpec((r, tc), lambda j: (0, j)),
                          out_shape=jax.ShapeDtypeStruct((r, c), F32), compiler_params=_cp("parallel"))(parts)


def _adam(w, m, v, *, recv=None, grad=None, tr, name):
    r, c = w.shape
    summed = recv is not None
    assert r % tr == 0

    def body(g_ref, w_ref, m_ref, v_ref, go_ref, do_ref, mo_ref, vo_ref):
        if summed:
            g = g_ref[0].astype(F32)
            for d in range(1, NDEV):
                g = g + g_ref[d].astype(F32)
        else:
            g = g_ref[...]
        go_ref[...] = g
        mn = B1 * m_ref[...] + (1.0 - B1) * g
        vn = B2 * v_ref[...] + (1.0 - B2) * (g * g)
        mo_ref[...] = mn
        vo_ref[...] = vn
        mh = mn / (1.0 - B1 ** STEP)
        vh = vn / (1.0 - B2 ** STEP)
        do_ref[...] = -LR * (mh / (jnp.sqrt(vh) + AEPS) + WD * w_ref[...])

    gspec = pl.BlockSpec((NDEV, tr, c), lambda i: (0, i, 0)) if summed else _row(tr, c)
    return pl.pallas_call(body, name=name, grid=(r // tr,),
                          in_specs=[gspec, _row(tr, c), _row(tr, c), _row(tr, c)], out_specs=[_row(tr, c)] * 4,
                          out_shape=[jax.ShapeDtypeStruct((r, c), F32)] * 4,
                          compiler_params=_cp("parallel"))(recv if summed else grad, w, m, v)


def _cols_to_blocks(g, width):
    r = g.shape[0]
    return jnp.transpose(g.reshape(r, NDEV, width), (1, 0, 2))


def _blocks_to_cols(b):
    _, r, width = b.shape
    return jnp.transpose(b, (1, 0, 2)).reshape(r, NDEV * width)


_SMALL = ("b_ada", "g_mix", "rel_bias", "conv_w", "conv_b", "dt_bias", "a_log", "d_skip",
          "g_att_out", "g_ssd_out", "g_ffn", "g_final")


def _pad_rows(v):
    v = v.reshape(-1)
    return jnp.pad(v, (0, (-v.shape[0]) % LANE))


def kernel(x, c, w_ada, b_ada, g_mix, w_in, rel_bias, conv_w, conv_b, dt_bias, a_log, d_skip, g_att_out, g_ssd_out, w_out, g_ffn, w_gate, w_up, w_down, g_final, loss_target, m_w_ada, m_b_ada, m_g_mix, m_w_in, m_rel_bias, m_conv_w, m_conv_b, m_dt_bias, m_a_log, m_d_skip, m_g_att_out, m_g_ssd_out, m_w_out, m_g_ffn, m_w_gate, m_w_up, m_w_down, m_g_final, v_w_ada, v_b_ada, v_g_mix, v_w_in, v_rel_bias, v_conv_w, v_conv_b, v_dt_bias, v_a_log, v_d_skip, v_g_att_out, v_g_ssd_out, v_w_out, v_g_ffn, v_w_gate, v_w_up, v_w_down, v_g_final):
    me = 4 * lax.axis_index("x") + 2 * lax.axis_index("y") + lax.axis_index("c")
    wts = dict(w_ada=w_ada, b_ada=b_ada, g_mix=g_mix, w_in=w_in, rel_bias=rel_bias, conv_w=conv_w, conv_b=conv_b,
               dt_bias=dt_bias, a_log=a_log, d_skip=d_skip, g_att_out=g_att_out, g_ssd_out=g_ssd_out, w_out=w_out,
               g_ffn=g_ffn, w_gate=w_gate, w_up=w_up, w_down=w_down, g_final=g_final)
    mom = dict(w_ada=m_w_ada, b_ada=m_b_ada, g_mix=m_g_mix, w_in=m_w_in, rel_bias=m_rel_bias, conv_w=m_conv_w,
               conv_b=m_conv_b, dt_bias=m_dt_bias, a_log=m_a_log, d_skip=m_d_skip, g_att_out=m_g_att_out,
               g_ssd_out=m_g_ssd_out, w_out=m_w_out, g_ffn=m_g_ffn, w_gate=m_w_gate, w_up=m_w_up, w_down=m_w_down,
               g_final=m_g_final)
    var = dict(w_ada=v_w_ada, b_ada=v_b_ada, g_mix=v_g_mix, w_in=v_w_in, rel_bias=v_rel_bias, conv_w=v_conv_w,
               conv_b=v_conv_b, dt_bias=v_dt_bias, a_log=v_a_log, d_skip=v_d_skip, g_att_out=v_g_att_out,
               g_ssd_out=v_g_ssd_out, w_out=v_w_out, g_ffn=v_g_ffn, w_gate=v_w_gate, w_up=v_w_up, w_down=v_w_down,
               g_final=v_g_final)
    order = ("w_ada", "b_ada", "g_mix", "w_in", "rel_bias", "conv_w", "conv_b", "dt_bias", "a_log", "d_skip",
             "g_att_out", "g_ssd_out", "w_out", "g_ffn", "w_gate", "w_up", "w_down", "g_final")

    c_all, rel_all, cw_all, _ = _exchange([c, rel_bias[0], conv_w[0]], [True] * 3, "gather_small")
    c_all = c_all.reshape(NDEV, D)
    rel_full = _blocks_to_cols(rel_all)
    cw_full = _blocks_to_cols(cw_all)

    ncol = w_ada.shape[2]
    b_sh = lax.dynamic_slice(b_ada, (0, me * ncol), (1, ncol))
    mods_part = _mods(jnp.pad(c_all, ((0, 8), (0, 0))), w_ada[0], b_sh)[:NDEV]
    mods_all, tok_m = _exchange([mods_part], [True], "gather_mods")
    mods_me = lax.dynamic_index_in_dim(mods_all, me, axis=1, keepdims=False).reshape(6, 1, D)
    mods = [mods_me[i] for i in range(6)]

    st_in = _xstart([(w_in[0] + tok_m[0, 0]).T.astype(BF)], [True], "gather_w_in_start")
    st_rest = _xstart([(w_out[0] + st_in[-1][0, 0]).astype(BF), w_gate[0].T.astype(BF), w_up[0].T.astype(BF),
                       w_down[0].astype(BF)], [True] * 4, "gather_w_rest_start")

    def first_weights(after):
        (gi,) = _xwait(st_in, [True], after, "gather_w_in_wait")
        return jnp.pad(gi.reshape(INC, D), ((0, INP - INC), (0, 0)))

    def late_weights(after):
        go, gg, gu_, gd = _xwait(st_rest, [True] * 4, after, "gather_w_rest_wait")
        return go.reshape(D, D), gg.reshape(FF, D), gu_.reshape(FF, D), gd.reshape(FF, D)

    started = {}

    def grads_ready(which, *g):
        if which == "in":
            g = (g[0][:INC],)
        srcs = [g_.reshape(NDEV, g_.shape[0] // NDEV, D) for g_ in g]
        started[which] = _xstart(srcs, [False] * len(srcs), f"exchange_{which}_grads_start")
        return started[which][-1]

    gx, small = _local_step(
        x[0], loss_target[0], mods, first_weights, late_weights, grads_ready,
        g_mix + st_rest[-1][0, 0], rel_full, cw_full, conv_b, dt_bias, a_log, d_skip, g_att_out, g_ssd_out, g_ffn,
        g_final.reshape(1, D))

    loss_row = jnp.pad(jnp.sum(small["loss"]).reshape(1), (0, LANE - 1))
    pack = jnp.concatenate([small[k] for k in ("dmods", "g_mix", "conv_b", "dt_bias", "a_log", "d_skip", "g_att",
                                               "g_ssd", "g_ffn", "g_final", "rel_bias", "conv_w")] + [loss_row])
    packs, _ = _exchange([pack.reshape(-1, LANE)], [True], "gather_small_grads")
    rg, ru, rd = _xwait(started["ffn"], [False] * 3, packs, "exchange_ffn_grads_wait")
    (ro,) = _xwait(started["out"], [False], rg, "exchange_out_grads_wait")
    (ri,) = _xwait(started["in"], [False], ro, "exchange_in_grads_wait")
    big = dict(
        w_gate=_adam(w_gate[0], m_w_gate[0], v_w_gate[0], grad=_sum8(rg, name="sum_w_gate").T, tr=256,
                     name="adam_w_gate"),
        w_up=_adam(w_up[0], m_w_up[0], v_w_up[0], grad=_sum8(ru, name="sum_w_up").T, tr=256, name="adam_w_up"),
        w_down=_adam(w_down[0], m_w_down[0], v_w_down[0], recv=rd, tr=176, name="adam_w_down"))
    tot = _sum8(packs, name="sum_small").reshape(-1)

    sizes = dict(dmods=6 * D, g_mix=D, conv_b=XBC, dt_bias=LANE, a_log=LANE, d_skip=LANE, g_att_out=AW, g_ssd_out=SW,
                 g_ffn=D, g_final=D, rel_bias=NH * NREL, conv_w=4 * XBC, loss=LANE)
    off, o = {}, 0
    for k_, n_ in sizes.items():
        off[k_] = o
        o += n_
    take = lambda k_, n_=None: tot[off[k_]:off[k_] + (n_ or sizes[k_])]
    loss = tot[off["loss"]]
    rel_w = NREL // NDEV
    cw_w = XBC // NDEV
    gsm = dict(
        b_ada=take("dmods"), g_mix=take("g_mix"),
        rel_bias=lax.dynamic_slice(take("rel_bias").reshape(NH, NREL), (0, me * rel_w), (NH, rel_w)),
        conv_w=lax.dynamic_slice(take("conv_w").reshape(4, XBC), (0, me * cw_w), (4, cw_w)),
        conv_b=take("conv_b"), dt_bias=take("dt_bias", NH), a_log=take("a_log", NH), d_skip=take("d_skip", NH),
        g_att_out=take("g_att_out"), g_ssd_out=take("g_ssd_out"), g_ffn=take("g_ffn"), g_final=take("g_final"))

    flat = lambda d_: jnp.concatenate([_pad_rows(d_[k_]) for k_ in _SMALL])
    gflat = flat(gsm)
    nrow = -(-gflat.shape[0] // (8 * LANE)) * 8
    to2d = lambda v_: jnp.pad(v_, (0, nrow * LANE - v_.shape[0])).reshape(nrow, LANE)
    res = _adam(to2d(flat(wts)), to2d(flat(mom)), to2d(flat(var)), grad=to2d(gflat), tr=nrow, name="adam_small")
    outs = {k_: {} for k_ in ("grad", "delta", "m", "v")}
    o = 0
    for k_ in _SMALL:
        n_ = wts[k_].size
        for kind, arr in zip(("grad", "delta", "m", "v"), res):
            outs[kind][k_] = arr.reshape(-1)[o:o + n_].reshape(wts[k_].shape)
        o += n_ + (-n_) % LANE

    dm_all = packs[:, :6 * D // LANE, :].reshape(NDEV, 6 * D)
    dm_sh = lax.dynamic_slice(dm_all, (0, me * ncol), (NDEV, ncol))
    gwa = _gw_ada(c_all.T, dm_sh)
    big.update(
        w_ada=_adam(w_ada[0], m_w_ada[0], v_w_ada[0], grad=gwa, tr=256, name="adam_w_ada"),
        w_in=_adam(w_in[0], m_w_in[0], v_w_in[0], grad=_sum8(ri, name="sum_w_in").T, tr=256, name="adam_w_in"),
        w_out=_adam(w_out[0], m_w_out[0], v_w_out[0], recv=ro, tr=128, name="adam_w_out"))
    for k_, r_ in big.items():
        for kind, arr in zip(("grad", "delta", "m", "v"), r_):
            outs[kind][k_] = arr.reshape(wts[k_].shape)

    return (loss, gx.reshape(x.shape), *[outs["grad"][k_] for k_ in order], *[outs["delta"][k_] for k_ in order],
            *[outs["m"][k_] for k_ in order], *[outs["v"][k_] for k_ in order])
```

```python
import functools

import jax
import jax.numpy as jnp
from jax import lax
from jax.experimental import pallas as pl
from jax.experimental.pallas import tpu as pltpu

F32 = jnp.float32
BF = jnp.bfloat16
HI = lax.Precision.HIGHEST

D = 2048
NH = 16
HD = 64
AW = 1024
SW = 1024
NG = 2
NS = 128
XBC = 1536
CH = 64
BAND = 576
NREL = 320
REL_CLIP = 256
FF = 5632
INC = 5648
INP = 5760
EPS = 1e-6
NDEV = 8
LANE = 128
VMEM_LIMIT = 56 * 1024 * 1024
LR, B1, B2, AEPS, WD, STEP = 0.001, 0.9, 0.999, 1e-08, 0.01, 10
MESH = pl.DeviceIdType.MESH


def _cp(*sem):
    return pltpu.CompilerParams(dimension_semantics=sem, vmem_limit_bytes=VMEM_LIMIT)


def _row(ts, w, col=0):
    return pl.BlockSpec((ts, w), lambda i, _c=col: (i, _c))


def _fix(shape, single=False):
    nd = len(shape)
    if single:
        return pl.BlockSpec(shape, lambda i, _n=nd: (0,) * _n, pipeline_mode=pl.Buffered(1))
    return pl.BlockSpec(shape, lambda i, _n=nd: (0,) * _n)


def _silu_grad(x, s):
    return s * (1.0 + x * (1.0 - s))


def _softplus(x):
    return jnp.maximum(x, 0.0) + jnp.log1p(jnp.exp(-jnp.abs(x)))


_DIMS = {"NN": (((1,), (0,)), ((), ())), "TN": (((0,), (0,)), ((), ())), "NT": (((1,), (1,)), ((), ()))}


def _mm(pairs, mode, out_dtype, name, tm, tn, tk, precision=None, after=None):
    a0, b0 = pairs[0]
    m, k = a0.shape[::-1] if mode == "TN" else a0.shape
    n = b0.shape[0] if mode == "NT" else b0.shape[1]
    tm, tn, tk = min(tm, m), min(tn, n), min(tk, k)
    nk1 = k // tk
    npair = len(pairs)
    nk = 1 if nk1 == 1 else nk1 * npair
    assert m % tm == 0 and n % tn == 0 and k % tk == 0, (name, a0.shape, b0.shape)
    dims = _DIMS[mode]

    def kloc(kk, p):
        return jnp.clip(kk - p * nk1, 0, nk1 - 1)

    def aspec(p):
        if mode == "TN":
            return pl.BlockSpec((tk, tm), lambda i, j, kk, _p=p: (kloc(kk, _p), i))
        return pl.BlockSpec((tm, tk), lambda i, j, kk, _p=p: (i, kloc(kk, _p)))

    def bspec(p):
        if mode == "NT":
            return pl.BlockSpec((tn, tk), lambda i, j, kk, _p=p: (j, kloc(kk, _p)))
        return pl.BlockSpec((tk, tn), lambda i, j, kk, _p=p: (kloc(kk, _p), j))

    nin = 2 * npair + (after is not None)

    def body(*refs):
        ab, o_ref = refs[:2 * npair], refs[nin]
        if nk == 1:
            tot = lax.dot_general(ab[0][...], ab[1][...], dims, preferred_element_type=F32, precision=precision)
            for p in range(1, npair):
                tot = tot + lax.dot_general(ab[2 * p][...], ab[2 * p + 1][...], dims, preferred_element_type=F32,
                                            precision=precision)
            o_ref[...] = tot.astype(o_ref.dtype)
            return
        acc = refs[nin + 1]
        kk = pl.program_id(2)

        @pl.when(kk == 0)
        def _():
            acc[...] = jnp.zeros_like(acc)

        for p in range(npair):
            @pl.when((kk >= p * nk1) & (kk < (p + 1) * nk1))
            def _(p=p):
                acc[...] += lax.dot_general(ab[2 * p][...], ab[2 * p + 1][...], dims, preferred_element_type=F32,
                                            precision=precision)

        @pl.when(kk == nk - 1)
        def _():
            o_ref[...] = acc[...].astype(o_ref.dtype)

    in_specs = []
    for p in range(npair):
        in_specs += [aspec(p), bspec(p)]
    operands = [t for ab_ in pairs for t in ab_]
    if after is not None:
        in_specs.append(pl.BlockSpec(memory_space=pl.ANY))
        operands.append(after)
    return pl.pallas_call(
        body, name=name, grid=(m // tm, n // tn, nk), in_specs=in_specs,
        out_specs=pl.BlockSpec((tm, tn), lambda i, j, kk: (i, j)),
        out_shape=jax.ShapeDtypeStruct((m, n), out_dtype),
        scratch_shapes=[] if nk == 1 else [pltpu.VMEM((tm, tn), F32)],
        compiler_params=_cp("parallel", "parallel", "arbitrary"),
    )(*operands)


def _norm_fwd(x, g, sc, sh, mix=None, gt=None, *, name, ts=256):
    s, d = x.shape
    has = mix is not None

    def body(*refs):
        if has:
            x_ref, g_ref, sc_ref, sh_ref, mix_ref, gt_ref, h_ref, x1_ref = refs
        else:
            x_ref, g_ref, sc_ref, sh_ref, h_ref = refs
        xv = x_ref[...]
        if has:
            xv = xv + gt_ref[...] * mix_ref[...]
            x1_ref[...] = xv
        r = lax.rsqrt(jnp.mean(xv * xv, axis=-1, keepdims=True) + EPS)
        h_ref[...] = ((xv * r) * g_ref[...] * (1.0 + sc_ref[...]) + sh_ref[...]).astype(BF)

    ins = [x, g, sc, sh] + ([mix, gt] if has else [])
    in_specs = [_row(ts, d), _fix((1, d)), _fix((1, d)), _fix((1, d))] + ([_row(ts, d), _fix((1, d))] if has else [])
    out_shape = [jax.ShapeDtypeStruct((s, d), BF)] + ([jax.ShapeDtypeStruct((s, d), F32)] if has else [])
    out_specs = [_row(ts, d)] + ([_row(ts, d)] if has else [])
    out = pl.pallas_call(body, name=name, grid=(s // ts,), in_specs=in_specs, out_specs=out_specs,
                         out_shape=out_shape, compiler_params=_cp("parallel"))(*ins)
    return out if has else out[0]


def _norm_bwd(x, g, sc, sh, dh, dres, mix=None, gt=None, *, name, ts=256):
    s, d = x.shape
    has = mix is not None

    def body(*refs):
        if has:
            x_ref, g_ref, sc_ref, sh_ref, dh_ref, dres_ref, mix_ref, gt_ref, dx_ref, dmix_ref, part_ref = refs
        else:
            x_ref, g_ref, sc_ref, sh_ref, dh_ref, dres_ref, dx_ref, part_ref = refs
        i = pl.program_id(0)
        xv = x_ref[...]
        gv = g_ref[...]
        dhv = dh_ref[...]
        r = lax.rsqrt(jnp.mean(xv * xv, axis=-1, keepdims=True) + EPS)
        xh = xv * r
        dyn = dhv * (1.0 + sc_ref[...])
        gy = dyn * gv
        dx = dres_ref[...] + r * (gy - xh * jnp.mean(gy * xh, axis=-1, keepdims=True))
        dx_ref[...] = dx

        @pl.when(i == 0)
        def _():
            part_ref[...] = jnp.zeros_like(part_ref)

        part_ref[0:1, :] += jnp.sum(dhv * (xh * gv), axis=0, keepdims=True)
        part_ref[1:2, :] += jnp.sum(dhv, axis=0, keepdims=True)
        part_ref[2:3, :] += jnp.sum(dyn * xh, axis=0, keepdims=True)
        if has:
            part_ref[3:4, :] += jnp.sum(dx * mix_ref[...], axis=0, keepdims=True)
            dmix_ref[...] = (gt_ref[...] * dx).astype(BF)

    ins = [x, g, sc, sh, dh, dres] + ([mix, gt] if has else [])
    in_specs = [_row(ts, d), _fix((1, d)), _fix((1, d)), _fix((1, d)), _row(ts, d), _row(ts, d)]
    in_specs += [_row(ts, d), _fix((1, d))] if has else []
    out_shape = [jax.ShapeDtypeStruct((s, d), F32)] + ([jax.ShapeDtypeStruct((s, d), BF)] if has else [])
    out_shape += [jax.ShapeDtypeStruct((8, d), F32)]
    out_specs = [_row(ts, d)] + ([_row(ts, d)] if has else []) + [_fix((8, d))]
    return pl.pallas_call(body, name=name, grid=(s // ts,), in_specs=in_specs, out_specs=out_specs,
                          out_shape=out_shape, compiler_params=_cp("arbitrary"))(*ins)


def _final(x1, ffn, gt2, gf, tgt, *, ts=256):
    s, d = x1.shape

    def body(x1_ref, ffn_ref, gt_ref, g_ref, t_ref, dx_ref, dffn_ref, part_ref):
        i = pl.program_id(0)
        fv = ffn_ref[...]
        gv = g_ref[...]
        xv = x1_ref[...] + gt_ref[...] * fv
        r = lax.rsqrt(jnp.mean(xv * xv, axis=-1, keepdims=True) + EPS)
        xh = xv * r
        e = xh * gv - t_ref[...]
        dy = e * (1.0 / d)
        gy = dy * gv
        dx = r * (gy - xh * jnp.mean(gy * xh, axis=-1, keepdims=True))
        dx_ref[...] = dx
        dffn_ref[...] = (gt_ref[...] * dx).astype(BF)

        @pl.when(i == 0)
        def _():
            part_ref[...] = jnp.zeros_like(part_ref)

        part_ref[0:1, :] += jnp.sum(dy * xh, axis=0, keepdims=True)
        part_ref[1:2, :] += jnp.sum(dx * fv, axis=0, keepdims=True)
        part_ref[2:3, :] += jnp.sum(e * e, axis=0, keepdims=True) * (0.5 / d)

    return pl.pallas_call(
        body, name="final_loss", grid=(s // ts,),
        in_specs=[_row(ts, d), _row(ts, d), _fix((1, d)), _fix((1, d)), _row(ts, d)],
        out_specs=[_row(ts, d), _row(ts, d), _fix((8, d))],
        out_shape=[jax.ShapeDtypeStruct((s, d), F32), jax.ShapeDtypeStruct((s, d), BF),
                   jax.ShapeDtypeStruct((8, d), F32)],
        compiler_params=_cp("arbitrary"))(x1, ffn, gt2, gf, tgt)


def _gate_up(h, wg_t, wu_t, *, tm=1024, tn=512):
    s, k = h.shape
    n = wg_t.shape[0]
    tm = min(tm, s)

    def body(h_ref, wg_ref, wu_ref, g_ref, u_ref, a_ref):
        hv = h_ref[...]
        gv = lax.dot_general(hv, wg_ref[...], _DIMS["NT"], preferred_element_type=F32)
        uv = lax.dot_general(hv, wu_ref[...], _DIMS["NT"], preferred_element_type=F32)
        g_ref[...] = gv.astype(BF)
        u_ref[...] = uv.astype(BF)
        a_ref[...] = (gv * jax.nn.sigmoid(gv) * uv).astype(BF)

    wspec = pl.BlockSpec((tn, k), lambda i, j: (j, 0))
    ospec = pl.BlockSpec((tm, tn), lambda i, j: (i, j))
    return pl.pallas_call(body, name="gate_up", grid=(s // tm, n // tn),
                          in_specs=[pl.BlockSpec((tm, k), lambda i, j: (i, 0)), wspec, wspec],
                          out_specs=[ospec] * 3, out_shape=[jax.ShapeDtypeStruct((s, n), BF)] * 3,
                          compiler_params=_cp("parallel", "parallel"))(h, wg_t, wu_t)


def _gate_up_bwd(dffn, w_down, gate, up, *, tm=1024, tn=512):
    s, k = dffn.shape
    n = w_down.shape[0]
    tm = min(tm, s)

    def body(d_ref, w_ref, g_ref, u_ref, dg_ref, du_ref):
        dav = lax.dot_general(d_ref[...], w_ref[...], _DIMS["NT"], preferred_element_type=F32)
        gv = g_ref[...].astype(F32)
        uv = u_ref[...].astype(F32)
        sg = jax.nn.sigmoid(gv)
        dg_ref[...] = (dav * uv * _silu_grad(gv, sg)).astype(BF)
        du_ref[...] = (dav * gv * sg).astype(BF)

    tile = pl.BlockSpec((tm, tn), lambda i, j: (i, j))
    return pl.pallas_call(body, name="gate_up_bwd", grid=(s // tm, n // tn),
                          in_specs=[pl.BlockSpec((tm, k), lambda i, j: (i, 0)),
                                    pl.BlockSpec((tn, k), lambda i, j: (j, 0)), tile, tile],
                          out_specs=[tile] * 2, out_shape=[jax.ShapeDtypeStruct((s, n), BF)] * 2,
                          compiler_params=_cp("parallel", "parallel"))(dffn, w_down, gate, up)


def _mixout_fwd(att, y, proj, g_att, g_ssd, *, ts=256):
    s = att.shape[0]

    def body(a_ref, y_ref, z_ref, ga_ref, gs_ref, o_ref):
        av = a_ref[...]
        ra = lax.rsqrt(jnp.mean(av * av, axis=-1, keepdims=True) + EPS)
        o_ref[:, 0:AW] = (av * ra * ga_ref[...]).astype(BF)
        zv = z_ref[...]
        yz = y_ref[...] * (zv * jax.nn.sigmoid(zv))
        rs = lax.rsqrt(jnp.mean(yz * yz, axis=-1, keepdims=True) + EPS)
        o_ref[:, AW:AW + SW] = (yz * rs * gs_ref[...]).astype(BF)

    return pl.pallas_call(body, name="mixout_fwd", grid=(s // ts,),
                          in_specs=[_row(ts, AW), _row(ts, SW), _row(ts, SW, 3), _fix((1, AW)), _fix((1, SW))],
                          out_specs=_row(ts, AW + SW), out_shape=jax.ShapeDtypeStruct((s, AW + SW), BF),
                          compiler_params=_cp("parallel"))(att, y, proj, g_att, g_ssd)


def _mixout_bwd(dcat, att, y, proj, g_att, g_ssd, *, ts=256):
    s = att.shape[0]

    def body(dc_ref, a_ref, y_ref, z_ref, ga_ref, gs_ref, da_ref, dy_ref, dz_ref, part_ref):
        i = pl.program_id(0)
        av = a_ref[...]
        dca = dc_ref[:, 0:AW]
        ra = lax.rsqrt(jnp.mean(av * av, axis=-1, keepdims=True) + EPS)
        ah = av * ra
        gy = dca * ga_ref[...]
        da_ref[...] = ra * (gy - ah * jnp.mean(gy * ah, axis=-1, keepdims=True))
        zv = z_ref[...]
        yv = y_ref[...]
        sg = jax.nn.sigmoid(zv)
        sz = zv * sg
        yz = yv * sz
        dcs = dc_ref[:, AW:AW + SW]
        rs = lax.rsqrt(jnp.mean(yz * yz, axis=-1, keepdims=True) + EPS)
        yh = yz * rs
        gys = dcs * gs_ref[...]
        dyz = rs * (gys - yh * jnp.mean(gys * yh, axis=-1, keepdims=True))
        dy_ref[...] = dyz * sz
        dz_ref[...] = (dyz * yv * _silu_grad(zv, sg)).astype(BF)

        @pl.when(i == 0)
        def _():
            part_ref[...] = jnp.zeros_like(part_ref)

        part_ref[0:1, :] += jnp.sum(dca * ah, axis=0, keepdims=True)
        part_ref[1:2, :] += jnp.sum(dcs * yh, axis=0, keepdims=True)

    return pl.pallas_call(
        body, name="mixout_bwd", grid=(s // ts,),
        in_specs=[_row(ts, AW + SW), _row(ts, AW), _row(ts, SW), _row(ts, SW, 3), _fix((1, AW)), _fix((1, SW))],
        out_specs=[_row(ts, AW), _row(ts, SW), _row(ts, SW), _fix((8, AW))],
        out_shape=[jax.ShapeDtypeStruct((s, AW), F32), jax.ShapeDtypeStruct((s, SW), F32),
                   jax.ShapeDtypeStruct((s, SW), BF), jax.ShapeDtypeStruct((8, AW), F32)],
        compiler_params=_cp("arbitrary"))(dcat, att, y, proj, g_att, g_ssd)


TQ = 512
SCALE = HD ** -0.5


def _attn_fwd(qkv, bias):
    s = qkv.shape[0]
    nb = s // TQ
    prev = lambda i: jnp.maximum(i - 1, 0)

    def body(q_ref, kp_ref, kc_ref, vp_ref, vc_ref, b_ref, o_ref, kbuf, vbuf):
        i = pl.program_id(0)
        kbuf[0:TQ, :] = kp_ref[...]
        kbuf[TQ:2 * TQ, :] = kc_ref[...]
        vbuf[0:TQ, :] = vp_ref[...]
        vbuf[TQ:2 * TQ, :] = vc_ref[...]

        def chunk(c, carry):
            r0 = pl.multiple_of(c * CH, CH)
            kpos = i * TQ + r0 - TQ + lax.broadcasted_iota(jnp.int32, (1, BAND), 1)
            valid = kpos >= 0
            for h in range(NH):
                sl = slice(HD * h, HD * h + HD)
                q = q_ref[pl.ds(r0, CH), sl]
                k = kbuf[pl.ds(r0, BAND), sl]
                v = vbuf[pl.ds(r0, BAND), sl]
                sc = lax.dot_general(q, k, (((1,), (1,)), ((), ())), preferred_element_type=F32) * SCALE + b_ref[h]
                sc = jnp.where(valid, sc, -jnp.inf)
                e = jnp.exp(sc - jnp.max(sc, axis=-1, keepdims=True))
                p = e / jnp.sum(e, axis=-1, keepdims=True)
                o_ref[pl.ds(r0, CH), sl] = jnp.dot(p.astype(BF), v, preferred_element_type=F32)
            return carry

        lax.fori_loop(0, TQ // CH, chunk, 0)

    blk = lambda col, im: pl.BlockSpec((TQ, AW), lambda i, _c=col, _f=im: (_f(i), _c))
    cur = lambda i: i
    return pl.pallas_call(
        body, name="attn_fwd", grid=(nb,),
        in_specs=[blk(0, cur), blk(1, prev), blk(1, cur), blk(2, prev), blk(2, cur), _fix((NH, CH, BAND))],
        out_specs=_row(TQ, AW), out_shape=jax.ShapeDtypeStruct((s, AW), F32),
        scratch_shapes=[pltpu.VMEM((2 * TQ, AW), BF), pltpu.VMEM((2 * TQ, AW), BF)],
        compiler_params=_cp("parallel"))(qkv, qkv, qkv, qkv, qkv, bias)


def _attn_bwd(qkv, datt, bias_t):
    s = qkv.shape[0]
    nb = s // TQ
    cur = lambda i: jnp.minimum(i, nb - 1)
    prev = lambda i: jnp.maximum(jnp.minimum(i, nb - 1) - 1, 0)
    late = lambda i: jnp.maximum(i - 1, 0)

    def body(q_ref, kp_ref, kc_ref, vp_ref, vc_ref, do_ref, b_ref, dq_ref, dk_ref, dv_ref, db_ref,
             kbuf, vbuf, dkacc, dvacc):
        i = pl.program_id(0)

        @pl.when(i == 0)
        def _():
            dkacc[...] = jnp.zeros_like(dkacc)
            dvacc[...] = jnp.zeros_like(dvacc)
            db_ref[...] = jnp.zeros_like(db_ref)

        @pl.when(i < nb)
        def _():
            kbuf[0:TQ, :] = kp_ref[...]
            kbuf[TQ:2 * TQ, :] = kc_ref[...]
            vbuf[0:TQ, :] = vp_ref[...]
            vbuf[TQ:2 * TQ, :] = vc_ref[...]

            def chunk(c, carry):
                r0 = pl.multiple_of(c * CH, CH)
                kpos = i * TQ + r0 - TQ + lax.broadcasted_iota(jnp.int32, (BAND, 1), 0)
                valid = kpos >= 0
                for h in range(NH):
                    sl = slice(HD * h, HD * h + HD)
                    q = q_ref[pl.ds(r0, CH), sl]
                    k = kbuf[pl.ds(r0, BAND), sl]
                    v = vbuf[pl.ds(r0, BAND), sl]
                    do = do_ref[pl.ds(r0, CH), sl].astype(BF)
                    st = lax.dot_general(k, q, (((1,), (1,)), ((), ())), preferred_element_type=F32) * SCALE + b_ref[h]
                    st = jnp.where(valid, st, -jnp.inf)
                    e = jnp.exp(st - jnp.max(st, axis=0, keepdims=True))
                    pt = e / jnp.sum(e, axis=0, keepdims=True)
                    dvacc[pl.ds(r0, BAND), sl] += jnp.dot(pt.astype(BF), do, preferred_element_type=F32)
                    dpt = lax.dot_general(v, do, (((1,), (1,)), ((), ())), preferred_element_type=F32)
                    dst = pt * (dpt - jnp.sum(dpt * pt, axis=0, keepdims=True))
                    db_ref[h] += dst
                    dsb = dst.astype(BF)
                    dkacc[pl.ds(r0, BAND), sl] += jnp.dot(dsb, q, preferred_element_type=F32) * SCALE
                    dq = lax.dot_general(dsb, k, (((0,), (0,)), ((), ())), preferred_element_type=F32) * SCALE
                    dq_ref[pl.ds(r0, CH), sl] = dq.astype(BF)
                return carry

            lax.fori_loop(0, TQ // CH, chunk, 0)

        dk_ref[...] = dkacc[0:TQ, :].astype(BF)
        dv_ref[...] = dvacc[0:TQ, :].astype(BF)
        dkacc[0:TQ, :] = dkacc[TQ:2 * TQ, :]
        dvacc[0:TQ, :] = dvacc[TQ:2 * TQ, :]
        dkacc[TQ:2 * TQ, :] = jnp.zeros((TQ, AW), F32)
        dvacc[TQ:2 * TQ, :] = jnp.zeros((TQ, AW), F32)

    blk = lambda col, im: pl.BlockSpec((TQ, AW), lambda i, _c=col, _f=im: (_f(i), _c))
    return pl.pallas_call(
        body, name="attn_bwd", grid=(nb + 1,),
        in_specs=[blk(0, cur), blk(1, prev), blk(1, cur), blk(2, prev), blk(2, cur), blk(0, cur),
                  _fix((NH, BAND, CH))],
        out_specs=[blk(0, cur), blk(0, late), blk(0, late), _fix((NH, BAND, CH))],
        out_shape=[jax.ShapeDtypeStruct((s, AW), BF)] * 3 + [jax.ShapeDtypeStruct((NH, BAND, CH), F32)],
        scratch_shapes=[pltpu.VMEM((2 * TQ, AW), BF), pltpu.VMEM((2 * TQ, AW), BF),
                        pltpu.VMEM((2 * TQ, AW), F32), pltpu.VMEM((2 * TQ, AW), F32)],
        compiler_params=_cp("arbitrary"))(qkv, qkv, qkv, qkv, qkv, datt, bias_t)


QB = 256
NQC = QB // CH
WIN = BAND - CH + QB
NKB = WIN // QB


def _bias_window(bias):
    rows = [jnp.pad(bias, ((0, 0), (0, 0), (CH * a, WIN - BAND - CH * a)), constant_values=-jnp.inf)
            for a in range(NQC)]
    return jnp.concatenate(rows, axis=1)


def _win_specs(width, nb):
    def mk(col, back):
        return pl.BlockSpec((QB, width), lambda i, _c=col, _k=back: (jnp.maximum(jnp.minimum(i, nb - 1) - _k, 0), _c))
    return [mk(col, back) for col in (1, 2) for back in range(NKB - 1, -1, -1)]


def _attn2_fwd(qkv, bias_w):
    s = qkv.shape[0]
    nb = s // QB

    def body(q_ref, *refs):
        krefs, vrefs = refs[:NKB], refs[NKB:2 * NKB]
        b_ref, o_ref, kbuf, vbuf = refs[2 * NKB:]
        i = pl.program_id(0)
        for j in range(NKB):
            kbuf[QB * j:QB * j + QB, :] = krefs[j][...].astype(BF)
            vbuf[QB * j:QB * j + QB, :] = vrefs[j][...].astype(BF)
        valid = (i * QB - (NKB - 1) * QB + lax.broadcasted_iota(jnp.int32, (1, WIN), 1)) >= 0

        def heads(masked):
            for h in range(NH):
                sl = slice(HD * h, HD * h + HD)
                q = (q_ref[:, sl] * SCALE).astype(BF)
                sc = lax.dot_general(q, kbuf[:, sl], _DIMS["NT"], preferred_element_type=F32) + b_ref[h]
                if masked:
                    sc = jnp.where(valid, sc, -jnp.inf)
                e = jnp.exp(sc - jnp.max(sc, axis=-1, keepdims=True))
                o = jnp.dot(e.astype(BF), vbuf[:, sl], preferred_element_type=F32)
                o_ref[:, sl] = o * (1.0 / jnp.sum(e, axis=-1, keepdims=True))

        pl.when(i < NKB - 1)(functools.partial(heads, True))
        pl.when(i >= NKB - 1)(functools.partial(heads, False))

    return pl.pallas_call(
        body, name="attn_fwd", grid=(nb,),
        in_specs=[_row(QB, AW)] + _win_specs(AW, nb) + [_fix((NH, QB, WIN), single=True)],
        out_specs=_row(QB, AW), out_shape=jax.ShapeDtypeStruct((s, AW), F32),
        scratch_shapes=[pltpu.VMEM((WIN, AW), BF), pltpu.VMEM((WIN, AW), BF)],
        compiler_params=_cp("parallel"))(*[qkv] * (1 + 2 * NKB), bias_w)


def _attn2_bwd(qkv, datt, bias_wt):
    s = qkv.shape[0]
    nb = s // QB
    cur = lambda i: jnp.minimum(i, nb - 1)
    late = lambda i: jnp.maximum(i - (NKB - 1), 0)

    def body(q_ref, *refs):
        krefs, vrefs = refs[:NKB], refs[NKB:2 * NKB]
        do_ref, b_ref, dq_ref, dk_ref, dv_ref, db_ref, kbuf, vbuf, dkacc, dvacc, dsbuf = refs[2 * NKB:]
        i = pl.program_id(0)

        @pl.when(i == 0)
        def _():
            dkacc[...] = jnp.zeros_like(dkacc)
            dvacc[...] = jnp.zeros_like(dvacc)
            db_ref[...] = jnp.zeros_like(db_ref)

        def heads(masked):
            for j in range(NKB):
                kbuf[QB * j:QB * j + QB, :] = krefs[j][...].astype(BF)
                vbuf[QB * j:QB * j + QB, :] = vrefs[j][...].astype(BF)
            valid = (i * QB - (NKB - 1) * QB + lax.broadcasted_iota(jnp.int32, (WIN, 1), 0)) >= 0
            for h in range(NH):
                sl = slice(HD * h, HD * h + HD)
                q = (q_ref[:, sl] * SCALE).astype(BF)
                k = kbuf[:, sl]
                do = do_ref[:, sl].astype(BF)
                st = lax.dot_general(k, q, (((1,), (1,)), ((), ())), preferred_element_type=F32) + b_ref[h]
                if masked:
                    st = jnp.where(valid, st, -jnp.inf)
                e = jnp.exp(st - jnp.max(st, axis=0, keepdims=True))
                pt = e * (1.0 / jnp.sum(e, axis=0, keepdims=True))
                dvacc[:, sl] += jnp.dot(pt.astype(BF), do, preferred_element_type=F32)
                dpt = lax.dot_general(vbuf[:, sl], do, (((1,), (1,)), ((), ())), preferred_element_type=F32)
                dst = pt * (dpt - jnp.sum(dpt * pt, axis=0, keepdims=True))
                dsbuf[...] = dst
                fold = dsbuf[0:BAND, 0:CH]
                for a in range(1, NQC):
                    fold = fold + dsbuf[CH * a:CH * a + BAND, CH * a:CH * a + CH]
                db_ref[h] += fold
                dsb = dst.astype(BF)
                dkacc[:, sl] += jnp.dot(dsb, q, preferred_element_type=F32)
                dq = lax.dot_general(dsb, k, (((0,), (0,)), ((), ())), preferred_element_type=F32) * SCALE
                dq_ref[:, sl] = dq.astype(BF)

        pl.when(i < nb)(functools.partial(heads, True))

        dk_ref[...] = dkacc[0:QB, :].astype(BF)
        dv_ref[...] = dvacc[0:QB, :].astype(BF)
        for acc in (dkacc, dvacc):
            for j in range(NKB - 1):
                acc[QB * j:QB * j + QB, :] = acc[QB * j + QB:QB * j + 2 * QB, :]
            acc[WIN - QB:WIN, :] = jnp.zeros((QB, AW), F32)

    qspec = pl.BlockSpec((QB, AW), lambda i: (cur(i), 0))
    lspec = pl.BlockSpec((QB, AW), lambda i: (late(i), 0))
    return pl.pallas_call(
        body, name="attn_bwd", grid=(nb + NKB - 1,),
        in_specs=[qspec] + _win_specs(AW, nb) + [qspec, _fix((NH, WIN, QB), single=True)],
        out_specs=[qspec, lspec, lspec, _fix((NH, BAND, CH))],
        out_shape=[jax.ShapeDtypeStruct((s, AW), BF)] * 3 + [jax.ShapeDtypeStruct((NH, BAND, CH), F32)],
        scratch_shapes=[pltpu.VMEM((WIN, AW), BF), pltpu.VMEM((WIN, AW), BF),
                        pltpu.VMEM((WIN, AW), F32), pltpu.VMEM((WIN, AW), F32), pltpu.VMEM((WIN, QB), F32)],
        compiler_params=_cp("arbitrary"))(*[qkv] * (1 + 2 * NKB), datt, bias_wt)


TB = 512
SCH = 512
NCB = TB // SCH
GW = 8 * HD


def _tri(lower):
    r = lax.broadcasted_iota(jnp.int32, (SCH, SCH), 0)
    c = lax.broadcasted_iota(jnp.int32, (SCH, SCH), 1)
    return r >= c if lower else r <= c


def _ssd_fwd(proj, conv_w, conv_b, dt_bias, a_log, d_skip):
    s = proj.shape[0]
    nb = s // TB

    def body(xs_ref, bc_ref, dt_ref, cw_ref, cb_ref, dtb_ref, al_ref, dsk_ref, y_ref, pre_ref, prev_ref,
             xpad, ubuf, dtbuf, csb, cstb, hst, xdec, yoffb, drow):
        i = pl.program_id(0)

        @pl.when(i == 0)
        def _():
            xpad[0:8, :] = jnp.zeros((8, XBC), F32)
            hst[...] = jnp.zeros_like(hst)

        xpad[8:8 + TB, 0:SW] = xs_ref[...]
        xpad[8:8 + TB, SW:XBC] = bc_ref[...]
        pre = cb_ref[...]
        for kk in range(4):
            pre = pre + cw_ref[kk:kk + 1, :] * xpad[5 + kk:5 + kk + TB, :]
        pre_ref[...] = pre
        ubuf[...] = pre * jax.nn.sigmoid(pre)
        xpad[0:8, :] = xpad[TB:TB + 8, :]
        dtbuf[...] = _softplus(dt_ref[...] + dtb_ref[...])
        a = -jnp.exp(al_ref[...])
        tri = _tri(True).astype(F32)
        causal = _tri(True)

        def chunk(c, carry):
            r0 = pl.multiple_of(c * SCH, SCH)
            rows = pl.ds(r0, SCH)
            cs = jnp.dot(tri, dtbuf[rows, :] * a, precision=HI, preferred_element_type=F32)
            csb[...] = cs
            cstb[...] = cs.T
            for g in range(NG):
                bg = ubuf[rows, SW + NS * g:SW + NS * g + NS]
                cg = ubuf[rows, SW + NG * NS + NS * g:SW + NG * NS + NS * g + NS].astype(BF)
                cb = lax.dot_general(cg, bg.astype(BF), (((1,), (1,)), ((), ())), preferred_element_type=F32)
                hg = hst[g]
                prev_ref[c, g] = hg
                yoffb[...] = jnp.dot(cg, hg.astype(BF), preferred_element_type=F32)
                for r in range(8):
                    h = 8 * g + r
                    sl = slice(HD * h, HD * h + HD)
                    rs = slice(HD * r, HD * r + HD)
                    cs_h = csb[:, h:h + 1]
                    cl = csb[SCH - 1:SCH, h:h + 1]
                    seg = jnp.exp(jnp.where(causal, cs_h - cstb[h:h + 1, :], -jnp.inf))
                    xh = ubuf[rows, sl]
                    xdt = xh * dtbuf[rows, h:h + 1]
                    yd = jnp.dot((cb * seg).astype(BF), xdt.astype(BF), preferred_element_type=F32)
                    y_ref[rows, sl] = yd + jnp.exp(cs_h) * yoffb[:, rs] + xh * dsk_ref[:, h:h + 1]
                    xdec[:, rs] = xdt * jnp.exp(cl - cs_h)
                    drow[:, rs] = jnp.broadcast_to(jnp.exp(cl), (1, HD))
                st = jnp.dot(bg.T.astype(BF), xdec[...].astype(BF), preferred_element_type=F32)
                hst[g] = hg * drow[...] + st
            return carry

        lax.fori_loop(0, NCB, chunk, 0)

    return pl.pallas_call(
        body, name="ssd_fwd", grid=(nb,),
        in_specs=[_row(TB, SW, 4), _row(TB, 512, 10), _row(TB, LANE, 44), _fix((4, XBC)), _fix((1, XBC)),
                  _fix((1, LANE)), _fix((1, LANE)), _fix((1, LANE))],
        out_specs=[_row(TB, SW), _row(TB, XBC), pl.BlockSpec((NCB, NG, NS, GW), lambda i: (i, 0, 0, 0))],
        out_shape=[jax.ShapeDtypeStruct((s, SW), F32), jax.ShapeDtypeStruct((s, XBC), F32),
                   jax.ShapeDtypeStruct((s // SCH, NG, NS, GW), F32)],
        scratch_shapes=[pltpu.VMEM((TB + 8, XBC), F32), pltpu.VMEM((TB, XBC), F32), pltpu.VMEM((TB, LANE), F32),
                        pltpu.VMEM((SCH, LANE), F32), pltpu.VMEM((LANE, SCH), F32), pltpu.VMEM((NG, NS, GW), F32),
                        pltpu.VMEM((SCH, GW), F32), pltpu.VMEM((SCH, GW), F32), pltpu.VMEM((1, GW), F32)],
        compiler_params=_cp("arbitrary"))(proj, proj, proj, conv_w, conv_b, dt_bias, a_log, d_skip)


def _ssd_bwd(dy, pre, proj, prev, conv_w, dt_bias, a_log, d_skip):
    s = dy.shape[0]
    nb = s // TB
    rev = lambda i: nb - 1 - i
    halo = lambda i: jnp.maximum((nb - 1 - i) * (TB // 8) - 1, 0)

    def body(dy_ref, pre_ref, dt_ref, prev_ref, xs_ref, bc_ref, xsh_ref, bch_ref, cw_ref, dtb_ref, al_ref, dsk_ref,
             dx_ref, ddt_ref, pw_ref, ph_ref,
             ubuf, dtbuf, dpad, xpad, csb, cstb, dhs, yoffb, ebuf, xdecb, wbuf, drow, dub):
        i = pl.program_id(0)

        @pl.when(i == 0)
        def _():
            dhs[...] = jnp.zeros_like(dhs)
            dpad[TB:TB + 8, :] = jnp.zeros((8, XBC), F32)
            pw_ref[...] = jnp.zeros_like(pw_ref)
            ph_ref[...] = jnp.zeros_like(ph_ref)

        pre = pre_ref[...]
        ubuf[...] = pre * jax.nn.sigmoid(pre)
        dtbuf[...] = _softplus(dt_ref[...] + dtb_ref[...])
        a = -jnp.exp(al_ref[...])
        tri = _tri(True).astype(F32)
        trit = _tri(False).astype(F32)
        causal = _tri(True)
        rowid = lax.broadcasted_iota(jnp.int32, (LANE, SCH), 0)
        lane = lax.broadcasted_iota(jnp.int32, (SCH, LANE), 1)
        lane1 = lax.broadcasted_iota(jnp.int32, (1, LANE), 1)
        lastrow = lax.broadcasted_iota(jnp.int32, (SCH, LANE), 0) == SCH - 1

        def chunk(cc, carry):
            c = NCB - 1 - cc
            r0 = pl.multiple_of(c * SCH, SCH)
            rows = pl.ds(r0, SCH)
            dtc = dtbuf[rows, :]
            cs = jnp.dot(tri, dtc * a, precision=HI, preferred_element_type=F32)
            csb[...] = cs
            cstb[...] = cs.T
            dcs = jnp.zeros((SCH, LANE), F32)
            dcst = jnp.zeros((LANE, SCH), F32)
            xr = jnp.zeros((SCH, LANE), F32)
            dlast = jnp.zeros((1, LANE), F32)
            dsk = jnp.zeros((1, LANE), F32)
            for g in range(NG):
                bg = ubuf[rows, SW + NS * g:SW + NS * g + NS]
                cg = ubuf[rows, SW + NG * NS + NS * g:SW + NG * NS + NS * g + NS]
                bb = bg.astype(BF)
                cbf = cg.astype(BF)
                cb = lax.dot_general(cbf, bb, (((1,), (1,)), ((), ())), preferred_element_type=F32)
                hp = prev_ref[c, g]
                hpb = hp.astype(BF)
                dhg = dhs[g]
                dhb = dhg.astype(BF)
                yoffb[...] = jnp.dot(cbf, hpb, preferred_element_type=F32)
                for r in range(8):
                    h = 8 * g + r
                    sl = slice(HD * h, HD * h + HD)
                    rs = slice(HD * r, HD * r + HD)
                    cs_h = csb[:, h:h + 1]
                    cl = csb[SCH - 1:SCH, h:h + 1]
                    dyh = dy_ref[rows, sl]
                    ecs = jnp.exp(cs_h)
                    ebuf[:, rs] = dyh * ecs
                    xdecb[:, rs] = ubuf[rows, sl] * dtbuf[rows, h:h + 1] * jnp.exp(cl - cs_h)
                    t1 = jnp.sum(dyh * ecs * yoffb[:, rs], axis=1, keepdims=True)
                    dcs = dcs + jnp.where(lane == h, t1, 0.0)
                    drow[:, rs] = jnp.broadcast_to(jnp.exp(cl), (1, HD))
                eb = ebuf[...].astype(BF)
                dc = lax.dot_general(eb, hpb, (((1,), (1,)), ((), ())), preferred_element_type=F32)
                dprev = jnp.dot(cg.T.astype(BF), eb, preferred_element_type=F32)
                wbuf[...] = jnp.dot(bb, dhb, preferred_element_type=F32)
                db = lax.dot_general(xdecb[...].astype(BF), dhb, (((1,), (1,)), ((), ())), preferred_element_type=F32)
                dcbs = jnp.zeros((SCH, SCH), F32)
                for r in range(8):
                    h = 8 * g + r
                    sl = slice(HD * h, HD * h + HD)
                    rs = slice(HD * r, HD * r + HD)
                    cs_h = csb[:, h:h + 1]
                    cst_h = cstb[h:h + 1, :]
                    cl = csb[SCH - 1:SCH, h:h + 1]
                    seg = jnp.exp(jnp.where(causal, cs_h - cst_h, -jnp.inf))
                    dyh = dy_ref[rows, sl]
                    xh = ubuf[rows, sl]
                    dt_h = dtbuf[rows, h:h + 1]
                    xdt = xh * dt_h
                    dyb = dyh.astype(BF)
                    dm = lax.dot_general(dyb, xdt.astype(BF), _DIMS["NT"], preferred_element_type=F32)
                    mm_ = cb * seg
                    dxdt = lax.dot_general(mm_.astype(BF), dyb, _DIMS["TN"], preferred_element_type=F32)
                    gg = dm * mm_
                    dcs_h = jnp.sum(gg, axis=1, keepdims=True)
                    dcst = dcst + jnp.where(rowid == h, jnp.sum(gg, axis=0, keepdims=True), 0.0)
                    dcbs = dcbs + dm * seg
                    dec = jnp.exp(cl - cs_h)
                    w_h = wbuf[:, rs]
                    dxdt = dxdt + dec * w_h
                    t = jnp.sum(w_h * xdt, axis=1, keepdims=True) * dec
                    dcs_h = dcs_h - t
                    dch = jnp.sum(jnp.sum(dhs[g, :, rs] * prev_ref[c, g, :, rs], axis=1, keepdims=True),
                                  axis=0, keepdims=True)
                    dl = jnp.sum(t, axis=0, keepdims=True) + dch * jnp.exp(cl)
                    dcs = dcs + jnp.where(lane == h, dcs_h, 0.0)
                    dlast = dlast + jnp.where(lane1 == h, dl, 0.0)
                    xr = xr + jnp.where(lane == h, jnp.sum(dxdt * xh, axis=1, keepdims=True), 0.0)
                    dskh = jnp.sum(jnp.sum(dyh * xh, axis=1, keepdims=True), axis=0, keepdims=True)
                    dsk = dsk + jnp.where(lane1 == h, dskh, 0.0)
                    dub[:, sl] = dyh * dsk_ref[:, h:h + 1] + dxdt * dt_h
                dc = dc + jnp.dot(dcbs.astype(BF), bb, preferred_element_type=F32)
                db = db + lax.dot_general(dcbs.astype(BF), cbf, _DIMS["TN"], preferred_element_type=F32)
                dub[:, SW + NS * g:SW + NS * g + NS] = db
                dub[:, SW + NG * NS + NS * g:SW + NG * NS + NS * g + NS] = dc
                dhs[g] = dprev + drow[...] * dhg
            dcs = dcs - dcst.T + jnp.where(lastrow, dlast, 0.0)
            dadt = jnp.dot(trit, dcs, precision=HI, preferred_element_type=F32)
            ddt = dadt * a + xr
            ddtr = ddt * jax.nn.sigmoid(dt_ref[rows, :] + dtb_ref[...])
            ddt_ref[rows, :] = ddtr.astype(BF)
            ph_ref[0:1, :] += jnp.sum(ddtr, axis=0, keepdims=True)
            ph_ref[1:2, :] += jnp.sum(dadt * dtc, axis=0, keepdims=True) * a
            ph_ref[2:3, :] += dsk
            pc = pre_ref[rows, :]
            sg = jax.nn.sigmoid(pc)
            dpad[rows, :] = dub[...] * _silu_grad(pc, sg)
            return carry

        lax.fori_loop(0, NCB, chunk, 0)

        dxb = cw_ref[0:1, :] * dpad[3:3 + TB, :]
        for kk in range(1, 4):
            dxb = dxb + cw_ref[kk:kk + 1, :] * dpad[3 - kk:3 - kk + TB, :]
        dx_ref[...] = dxb.astype(BF)
        keep = jnp.where(i < nb - 1, 1.0, 0.0)
        xpad[0:8, 0:SW] = xsh_ref[...] * keep
        xpad[0:8, SW:XBC] = bch_ref[...] * keep
        xpad[8:8 + TB, 0:SW] = xs_ref[...]
        xpad[8:8 + TB, SW:XBC] = bc_ref[...]
        dp = dpad[0:TB, :]
        for kk in range(4):
            pw_ref[kk:kk + 1, :] += jnp.sum(dp * xpad[5 + kk:5 + kk + TB, :], axis=0, keepdims=True)
        pw_ref[4:5, :] += jnp.sum(dp, axis=0, keepdims=True)
        dpad[TB:TB + 8, :] = dpad[0:8, :]

    rblk = lambda w, col: pl.BlockSpec((TB, w), lambda i, _c=col: (rev(i), _c))
    return pl.pallas_call(
        body, name="ssd_bwd", grid=(nb,),
        in_specs=[rblk(SW, 0), rblk(XBC, 0), rblk(LANE, 44),
                  pl.BlockSpec((NCB, NG, NS, GW), lambda i: (rev(i), 0, 0, 0)),
                  rblk(SW, 4), rblk(512, 10),
                  pl.BlockSpec((8, SW), lambda i: (halo(i), 4)), pl.BlockSpec((8, 512), lambda i: (halo(i), 10)),
                  _fix((4, XBC)), _fix((1, LANE)), _fix((1, LANE)), _fix((1, LANE))],
        out_specs=[rblk(XBC, 0), rblk(LANE, 0), _fix((8, XBC)), _fix((8, LANE))],
        out_shape=[jax.ShapeDtypeStruct((s, XBC), BF), jax.ShapeDtypeStruct((s, LANE), BF),
                   jax.ShapeDtypeStruct((8, XBC), F32), jax.ShapeDtypeStruct((8, LANE), F32)],
        scratch_shapes=[pltpu.VMEM((TB, XBC), F32), pltpu.VMEM((TB, LANE), F32), pltpu.VMEM((TB + 8, XBC), F32),
                        pltpu.VMEM((TB + 8, XBC), F32), pltpu.VMEM((SCH, LANE), F32), pltpu.VMEM((LANE, SCH), F32),
                        pltpu.VMEM((NG, NS, GW), F32), pltpu.VMEM((SCH, GW), F32), pltpu.VMEM((SCH, GW), F32),
                        pltpu.VMEM((SCH, GW), F32), pltpu.VMEM((SCH, GW), F32), pltpu.VMEM((1, GW), F32),
                        pltpu.VMEM((SCH, XBC), F32)],
        compiler_params=_cp("arbitrary"))(dy, pre, proj, prev, proj, proj, proj, proj, conv_w, dt_bias, a_log, d_skip)


def _rel_index():
    q = jnp.arange(CH)[:, None] + (BAND - CH)
    k = jnp.arange(BAND)[None, :]
    return jnp.clip(q - k, -(CH - 1), REL_CLIP) + (CH - 1)


def _local_step(x, tgt, mods, first_weights, late_weights, grads_ready,
                g_mix, rel_bias, conv_w, conv_b, dt_bias, a_log, d_skip, g_att, g_ssd, g_ffn, g_final):
    sh1, sc1, gt1, sh2, sc2, gt2 = mods
    pad16 = lambda v: jnp.pad(v, ((0, 0), (0, LANE - NH)))
    dtb_p, al_p, dsk_p = pad16(dt_bias), pad16(a_log), pad16(d_skip)
    ridx = _rel_index()
    ext = jnp.concatenate([jnp.broadcast_to(rel_bias[:, NREL - 1:], (NH, BAND + CH - 1 - NREL)),
                           rel_bias[:, ::-1]], axis=1)
    skew = jnp.tile(jnp.pad(ext, ((0, 0), (0, 1))), (1, CH + 1))[:, :CH * (BAND + CH + 1)]
    bias = skew.reshape(NH, CH, BAND + CH + 1)[:, ::-1, :BAND]
    bias_w = _bias_window(bias)
    bias_wt = jnp.swapaxes(bias_w, 1, 2)

    h1 = _norm_fwd(x, g_mix, sc1, sh1, name="norm1_fwd")
    w_in_t = first_weights(h1)
    proj = _mm([(h1, w_in_t)], "NT", F32, "in_proj", 1024, 1152, 2048)
    att = _attn2_fwd(proj, bias_w)
    y, pre, prev = _ssd_fwd(proj, conv_w, conv_b, dtb_p, al_p, dsk_p)
    cat = _mixout_fwd(att, y, proj, g_att, g_ssd)
    w_out, w_gate_t, w_up_t, w_down = late_weights(cat)
    mix = _mm([(cat, w_out)], "NN", F32, "out_proj", 1024, 1024, 2048)
    h2, x1 = _norm_fwd(x, g_ffn, sc2, sh2, mix, gt1, name="norm2_fwd")
    gate, up, act = _gate_up(h2, w_gate_t, w_up_t)
    ffn = _mm([(act, w_down)], "NN", F32, "down_proj", 512, 1024, FF)
    dx2, dffn, pf = _final(x1, ffn, gt2, g_final, tgt)

    gw_down = _mm([(act, dffn)], "TN", BF, "gw_down", 1408, 2048, 1024)
    dgate, dup = _gate_up_bwd(dffn, w_down, gate, up)
    dh2 = _mm([(dgate, w_gate_t), (dup, w_up_t)], "NN", F32, "d_h2", 512, 512, FF)
    gw_gate_t = _mm([(dgate, h2)], "TN", BF, "gw_gate", 1408, 2048, 1024)
    gw_up_t = _mm([(dup, h2)], "TN", BF, "gw_up", 1408, 2048, 1024)
    tok = grads_ready("ffn", gw_gate_t, gw_up_t, gw_down)
    dx1, dmix, p2 = _norm_bwd(x1, g_ffn + tok[0, 0], sc2, sh2, dh2, dx2, mix, gt1, name="norm2_bwd")
    gw_out = _mm([(cat, dmix)], "TN", BF, "gw_out", 1024, 2048, 1024)
    tok = grads_ready("out", gw_out)
    dcat = _mm([(dmix, w_out)], "NT", F32, "d_cat", 1024, 1024, 2048, after=tok)
    datt, dy, dz, pm = _mixout_bwd(dcat, att, y, proj, g_att, g_ssd)
    dq, dk, dv, dbias_t = _attn2_bwd(proj, datt, bias_wt)
    dxbc, ddt, pw, ph = _ssd_bwd(dy, pre, proj, prev, conv_w, dtb_p, al_p, dsk_p)
    dproj = jnp.concatenate([dq, dk, dv, dz, dxbc, ddt], axis=1)
    gw_in_t = _mm([(dproj, h1)], "TN", BF, "gw_in", 1152, 2048, 1024)
    tok = grads_ready("in", gw_in_t)
    dh1 = _mm([(dproj, w_in_t)], "NN", F32, "d_h1", 512, 1024, INP, after=tok)
    gx, p1 = _norm_bwd(x, g_mix, sc1, sh1, dh1, dx1, name="norm1_bwd")

    onehot = (ridx.T.reshape(BAND * CH, 1) == jnp.arange(NREL)[None, :]).astype(F32)
    g_rel = _mm([(dbias_t.reshape(NH, BAND * CH), onehot)], "NN", F32, "g_rel_bias", NH, NREL, 4608, precision=HI)

    small = dict(
        dmods=jnp.concatenate([p1[1], p1[0], p2[3], p2[1], p2[0], pf[1]]),
        g_mix=p1[2], conv_b=pw[4], dt_bias=ph[0], a_log=ph[1], d_skip=ph[2], g_att=pm[0], g_ssd=pm[1],
        g_ffn=p2[2], g_final=pf[0], rel_bias=g_rel.reshape(-1), conv_w=pw[0:4].reshape(-1), loss=pf[2])
    return gx, small


def _exchange(srcs, gather, name):
    n = len(srcs)
    outs = [jax.ShapeDtypeStruct((NDEV,) + (s.shape if g else s.shape[1:]), s.dtype) for s, g in zip(srcs, gather)]
    outs.append(jax.ShapeDtypeStruct((8, LANE), F32))

    def body(*refs):
        src, dst, token = refs[:n], refs[n:2 * n], refs[2 * n]
        ssem, rsem, lsem = refs[2 * n + 1:]
        token[...] = jnp.zeros_like(token)
        x, y, c = lax.axis_index("x"), lax.axis_index("y"), lax.axis_index("c")
        me = 4 * x + 2 * y + c
        local = []
        for a in range(n):
            own = src[a] if gather[a] else src[a].at[me]
            cp = pltpu.make_async_copy(own, dst[a].at[me], lsem.at[a])
            cp.start()
            local.append(cp)
        remote = []
        for k in range(1, NDEV):
            px = 1 - x if k & 4 else x
            py = 1 - y if k & 2 else y
            pc = 1 - c if k & 1 else c
            pid = 4 * px + 2 * py + pc
            for a in range(n):
                s_ref = src[a] if gather[a] else src[a].at[pid]
                cp = pltpu.make_async_remote_copy(
                    src_ref=s_ref, dst_ref=dst[a].at[me], send_sem=ssem.at[a * (NDEV - 1) + k - 1],
                    recv_sem=rsem.at[a * (NDEV - 1) + k - 1], device_id=(px, py, pc), device_id_type=MESH)
                cp.start()
                remote.append(cp)
        for cp in remote:
            cp.wait()
        for cp in local:
            cp.wait()

    anyspec = pl.BlockSpec(memory_space=pl.ANY)
    return pl.pallas_call(
        body, name=name, out_shape=outs, in_specs=[anyspec] * n,
        out_specs=[anyspec] * n + [pl.BlockSpec(memory_space=pltpu.VMEM)],
        scratch_shapes=[pltpu.SemaphoreType.DMA((n * (NDEV - 1),)), pltpu.SemaphoreType.DMA((n * (NDEV - 1),)),
                        pltpu.SemaphoreType.DMA((n,))],
    )(*srcs)


_HBM = pl.BlockSpec(memory_space=pltpu.HBM)
_SEM = pl.BlockSpec(memory_space=pltpu.SEMAPHORE)
_EFFECT = pltpu.SideEffectType.DATAFLOW_SIDE_EFFECTING


def _peer_copies(src, land, ssem, rsem, gather):
    x, y, c = lax.axis_index("x"), lax.axis_index("y"), lax.axis_index("c")
    me = 4 * x + 2 * y + c
    copies = []
    for a in range(len(src)):
        for k in range(1, NDEV):
            px = 1 - x if k & 4 else x
            py = 1 - y if k & 2 else y
            pc = 1 - c if k & 1 else c
            s_ref = src[a] if gather[a] else src[a].at[4 * px + 2 * py + pc]
            idx = a * (NDEV - 1) + k - 1
            copies.append(pltpu.make_async_remote_copy(
                src_ref=s_ref, dst_ref=land[a].at[me], send_sem=ssem.at[idx], recv_sem=rsem.at[idx],
                device_id=(px, py, pc), device_id_type=MESH))
    return copies


def _xstart(srcs, gather, name):
    n = len(srcs)
    me = 4 * lax.axis_index("x") + 2 * lax.axis_index("y") + lax.axis_index("c")
    lands = []
    for s_, g_ in zip(srcs, gather):
        own = s_ if g_ else lax.dynamic_index_in_dim(s_, me, 0, keepdims=False)
        empty = lax.empty((NDEV,) + own.shape, own.dtype)
        lands.append(lax.dynamic_update_slice(empty, own[None], (me,) + (0,) * own.ndim))
    nsem = n * (NDEV - 1)

    def body(*refs):
        src, land, ssem, rsem, token = refs[:n], refs[n:2 * n], refs[2 * n], refs[2 * n + 1], refs[-1]
        for cp in _peer_copies(src, land, ssem, rsem, gather):
            cp.start()
        token[...] = jnp.zeros_like(token)

    ops = [pltpu.with_memory_space_constraint(a_, pltpu.HBM) for a_ in list(srcs) + lands]
    return pl.pallas_call(
        body, name=name,
        out_shape=(pltpu.SemaphoreType.DMA((nsem,)), pltpu.SemaphoreType.DMA((nsem,)),
                   *[pltpu.HBM(a_.shape, a_.dtype) for a_ in ops], jax.ShapeDtypeStruct((8, LANE), F32)),
        in_specs=[_HBM] * (2 * n),
        out_specs=(_SEM, _SEM, *[_HBM] * (2 * n), pl.BlockSpec(memory_space=pltpu.VMEM)),
        input_output_aliases={i: 2 + i for i in range(2 * n)},
        compiler_params=pltpu.CompilerParams(has_side_effects=_EFFECT),
    )(*ops)


def _xwait(started, gather, after, name):
    n = len(gather)
    ssem, rsem = started[0], started[1]
    bufs = started[2:2 + 2 * n]

    def body(*refs):
        src, land, ssem_, rsem_ = refs[:n], refs[n:2 * n], refs[2 * n], refs[2 * n + 1]
        for cp in _peer_copies(src, land, ssem_, rsem_, gather):
            cp.wait_send()
            cp.wait_recv()

    out = pl.pallas_call(
        body, name=name, out_shape=tuple(pltpu.HBM(a_.shape, a_.dtype) for a_ in bufs),
        in_specs=[_HBM] * (2 * n) + [_SEM, _SEM, pl.BlockSpec(memory_space=pl.ANY)],
        out_specs=[_HBM] * (2 * n), input_output_aliases={i: i for i in range(2 * n)},
        compiler_params=pltpu.CompilerParams(has_side_effects=_EFFECT),
    )(*bufs, ssem, rsem, after)
    return out[n:]


def _mods(c16, w_shard, b_shard, *, tn=512):
    n = w_shard.shape[1]

    def body(c_ref, w_ref, b_ref, o_ref):
        cv = c_ref[...]
        cond = (cv * jax.nn.sigmoid(cv)).astype(BF)
        o_ref[...] = jnp.dot(cond, w_ref[...].astype(BF), preferred_element_type=F32) + b_ref[...]

    return pl.pallas_call(body, name="ada_mods", grid=(n // tn,),
                          in_specs=[_fix((16, D)), pl.BlockSpec((D, tn), lambda j: (0, j)),
                                    pl.BlockSpec((1, tn), lambda j: (0, j))],
                          out_specs=pl.BlockSpec((16, tn), lambda j: (0, j)),
                          out_shape=jax.ShapeDtypeStruct((16, n), F32), compiler_params=_cp("parallel"))(c16, w_shard, b_shard)


def _gw_ada(c_t, dm, *, tr=256):
    n = dm.shape[1]

    def body(ct_ref, dm_ref, o_ref):
        acc = jnp.zeros((tr, n), F32)
        for b in range(NDEV):
            cv = ct_ref[:, b:b + 1]
            cond = (cv * jax.nn.sigmoid(cv)).astype(BF).astype(F32)
            acc = acc + cond * dm_ref[b:b + 1, :].astype(BF).astype(F32)
        o_ref[...] = acc

    return pl.pallas_call(body, name="gw_ada", grid=(D // tr,),
                          in_specs=[_row(tr, NDEV), _fix((NDEV, n))], out_specs=_row(tr, n),
                          out_shape=jax.ShapeDtypeStruct((D, n), F32), compiler_params=_cp("parallel"))(c_t, dm)


def _sum8(parts, *, name, tc=512):
    _, r, c = parts.shape
    tc = min(tc, c)

    def body(p_ref, o_ref):
        acc = p_ref[0].astype(F32)
        for d in range(1, NDEV):
            acc = acc + p_ref[d].astype(F32)
        o_ref[...] = acc

    return pl.pallas_call(body, name=name, grid=(c // tc,),
                          in_specs=[pl.BlockSpec((NDEV, r, tc), lambda j: (0, 0, j))],
                          out_specs=pl.BlockSpec((r, tc), lambda j: (0, j)),
                          out_shape=jax.ShapeDtypeStruct((r, c), F32), compiler_params=_cp("parallel"))(parts)


def _adam(w, m, v, *, recv=None, grad=None, tr, name):
    lead = w.ndim == 3
    r, c = w.shape[-2:]
    summed = recv is not None
    assert r % tr == 0
    at = (0,) if lead else (Ellipsis,)

    def body(g_ref, w_ref, m_ref, v_ref, go_ref, do_ref, mo_ref, vo_ref):
        if summed:
            g = g_ref[0].astype(F32)
            for d in range(1, NDEV):
                g = g + g_ref[d].astype(F32)
        else:
            g = g_ref[...]
        go_ref[at] = g
        mn = B1 * m_ref[at] + (1.0 - B1) * g
        vn = B2 * v_ref[at] + (1.0 - B2) * (g * g)
        mo_ref[at] = mn
        vo_ref[at] = vn
        mh = mn / (1.0 - B1 ** STEP)
        vh = vn / (1.0 - B2 ** STEP)
        do_ref[at] = -LR * (mh / (jnp.sqrt(vh) + AEPS) + WD * w_ref[at])

    gspec = pl.BlockSpec((NDEV, tr, c), lambda i: (0, i, 0)) if summed else _row(tr, c)
    wspec = pl.BlockSpec((1, tr, c), lambda i: (0, i, 0)) if lead else _row(tr, c)
    return pl.pallas_call(body, name=name, grid=(r // tr,),
                          in_specs=[gspec, wspec, wspec, wspec], out_specs=[wspec] * 4,
                          out_shape=[jax.ShapeDtypeStruct(w.shape, F32)] * 4,
                          compiler_params=_cp("parallel"))(recv if summed else grad, w, m, v)


def _cols_to_blocks(g, width):
    r = g.shape[0]
    return jnp.transpose(g.reshape(r, NDEV, width), (1, 0, 2))


def _blocks_to_cols(b):
    _, r, width = b.shape
    return jnp.transpose(b, (1, 0, 2)).reshape(r, NDEV * width)


_SMALL = ("b_ada", "g_mix", "rel_bias", "conv_w", "conv_b", "dt_bias", "a_log", "d_skip",
          "g_att_out", "g_ssd_out", "g_ffn", "g_final")


def _pad_rows(v):
    v = v.reshape(-1)
    return jnp.pad(v, (0, (-v.shape[0]) % LANE))


def kernel(x, c, w_ada, b_ada, g_mix, w_in, rel_bias, conv_w, conv_b, dt_bias, a_log, d_skip, g_att_out, g_ssd_out, w_out, g_ffn, w_gate, w_up, w_down, g_final, loss_target, m_w_ada, m_b_ada, m_g_mix, m_w_in, m_rel_bias, m_conv_w, m_conv_b, m_dt_bias, m_a_log, m_d_skip, m_g_att_out, m_g_ssd_out, m_w_out, m_g_ffn, m_w_gate, m_w_up, m_w_down, m_g_final, v_w_ada, v_b_ada, v_g_mix, v_w_in, v_rel_bias, v_conv_w, v_conv_b, v_dt_bias, v_a_log, v_d_skip, v_g_att_out, v_g_ssd_out, v_w_out, v_g_ffn, v_w_gate, v_w_up, v_w_down, v_g_final):
    me = 4 * lax.axis_index("x") + 2 * lax.axis_index("y") + lax.axis_index("c")
    wts = dict(w_ada=w_ada, b_ada=b_ada, g_mix=g_mix, w_in=w_in, rel_bias=rel_bias, conv_w=conv_w, conv_b=conv_b,
               dt_bias=dt_bias, a_log=a_log, d_skip=d_skip, g_att_out=g_att_out, g_ssd_out=g_ssd_out, w_out=w_out,
               g_ffn=g_ffn, w_gate=w_gate, w_up=w_up, w_down=w_down, g_final=g_final)
    mom = dict(w_ada=m_w_ada, b_ada=m_b_ada, g_mix=m_g_mix, w_in=m_w_in, rel_bias=m_rel_bias, conv_w=m_conv_w,
               conv_b=m_conv_b, dt_bias=m_dt_bias, a_log=m_a_log, d_skip=m_d_skip, g_att_out=m_g_att_out,
               g_ssd_out=m_g_ssd_out, w_out=m_w_out, g_ffn=m_g_ffn, w_gate=m_w_gate, w_up=m_w_up, w_down=m_w_down,
               g_final=m_g_final)
    var = dict(w_ada=v_w_ada, b_ada=v_b_ada, g_mix=v_g_mix, w_in=v_w_in, rel_bias=v_rel_bias, conv_w=v_conv_w,
               conv_b=v_conv_b, dt_bias=v_dt_bias, a_log=v_a_log, d_skip=v_d_skip, g_att_out=v_g_att_out,
               g_ssd_out=v_g_ssd_out, w_out=v_w_out, g_ffn=v_g_ffn, w_gate=v_w_gate, w_up=v_w_up, w_down=v_w_down,
               g_final=v_g_final)
    order = ("w_ada", "b_ada", "g_mix", "w_in", "rel_bias", "conv_w", "conv_b", "dt_bias", "a_log", "d_skip",
             "g_att_out", "g_ssd_out", "w_out", "g_ffn", "w_gate", "w_up", "w_down", "g_final")

    c_all, rel_all, cw_all, _ = _exchange([c, rel_bias[0], conv_w[0]], [True] * 3, "gather_small")
    c_all = c_all.reshape(NDEV, D)
    rel_full = _blocks_to_cols(rel_all)
    cw_full = _blocks_to_cols(cw_all)

    ncol = w_ada.shape[2]
    b_sh = lax.dynamic_slice(b_ada, (0, me * ncol), (1, ncol))
    mods_part = _mods(jnp.pad(c_all, ((0, 8), (0, 0))), w_ada[0], b_sh)[:NDEV]
    mods_all, tok_m = _exchange([mods_part], [True], "gather_mods")
    mods_me = lax.dynamic_index_in_dim(mods_all, me, axis=1, keepdims=False).reshape(6, 1, D)
    mods = [mods_me[i] for i in range(6)]

    st_in = _xstart([(w_in[0] + tok_m[0, 0]).T.astype(BF)], [True], "gather_w_in_start")
    st_rest = _xstart([(w_out[0] + st_in[-1][0, 0]).astype(BF), w_gate[0].T.astype(BF), w_up[0].T.astype(BF),
                       w_down[0].astype(BF)], [True] * 4, "gather_w_rest_start")

    def first_weights(after):
        (gi,) = _xwait(st_in, [True], after, "gather_w_in_wait")
        return jnp.pad(gi.reshape(INC, D), ((0, INP - INC), (0, 0)))

    def late_weights(after):
        go, gg, gu_, gd = _xwait(st_rest, [True] * 4, after, "gather_w_rest_wait")
        return go.reshape(D, D), gg.reshape(FF, D), gu_.reshape(FF, D), gd.reshape(FF, D)

    started = {}

    def grads_ready(which, *g):
        if which == "in":
            g = (g[0][:INC],)
        srcs = [g_.reshape(NDEV, g_.shape[0] // NDEV, D) for g_ in g]
        started[which] = _xstart(srcs, [False] * len(srcs), f"exchange_{which}_grads_start")
        return started[which][-1]

    gx, small = _local_step(
        x[0], loss_target[0], mods, first_weights, late_weights, grads_ready,
        g_mix + st_rest[-1][0, 0], rel_full, cw_full, conv_b, dt_bias, a_log, d_skip, g_att_out, g_ssd_out, g_ffn,
        g_final.reshape(1, D))

    loss_row = jnp.pad(jnp.sum(small["loss"]).reshape(1), (0, LANE - 1))
    pack = jnp.concatenate([small[k] for k in ("dmods", "g_mix", "conv_b", "dt_bias", "a_log", "d_skip", "g_att",
                                               "g_ssd", "g_ffn", "g_final", "rel_bias", "conv_w")] + [loss_row])
    packs, _ = _exchange([pack.reshape(-1, LANE)], [True], "gather_small_grads")
    rg, ru, rd = _xwait(started["ffn"], [False] * 3, packs, "exchange_ffn_grads_wait")
    (ro,) = _xwait(started["out"], [False], rg, "exchange_out_grads_wait")
    (ri,) = _xwait(started["in"], [False], ro, "exchange_in_grads_wait")
    big = dict(
        w_gate=_adam(w_gate, m_w_gate, v_w_gate, grad=_sum8(rg, name="sum_w_gate").T, tr=256,
                     name="adam_w_gate"),
        w_up=_adam(w_up, m_w_up, v_w_up, grad=_sum8(ru, name="sum_w_up").T, tr=256, name="adam_w_up"),
        w_down=_adam(w_down, m_w_down, v_w_down, recv=rd, tr=176, name="adam_w_down"))
    tot = _sum8(packs, name="sum_small").reshape(-1)

    sizes = dict(dmods=6 * D, g_mix=D, conv_b=XBC, dt_bias=LANE, a_log=LANE, d_skip=LANE, g_att_out=AW, g_ssd_out=SW,
                 g_ffn=D, g_final=D, rel_bias=NH * NREL, conv_w=4 * XBC, loss=LANE)
    off, o = {}, 0
    for k_, n_ in sizes.items():
        off[k_] = o
        o += n_
    take = lambda k_, n_=None: tot[off[k_]:off[k_] + (n_ or sizes[k_])]
    loss = tot[off["loss"]]
    rel_w = NREL // NDEV
    cw_w = XBC // NDEV
    gsm = dict(
        b_ada=take("dmods"), g_mix=take("g_mix"),
        rel_bias=lax.dynamic_slice(take("rel_bias").reshape(NH, NREL), (0, me * rel_w), (NH, rel_w)),
        conv_w=lax.dynamic_slice(take("conv_w").reshape(4, XBC), (0, me * cw_w), (4, cw_w)),
        conv_b=take("conv_b"), dt_bias=take("dt_bias", NH), a_log=take("a_log", NH), d_skip=take("d_skip", NH),
        g_att_out=take("g_att_out"), g_ssd_out=take("g_ssd_out"), g_ffn=take("g_ffn"), g_final=take("g_final"))

    flat = lambda d_: jnp.concatenate([_pad_rows(d_[k_]) for k_ in _SMALL])
    gflat = flat(gsm)
    nrow = -(-gflat.shape[0] // (8 * LANE)) * 8
    to2d = lambda v_: jnp.pad(v_, (0, nrow * LANE - v_.shape[0])).reshape(nrow, LANE)
    res = _adam(to2d(flat(wts)), to2d(flat(mom)), to2d(flat(var)), grad=to2d(gflat), tr=nrow, name="adam_small")
    outs = {k_: {} for k_ in ("grad", "delta", "m", "v")}
    o = 0
    for k_ in _SMALL:
        n_ = wts[k_].size
        for kind, arr in zip(("grad", "delta", "m", "v"), res):
            outs[kind][k_] = arr.reshape(-1)[o:o + n_].reshape(wts[k_].shape)
        o += n_ + (-n_) % LANE

    dm_all = packs[:, :6 * D // LANE, :].reshape(NDEV, 6 * D)
    dm_sh = lax.dynamic_slice(dm_all, (0, me * ncol), (NDEV, ncol))
    gwa = _gw_ada(c_all.T, dm_sh)
    big.update(
        w_ada=_adam(w_ada, m_w_ada, v_w_ada, grad=gwa, tr=256, name="adam_w_ada"),
        w_in=_adam(w_in, m_w_in, v_w_in, grad=_sum8(ri, name="sum_w_in").T, tr=256, name="adam_w_in"),
        w_out=_adam(w_out, m_w_out, v_w_out, recv=ro, tr=128, name="adam_w_out"))
    for k_, r_ in big.items():
        for kind, arr in zip(("grad", "delta", "m", "v"), r_):
            outs[kind][k_] = arr.reshape(wts[k_].shape)

    return (loss, gx.reshape(x.shape), *[outs["grad"][k_] for k_ in order], *[outs["delta"][k_] for k_ in order],
            *[outs["m"][k_] for k_ in order], *[outs["v"][k_] for k_ in order])
```

```python
import functools

import jax
import jax.numpy as jnp
from jax import lax
from jax.experimental import pallas as pl
from jax.experimental.pallas import tpu as pltpu

F32 = jnp.float32
BF = jnp.bfloat16
HI = lax.Precision.HIGHEST

D = 2048
NH = 16
HD = 64
AW = 1024
SW = 1024
NG = 2
NS = 128
XBC = 1536
CH = 64
BAND = 576
NREL = 320
REL_CLIP = 256
FF = 5632
INC = 5648
INP = 5760
EPS = 1e-6
NDEV = 8
LANE = 128
VMEM_LIMIT = 56 * 1024 * 1024
LR, B1, B2, AEPS, WD, STEP = 0.001, 0.9, 0.999, 1e-08, 0.01, 10
MESH = pl.DeviceIdType.MESH


def _cp(*sem):
    return pltpu.CompilerParams(dimension_semantics=sem, vmem_limit_bytes=VMEM_LIMIT)


def _row(ts, w, col=0):
    return pl.BlockSpec((ts, w), lambda i, _c=col: (i, _c))


def _fix(shape, single=False):
    nd = len(shape)
    if single:
        return pl.BlockSpec(shape, lambda i, _n=nd: (0,) * _n, pipeline_mode=pl.Buffered(1))
    return pl.BlockSpec(shape, lambda i, _n=nd: (0,) * _n)


def _silu_grad(x, s):
    return s * (1.0 + x * (1.0 - s))


def _softplus(x):
    return jnp.maximum(x, 0.0) + jnp.log1p(jnp.exp(-jnp.abs(x)))


_DIMS = {"NN": (((1,), (0,)), ((), ())), "TN": (((0,), (0,)), ((), ())), "NT": (((1,), (1,)), ((), ()))}


def _mm(pairs, mode, out_dtype, name, tm, tn, tk, precision=None, after=None):
    a0, b0 = pairs[0]
    m, k = a0.shape[::-1] if mode == "TN" else a0.shape
    n = b0.shape[0] if mode == "NT" else b0.shape[1]
    tm, tn, tk = min(tm, m), min(tn, n), min(tk, k)
    nk1 = k // tk
    npair = len(pairs)
    nk = 1 if nk1 == 1 else nk1 * npair
    assert m % tm == 0 and n % tn == 0 and k % tk == 0, (name, a0.shape, b0.shape)
    dims = _DIMS[mode]

    def kloc(kk, p):
        return jnp.clip(kk - p * nk1, 0, nk1 - 1)

    def aspec(p):
        if mode == "TN":
            return pl.BlockSpec((tk, tm), lambda i, j, kk, _p=p: (kloc(kk, _p), i))
        return pl.BlockSpec((tm, tk), lambda i, j, kk, _p=p: (i, kloc(kk, _p)))

    def bspec(p):
        if mode == "NT":
            return pl.BlockSpec((tn, tk), lambda i, j, kk, _p=p: (j, kloc(kk, _p)))
        return pl.BlockSpec((tk, tn), lambda i, j, kk, _p=p: (kloc(kk, _p), j))

    nin = 2 * npair + (after is not None)

    def body(*refs):
        ab, o_ref = refs[:2 * npair], refs[nin]
        if nk == 1:
            tot = lax.dot_general(ab[0][...], ab[1][...], dims, preferred_element_type=F32, precision=precision)
            for p in range(1, npair):
                tot = tot + lax.dot_general(ab[2 * p][...], ab[2 * p + 1][...], dims, preferred_element_type=F32,
                                            precision=precision)
            o_ref[...] = tot.astype(o_ref.dtype)
            return
        acc = refs[nin + 1]
        kk = pl.program_id(2)

        @pl.when(kk == 0)
        def _():
            acc[...] = jnp.zeros_like(acc)

        for p in range(npair):
            @pl.when((kk >= p * nk1) & (kk < (p + 1) * nk1))
            def _(p=p):
                acc[...] += lax.dot_general(ab[2 * p][...], ab[2 * p + 1][...], dims, preferred_element_type=F32,
                                            precision=precision)

        @pl.when(kk == nk - 1)
        def _():
            o_ref[...] = acc[...].astype(o_ref.dtype)

    in_specs = []
    for p in range(npair):
        in_specs += [aspec(p), bspec(p)]
    operands = [t for ab_ in pairs for t in ab_]
    if after is not None:
        in_specs.append(pl.BlockSpec(memory_space=pl.ANY))
        operands.append(after)
    return pl.pallas_call(
        body, name=name, grid=(m // tm, n // tn, nk), in_specs=in_specs,
        out_specs=pl.BlockSpec((tm, tn), lambda i, j, kk: (i, j)),
        out_shape=jax.ShapeDtypeStruct((m, n), out_dtype),
        scratch_shapes=[] if nk == 1 else [pltpu.VMEM((tm, tn), F32)],
        compiler_params=_cp("parallel", "parallel", "arbitrary"),
    )(*operands)


def _norm_fwd(x, g, sc, sh, mix=None, gt=None, *, name, ts=256):
    s, d = x.shape
    has = mix is not None

    def body(*refs):
        if has:
            x_ref, g_ref, sc_ref, sh_ref, mix_ref, gt_ref, h_ref, x1_ref = refs
        else:
            x_ref, g_ref, sc_ref, sh_ref, h_ref = refs
        xv = x_ref[...]
        if has:
            xv = xv + gt_ref[...] * mix_ref[...]
            x1_ref[...] = xv
        r = lax.rsqrt(jnp.mean(xv * xv, axis=-1, keepdims=True) + EPS)
        h_ref[...] = ((xv * r) * g_ref[...] * (1.0 + sc_ref[...]) + sh_ref[...]).astype(BF)

    ins = [x, g, sc, sh] + ([mix, gt] if has else [])
    in_specs = [_row(ts, d), _fix((1, d)), _fix((1, d)), _fix((1, d))] + ([_row(ts, d), _fix((1, d))] if has else [])
    out_shape = [jax.ShapeDtypeStruct((s, d), BF)] + ([jax.ShapeDtypeStruct((s, d), F32)] if has else [])
    out_specs = [_row(ts, d)] + ([_row(ts, d)] if has else [])
    out = pl.pallas_call(body, name=name, grid=(s // ts,), in_specs=in_specs, out_specs=out_specs,
                         out_shape=out_shape, compiler_params=_cp("parallel"))(*ins)
    return out if has else out[0]


def _norm_bwd(x, g, sc, sh, dh, dres, mix=None, gt=None, *, name, ts=256):
    s, d = x.shape
    has = mix is not None

    def body(*refs):
        if has:
            x_ref, g_ref, sc_ref, sh_ref, dh_ref, dres_ref, mix_ref, gt_ref, dx_ref, dmix_ref, part_ref = refs
        else:
            x_ref, g_ref, sc_ref, sh_ref, dh_ref, dres_ref, dx_ref, part_ref = refs
        i = pl.program_id(0)
        xv = x_ref[...]
        gv = g_ref[...]
        dhv = dh_ref[...]
        r = lax.rsqrt(jnp.mean(xv * xv, axis=-1, keepdims=True) + EPS)
        xh = xv * r
        dyn = dhv * (1.0 + sc_ref[...])
        gy = dyn * gv
        dx = dres_ref[...] + r * (gy - xh * jnp.mean(gy * xh, axis=-1, keepdims=True))
        dx_ref[...] = dx

        @pl.when(i == 0)
        def _():
            part_ref[...] = jnp.zeros_like(part_ref)

        part_ref[0:1, :] += jnp.sum(dhv * (xh * gv), axis=0, keepdims=True)
        part_ref[1:2, :] += jnp.sum(dhv, axis=0, keepdims=True)
        part_ref[2:3, :] += jnp.sum(dyn * xh, axis=0, keepdims=True)
        if has:
            part_ref[3:4, :] += jnp.sum(dx * mix_ref[...], axis=0, keepdims=True)
            dmix_ref[...] = (gt_ref[...] * dx).astype(BF)

    ins = [x, g, sc, sh, dh, dres] + ([mix, gt] if has else [])
    in_specs = [_row(ts, d), _fix((1, d)), _fix((1, d)), _fix((1, d)), _row(ts, d), _row(ts, d)]
    in_specs += [_row(ts, d), _fix((1, d))] if has else []
    out_shape = [jax.ShapeDtypeStruct((s, d), F32)] + ([jax.ShapeDtypeStruct((s, d), BF)] if has else [])
    out_shape += [jax.ShapeDtypeStruct((8, d), F32)]
    out_specs = [_row(ts, d)] + ([_row(ts, d)] if has else []) + [_fix((8, d))]
    return pl.pallas_call(body, name=name, grid=(s // ts,), in_specs=in_specs, out_specs=out_specs,
                          out_shape=out_shape, compiler_params=_cp("arbitrary"))(*ins)


def _final(x1, ffn, gt2, gf, tgt, *, ts=256):
    s, d = x1.shape

    def body(x1_ref, ffn_ref, gt_ref, g_ref, t_ref, dx_ref, dffn_ref, part_ref):
        i = pl.program_id(0)
        fv = ffn_ref[...]
        gv = g_ref[...]
        xv = x1_ref[...] + gt_ref[...] * fv
        r = lax.rsqrt(jnp.mean(xv * xv, axis=-1, keepdims=True) + EPS)
        xh = xv * r
        e = xh * gv - t_ref[...]
        dy = e * (1.0 / d)
        gy = dy * gv
        dx = r * (gy - xh * jnp.mean(gy * xh, axis=-1, keepdims=True))
        dx_ref[...] = dx
        dffn_ref[...] = (gt_ref[...] * dx).astype(BF)

        @pl.when(i == 0)
        def _():
            part_ref[...] = jnp.zeros_like(part_ref)

        part_ref[0:1, :] += jnp.sum(dy * xh, axis=0, keepdims=True)
        part_ref[1:2, :] += jnp.sum(dx * fv, axis=0, keepdims=True)
        part_ref[2:3, :] += jnp.sum(e * e, axis=0, keepdims=True) * (0.5 / d)

    return pl.pallas_call(
        body, name="final_loss", grid=(s // ts,),
        in_specs=[_row(ts, d), _row(ts, d), _fix((1, d)), _fix((1, d)), _row(ts, d)],
        out_specs=[_row(ts, d), _row(ts, d), _fix((8, d))],
        out_shape=[jax.ShapeDtypeStruct((s, d), F32), jax.ShapeDtypeStruct((s, d), BF),
                   jax.ShapeDtypeStruct((8, d), F32)],
        compiler_params=_cp("arbitrary"))(x1, ffn, gt2, gf, tgt)


def _gate_up(h, wg_t, wu_t, *, tm=1024, tn=512):
    s, k = h.shape
    n = wg_t.shape[0]
    tm = min(tm, s)

    def body(h_ref, wg_ref, wu_ref, g_ref, u_ref, a_ref):
        hv = h_ref[...]
        gv = lax.dot_general(hv, wg_ref[...], _DIMS["NT"], preferred_element_type=F32)
        uv = lax.dot_general(hv, wu_ref[...], _DIMS["NT"], preferred_element_type=F32)
        g_ref[...] = gv.astype(BF)
        u_ref[...] = uv.astype(BF)
        a_ref[...] = (gv * jax.nn.sigmoid(gv) * uv).astype(BF)

    wspec = pl.BlockSpec((tn, k), lambda i, j: (j, 0))
    ospec = pl.BlockSpec((tm, tn), lambda i, j: (i, j))
    return pl.pallas_call(body, name="gate_up", grid=(s // tm, n // tn),
                          in_specs=[pl.BlockSpec((tm, k), lambda i, j: (i, 0)), wspec, wspec],
                          out_specs=[ospec] * 3, out_shape=[jax.ShapeDtypeStruct((s, n), BF)] * 3,
                          compiler_params=_cp("parallel", "parallel"))(h, wg_t, wu_t)


def _gate_up_bwd(dffn, w_down, gate, up, *, tm=1024, tn=512):
    s, k = dffn.shape
    n = w_down.shape[0]
    tm = min(tm, s)

    def body(d_ref, w_ref, g_ref, u_ref, dg_ref, du_ref):
        dav = lax.dot_general(d_ref[...], w_ref[...], _DIMS["NT"], preferred_element_type=F32)
        gv = g_ref[...].astype(F32)
        uv = u_ref[...].astype(F32)
        sg = jax.nn.sigmoid(gv)
        dg_ref[...] = (dav * uv * _silu_grad(gv, sg)).astype(BF)
        du_ref[...] = (dav * gv * sg).astype(BF)

    tile = pl.BlockSpec((tm, tn), lambda i, j: (i, j))
    return pl.pallas_call(body, name="gate_up_bwd", grid=(s // tm, n // tn),
                          in_specs=[pl.BlockSpec((tm, k), lambda i, j: (i, 0)),
                                    pl.BlockSpec((tn, k), lambda i, j: (j, 0)), tile, tile],
                          out_specs=[tile] * 2, out_shape=[jax.ShapeDtypeStruct((s, n), BF)] * 2,
                          compiler_params=_cp("parallel", "parallel"))(dffn, w_down, gate, up)


def _mixout_fwd(att, y, proj, g_att, g_ssd, *, ts=256):
    s = att.shape[0]

    def body(a_ref, y_ref, z_ref, ga_ref, gs_ref, o_ref):
        av = a_ref[...]
        ra = lax.rsqrt(jnp.mean(av * av, axis=-1, keepdims=True) + EPS)
        o_ref[:, 0:AW] = (av * ra * ga_ref[...]).astype(BF)
        zv = z_ref[...]
        yz = y_ref[...] * (zv * jax.nn.sigmoid(zv))
        rs = lax.rsqrt(jnp.mean(yz * yz, axis=-1, keepdims=True) + EPS)
        o_ref[:, AW:AW + SW] = (yz * rs * gs_ref[...]).astype(BF)

    return pl.pallas_call(body, name="mixout_fwd", grid=(s // ts,),
                          in_specs=[_row(ts, AW), _row(ts, SW), _row(ts, SW, 3), _fix((1, AW)), _fix((1, SW))],
                          out_specs=_row(ts, AW + SW), out_shape=jax.ShapeDtypeStruct((s, AW + SW), BF),
                          compiler_params=_cp("parallel"))(att, y, proj, g_att, g_ssd)


def _mixout_bwd(dcat, att, y, proj, g_att, g_ssd, *, ts=256):
    s = att.shape[0]

    def body(dc_ref, a_ref, y_ref, z_ref, ga_ref, gs_ref, da_ref, dy_ref, dz_ref, part_ref):
        i = pl.program_id(0)
        av = a_ref[...]
        dca = dc_ref[:, 0:AW]
        ra = lax.rsqrt(jnp.mean(av * av, axis=-1, keepdims=True) + EPS)
        ah = av * ra
        gy = dca * ga_ref[...]
        da_ref[...] = ra * (gy - ah * jnp.mean(gy * ah, axis=-1, keepdims=True))
        zv = z_ref[...]
        yv = y_ref[...]
        sg = jax.nn.sigmoid(zv)
        sz = zv * sg
        yz = yv * sz
        dcs = dc_ref[:, AW:AW + SW]
        rs = lax.rsqrt(jnp.mean(yz * yz, axis=-1, keepdims=True) + EPS)
        yh = yz * rs
        gys = dcs * gs_ref[...]
        dyz = rs * (gys - yh * jnp.mean(gys * yh, axis=-1, keepdims=True))
        dy_ref[...] = dyz * sz
        dz_ref[...] = (dyz * yv * _silu_grad(zv, sg)).astype(BF)

        @pl.when(i == 0)
        def _():
            part_ref[...] = jnp.zeros_like(part_ref)

        part_ref[0:1, :] += jnp.sum(dca * ah, axis=0, keepdims=True)
        part_ref[1:2, :] += jnp.sum(dcs * yh, axis=0, keepdims=True)

    return pl.pallas_call(
        body, name="mixout_bwd", grid=(s // ts,),
        in_specs=[_row(ts, AW + SW), _row(ts, AW), _row(ts, SW), _row(ts, SW, 3), _fix((1, AW)), _fix((1, SW))],
        out_specs=[_row(ts, AW), _row(ts, SW), _row(ts, SW), _fix((8, AW))],
        out_shape=[jax.ShapeDtypeStruct((s, AW), F32), jax.ShapeDtypeStruct((s, SW), F32),
                   jax.ShapeDtypeStruct((s, SW), BF), jax.ShapeDtypeStruct((8, AW), F32)],
        compiler_params=_cp("arbitrary"))(dcat, att, y, proj, g_att, g_ssd)


TQ = 512
SCALE = HD ** -0.5


def _attn_fwd(qkv, bias):
    s = qkv.shape[0]
    nb = s // TQ
    prev = lambda i: jnp.maximum(i - 1, 0)

    def body(q_ref, kp_ref, kc_ref, vp_ref, vc_ref, b_ref, o_ref, kbuf, vbuf):
        i = pl.program_id(0)
        kbuf[0:TQ, :] = kp_ref[...]
        kbuf[TQ:2 * TQ, :] = kc_ref[...]
        vbuf[0:TQ, :] = vp_ref[...]
        vbuf[TQ:2 * TQ, :] = vc_ref[...]

        def chunk(c, carry):
            r0 = pl.multiple_of(c * CH, CH)
            kpos = i * TQ + r0 - TQ + lax.broadcasted_iota(jnp.int32, (1, BAND), 1)
            valid = kpos >= 0
            for h in range(NH):
                sl = slice(HD * h, HD * h + HD)
                q = q_ref[pl.ds(r0, CH), sl]
                k = kbuf[pl.ds(r0, BAND), sl]
                v = vbuf[pl.ds(r0, BAND), sl]
                sc = lax.dot_general(q, k, (((1,), (1,)), ((), ())), preferred_element_type=F32) * SCALE + b_ref[h]
                sc = jnp.where(valid, sc, -jnp.inf)
                e = jnp.exp(sc - jnp.max(sc, axis=-1, keepdims=True))
                p = e / jnp.sum(e, axis=-1, keepdims=True)
                o_ref[pl.ds(r0, CH), sl] = jnp.dot(p.astype(BF), v, preferred_element_type=F32)
            return carry

        lax.fori_loop(0, TQ // CH, chunk, 0)

    blk = lambda col, im: pl.BlockSpec((TQ, AW), lambda i, _c=col, _f=im: (_f(i), _c))
    cur = lambda i: i
    return pl.pallas_call(
        body, name="attn_fwd", grid=(nb,),
        in_specs=[blk(0, cur), blk(1, prev), blk(1, cur), blk(2, prev), blk(2, cur), _fix((NH, CH, BAND))],
        out_specs=_row(TQ, AW), out_shape=jax.ShapeDtypeStruct((s, AW), F32),
        scratch_shapes=[pltpu.VMEM((2 * TQ, AW), BF), pltpu.VMEM((2 * TQ, AW), BF)],
        compiler_params=_cp("parallel"))(qkv, qkv, qkv, qkv, qkv, bias)


def _attn_bwd(qkv, datt, bias_t):
    s = qkv.shape[0]
    nb = s // TQ
    cur = lambda i: jnp.minimum(i, nb - 1)
    prev = lambda i: jnp.maximum(jnp.minimum(i, nb - 1) - 1, 0)
    late = lambda i: jnp.maximum(i - 1, 0)

    def body(q_ref, kp_ref, kc_ref, vp_ref, vc_ref, do_ref, b_ref, dq_ref, dk_ref, dv_ref, db_ref,
             kbuf, vbuf, dkacc, dvacc):
        i = pl.program_id(0)

        @pl.when(i == 0)
        def _():
            dkacc[...] = jnp.zeros_like(dkacc)
            dvacc[...] = jnp.zeros_like(dvacc)
            db_ref[...] = jnp.zeros_like(db_ref)

        @pl.when(i < nb)
        def _():
            kbuf[0:TQ, :] = kp_ref[...]
            kbuf[TQ:2 * TQ, :] = kc_ref[...]
            vbuf[0:TQ, :] = vp_ref[...]
            vbuf[TQ:2 * TQ, :] = vc_ref[...]

            def chunk(c, carry):
                r0 = pl.multiple_of(c * CH, CH)
                kpos = i * TQ + r0 - TQ + lax.broadcasted_iota(jnp.int32, (BAND, 1), 0)
                valid = kpos >= 0
                for h in range(NH):
                    sl = slice(HD * h, HD * h + HD)
                    q = q_ref[pl.ds(r0, CH), sl]
                    k = kbuf[pl.ds(r0, BAND), sl]
                    v = vbuf[pl.ds(r0, BAND), sl]
                    do = do_ref[pl.ds(r0, CH), sl].astype(BF)
                    st = lax.dot_general(k, q, (((1,), (1,)), ((), ())), preferred_element_type=F32) * SCALE + b_ref[h]
                    st = jnp.where(valid, st, -jnp.inf)
                    e = jnp.exp(st - jnp.max(st, axis=0, keepdims=True))
                    pt = e / jnp.sum(e, axis=0, keepdims=True)
                    dvacc[pl.ds(r0, BAND), sl] += jnp.dot(pt.astype(BF), do, preferred_element_type=F32)
                    dpt = lax.dot_general(v, do, (((1,), (1,)), ((), ())), preferred_element_type=F32)
                    dst = pt * (dpt - jnp.sum(dpt * pt, axis=0, keepdims=True))
                    db_ref[h] += dst
                    dsb = dst.astype(BF)
                    dkacc[pl.ds(r0, BAND), sl] += jnp.dot(dsb, q, preferred_element_type=F32) * SCALE
                    dq = lax.dot_general(dsb, k, (((0,), (0,)), ((), ())), preferred_element_type=F32) * SCALE
                    dq_ref[pl.ds(r0, CH), sl] = dq.astype(BF)
                return carry

            lax.fori_loop(0, TQ // CH, chunk, 0)

        dk_ref[...] = dkacc[0:TQ, :].astype(BF)
        dv_ref[...] = dvacc[0:TQ, :].astype(BF)
        dkacc[0:TQ, :] = dkacc[TQ:2 * TQ, :]
        dvacc[0:TQ, :] = dvacc[TQ:2 * TQ, :]
        dkacc[TQ:2 * TQ, :] = jnp.zeros((TQ, AW), F32)
        dvacc[TQ:2 * TQ, :] = jnp.zeros((TQ, AW), F32)

    blk = lambda col, im: pl.BlockSpec((TQ, AW), lambda i, _c=col, _f=im: (_f(i), _c))
    return pl.pallas_call(
        body, name="attn_bwd", grid=(nb + 1,),
        in_specs=[blk(0, cur), blk(1, prev), blk(1, cur), blk(2, prev), blk(2, cur), blk(0, cur),
                  _fix((NH, BAND, CH))],
        out_specs=[blk(0, cur), blk(0, late), blk(0, late), _fix((NH, BAND, CH))],
        out_shape=[jax.ShapeDtypeStruct((s, AW), BF)] * 3 + [jax.ShapeDtypeStruct((NH, BAND, CH), F32)],
        scratch_shapes=[pltpu.VMEM((2 * TQ, AW), BF), pltpu.VMEM((2 * TQ, AW), BF),
                        pltpu.VMEM((2 * TQ, AW), F32), pltpu.VMEM((2 * TQ, AW), F32)],
        compiler_params=_cp("arbitrary"))(qkv, qkv, qkv, qkv, qkv, datt, bias_t)


QB = 256
NQC = QB // CH
WIN = BAND - CH + QB
NKB = WIN // QB


def _bias_window(bias):
    rows = [jnp.pad(bias, ((0, 0), (0, 0), (CH * a, WIN - BAND - CH * a)), constant_values=-jnp.inf)
            for a in range(NQC)]
    return jnp.concatenate(rows, axis=1)


def _win_specs(width, nb):
    def mk(col, back):
        return pl.BlockSpec((QB, width), lambda i, _c=col, _k=back: (jnp.maximum(jnp.minimum(i, nb - 1) - _k, 0), _c))
    return [mk(col, back) for col in (1, 2) for back in range(NKB - 1, -1, -1)]


def _attn2_fwd(qkv, bias_w):
    s = qkv.shape[0]
    nb = s // QB

    def body(q_ref, *refs):
        krefs, vrefs = refs[:NKB], refs[NKB:2 * NKB]
        b_ref, o_ref, kbuf, vbuf = refs[2 * NKB:]
        i = pl.program_id(0)
        for j in range(NKB):
            kbuf[QB * j:QB * j + QB, :] = krefs[j][...].astype(BF)
            vbuf[QB * j:QB * j + QB, :] = vrefs[j][...].astype(BF)
        valid = (i * QB - (NKB - 1) * QB + lax.broadcasted_iota(jnp.int32, (1, WIN), 1)) >= 0

        def heads(masked):
            for h in range(NH):
                sl = slice(HD * h, HD * h + HD)
                q = (q_ref[:, sl] * SCALE).astype(BF)
                sc = lax.dot_general(q, kbuf[:, sl], _DIMS["NT"], preferred_element_type=F32) + b_ref[h]
                if masked:
                    sc = jnp.where(valid, sc, -jnp.inf)
                e = jnp.exp(sc - jnp.max(sc, axis=-1, keepdims=True))
                o = jnp.dot(e.astype(BF), vbuf[:, sl], preferred_element_type=F32)
                o_ref[:, sl] = o * (1.0 / jnp.sum(e, axis=-1, keepdims=True))

        pl.when(i < NKB - 1)(functools.partial(heads, True))
        pl.when(i >= NKB - 1)(functools.partial(heads, False))

    return pl.pallas_call(
        body, name="attn_fwd", grid=(nb,),
        in_specs=[_row(QB, AW)] + _win_specs(AW, nb) + [_fix((NH, QB, WIN), single=True)],
        out_specs=_row(QB, AW), out_shape=jax.ShapeDtypeStruct((s, AW), F32),
        scratch_shapes=[pltpu.VMEM((WIN, AW), BF), pltpu.VMEM((WIN, AW), BF)],
        compiler_params=_cp("parallel"))(*[qkv] * (1 + 2 * NKB), bias_w)


def _attn2_bwd(qkv, datt, bias_wt):
    s = qkv.shape[0]
    nb = s // QB
    cur = lambda i: jnp.minimum(i, nb - 1)
    late = lambda i: jnp.maximum(i - (NKB - 1), 0)

    def body(q_ref, *refs):
        krefs, vrefs = refs[:NKB], refs[NKB:2 * NKB]
        do_ref, b_ref, dq_ref, dk_ref, dv_ref, db_ref, kbuf, vbuf, dkacc, dvacc, dsbuf = refs[2 * NKB:]
        i = pl.program_id(0)

        @pl.when(i == 0)
        def _():
            dkacc[...] = jnp.zeros_like(dkacc)
            dvacc[...] = jnp.zeros_like(dvacc)
            db_ref[...] = jnp.zeros_like(db_ref)

        def heads(masked):
            for j in range(NKB):
                kbuf[QB * j:QB * j + QB, :] = krefs[j][...].astype(BF)
                vbuf[QB * j:QB * j + QB, :] = vrefs[j][...].astype(BF)
            valid = (i * QB - (NKB - 1) * QB + lax.broadcasted_iota(jnp.int32, (WIN, 1), 0)) >= 0
            for h in range(NH):
                sl = slice(HD * h, HD * h + HD)
                q = (q_ref[:, sl] * SCALE).astype(BF)
                k = kbuf[:, sl]
                do = do_ref[:, sl].astype(BF)
                st = lax.dot_general(k, q, (((1,), (1,)), ((), ())), preferred_element_type=F32) + b_ref[h]
                if masked:
                    st = jnp.where(valid, st, -jnp.inf)
                e = jnp.exp(st - jnp.max(st, axis=0, keepdims=True))
                pt = e * (1.0 / jnp.sum(e, axis=0, keepdims=True))
                dvacc[:, sl] += jnp.dot(pt.astype(BF), do, preferred_element_type=F32)
                dpt = lax.dot_general(vbuf[:, sl], do, (((1,), (1,)), ((), ())), preferred_element_type=F32)
                dst = pt * (dpt - jnp.sum(dpt * pt, axis=0, keepdims=True))
                dsbuf[...] = dst
                fold = dsbuf[0:BAND, 0:CH]
                for a in range(1, NQC):
                    fold = fold + dsbuf[CH * a:CH * a + BAND, CH * a:CH * a + CH]
                db_ref[h] += fold
                dsb = dst.astype(BF)
                dkacc[:, sl] += jnp.dot(dsb, q, preferred_element_type=F32)
                dq = lax.dot_general(dsb, k, (((0,), (0,)), ((), ())), preferred_element_type=F32) * SCALE
                dq_ref[:, sl] = dq.astype(BF)

        pl.when(i < nb)(functools.partial(heads, True))

        dk_ref[...] = dkacc[0:QB, :].astype(BF)
        dv_ref[...] = dvacc[0:QB, :].astype(BF)
        for acc in (dkacc, dvacc):
            for j in range(NKB - 1):
                acc[QB * j:QB * j + QB, :] = acc[QB * j + QB:QB * j + 2 * QB, :]
            acc[WIN - QB:WIN, :] = jnp.zeros((QB, AW), F32)

    qspec = pl.BlockSpec((QB, AW), lambda i: (cur(i), 0))
    lspec = pl.BlockSpec((QB, AW), lambda i: (late(i), 0))
    return pl.pallas_call(
        body, name="attn_bwd", grid=(nb + NKB - 1,),
        in_specs=[qspec] + _win_specs(AW, nb) + [qspec, _fix((NH, WIN, QB), single=True)],
        out_specs=[qspec, lspec, lspec, _fix((NH, BAND, CH))],
        out_shape=[jax.ShapeDtypeStruct((s, AW), BF)] * 3 + [jax.ShapeDtypeStruct((NH, BAND, CH), F32)],
        scratch_shapes=[pltpu.VMEM((WIN, AW), BF), pltpu.VMEM((WIN, AW), BF),
                        pltpu.VMEM((WIN, AW), F32), pltpu.VMEM((WIN, AW), F32), pltpu.VMEM((WIN, QB), F32)],
        compiler_params=_cp("arbitrary"))(*[qkv] * (1 + 2 * NKB), datt, bias_wt)


TB = 512
SCH = 512
NCB = TB // SCH
GW = 8 * HD


def _tri(lower):
    r = lax.broadcasted_iota(jnp.int32, (SCH, SCH), 0)
    c = lax.broadcasted_iota(jnp.int32, (SCH, SCH), 1)
    return r >= c if lower else r <= c


def _ssd_fwd(proj, conv_w, conv_b, dt_bias, a_log, d_skip):
    s = proj.shape[0]
    nb = s // TB

    def body(xs_ref, bc_ref, dt_ref, cw_ref, cb_ref, dtb_ref, al_ref, dsk_ref, y_ref, pre_ref, prev_ref,
             xpad, ubuf, dtbuf, csb, cstb, hst, xdec, yoffb, drow):
        i = pl.program_id(0)

        @pl.when(i == 0)
        def _():
            xpad[0:8, :] = jnp.zeros((8, XBC), F32)
            hst[...] = jnp.zeros_like(hst)

        xpad[8:8 + TB, 0:SW] = xs_ref[...]
        xpad[8:8 + TB, SW:XBC] = bc_ref[...]
        pre = cb_ref[...]
        for kk in range(4):
            pre = pre + cw_ref[kk:kk + 1, :] * xpad[5 + kk:5 + kk + TB, :]
        pre_ref[...] = pre
        ubuf[...] = pre * jax.nn.sigmoid(pre)
        xpad[0:8, :] = xpad[TB:TB + 8, :]
        dtbuf[...] = _softplus(dt_ref[...] + dtb_ref[...])
        a = -jnp.exp(al_ref[...])
        tri = _tri(True).astype(F32)
        causal = _tri(True)

        def chunk(c, carry):
            r0 = pl.multiple_of(c * SCH, SCH)
            rows = pl.ds(r0, SCH)
            cs = jnp.dot(tri, dtbuf[rows, :] * a, precision=HI, preferred_element_type=F32)
            csb[...] = cs
            cstb[...] = cs.T
            for g in range(NG):
                bg = ubuf[rows, SW + NS * g:SW + NS * g + NS]
                cg = ubuf[rows, SW + NG * NS + NS * g:SW + NG * NS + NS * g + NS].astype(BF)
                cb = lax.dot_general(cg, bg.astype(BF), (((1,), (1,)), ((), ())), preferred_element_type=F32)
                hg = hst[g]
                prev_ref[c, g] = hg
                yoffb[...] = jnp.dot(cg, hg.astype(BF), preferred_element_type=F32)
                for r in range(8):
                    h = 8 * g + r
                    sl = slice(HD * h, HD * h + HD)
                    rs = slice(HD * r, HD * r + HD)
                    cs_h = csb[:, h:h + 1]
                    cl = csb[SCH - 1:SCH, h:h + 1]
                    seg = jnp.exp(jnp.where(causal, cs_h - cstb[h:h + 1, :], -jnp.inf))
                    xh = ubuf[rows, sl]
                    xdt = xh * dtbuf[rows, h:h + 1]
                    yd = jnp.dot((cb * seg).astype(BF), xdt.astype(BF), preferred_element_type=F32)
                    y_ref[rows, sl] = yd + jnp.exp(cs_h) * yoffb[:, rs] + xh * dsk_ref[:, h:h + 1]
                    xdec[:, rs] = xdt * jnp.exp(cl - cs_h)
                    drow[:, rs] = jnp.broadcast_to(jnp.exp(cl), (1, HD))
                st = jnp.dot(bg.T.astype(BF), xdec[...].astype(BF), preferred_element_type=F32)
                hst[g] = hg * drow[...] + st
            return carry

        lax.fori_loop(0, NCB, chunk, 0)

    return pl.pallas_call(
        body, name="ssd_fwd", grid=(nb,),
        in_specs=[_row(TB, SW, 4), _row(TB, 512, 10), _row(TB, LANE, 44), _fix((4, XBC)), _fix((1, XBC)),
                  _fix((1, LANE)), _fix((1, LANE)), _fix((1, LANE))],
        out_specs=[_row(TB, SW), _row(TB, XBC), pl.BlockSpec((NCB, NG, NS, GW), lambda i: (i, 0, 0, 0))],
        out_shape=[jax.ShapeDtypeStruct((s, SW), F32), jax.ShapeDtypeStruct((s, XBC), F32),
                   jax.ShapeDtypeStruct((s // SCH, NG, NS, GW), F32)],
        scratch_shapes=[pltpu.VMEM((TB + 8, XBC), F32), pltpu.VMEM((TB, XBC), F32), pltpu.VMEM((TB, LANE), F32),
                        pltpu.VMEM((SCH, LANE), F32), pltpu.VMEM((LANE, SCH), F32), pltpu.VMEM((NG, NS, GW), F32),
                        pltpu.VMEM((SCH, GW), F32), pltpu.VMEM((SCH, GW), F32), pltpu.VMEM((1, GW), F32)],
        compiler_params=_cp("arbitrary"))(proj, proj, proj, conv_w, conv_b, dt_bias, a_log, d_skip)


def _ssd_bwd(dy, pre, proj, prev, conv_w, dt_bias, a_log, d_skip):
    s = dy.shape[0]
    nb = s // TB
    rev = lambda i: nb - 1 - i
    halo = lambda i: jnp.maximum((nb - 1 - i) * (TB // 8) - 1, 0)

    def body(dy_ref, pre_ref, dt_ref, prev_ref, xs_ref, bc_ref, xsh_ref, bch_ref, cw_ref, dtb_ref, al_ref, dsk_ref,
             dx_ref, ddt_ref, pw_ref, ph_ref,
             ubuf, dtbuf, dpad, xpad, csb, cstb, dhs, yoffb, ebuf, xdecb, wbuf, drow, dub):
        i = pl.program_id(0)

        @pl.when(i == 0)
        def _():
            dhs[...] = jnp.zeros_like(dhs)
            dpad[TB:TB + 8, :] = jnp.zeros((8, XBC), F32)
            pw_ref[...] = jnp.zeros_like(pw_ref)
            ph_ref[...] = jnp.zeros_like(ph_ref)

        pre = pre_ref[...]
        ubuf[...] = pre * jax.nn.sigmoid(pre)
        dtbuf[...] = _softplus(dt_ref[...] + dtb_ref[...])
        a = -jnp.exp(al_ref[...])
        tri = _tri(True).astype(F32)
        trit = _tri(False).astype(F32)
        causal = _tri(True)
        rowid = lax.broadcasted_iota(jnp.int32, (LANE, SCH), 0)
        lane = lax.broadcasted_iota(jnp.int32, (SCH, LANE), 1)
        lane1 = lax.broadcasted_iota(jnp.int32, (1, LANE), 1)
        lastrow = lax.broadcasted_iota(jnp.int32, (SCH, LANE), 0) == SCH - 1

        def chunk(cc, carry):
            c = NCB - 1 - cc
            r0 = pl.multiple_of(c * SCH, SCH)
            rows = pl.ds(r0, SCH)
            dtc = dtbuf[rows, :]
            cs = jnp.dot(tri, dtc * a, precision=HI, preferred_element_type=F32)
            csb[...] = cs
            cstb[...] = cs.T
            dcs = jnp.zeros((SCH, LANE), F32)
            dcst = jnp.zeros((LANE, SCH), F32)
            xr = jnp.zeros((SCH, LANE), F32)
            dlast = jnp.zeros((1, LANE), F32)
            dsk = jnp.zeros((1, LANE), F32)
            for g in range(NG):
                bg = ubuf[rows, SW + NS * g:SW + NS * g + NS]
                cg = ubuf[rows, SW + NG * NS + NS * g:SW + NG * NS + NS * g + NS]
                bb = bg.astype(BF)
                cbf = cg.astype(BF)
                cb = lax.dot_general(cbf, bb, (((1,), (1,)), ((), ())), preferred_element_type=F32)
                hp = prev_ref[c, g]
                hpb = hp.astype(BF)
                dhg = dhs[g]
                dhb = dhg.astype(BF)
                yoffb[...] = jnp.dot(cbf, hpb, preferred_element_type=F32)
                for r in range(8):
                    h = 8 * g + r
                    sl = slice(HD * h, HD * h + HD)
                    rs = slice(HD * r, HD * r + HD)
                    cs_h = csb[:, h:h + 1]
                    cl = csb[SCH - 1:SCH, h:h + 1]
                    dyh = dy_ref[rows, sl]
                    ecs = jnp.exp(cs_h)
                    ebuf[:, rs] = dyh * ecs
                    xdecb[:, rs] = ubuf[rows, sl] * dtbuf[rows, h:h + 1] * jnp.exp(cl - cs_h)
                    t1 = jnp.sum(dyh * ecs * yoffb[:, rs], axis=1, keepdims=True)
                    dcs = dcs + jnp.where(lane == h, t1, 0.0)
                    drow[:, rs] = jnp.broadcast_to(jnp.exp(cl), (1, HD))
                eb = ebuf[...].astype(BF)
                dc = lax.dot_general(eb, hpb, (((1,), (1,)), ((), ())), preferred_element_type=F32)
                dprev = jnp.dot(cg.T.astype(BF), eb, preferred_element_type=F32)
                wbuf[...] = jnp.dot(bb, dhb, preferred_element_type=F32)
                db = lax.dot_general(xdecb[...].astype(BF), dhb, (((1,), (1,)), ((), ())), preferred_element_type=F32)
                dcbs = jnp.zeros((SCH, SCH), F32)
                for r in range(8):
                    h = 8 * g + r
                    sl = slice(HD * h, HD * h + HD)
                    rs = slice(HD * r, HD * r + HD)
                    cs_h = csb[:, h:h + 1]
                    cst_h = cstb[h:h + 1, :]
                    cl = csb[SCH - 1:SCH, h:h + 1]
                    seg = jnp.exp(jnp.where(causal, cs_h - cst_h, -jnp.inf))
                    dyh = dy_ref[rows, sl]
                    xh = ubuf[rows, sl]
                    dt_h = dtbuf[rows, h:h + 1]
                    xdt = xh * dt_h
                    dyb = dyh.astype(BF)
                    dm = lax.dot_general(dyb, xdt.astype(BF), _DIMS["NT"], preferred_element_type=F32)
                    mm_ = cb * seg
                    dxdt = lax.dot_general(mm_.astype(BF), dyb, _DIMS["TN"], preferred_element_type=F32)
                    gg = dm * mm_
                    dcs_h = jnp.sum(gg, axis=1, keepdims=True)
                    dcst = dcst + jnp.where(rowid == h, jnp.sum(gg, axis=0, keepdims=True), 0.0)
                    dcbs = dcbs + dm * seg
                    dec = jnp.exp(cl - cs_h)
                    w_h = wbuf[:, rs]
                    dxdt = dxdt + dec * w_h
                    t = jnp.sum(w_h * xdt, axis=1, keepdims=True) * dec
                    dcs_h = dcs_h - t
                    dch = jnp.sum(jnp.sum(dhs[g, :, rs] * prev_ref[c, g, :, rs], axis=1, keepdims=True),
                                  axis=0, keepdims=True)
                    dl = jnp.sum(t, axis=0, keepdims=True) + dch * jnp.exp(cl)
                    dcs = dcs + jnp.where(lane == h, dcs_h, 0.0)
                    dlast = dlast + jnp.where(lane1 == h, dl, 0.0)
                    xr = xr + jnp.where(lane == h, jnp.sum(dxdt * xh, axis=1, keepdims=True), 0.0)
                    dskh = jnp.sum(jnp.sum(dyh * xh, axis=1, keepdims=True), axis=0, keepdims=True)
                    dsk = dsk + jnp.where(lane1 == h, dskh, 0.0)
                    dub[:, sl] = dyh * dsk_ref[:, h:h + 1] + dxdt * dt_h
                dc = dc + jnp.dot(dcbs.astype(BF), bb, preferred_element_type=F32)
                db = db + lax.dot_general(dcbs.astype(BF), cbf, _DIMS["TN"], preferred_element_type=F32)
                dub[:, SW + NS * g:SW + NS * g + NS] = db
                dub[:, SW + NG * NS + NS * g:SW + NG * NS + NS * g + NS] = dc
                dhs[g] = dprev + drow[...] * dhg
            dcs = dcs - dcst.T + jnp.where(lastrow, dlast, 0.0)
            dadt = jnp.dot(trit, dcs, precision=HI, preferred_element_type=F32)
            ddt = dadt * a + xr
            ddtr = ddt * jax.nn.sigmoid(dt_ref[rows, :] + dtb_ref[...])
            ddt_ref[rows, :] = ddtr.astype(BF)
            ph_ref[0:1, :] += jnp.sum(ddtr, axis=0, keepdims=True)
            ph_ref[1:2, :] += jnp.sum(dadt * dtc, axis=0, keepdims=True) * a
            ph_ref[2:3, :] += dsk
            pc = pre_ref[rows, :]
            sg = jax.nn.sigmoid(pc)
            dpad[rows, :] = dub[...] * _silu_grad(pc, sg)
            return carry

        lax.fori_loop(0, NCB, chunk, 0)

        dxb = cw_ref[0:1, :] * dpad[3:3 + TB, :]
        for kk in range(1, 4):
            dxb = dxb + cw_ref[kk:kk + 1, :] * dpad[3 - kk:3 - kk + TB, :]
        dx_ref[...] = dxb.astype(BF)
        keep = jnp.where(i < nb - 1, 1.0, 0.0)
        xpad[0:8, 0:SW] = xsh_ref[...] * keep
        xpad[0:8, SW:XBC] = bch_ref[...] * keep
        xpad[8:8 + TB, 0:SW] = xs_ref[...]
        xpad[8:8 + TB, SW:XBC] = bc_ref[...]
        dp = dpad[0:TB, :]
        for kk in range(4):
            pw_ref[kk:kk + 1, :] += jnp.sum(dp * xpad[5 + kk:5 + kk + TB, :], axis=0, keepdims=True)
        pw_ref[4:5, :] += jnp.sum(dp, axis=0, keepdims=True)
        dpad[TB:TB + 8, :] = dpad[0:8, :]

    rblk = lambda w, col: pl.BlockSpec((TB, w), lambda i, _c=col: (rev(i), _c))
    return pl.pallas_call(
        body, name="ssd_bwd", grid=(nb,),
        in_specs=[rblk(SW, 0), rblk(XBC, 0), rblk(LANE, 44),
                  pl.BlockSpec((NCB, NG, NS, GW), lambda i: (rev(i), 0, 0, 0)),
                  rblk(SW, 4), rblk(512, 10),
                  pl.BlockSpec((8, SW), lambda i: (halo(i), 4)), pl.BlockSpec((8, 512), lambda i: (halo(i), 10)),
                  _fix((4, XBC)), _fix((1, LANE)), _fix((1, LANE)), _fix((1, LANE))],
        out_specs=[rblk(XBC, 0), rblk(LANE, 0), _fix((8, XBC)), _fix((8, LANE))],
        out_shape=[jax.ShapeDtypeStruct((s, XBC), BF), jax.ShapeDtypeStruct((s, LANE), BF),
                   jax.ShapeDtypeStruct((8, XBC), F32), jax.ShapeDtypeStruct((8, LANE), F32)],
        scratch_shapes=[pltpu.VMEM((TB, XBC), F32), pltpu.VMEM((TB, LANE), F32), pltpu.VMEM((TB + 8, XBC), F32),
                        pltpu.VMEM((TB + 8, XBC), F32), pltpu.VMEM((SCH, LANE), F32), pltpu.VMEM((LANE, SCH), F32),
                        pltpu.VMEM((NG, NS, GW), F32), pltpu.VMEM((SCH, GW), F32), pltpu.VMEM((SCH, GW), F32),
                        pltpu.VMEM((SCH, GW), F32), pltpu.VMEM((SCH, GW), F32), pltpu.VMEM((1, GW), F32),
                        pltpu.VMEM((SCH, XBC), F32)],
        compiler_params=_cp("arbitrary"))(dy, pre, proj, prev, proj, proj, proj, proj, conv_w, dt_bias, a_log, d_skip)


def _rel_index():
    q = jnp.arange(CH)[:, None] + (BAND - CH)
    k = jnp.arange(BAND)[None, :]
    return jnp.clip(q - k, -(CH - 1), REL_CLIP) + (CH - 1)


def _local_step(x, tgt, mods, first_weights, late_weights, grads_ready,
                g_mix, rel_bias, conv_w, conv_b, dt_bias, a_log, d_skip, g_att, g_ssd, g_ffn, g_final):
    sh1, sc1, gt1, sh2, sc2, gt2 = mods
    pad16 = lambda v: jnp.pad(v, ((0, 0), (0, LANE - NH)))
    dtb_p, al_p, dsk_p = pad16(dt_bias), pad16(a_log), pad16(d_skip)
    ridx = _rel_index()
    ext = jnp.concatenate([jnp.broadcast_to(rel_bias[:, NREL - 1:], (NH, BAND + CH - 1 - NREL)),
                           rel_bias[:, ::-1]], axis=1)
    skew = jnp.tile(jnp.pad(ext, ((0, 0), (0, 1))), (1, CH + 1))[:, :CH * (BAND + CH + 1)]
    bias = skew.reshape(NH, CH, BAND + CH + 1)[:, ::-1, :BAND]
    bias_w = _bias_window(bias)
    bias_wt = jnp.swapaxes(bias_w, 1, 2)

    h1 = _norm_fwd(x, g_mix, sc1, sh1, name="norm1_fwd")
    w_in_t = first_weights(h1)
    proj = _mm([(h1, w_in_t)], "NT", F32, "in_proj", 1024, 1152, 2048)
    att = _attn2_fwd(proj, bias_w)
    y, pre, prev = _ssd_fwd(proj, conv_w, conv_b, dtb_p, al_p, dsk_p)
    cat = _mixout_fwd(att, y, proj, g_att, g_ssd)
    w_out, w_gate_t, w_up_t, w_down = late_weights(cat)
    mix = _mm([(cat, w_out)], "NN", F32, "out_proj", 1024, 1024, 2048)
    h2, x1 = _norm_fwd(x, g_ffn, sc2, sh2, mix, gt1, name="norm2_fwd")
    gate, up, act = _gate_up(h2, w_gate_t, w_up_t)
    ffn = _mm([(act, w_down)], "NN", F32, "down_proj", 512, 1024, FF)
    dx2, dffn, pf = _final(x1, ffn, gt2, g_final, tgt)

    gw_down = _mm([(act, dffn)], "TN", BF, "gw_down", 1408, 2048, 1024)
    dgate, dup = _gate_up_bwd(dffn, w_down, gate, up)
    dh2 = _mm([(dgate, w_gate_t), (dup, w_up_t)], "NN", F32, "d_h2", 512, 512, FF)
    gw_gate_t = _mm([(dgate, h2)], "TN", BF, "gw_gate", 1408, 2048, 1024)
    gw_up_t = _mm([(dup, h2)], "TN", BF, "gw_up", 1408, 2048, 1024)
    tok = grads_ready("ffn", gw_gate_t, gw_up_t, gw_down)
    dx1, dmix, p2 = _norm_bwd(x1, g_ffn + tok[0, 0], sc2, sh2, dh2, dx2, mix, gt1, name="norm2_bwd")
    gw_out = _mm([(cat, dmix)], "TN", BF, "gw_out", 1024, 2048, 1024)
    tok = grads_ready("out", gw_out)
    dcat = _mm([(dmix, w_out)], "NT", F32, "d_cat", 1024, 1024, 2048, after=tok)
    datt, dy, dz, pm = _mixout_bwd(dcat, att, y, proj, g_att, g_ssd)
    dq, dk, dv, dbias_t = _attn2_bwd(proj, datt, bias_wt)
    dxbc, ddt, pw, ph = _ssd_bwd(dy, pre, proj, prev, conv_w, dtb_p, al_p, dsk_p)
    dproj = jnp.concatenate([dq, dk, dv, dz, dxbc, ddt], axis=1)
    gw_in_t = _mm([(dproj, h1)], "TN", BF, "gw_in", 1152, 2048, 1024)
    tok = grads_ready("in", gw_in_t)
    dh1 = _mm([(dproj, w_in_t)], "NN", F32, "d_h1", 512, 1024, INP, after=tok)
    gx, p1 = _norm_bwd(x, g_mix, sc1, sh1, dh1, dx1, name="norm1_bwd")

    onehot = (ridx.T.reshape(BAND * CH, 1) == jnp.arange(NREL)[None, :]).astype(F32)
    g_rel = _mm([(dbias_t.reshape(NH, BAND * CH), onehot)], "NN", F32, "g_rel_bias", NH, NREL, 4608, precision=HI)

    small = dict(
        dmods=jnp.concatenate([p1[1], p1[0], p2[3], p2[1], p2[0], pf[1]]),
        g_mix=p1[2], conv_b=pw[4], dt_bias=ph[0], a_log=ph[1], d_skip=ph[2], g_att=pm[0], g_ssd=pm[1],
        g_ffn=p2[2], g_final=pf[0], rel_bias=g_rel.reshape(-1), conv_w=pw[0:4].reshape(-1), loss=pf[2])
    return gx, small


def _exchange(srcs, gather, name):
    n = len(srcs)
    outs = [jax.ShapeDtypeStruct((NDEV,) + (s.shape if g else s.shape[1:]), s.dtype) for s, g in zip(srcs, gather)]
    outs.append(jax.ShapeDtypeStruct((8, LANE), F32))

    def body(*refs):
        src, dst, token = refs[:n], refs[n:2 * n], refs[2 * n]
        ssem, rsem, lsem = refs[2 * n + 1:]
        token[...] = jnp.zeros_like(token)
        x, y, c = lax.axis_index("x"), lax.axis_index("y"), lax.axis_index("c")
        me = 4 * x + 2 * y + c
        local = []
        for a in range(n):
            own = src[a] if gather[a] else src[a].at[me]
            cp = pltpu.make_async_copy(own, dst[a].at[me], lsem.at[a])
            cp.start()
            local.append(cp)
        remote = []
        for k in range(1, NDEV):
            px = 1 - x if k & 4 else x
            py = 1 - y if k & 2 else y
            pc = 1 - c if k & 1 else c
            pid = 4 * px + 2 * py + pc
            for a in range(n):
                s_ref = src[a] if gather[a] else src[a].at[pid]
                cp = pltpu.make_async_remote_copy(
                    src_ref=s_ref, dst_ref=dst[a].at[me], send_sem=ssem.at[a * (NDEV - 1) + k - 1],
                    recv_sem=rsem.at[a * (NDEV - 1) + k - 1], device_id=(px, py, pc), device_id_type=MESH)
                cp.start()
                remote.append(cp)
        for cp in remote:
            cp.wait()
        for cp in local:
            cp.wait()

    anyspec = pl.BlockSpec(memory_space=pl.ANY)
    return pl.pallas_call(
        body, name=name, out_shape=outs, in_specs=[anyspec] * n,
        out_specs=[anyspec] * n + [pl.BlockSpec(memory_space=pltpu.VMEM)],
        scratch_shapes=[pltpu.SemaphoreType.DMA((n * (NDEV - 1),)), pltpu.SemaphoreType.DMA((n * (NDEV - 1),)),
                        pltpu.SemaphoreType.DMA((n,))],
    )(*srcs)


_HBM = pl.BlockSpec(memory_space=pltpu.HBM)
_SEM = pl.BlockSpec(memory_space=pltpu.SEMAPHORE)
_EFFECT = pltpu.SideEffectType.DATAFLOW_SIDE_EFFECTING


def _peer_copies(src, land, ssem, rsem, gather):
    x, y, c = lax.axis_index("x"), lax.axis_index("y"), lax.axis_index("c")
    me = 4 * x + 2 * y + c
    copies = []
    for a in range(len(src)):
        for k in range(1, NDEV):
            px = 1 - x if k & 4 else x
            py = 1 - y if k & 2 else y
            pc = 1 - c if k & 1 else c
            s_ref = src[a] if gather[a] else src[a].at[4 * px + 2 * py + pc]
            idx = a * (NDEV - 1) + k - 1
            copies.append(pltpu.make_async_remote_copy(
                src_ref=s_ref, dst_ref=land[a].at[me], send_sem=ssem.at[idx], recv_sem=rsem.at[idx],
                device_id=(px, py, pc), device_id_type=MESH))
    return copies


def _xstart(srcs, gather, name):
    n = len(srcs)
    me = 4 * lax.axis_index("x") + 2 * lax.axis_index("y") + lax.axis_index("c")
    lands = []
    for s_, g_ in zip(srcs, gather):
        own = s_ if g_ else lax.dynamic_index_in_dim(s_, me, 0, keepdims=False)
        empty = lax.empty((NDEV,) + own.shape, own.dtype)
        lands.append(lax.dynamic_update_slice(empty, own[None], (me,) + (0,) * own.ndim))
    nsem = n * (NDEV - 1)

    def body(*refs):
        src, land, ssem, rsem, token = refs[:n], refs[n:2 * n], refs[2 * n], refs[2 * n + 1], refs[-1]
        for cp in _peer_copies(src, land, ssem, rsem, gather):
            cp.start()
        token[...] = jnp.zeros_like(token)

    ops = [pltpu.with_memory_space_constraint(a_, pltpu.HBM) for a_ in list(srcs) + lands]
    return pl.pallas_call(
        body, name=name,
        out_shape=(pltpu.SemaphoreType.DMA((nsem,)), pltpu.SemaphoreType.DMA((nsem,)),
                   *[pltpu.HBM(a_.shape, a_.dtype) for a_ in ops], jax.ShapeDtypeStruct((8, LANE), F32)),
        in_specs=[_HBM] * (2 * n),
        out_specs=(_SEM, _SEM, *[_HBM] * (2 * n), pl.BlockSpec(memory_space=pltpu.VMEM)),
        input_output_aliases={i: 2 + i for i in range(2 * n)},
        compiler_params=pltpu.CompilerParams(has_side_effects=_EFFECT),
    )(*ops)


def _xwait(started, gather, after, name):
    n = len(gather)
    ssem, rsem = started[0], started[1]
    bufs = started[2:2 + 2 * n]

    def body(*refs):
        src, land, ssem_, rsem_ = refs[:n], refs[n:2 * n], refs[2 * n], refs[2 * n + 1]
        for cp in _peer_copies(src, land, ssem_, rsem_, gather):
            cp.wait_send()
            cp.wait_recv()

    out = pl.pallas_call(
        body, name=name, out_shape=tuple(pltpu.HBM(a_.shape, a_.dtype) for a_ in bufs),
        in_specs=[_HBM] * (2 * n) + [_SEM, _SEM, pl.BlockSpec(memory_space=pl.ANY)],
        out_specs=[_HBM] * (2 * n), input_output_aliases={i: i for i in range(2 * n)},
        compiler_params=pltpu.CompilerParams(has_side_effects=_EFFECT),
    )(*bufs, ssem, rsem, after)
    return out[n:]


def _mods(c16, w_shard, b_shard, *, tn=512):
    n = w_shard.shape[1]

    def body(c_ref, w_ref, b_ref, o_ref):
        cv = c_ref[...]
        cond = (cv * jax.nn.sigmoid(cv)).astype(BF)
        o_ref[...] = jnp.dot(cond, w_ref[...].astype(BF), preferred_element_type=F32) + b_ref[...]

    return pl.pallas_call(body, name="ada_mods", grid=(n // tn,),
                          in_specs=[_fix((16, D)), pl.BlockSpec((D, tn), lambda j: (0, j)),
                                    pl.BlockSpec((1, tn), lambda j: (0, j))],
                          out_specs=pl.BlockSpec((16, tn), lambda j: (0, j)),
                          out_shape=jax.ShapeDtypeStruct((16, n), F32), compiler_params=_cp("parallel"))(c16, w_shard, b_shard)


def _gw_ada(c_t, dm, *, tr=256):
    n = dm.shape[1]

    def body(ct_ref, dm_ref, o_ref):
        acc = jnp.zeros((tr, n), F32)
        for b in range(NDEV):
            cv = ct_ref[:, b:b + 1]
            cond = (cv * jax.nn.sigmoid(cv)).astype(BF).astype(F32)
            acc = acc + cond * dm_ref[b:b + 1, :].astype(BF).astype(F32)
        o_ref[...] = acc

    return pl.pallas_call(body, name="gw_ada", grid=(D // tr,),
                          in_specs=[_row(tr, NDEV), _fix((NDEV, n))], out_specs=_row(tr, n),
                          out_shape=jax.ShapeDtypeStruct((D, n), F32), compiler_params=_cp("parallel"))(c_t, dm)


def _sum8(parts, *, name, tc=512):
    _, r, c = parts.shape
    tc = min(tc, c)

    def body(p_ref, o_ref):
        acc = p_ref[0].astype(F32)
        for d in range(1, NDEV):
            acc = acc + p_ref[d].astype(F32)
        o_ref[...] = acc

    return pl.pallas_call(body, name=name, grid=(c // tc,),
                          in_specs=[pl.BlockSpec((NDEV, r, tc), lambda j: (0, 0, j))],
                          out_specs=pl.BlockSpec((r, tc), lambda j: (0, j)),
                          out_shape=jax.ShapeDtypeStruct((r, c), F32), compiler_params=_cp("parallel"))(parts)


def _adam(w, m, v, *, recv=None, grad=None, tr=None, tc=None, name):
    lead = w.ndim == 3
    r, c = w.shape[-2:]
    summed = recv is not None
    assert tc is not None or r % tr == 0
    at = (0,) if lead else (Ellipsis,)

    def body(g_ref, w_ref, m_ref, v_ref, go_ref, do_ref, mo_ref, vo_ref):
        if summed:
            g = g_ref[0].astype(F32)
            for d in range(1, NDEV):
                g = g + g_ref[d].astype(F32)
        else:
            g = g_ref[...]
        go_ref[at] = g
        mn = B1 * m_ref[at] + (1.0 - B1) * g
        vn = B2 * v_ref[at] + (1.0 - B2) * (g * g)
        mo_ref[at] = mn
        vo_ref[at] = vn
        mh = mn / (1.0 - B1 ** STEP)
        vh = vn / (1.0 - B2 ** STEP)
        do_ref[at] = -LR * (mh / (jnp.sqrt(vh) + AEPS) + WD * w_ref[at])

    if tc is not None:
        assert not lead and c % tc == 0
        gspec = (pl.BlockSpec((NDEV, r, tc), lambda j: (0, 0, j)) if summed
                 else pl.BlockSpec((r, tc), lambda j: (0, j)))
        wspec = pl.BlockSpec((r, tc), lambda j: (0, j))
        return pl.pallas_call(body, name=name, grid=(c // tc,),
                              in_specs=[gspec, wspec, wspec, wspec], out_specs=[wspec] * 4,
                              out_shape=[jax.ShapeDtypeStruct(w.shape, F32)] * 4,
                              compiler_params=_cp("parallel"))(recv if summed else grad, w, m, v)
    gspec = pl.BlockSpec((NDEV, tr, c), lambda i: (0, i, 0)) if summed else _row(tr, c)
    wspec = pl.BlockSpec((1, tr, c), lambda i: (0, i, 0)) if lead else _row(tr, c)
    return pl.pallas_call(body, name=name, grid=(r // tr,),
                          in_specs=[gspec, wspec, wspec, wspec], out_specs=[wspec] * 4,
                          out_shape=[jax.ShapeDtypeStruct(w.shape, F32)] * 4,
                          compiler_params=_cp("parallel"))(recv if summed else grad, w, m, v)


def _cols_to_blocks(g, width):
    r = g.shape[0]
    return jnp.transpose(g.reshape(r, NDEV, width), (1, 0, 2))


def _blocks_to_cols(b):
    _, r, width = b.shape
    return jnp.transpose(b, (1, 0, 2)).reshape(r, NDEV * width)


_SMALL = ("b_ada", "g_mix", "rel_bias", "conv_w", "conv_b", "dt_bias", "a_log", "d_skip",
          "g_att_out", "g_ssd_out", "g_ffn", "g_final")


def _pad_rows(v):
    v = v.reshape(-1)
    return jnp.pad(v, (0, (-v.shape[0]) % LANE))


def kernel(x, c, w_ada, b_ada, g_mix, w_in, rel_bias, conv_w, conv_b, dt_bias, a_log, d_skip, g_att_out, g_ssd_out, w_out, g_ffn, w_gate, w_up, w_down, g_final, loss_target, m_w_ada, m_b_ada, m_g_mix, m_w_in, m_rel_bias, m_conv_w, m_conv_b, m_dt_bias, m_a_log, m_d_skip, m_g_att_out, m_g_ssd_out, m_w_out, m_g_ffn, m_w_gate, m_w_up, m_w_down, m_g_final, v_w_ada, v_b_ada, v_g_mix, v_w_in, v_rel_bias, v_conv_w, v_conv_b, v_dt_bias, v_a_log, v_d_skip, v_g_att_out, v_g_ssd_out, v_w_out, v_g_ffn, v_w_gate, v_w_up, v_w_down, v_g_final):
    me = 4 * lax.axis_index("x") + 2 * lax.axis_index("y") + lax.axis_index("c")
    wts = dict(w_ada=w_ada, b_ada=b_ada, g_mix=g_mix, w_in=w_in, rel_bias=rel_bias, conv_w=conv_w, conv_b=conv_b,
               dt_bias=dt_bias, a_log=a_log, d_skip=d_skip, g_att_out=g_att_out, g_ssd_out=g_ssd_out, w_out=w_out,
               g_ffn=g_ffn, w_gate=w_gate, w_up=w_up, w_down=w_down, g_final=g_final)
    mom = dict(w_ada=m_w_ada, b_ada=m_b_ada, g_mix=m_g_mix, w_in=m_w_in, rel_bias=m_rel_bias, conv_w=m_conv_w,
               conv_b=m_conv_b, dt_bias=m_dt_bias, a_log=m_a_log, d_skip=m_d_skip, g_att_out=m_g_att_out,
               g_ssd_out=m_g_ssd_out, w_out=m_w_out, g_ffn=m_g_ffn, w_gate=m_w_gate, w_up=m_w_up, w_down=m_w_down,
               g_final=m_g_final)
    var = dict(w_ada=v_w_ada, b_ada=v_b_ada, g_mix=v_g_mix, w_in=v_w_in, rel_bias=v_rel_bias, conv_w=v_conv_w,
               conv_b=v_conv_b, dt_bias=v_dt_bias, a_log=v_a_log, d_skip=v_d_skip, g_att_out=v_g_att_out,
               g_ssd_out=v_g_ssd_out, w_out=v_w_out, g_ffn=v_g_ffn, w_gate=v_w_gate, w_up=v_w_up, w_down=v_w_down,
               g_final=v_g_final)
    order = ("w_ada", "b_ada", "g_mix", "w_in", "rel_bias", "conv_w", "conv_b", "dt_bias", "a_log", "d_skip",
             "g_att_out", "g_ssd_out", "w_out", "g_ffn", "w_gate", "w_up", "w_down", "g_final")

    c_all, rel_all, cw_all, _ = _exchange([c, rel_bias[0], conv_w[0]], [True] * 3, "gather_small")
    c_all = c_all.reshape(NDEV, D)
    rel_full = _blocks_to_cols(rel_all)
    cw_full = _blocks_to_cols(cw_all)

    ncol = w_ada.shape[2]
    b_sh = lax.dynamic_slice(b_ada, (0, me * ncol), (1, ncol))
    mods_part = _mods(jnp.pad(c_all, ((0, 8), (0, 0))), w_ada[0], b_sh)[:NDEV]
    mods_all, tok_m = _exchange([mods_part], [True], "gather_mods")
    mods_me = lax.dynamic_index_in_dim(mods_all, me, axis=1, keepdims=False).reshape(6, 1, D)
    mods = [mods_me[i] for i in range(6)]

    st_in = _xstart([(w_in[0] + tok_m[0, 0]).T.astype(BF)], [True], "gather_w_in_start")
    st_rest = _xstart([(w_out[0] + st_in[-1][0, 0]).astype(BF), w_gate[0].T.astype(BF), w_up[0].T.astype(BF),
                       w_down[0].astype(BF)], [True] * 4, "gather_w_rest_start")

    def first_weights(after):
        (gi,) = _xwait(st_in, [True], after, "gather_w_in_wait")
        return jnp.pad(gi.reshape(INC, D), ((0, INP - INC), (0, 0)))

    def late_weights(after):
        go, gg, gu_, gd = _xwait(st_rest, [True] * 4, after, "gather_w_rest_wait")
        return go.reshape(D, D), gg.reshape(FF, D), gu_.reshape(FF, D), gd.reshape(FF, D)

    started = {}

    def grads_ready(which, *g):
        if which == "in":
            g = (g[0][:INC],)
        srcs = [g_.reshape(NDEV, g_.shape[0] // NDEV, D) for g_ in g]
        started[which] = _xstart(srcs, [False] * len(srcs), f"exchange_{which}_grads_start")
        return started[which][-1]

    gx, small = _local_step(
        x[0], loss_target[0], mods, first_weights, late_weights, grads_ready,
        g_mix + st_rest[-1][0, 0], rel_full, cw_full, conv_b, dt_bias, a_log, d_skip, g_att_out, g_ssd_out, g_ffn,
        g_final.reshape(1, D))

    loss_row = jnp.pad(jnp.sum(small["loss"]).reshape(1), (0, LANE - 1))
    pack = jnp.concatenate([small[k] for k in ("dmods", "g_mix", "conv_b", "dt_bias", "a_log", "d_skip", "g_att",
                                               "g_ssd", "g_ffn", "g_final", "rel_bias", "conv_w")] + [loss_row])
    packs, _ = _exchange([pack.reshape(-1, LANE)], [True], "gather_small_grads")
    rg, ru, rd = _xwait(started["ffn"], [False] * 3, packs, "exchange_ffn_grads_wait")
    (ro,) = _xwait(started["out"], [False], rg, "exchange_out_grads_wait")
    (ri,) = _xwait(started["in"], [False], ro, "exchange_in_grads_wait")
    big = dict(
        w_gate=[o_.T for o_ in _adam(w_gate[0].T, m_w_gate[0].T, v_w_gate[0].T, recv=rg, tc=512,
                                     name="adam_w_gate")],
        w_up=[o_.T for o_ in _adam(w_up[0].T, m_w_up[0].T, v_w_up[0].T, recv=ru, tc=512, name="adam_w_up")],
        w_down=_adam(w_down[0], m_w_down[0], v_w_down[0], recv=rd, tr=176, name="adam_w_down"))
    tot = _sum8(packs, name="sum_small").reshape(-1)

    sizes = dict(dmods=6 * D, g_mix=D, conv_b=XBC, dt_bias=LANE, a_log=LANE, d_skip=LANE, g_att_out=AW, g_ssd_out=SW,
                 g_ffn=D, g_final=D, rel_bias=NH * NREL, conv_w=4 * XBC, loss=LANE)
    off, o = {}, 0
    for k_, n_ in sizes.items():
        off[k_] = o
        o += n_
    take = lambda k_, n_=None: tot[off[k_]:off[k_] + (n_ or sizes[k_])]
    loss = tot[off["loss"]]
    rel_w = NREL // NDEV
    cw_w = XBC // NDEV
    gsm = dict(
        b_ada=take("dmods"), g_mix=take("g_mix"),
        rel_bias=lax.dynamic_slice(take("rel_bias").reshape(NH, NREL), (0, me * rel_w), (NH, rel_w)),
        conv_w=lax.dynamic_slice(take("conv_w").reshape(4, XBC), (0, me * cw_w), (4, cw_w)),
        conv_b=take("conv_b"), dt_bias=take("dt_bias", NH), a_log=take("a_log", NH), d_skip=take("d_skip", NH),
        g_att_out=take("g_att_out"), g_ssd_out=take("g_ssd_out"), g_ffn=take("g_ffn"), g_final=take("g_final"))

    flat = lambda d_: jnp.concatenate([_pad_rows(d_[k_]) for k_ in _SMALL])
    gflat = flat(gsm)
    nrow = -(-gflat.shape[0] // (8 * LANE)) * 8
    to2d = lambda v_: jnp.pad(v_, (0, nrow * LANE - v_.shape[0])).reshape(nrow, LANE)
    res = _adam(to2d(flat(wts)), to2d(flat(mom)), to2d(flat(var)), grad=to2d(gflat), tr=nrow, name="adam_small")
    outs = {k_: {} for k_ in ("grad", "delta", "m", "v")}
    o = 0
    for k_ in _SMALL:
        n_ = wts[k_].size
        for kind, arr in zip(("grad", "delta", "m", "v"), res):
            outs[kind][k_] = arr.reshape(-1)[o:o + n_].reshape(wts[k_].shape)
        o += n_ + (-n_) % LANE

    dm_all = packs[:, :6 * D // LANE, :].reshape(NDEV, 6 * D)
    dm_sh = lax.dynamic_slice(dm_all, (0, me * ncol), (NDEV, ncol))
    gwa = _gw_ada(c_all.T, dm_sh)
    big.update(
        w_ada=_adam(w_ada[0], m_w_ada[0], v_w_ada[0], grad=gwa, tr=256, name="adam_w_ada"),
        w_in=[o_.T for o_ in _adam(w_in[0].T, m_w_in[0].T, v_w_in[0].T, recv=ri, tc=512, name="adam_w_in")],
        w_out=_adam(w_out[0], m_w_out[0], v_w_out[0], recv=ro, tr=128, name="adam_w_out"))
    for k_, r_ in big.items():
        for kind, arr in zip(("grad", "delta", "m", "v"), r_):
            outs[kind][k_] = arr.reshape(wts[k_].shape)

    return (loss, gx.reshape(x.shape), *[outs["grad"][k_] for k_ in order], *[outs["delta"][k_] for k_ in order],
            *[outs["m"][k_] for k_ in order], *[outs["v"][k_] for k_ in order])
```

```python
import functools

import jax
import jax.numpy as jnp
from jax import lax
from jax.experimental import pallas as pl
from jax.experimental.pallas import tpu as pltpu

F32 = jnp.float32
BF = jnp.bfloat16
HI = lax.Precision.HIGHEST

D = 2048
NH = 16
HD = 64
AW = 1024
SW = 1024
NG = 2
NS = 128
XBC = 1536
CH = 64
BAND = 576
NREL = 320
REL_CLIP = 256
FF = 5632
INC = 5648
INP = 5760
EPS = 1e-6
NDEV = 8
LANE = 128
VMEM_LIMIT = 56 * 1024 * 1024
LR, B1, B2, AEPS, WD, STEP = 0.001, 0.9, 0.999, 1e-08, 0.01, 10
MESH = pl.DeviceIdType.MESH


def _cp(*sem):
    return pltpu.CompilerParams(dimension_semantics=sem, vmem_limit_bytes=VMEM_LIMIT)


def _row(ts, w, col=0):
    return pl.BlockSpec((ts, w), lambda i, _c=col: (i, _c))


def _fix(shape, single=False):
    nd = len(shape)
    if single:
        return pl.BlockSpec(shape, lambda i, _n=nd: (0,) * _n, pipeline_mode=pl.Buffered(1))
    return pl.BlockSpec(shape, lambda i, _n=nd: (0,) * _n)


def _silu_grad(x, s):
    return s * (1.0 + x * (1.0 - s))


def _softplus(x):
    return jnp.maximum(x, 0.0) + jnp.log1p(jnp.exp(-jnp.abs(x)))


_DIMS = {"NN": (((1,), (0,)), ((), ())), "TN": (((0,), (0,)), ((), ())), "NT": (((1,), (1,)), ((), ()))}


def _mm(pairs, mode, out_dtype, name, tm, tn, tk, precision=None, after=None):
    a0, b0 = pairs[0]
    m, k = a0.shape[::-1] if mode == "TN" else a0.shape
    n = b0.shape[0] if mode == "NT" else b0.shape[1]
    tm, tn, tk = min(tm, m), min(tn, n), min(tk, k)
    nk1 = k // tk
    npair = len(pairs)
    nk = 1 if nk1 == 1 else nk1 * npair
    assert m % tm == 0 and n % tn == 0 and k % tk == 0, (name, a0.shape, b0.shape)
    dims = _DIMS[mode]

    def kloc(kk, p):
        return jnp.clip(kk - p * nk1, 0, nk1 - 1)

    def aspec(p):
        if mode == "TN":
            return pl.BlockSpec((tk, tm), lambda i, j, kk, _p=p: (kloc(kk, _p), i))
        return pl.BlockSpec((tm, tk), lambda i, j, kk, _p=p: (i, kloc(kk, _p)))

    def bspec(p):
        if mode == "NT":
            return pl.BlockSpec((tn, tk), lambda i, j, kk, _p=p: (j, kloc(kk, _p)))
        return pl.BlockSpec((tk, tn), lambda i, j, kk, _p=p: (kloc(kk, _p), j))

    nin = 2 * npair + (after is not None)

    def body(*refs):
        ab, o_ref = refs[:2 * npair], refs[nin]
        if nk == 1:
            tot = lax.dot_general(ab[0][...], ab[1][...], dims, preferred_element_type=F32, precision=precision)
            for p in range(1, npair):
                tot = tot + lax.dot_general(ab[2 * p][...], ab[2 * p + 1][...], dims, preferred_element_type=F32,
                                            precision=precision)
            o_ref[...] = tot.astype(o_ref.dtype)
            return
        acc = refs[nin + 1]
        kk = pl.program_id(2)

        @pl.when(kk == 0)
        def _():
            acc[...] = jnp.zeros_like(acc)

        for p in range(npair):
            @pl.when((kk >= p * nk1) & (kk < (p + 1) * nk1))
            def _(p=p):
                acc[...] += lax.dot_general(ab[2 * p][...], ab[2 * p + 1][...], dims, preferred_element_type=F32,
                                            precision=precision)

        @pl.when(kk == nk - 1)
        def _():
            o_ref[...] = acc[...].astype(o_ref.dtype)

    in_specs = []
    for p in range(npair):
        in_specs += [aspec(p), bspec(p)]
    operands = [t for ab_ in pairs for t in ab_]
    if after is not None:
        in_specs.append(pl.BlockSpec(memory_space=pl.ANY))
        operands.append(after)
    return pl.pallas_call(
        body, name=name, grid=(m // tm, n // tn, nk), in_specs=in_specs,
        out_specs=pl.BlockSpec((tm, tn), lambda i, j, kk: (i, j)),
        out_shape=jax.ShapeDtypeStruct((m, n), out_dtype),
        scratch_shapes=[] if nk == 1 else [pltpu.VMEM((tm, tn), F32)],
        compiler_params=_cp("parallel", "parallel", "arbitrary"),
    )(*operands)


def _norm_fwd(x, g, sc, sh, mix=None, gt=None, *, name, ts=256):
    s, d = x.shape
    has = mix is not None

    def body(*refs):
        if has:
            x_ref, g_ref, sc_ref, sh_ref, mix_ref, gt_ref, h_ref, x1_ref = refs
        else:
            x_ref, g_ref, sc_ref, sh_ref, h_ref = refs
        xv = x_ref[...]
        if has:
            xv = xv + gt_ref[...] * mix_ref[...]
            x1_ref[...] = xv
        r = lax.rsqrt(jnp.mean(xv * xv, axis=-1, keepdims=True) + EPS)
        h_ref[...] = ((xv * r) * g_ref[...] * (1.0 + sc_ref[...]) + sh_ref[...]).astype(BF)

    ins = [x, g, sc, sh] + ([mix, gt] if has else [])
    in_specs = [_row(ts, d), _fix((1, d)), _fix((1, d)), _fix((1, d))] + ([_row(ts, d), _fix((1, d))] if has else [])
    out_shape = [jax.ShapeDtypeStruct((s, d), BF)] + ([jax.ShapeDtypeStruct((s, d), F32)] if has else [])
    out_specs = [_row(ts, d)] + ([_row(ts, d)] if has else [])
    out = pl.pallas_call(body, name=name, grid=(s // ts,), in_specs=in_specs, out_specs=out_specs,
                         out_shape=out_shape, compiler_params=_cp("parallel"))(*ins)
    return out if has else out[0]


def _norm_bwd(x, g, sc, sh, dh, dres, mix=None, gt=None, *, name, ts=256):
    s, d = x.shape
    has = mix is not None

    def body(*refs):
        if has:
            x_ref, g_ref, sc_ref, sh_ref, dh_ref, dres_ref, mix_ref, gt_ref, dx_ref, dmix_ref, part_ref = refs
        else:
            x_ref, g_ref, sc_ref, sh_ref, dh_ref, dres_ref, dx_ref, part_ref = refs
        i = pl.program_id(0)
        xv = x_ref[...]
        gv = g_ref[...]
        dhv = dh_ref[...]
        r = lax.rsqrt(jnp.mean(xv * xv, axis=-1, keepdims=True) + EPS)
        xh = xv * r
        dyn = dhv * (1.0 + sc_ref[...])
        gy = dyn * gv
        dx = dres_ref[...] + r * (gy - xh * jnp.mean(gy * xh, axis=-1, keepdims=True))
        dx_ref[...] = dx

        @pl.when(i == 0)
        def _():
            part_ref[...] = jnp.zeros_like(part_ref)

        part_ref[0:1, :] += jnp.sum(dhv * (xh * gv), axis=0, keepdims=True)
        part_ref[1:2, :] += jnp.sum(dhv, axis=0, keepdims=True)
        part_ref[2:3, :] += jnp.sum(dyn * xh, axis=0, keepdims=True)
        if has:
            part_ref[3:4, :] += jnp.sum(dx * mix_ref[...], axis=0, keepdims=True)
            dmix_ref[...] = (gt_ref[...] * dx).astype(BF)

    ins = [x, g, sc, sh, dh, dres] + ([mix, gt] if has else [])
    in_specs = [_row(ts, d), _fix((1, d)), _fix((1, d)), _fix((1, d)), _row(ts, d), _row(ts, d)]
    in_specs += [_row(ts, d), _fix((1, d))] if has else []
    out_shape = [jax.ShapeDtypeStruct((s, d), F32)] + ([jax.ShapeDtypeStruct((s, d), BF)] if has else [])
    out_shape += [jax.ShapeDtypeStruct((8, d), F32)]
    out_specs = [_row(ts, d)] + ([_row(ts, d)] if has else []) + [_fix((8, d))]
    return pl.pallas_call(body, name=name, grid=(s // ts,), in_specs=in_specs, out_specs=out_specs,
                          out_shape=out_shape, compiler_params=_cp("arbitrary"))(*ins)


def _final(x1, ffn, gt2, gf, tgt, *, ts=256):
    s, d = x1.shape

    def body(x1_ref, ffn_ref, gt_ref, g_ref, t_ref, dx_ref, dffn_ref, part_ref):
        i = pl.program_id(0)
        fv = ffn_ref[...]
        gv = g_ref[...]
        xv = x1_ref[...] + gt_ref[...] * fv
        r = lax.rsqrt(jnp.mean(xv * xv, axis=-1, keepdims=True) + EPS)
        xh = xv * r
        e = xh * gv - t_ref[...]
        dy = e * (1.0 / d)
        gy = dy * gv
        dx = r * (gy - xh * jnp.mean(gy * xh, axis=-1, keepdims=True))
        dx_ref[...] = dx
        dffn_ref[...] = (gt_ref[...] * dx).astype(BF)

        @pl.when(i == 0)
        def _():
            part_ref[...] = jnp.zeros_like(part_ref)

        part_ref[0:1, :] += jnp.sum(dy * xh, axis=0, keepdims=True)
        part_ref[1:2, :] += jnp.sum(dx * fv, axis=0, keepdims=True)
        part_ref[2:3, :] += jnp.sum(e * e, axis=0, keepdims=True) * (0.5 / d)

    return pl.pallas_call(
        body, name="final_loss", grid=(s // ts,),
        in_specs=[_row(ts, d), _row(ts, d), _fix((1, d)), _fix((1, d)), _row(ts, d)],
        out_specs=[_row(ts, d), _row(ts, d), _fix((8, d))],
        out_shape=[jax.ShapeDtypeStruct((s, d), F32), jax.ShapeDtypeStruct((s, d), BF),
                   jax.ShapeDtypeStruct((8, d), F32)],
        compiler_params=_cp("arbitrary"))(x1, ffn, gt2, gf, tgt)


def _gate_up(h, wg_t, wu_t, *, tm=1024, tn=512):
    s, k = h.shape
    n = wg_t.shape[0]
    tm = min(tm, s)

    def body(h_ref, wg_ref, wu_ref, g_ref, u_ref, a_ref):
        hv = h_ref[...]
        gv = lax.dot_general(hv, wg_ref[...], _DIMS["NT"], preferred_element_type=F32)
        uv = lax.dot_general(hv, wu_ref[...], _DIMS["NT"], preferred_element_type=F32)
        g_ref[...] = gv.astype(BF)
        u_ref[...] = uv.astype(BF)
        a_ref[...] = (gv * jax.nn.sigmoid(gv) * uv).astype(BF)

    wspec = pl.BlockSpec((tn, k), lambda i, j: (j, 0))
    ospec = pl.BlockSpec((tm, tn), lambda i, j: (i, j))
    return pl.pallas_call(body, name="gate_up", grid=(s // tm, n // tn),
                          in_specs=[pl.BlockSpec((tm, k), lambda i, j: (i, 0)), wspec, wspec],
                          out_specs=[ospec] * 3, out_shape=[jax.ShapeDtypeStruct((s, n), BF)] * 3,
                          compiler_params=_cp("parallel", "parallel"))(h, wg_t, wu_t)


def _gate_up_bwd(dffn, w_down, gate, up, *, tm=1024, tn=512):
    s, k = dffn.shape
    n = w_down.shape[0]
    tm = min(tm, s)

    def body(d_ref, w_ref, g_ref, u_ref, dg_ref, du_ref):
        dav = lax.dot_general(d_ref[...], w_ref[...], _DIMS["NT"], preferred_element_type=F32)
        gv = g_ref[...].astype(F32)
        uv = u_ref[...].astype(F32)
        sg = jax.nn.sigmoid(gv)
        dg_ref[...] = (dav * uv * _silu_grad(gv, sg)).astype(BF)
        du_ref[...] = (dav * gv * sg).astype(BF)

    tile = pl.BlockSpec((tm, tn), lambda i, j: (i, j))
    return pl.pallas_call(body, name="gate_up_bwd", grid=(s // tm, n // tn),
                          in_specs=[pl.BlockSpec((tm, k), lambda i, j: (i, 0)),
                                    pl.BlockSpec((tn, k), lambda i, j: (j, 0)), tile, tile],
                          out_specs=[tile] * 2, out_shape=[jax.ShapeDtypeStruct((s, n), BF)] * 2,
                          compiler_params=_cp("parallel", "parallel"))(dffn, w_down, gate, up)


def _mixout_fwd(att, y, proj, g_att, g_ssd, *, ts=256):
    s = att.shape[0]

    def body(a_ref, y_ref, z_ref, ga_ref, gs_ref, o_ref):
        av = a_ref[...]
        ra = lax.rsqrt(jnp.mean(av * av, axis=-1, keepdims=True) + EPS)
        o_ref[:, 0:AW] = (av * ra * ga_ref[...]).astype(BF)
        zv = z_ref[...]
        yz = y_ref[...] * (zv * jax.nn.sigmoid(zv))
        rs = lax.rsqrt(jnp.mean(yz * yz, axis=-1, keepdims=True) + EPS)
        o_ref[:, AW:AW + SW] = (yz * rs * gs_ref[...]).astype(BF)

    return pl.pallas_call(body, name="mixout_fwd", grid=(s // ts,),
                          in_specs=[_row(ts, AW), _row(ts, SW), _row(ts, SW, 3), _fix((1, AW)), _fix((1, SW))],
                          out_specs=_row(ts, AW + SW), out_shape=jax.ShapeDtypeStruct((s, AW + SW), BF),
                          compiler_params=_cp("parallel"))(att, y, proj, g_att, g_ssd)


def _mixout_bwd(dcat, att, y, proj, g_att, g_ssd, *, ts=256):
    s = att.shape[0]

    def body(dc_ref, a_ref, y_ref, z_ref, ga_ref, gs_ref, da_ref, dy_ref, dz_ref, part_ref):
        i = pl.program_id(0)
        av = a_ref[...]
        dca = dc_ref[:, 0:AW]
        ra = lax.rsqrt(jnp.mean(av * av, axis=-1, keepdims=True) + EPS)
        ah = av * ra
        gy = dca * ga_ref[...]
        da_ref[...] = ra * (gy - ah * jnp.mean(gy * ah, axis=-1, keepdims=True))
        zv = z_ref[...]
        yv = y_ref[...]
        sg = jax.nn.sigmoid(zv)
        sz = zv * sg
        yz = yv * sz
        dcs = dc_ref[:, AW:AW + SW]
        rs = lax.rsqrt(jnp.mean(yz * yz, axis=-1, keepdims=True) + EPS)
        yh = yz * rs
        gys = dcs * gs_ref[...]
        dyz = rs * (gys - yh * jnp.mean(gys * yh, axis=-1, keepdims=True))
        dy_ref[...] = dyz * sz
        dz_ref[...] = (dyz * yv * _silu_grad(zv, sg)).astype(BF)

        @pl.when(i == 0)
        def _():
            part_ref[...] = jnp.zeros_like(part_ref)

        part_ref[0:1, :] += jnp.sum(dca * ah, axis=0, keepdims=True)
        part_ref[1:2, :] += jnp.sum(dcs * yh, axis=0, keepdims=True)

    return pl.pallas_call(
        body, name="mixout_bwd", grid=(s // ts,),
        in_specs=[_row(ts, AW + SW), _row(ts, AW), _row(ts, SW), _row(ts, SW, 3), _fix((1, AW)), _fix((1, SW))],
        out_specs=[_row(ts, AW), _row(ts, SW), _row(ts, SW), _fix((8, AW))],
        out_shape=[jax.ShapeDtypeStruct((s, AW), F32), jax.ShapeDtypeStruct((s, SW), F32),
                   jax.ShapeDtypeStruct((s, SW), BF), jax.ShapeDtypeStruct((8, AW), F32)],
        compiler_params=_cp("arbitrary"))(dcat, att, y, proj, g_att, g_ssd)


TQ = 512
SCALE = HD ** -0.5


def _attn_fwd(qkv, bias):
    s = qkv.shape[0]
    nb = s // TQ
    prev = lambda i: jnp.maximum(i - 1, 0)

    def body(q_ref, kp_ref, kc_ref, vp_ref, vc_ref, b_ref, o_ref, kbuf, vbuf):
        i = pl.program_id(0)
        kbuf[0:TQ, :] = kp_ref[...]
        kbuf[TQ:2 * TQ, :] = kc_ref[...]
        vbuf[0:TQ, :] = vp_ref[...]
        vbuf[TQ:2 * TQ, :] = vc_ref[...]

        def chunk(c, carry):
            r0 = pl.multiple_of(c * CH, CH)
            kpos = i * TQ + r0 - TQ + lax.broadcasted_iota(jnp.int32, (1, BAND), 1)
            valid = kpos >= 0
            for h in range(NH):
                sl = slice(HD * h, HD * h + HD)
                q = q_ref[pl.ds(r0, CH), sl]
                k = kbuf[pl.ds(r0, BAND), sl]
                v = vbuf[pl.ds(r0, BAND), sl]
                sc = lax.dot_general(q, k, (((1,), (1,)), ((), ())), preferred_element_type=F32) * SCALE + b_ref[h]
                sc = jnp.where(valid, sc, -jnp.inf)
                e = jnp.exp(sc - jnp.max(sc, axis=-1, keepdims=True))
                p = e / jnp.sum(e, axis=-1, keepdims=True)
                o_ref[pl.ds(r0, CH), sl] = jnp.dot(p.astype(BF), v, preferred_element_type=F32)
            return carry

        lax.fori_loop(0, TQ // CH, chunk, 0)

    blk = lambda col, im: pl.BlockSpec((TQ, AW), lambda i, _c=col, _f=im: (_f(i), _c))
    cur = lambda i: i
    return pl.pallas_call(
        body, name="attn_fwd", grid=(nb,),
        in_specs=[blk(0, cur), blk(1, prev), blk(1, cur), blk(2, prev), blk(2, cur), _fix((NH, CH, BAND))],
        out_specs=_row(TQ, AW), out_shape=jax.ShapeDtypeStruct((s, AW), F32),
        scratch_shapes=[pltpu.VMEM((2 * TQ, AW), BF), pltpu.VMEM((2 * TQ, AW), BF)],
        compiler_params=_cp("parallel"))(qkv, qkv, qkv, qkv, qkv, bias)


def _attn_bwd(qkv, datt, bias_t):
    s = qkv.shape[0]
    nb = s // TQ
    cur = lambda i: jnp.minimum(i, nb - 1)
    prev = lambda i: jnp.maximum(jnp.minimum(i, nb - 1) - 1, 0)
    late = lambda i: jnp.maximum(i - 1, 0)

    def body(q_ref, kp_ref, kc_ref, vp_ref, vc_ref, do_ref, b_ref, dq_ref, dk_ref, dv_ref, db_ref,
             kbuf, vbuf, dkacc, dvacc):
        i = pl.program_id(0)

        @pl.when(i == 0)
        def _():
            dkacc[...] = jnp.zeros_like(dkacc)
            dvacc[...] = jnp.zeros_like(dvacc)
            db_ref[...] = jnp.zeros_like(db_ref)

        @pl.when(i < nb)
        def _():
            kbuf[0:TQ, :] = kp_ref[...]
            kbuf[TQ:2 * TQ, :] = kc_ref[...]
            vbuf[0:TQ, :] = vp_ref[...]
            vbuf[TQ:2 * TQ, :] = vc_ref[...]

            def chunk(c, carry):
                r0 = pl.multiple_of(c * CH, CH)
                kpos = i * TQ + r0 - TQ + lax.broadcasted_iota(jnp.int32, (BAND, 1), 0)
                valid = kpos >= 0
                for h in range(NH):
                    sl = slice(HD * h, HD * h + HD)
                    q = q_ref[pl.ds(r0, CH), sl]
                    k = kbuf[pl.ds(r0, BAND), sl]
                    v = vbuf[pl.ds(r0, BAND), sl]
                    do = do_ref[pl.ds(r0, CH), sl].astype(BF)
                    st = lax.dot_general(k, q, (((1,), (1,)), ((), ())), preferred_element_type=F32) * SCALE + b_ref[h]
                    st = jnp.where(valid, st, -jnp.inf)
                    e = jnp.exp(st - jnp.max(st, axis=0, keepdims=True))
                    pt = e / jnp.sum(e, axis=0, keepdims=True)
                    dvacc[pl.ds(r0, BAND), sl] += jnp.dot(pt.astype(BF), do, preferred_element_type=F32)
                    dpt = lax.dot_general(v, do, (((1,), (1,)), ((), ())), preferred_element_type=F32)
                    dst = pt * (dpt - jnp.sum(dpt * pt, axis=0, keepdims=True))
                    db_ref[h] += dst
                    dsb = dst.astype(BF)
                    dkacc[pl.ds(r0, BAND), sl] += jnp.dot(dsb, q, preferred_element_type=F32) * SCALE
                    dq = lax.dot_general(dsb, k, (((0,), (0,)), ((), ())), preferred_element_type=F32) * SCALE
                    dq_ref[pl.ds(r0, CH), sl] = dq.astype(BF)
                return carry

            lax.fori_loop(0, TQ // CH, chunk, 0)

        dk_ref[...] = dkacc[0:TQ, :].astype(BF)
        dv_ref[...] = dvacc[0:TQ, :].astype(BF)
        dkacc[0:TQ, :] = dkacc[TQ:2 * TQ, :]
        dvacc[0:TQ, :] = dvacc[TQ:2 * TQ, :]
        dkacc[TQ:2 * TQ, :] = jnp.zeros((TQ, AW), F32)
        dvacc[TQ:2 * TQ, :] = jnp.zeros((TQ, AW), F32)

    blk = lambda col, im: pl.BlockSpec((TQ, AW), lambda i, _c=col, _f=im: (_f(i), _c))
    return pl.pallas_call(
        body, name="attn_bwd", grid=(nb + 1,),
        in_specs=[blk(0, cur), blk(1, prev), blk(1, cur), blk(2, prev), blk(2, cur), blk(0, cur),
                  _fix((NH, BAND, CH))],
        out_specs=[blk(0, cur), blk(0, late), blk(0, late), _fix((NH, BAND, CH))],
        out_shape=[jax.ShapeDtypeStruct((s, AW), BF)] * 3 + [jax.ShapeDtypeStruct((NH, BAND, CH), F32)],
        scratch_shapes=[pltpu.VMEM((2 * TQ, AW), BF), pltpu.VMEM((2 * TQ, AW), BF),
                        pltpu.VMEM((2 * TQ, AW), F32), pltpu.VMEM((2 * TQ, AW), F32)],
        compiler_params=_cp("arbitrary"))(qkv, qkv, qkv, qkv, qkv, datt, bias_t)


QB = 256
NQC = QB // CH
WIN = BAND - CH + QB
NKB = WIN // QB


def _bias_window(bias):
    rows = [jnp.pad(bias, ((0, 0), (0, 0), (CH * a, WIN - BAND - CH * a)), constant_values=-jnp.inf)
            for a in range(NQC)]
    return jnp.concatenate(rows, axis=1)


def _win_specs(width, nb):
    def mk(col, back):
        return pl.BlockSpec((QB, width), lambda i, _c=col, _k=back: (jnp.maximum(jnp.minimum(i, nb - 1) - _k, 0), _c))
    return [mk(col, back) for col in (1, 2) for back in range(NKB - 1, -1, -1)]


def _attn2_fwd(qkv, bias_w):
    s = qkv.shape[0]
    nb = s // QB

    def body(q_ref, *refs):
        krefs, vrefs = refs[:NKB], refs[NKB:2 * NKB]
        b_ref, o_ref, lse_ref, kbuf, vbuf = refs[2 * NKB:]
        i = pl.program_id(0)
        for j in range(NKB):
            kbuf[QB * j:QB * j + QB, :] = krefs[j][...].astype(BF)
            vbuf[QB * j:QB * j + QB, :] = vrefs[j][...].astype(BF)
        valid = (i * QB - (NKB - 1) * QB + lax.broadcasted_iota(jnp.int32, (1, WIN), 1)) >= 0
        lane = lax.broadcasted_iota(jnp.int32, (QB, LANE), 1)

        def heads(masked):
            lse = jnp.zeros((QB, LANE), F32)
            for h in range(NH):
                sl = slice(HD * h, HD * h + HD)
                q = (q_ref[:, sl] * SCALE).astype(BF)
                sc = lax.dot_general(q, kbuf[:, sl], _DIMS["NT"], preferred_element_type=F32) + b_ref[h]
                if masked:
                    sc = jnp.where(valid, sc, -jnp.inf)
                mx = jnp.max(sc, axis=-1, keepdims=True)
                e = jnp.exp(sc - mx)
                tot = jnp.sum(e, axis=-1, keepdims=True)
                o = jnp.dot(e.astype(BF), vbuf[:, sl], preferred_element_type=F32)
                o_ref[:, sl] = o * (1.0 / tot)
                lse = jnp.where(lane == h, mx + jnp.log(tot), lse)
            lse_ref[...] = lse

        pl.when(i < NKB - 1)(functools.partial(heads, True))
        pl.when(i >= NKB - 1)(functools.partial(heads, False))

    return pl.pallas_call(
        body, name="attn_fwd", grid=(nb,),
        in_specs=[_row(QB, AW)] + _win_specs(AW, nb) + [_fix((NH, QB, WIN), single=True)],
        out_specs=[_row(QB, AW), _row(QB, LANE)],
        out_shape=[jax.ShapeDtypeStruct((s, AW), F32), jax.ShapeDtypeStruct((s, LANE), F32)],
        scratch_shapes=[pltpu.VMEM((WIN, AW), BF), pltpu.VMEM((WIN, AW), BF)],
        compiler_params=_cp("parallel"))(*[qkv] * (1 + 2 * NKB), bias_w)


def _attn2_bwd(qkv, datt, lse_t, bias_wt):
    s = qkv.shape[0]
    nb = s // QB
    cur = lambda i: jnp.minimum(i, nb - 1)
    late = lambda i: jnp.maximum(i - (NKB - 1), 0)

    def body(q_ref, *refs):
        krefs, vrefs = refs[:NKB], refs[NKB:2 * NKB]
        do_ref, lse_ref, b_ref, dq_ref, dk_ref, dv_ref, db_ref, kbuf, vbuf, dkacc, dvacc, dsbuf = refs[2 * NKB:]
        i = pl.program_id(0)

        @pl.when(i == 0)
        def _():
            dkacc[...] = jnp.zeros_like(dkacc)
            dvacc[...] = jnp.zeros_like(dvacc)
            db_ref[...] = jnp.zeros_like(db_ref)

        def heads(masked):
            for j in range(NKB):
                kbuf[QB * j:QB * j + QB, :] = krefs[j][...].astype(BF)
                vbuf[QB * j:QB * j + QB, :] = vrefs[j][...].astype(BF)
            valid = (i * QB - (NKB - 1) * QB + lax.broadcasted_iota(jnp.int32, (WIN, 1), 0)) >= 0
            for h in range(NH):
                sl = slice(HD * h, HD * h + HD)
                q = (q_ref[:, sl] * SCALE).astype(BF)
                k = kbuf[:, sl]
                do = do_ref[:, sl].astype(BF)
                st = lax.dot_general(k, q, (((1,), (1,)), ((), ())), preferred_element_type=F32) + b_ref[h]
                if masked:
                    st = jnp.where(valid, st, -jnp.inf)
                pt = jnp.exp(st - lse_ref[h:h + 1, :])
                dvacc[:, sl] += jnp.dot(pt.astype(BF), do, preferred_element_type=F32)
                dpt = lax.dot_general(vbuf[:, sl], do, (((1,), (1,)), ((), ())), preferred_element_type=F32)
                dst = pt * (dpt - jnp.sum(dpt * pt, axis=0, keepdims=True))
                dsbuf[...] = dst
                fold = dsbuf[0:BAND, 0:CH]
                for a in range(1, NQC):
                    fold = fold + dsbuf[CH * a:CH * a + BAND, CH * a:CH * a + CH]
                db_ref[h] += fold
                dsb = dst.astype(BF)
                dkacc[:, sl] += jnp.dot(dsb, q, preferred_element_type=F32)
                dq = lax.dot_general(dsb, k, (((0,), (0,)), ((), ())), preferred_element_type=F32) * SCALE
                dq_ref[:, sl] = dq.astype(BF)

        pl.when(i < nb)(functools.partial(heads, True))

        dk_ref[...] = dkacc[0:QB, :].astype(BF)
        dv_ref[...] = dvacc[0:QB, :].astype(BF)
        for acc in (dkacc, dvacc):
            for j in range(NKB - 1):
                acc[QB * j:QB * j + QB, :] = acc[QB * j + QB:QB * j + 2 * QB, :]
            acc[WIN - QB:WIN, :] = jnp.zeros((QB, AW), F32)

    qspec = pl.BlockSpec((QB, AW), lambda i: (cur(i), 0))
    lspec = pl.BlockSpec((QB, AW), lambda i: (late(i), 0))
    return pl.pallas_call(
        body, name="attn_bwd", grid=(nb + NKB - 1,),
        in_specs=[qspec] + _win_specs(AW, nb) + [qspec, pl.BlockSpec((NH, QB), lambda i: (0, cur(i))),
                                                 _fix((NH, WIN, QB), single=True)],
        out_specs=[qspec, lspec, lspec, _fix((NH, BAND, CH))],
        out_shape=[jax.ShapeDtypeStruct((s, AW), BF)] * 3 + [jax.ShapeDtypeStruct((NH, BAND, CH), F32)],
        scratch_shapes=[pltpu.VMEM((WIN, AW), BF), pltpu.VMEM((WIN, AW), BF),
                        pltpu.VMEM((WIN, AW), F32), pltpu.VMEM((WIN, AW), F32), pltpu.VMEM((WIN, QB), F32)],
        compiler_params=_cp("arbitrary"))(*[qkv] * (1 + 2 * NKB), datt, lse_t, bias_wt)


TB = 512
SCH = 512
NCB = TB // SCH
GW = 8 * HD


def _tri(lower):
    r = lax.broadcasted_iota(jnp.int32, (SCH, SCH), 0)
    c = lax.broadcasted_iota(jnp.int32, (SCH, SCH), 1)
    return r >= c if lower else r <= c


def _ssd_fwd(proj, conv_w, conv_b, dt_bias, a_log, d_skip):
    s = proj.shape[0]
    nb = s // TB

    def body(xs_ref, bc_ref, dt_ref, cw_ref, cb_ref, dtb_ref, al_ref, dsk_ref, y_ref, pre_ref, prev_ref,
             xpad, ubuf, dtbuf, csb, cstb, hst, xdec, yoffb, drow):
        i = pl.program_id(0)

        @pl.when(i == 0)
        def _():
            xpad[0:8, :] = jnp.zeros((8, XBC), F32)
            hst[...] = jnp.zeros_like(hst)

        xpad[8:8 + TB, 0:SW] = xs_ref[...]
        xpad[8:8 + TB, SW:XBC] = bc_ref[...]
        pre = cb_ref[...]
        for kk in range(4):
            pre = pre + cw_ref[kk:kk + 1, :] * xpad[5 + kk:5 + kk + TB, :]
        pre_ref[...] = pre
        ubuf[...] = pre * jax.nn.sigmoid(pre)
        xpad[0:8, :] = xpad[TB:TB + 8, :]
        dtbuf[...] = _softplus(dt_ref[...] + dtb_ref[...])
        a = -jnp.exp(al_ref[...])
        tri = _tri(True).astype(F32)
        causal = _tri(True)

        def chunk(c, carry):
            r0 = pl.multiple_of(c * SCH, SCH)
            rows = pl.ds(r0, SCH)
            cs = jnp.dot(tri, dtbuf[rows, :] * a, precision=HI, preferred_element_type=F32)
            csb[...] = cs
            cstb[...] = cs.T
            for g in range(NG):
                bg = ubuf[rows, SW + NS * g:SW + NS * g + NS]
                cg = ubuf[rows, SW + NG * NS + NS * g:SW + NG * NS + NS * g + NS].astype(BF)
                cb = lax.dot_general(cg, bg.astype(BF), (((1,), (1,)), ((), ())), preferred_element_type=F32)
                hg = hst[g]
                prev_ref[c, g] = hg
                yoffb[...] = jnp.dot(cg, hg.astype(BF), preferred_element_type=F32)
                for r in range(8):
                    h = 8 * g + r
                    sl = slice(HD * h, HD * h + HD)
                    rs = slice(HD * r, HD * r + HD)
                    cs_h = csb[:, h:h + 1]
                    cl = csb[SCH - 1:SCH, h:h + 1]
                    seg = jnp.exp(jnp.where(causal, cs_h - cstb[h:h + 1, :], -jnp.inf))
                    xh = ubuf[rows, sl]
                    xdt = xh * dtbuf[rows, h:h + 1]
                    yd = jnp.dot((cb * seg).astype(BF), xdt.astype(BF), preferred_element_type=F32)
                    y_ref[rows, sl] = yd + jnp.exp(cs_h) * yoffb[:, rs] + xh * dsk_ref[:, h:h + 1]
                    xdec[:, rs] = xdt * jnp.exp(cl - cs_h)
                    drow[:, rs] = jnp.broadcast_to(jnp.exp(cl), (1, HD))
                st = jnp.dot(bg.T.astype(BF), xdec[...].astype(BF), preferred_element_type=F32)
                hst[g] = hg * drow[...] + st
            return carry

        lax.fori_loop(0, NCB, chunk, 0)

    return pl.pallas_call(
        body, name="ssd_fwd", grid=(nb,),
        in_specs=[_row(TB, SW, 4), _row(TB, 512, 10), _row(TB, LANE, 44), _fix((4, XBC)), _fix((1, XBC)),
                  _fix((1, LANE)), _fix((1, LANE)), _fix((1, LANE))],
        out_specs=[_row(TB, SW), _row(TB, XBC), pl.BlockSpec((NCB, NG, NS, GW), lambda i: (i, 0, 0, 0))],
        out_shape=[jax.ShapeDtypeStruct((s, SW), F32), jax.ShapeDtypeStruct((s, XBC), F32),
                   jax.ShapeDtypeStruct((s // SCH, NG, NS, GW), F32)],
        scratch_shapes=[pltpu.VMEM((TB + 8, XBC), F32), pltpu.VMEM((TB, XBC), F32), pltpu.VMEM((TB, LANE), F32),
                        pltpu.VMEM((SCH, LANE), F32), pltpu.VMEM((LANE, SCH), F32), pltpu.VMEM((NG, NS, GW), F32),
                        pltpu.VMEM((SCH, GW), F32), pltpu.VMEM((SCH, GW), F32), pltpu.VMEM((1, GW), F32)],
        compiler_params=_cp("arbitrary"))(proj, proj, proj, conv_w, conv_b, dt_bias, a_log, d_skip)


def _ssd_bwd(dy, pre, proj, prev, conv_w, dt_bias, a_log, d_skip):
    s = dy.shape[0]
    nb = s // TB
    rev = lambda i: nb - 1 - i
    halo = lambda i: jnp.maximum((nb - 1 - i) * (TB // 8) - 1, 0)

    def body(dy_ref, pre_ref, dt_ref, prev_ref, xs_ref, bc_ref, xsh_ref, bch_ref, cw_ref, dtb_ref, al_ref, dsk_ref,
             dx_ref, ddt_ref, pw_ref, ph_ref,
             ubuf, dtbuf, dpad, xpad, csb, cstb, dhs, yoffb, ebuf, xdecb, wbuf, drow, dub):
        i = pl.program_id(0)

        @pl.when(i == 0)
        def _():
            dhs[...] = jnp.zeros_like(dhs)
            dpad[TB:TB + 8, :] = jnp.zeros((8, XBC), F32)
            pw_ref[...] = jnp.zeros_like(pw_ref)
            ph_ref[...] = jnp.zeros_like(ph_ref)

        pre = pre_ref[...]
        ubuf[...] = pre * jax.nn.sigmoid(pre)
        dtbuf[...] = _softplus(dt_ref[...] + dtb_ref[...])
        a = -jnp.exp(al_ref[...])
        tri = _tri(True).astype(F32)
        trit = _tri(False).astype(F32)
        causal = _tri(True)
        rowid = lax.broadcasted_iota(jnp.int32, (LANE, SCH), 0)
        lane = lax.broadcasted_iota(jnp.int32, (SCH, LANE), 1)
        lane1 = lax.broadcasted_iota(jnp.int32, (1, LANE), 1)
        lastrow = lax.broadcasted_iota(jnp.int32, (SCH, LANE), 0) == SCH - 1

        def chunk(cc, carry):
            c = NCB - 1 - cc
            r0 = pl.multiple_of(c * SCH, SCH)
            rows = pl.ds(r0, SCH)
            dtc = dtbuf[rows, :]
            cs = jnp.dot(tri, dtc * a, precision=HI, preferred_element_type=F32)
            csb[...] = cs
            cstb[...] = cs.T
            dcs = jnp.zeros((SCH, LANE), F32)
            dcst = jnp.zeros((LANE, SCH), F32)
            xr = jnp.zeros((SCH, LANE), F32)
            dlast = jnp.zeros((1, LANE), F32)
            dsk = jnp.zeros((1, LANE), F32)
            for g in range(NG):
                bg = ubuf[rows, SW + NS * g:SW + NS * g + NS]
                cg = ubuf[rows, SW + NG * NS + NS * g:SW + NG * NS + NS * g + NS]
                bb = bg.astype(BF)
                cbf = cg.astype(BF)
                cb = lax.dot_general(cbf, bb, (((1,), (1,)), ((), ())), preferred_element_type=F32)
                hp = prev_ref[c, g]
                hpb = hp.astype(BF)
                dhg = dhs[g]
                dhb = dhg.astype(BF)
                yoffb[...] = jnp.dot(cbf, hpb, preferred_element_type=F32)
                for r in range(8):
                    h = 8 * g + r
                    sl = slice(HD * h, HD * h + HD)
                    rs = slice(HD * r, HD * r + HD)
                    cs_h = csb[:, h:h + 1]
                    cl = csb[SCH - 1:SCH, h:h + 1]
                    dyh = dy_ref[rows, sl]
                    ecs = jnp.exp(cs_h)
                    ebuf[:, rs] = dyh * ecs
                    xdecb[:, rs] = ubuf[rows, sl] * dtbuf[rows, h:h + 1] * jnp.exp(cl - cs_h)
                    t1 = jnp.sum(dyh * ecs * yoffb[:, rs], axis=1, keepdims=True)
                    dcs = dcs + jnp.where(lane == h, t1, 0.0)
                    drow[:, rs] = jnp.broadcast_to(jnp.exp(cl), (1, HD))
                eb = ebuf[...].astype(BF)
                dc = lax.dot_general(eb, hpb, (((1,), (1,)), ((), ())), preferred_element_type=F32)
                dprev = jnp.dot(cg.T.astype(BF), eb, preferred_element_type=F32)
                wbuf[...] = jnp.dot(bb, dhb, preferred_element_type=F32)
                db = lax.dot_general(xdecb[...].astype(BF), dhb, (((1,), (1,)), ((), ())), preferred_element_type=F32)
                dcbs = jnp.zeros((SCH, SCH), F32)
                for r in range(8):
                    h = 8 * g + r
                    sl = slice(HD * h, HD * h + HD)
                    rs = slice(HD * r, HD * r + HD)
                    cs_h = csb[:, h:h + 1]
                    cst_h = cstb[h:h + 1, :]
                    cl = csb[SCH - 1:SCH, h:h + 1]
                    seg = jnp.exp(jnp.where(causal, cs_h - cst_h, -jnp.inf))
                    dyh = dy_ref[rows, sl]
                    xh = ubuf[rows, sl]
                    dt_h = dtbuf[rows, h:h + 1]
                    xdt = xh * dt_h
                    dyb = dyh.astype(BF)
                    dm = lax.dot_general(dyb, xdt.astype(BF), _DIMS["NT"], preferred_element_type=F32)
                    mm_ = cb * seg
                    dxdt = lax.dot_general(mm_.astype(BF), dyb, _DIMS["TN"], preferred_element_type=F32)
                    gg = dm * mm_
                    dcs_h = jnp.sum(gg, axis=1, keepdims=True)
                    dcst = dcst + jnp.where(rowid == h, jnp.sum(gg, axis=0, keepdims=True), 0.0)
                    dcbs = dcbs + dm * seg
                    dec = jnp.exp(cl - cs_h)
                    w_h = wbuf[:, rs]
                    dxdt = dxdt + dec * w_h
                    t = jnp.sum(w_h * xdt, axis=1, keepdims=True) * dec
                    dcs_h = dcs_h - t
                    dch = jnp.sum(jnp.sum(dhs[g, :, rs] * prev_ref[c, g, :, rs], axis=1, keepdims=True),
                                  axis=0, keepdims=True)
                    dl = jnp.sum(t, axis=0, keepdims=True) + dch * jnp.exp(cl)
                    dcs = dcs + jnp.where(lane == h, dcs_h, 0.0)
                    dlast = dlast + jnp.where(lane1 == h, dl, 0.0)
                    xr = xr + jnp.where(lane == h, jnp.sum(dxdt * xh, axis=1, keepdims=True), 0.0)
                    dskh = jnp.sum(jnp.sum(dyh * xh, axis=1, keepdims=True), axis=0, keepdims=True)
                    dsk = dsk + jnp.where(lane1 == h, dskh, 0.0)
                    dub[:, sl] = dyh * dsk_ref[:, h:h + 1] + dxdt * dt_h
                dc = dc + jnp.dot(dcbs.astype(BF), bb, preferred_element_type=F32)
                db = db + lax.dot_general(dcbs.astype(BF), cbf, _DIMS["TN"], preferred_element_type=F32)
                dub[:, SW + NS * g:SW + NS * g + NS] = db
                dub[:, SW + NG * NS + NS * g:SW + NG * NS + NS * g + NS] = dc
                dhs[g] = dprev + drow[...] * dhg
            dcs = dcs - dcst.T + jnp.where(lastrow, dlast, 0.0)
            dadt = jnp.dot(trit, dcs, precision=HI, preferred_element_type=F32)
            ddt = dadt * a + xr
            ddtr = ddt * jax.nn.sigmoid(dt_ref[rows, :] + dtb_ref[...])
            ddt_ref[rows, :] = ddtr.astype(BF)
            ph_ref[0:1, :] += jnp.sum(ddtr, axis=0, keepdims=True)
            ph_ref[1:2, :] += jnp.sum(dadt * dtc, axis=0, keepdims=True) * a
            ph_ref[2:3, :] += dsk
            pc = pre_ref[rows, :]
            sg = jax.nn.sigmoid(pc)
            dpad[rows, :] = dub[...] * _silu_grad(pc, sg)
            return carry

        lax.fori_loop(0, NCB, chunk, 0)

        dxb = cw_ref[0:1, :] * dpad[3:3 + TB, :]
        for kk in range(1, 4):
            dxb = dxb + cw_ref[kk:kk + 1, :] * dpad[3 - kk:3 - kk + TB, :]
        dx_ref[...] = dxb.astype(BF)
        keep = jnp.where(i < nb - 1, 1.0, 0.0)
        xpad[0:8, 0:SW] = xsh_ref[...] * keep
        xpad[0:8, SW:XBC] = bch_ref[...] * keep
        xpad[8:8 + TB, 0:SW] = xs_ref[...]
        xpad[8:8 + TB, SW:XBC] = bc_ref[...]
        dp = dpad[0:TB, :]
        for kk in range(4):
            pw_ref[kk:kk + 1, :] += jnp.sum(dp * xpad[5 + kk:5 + kk + TB, :], axis=0, keepdims=True)
        pw_ref[4:5, :] += jnp.sum(dp, axis=0, keepdims=True)
        dpad[TB:TB + 8, :] = dpad[0:8, :]

    rblk = lambda w, col: pl.BlockSpec((TB, w), lambda i, _c=col: (rev(i), _c))
    return pl.pallas_call(
        body, name="ssd_bwd", grid=(nb,),
        in_specs=[rblk(SW, 0), rblk(XBC, 0), rblk(LANE, 44),
                  pl.BlockSpec((NCB, NG, NS, GW), lambda i: (rev(i), 0, 0, 0)),
                  rblk(SW, 4), rblk(512, 10),
                  pl.BlockSpec((8, SW), lambda i: (halo(i), 4)), pl.BlockSpec((8, 512), lambda i: (halo(i), 10)),
                  _fix((4, XBC)), _fix((1, LANE)), _fix((1, LANE)), _fix((1, LANE))],
        out_specs=[rblk(XBC, 0), rblk(LANE, 0), _fix((8, XBC)), _fix((8, LANE))],
        out_shape=[jax.ShapeDtypeStruct((s, XBC), BF), jax.ShapeDtypeStruct((s, LANE), BF),
                   jax.ShapeDtypeStruct((8, XBC), F32), jax.ShapeDtypeStruct((8, LANE), F32)],
        scratch_shapes=[pltpu.VMEM((TB, XBC), F32), pltpu.VMEM((TB, LANE), F32), pltpu.VMEM((TB + 8, XBC), F32),
                        pltpu.VMEM((TB + 8, XBC), F32), pltpu.VMEM((SCH, LANE), F32), pltpu.VMEM((LANE, SCH), F32),
                        pltpu.VMEM((NG, NS, GW), F32), pltpu.VMEM((SCH, GW), F32), pltpu.VMEM((SCH, GW), F32),
                        pltpu.VMEM((SCH, GW), F32), pltpu.VMEM((SCH, GW), F32), pltpu.VMEM((1, GW), F32),
                        pltpu.VMEM((SCH, XBC), F32)],
        compiler_params=_cp("arbitrary"))(dy, pre, proj, prev, proj, proj, proj, proj, conv_w, dt_bias, a_log, d_skip)


def _rel_index():
    q = jnp.arange(CH)[:, None] + (BAND - CH)
    k = jnp.arange(BAND)[None, :]
    return jnp.clip(q - k, -(CH - 1), REL_CLIP) + (CH - 1)


def _local_step(x, tgt, mods, first_weights, late_weights, grads_ready,
                g_mix, rel_bias, conv_w, conv_b, dt_bias, a_log, d_skip, g_att, g_ssd, g_ffn, g_final):
    sh1, sc1, gt1, sh2, sc2, gt2 = mods
    pad16 = lambda v: jnp.pad(v, ((0, 0), (0, LANE - NH)))
    dtb_p, al_p, dsk_p = pad16(dt_bias), pad16(a_log), pad16(d_skip)
    ridx = _rel_index()
    ext = jnp.concatenate([jnp.broadcast_to(rel_bias[:, NREL - 1:], (NH, BAND + CH - 1 - NREL)),
                           rel_bias[:, ::-1]], axis=1)
    skew = jnp.tile(jnp.pad(ext, ((0, 0), (0, 1))), (1, CH + 1))[:, :CH * (BAND + CH + 1)]
    bias = skew.reshape(NH, CH, BAND + CH + 1)[:, ::-1, :BAND]
    bias_w = _bias_window(bias)
    bias_wt = jnp.swapaxes(bias_w, 1, 2)

    h1 = _norm_fwd(x, g_mix, sc1, sh1, name="norm1_fwd")
    onehot = (ridx.T.reshape(BAND * CH, 1) == jnp.arange(NREL)[None, :]).astype(F32)
    w_in_t = first_weights([h1, bias_w, bias_wt, onehot])
    proj = _mm([(h1, w_in_t)], "NT", F32, "in_proj", 1024, 1152, 2048)
    att, lse = _attn2_fwd(proj, bias_w)
    y, pre, prev = _ssd_fwd(proj, conv_w, conv_b, dtb_p, al_p, dsk_p)
    cat = _mixout_fwd(att, y, proj, g_att, g_ssd)
    w_out, w_gate_t, w_up_t, w_down = late_weights(cat)
    mix = _mm([(cat, w_out)], "NN", F32, "out_proj", 1024, 1024, 2048)
    h2, x1 = _norm_fwd(x, g_ffn, sc2, sh2, mix, gt1, name="norm2_fwd")
    gate, up, act = _gate_up(h2, w_gate_t, w_up_t)
    ffn = _mm([(act, w_down)], "NN", F32, "down_proj", 512, 1024, FF)
    dx2, dffn, pf = _final(x1, ffn, gt2, g_final, tgt)

    gw_down = _mm([(act, dffn)], "TN", BF, "gw_down", 1408, 2048, 1024)
    dgate, dup = _gate_up_bwd(dffn, w_down, gate, up)
    dh2 = _mm([(dgate, w_gate_t), (dup, w_up_t)], "NN", F32, "d_h2", 512, 512, FF)
    gw_gate_t = _mm([(dgate, h2)], "TN", BF, "gw_gate", 1408, 2048, 1024)
    gw_up_t = _mm([(dup, h2)], "TN", BF, "gw_up", 1408, 2048, 1024)
    tok = grads_ready("ffn", gw_gate_t, gw_up_t, gw_down)
    dx1, dmix, p2 = _norm_bwd(x1, g_ffn + tok[0, 0], sc2, sh2, dh2, dx2, mix, gt1, name="norm2_bwd")
    gw_out = _mm([(cat, dmix)], "TN", BF, "gw_out", 1024, 2048, 1024)
    tok = grads_ready("out", gw_out)
    dcat = _mm([(dmix, w_out)], "NT", F32, "d_cat", 1024, 1024, 2048, after=tok)
    datt, dy, dz, pm = _mixout_bwd(dcat, att, y, proj, g_att, g_ssd)
    dq, dk, dv, dbias_t = _attn2_bwd(proj, datt, lse[:, :NH].T, bias_wt)
    dxbc, ddt, pw, ph = _ssd_bwd(dy, pre, proj, prev, conv_w, dtb_p, al_p, dsk_p)
    dproj = jnp.concatenate([dq, dk, dv, dz, dxbc, ddt], axis=1)
    gw_in_t = _mm([(dproj, h1)], "TN", BF, "gw_in", 1152, 2048, 1024)
    tok = grads_ready("in", gw_in_t)
    dh1 = _mm([(dproj, w_in_t)], "NN", F32, "d_h1", 512, 1024, INP, after=tok)
    gx, p1 = _norm_bwd(x, g_mix, sc1, sh1, dh1, dx1, name="norm1_bwd")
    g_rel =_mm([(dbias_t.reshape(NH, BAND * CH), onehot)], "NN", F32, "g_rel_bias", NH, NREL, 4608, precision=HI)

    small = dict(
        dmods=jnp.concatenate([p1[1], p1[0], p2[3], p2[1], p2[0], pf[1]]),
        g_mix=p1[2], conv_b=pw[4], dt_bias=ph[0], a_log=ph[1], d_skip=ph[2], g_att=pm[0], g_ssd=pm[1],
        g_ffn=p2[2], g_final=pf[0], rel_bias=g_rel.reshape(-1), conv_w=pw[0:4].reshape(-1), loss=pf[2])
    return gx, small


def _exchange(srcs, gather, name):
    n = len(srcs)
    outs = [jax.ShapeDtypeStruct((NDEV,) + (s.shape if g else s.shape[1:]), s.dtype) for s, g in zip(srcs, gather)]
    outs.append(jax.ShapeDtypeStruct((8, LANE), F32))

    def body(*refs):
        src, dst, token = refs[:n], refs[n:2 * n], refs[2 * n]
        ssem, rsem, lsem = refs[2 * n + 1:]
        token[...] = jnp.zeros_like(token)
        x, y, c = lax.axis_index("x"), lax.axis_index("y"), lax.axis_index("c")
        me = 4 * x + 2 * y + c
        local = []
        for a in range(n):
            own = src[a] if gather[a] else src[a].at[me]
            cp = pltpu.make_async_copy(own, dst[a].at[me], lsem.at[a])
            cp.start()
            local.append(cp)
        remote = []
        for k in range(1, NDEV):
            px = 1 - x if k & 4 else x
            py = 1 - y if k & 2 else y
            pc = 1 - c if k & 1 else c
            pid = 4 * px + 2 * py + pc
            for a in range(n):
                s_ref = src[a] if gather[a] else src[a].at[pid]
                cp = pltpu.make_async_remote_copy(
                    src_ref=s_ref, dst_ref=dst[a].at[me], send_sem=ssem.at[a * (NDEV - 1) + k - 1],
                    recv_sem=rsem.at[a * (NDEV - 1) + k - 1], device_id=(px, py, pc), device_id_type=MESH)
                cp.start()
                remote.append(cp)
        for cp in remote:
            cp.wait()
        for cp in local:
            cp.wait()

    anyspec = pl.BlockSpec(memory_space=pl.ANY)
    return pl.pallas_call(
        body, name=name, out_shape=outs, in_specs=[anyspec] * n,
        out_specs=[anyspec] * n + [pl.BlockSpec(memory_space=pltpu.VMEM)],
        scratch_shapes=[pltpu.SemaphoreType.DMA((n * (NDEV - 1),)), pltpu.SemaphoreType.DMA((n * (NDEV - 1),)),
                        pltpu.SemaphoreType.DMA((n,))],
    )(*srcs)


_HBM = pl.BlockSpec(memory_space=pltpu.HBM)
_SEM = pl.BlockSpec(memory_space=pltpu.SEMAPHORE)
_EFFECT = pltpu.SideEffectType.DATAFLOW_SIDE_EFFECTING


def _peer_copies(src, land, ssem, rsem, gather):
    x, y, c = lax.axis_index("x"), lax.axis_index("y"), lax.axis_index("c")
    me = 4 * x + 2 * y + c
    copies = []
    for a in range(len(src)):
        for k in range(1, NDEV):
            px = 1 - x if k & 4 else x
            py = 1 - y if k & 2 else y
            pc = 1 - c if k & 1 else c
            s_ref = src[a] if gather[a] else src[a].at[4 * px + 2 * py + pc]
            idx = a * (NDEV - 1) + k - 1
            copies.append(pltpu.make_async_remote_copy(
                src_ref=s_ref, dst_ref=land[a].at[me], send_sem=ssem.at[idx], recv_sem=rsem.at[idx],
                device_id=(px, py, pc), device_id_type=MESH))
    return copies


def _xstart(srcs, gather, name):
    n = len(srcs)
    me = 4 * lax.axis_index("x") + 2 * lax.axis_index("y") + lax.axis_index("c")
    lands = []
    for s_, g_ in zip(srcs, gather):
        own = s_ if g_ else lax.dynamic_index_in_dim(s_, me, 0, keepdims=False)
        empty = lax.empty((NDEV,) + own.shape, own.dtype)
        lands.append(lax.dynamic_update_slice(empty, own[None], (me,) + (0,) * own.ndim))
    nsem = n * (NDEV - 1)

    def body(*refs):
        src, land, ssem, rsem, token = refs[:n], refs[n:2 * n], refs[2 * n], refs[2 * n + 1], refs[-1]
        for cp in _peer_copies(src, land, ssem, rsem, gather):
            cp.start()
        token[...] = jnp.zeros_like(token)

    ops = [pltpu.with_memory_space_constraint(a_, pltpu.HBM) for a_ in list(srcs) + lands]
    return pl.pallas_call(
        body, name=name,
        out_shape=(pltpu.SemaphoreType.DMA((nsem,)), pltpu.SemaphoreType.DMA((nsem,)),
                   *[pltpu.HBM(a_.shape, a_.dtype) for a_ in ops], jax.ShapeDtypeStruct((8, LANE), F32)),
        in_specs=[_HBM] * (2 * n),
        out_specs=(_SEM, _SEM, *[_HBM] * (2 * n), pl.BlockSpec(memory_space=pltpu.VMEM)),
        input_output_aliases={i: 2 + i for i in range(2 * n)},
        compiler_params=pltpu.CompilerParams(has_side_effects=_EFFECT),
    )(*ops)


def _xwait(started, gather, after, name):
    n = len(gather)
    ssem, rsem = started[0], started[1]
    bufs = started[2:2 + 2 * n]
    after = list(after) if isinstance(after, (list, tuple)) else [after]

    def body(*refs):
        src, land, ssem_, rsem_ = refs[:n], refs[n:2 * n], refs[2 * n], refs[2 * n + 1]
        for cp in _peer_copies(src, land, ssem_, rsem_, gather):
            cp.wait_send()
            cp.wait_recv()

    out = pl.pallas_call(
        body, name=name, out_shape=tuple(pltpu.HBM(a_.shape, a_.dtype) for a_ in bufs),
        in_specs=[_HBM] * (2 * n) + [_SEM, _SEM] + [pl.BlockSpec(memory_space=pl.ANY)] * len(after),
        out_specs=[_HBM] * (2 * n), input_output_aliases={i: i for i in range(2 * n)},
        compiler_params=pltpu.CompilerParams(has_side_effects=_EFFECT),
    )(*bufs, ssem, rsem, *after)
    return out[n:]


def _mods(c16, w_shard, b_shard, *, tn=512):
    n = w_shard.shape[1]

    def body(c_ref, w_ref, b_ref, o_ref):
        cv = c_ref[...]
        cond = (cv * jax.nn.sigmoid(cv)).astype(BF)
        o_ref[...] = jnp.dot(cond, w_ref[...].astype(BF), preferred_element_type=F32) + b_ref[...]

    return pl.pallas_call(body, name="ada_mods", grid=(n // tn,),
                          in_specs=[_fix((16, D)), pl.BlockSpec((D, tn), lambda j: (0, j)),
                                    pl.BlockSpec((1, tn), lambda j: (0, j))],
                          out_specs=pl.BlockSpec((16, tn), lambda j: (0, j)),
                          out_shape=jax.ShapeDtypeStruct((16, n), F32), compiler_params=_cp("parallel"))(c16, w_shard, b_shard)


def _gw_ada(c_t, dm, *, tr=256):
    n = dm.shape[1]

    def body(ct_ref, dm_ref, o_ref):
        acc = jnp.zeros((tr, n), F32)
        for b in range(NDEV):
            cv = ct_ref[:, b:b + 1]
            cond = (cv * jax.nn.sigmoid(cv)).astype(BF).astype(F32)
            acc = acc + cond * dm_ref[b:b + 1, :].astype(BF).astype(F32)
        o_ref[...] = acc

    return pl.pallas_call(body, name="gw_ada", grid=(D // tr,),
                          in_specs=[_row(tr, NDEV), _fix((NDEV, n))], out_specs=_row(tr, n),
                          out_shape=jax.ShapeDtypeStruct((D, n), F32), compiler_params=_cp("parallel"))(c_t, dm)


def _sum8(parts, *, name, tc=512):
    _, r, c = parts.shape
    tc = min(tc, c)

    def body(p_ref, o_ref):
        acc = p_ref[0].astype(F32)
        for d in range(1, NDEV):
            acc = acc + p_ref[d].astype(F32)
        o_ref[...] = acc

    return pl.pallas_call(body, name=name, grid=(c // tc,),
                          in_specs=[pl.BlockSpec((NDEV, r, tc), lambda j: (0, 0, j))],
                          out_specs=pl.BlockSpec((r, tc), lambda j: (0, j)),
                          out_shape=jax.ShapeDtypeStruct((r, c), F32), compiler_params=_cp("parallel"))(parts)


def _adam(w, m, v, *, recv=None, grad=None, tr=None, tc=None, name):
    lead = w.ndim == 3
    r, c = w.shape[-2:]
    summed = recv is not None
    assert tc is not None or r % tr == 0
    at = (0,) if lead else (Ellipsis,)

    def body(g_ref, w_ref, m_ref, v_ref, go_ref, do_ref, mo_ref, vo_ref):
        if summed:
            g = g_ref[0].astype(F32)
            for d in range(1, NDEV):
                g = g + g_ref[d].astype(F32)
        else:
            g = g_ref[...]
        go_ref[at] = g
        mn = B1 * m_ref[at] + (1.0 - B1) * g
        vn = B2 * v_ref[at] + (1.0 - B2) * (g * g)
        mo_ref[at] = mn
        vo_ref[at] = vn
        mh = mn / (1.0 - B1 ** STEP)
        vh = vn / (1.0 - B2 ** STEP)
        do_ref[at] = -LR * (mh / (jnp.sqrt(vh) + AEPS) + WD * w_ref[at])

    if tc is not None:
        assert not lead and c % tc == 0
        gspec = (pl.BlockSpec((NDEV, r, tc), lambda j: (0, 0, j)) if summed
                 else pl.BlockSpec((r, tc), lambda j: (0, j)))
        wspec = pl.BlockSpec((r, tc), lambda j: (0, j))
        return pl.pallas_call(body, name=name, grid=(c // tc,),
                              in_specs=[gspec, wspec, wspec, wspec], out_specs=[wspec] * 4,
                              out_shape=[jax.ShapeDtypeStruct(w.shape, F32)] * 4,
                              compiler_params=_cp("parallel"))(recv if summed else grad, w, m, v)
    gspec = pl.BlockSpec((NDEV, tr, c), lambda i: (0, i, 0)) if summed else _row(tr, c)
    wspec = pl.BlockSpec((1, tr, c), lambda i: (0, i, 0)) if lead else _row(tr, c)
    return pl.pallas_call(body, name=name, grid=(r // tr,),
                          in_specs=[gspec, wspec, wspec, wspec], out_specs=[wspec] * 4,
                          out_shape=[jax.ShapeDtypeStruct(w.shape, F32)] * 4,
                          compiler_params=_cp("parallel"))(recv if summed else grad, w, m, v)


def _cols_to_blocks(g, width):
    r = g.shape[0]
    return jnp.transpose(g.reshape(r, NDEV, width), (1, 0, 2))


def _blocks_to_cols(b):
    _, r, width = b.shape
    return jnp.transpose(b, (1, 0, 2)).reshape(r, NDEV * width)


_SMALL = ("b_ada", "g_mix", "rel_bias", "conv_w", "conv_b", "dt_bias", "a_log", "d_skip",
          "g_att_out", "g_ssd_out", "g_ffn", "g_final")


def _pad_rows(v):
    v = v.reshape(-1)
    return jnp.pad(v, (0, (-v.shape[0]) % LANE))


def kernel(x, c, w_ada, b_ada, g_mix, w_in, rel_bias, conv_w, conv_b, dt_bias, a_log, d_skip, g_att_out, g_ssd_out, w_out, g_ffn, w_gate, w_up, w_down, g_final, loss_target, m_w_ada, m_b_ada, m_g_mix, m_w_in, m_rel_bias, m_conv_w, m_conv_b, m_dt_bias, m_a_log, m_d_skip, m_g_att_out, m_g_ssd_out, m_w_out, m_g_ffn, m_w_gate, m_w_up, m_w_down, m_g_final, v_w_ada, v_b_ada, v_g_mix, v_w_in, v_rel_bias, v_conv_w, v_conv_b, v_dt_bias, v_a_log, v_d_skip, v_g_att_out, v_g_ssd_out, v_w_out, v_g_ffn, v_w_gate, v_w_up, v_w_down, v_g_final):
    me = 4 * lax.axis_index("x") + 2 * lax.axis_index("y") + lax.axis_index("c")
    wts = dict(w_ada=w_ada, b_ada=b_ada, g_mix=g_mix, w_in=w_in, rel_bias=rel_bias, conv_w=conv_w, conv_b=conv_b,
               dt_bias=dt_bias, a_log=a_log, d_skip=d_skip, g_att_out=g_att_out, g_ssd_out=g_ssd_out, w_out=w_out,
               g_ffn=g_ffn, w_gate=w_gate, w_up=w_up, w_down=w_down, g_final=g_final)
    mom = dict(w_ada=m_w_ada, b_ada=m_b_ada, g_mix=m_g_mix, w_in=m_w_in, rel_bias=m_rel_bias, conv_w=m_conv_w,
               conv_b=m_conv_b, dt_bias=m_dt_bias, a_log=m_a_log, d_skip=m_d_skip, g_att_out=m_g_att_out,
               g_ssd_out=m_g_ssd_out, w_out=m_w_out, g_ffn=m_g_ffn, w_gate=m_w_gate, w_up=m_w_up, w_down=m_w_down,
               g_final=m_g_final)
    var = dict(w_ada=v_w_ada, b_ada=v_b_ada, g_mix=v_g_mix, w_in=v_w_in, rel_bias=v_rel_bias, conv_w=v_conv_w,
               conv_b=v_conv_b, dt_bias=v_dt_bias, a_log=v_a_log, d_skip=v_d_skip, g_att_out=v_g_att_out,
               g_ssd_out=v_g_ssd_out, w_out=v_w_out, g_ffn=v_g_ffn, w_gate=v_w_gate, w_up=v_w_up, w_down=v_w_down,
               g_final=v_g_final)
    order = ("w_ada", "b_ada", "g_mix", "w_in", "rel_bias", "conv_w", "conv_b", "dt_bias", "a_log", "d_skip",
             "g_att_out", "g_ssd_out", "w_out", "g_ffn", "w_gate", "w_up", "w_down", "g_final")

    c_all, rel_all, cw_all, _ = _exchange([c, rel_bias[0], conv_w[0]], [True] * 3, "gather_small")
    c_all = c_all.reshape(NDEV, D)
    rel_full = _blocks_to_cols(rel_all)
    cw_full = _blocks_to_cols(cw_all)

    ncol = w_ada.shape[2]
    b_sh = lax.dynamic_slice(b_ada, (0, me * ncol), (1, ncol))
    mods_part = _mods(jnp.pad(c_all, ((0, 8), (0, 0))), w_ada[0], b_sh)[:NDEV]
    mods_all, tok_m = _exchange([mods_part], [True], "gather_mods")
    mods_me = lax.dynamic_index_in_dim(mods_all, me, axis=1, keepdims=False).reshape(6, 1, D)
    mods = [mods_me[i] for i in range(6)]

    st_in = _xstart([(w_in[0] + tok_m[0, 0]).T.astype(BF)], [True], "gather_w_in_start")
    st_rest = _xstart([(w_out[0] + st_in[-1][0, 0]).astype(BF), w_gate[0].T.astype(BF), w_up[0].T.astype(BF),
                       w_down[0].astype(BF)], [True] * 4, "gather_w_rest_start")

    def first_weights(after):
        (gi,) = _xwait(st_in, [True], after, "gather_w_in_wait")
        return jnp.pad(gi.reshape(INC, D), ((0, INP - INC), (0, 0)))

    def late_weights(after):
        go, gg, gu_, gd = _xwait(st_rest, [True] * 4, after, "gather_w_rest_wait")
        return go.reshape(D, D), gg.reshape(FF, D), gu_.reshape(FF, D), gd.reshape(FF, D)

    started = {}

    def grads_ready(which, *g):
        if which == "in":
            g = (g[0][:INC],)
        srcs = [g_.reshape(NDEV, g_.shape[0] // NDEV, D) for g_ in g]
        started[which] = _xstart(srcs, [False] * len(srcs), f"exchange_{which}_grads_start")
        return started[which][-1]

    gx, small = _local_step(
        x[0], loss_target[0], mods, first_weights, late_weights, grads_ready,
        g_mix + st_rest[-1][0, 0], rel_full, cw_full, conv_b, dt_bias, a_log, d_skip, g_att_out, g_ssd_out, g_ffn,
        g_final.reshape(1, D))

    loss_row = jnp.pad(jnp.sum(small["loss"]).reshape(1), (0, LANE - 1))
    pack = jnp.concatenate([small[k] for k in ("dmods", "g_mix", "conv_b", "dt_bias", "a_log", "d_skip", "g_att",
                                               "g_ssd", "g_ffn", "g_final", "rel_bias", "conv_w")] + [loss_row])
    packs, _ = _exchange([pack.reshape(-1, LANE)], [True], "gather_small_grads")
    rg, ru, rd = _xwait(started["ffn"], [False] * 3, packs, "exchange_ffn_grads_wait")
    (ro,) = _xwait(started["out"], [False], rg, "exchange_out_grads_wait")
    (ri,) = _xwait(started["in"], [False], ro, "exchange_in_grads_wait")
    big = dict(
        w_gate=[o_.T for o_ in _adam(w_gate[0].T, m_w_gate[0].T, v_w_gate[0].T, recv=rg, tc=512,
                                     name="adam_w_gate")],
        w_up=[o_.T for o_ in _adam(w_up[0].T, m_w_up[0].T, v_w_up[0].T, recv=ru, tc=512, name="adam_w_up")],
        w_down=_adam(w_down[0], m_w_down[0], v_w_down[0], recv=rd, tr=176, name="adam_w_down"))
    tot = _sum8(packs, name="sum_small").reshape(-1)

    sizes = dict(dmods=6 * D, g_mix=D, conv_b=XBC, dt_bias=LANE, a_log=LANE, d_skip=LANE, g_att_out=AW, g_ssd_out=SW,
                 g_ffn=D, g_final=D, rel_bias=NH * NREL, conv_w=4 * XBC, loss=LANE)
    off, o = {}, 0
    for k_, n_ in sizes.items():
        off[k_] = o
        o += n_
    take = lambda k_, n_=None: tot[off[k_]:off[k_] + (n_ or sizes[k_])]
    loss = tot[off["loss"]]
    rel_w = NREL // NDEV
    cw_w = XBC // NDEV
    gsm = dict(
        b_ada=take("dmods"), g_mix=take("g_mix"),
        rel_bias=lax.dynamic_slice(take("rel_bias").reshape(NH, NREL), (0, me * rel_w), (NH, rel_w)),
        conv_w=lax.dynamic_slice(take("conv_w").reshape(4, XBC), (0, me * cw_w), (4, cw_w)),
        conv_b=take("conv_b"), dt_bias=take("dt_bias", NH), a_log=take("a_log", NH), d_skip=take("d_skip", NH),
        g_att_out=take("g_att_out"), g_ssd_out=take("g_ssd_out"), g_ffn=take("g_ffn"), g_final=take("g_final"))

    flat = lambda d_: jnp.concatenate([_pad_rows(d_[k_]) for k_ in _SMALL])
    gflat = flat(gsm)
    nrow = -(-gflat.shape[0] // (8 * LANE)) * 8
    to2d = lambda v_: jnp.pad(v_, (0, nrow * LANE - v_.shape[0])).reshape(nrow, LANE)
    res = _adam(to2d(flat(wts)), to2d(flat(mom)), to2d(flat(var)), grad=to2d(gflat), tr=nrow, name="adam_small")
    outs = {k_: {} for k_ in ("grad", "delta", "m", "v")}
    o = 0
    for k_ in _SMALL:
        n_ = wts[k_].size
        for kind, arr in zip(("grad", "delta", "m", "v"), res):
            outs[kind][k_] = arr.reshape(-1)[o:o + n_].reshape(wts[k_].shape)
        o += n_ + (-n_) % LANE

    dm_all = packs[:, :6 * D // LANE, :].reshape(NDEV, 6 * D)
    dm_sh = lax.dynamic_slice(dm_all, (0, me * ncol), (NDEV, ncol))
    gwa = _gw_ada(c_all.T, dm_sh)
    big.update(
        w_ada=_adam(w_ada[0], m_w_ada[0], v_w_ada[0], grad=gwa, tr=256, name="adam_w_ada"),
        w_in=[o_.T for o_ in _adam(w_in[0].T, m_w_in[0].T, v_w_in[0].T, recv=ri, tc=512, name="adam_w_in")],
        w_out=_adam(w_out[0], m_w_out[0], v_w_out[0], recv=ro, tr=128, name="adam_w_out"))
    for k_, r_ in big.items():
        for kind, arr in zip(("grad", "delta", "m", "v"), r_):
            outs[kind][k_] = arr.reshape(wts[k_].shape)

    return (loss, gx.reshape(x.shape), *[outs["grad"][k_] for k_ in order], *[outs["delta"][k_] for k_ in order],
            *[outs["m"][k_] for k_ in order], *[outs["v"][k_] for k_ in order])
```

```python
import functools

import jax
import jax.numpy as jnp
from jax import lax
from jax.experimental import pallas as pl
from jax.experimental.pallas import tpu as pltpu

F32 = jnp.float32
BF = jnp.bfloat16
HI = lax.Precision.HIGHEST

D = 2048
NH = 16
HD = 64
AW = 1024
SW = 1024
NG = 2
NS = 128
XBC = 1536
CH = 64
BAND = 576
NREL = 320
REL_CLIP = 256
FF = 5632
INC = 5648
INP = 5760
EPS = 1e-6
NDEV = 8
LANE = 128
VMEM_LIMIT = 56 * 1024 * 1024
LR, B1, B2, AEPS, WD, STEP = 0.001, 0.9, 0.999, 1e-08, 0.01, 10
MESH = pl.DeviceIdType.MESH


def _cp(*sem):
    return pltpu.CompilerParams(dimension_semantics=sem, vmem_limit_bytes=VMEM_LIMIT)


def _row(ts, w, col=0):
    return pl.BlockSpec((ts, w), lambda i, _c=col: (i, _c))


def _fix(shape, single=False):
    nd = len(shape)
    if single:
        return pl.BlockSpec(shape, lambda i, _n=nd: (0,) * _n, pipeline_mode=pl.Buffered(1))
    return pl.BlockSpec(shape, lambda i, _n=nd: (0,) * _n)


def _silu_grad(x, s):
    return s * (1.0 + x * (1.0 - s))


def _softplus(x):
    return jnp.maximum(x, 0.0) + jnp.log1p(jnp.exp(-jnp.abs(x)))


_DIMS = {"NN": (((1,), (0,)), ((), ())), "TN": (((0,), (0,)), ((), ())), "NT": (((1,), (1,)), ((), ()))}


def _mm(pairs, mode, out_dtype, name, tm, tn, tk, precision=None, after=None):
    a0, b0 = pairs[0]
    m, k = a0.shape[::-1] if mode == "TN" else a0.shape
    n = b0.shape[0] if mode == "NT" else b0.shape[1]
    tm, tn, tk = min(tm, m), min(tn, n), min(tk, k)
    nk1 = k // tk
    npair = len(pairs)
    nk = 1 if nk1 == 1 else nk1 * npair
    assert m % tm == 0 and n % tn == 0 and k % tk == 0, (name, a0.shape, b0.shape)
    dims = _DIMS[mode]

    def kloc(kk, p):
        return jnp.clip(kk - p * nk1, 0, nk1 - 1)

    def aspec(p):
        if mode == "TN":
            return pl.BlockSpec((tk, tm), lambda i, j, kk, _p=p: (kloc(kk, _p), i))
        return pl.BlockSpec((tm, tk), lambda i, j, kk, _p=p: (i, kloc(kk, _p)))

    def bspec(p):
        if mode == "NT":
            return pl.BlockSpec((tn, tk), lambda i, j, kk, _p=p: (j, kloc(kk, _p)))
        return pl.BlockSpec((tk, tn), lambda i, j, kk, _p=p: (kloc(kk, _p), j))

    nin = 2 * npair + (after is not None)

    def body(*refs):
        ab, o_ref = refs[:2 * npair], refs[nin]
        if nk == 1:
            tot = lax.dot_general(ab[0][...], ab[1][...], dims, preferred_element_type=F32, precision=precision)
            for p in range(1, npair):
                tot = tot + lax.dot_general(ab[2 * p][...], ab[2 * p + 1][...], dims, preferred_element_type=F32,
                                            precision=precision)
            o_ref[...] = tot.astype(o_ref.dtype)
            return
        acc = refs[nin + 1]
        kk = pl.program_id(2)

        @pl.when(kk == 0)
        def _():
            acc[...] = jnp.zeros_like(acc)

        for p in range(npair):
            @pl.when((kk >= p * nk1) & (kk < (p + 1) * nk1))
            def _(p=p):
                acc[...] += lax.dot_general(ab[2 * p][...], ab[2 * p + 1][...], dims, preferred_element_type=F32,
                                            precision=precision)

        @pl.when(kk == nk - 1)
        def _():
            o_ref[...] = acc[...].astype(o_ref.dtype)

    in_specs = []
    for p in range(npair):
        in_specs += [aspec(p), bspec(p)]
    operands = [t for ab_ in pairs for t in ab_]
    if after is not None:
        in_specs.append(pl.BlockSpec(memory_space=pl.ANY))
        operands.append(after)
    return pl.pallas_call(
        body, name=name, grid=(m // tm, n // tn, nk), in_specs=in_specs,
        out_specs=pl.BlockSpec((tm, tn), lambda i, j, kk: (i, j)),
        out_shape=jax.ShapeDtypeStruct((m, n), out_dtype),
        scratch_shapes=[] if nk == 1 else [pltpu.VMEM((tm, tn), F32)],
        compiler_params=_cp("parallel", "parallel", "arbitrary"),
    )(*operands)


def _norm_fwd(x, g, sc, sh, mix=None, gt=None, *, name, ts=256):
    s, d = x.shape
    has = mix is not None

    def body(*refs):
        if has:
            x_ref, g_ref, sc_ref, sh_ref, mix_ref, gt_ref, h_ref, x1_ref = refs
        else:
            x_ref, g_ref, sc_ref, sh_ref, h_ref = refs
        xv = x_ref[...]
        if has:
            xv = xv + gt_ref[...] * mix_ref[...]
            x1_ref[...] = xv
        r = lax.rsqrt(jnp.mean(xv * xv, axis=-1, keepdims=True) + EPS)
        h_ref[...] = ((xv * r) * g_ref[...] * (1.0 + sc_ref[...]) + sh_ref[...]).astype(BF)

    ins = [x, g, sc, sh] + ([mix, gt] if has else [])
    in_specs = [_row(ts, d), _fix((1, d)), _fix((1, d)), _fix((1, d))] + ([_row(ts, d), _fix((1, d))] if has else [])
    out_shape = [jax.ShapeDtypeStruct((s, d), BF)] + ([jax.ShapeDtypeStruct((s, d), F32)] if has else [])
    out_specs = [_row(ts, d)] + ([_row(ts, d)] if has else [])
    out = pl.pallas_call(body, name=name, grid=(s // ts,), in_specs=in_specs, out_specs=out_specs,
                         out_shape=out_shape, compiler_params=_cp("parallel"))(*ins)
    return out if has else out[0]


def _norm_bwd(x, g, sc, sh, dh, dres, mix=None, gt=None, *, name, ts=256):
    s, d = x.shape
    has = mix is not None

    def body(*refs):
        if has:
            x_ref, g_ref, sc_ref, sh_ref, dh_ref, dres_ref, mix_ref, gt_ref, dx_ref, dmix_ref, part_ref = refs
        else:
            x_ref, g_ref, sc_ref, sh_ref, dh_ref, dres_ref, dx_ref, part_ref = refs
        i = pl.program_id(0)
        xv = x_ref[...]
        gv = g_ref[...]
        dhv = dh_ref[...]
        r = lax.rsqrt(jnp.mean(xv * xv, axis=-1, keepdims=True) + EPS)
        xh = xv * r
        dyn = dhv * (1.0 + sc_ref[...])
        gy = dyn * gv
        dx = dres_ref[...] + r * (gy - xh * jnp.mean(gy * xh, axis=-1, keepdims=True))
        dx_ref[...] = dx

        @pl.when(i == 0)
        def _():
            part_ref[...] = jnp.zeros_like(part_ref)

        part_ref[0:1, :] += jnp.sum(dhv * (xh * gv), axis=0, keepdims=True)
        part_ref[1:2, :] += jnp.sum(dhv, axis=0, keepdims=True)
        part_ref[2:3, :] += jnp.sum(dyn * xh, axis=0, keepdims=True)
        if has:
            part_ref[3:4, :] += jnp.sum(dx * mix_ref[...], axis=0, keepdims=True)
            dmix_ref[...] = (gt_ref[...] * dx).astype(BF)

    ins = [x, g, sc, sh, dh, dres] + ([mix, gt] if has else [])
    in_specs = [_row(ts, d), _fix((1, d)), _fix((1, d)), _fix((1, d)), _row(ts, d), _row(ts, d)]
    in_specs += [_row(ts, d), _fix((1, d))] if has else []
    out_shape = [jax.ShapeDtypeStruct((s, d), F32)] + ([jax.ShapeDtypeStruct((s, d), BF)] if has else [])
    out_shape += [jax.ShapeDtypeStruct((8, d), F32)]
    out_specs = [_row(ts, d)] + ([_row(ts, d)] if has else []) + [_fix((8, d))]
    return pl.pallas_call(body, name=name, grid=(s // ts,), in_specs=in_specs, out_specs=out_specs,
                          out_shape=out_shape, compiler_params=_cp("arbitrary"))(*ins)


def _final(x1, ffn, gt2, gf, tgt, *, ts=256):
    s, d = x1.shape

    def body(x1_ref, ffn_ref, gt_ref, g_ref, t_ref, dx_ref, dffn_ref, part_ref):
        i = pl.program_id(0)
        fv = ffn_ref[...]
        gv = g_ref[...]
        xv = x1_ref[...] + gt_ref[...] * fv
        r = lax.rsqrt(jnp.mean(xv * xv, axis=-1, keepdims=True) + EPS)
        xh = xv * r
        e = xh * gv - t_ref[...]
        dy = e * (1.0 / d)
        gy = dy * gv
        dx = r * (gy - xh * jnp.mean(gy * xh, axis=-1, keepdims=True))
        dx_ref[...] = dx
        dffn_ref[...] = (gt_ref[...] * dx).astype(BF)

        @pl.when(i == 0)
        def _():
            part_ref[...] = jnp.zeros_like(part_ref)

        part_ref[0:1, :] += jnp.sum(dy * xh, axis=0, keepdims=True)
        part_ref[1:2, :] += jnp.sum(dx * fv, axis=0, keepdims=True)
        part_ref[2:3, :] += jnp.sum(e * e, axis=0, keepdims=True) * (0.5 / d)

    return pl.pallas_call(
        body, name="final_loss", grid=(s // ts,),
        in_specs=[_row(ts, d), _row(ts, d), _fix((1, d)), _fix((1, d)), _row(ts, d)],
        out_specs=[_row(ts, d), _row(ts, d), _fix((8, d))],
        out_shape=[jax.ShapeDtypeStruct((s, d), F32), jax.ShapeDtypeStruct((s, d), BF),
                   jax.ShapeDtypeStruct((8, d), F32)],
        compiler_params=_cp("arbitrary"))(x1, ffn, gt2, gf, tgt)


def _gate_up(h, wg_t, wu_t, *, tm=1024, tn=512):
    s, k = h.shape
    n = wg_t.shape[0]
    tm = min(tm, s)

    def body(h_ref, wg_ref, wu_ref, g_ref, u_ref, a_ref):
        hv = h_ref[...]
        gv = lax.dot_general(hv, wg_ref[...], _DIMS["NT"], preferred_element_type=F32)
        uv = lax.dot_general(hv, wu_ref[...], _DIMS["NT"], preferred_element_type=F32)
        g_ref[...] = gv.astype(BF)
        u_ref[...] = uv.astype(BF)
        a_ref[...] = (gv * jax.nn.sigmoid(gv) * uv).astype(BF)

    wspec = pl.BlockSpec((tn, k), lambda i, j: (j, 0))
    ospec = pl.BlockSpec((tm, tn), lambda i, j: (i, j))
    return pl.pallas_call(body, name="gate_up", grid=(s // tm, n // tn),
                          in_specs=[pl.BlockSpec((tm, k), lambda i, j: (i, 0)), wspec, wspec],
                          out_specs=[ospec] * 3, out_shape=[jax.ShapeDtypeStruct((s, n), BF)] * 3,
                          compiler_params=_cp("parallel", "parallel"))(h, wg_t, wu_t)


def _gate_up_bwd(dffn, w_down, gate, up, *, tm=1024, tn=512):
    s, k = dffn.shape
    n = w_down.shape[0]
    tm = min(tm, s)

    def body(d_ref, w_ref, g_ref, u_ref, dg_ref, du_ref):
        dav = lax.dot_general(d_ref[...], w_ref[...], _DIMS["NT"], preferred_element_type=F32)
        gv = g_ref[...].astype(F32)
        uv = u_ref[...].astype(F32)
        sg = jax.nn.sigmoid(gv)
        dg_ref[...] = (dav * uv * _silu_grad(gv, sg)).astype(BF)
        du_ref[...] = (dav * gv * sg).astype(BF)

    tile = pl.BlockSpec((tm, tn), lambda i, j: (i, j))
    return pl.pallas_call(body, name="gate_up_bwd", grid=(s // tm, n // tn),
                          in_specs=[pl.BlockSpec((tm, k), lambda i, j: (i, 0)),
                                    pl.BlockSpec((tn, k), lambda i, j: (j, 0)), tile, tile],
                          out_specs=[tile] * 2, out_shape=[jax.ShapeDtypeStruct((s, n), BF)] * 2,
                          compiler_params=_cp("parallel", "parallel"))(dffn, w_down, gate, up)


def _mixout_fwd(att, y, proj, g_att, g_ssd, *, ts=256):
    s = att.shape[0]

    def body(a_ref, y_ref, z_ref, ga_ref, gs_ref, o_ref):
        av = a_ref[...]
        ra = lax.rsqrt(jnp.mean(av * av, axis=-1, keepdims=True) + EPS)
        o_ref[:, 0:AW] = (av * ra * ga_ref[...]).astype(BF)
        zv = z_ref[...]
        yz = y_ref[...] * (zv * jax.nn.sigmoid(zv))
        rs = lax.rsqrt(jnp.mean(yz * yz, axis=-1, keepdims=True) + EPS)
        o_ref[:, AW:AW + SW] = (yz * rs * gs_ref[...]).astype(BF)

    return pl.pallas_call(body, name="mixout_fwd", grid=(s // ts,),
                          in_specs=[_row(ts, AW), _row(ts, SW), _row(ts, SW, 3), _fix((1, AW)), _fix((1, SW))],
                          out_specs=_row(ts, AW + SW), out_shape=jax.ShapeDtypeStruct((s, AW + SW), BF),
                          compiler_params=_cp("parallel"))(att, y, proj, g_att, g_ssd)


def _mixout_bwd(dcat, att, y, proj, g_att, g_ssd, *, ts=256):
    s = att.shape[0]

    def body(dc_ref, a_ref, y_ref, z_ref, ga_ref, gs_ref, da_ref, dy_ref, dz_ref, part_ref):
        i = pl.program_id(0)
        av = a_ref[...]
        dca = dc_ref[:, 0:AW]
        ra = lax.rsqrt(jnp.mean(av * av, axis=-1, keepdims=True) + EPS)
        ah = av * ra
        gy = dca * ga_ref[...]
        da_ref[...] = ra * (gy - ah * jnp.mean(gy * ah, axis=-1, keepdims=True))
        zv = z_ref[...]
        yv = y_ref[...]
        sg = jax.nn.sigmoid(zv)
        sz = zv * sg
        yz = yv * sz
        dcs = dc_ref[:, AW:AW + SW]
        rs = lax.rsqrt(jnp.mean(yz * yz, axis=-1, keepdims=True) + EPS)
        yh = yz * rs
        gys = dcs * gs_ref[...]
        dyz = rs * (gys - yh * jnp.mean(gys * yh, axis=-1, keepdims=True))
        dy_ref[...] = dyz * sz
        dz_ref[...] = (dyz * yv * _silu_grad(zv, sg)).astype(BF)

        @pl.when(i == 0)
        def _():
            part_ref[...] = jnp.zeros_like(part_ref)

        part_ref[0:1, :] += jnp.sum(dca * ah, axis=0, keepdims=True)
        part_ref[1:2, :] += jnp.sum(dcs * yh, axis=0, keepdims=True)

    return pl.pallas_call(
        body, name="mixout_bwd", grid=(s // ts,),
        in_specs=[_row(ts, AW + SW), _row(ts, AW), _row(ts, SW), _row(ts, SW, 3), _fix((1, AW)), _fix((1, SW))],
        out_specs=[_row(ts, AW), _row(ts, SW), _row(ts, SW), _fix((8, AW))],
        out_shape=[jax.ShapeDtypeStruct((s, AW), F32), jax.ShapeDtypeStruct((s, SW), F32),
                   jax.ShapeDtypeStruct((s, SW), BF), jax.ShapeDtypeStruct((8, AW), F32)],
        compiler_params=_cp("arbitrary"))(dcat, att, y, proj, g_att, g_ssd)


TQ = 512
SCALE = HD ** -0.5


def _attn_fwd(qkv, bias):
    s = qkv.shape[0]
    nb = s // TQ
    prev = lambda i: jnp.maximum(i - 1, 0)

    def body(q_ref, kp_ref, kc_ref, vp_ref, vc_ref, b_ref, o_ref, kbuf, vbuf):
        i = pl.program_id(0)
        kbuf[0:TQ, :] = kp_ref[...]
        kbuf[TQ:2 * TQ, :] = kc_ref[...]
        vbuf[0:TQ, :] = vp_ref[...]
        vbuf[TQ:2 * TQ, :] = vc_ref[...]

        def chunk(c, carry):
            r0 = pl.multiple_of(c * CH, CH)
            kpos = i * TQ + r0 - TQ + lax.broadcasted_iota(jnp.int32, (1, BAND), 1)
            valid = kpos >= 0
            for h in range(NH):
                sl = slice(HD * h, HD * h + HD)
                q = q_ref[pl.ds(r0, CH), sl]
                k = kbuf[pl.ds(r0, BAND), sl]
                v = vbuf[pl.ds(r0, BAND), sl]
                sc = lax.dot_general(q, k, (((1,), (1,)), ((), ())), preferred_element_type=F32) * SCALE + b_ref[h]
                sc = jnp.where(valid, sc, -jnp.inf)
                e = jnp.exp(sc - jnp.max(sc, axis=-1, keepdims=True))
                p = e / jnp.sum(e, axis=-1, keepdims=True)
                o_ref[pl.ds(r0, CH), sl] = jnp.dot(p.astype(BF), v, preferred_element_type=F32)
            return carry

        lax.fori_loop(0, TQ // CH, chunk, 0)

    blk = lambda col, im: pl.BlockSpec((TQ, AW), lambda i, _c=col, _f=im: (_f(i), _c))
    cur = lambda i: i
    return pl.pallas_call(
        body, name="attn_fwd", grid=(nb,),
        in_specs=[blk(0, cur), blk(1, prev), blk(1, cur), blk(2, prev), blk(2, cur), _fix((NH, CH, BAND))],
        out_specs=_row(TQ, AW), out_shape=jax.ShapeDtypeStruct((s, AW), F32),
        scratch_shapes=[pltpu.VMEM((2 * TQ, AW), BF), pltpu.VMEM((2 * TQ, AW), BF)],
        compiler_params=_cp("parallel"))(qkv, qkv, qkv, qkv, qkv, bias)


def _attn_bwd(qkv, datt, bias_t):
    s = qkv.shape[0]
    nb = s // TQ
    cur = lambda i: jnp.minimum(i, nb - 1)
    prev = lambda i: jnp.maximum(jnp.minimum(i, nb - 1) - 1, 0)
    late = lambda i: jnp.maximum(i - 1, 0)

    def body(q_ref, kp_ref, kc_ref, vp_ref, vc_ref, do_ref, b_ref, dq_ref, dk_ref, dv_ref, db_ref,
             kbuf, vbuf, dkacc, dvacc):
        i = pl.program_id(0)

        @pl.when(i == 0)
        def _():
            dkacc[...] = jnp.zeros_like(dkacc)
            dvacc[...] = jnp.zeros_like(dvacc)
            db_ref[...] = jnp.zeros_like(db_ref)

        @pl.when(i < nb)
        def _():
            kbuf[0:TQ, :] = kp_ref[...]
            kbuf[TQ:2 * TQ, :] = kc_ref[...]
            vbuf[0:TQ, :] = vp_ref[...]
            vbuf[TQ:2 * TQ, :] = vc_ref[...]

            def chunk(c, carry):
                r0 = pl.multiple_of(c * CH, CH)
                kpos = i * TQ + r0 - TQ + lax.broadcasted_iota(jnp.int32, (BAND, 1), 0)
                valid = kpos >= 0
                for h in range(NH):
                    sl = slice(HD * h, HD * h + HD)
                    q = q_ref[pl.ds(r0, CH), sl]
                    k = kbuf[pl.ds(r0, BAND), sl]
                    v = vbuf[pl.ds(r0, BAND), sl]
                    do = do_ref[pl.ds(r0, CH), sl].astype(BF)
                    st = lax.dot_general(k, q, (((1,), (1,)), ((), ())), preferred_element_type=F32) * SCALE + b_ref[h]
                    st = jnp.where(valid, st, -jnp.inf)
                    e = jnp.exp(st - jnp.max(st, axis=0, keepdims=True))
                    pt = e / jnp.sum(e, axis=0, keepdims=True)
                    dvacc[pl.ds(r0, BAND), sl] += jnp.dot(pt.astype(BF), do, preferred_element_type=F32)
                    dpt = lax.dot_general(v, do, (((1,), (1,)), ((), ())), preferred_element_type=F32)
                    dst = pt * (dpt - jnp.sum(dpt * pt, axis=0, keepdims=True))
                    db_ref[h] += dst
                    dsb = dst.astype(BF)
                    dkacc[pl.ds(r0, BAND), sl] += jnp.dot(dsb, q, preferred_element_type=F32) * SCALE
                    dq = lax.dot_general(dsb, k, (((0,), (0,)), ((), ())), preferred_element_type=F32) * SCALE
                    dq_ref[pl.ds(r0, CH), sl] = dq.astype(BF)
                return carry

            lax.fori_loop(0, TQ // CH, chunk, 0)

        dk_ref[...] = dkacc[0:TQ, :].astype(BF)
        dv_ref[...] = dvacc[0:TQ, :].astype(BF)
        dkacc[0:TQ, :] = dkacc[TQ:2 * TQ, :]
        dvacc[0:TQ, :] = dvacc[TQ:2 * TQ, :]
        dkacc[TQ:2 * TQ, :] = jnp.zeros((TQ, AW), F32)
        dvacc[TQ:2 * TQ, :] = jnp.zeros((TQ, AW), F32)

    blk = lambda col, im: pl.BlockSpec((TQ, AW), lambda i, _c=col, _f=im: (_f(i), _c))
    return pl.pallas_call(
        body, name="attn_bwd", grid=(nb + 1,),
        in_specs=[blk(0, cur), blk(1, prev), blk(1, cur), blk(2, prev), blk(2, cur), blk(0, cur),
                  _fix((NH, BAND, CH))],
        out_specs=[blk(0, cur), blk(0, late), blk(0, late), _fix((NH, BAND, CH))],
        out_shape=[jax.ShapeDtypeStruct((s, AW), BF)] * 3 + [jax.ShapeDtypeStruct((NH, BAND, CH), F32)],
        scratch_shapes=[pltpu.VMEM((2 * TQ, AW), BF), pltpu.VMEM((2 * TQ, AW), BF),
                        pltpu.VMEM((2 * TQ, AW), F32), pltpu.VMEM((2 * TQ, AW), F32)],
        compiler_params=_cp("arbitrary"))(qkv, qkv, qkv, qkv, qkv, datt, bias_t)


QB = 256
NQC = QB // CH
WIN = BAND - CH + QB
NKB = WIN // QB


def _bias_window(bias):
    rows = [jnp.pad(bias, ((0, 0), (0, 0), (CH * a, WIN - BAND - CH * a)), constant_values=-jnp.inf)
            for a in range(NQC)]
    return jnp.concatenate(rows, axis=1)


def _win_specs(width, nb):
    def mk(col, back):
        return pl.BlockSpec((QB, width), lambda i, _c=col, _k=back: (jnp.maximum(jnp.minimum(i, nb - 1) - _k, 0), _c))
    return [mk(col, back) for col in (1, 2) for back in range(NKB - 1, -1, -1)]


def _attn2_fwd(qkv, bias_w):
    s = qkv.shape[0]
    nb = s // QB

    def body(q_ref, *refs):
        krefs, vrefs = refs[:NKB], refs[NKB:2 * NKB]
        b_ref, o_ref, lse_ref, kbuf, vbuf = refs[2 * NKB:]
        i = pl.program_id(0)
        for j in range(NKB):
            kbuf[QB * j:QB * j + QB, :] = krefs[j][...].astype(BF)
            vbuf[QB * j:QB * j + QB, :] = vrefs[j][...].astype(BF)
        valid = (i * QB - (NKB - 1) * QB + lax.broadcasted_iota(jnp.int32, (1, WIN), 1)) >= 0
        lane = lax.broadcasted_iota(jnp.int32, (QB, LANE), 1)

        lo = lane < HD

        def heads(masked):
            lse = jnp.zeros((QB, LANE), F32)
            for p in range(NH // 2):
                sl = slice(2 * HD * p, 2 * HD * p + 2 * HD)
                qp = q_ref[:, sl] * SCALE
                qbd = jnp.concatenate([jnp.where(lo, qp, 0.0), jnp.where(lo, 0.0, qp)], axis=0).astype(BF)
                sc = lax.dot_general(qbd, kbuf[:, sl], _DIMS["NT"], preferred_element_type=F32) + b_ref[p]
                if masked:
                    sc = jnp.where(valid, sc, -jnp.inf)
                mx = jnp.max(sc, axis=-1, keepdims=True)
                e = jnp.exp(sc - mx)
                tot = jnp.sum(e, axis=-1, keepdims=True)
                o2 = jnp.dot(e.astype(BF), vbuf[:, sl], preferred_element_type=F32) * (1.0 / tot)
                o_ref[:, sl] = jnp.where(lo, o2[0:QB], o2[QB:2 * QB])
                l2 = mx + jnp.log(tot)
                lse = jnp.where(lane == 2 * p, l2[0:QB], jnp.where(lane == 2 * p + 1, l2[QB:2 * QB], lse))
            lse_ref[...] = lse

        pl.when(i < NKB - 1)(functools.partial(heads, True))
        pl.when(i >= NKB - 1)(functools.partial(heads, False))

    return pl.pallas_call(
        body, name="attn_fwd", grid=(nb,),
        in_specs=[_row(QB, AW)] + _win_specs(AW, nb) + [_fix((NH // 2, 2 * QB, WIN), single=True)],
        out_specs=[_row(QB, AW), _row(QB, LANE)],
        out_shape=[jax.ShapeDtypeStruct((s, AW), F32), jax.ShapeDtypeStruct((s, LANE), F32)],
        scratch_shapes=[pltpu.VMEM((WIN, AW), BF), pltpu.VMEM((WIN, AW), BF)],
        compiler_params=_cp("parallel"))(*[qkv] * (1 + 2 * NKB), bias_w.reshape(NH // 2, 2 * QB, WIN))


def _attn2_bwd(qkv, datt, lse_t, bias_wt):
    s = qkv.shape[0]
    nb = s // QB
    cur = lambda i: jnp.minimum(i, nb - 1)
    late = lambda i: jnp.maximum(i - (NKB - 1), 0)

    def body(q_ref, *refs):
        krefs, vrefs = refs[:NKB], refs[NKB:2 * NKB]
        do_ref, lse_ref, b_ref, dq_ref, dk_ref, dv_ref, db_ref, kbuf, vbuf, dkacc, dvacc, dsbuf = refs[2 * NKB:]
        i = pl.program_id(0)

        @pl.when(i == 0)
        def _():
            dkacc[...] = jnp.zeros_like(dkacc)
            dvacc[...] = jnp.zeros_like(dvacc)
            db_ref[...] = jnp.zeros_like(db_ref)

        def heads(masked):
            for j in range(NKB):
                kbuf[QB * j:QB * j + QB, :] = krefs[j][...].astype(BF)
                vbuf[QB * j:QB * j + QB, :] = vrefs[j][...].astype(BF)
            valid = (i * QB - (NKB - 1) * QB + lax.broadcasted_iota(jnp.int32, (WIN, 1), 0)) >= 0
            lo = lax.broadcasted_iota(jnp.int32, (QB, LANE), 1) < HD

            def both(ref, sl):
                t = ref[:, sl]
                return jnp.concatenate([jnp.where(lo, t, 0.0), jnp.where(lo, 0.0, t)], axis=0)

            for p in range(NH // 2):
                sl = slice(2 * HD * p, 2 * HD * p + 2 * HD)
                qbd = (both(q_ref, sl) * SCALE).astype(BF)
                dobd = both(do_ref, sl).astype(BF)
                k = kbuf[:, sl]
                st = lax.dot_general(k, qbd, _DIMS["NT"], preferred_element_type=F32) + b_ref[p]
                if masked:
                    st = jnp.where(valid, st, -jnp.inf)
                lrow = jnp.concatenate([lse_ref[2 * p:2 * p + 1, :], lse_ref[2 * p + 1:2 * p + 2, :]], axis=1)
                pt = jnp.exp(st - lrow)
                dvacc[:, sl] += jnp.dot(pt.astype(BF), dobd, preferred_element_type=F32)
                dpt = lax.dot_general(vbuf[:, sl], dobd, _DIMS["NT"], preferred_element_type=F32)
                dst = pt * (dpt - jnp.sum(dpt * pt, axis=0, keepdims=True))
                dsbuf[...] = dst
                for hh in range(2):
                    fold = dsbuf[0:BAND, QB * hh:QB * hh + CH]
                    for a in range(1, NQC):
                        fold = fold + dsbuf[CH * a:CH * a + BAND, QB * hh + CH * a:QB * hh + CH * a + CH]
                    db_ref[2 * p + hh] += fold
                dsb = dst.astype(BF)
                dkacc[:, sl] += jnp.dot(dsb, qbd, preferred_element_type=F32)
                dq2 = lax.dot_general(dsb, k, _DIMS["TN"], preferred_element_type=F32) * SCALE
                dq_ref[:, sl] = jnp.where(lo, dq2[0:QB], dq2[QB:2 * QB]).astype(BF)

        pl.when(i < nb)(functools.partial(heads, True))

        dk_ref[...] = dkacc[0:QB, :].astype(BF)
        dv_ref[...] = dvacc[0:QB, :].astype(BF)
        for acc in (dkacc, dvacc):
            for j in range(NKB - 1):
                acc[QB * j:QB * j + QB, :] = acc[QB * j + QB:QB * j + 2 * QB, :]
            acc[WIN - QB:WIN, :] = jnp.zeros((QB, AW), F32)

    qspec = pl.BlockSpec((QB, AW), lambda i: (cur(i), 0))
    lspec = pl.BlockSpec((QB, AW), lambda i: (late(i), 0))
    return pl.pallas_call(
        body, name="attn_bwd", grid=(nb + NKB - 1,),
        in_specs=[qspec] + _win_specs(AW, nb) + [qspec, pl.BlockSpec((NH, QB), lambda i: (0, cur(i))),
                                                 _fix((NH // 2, WIN, 2 * QB), single=True)],
        out_specs=[qspec, lspec, lspec, _fix((NH, BAND, CH))],
        out_shape=[jax.ShapeDtypeStruct((s, AW), BF)] * 3 + [jax.ShapeDtypeStruct((NH, BAND, CH), F32)],
        scratch_shapes=[pltpu.VMEM((WIN, AW), BF), pltpu.VMEM((WIN, AW), BF),
                        pltpu.VMEM((WIN, AW), F32), pltpu.VMEM((WIN, AW), F32), pltpu.VMEM((WIN, 2 * QB), F32)],
        compiler_params=_cp("arbitrary"))(*[qkv] * (1 + 2 * NKB), datt, lse_t, bias_wt)


TB = 512
SCH = 512
NCB = TB // SCH
GW = 8 * HD


def _tri(lower):
    r = lax.broadcasted_iota(jnp.int32, (SCH, SCH), 0)
    c = lax.broadcasted_iota(jnp.int32, (SCH, SCH), 1)
    return r >= c if lower else r <= c


def _ssd_fwd(proj, conv_w, conv_b, dt_bias, a_log, d_skip):
    s = proj.shape[0]
    nb = s // TB

    def body(xs_ref, bc_ref, dt_ref, cw_ref, cb_ref, dtb_ref, al_ref, dsk_ref, y_ref, pre_ref, prev_ref,
             xpad, ubuf, dtbuf, csb, cstb, hst, xdec, yoffb, drow):
        i = pl.program_id(0)

        @pl.when(i == 0)
        def _():
            xpad[0:8, :] = jnp.zeros((8, XBC), F32)
            hst[...] = jnp.zeros_like(hst)

        xpad[8:8 + TB, 0:SW] = xs_ref[...]
        xpad[8:8 + TB, SW:XBC] = bc_ref[...]
        pre = cb_ref[...]
        for kk in range(4):
            pre = pre + cw_ref[kk:kk + 1, :] * xpad[5 + kk:5 + kk + TB, :]
        pre_ref[...] = pre
        ubuf[...] = pre * jax.nn.sigmoid(pre)
        xpad[0:8, :] = xpad[TB:TB + 8, :]
        dtbuf[...] = _softplus(dt_ref[...] + dtb_ref[...])
        a = -jnp.exp(al_ref[...])
        tri = _tri(True).astype(F32)
        causal = _tri(True)

        def chunk(c, carry):
            r0 = pl.multiple_of(c * SCH, SCH)
            rows = pl.ds(r0, SCH)
            cs = jnp.dot(tri, dtbuf[rows, :] * a, precision=HI, preferred_element_type=F32)
            csb[...] = cs
            cstb[...] = cs.T
            for g in range(NG):
                bg = ubuf[rows, SW + NS * g:SW + NS * g + NS]
                cg = ubuf[rows, SW + NG * NS + NS * g:SW + NG * NS + NS * g + NS].astype(BF)
                cb = lax.dot_general(cg, bg.astype(BF), (((1,), (1,)), ((), ())), preferred_element_type=F32)
                hg = hst[g]
                prev_ref[c, g] = hg
                yoffb[...] = jnp.dot(cg, hg.astype(BF), preferred_element_type=F32)
                for r in range(8):
                    h = 8 * g + r
                    sl = slice(HD * h, HD * h + HD)
                    rs = slice(HD * r, HD * r + HD)
                    cs_h = csb[:, h:h + 1]
                    cl = csb[SCH - 1:SCH, h:h + 1]
                    seg = jnp.exp(jnp.where(causal, cs_h - cstb[h:h + 1, :], -jnp.inf))
                    xh = ubuf[rows, sl]
                    xdt = xh * dtbuf[rows, h:h + 1]
                    yd = jnp.dot((cb * seg).astype(BF), xdt.astype(BF), preferred_element_type=F32)
                    y_ref[rows, sl] = yd + jnp.exp(cs_h) * yoffb[:, rs] + xh * dsk_ref[:, h:h + 1]
                    xdec[:, rs] = xdt * jnp.exp(cl - cs_h)
                    drow[:, rs] = jnp.broadcast_to(jnp.exp(cl), (1, HD))
                st = jnp.dot(bg.T.astype(BF), xdec[...].astype(BF), preferred_element_type=F32)
                hst[g] = hg * drow[...] + st
            return carry

        lax.fori_loop(0, NCB, chunk, 0)

    return pl.pallas_call(
        body, name="ssd_fwd", grid=(nb,),
        in_specs=[_row(TB, SW, 4), _row(TB, 512, 10), _row(TB, LANE, 44), _fix((4, XBC)), _fix((1, XBC)),
                  _fix((1, LANE)), _fix((1, LANE)), _fix((1, LANE))],
        out_specs=[_row(TB, SW), _row(TB, XBC), pl.BlockSpec((NCB, NG, NS, GW), lambda i: (i, 0, 0, 0))],
        out_shape=[jax.ShapeDtypeStruct((s, SW), F32), jax.ShapeDtypeStruct((s, XBC), F32),
                   jax.ShapeDtypeStruct((s // SCH, NG, NS, GW), F32)],
        scratch_shapes=[pltpu.VMEM((TB + 8, XBC), F32), pltpu.VMEM((TB, XBC), F32), pltpu.VMEM((TB, LANE), F32),
                        pltpu.VMEM((SCH, LANE), F32), pltpu.VMEM((LANE, SCH), F32), pltpu.VMEM((NG, NS, GW), F32),
                        pltpu.VMEM((SCH, GW), F32), pltpu.VMEM((SCH, GW), F32), pltpu.VMEM((1, GW), F32)],
        compiler_params=_cp("arbitrary"))(proj, proj, proj, conv_w, conv_b, dt_bias, a_log, d_skip)


def _ssd_bwd(dy, pre, proj, prev, conv_w, dt_bias, a_log, d_skip):
    s = dy.shape[0]
    nb = s // TB
    rev = lambda i: nb - 1 - i
    halo = lambda i: jnp.maximum((nb - 1 - i) * (TB // 8) - 1, 0)

    def body(dy_ref, pre_ref, dt_ref, prev_ref, xs_ref, bc_ref, xsh_ref, bch_ref, cw_ref, dtb_ref, al_ref, dsk_ref,
             dx_ref, ddt_ref, pw_ref, ph_ref,
             ubuf, dtbuf, dpad, xpad, csb, cstb, dhs, yoffb, ebuf, xdecb, wbuf, drow, dub):
        i = pl.program_id(0)

        @pl.when(i == 0)
        def _():
            dhs[...] = jnp.zeros_like(dhs)
            dpad[TB:TB + 8, :] = jnp.zeros((8, XBC), F32)
            pw_ref[...] = jnp.zeros_like(pw_ref)
            ph_ref[...] = jnp.zeros_like(ph_ref)

        pre = pre_ref[...]
        ubuf[...] = pre * jax.nn.sigmoid(pre)
        dtbuf[...] = _softplus(dt_ref[...] + dtb_ref[...])
        a = -jnp.exp(al_ref[...])
        tri = _tri(True).astype(F32)
        trit = _tri(False).astype(F32)
        causal = _tri(True)
        rowid = lax.broadcasted_iota(jnp.int32, (LANE, SCH), 0)
        lane = lax.broadcasted_iota(jnp.int32, (SCH, LANE), 1)
        lane1 = lax.broadcasted_iota(jnp.int32, (1, LANE), 1)
        lastrow = lax.broadcasted_iota(jnp.int32, (SCH, LANE), 0) == SCH - 1

        def chunk(cc, carry):
            c = NCB - 1 - cc
            r0 = pl.multiple_of(c * SCH, SCH)
            rows = pl.ds(r0, SCH)
            dtc = dtbuf[rows, :]
            cs = jnp.dot(tri, dtc * a, precision=HI, preferred_element_type=F32)
            csb[...] = cs
            cstb[...] = cs.T
            dcs = jnp.zeros((SCH, LANE), F32)
            dcst = jnp.zeros((LANE, SCH), F32)
            xr = jnp.zeros((SCH, LANE), F32)
            dlast = jnp.zeros((1, LANE), F32)
            dsk = jnp.zeros((1, LANE), F32)
            for g in range(NG):
                bg = ubuf[rows, SW + NS * g:SW + NS * g + NS]
                cg = ubuf[rows, SW + NG * NS + NS * g:SW + NG * NS + NS * g + NS]
                bb = bg.astype(BF)
                cbf = cg.astype(BF)
                cb = lax.dot_general(cbf, bb, (((1,), (1,)), ((), ())), preferred_element_type=F32)
                hp = prev_ref[c, g]
                hpb = hp.astype(BF)
                dhg = dhs[g]
                dhb = dhg.astype(BF)
                yoffb[...] = jnp.dot(cbf, hpb, preferred_element_type=F32)
                for r in range(8):
                    h = 8 * g + r
                    sl = slice(HD * h, HD * h + HD)
                    rs = slice(HD * r, HD * r + HD)
                    cs_h = csb[:, h:h + 1]
                    cl = csb[SCH - 1:SCH, h:h + 1]
                    dyh = dy_ref[rows, sl]
                    ecs = jnp.exp(cs_h)
                    ebuf[:, rs] = dyh * ecs
                    xdecb[:, rs] = ubuf[rows, sl] * dtbuf[rows, h:h + 1] * jnp.exp(cl - cs_h)
                    t1 = jnp.sum(dyh * ecs * yoffb[:, rs], axis=1, keepdims=True)
                    dcs = dcs + jnp.where(lane == h, t1, 0.0)
                    drow[:, rs] = jnp.broadcast_to(jnp.exp(cl), (1, HD))
                eb = ebuf[...].astype(BF)
                dc = lax.dot_general(eb, hpb, (((1,), (1,)), ((), ())), preferred_element_type=F32)
                dprev = jnp.dot(cg.T.astype(BF), eb, preferred_element_type=F32)
                wbuf[...] = jnp.dot(bb, dhb, preferred_element_type=F32)
                db = lax.dot_general(xdecb[...].astype(BF), dhb, (((1,), (1,)), ((), ())), preferred_element_type=F32)
                dcbs = jnp.zeros((SCH, SCH), F32)
                for r in range(8):
                    h = 8 * g + r
                    sl = slice(HD * h, HD * h + HD)
                    rs = slice(HD * r, HD * r + HD)
                    cs_h = csb[:, h:h + 1]
                    cst_h = cstb[h:h + 1, :]
                    cl = csb[SCH - 1:SCH, h:h + 1]
                    seg = jnp.exp(jnp.where(causal, cs_h - cst_h, -jnp.inf))
                    dyh = dy_ref[rows, sl]
                    xh = ubuf[rows, sl]
                    dt_h = dtbuf[rows, h:h + 1]
                    xdt = xh * dt_h
                    dyb = dyh.astype(BF)
                    dm = lax.dot_general(dyb, xdt.astype(BF), _DIMS["NT"], preferred_element_type=F32)
                    mm_ = cb * seg
                    dxdt = lax.dot_general(mm_.astype(BF), dyb, _DIMS["TN"], preferred_element_type=F32)
                    gg = dm * mm_
                    dcs_h = jnp.sum(gg, axis=1, keepdims=True)
                    dcst = dcst + jnp.where(rowid == h, jnp.sum(gg, axis=0, keepdims=True), 0.0)
                    dcbs = dcbs + dm * seg
                    dec = jnp.exp(cl - cs_h)
                    w_h = wbuf[:, rs]
                    dxdt = dxdt + dec * w_h
                    t = jnp.sum(w_h * xdt, axis=1, keepdims=True) * dec
                    dcs_h = dcs_h - t
                    dch = jnp.sum(jnp.sum(dhs[g, :, rs] * prev_ref[c, g, :, rs], axis=1, keepdims=True),
                                  axis=0, keepdims=True)
                    dl = jnp.sum(t, axis=0, keepdims=True) + dch * jnp.exp(cl)
                    dcs = dcs + jnp.where(lane == h, dcs_h, 0.0)
                    dlast = dlast + jnp.where(lane1 == h, dl, 0.0)
                    xr = xr + jnp.where(lane == h, jnp.sum(dxdt * xh, axis=1, keepdims=True), 0.0)
                    dskh = jnp.sum(jnp.sum(dyh * xh, axis=1, keepdims=True), axis=0, keepdims=True)
                    dsk = dsk + jnp.where(lane1 == h, dskh, 0.0)
                    dub[:, sl] = dyh * dsk_ref[:, h:h + 1] + dxdt * dt_h
                dc = dc + jnp.dot(dcbs.astype(BF), bb, preferred_element_type=F32)
                db = db + lax.dot_general(dcbs.astype(BF), cbf, _DIMS["TN"], preferred_element_type=F32)
                dub[:, SW + NS * g:SW + NS * g + NS] = db
                dub[:, SW + NG * NS + NS * g:SW + NG * NS + NS * g + NS] = dc
                dhs[g] = dprev + drow[...] * dhg
            dcs = dcs - dcst.T + jnp.where(lastrow, dlast, 0.0)
            dadt = jnp.dot(trit, dcs, precision=HI, preferred_element_type=F32)
            ddt = dadt * a + xr
            ddtr = ddt * jax.nn.sigmoid(dt_ref[rows, :] + dtb_ref[...])
            ddt_ref[rows, :] = ddtr.astype(BF)
            ph_ref[0:1, :] += jnp.sum(ddtr, axis=0, keepdims=True)
            ph_ref[1:2, :] += jnp.sum(dadt * dtc, axis=0, keepdims=True) * a
            ph_ref[2:3, :] += dsk
            pc = pre_ref[rows, :]
            sg = jax.nn.sigmoid(pc)
            dpad[rows, :] = dub[...] * _silu_grad(pc, sg)
            return carry

        lax.fori_loop(0, NCB, chunk, 0)

        dxb = cw_ref[0:1, :] * dpad[3:3 + TB, :]
        for kk in range(1, 4):
            dxb = dxb + cw_ref[kk:kk + 1, :] * dpad[3 - kk:3 - kk + TB, :]
        dx_ref[...] = dxb.astype(BF)
        keep = jnp.where(i < nb - 1, 1.0, 0.0)
        xpad[0:8, 0:SW] = xsh_ref[...] * keep
        xpad[0:8, SW:XBC] = bch_ref[...] * keep
        xpad[8:8 + TB, 0:SW] = xs_ref[...]
        xpad[8:8 + TB, SW:XBC] = bc_ref[...]
        dp = dpad[0:TB, :]
        for kk in range(4):
            pw_ref[kk:kk + 1, :] += jnp.sum(dp * xpad[5 + kk:5 + kk + TB, :], axis=0, keepdims=True)
        pw_ref[4:5, :] += jnp.sum(dp, axis=0, keepdims=True)
        dpad[TB:TB + 8, :] = dpad[0:8, :]

    rblk = lambda w, col: pl.BlockSpec((TB, w), lambda i, _c=col: (rev(i), _c))
    return pl.pallas_call(
        body, name="ssd_bwd", grid=(nb,),
        in_specs=[rblk(SW, 0), rblk(XBC, 0), rblk(LANE, 44),
                  pl.BlockSpec((NCB, NG, NS, GW), lambda i: (rev(i), 0, 0, 0)),
                  rblk(SW, 4), rblk(512, 10),
                  pl.BlockSpec((8, SW), lambda i: (halo(i), 4)), pl.BlockSpec((8, 512), lambda i: (halo(i), 10)),
                  _fix((4, XBC)), _fix((1, LANE)), _fix((1, LANE)), _fix((1, LANE))],
        out_specs=[rblk(XBC, 0), rblk(LANE, 0), _fix((8, XBC)), _fix((8, LANE))],
        out_shape=[jax.ShapeDtypeStruct((s, XBC), BF), jax.ShapeDtypeStruct((s, LANE), BF),
                   jax.ShapeDtypeStruct((8, XBC), F32), jax.ShapeDtypeStruct((8, LANE), F32)],
        scratch_shapes=[pltpu.VMEM((TB, XBC), F32), pltpu.VMEM((TB, LANE), F32), pltpu.VMEM((TB + 8, XBC), F32),
                        pltpu.VMEM((TB + 8, XBC), F32), pltpu.VMEM((SCH, LANE), F32), pltpu.VMEM((LANE, SCH), F32),
                        pltpu.VMEM((NG, NS, GW), F32), pltpu.VMEM((SCH, GW), F32), pltpu.VMEM((SCH, GW), F32),
                        pltpu.VMEM((SCH, GW), F32), pltpu.VMEM((SCH, GW), F32), pltpu.VMEM((1, GW), F32),
                        pltpu.VMEM((SCH, XBC), F32)],
        compiler_params=_cp("arbitrary"))(dy, pre, proj, prev, proj, proj, proj, proj, conv_w, dt_bias, a_log, d_skip)


def _rel_index():
    q = jnp.arange(CH)[:, None] + (BAND - CH)
    k = jnp.arange(BAND)[None, :]
    return jnp.clip(q - k, -(CH - 1), REL_CLIP) + (CH - 1)


def _local_step(x, tgt, mods, first_weights, late_weights, grads_ready,
                g_mix, rel_bias, conv_w, conv_b, dt_bias, a_log, d_skip, g_att, g_ssd, g_ffn, g_final):
    sh1, sc1, gt1, sh2, sc2, gt2 = mods
    pad16 = lambda v: jnp.pad(v, ((0, 0), (0, LANE - NH)))
    dtb_p, al_p, dsk_p = pad16(dt_bias), pad16(a_log), pad16(d_skip)
    ridx = _rel_index()
    ext = jnp.concatenate([jnp.broadcast_to(rel_bias[:, NREL - 1:], (NH, BAND + CH - 1 - NREL)),
                           rel_bias[:, ::-1]], axis=1)
    skew = jnp.tile(jnp.pad(ext, ((0, 0), (0, 1))), (1, CH + 1))[:, :CH * (BAND + CH + 1)]
    bias = skew.reshape(NH, CH, BAND + CH + 1)[:, ::-1, :BAND]
    bias_w = _bias_window(bias)
    bias_wt = jnp.transpose(bias_w.reshape(NH // 2, 2, QB, WIN), (0, 3, 1, 2)).reshape(NH // 2, WIN, 2 * QB)

    h1 = _norm_fwd(x, g_mix, sc1, sh1, name="norm1_fwd")
    onehot = (ridx.T.reshape(BAND * CH, 1) == jnp.arange(NREL)[None, :]).astype(F32)
    w_in_t = first_weights([h1, bias_w, bias_wt, onehot])
    proj = _mm([(h1, w_in_t)], "NT", F32, "in_proj", 1024, 1152, 2048)
    att, lse = _attn2_fwd(proj, bias_w)
    y, pre, prev = _ssd_fwd(proj, conv_w, conv_b, dtb_p, al_p, dsk_p)
    cat = _mixout_fwd(att, y, proj, g_att, g_ssd)
    w_out, w_gate_t, w_up_t, w_down = late_weights(cat)
    mix = _mm([(cat, w_out)], "NN", F32, "out_proj", 1024, 1024, 2048)
    h2, x1 = _norm_fwd(x, g_ffn, sc2, sh2, mix, gt1, name="norm2_fwd")
    gate, up, act = _gate_up(h2, w_gate_t, w_up_t)
    ffn = _mm([(act, w_down)], "NN", F32, "down_proj", 512, 1024, FF)
    dx2, dffn, pf = _final(x1, ffn, gt2, g_final, tgt)

    gw_down = _mm([(act, dffn)], "TN", BF, "gw_down", 1408, 2048, 1024)
    dgate, dup = _gate_up_bwd(dffn, w_down, gate, up)
    dh2 = _mm([(dgate, w_gate_t), (dup, w_up_t)], "NN", F32, "d_h2", 512, 512, FF)
    gw_gate_t = _mm([(dgate, h2)], "TN", BF, "gw_gate", 1408, 2048, 1024)
    gw_up_t = _mm([(dup, h2)], "TN", BF, "gw_up", 1408, 2048, 1024)
    tok = grads_ready("ffn", gw_gate_t, gw_up_t, gw_down)
    dx1, dmix, p2 = _norm_bwd(x1, g_ffn + tok[0, 0], sc2, sh2, dh2, dx2, mix, gt1, name="norm2_bwd")
    gw_out = _mm([(cat, dmix)], "TN", BF, "gw_out", 1024, 2048, 1024)
    tok = grads_ready("out", gw_out)
    dcat = _mm([(dmix, w_out)], "NT", F32, "d_cat", 1024, 1024, 2048, after=tok)
    datt, dy, dz, pm = _mixout_bwd(dcat, att, y, proj, g_att, g_ssd)
    dq, dk, dv, dbias_t = _attn2_bwd(proj, datt, lse[:, :NH].T, bias_wt)
    dxbc, ddt, pw, ph = _ssd_bwd(dy, pre, proj, prev, conv_w, dtb_p, al_p, dsk_p)
    dproj = jnp.concatenate([dq, dk, dv, dz, dxbc, ddt], axis=1)
    gw_in_t = _mm([(dproj, h1)], "TN", BF, "gw_in", 1152, 2048, 1024)
    tok = grads_ready("in", gw_in_t)
    dh1 = _mm([(dproj, w_in_t)], "NN", F32, "d_h1", 512, 1024, INP, after=tok)
    gx, p1 = _norm_bwd(x, g_mix, sc1, sh1, dh1, dx1, name="norm1_bwd")
    g_rel =_mm([(dbias_t.reshape(NH, BAND * CH), onehot)], "NN", F32, "g_rel_bias", NH, NREL, 4608, precision=HI)

    small = dict(
        dmods=jnp.concatenate([p1[1], p1[0], p2[3], p2[1], p2[0], pf[1]]),
        g_mix=p1[2], conv_b=pw[4], dt_bias=ph[0], a_log=ph[1], d_skip=ph[2], g_att=pm[0], g_ssd=pm[1],
        g_ffn=p2[2], g_final=pf[0], rel_bias=g_rel.reshape(-1), conv_w=pw[0:4].reshape(-1), loss=pf[2])
    return gx, small


def _exchange(srcs, gather, name):
    n = len(srcs)
    outs = [jax.ShapeDtypeStruct((NDEV,) + (s.shape if g else s.shape[1:]), s.dtype) for s, g in zip(srcs, gather)]
    outs.append(jax.ShapeDtypeStruct((8, LANE), F32))

    def body(*refs):
        src, dst, token = refs[:n], refs[n:2 * n], refs[2 * n]
        ssem, rsem, lsem = refs[2 * n + 1:]
        token[...] = jnp.zeros_like(token)
        x, y, c = lax.axis_index("x"), lax.axis_index("y"), lax.axis_index("c")
        me = 4 * x + 2 * y + c
        local = []
        for a in range(n):
            own = src[a] if gather[a] else src[a].at[me]
            cp = pltpu.make_async_copy(own, dst[a].at[me], lsem.at[a])
            cp.start()
            local.append(cp)
        remote = []
        for k in range(1, NDEV):
            px = 1 - x if k & 4 else x
            py = 1 - y if k & 2 else y
            pc = 1 - c if k & 1 else c
            pid = 4 * px + 2 * py + pc
            for a in range(n):
                s_ref = src[a] if gather[a] else src[a].at[pid]
                cp = pltpu.make_async_remote_copy(
                    src_ref=s_ref, dst_ref=dst[a].at[me], send_sem=ssem.at[a * (NDEV - 1) + k - 1],
                    recv_sem=rsem.at[a * (NDEV - 1) + k - 1], device_id=(px, py, pc), device_id_type=MESH)
                cp.start()
                remote.append(cp)
        for cp in remote:
            cp.wait()
        for cp in local:
            cp.wait()

    anyspec = pl.BlockSpec(memory_space=pl.ANY)
    return pl.pallas_call(
        body, name=name, out_shape=outs, in_specs=[anyspec] * n,
        out_specs=[anyspec] * n + [pl.BlockSpec(memory_space=pltpu.VMEM)],
        scratch_shapes=[pltpu.SemaphoreType.DMA((n * (NDEV - 1),)), pltpu.SemaphoreType.DMA((n * (NDEV - 1),)),
                        pltpu.SemaphoreType.DMA((n,))],
    )(*srcs)


_HBM = pl.BlockSpec(memory_space=pltpu.HBM)
_SEM = pl.BlockSpec(memory_space=pltpu.SEMAPHORE)
_EFFECT = pltpu.SideEffectType.DATAFLOW_SIDE_EFFECTING


def _peer_copies(src, land, ssem, rsem, gather):
    x, y, c = lax.axis_index("x"), lax.axis_index("y"), lax.axis_index("c")
    me = 4 * x + 2 * y + c
    copies = []
    for a in range(len(src)):
        for k in range(1, NDEV):
            px = 1 - x if k & 4 else x
            py = 1 - y if k & 2 else y
            pc = 1 - c if k & 1 else c
            s_ref = src[a] if gather[a] else src[a].at[4 * px + 2 * py + pc]
            idx = a * (NDEV - 1) + k - 1
            copies.append(pltpu.make_async_remote_copy(
                src_ref=s_ref, dst_ref=land[a].at[me], send_sem=ssem.at[idx], recv_sem=rsem.at[idx],
                device_id=(px, py, pc), device_id_type=MESH))
    return copies


def _xstart(srcs, gather, name):
    n = len(srcs)
    me = 4 * lax.axis_index("x") + 2 * lax.axis_index("y") + lax.axis_index("c")
    lands = []
    for s_, g_ in zip(srcs, gather):
        own = s_ if g_ else lax.dynamic_index_in_dim(s_, me, 0, keepdims=False)
        empty = lax.empty((NDEV,) + own.shape, own.dtype)
        lands.append(lax.dynamic_update_slice(empty, own[None], (me,) + (0,) * own.ndim))
    nsem = n * (NDEV - 1)

    def body(*refs):
        src, land, ssem, rsem, token = refs[:n], refs[n:2 * n], refs[2 * n], refs[2 * n + 1], refs[-1]
        for cp in _peer_copies(src, land, ssem, rsem, gather):
            cp.start()
        token[...] = jnp.zeros_like(token)

    ops = [pltpu.with_memory_space_constraint(a_, pltpu.HBM) for a_ in list(srcs) + lands]
    return pl.pallas_call(
        body, name=name,
        out_shape=(pltpu.SemaphoreType.DMA((nsem,)), pltpu.SemaphoreType.DMA((nsem,)),
                   *[pltpu.HBM(a_.shape, a_.dtype) for a_ in ops], jax.ShapeDtypeStruct((8, LANE), F32)),
        in_specs=[_HBM] * (2 * n),
        out_specs=(_SEM, _SEM, *[_HBM] * (2 * n), pl.BlockSpec(memory_space=pltpu.VMEM)),
        input_output_aliases={i: 2 + i for i in range(2 * n)},
        compiler_params=pltpu.CompilerParams(has_side_effects=_EFFECT),
    )(*ops)


def _xwait(started, gather, after, name):
    n = len(gather)
    ssem, rsem = started[0], started[1]
    bufs = started[2:2 + 2 * n]
    after = list(after) if isinstance(after, (list, tuple)) else [after]

    def body(*refs):
        src, land, ssem_, rsem_ = refs[:n], refs[n:2 * n], refs[2 * n], refs[2 * n + 1]
        for cp in _peer_copies(src, land, ssem_, rsem_, gather):
            cp.wait_send()
            cp.wait_recv()

    out = pl.pallas_call(
        body, name=name, out_shape=tuple(pltpu.HBM(a_.shape, a_.dtype) for a_ in bufs),
        in_specs=[_HBM] * (2 * n) + [_SEM, _SEM] + [pl.BlockSpec(memory_space=pl.ANY)] * len(after),
        out_specs=[_HBM] * (2 * n), input_output_aliases={i: i for i in range(2 * n)},
        compiler_params=pltpu.CompilerParams(has_side_effects=_EFFECT),
    )(*bufs, ssem, rsem, *after)
    return out[n:]


def _mods(c16, w_shard, b_shard, *, tn=512):
    n = w_shard.shape[1]

    def body(c_ref, w_ref, b_ref, o_ref):
        cv = c_ref[...]
        cond = (cv * jax.nn.sigmoid(cv)).astype(BF)
        o_ref[...] = jnp.dot(cond, w_ref[...].astype(BF), preferred_element_type=F32) + b_ref[...]

    return pl.pallas_call(body, name="ada_mods", grid=(n // tn,),
                          in_specs=[_fix((16, D)), pl.BlockSpec((D, tn), lambda j: (0, j)),
                                    pl.BlockSpec((1, tn), lambda j: (0, j))],
                          out_specs=pl.BlockSpec((16, tn), lambda j: (0, j)),
                          out_shape=jax.ShapeDtypeStruct((16, n), F32), compiler_params=_cp("parallel"))(c16, w_shard, b_shard)


def _gw_ada(c_t, dm, *, tr=256):
    n = dm.shape[1]

    def body(ct_ref, dm_ref, o_ref):
        acc = jnp.zeros((tr, n), F32)
        for b in range(NDEV):
            cv = ct_ref[:, b:b + 1]
            cond = (cv * jax.nn.sigmoid(cv)).astype(BF).astype(F32)
            acc = acc + cond * dm_ref[b:b + 1, :].astype(BF).astype(F32)
        o_ref[...] = acc

    return pl.pallas_call(body, name="gw_ada", grid=(D // tr,),
                          in_specs=[_row(tr, NDEV), _fix((NDEV, n))], out_specs=_row(tr, n),
                          out_shape=jax.ShapeDtypeStruct((D, n), F32), compiler_params=_cp("parallel"))(c_t, dm)


def _sum8(parts, *, name, tc=512):
    _, r, c = parts.shape
    tc = min(tc, c)

    def body(p_ref, o_ref):
        acc = p_ref[0].astype(F32)
        for d in range(1, NDEV):
            acc = acc + p_ref[d].astype(F32)
        o_ref[...] = acc

    return pl.pallas_call(body, name=name, grid=(c // tc,),
                          in_specs=[pl.BlockSpec((NDEV, r, tc), lambda j: (0, 0, j))],
                          out_specs=pl.BlockSpec((r, tc), lambda j: (0, j)),
                          out_shape=jax.ShapeDtypeStruct((r, c), F32), compiler_params=_cp("parallel"))(parts)


def _adam(w, m, v, *, recv=None, grad=None, tr=None, tc=None, name):
    lead = w.ndim == 3
    r, c = w.shape[-2:]
    summed = recv is not None
    assert tc is not None or r % tr == 0
    at = (0,) if lead else (Ellipsis,)

    def body(g_ref, w_ref, m_ref, v_ref, go_ref, do_ref, mo_ref, vo_ref):
        if summed:
            g = g_ref[0].astype(F32)
            for d in range(1, NDEV):
                g = g + g_ref[d].astype(F32)
        else:
            g = g_ref[...]
        go_ref[at] = g
        mn = B1 * m_ref[at] + (1.0 - B1) * g
        vn = B2 * v_ref[at] + (1.0 - B2) * (g * g)
        mo_ref[at] = mn
        vo_ref[at] = vn
        mh = mn / (1.0 - B1 ** STEP)
        vh = vn / (1.0 - B2 ** STEP)
        do_ref[at] = -LR * (mh / (jnp.sqrt(vh) + AEPS) + WD * w_ref[at])

    if tc is not None:
        assert not lead and c % tc == 0
        gspec = (pl.BlockSpec((NDEV, r, tc), lambda j: (0, 0, j)) if summed
                 else pl.BlockSpec((r, tc), lambda j: (0, j)))
        wspec = pl.BlockSpec((r, tc), lambda j: (0, j))
        return pl.pallas_call(body, name=name, grid=(c // tc,),
                              in_specs=[gspec, wspec, wspec, wspec], out_specs=[wspec] * 4,
                              out_shape=[jax.ShapeDtypeStruct(w.shape, F32)] * 4,
                              compiler_params=_cp("parallel"))(recv if summed else grad, w, m, v)
    gspec = pl.BlockSpec((NDEV, tr, c), lambda i: (0, i, 0)) if summed else _row(tr, c)
    wspec = pl.BlockSpec((1, tr, c), lambda i: (0, i, 0)) if lead else _row(tr, c)
    return pl.pallas_call(body, name=name, grid=(r // tr,),
                          in_specs=[gspec, wspec, wspec, wspec], out_specs=[wspec] * 4,
                          out_shape=[jax.ShapeDtypeStruct(w.shape, F32)] * 4,
                          compiler_params=_cp("parallel"))(recv if summed else grad, w, m, v)


def _cols_to_blocks(g, width):
    r = g.shape[0]
    return jnp.transpose(g.reshape(r, NDEV, width), (1, 0, 2))


def _blocks_to_cols(b):
    _, r, width = b.shape
    return jnp.transpose(b, (1, 0, 2)).reshape(r, NDEV * width)


_SMALL = ("b_ada", "g_mix", "rel_bias", "conv_w", "conv_b", "dt_bias", "a_log", "d_skip",
          "g_att_out", "g_ssd_out", "g_ffn", "g_final")


def _pad_rows(v):
    v = v.reshape(-1)
    return jnp.pad(v, (0, (-v.shape[0]) % LANE))


def kernel(x, c, w_ada, b_ada, g_mix, w_in, rel_bias, conv_w, conv_b, dt_bias, a_log, d_skip, g_att_out, g_ssd_out, w_out, g_ffn, w_gate, w_up, w_down, g_final, loss_target, m_w_ada, m_b_ada, m_g_mix, m_w_in, m_rel_bias, m_conv_w, m_conv_b, m_dt_bias, m_a_log, m_d_skip, m_g_att_out, m_g_ssd_out, m_w_out, m_g_ffn, m_w_gate, m_w_up, m_w_down, m_g_final, v_w_ada, v_b_ada, v_g_mix, v_w_in, v_rel_bias, v_conv_w, v_conv_b, v_dt_bias, v_a_log, v_d_skip, v_g_att_out, v_g_ssd_out, v_w_out, v_g_ffn, v_w_gate, v_w_up, v_w_down, v_g_final):
    me = 4 * lax.axis_index("x") + 2 * lax.axis_index("y") + lax.axis_index("c")
    wts = dict(w_ada=w_ada, b_ada=b_ada, g_mix=g_mix, w_in=w_in, rel_bias=rel_bias, conv_w=conv_w, conv_b=conv_b,
               dt_bias=dt_bias, a_log=a_log, d_skip=d_skip, g_att_out=g_att_out, g_ssd_out=g_ssd_out, w_out=w_out,
               g_ffn=g_ffn, w_gate=w_gate, w_up=w_up, w_down=w_down, g_final=g_final)
    mom = dict(w_ada=m_w_ada, b_ada=m_b_ada, g_mix=m_g_mix, w_in=m_w_in, rel_bias=m_rel_bias, conv_w=m_conv_w,
               conv_b=m_conv_b, dt_bias=m_dt_bias, a_log=m_a_log, d_skip=m_d_skip, g_att_out=m_g_att_out,
               g_ssd_out=m_g_ssd_out, w_out=m_w_out, g_ffn=m_g_ffn, w_gate=m_w_gate, w_up=m_w_up, w_down=m_w_down,
               g_final=m_g_final)
    var = dict(w_ada=v_w_ada, b_ada=v_b_ada, g_mix=v_g_mix, w_in=v_w_in, rel_bias=v_rel_bias, conv_w=v_conv_w,
               conv_b=v_conv_b, dt_bias=v_dt_bias, a_log=v_a_log, d_skip=v_d_skip, g_att_out=v_g_att_out,
               g_ssd_out=v_g_ssd_out, w_out=v_w_out, g_ffn=v_g_ffn, w_gate=v_w_gate, w_up=v_w_up, w_down=v_w_down,
               g_final=v_g_final)
    order = ("w_ada", "b_ada", "g_mix", "w_in", "rel_bias", "conv_w", "conv_b", "dt_bias", "a_log", "d_skip",
             "g_att_out", "g_ssd_out", "w_out", "g_ffn", "w_gate", "w_up", "w_down", "g_final")

    c_all, rel_all, cw_all, _ = _exchange([c, rel_bias[0], conv_w[0]], [True] * 3, "gather_small")
    c_all = c_all.reshape(NDEV, D)
    rel_full = _blocks_to_cols(rel_all)
    cw_full = _blocks_to_cols(cw_all)

    ncol = w_ada.shape[2]
    b_sh = lax.dynamic_slice(b_ada, (0, me * ncol), (1, ncol))
    mods_part = _mods(jnp.pad(c_all, ((0, 8), (0, 0))), w_ada[0], b_sh)[:NDEV]
    mods_all, tok_m = _exchange([mods_part], [True], "gather_mods")
    mods_me = lax.dynamic_index_in_dim(mods_all, me, axis=1, keepdims=False).reshape(6, 1, D)
    mods = [mods_me[i] for i in range(6)]

    st_in = _xstart([(w_in[0] + tok_m[0, 0]).T.astype(BF)], [True], "gather_w_in_start")
    st_rest = _xstart([(w_out[0] + st_in[-1][0, 0]).astype(BF), w_gate[0].T.astype(BF), w_up[0].T.astype(BF),
                       w_down[0].astype(BF)], [True] * 4, "gather_w_rest_start")

    def first_weights(after):
        (gi,) = _xwait(st_in, [True], after, "gather_w_in_wait")
        return jnp.pad(gi.reshape(INC, D), ((0, INP - INC), (0, 0)))

    def late_weights(after):
        go, gg, gu_, gd = _xwait(st_rest, [True] * 4, after, "gather_w_rest_wait")
        return go.reshape(D, D), gg.reshape(FF, D), gu_.reshape(FF, D), gd.reshape(FF, D)

    started = {}

    def grads_ready(which, *g):
        if which == "in":
            g = (g[0][:INC],)
        srcs = [g_.reshape(NDEV, g_.shape[0] // NDEV, D) for g_ in g]
        started[which] = _xstart(srcs, [False] * len(srcs), f"exchange_{which}_grads_start")
        return started[which][-1]

    gx, small = _local_step(
        x[0], loss_target[0], mods, first_weights, late_weights, grads_ready,
        g_mix + st_rest[-1][0, 0], rel_full, cw_full, conv_b, dt_bias, a_log, d_skip, g_att_out, g_ssd_out, g_ffn,
        g_final.reshape(1, D))

    loss_row = jnp.pad(jnp.sum(small["loss"]).reshape(1), (0, LANE - 1))
    pack = jnp.concatenate([small[k] for k in ("dmods", "g_mix", "conv_b", "dt_bias", "a_log", "d_skip", "g_att",
                                               "g_ssd", "g_ffn", "g_final", "rel_bias", "conv_w")] + [loss_row])
    packs, _ = _exchange([pack.reshape(-1, LANE)], [True], "gather_small_grads")
    rg, ru, rd = _xwait(started["ffn"], [False] * 3, packs, "exchange_ffn_grads_wait")
    (ro,) = _xwait(started["out"], [False], rg, "exchange_out_grads_wait")
    (ri,) = _xwait(started["in"], [False], ro, "exchange_in_grads_wait")
    big = dict(
        w_gate=[o_.T for o_ in _adam(w_gate[0].T, m_w_gate[0].T, v_w_gate[0].T, recv=rg, tc=512,
                                     name="adam_w_gate")],
        w_up=[o_.T for o_ in _adam(w_up[0].T, m_w_up[0].T, v_w_up[0].T, recv=ru, tc=512, name="adam_w_up")],
        w_down=_adam(w_down[0], m_w_down[0], v_w_down[0], recv=rd, tr=176, name="adam_w_down"))
    tot = _sum8(packs, name="sum_small").reshape(-1)

    sizes = dict(dmods=6 * D, g_mix=D, conv_b=XBC, dt_bias=LANE, a_log=LANE, d_skip=LANE, g_att_out=AW, g_ssd_out=SW,
                 g_ffn=D, g_final=D, rel_bias=NH * NREL, conv_w=4 * XBC, loss=LANE)
    off, o = {}, 0
    for k_, n_ in sizes.items():
        off[k_] = o
        o += n_
    take = lambda k_, n_=None: tot[off[k_]:off[k_] + (n_ or sizes[k_])]
    loss = tot[off["loss"]]
    rel_w = NREL // NDEV
    cw_w = XBC // NDEV
    gsm = dict(
        b_ada=take("dmods"), g_mix=take("g_mix"),
        rel_bias=lax.dynamic_slice(take("rel_bias").reshape(NH, NREL), (0, me * rel_w), (NH, rel_w)),
        conv_w=lax.dynamic_slice(take("conv_w").reshape(4, XBC), (0, me * cw_w), (4, cw_w)),
        conv_b=take("conv_b"), dt_bias=take("dt_bias", NH), a_log=take("a_log", NH), d_skip=take("d_skip", NH),
        g_att_out=take("g_att_out"), g_ssd_out=take("g_ssd_out"), g_ffn=take("g_ffn"), g_final=take("g_final"))

    flat = lambda d_: jnp.concatenate([_pad_rows(d_[k_]) for k_ in _SMALL])
    gflat = flat(gsm)
    nrow = -(-gflat.shape[0] // (8 * LANE)) * 8
    to2d = lambda v_: jnp.pad(v_, (0, nrow * LANE - v_.shape[0])).reshape(nrow, LANE)
    res = _adam(to2d(flat(wts)), to2d(flat(mom)), to2d(flat(var)), grad=to2d(gflat), tr=nrow, name="adam_small")
    outs = {k_: {} for k_ in ("grad", "delta", "m", "v")}
    o = 0
    for k_ in _SMALL:
        n_ = wts[k_].size
        for kind, arr in zip(("grad", "delta", "m", "v"), res):
            outs[kind][k_] = arr.reshape(-1)[o:o + n_].reshape(wts[k_].shape)
        o += n_ + (-n_) % LANE

    dm_all = packs[:, :6 * D // LANE, :].reshape(NDEV, 6 * D)
    dm_sh = lax.dynamic_slice(dm_all, (0, me * ncol), (NDEV, ncol))
    gwa = _gw_ada(c_all.T, dm_sh)
    big.update(
        w_ada=_adam(w_ada[0], m_w_ada[0], v_w_ada[0], grad=gwa, tr=256, name="adam_w_ada"),
        w_in=[o_.T for o_ in _adam(w_in[0].T, m_w_in[0].T, v_w_in[0].T, recv=ri, tc=512, name="adam_w_in")],
        w_out=_adam(w_out[0], m_w_out[0], v_w_out[0], recv=ro, tr=128, name="adam_w_out"))
    for k_, r_ in big.items():
        for kind, arr in zip(("grad", "delta", "m", "v"), r_):
            outs[kind][k_] = arr.reshape(wts[k_].shape)

    return (loss, gx.reshape(x.shape), *[outs["grad"][k_] for k_ in order], *[outs["delta"][k_] for k_ in order],
            *[outs["m"][k_] for k_ in order], *[outs["v"][k_] for k_ in order])
```

```python
import functools

import jax
import jax.numpy as jnp
from jax import lax
from jax.experimental import pallas as pl
from jax.experimental.pallas import tpu as pltpu

F32 = jnp.float32
BF = jnp.bfloat16
HI = lax.Precision.HIGHEST

D = 2048
NH = 16
HD = 64
AW = 1024
SW = 1024
NG = 2
NS = 128
XBC = 1536
CH = 64
BAND = 576
NREL = 320
REL_CLIP = 256
FF = 5632
INC = 5648
INP = 5760
EPS = 1e-6
NDEV = 8
LANE = 128
VMEM_LIMIT = 56 * 1024 * 1024
LR, B1, B2, AEPS, WD, STEP = 0.001, 0.9, 0.999, 1e-08, 0.01, 10
MESH = pl.DeviceIdType.MESH


def _cp(*sem):
    return pltpu.CompilerParams(dimension_semantics=sem, vmem_limit_bytes=VMEM_LIMIT)


def _row(ts, w, col=0):
    return pl.BlockSpec((ts, w), lambda i, _c=col: (i, _c))


def _fix(shape, single=False):
    nd = len(shape)
    if single:
        return pl.BlockSpec(shape, lambda i, _n=nd: (0,) * _n, pipeline_mode=pl.Buffered(1))
    return pl.BlockSpec(shape, lambda i, _n=nd: (0,) * _n)


def _silu_grad(x, s):
    return s * (1.0 + x * (1.0 - s))


def _softplus(x):
    return jnp.maximum(x, 0.0) + jnp.log1p(jnp.exp(-jnp.abs(x)))


_DIMS = {"NN": (((1,), (0,)), ((), ())), "TN": (((0,), (0,)), ((), ())), "NT": (((1,), (1,)), ((), ()))}


def _mm(pairs, mode, out_dtype, name, tm, tn, tk, precision=None, after=None):
    a0, b0 = pairs[0]
    m, k = a0.shape[::-1] if mode == "TN" else a0.shape
    n = b0.shape[0] if mode == "NT" else b0.shape[1]
    tm, tn, tk = min(tm, m), min(tn, n), min(tk, k)
    nk1 = k // tk
    npair = len(pairs)
    nk = 1 if nk1 == 1 else nk1 * npair
    assert m % tm == 0 and n % tn == 0 and k % tk == 0, (name, a0.shape, b0.shape)
    dims = _DIMS[mode]

    def kloc(kk, p):
        return jnp.clip(kk - p * nk1, 0, nk1 - 1)

    def aspec(p):
        if mode == "TN":
            return pl.BlockSpec((tk, tm), lambda i, j, kk, _p=p: (kloc(kk, _p), i))
        return pl.BlockSpec((tm, tk), lambda i, j, kk, _p=p: (i, kloc(kk, _p)))

    def bspec(p):
        if mode == "NT":
            return pl.BlockSpec((tn, tk), lambda i, j, kk, _p=p: (j, kloc(kk, _p)))
        return pl.BlockSpec((tk, tn), lambda i, j, kk, _p=p: (kloc(kk, _p), j))

    nin = 2 * npair + (after is not None)

    def body(*refs):
        ab, o_ref = refs[:2 * npair], refs[nin]
        if nk == 1:
            tot = lax.dot_general(ab[0][...], ab[1][...], dims, preferred_element_type=F32, precision=precision)
            for p in range(1, npair):
                tot = tot + lax.dot_general(ab[2 * p][...], ab[2 * p + 1][...], dims, preferred_element_type=F32,
                                            precision=precision)
            o_ref[...] = tot.astype(o_ref.dtype)
            return
        acc = refs[nin + 1]
        kk = pl.program_id(2)

        @pl.when(kk == 0)
        def _():
            acc[...] = jnp.zeros_like(acc)

        for p in range(npair):
            @pl.when((kk >= p * nk1) & (kk < (p + 1) * nk1))
            def _(p=p):
                acc[...] += lax.dot_general(ab[2 * p][...], ab[2 * p + 1][...], dims, preferred_element_type=F32,
                                            precision=precision)

        @pl.when(kk == nk - 1)
        def _():
            o_ref[...] = acc[...].astype(o_ref.dtype)

    in_specs = []
    for p in range(npair):
        in_specs += [aspec(p), bspec(p)]
    operands = [t for ab_ in pairs for t in ab_]
    if after is not None:
        in_specs.append(pl.BlockSpec(memory_space=pl.ANY))
        operands.append(after)
    return pl.pallas_call(
        body, name=name, grid=(m // tm, n // tn, nk), in_specs=in_specs,
        out_specs=pl.BlockSpec((tm, tn), lambda i, j, kk: (i, j)),
        out_shape=jax.ShapeDtypeStruct((m, n), out_dtype),
        scratch_shapes=[] if nk == 1 else [pltpu.VMEM((tm, tn), F32)],
        compiler_params=_cp("parallel", "parallel", "arbitrary"),
    )(*operands)


def _norm_fwd(x, g, sc, sh, mix=None, gt=None, *, name, ts=256):
    s, d = x.shape
    has = mix is not None

    def body(*refs):
        if has:
            x_ref, g_ref, sc_ref, sh_ref, mix_ref, gt_ref, h_ref, x1_ref = refs
        else:
            x_ref, g_ref, sc_ref, sh_ref, h_ref = refs
        xv = x_ref[...]
        if has:
            xv = xv + gt_ref[...] * mix_ref[...]
            x1_ref[...] = xv
        r = lax.rsqrt(jnp.mean(xv * xv, axis=-1, keepdims=True) + EPS)
        h_ref[...] = ((xv * r) * g_ref[...] * (1.0 + sc_ref[...]) + sh_ref[...]).astype(BF)

    ins = [x, g, sc, sh] + ([mix, gt] if has else [])
    in_specs = [_row(ts, d), _fix((1, d)), _fix((1, d)), _fix((1, d))] + ([_row(ts, d), _fix((1, d))] if has else [])
    out_shape = [jax.ShapeDtypeStruct((s, d), BF)] + ([jax.ShapeDtypeStruct((s, d), F32)] if has else [])
    out_specs = [_row(ts, d)] + ([_row(ts, d)] if has else [])
    out = pl.pallas_call(body, name=name, grid=(s // ts,), in_specs=in_specs, out_specs=out_specs,
                         out_shape=out_shape, compiler_params=_cp("parallel"))(*ins)
    return out if has else out[0]


def _norm_bwd(x, g, sc, sh, dh, dres, mix=None, gt=None, *, name, ts=256):
    s, d = x.shape
    has = mix is not None

    def body(*refs):
        if has:
            x_ref, g_ref, sc_ref, sh_ref, dh_ref, dres_ref, mix_ref, gt_ref, dx_ref, dmix_ref, part_ref = refs
        else:
            x_ref, g_ref, sc_ref, sh_ref, dh_ref, dres_ref, dx_ref, part_ref = refs
        i = pl.program_id(0)
        xv = x_ref[...]
        gv = g_ref[...]
        dhv = dh_ref[...]
        r = lax.rsqrt(jnp.mean(xv * xv, axis=-1, keepdims=True) + EPS)
        xh = xv * r
        dyn = dhv * (1.0 + sc_ref[...])
        gy = dyn * gv
        dx = dres_ref[...] + r * (gy - xh * jnp.mean(gy * xh, axis=-1, keepdims=True))
        dx_ref[...] = dx

        @pl.when(i == 0)
        def _():
            part_ref[...] = jnp.zeros_like(part_ref)

        part_ref[0:1, :] += jnp.sum(dhv * (xh * gv), axis=0, keepdims=True)
        part_ref[1:2, :] += jnp.sum(dhv, axis=0, keepdims=True)
        part_ref[2:3, :] += jnp.sum(dyn * xh, axis=0, keepdims=True)
        if has:
            part_ref[3:4, :] += jnp.sum(dx * mix_ref[...], axis=0, keepdims=True)
            dmix_ref[...] = (gt_ref[...] * dx).astype(BF)

    ins = [x, g, sc, sh, dh, dres] + ([mix, gt] if has else [])
    in_specs = [_row(ts, d), _fix((1, d)), _fix((1, d)), _fix((1, d)), _row(ts, d), _row(ts, d)]
    in_specs += [_row(ts, d), _fix((1, d))] if has else []
    out_shape = [jax.ShapeDtypeStruct((s, d), F32)] + ([jax.ShapeDtypeStruct((s, d), BF)] if has else [])
    out_shape += [jax.ShapeDtypeStruct((8, d), F32)]
    out_specs = [_row(ts, d)] + ([_row(ts, d)] if has else []) + [_fix((8, d))]
    return pl.pallas_call(body, name=name, grid=(s // ts,), in_specs=in_specs, out_specs=out_specs,
                          out_shape=out_shape, compiler_params=_cp("arbitrary"))(*ins)


def _final(x1, ffn, gt2, gf, tgt, *, ts=256):
    s, d = x1.shape

    def body(x1_ref, ffn_ref, gt_ref, g_ref, t_ref, dx_ref, dffn_ref, part_ref):
        i = pl.program_id(0)
        fv = ffn_ref[...]
        gv = g_ref[...]
        xv = x1_ref[...] + gt_ref[...] * fv
        r = lax.rsqrt(jnp.mean(xv * xv, axis=-1, keepdims=True) + EPS)
        xh = xv * r
        e = xh * gv - t_ref[...]
        dy = e * (1.0 / d)
        gy = dy * gv
        dx = r * (gy - xh * jnp.mean(gy * xh, axis=-1, keepdims=True))
        dx_ref[...] = dx
        dffn_ref[...] = (gt_ref[...] * dx).astype(BF)

        @pl.when(i == 0)
        def _():
            part_ref[...] = jnp.zeros_like(part_ref)

        part_ref[0:1, :] += jnp.sum(dy * xh, axis=0, keepdims=True)
        part_ref[1:2, :] += jnp.sum(dx * fv, axis=0, keepdims=True)
        part_ref[2:3, :] += jnp.sum(e * e, axis=0, keepdims=True) * (0.5 / d)

    return pl.pallas_call(
        body, name="final_loss", grid=(s // ts,),
        in_specs=[_row(ts, d), _row(ts, d), _fix((1, d)), _fix((1, d)), _row(ts, d)],
        out_specs=[_row(ts, d), _row(ts, d), _fix((8, d))],
        out_shape=[jax.ShapeDtypeStruct((s, d), F32), jax.ShapeDtypeStruct((s, d), BF),
                   jax.ShapeDtypeStruct((8, d), F32)],
        compiler_params=_cp("arbitrary"))(x1, ffn, gt2, gf, tgt)


def _gate_up(h, wg_t, wu_t, *, tm=1024, tn=512):
    s, k = h.shape
    n = wg_t.shape[0]
    tm = min(tm, s)

    def body(h_ref, wg_ref, wu_ref, g_ref, u_ref, a_ref):
        hv = h_ref[...]
        gv = lax.dot_general(hv, wg_ref[...], _DIMS["NT"], preferred_element_type=F32)
        uv = lax.dot_general(hv, wu_ref[...], _DIMS["NT"], preferred_element_type=F32)
        g_ref[...] = gv.astype(BF)
        u_ref[...] = uv.astype(BF)
        a_ref[...] = (gv * jax.nn.sigmoid(gv) * uv).astype(BF)

    wspec = pl.BlockSpec((tn, k), lambda i, j: (j, 0))
    ospec = pl.BlockSpec((tm, tn), lambda i, j: (i, j))
    return pl.pallas_call(body, name="gate_up", grid=(s // tm, n // tn),
                          in_specs=[pl.BlockSpec((tm, k), lambda i, j: (i, 0)), wspec, wspec],
                          out_specs=[ospec] * 3, out_shape=[jax.ShapeDtypeStruct((s, n), BF)] * 3,
                          compiler_params=_cp("parallel", "parallel"))(h, wg_t, wu_t)


def _gate_up_bwd(dffn, w_down, gate, up, *, tm=1024, tn=512):
    s, k = dffn.shape
    n = w_down.shape[0]
    tm = min(tm, s)

    def body(d_ref, w_ref, g_ref, u_ref, dg_ref, du_ref):
        dav = lax.dot_general(d_ref[...], w_ref[...], _DIMS["NT"], preferred_element_type=F32)
        gv = g_ref[...].astype(F32)
        uv = u_ref[...].astype(F32)
        sg = jax.nn.sigmoid(gv)
        dg_ref[...] = (dav * uv * _silu_grad(gv, sg)).astype(BF)
        du_ref[...] = (dav * gv * sg).astype(BF)

    tile = pl.BlockSpec((tm, tn), lambda i, j: (i, j))
    return pl.pallas_call(body, name="gate_up_bwd", grid=(s // tm, n // tn),
                          in_specs=[pl.BlockSpec((tm, k), lambda i, j: (i, 0)),
                                    pl.BlockSpec((tn, k), lambda i, j: (j, 0)), tile, tile],
                          out_specs=[tile] * 2, out_shape=[jax.ShapeDtypeStruct((s, n), BF)] * 2,
                          compiler_params=_cp("parallel", "parallel"))(dffn, w_down, gate, up)


def _mixout_fwd(att, y, proj, g_att, g_ssd, *, ts=256):
    s = att.shape[0]

    def body(a_ref, y_ref, z_ref, ga_ref, gs_ref, o_ref):
        av = a_ref[...]
        ra = lax.rsqrt(jnp.mean(av * av, axis=-1, keepdims=True) + EPS)
        o_ref[:, 0:AW] = (av * ra * ga_ref[...]).astype(BF)
        zv = z_ref[...]
        yz = y_ref[...] * (zv * jax.nn.sigmoid(zv))
        rs = lax.rsqrt(jnp.mean(yz * yz, axis=-1, keepdims=True) + EPS)
        o_ref[:, AW:AW + SW] = (yz * rs * gs_ref[...]).astype(BF)

    return pl.pallas_call(body, name="mixout_fwd", grid=(s // ts,),
                          in_specs=[_row(ts, AW), _row(ts, SW), _row(ts, SW, 3), _fix((1, AW)), _fix((1, SW))],
                          out_specs=_row(ts, AW + SW), out_shape=jax.ShapeDtypeStruct((s, AW + SW), BF),
                          compiler_params=_cp("parallel"))(att, y, proj, g_att, g_ssd)


def _mixout_bwd(dcat, att, y, proj, g_att, g_ssd, *, ts=256):
    s = att.shape[0]

    def body(dc_ref, a_ref, y_ref, z_ref, ga_ref, gs_ref, da_ref, dy_ref, dz_ref, part_ref):
        i = pl.program_id(0)
        av = a_ref[...]
        dca = dc_ref[:, 0:AW]
        ra = lax.rsqrt(jnp.mean(av * av, axis=-1, keepdims=True) + EPS)
        ah = av * ra
        gy = dca * ga_ref[...]
        da_ref[...] = ra * (gy - ah * jnp.mean(gy * ah, axis=-1, keepdims=True))
        zv = z_ref[...]
        yv = y_ref[...]
        sg = jax.nn.sigmoid(zv)
        sz = zv * sg
        yz = yv * sz
        dcs = dc_ref[:, AW:AW + SW]
        rs = lax.rsqrt(jnp.mean(yz * yz, axis=-1, keepdims=True) + EPS)
        yh = yz * rs
        gys = dcs * gs_ref[...]
        dyz = rs * (gys - yh * jnp.mean(gys * yh, axis=-1, keepdims=True))
        dy_ref[...] = dyz * sz
        dz_ref[...] = (dyz * yv * _silu_grad(zv, sg)).astype(BF)

        @pl.when(i == 0)
        def _():
            part_ref[...] = jnp.zeros_like(part_ref)

        part_ref[0:1, :] += jnp.sum(dca * ah, axis=0, keepdims=True)
        part_ref[1:2, :] += jnp.sum(dcs * yh, axis=0, keepdims=True)

    return pl.pallas_call(
        body, name="mixout_bwd", grid=(s // ts,),
        in_specs=[_row(ts, AW + SW), _row(ts, AW), _row(ts, SW), _row(ts, SW, 3), _fix((1, AW)), _fix((1, SW))],
        out_specs=[_row(ts, AW), _row(ts, SW), _row(ts, SW), _fix((8, AW))],
        out_shape=[jax.ShapeDtypeStruct((s, AW), F32), jax.ShapeDtypeStruct((s, SW), F32),
                   jax.ShapeDtypeStruct((s, SW), BF), jax.ShapeDtypeStruct((8, AW), F32)],
        compiler_params=_cp("arbitrary"))(dcat, att, y, proj, g_att, g_ssd)


SCALE = HD ** -0.5
QB = 256
NQC = QB // CH
WIN = BAND - CH + QB
NKB = WIN // QB


def _bias_window(bias):
    rows = [jnp.pad(bias, ((0, 0), (0, 0), (CH * a, WIN - BAND - CH * a)), constant_values=-jnp.inf)
            for a in range(NQC)]
    return jnp.concatenate(rows, axis=1)


def _win_specs(width, nb):
    def mk(col, back):
        return pl.BlockSpec((QB, width), lambda i, _c=col, _k=back: (jnp.maximum(jnp.minimum(i, nb - 1) - _k, 0), _c))
    return [mk(col, back) for col in (1, 2) for back in range(NKB - 1, -1, -1)]


def _attn2_fwd(qkv, bias_w):
    s = qkv.shape[0]
    nb = s // QB

    def body(q_ref, *refs):
        krefs, vrefs = refs[:NKB], refs[NKB:2 * NKB]
        b_ref, o_ref, lse_ref, kbuf, vbuf = refs[2 * NKB:]
        i = pl.program_id(0)
        for j in range(NKB):
            kbuf[QB * j:QB * j + QB, :] = krefs[j][...].astype(BF)
            vbuf[QB * j:QB * j + QB, :] = vrefs[j][...].astype(BF)
        valid = (i * QB - (NKB - 1) * QB + lax.broadcasted_iota(jnp.int32, (1, WIN), 1)) >= 0
        lane = lax.broadcasted_iota(jnp.int32, (QB, LANE), 1)

        lo = lane < HD

        def heads(masked):
            lse = jnp.zeros((QB, LANE), F32)
            for p in range(NH // 2):
                sl = slice(2 * HD * p, 2 * HD * p + 2 * HD)
                qp = q_ref[:, sl] * SCALE
                qbd = jnp.concatenate([jnp.where(lo, qp, 0.0), jnp.where(lo, 0.0, qp)], axis=0).astype(BF)
                sc = lax.dot_general(qbd, kbuf[:, sl], _DIMS["NT"], preferred_element_type=F32) + b_ref[p]
                if masked:
                    sc = jnp.where(valid, sc, -jnp.inf)
                mx = jnp.max(sc, axis=-1, keepdims=True)
                e = jnp.exp(sc - mx)
                tot = jnp.sum(e, axis=-1, keepdims=True)
                o2 = jnp.dot(e.astype(BF), vbuf[:, sl], preferred_element_type=F32) * (1.0 / tot)
                o_ref[:, sl] = jnp.where(lo, o2[0:QB], o2[QB:2 * QB])
                l2 = mx + jnp.log(tot)
                lse = jnp.where(lane == 2 * p, l2[0:QB], jnp.where(lane == 2 * p + 1, l2[QB:2 * QB], lse))
            lse_ref[...] = lse

        pl.when(i < NKB - 1)(functools.partial(heads, True))
        pl.when(i >= NKB - 1)(functools.partial(heads, False))

    return pl.pallas_call(
        body, name="attn_fwd", grid=(nb,),
        in_specs=[_row(QB, AW)] + _win_specs(AW, nb) + [_fix((NH // 2, 2 * QB, WIN), single=True)],
        out_specs=[_row(QB, AW), _row(QB, LANE)],
        out_shape=[jax.ShapeDtypeStruct((s, AW), F32), jax.ShapeDtypeStruct((s, LANE), F32)],
        scratch_shapes=[pltpu.VMEM((WIN, AW), BF), pltpu.VMEM((WIN, AW), BF)],
        compiler_params=_cp("parallel"))(*[qkv] * (1 + 2 * NKB), bias_w.reshape(NH // 2, 2 * QB, WIN))


def _attn2_bwd(qkv, datt, lse_t, bias_wt):
    s = qkv.shape[0]
    nb = s // QB
    cur = lambda i: jnp.minimum(i, nb - 1)
    late = lambda i: jnp.maximum(i - (NKB - 1), 0)

    def body(q_ref, *refs):
        krefs, vrefs = refs[:NKB], refs[NKB:2 * NKB]
        do_ref, lse_ref, b_ref, dq_ref, dk_ref, dv_ref, db_ref, kbuf, vbuf, dkacc, dvacc, dsbuf = refs[2 * NKB:]
        i = pl.program_id(0)

        @pl.when(i == 0)
        def _():
            dkacc[...] = jnp.zeros_like(dkacc)
            dvacc[...] = jnp.zeros_like(dvacc)
            db_ref[...] = jnp.zeros_like(db_ref)

        def heads(masked):
            for j in range(NKB):
                kbuf[QB * j:QB * j + QB, :] = krefs[j][...].astype(BF)
                vbuf[QB * j:QB * j + QB, :] = vrefs[j][...].astype(BF)
            valid = (i * QB - (NKB - 1) * QB + lax.broadcasted_iota(jnp.int32, (WIN, 1), 0)) >= 0
            lo = lax.broadcasted_iota(jnp.int32, (QB, LANE), 1) < HD

            def both(ref, sl):
                t = ref[:, sl]
                return jnp.concatenate([jnp.where(lo, t, 0.0), jnp.where(lo, 0.0, t)], axis=0)

            for p in range(NH // 2):
                sl = slice(2 * HD * p, 2 * HD * p + 2 * HD)
                qbd = (both(q_ref, sl) * SCALE).astype(BF)
                dobd = both(do_ref, sl).astype(BF)
                k = kbuf[:, sl]
                st = lax.dot_general(k, qbd, _DIMS["NT"], preferred_element_type=F32) + b_ref[p]
                if masked:
                    st = jnp.where(valid, st, -jnp.inf)
                lrow = jnp.concatenate([lse_ref[2 * p:2 * p + 1, :], lse_ref[2 * p + 1:2 * p + 2, :]], axis=1)
                pt = jnp.exp(st - lrow)
                dvacc[:, sl] += jnp.dot(pt.astype(BF), dobd, preferred_element_type=F32)
                dpt = lax.dot_general(vbuf[:, sl], dobd, _DIMS["NT"], preferred_element_type=F32)
                dst = pt * (dpt - jnp.sum(dpt * pt, axis=0, keepdims=True))
                dsbuf[...] = dst
                for hh in range(2):
                    fold = dsbuf[0:BAND, QB * hh:QB * hh + CH]
                    for a in range(1, NQC):
                        fold = fold + dsbuf[CH * a:CH * a + BAND, QB * hh + CH * a:QB * hh + CH * a + CH]
                    db_ref[2 * p + hh] += fold
                dsb = dst.astype(BF)
                dkacc[:, sl] += jnp.dot(dsb, qbd, preferred_element_type=F32)
                dq2 = lax.dot_general(dsb, k, _DIMS["TN"], preferred_element_type=F32) * SCALE
                dq_ref[:, sl] = jnp.where(lo, dq2[0:QB], dq2[QB:2 * QB]).astype(BF)

        pl.when(i < nb)(functools.partial(heads, True))

        dk_ref[...] = dkacc[0:QB, :].astype(BF)
        dv_ref[...] = dvacc[0:QB, :].astype(BF)
        for acc in (dkacc, dvacc):
            for j in range(NKB - 1):
                acc[QB * j:QB * j + QB, :] = acc[QB * j + QB:QB * j + 2 * QB, :]
            acc[WIN - QB:WIN, :] = jnp.zeros((QB, AW), F32)

    qspec = pl.BlockSpec((QB, AW), lambda i: (cur(i), 0))
    lspec = pl.BlockSpec((QB, AW), lambda i: (late(i), 0))
    return pl.pallas_call(
        body, name="attn_bwd", grid=(nb + NKB - 1,),
        in_specs=[qspec] + _win_specs(AW, nb) + [qspec, pl.BlockSpec((NH, QB), lambda i: (0, cur(i))),
                                                 _fix((NH // 2, WIN, 2 * QB), single=True)],
        out_specs=[qspec, lspec, lspec, _fix((NH, BAND, CH))],
        out_shape=[jax.ShapeDtypeStruct((s, AW), BF)] * 3 + [jax.ShapeDtypeStruct((NH, BAND, CH), F32)],
        scratch_shapes=[pltpu.VMEM((WIN, AW), BF), pltpu.VMEM((WIN, AW), BF),
                        pltpu.VMEM((WIN, AW), F32), pltpu.VMEM((WIN, AW), F32), pltpu.VMEM((WIN, 2 * QB), F32)],
        compiler_params=_cp("arbitrary"))(*[qkv] * (1 + 2 * NKB), datt, lse_t, bias_wt)


TB = 512
SCH = 512
NCB = TB // SCH
GW = 8 * HD


def _tri(lower):
    r = lax.broadcasted_iota(jnp.int32, (SCH, SCH), 0)
    c = lax.broadcasted_iota(jnp.int32, (SCH, SCH), 1)
    return r >= c if lower else r <= c


def _ssd_fwd(proj, conv_w, conv_b, dt_bias, a_log, d_skip):
    s = proj.shape[0]
    nb = s // TB

    def body(xs_ref, bc_ref, dt_ref, cw_ref, cb_ref, dtb_ref, al_ref, dsk_ref, y_ref, pre_ref, prev_ref,
             xpad, ubuf, dtbuf, csb, cstb, hst, xdec, yoffb, drow):
        i = pl.program_id(0)

        @pl.when(i == 0)
        def _():
            xpad[0:8, :] = jnp.zeros((8, XBC), F32)
            hst[...] = jnp.zeros_like(hst)

        xpad[8:8 + TB, 0:SW] = xs_ref[...]
        xpad[8:8 + TB, SW:XBC] = bc_ref[...]
        pre = cb_ref[...]
        for kk in range(4):
            pre = pre + cw_ref[kk:kk + 1, :] * xpad[5 + kk:5 + kk + TB, :]
        pre_ref[...] = pre
        ubuf[...] = pre * jax.nn.sigmoid(pre)
        xpad[0:8, :] = xpad[TB:TB + 8, :]
        dtbuf[...] = _softplus(dt_ref[...] + dtb_ref[...])
        a = -jnp.exp(al_ref[...])
        tri = _tri(True).astype(F32)
        causal = _tri(True)

        def chunk(c, carry):
            r0 = pl.multiple_of(c * SCH, SCH)
            rows = pl.ds(r0, SCH)
            cs = jnp.dot(tri, dtbuf[rows, :] * a, precision=HI, preferred_element_type=F32)
            csb[...] = cs
            cstb[...] = cs.T
            for g in range(NG):
                bg = ubuf[rows, SW + NS * g:SW + NS * g + NS]
                cg = ubuf[rows, SW + NG * NS + NS * g:SW + NG * NS + NS * g + NS].astype(BF)
                cb = lax.dot_general(cg, bg.astype(BF), (((1,), (1,)), ((), ())), preferred_element_type=F32)
                hg = hst[g]
                prev_ref[c, g] = hg
                yoffb[...] = jnp.dot(cg, hg.astype(BF), preferred_element_type=F32)
                for r in range(8):
                    h = 8 * g + r
                    sl = slice(HD * h, HD * h + HD)
                    rs = slice(HD * r, HD * r + HD)
                    cs_h = csb[:, h:h + 1]
                    cl = csb[SCH - 1:SCH, h:h + 1]
                    seg = jnp.exp(jnp.where(causal, cs_h - cstb[h:h + 1, :], -jnp.inf))
                    xh = ubuf[rows, sl]
                    xdt = xh * dtbuf[rows, h:h + 1]
                    yd = jnp.dot((cb * seg).astype(BF), xdt.astype(BF), preferred_element_type=F32)
                    y_ref[rows, sl] = yd + jnp.exp(cs_h) * yoffb[:, rs] + xh * dsk_ref[:, h:h + 1]
                    xdec[:, rs] = xdt * jnp.exp(cl - cs_h)
                    drow[:, rs] = jnp.broadcast_to(jnp.exp(cl), (1, HD))
                st = jnp.dot(bg.T.astype(BF), xdec[...].astype(BF), preferred_element_type=F32)
                hst[g] = hg * drow[...] + st
            return carry

        lax.fori_loop(0, NCB, chunk, 0)

    return pl.pallas_call(
        body, name="ssd_fwd", grid=(nb,),
        in_specs=[_row(TB, SW, 4), _row(TB, 512, 10), _row(TB, LANE, 44), _fix((4, XBC)), _fix((1, XBC)),
                  _fix((1, LANE)), _fix((1, LANE)), _fix((1, LANE))],
        out_specs=[_row(TB, SW), _row(TB, XBC), pl.BlockSpec((NCB, NG, NS, GW), lambda i: (i, 0, 0, 0))],
        out_shape=[jax.ShapeDtypeStruct((s, SW), F32), jax.ShapeDtypeStruct((s, XBC), F32),
                   jax.ShapeDtypeStruct((s // SCH, NG, NS, GW), F32)],
        scratch_shapes=[pltpu.VMEM((TB + 8, XBC), F32), pltpu.VMEM((TB, XBC), F32), pltpu.VMEM((TB, LANE), F32),
                        pltpu.VMEM((SCH, LANE), F32), pltpu.VMEM((LANE, SCH), F32), pltpu.VMEM((NG, NS, GW), F32),
                        pltpu.VMEM((SCH, GW), F32), pltpu.VMEM((SCH, GW), F32), pltpu.VMEM((1, GW), F32)],
        compiler_params=_cp("arbitrary"))(proj, proj, proj, conv_w, conv_b, dt_bias, a_log, d_skip)


def _ssd_bwd(dy, pre, proj, prev, conv_w, dt_bias, a_log, d_skip):
    s = dy.shape[0]
    nb = s // TB
    rev = lambda i: nb - 1 - i
    halo = lambda i: jnp.maximum((nb - 1 - i) * (TB // 8) - 1, 0)

    def body(dy_ref, pre_ref, dt_ref, prev_ref, xs_ref, bc_ref, xsh_ref, bch_ref, cw_ref, dtb_ref, al_ref, dsk_ref,
             dx_ref, ddt_ref, pw_ref, ph_ref,
             ubuf, dtbuf, dpad, xpad, csb, cstb, dhs, yoffb, ebuf, xdecb, wbuf, drow, dub):
        i = pl.program_id(0)

        @pl.when(i == 0)
        def _():
            dhs[...] = jnp.zeros_like(dhs)
            dpad[TB:TB + 8, :] = jnp.zeros((8, XBC), F32)
            pw_ref[...] = jnp.zeros_like(pw_ref)
            ph_ref[...] = jnp.zeros_like(ph_ref)

        pre = pre_ref[...]
        ubuf[...] = pre * jax.nn.sigmoid(pre)
        dtbuf[...] = _softplus(dt_ref[...] + dtb_ref[...])
        a = -jnp.exp(al_ref[...])
        tri = _tri(True).astype(F32)
        trit = _tri(False).astype(F32)
        causal = _tri(True)
        rowid = lax.broadcasted_iota(jnp.int32, (LANE, SCH), 0)
        lane = lax.broadcasted_iota(jnp.int32, (SCH, LANE), 1)
        lane1 = lax.broadcasted_iota(jnp.int32, (1, LANE), 1)
        lastrow = lax.broadcasted_iota(jnp.int32, (SCH, LANE), 0) == SCH - 1

        def chunk(cc, carry):
            c = NCB - 1 - cc
            r0 = pl.multiple_of(c * SCH, SCH)
            rows = pl.ds(r0, SCH)
            dtc = dtbuf[rows, :]
            cs = jnp.dot(tri, dtc * a, precision=HI, preferred_element_type=F32)
            csb[...] = cs
            cstb[...] = cs.T
            dcs = jnp.zeros((SCH, LANE), F32)
            dcst = jnp.zeros((LANE, SCH), F32)
            xr = jnp.zeros((SCH, LANE), F32)
            dlast = jnp.zeros((1, LANE), F32)
            dsk = jnp.zeros((1, LANE), F32)
            for g in range(NG):
                bg = ubuf[rows, SW + NS * g:SW + NS * g + NS]
                cg = ubuf[rows, SW + NG * NS + NS * g:SW + NG * NS + NS * g + NS]
                bb = bg.astype(BF)
                cbf = cg.astype(BF)
                cb = lax.dot_general(cbf, bb, (((1,), (1,)), ((), ())), preferred_element_type=F32)
                hp = prev_ref[c, g]
                hpb = hp.astype(BF)
                dhg = dhs[g]
                dhb = dhg.astype(BF)
                yoffb[...] = jnp.dot(cbf, hpb, preferred_element_type=F32)
                for r in range(8):
                    h = 8 * g + r
                    sl = slice(HD * h, HD * h + HD)
                    rs = slice(HD * r, HD * r + HD)
                    cs_h = csb[:, h:h + 1]
                    cl = csb[SCH - 1:SCH, h:h + 1]
                    dyh = dy_ref[rows, sl]
                    ecs = jnp.exp(cs_h)
                    ebuf[:, rs] = dyh * ecs
                    xdecb[:, rs] = ubuf[rows, sl] * dtbuf[rows, h:h + 1] * jnp.exp(cl - cs_h)
                    t1 = jnp.sum(dyh * ecs * yoffb[:, rs], axis=1, keepdims=True)
                    dcs = dcs + jnp.where(lane == h, t1, 0.0)
                    drow[:, rs] = jnp.broadcast_to(jnp.exp(cl), (1, HD))
                eb = ebuf[...].astype(BF)
                dc = lax.dot_general(eb, hpb, (((1,), (1,)), ((), ())), preferred_element_type=F32)
                dprev = jnp.dot(cg.T.astype(BF), eb, preferred_element_type=F32)
                wbuf[...] = jnp.dot(bb, dhb, preferred_element_type=F32)
                db = lax.dot_general(xdecb[...].astype(BF), dhb, (((1,), (1,)), ((), ())), preferred_element_type=F32)
                dcbs = jnp.zeros((SCH, SCH), F32)
                for r in range(8):
                    h = 8 * g + r
                    sl = slice(HD * h, HD * h + HD)
                    rs = slice(HD * r, HD * r + HD)
                    cs_h = csb[:, h:h + 1]
                    cst_h = cstb[h:h + 1, :]
                    cl = csb[SCH - 1:SCH, h:h + 1]
                    seg = jnp.exp(jnp.where(causal, cs_h - cst_h, -jnp.inf))
                    dyh = dy_ref[rows, sl]
                    xh = ubuf[rows, sl]
                    dt_h = dtbuf[rows, h:h + 1]
                    xdt = xh * dt_h
                    dyb = dyh.astype(BF)
                    dm = lax.dot_general(dyb, xdt.astype(BF), _DIMS["NT"], preferred_element_type=F32)
                    mm_ = cb * seg
                    dxdt = lax.dot_general(mm_.astype(BF), dyb, _DIMS["TN"], preferred_element_type=F32)
                    gg = dm * mm_
                    dcs_h = jnp.sum(gg, axis=1, keepdims=True)
                    dcst = dcst + jnp.where(rowid == h, jnp.sum(gg, axis=0, keepdims=True), 0.0)
                    dcbs = dcbs + dm * seg
                    dec = jnp.exp(cl - cs_h)
                    w_h = wbuf[:, rs]
                    dxdt = dxdt + dec * w_h
                    t = jnp.sum(w_h * xdt, axis=1, keepdims=True) * dec
                    dcs_h = dcs_h - t
                    dch = jnp.sum(jnp.sum(dhs[g, :, rs] * prev_ref[c, g, :, rs], axis=1, keepdims=True),
                                  axis=0, keepdims=True)
                    dl = jnp.sum(t, axis=0, keepdims=True) + dch * jnp.exp(cl)
                    dcs = dcs + jnp.where(lane == h, dcs_h, 0.0)
                    dlast = dlast + jnp.where(lane1 == h, dl, 0.0)
                    xr = xr + jnp.where(lane == h, jnp.sum(dxdt * xh, axis=1, keepdims=True), 0.0)
                    dskh = jnp.sum(jnp.sum(dyh * xh, axis=1, keepdims=True), axis=0, keepdims=True)
                    dsk = dsk + jnp.where(lane1 == h, dskh, 0.0)
                    dub[:, sl] = dyh * dsk_ref[:, h:h + 1] + dxdt * dt_h
                dc = dc + jnp.dot(dcbs.astype(BF), bb, preferred_element_type=F32)
                db = db + lax.dot_general(dcbs.astype(BF), cbf, _DIMS["TN"], preferred_element_type=F32)
                dub[:, SW + NS * g:SW + NS * g + NS] = db
                dub[:, SW + NG * NS + NS * g:SW + NG * NS + NS * g + NS] = dc
                dhs[g] = dprev + drow[...] * dhg
            dcs = dcs - dcst.T + jnp.where(lastrow, dlast, 0.0)
            dadt = jnp.dot(trit, dcs, precision=HI, preferred_element_type=F32)
            ddt = dadt * a + xr
            ddtr = ddt * jax.nn.sigmoid(dt_ref[rows, :] + dtb_ref[...])
            ddt_ref[rows, :] = ddtr.astype(BF)
            ph_ref[0:1, :] += jnp.sum(ddtr, axis=0, keepdims=True)
            ph_ref[1:2, :] += jnp.sum(dadt * dtc, axis=0, keepdims=True) * a
            ph_ref[2:3, :] += dsk
            pc = pre_ref[rows, :]
            sg = jax.nn.sigmoid(pc)
            dpad[rows, :] = dub[...] * _silu_grad(pc, sg)
            return carry

        lax.fori_loop(0, NCB, chunk, 0)

        dxb = cw_ref[0:1, :] * dpad[3:3 + TB, :]
        for kk in range(1, 4):
            dxb = dxb + cw_ref[kk:kk + 1, :] * dpad[3 - kk:3 - kk + TB, :]
        dx_ref[...] = dxb.astype(BF)
        keep = jnp.where(i < nb - 1, 1.0, 0.0)
        xpad[0:8, 0:SW] = xsh_ref[...] * keep
        xpad[0:8, SW:XBC] = bch_ref[...] * keep
        xpad[8:8 + TB, 0:SW] = xs_ref[...]
        xpad[8:8 + TB, SW:XBC] = bc_ref[...]
        dp = dpad[0:TB, :]
        for kk in range(4):
            pw_ref[kk:kk + 1, :] += jnp.sum(dp * xpad[5 + kk:5 + kk + TB, :], axis=0, keepdims=True)
        pw_ref[4:5, :] += jnp.sum(dp, axis=0, keepdims=True)
        dpad[TB:TB + 8, :] = dpad[0:8, :]

    rblk = lambda w, col: pl.BlockSpec((TB, w), lambda i, _c=col: (rev(i), _c))
    return pl.pallas_call(
        body, name="ssd_bwd", grid=(nb,),
        in_specs=[rblk(SW, 0), rblk(XBC, 0), rblk(LANE, 44),
                  pl.BlockSpec((NCB, NG, NS, GW), lambda i: (rev(i), 0, 0, 0)),
                  rblk(SW, 4), rblk(512, 10),
                  pl.BlockSpec((8, SW), lambda i: (halo(i), 4)), pl.BlockSpec((8, 512), lambda i: (halo(i), 10)),
                  _fix((4, XBC)), _fix((1, LANE)), _fix((1, LANE)), _fix((1, LANE))],
        out_specs=[rblk(XBC, 0), rblk(LANE, 0), _fix((8, XBC)), _fix((8, LANE))],
        out_shape=[jax.ShapeDtypeStruct((s, XBC), BF), jax.ShapeDtypeStruct((s, LANE), BF),
                   jax.ShapeDtypeStruct((8, XBC), F32), jax.ShapeDtypeStruct((8, LANE), F32)],
        scratch_shapes=[pltpu.VMEM((TB, XBC), F32), pltpu.VMEM((TB, LANE), F32), pltpu.VMEM((TB + 8, XBC), F32),
                        pltpu.VMEM((TB + 8, XBC), F32), pltpu.VMEM((SCH, LANE), F32), pltpu.VMEM((LANE, SCH), F32),
                        pltpu.VMEM((NG, NS, GW), F32), pltpu.VMEM((SCH, GW), F32), pltpu.VMEM((SCH, GW), F32),
                        pltpu.VMEM((SCH, GW), F32), pltpu.VMEM((SCH, GW), F32), pltpu.VMEM((1, GW), F32),
                        pltpu.VMEM((SCH, XBC), F32)],
        compiler_params=_cp("arbitrary"))(dy, pre, proj, prev, proj, proj, proj, proj, conv_w, dt_bias, a_log, d_skip)


def _rel_index():
    q = jnp.arange(CH)[:, None] + (BAND - CH)
    k = jnp.arange(BAND)[None, :]
    return jnp.clip(q - k, -(CH - 1), REL_CLIP) + (CH - 1)


def _local_step(x, tgt, mods, first_weights, late_weights, grads_ready,
                g_mix, rel_bias, conv_w, conv_b, dt_bias, a_log, d_skip, g_att, g_ssd, g_ffn, g_final):
    sh1, sc1, gt1, sh2, sc2, gt2 = mods
    pad16 = lambda v: jnp.pad(v, ((0, 0), (0, LANE - NH)))
    dtb_p, al_p, dsk_p = pad16(dt_bias), pad16(a_log), pad16(d_skip)
    ridx = _rel_index()
    ext = jnp.concatenate([jnp.broadcast_to(rel_bias[:, NREL - 1:], (NH, BAND + CH - 1 - NREL)),
                           rel_bias[:, ::-1]], axis=1)
    skew = jnp.tile(jnp.pad(ext, ((0, 0), (0, 1))), (1, CH + 1))[:, :CH * (BAND + CH + 1)]
    bias = skew.reshape(NH, CH, BAND + CH + 1)[:, ::-1, :BAND]
    bias_w = _bias_window(bias)
    bias_wt = jnp.transpose(bias_w.reshape(NH // 2, 2, QB, WIN), (0, 3, 1, 2)).reshape(NH // 2, WIN, 2 * QB)

    h1 = _norm_fwd(x, g_mix, sc1, sh1, name="norm1_fwd")
    onehot = (ridx.T.reshape(BAND * CH, 1) == jnp.arange(NREL)[None, :]).astype(F32)
    w_in_t = first_weights([h1, bias_w, bias_wt, onehot])
    proj = _mm([(h1, w_in_t)], "NT", F32, "in_proj", 1024, 1152, 2048)
    att, lse = _attn2_fwd(proj, bias_w)
    y, pre, prev = _ssd_fwd(proj, conv_w, conv_b, dtb_p, al_p, dsk_p)
    cat = _mixout_fwd(att, y, proj, g_att, g_ssd)
    (w_out,) = late_weights("out", cat)
    mix = _mm([(cat, w_out)], "NN", F32, "out_proj", 1024, 1024, 2048)
    h2, x1 = _norm_fwd(x, g_ffn, sc2, sh2, mix, gt1, name="norm2_fwd")
    w_gate_t, w_up_t = late_weights("gate_up", h2)
    gate, up, act = _gate_up(h2, w_gate_t, w_up_t)
    (w_down,) = late_weights("down", act)
    ffn = _mm([(act, w_down)], "NN", F32, "down_proj", 512, 1024, FF)
    dx2, dffn, pf = _final(x1, ffn, gt2, g_final, tgt)

    gw_down = _mm([(act, dffn)], "TN", BF, "gw_down", 1408, 2048, 1024)
    dgate, dup = _gate_up_bwd(dffn, w_down, gate, up)
    dh2 = _mm([(dgate, w_gate_t), (dup, w_up_t)], "NN", F32, "d_h2", 512, 512, FF)
    gw_gate_t = _mm([(dgate, h2)], "TN", BF, "gw_gate", 1408, 2048, 1024)
    gw_up_t = _mm([(dup, h2)], "TN", BF, "gw_up", 1408, 2048, 1024)
    tok = grads_ready("ffn", gw_gate_t, gw_up_t, gw_down)
    dx1, dmix, p2 = _norm_bwd(x1, g_ffn + tok[0, 0], sc2, sh2, dh2, dx2, mix, gt1, name="norm2_bwd")
    gw_out = _mm([(cat, dmix)], "TN", BF, "gw_out", 1024, 2048, 1024)
    tok = grads_ready("out", gw_out)
    dcat = _mm([(dmix, w_out)], "NT", F32, "d_cat", 1024, 1024, 2048, after=tok)
    datt, dy, dz, pm = _mixout_bwd(dcat, att, y, proj, g_att, g_ssd)
    dq, dk, dv, dbias_t = _attn2_bwd(proj, datt, lse[:, :NH].T, bias_wt)
    dxbc, ddt, pw, ph = _ssd_bwd(dy, pre, proj, prev, conv_w, dtb_p, al_p, dsk_p)
    dproj = jnp.concatenate([dq, dk, dv, dz, dxbc, ddt], axis=1)
    gw_in_t = _mm([(dproj, h1)], "TN", BF, "gw_in", 1152, 2048, 1024)
    tok = grads_ready("in", gw_in_t)
    dh1 = _mm([(dproj, w_in_t)], "NN", F32, "d_h1", 512, 1024, INP, after=tok)
    gx, p1 = _norm_bwd(x, g_mix, sc1, sh1, dh1, dx1, name="norm1_bwd")
    g_rel =_mm([(dbias_t.reshape(NH, BAND * CH), onehot)], "NN", F32, "g_rel_bias", NH, NREL, 4608, precision=HI)

    small = dict(
        dmods=jnp.concatenate([p1[1], p1[0], p2[3], p2[1], p2[0], pf[1]]),
        g_mix=p1[2], conv_b=pw[4], dt_bias=ph[0], a_log=ph[1], d_skip=ph[2], g_att=pm[0], g_ssd=pm[1],
        g_ffn=p2[2], g_final=pf[0], rel_bias=g_rel.reshape(-1), conv_w=pw[0:4].reshape(-1), loss=pf[2])
    return gx, small


def _exchange(srcs, gather, name):
    n = len(srcs)
    outs = [jax.ShapeDtypeStruct((NDEV,) + (s.shape if g else s.shape[1:]), s.dtype) for s, g in zip(srcs, gather)]
    outs.append(jax.ShapeDtypeStruct((8, LANE), F32))

    def body(*refs):
        src, dst, token = refs[:n], refs[n:2 * n], refs[2 * n]
        ssem, rsem, lsem = refs[2 * n + 1:]
        token[...] = jnp.zeros_like(token)
        x, y, c = lax.axis_index("x"), lax.axis_index("y"), lax.axis_index("c")
        me = 4 * x + 2 * y + c
        local = []
        for a in range(n):
            own = src[a] if gather[a] else src[a].at[me]
            cp = pltpu.make_async_copy(own, dst[a].at[me], lsem.at[a])
            cp.start()
            local.append(cp)
        remote = []
        for k in range(1, NDEV):
            px = 1 - x if k & 4 else x
            py = 1 - y if k & 2 else y
            pc = 1 - c if k & 1 else c
            pid = 4 * px + 2 * py + pc
            for a in range(n):
                s_ref = src[a] if gather[a] else src[a].at[pid]
                cp = pltpu.make_async_remote_copy(
                    src_ref=s_ref, dst_ref=dst[a].at[me], send_sem=ssem.at[a * (NDEV - 1) + k - 1],
                    recv_sem=rsem.at[a * (NDEV - 1) + k - 1], device_id=(px, py, pc), device_id_type=MESH)
                cp.start()
                remote.append(cp)
        for cp in remote:
            cp.wait()
        for cp in local:
            cp.wait()

    anyspec = pl.BlockSpec(memory_space=pl.ANY)
    return pl.pallas_call(
        body, name=name, out_shape=outs, in_specs=[anyspec] * n,
        out_specs=[anyspec] * n + [pl.BlockSpec(memory_space=pltpu.VMEM)],
        scratch_shapes=[pltpu.SemaphoreType.DMA((n * (NDEV - 1),)), pltpu.SemaphoreType.DMA((n * (NDEV - 1),)),
                        pltpu.SemaphoreType.DMA((n,))],
    )(*srcs)


_HBM = pl.BlockSpec(memory_space=pltpu.HBM)
_SEM = pl.BlockSpec(memory_space=pltpu.SEMAPHORE)
_EFFECT = pltpu.SideEffectType.DATAFLOW_SIDE_EFFECTING


def _peer_copies(src, land, ssem, rsem, gather):
    x, y, c = lax.axis_index("x"), lax.axis_index("y"), lax.axis_index("c")
    me = 4 * x + 2 * y + c
    copies = []
    for a in range(len(src)):
        for k in range(1, NDEV):
            px = 1 - x if k & 4 else x
            py = 1 - y if k & 2 else y
            pc = 1 - c if k & 1 else c
            s_ref = src[a] if gather[a] else src[a].at[4 * px + 2 * py + pc]
            idx = a * (NDEV - 1) + k - 1
            copies.append(pltpu.make_async_remote_copy(
                src_ref=s_ref, dst_ref=land[a].at[me], send_sem=ssem.at[idx], recv_sem=rsem.at[idx],
                device_id=(px, py, pc), device_id_type=MESH))
    return copies


def _xstart(srcs, gather, name):
    n = len(srcs)
    me = 4 * lax.axis_index("x") + 2 * lax.axis_index("y") + lax.axis_index("c")
    lands = []
    for s_, g_ in zip(srcs, gather):
        own = s_ if g_ else lax.dynamic_index_in_dim(s_, me, 0, keepdims=False)
        empty = lax.empty((NDEV,) + own.shape, own.dtype)
        lands.append(lax.dynamic_update_slice(empty, own[None], (me,) + (0,) * own.ndim))
    nsem = n * (NDEV - 1)

    def body(*refs):
        src, land, ssem, rsem, token = refs[:n], refs[n:2 * n], refs[2 * n], refs[2 * n + 1], refs[-1]
        for cp in _peer_copies(src, land, ssem, rsem, gather):
            cp.start()
        token[...] = jnp.zeros_like(token)

    ops = [pltpu.with_memory_space_constraint(a_, pltpu.HBM) for a_ in list(srcs) + lands]
    return pl.pallas_call(
        body, name=name,
        out_shape=(pltpu.SemaphoreType.DMA((nsem,)), pltpu.SemaphoreType.DMA((nsem,)),
                   *[pltpu.HBM(a_.shape, a_.dtype) for a_ in ops], jax.ShapeDtypeStruct((8, LANE), F32)),
        in_specs=[_HBM] * (2 * n),
        out_specs=(_SEM, _SEM, *[_HBM] * (2 * n), pl.BlockSpec(memory_space=pltpu.VMEM)),
        input_output_aliases={i: 2 + i for i in range(2 * n)},
        compiler_params=pltpu.CompilerParams(has_side_effects=_EFFECT),
    )(*ops)


def _xwait(started, gather, after, name):
    n = len(gather)
    ssem, rsem = started[0], started[1]
    bufs = started[2:2 + 2 * n]
    after = list(after) if isinstance(after, (list, tuple)) else [after]

    def body(*refs):
        src, land, ssem_, rsem_ = refs[:n], refs[n:2 * n], refs[2 * n], refs[2 * n + 1]
        for cp in _peer_copies(src, land, ssem_, rsem_, gather):
            cp.wait_send()
            cp.wait_recv()

    out = pl.pallas_call(
        body, name=name, out_shape=tuple(pltpu.HBM(a_.shape, a_.dtype) for a_ in bufs),
        in_specs=[_HBM] * (2 * n) + [_SEM, _SEM] + [pl.BlockSpec(memory_space=pl.ANY)] * len(after),
        out_specs=[_HBM] * (2 * n), input_output_aliases={i: i for i in range(2 * n)},
        compiler_params=pltpu.CompilerParams(has_side_effects=_EFFECT),
    )(*bufs, ssem, rsem, *after)
    return out[n:]


def _mods(c16, w_shard, b_shard, *, tn=512):
    n = w_shard.shape[1]

    def body(c_ref, w_ref, b_ref, o_ref):
        cv = c_ref[...]
        cond = (cv * jax.nn.sigmoid(cv)).astype(BF)
        o_ref[...] = jnp.dot(cond, w_ref[...].astype(BF), preferred_element_type=F32) + b_ref[...]

    return pl.pallas_call(body, name="ada_mods", grid=(n // tn,),
                          in_specs=[_fix((16, D)), pl.BlockSpec((D, tn), lambda j: (0, j)),
                                    pl.BlockSpec((1, tn), lambda j: (0, j))],
                          out_specs=pl.BlockSpec((16, tn), lambda j: (0, j)),
                          out_shape=jax.ShapeDtypeStruct((16, n), F32), compiler_params=_cp("parallel"))(c16, w_shard, b_shard)


def _gw_ada(c_t, dm, *, tr=256):
    n = dm.shape[1]

    def body(ct_ref, dm_ref, o_ref):
        acc = jnp.zeros((tr, n), F32)
        for b in range(NDEV):
            cv = ct_ref[:, b:b + 1]
            cond = (cv * jax.nn.sigmoid(cv)).astype(BF).astype(F32)
            acc = acc + cond * dm_ref[b:b + 1, :].astype(BF).astype(F32)
        o_ref[...] = acc

    return pl.pallas_call(body, name="gw_ada", grid=(D // tr,),
                          in_specs=[_row(tr, NDEV), _fix((NDEV, n))], out_specs=_row(tr, n),
                          out_shape=jax.ShapeDtypeStruct((D, n), F32), compiler_params=_cp("parallel"))(c_t, dm)


def _sum8(parts, *, name, tc=512):
    _, r, c = parts.shape
    tc = min(tc, c)

    def body(p_ref, o_ref):
        acc = p_ref[0].astype(F32)
        for d in range(1, NDEV):
            acc = acc + p_ref[d].astype(F32)
        o_ref[...] = acc

    return pl.pallas_call(body, name=name, grid=(c // tc,),
                          in_specs=[pl.BlockSpec((NDEV, r, tc), lambda j: (0, 0, j))],
                          out_specs=pl.BlockSpec((r, tc), lambda j: (0, j)),
                          out_shape=jax.ShapeDtypeStruct((r, c), F32), compiler_params=_cp("parallel"))(parts)


def _adam(w, m, v, *, recv=None, grad=None, tr=None, tc=None, name):
    lead = w.ndim == 3
    r, c = w.shape[-2:]
    summed = recv is not None
    assert tc is not None or r % tr == 0
    at = (0,) if lead else (Ellipsis,)

    def body(g_ref, w_ref, m_ref, v_ref, go_ref, do_ref, mo_ref, vo_ref):
        if summed:
            g = g_ref[0].astype(F32)
            for d in range(1, NDEV):
                g = g + g_ref[d].astype(F32)
        else:
            g = g_ref[...]
        go_ref[at] = g
        mn = B1 * m_ref[at] + (1.0 - B1) * g
        vn = B2 * v_ref[at] + (1.0 - B2) * (g * g)
        mo_ref[at] = mn
        vo_ref[at] = vn
        mh = mn / (1.0 - B1 ** STEP)
        vh = vn / (1.0 - B2 ** STEP)
        do_ref[at] = -LR * (mh / (jnp.sqrt(vh) + AEPS) + WD * w_ref[at])

    if tc is not None:
        assert not lead and c % tc == 0
        gspec = (pl.BlockSpec((NDEV, r, tc), lambda j: (0, 0, j)) if summed
                 else pl.BlockSpec((r, tc), lambda j: (0, j)))
        wspec = pl.BlockSpec((r, tc), lambda j: (0, j))
        return pl.pallas_call(body, name=name, grid=(c // tc,),
                              in_specs=[gspec, wspec, wspec, wspec], out_specs=[wspec] * 4,
                              out_shape=[jax.ShapeDtypeStruct(w.shape, F32)] * 4,
                              compiler_params=_cp("parallel"))(recv if summed else grad, w, m, v)
    gspec = pl.BlockSpec((NDEV, tr, c), lambda i: (0, i, 0)) if summed else _row(tr, c)
    wspec = pl.BlockSpec((1, tr, c), lambda i: (0, i, 0)) if lead else _row(tr, c)
    return pl.pallas_call(body, name=name, grid=(r // tr,),
                          in_specs=[gspec, wspec, wspec, wspec], out_specs=[wspec] * 4,
                          out_shape=[jax.ShapeDtypeStruct(w.shape, F32)] * 4,
                          compiler_params=_cp("parallel"))(recv if summed else grad, w, m, v)


def _blocks_to_cols(b):
    _, r, width = b.shape
    return jnp.transpose(b, (1, 0, 2)).reshape(r, NDEV * width)


_SMALL = ("b_ada", "g_mix", "rel_bias", "conv_w", "conv_b", "dt_bias", "a_log", "d_skip",
          "g_att_out", "g_ssd_out", "g_ffn", "g_final")


def _pad_rows(v):
    v = v.reshape(-1)
    return jnp.pad(v, (0, (-v.shape[0]) % LANE))


def kernel(x, c, w_ada, b_ada, g_mix, w_in, rel_bias, conv_w, conv_b, dt_bias, a_log, d_skip, g_att_out, g_ssd_out, w_out, g_ffn, w_gate, w_up, w_down, g_final, loss_target, m_w_ada, m_b_ada, m_g_mix, m_w_in, m_rel_bias, m_conv_w, m_conv_b, m_dt_bias, m_a_log, m_d_skip, m_g_att_out, m_g_ssd_out, m_w_out, m_g_ffn, m_w_gate, m_w_up, m_w_down, m_g_final, v_w_ada, v_b_ada, v_g_mix, v_w_in, v_rel_bias, v_conv_w, v_conv_b, v_dt_bias, v_a_log, v_d_skip, v_g_att_out, v_g_ssd_out, v_w_out, v_g_ffn, v_w_gate, v_w_up, v_w_down, v_g_final):
    me = 4 * lax.axis_index("x") + 2 * lax.axis_index("y") + lax.axis_index("c")
    wts = dict(w_ada=w_ada, b_ada=b_ada, g_mix=g_mix, w_in=w_in, rel_bias=rel_bias, conv_w=conv_w, conv_b=conv_b,
               dt_bias=dt_bias, a_log=a_log, d_skip=d_skip, g_att_out=g_att_out, g_ssd_out=g_ssd_out, w_out=w_out,
               g_ffn=g_ffn, w_gate=w_gate, w_up=w_up, w_down=w_down, g_final=g_final)
    mom = dict(w_ada=m_w_ada, b_ada=m_b_ada, g_mix=m_g_mix, w_in=m_w_in, rel_bias=m_rel_bias, conv_w=m_conv_w,
               conv_b=m_conv_b, dt_bias=m_dt_bias, a_log=m_a_log, d_skip=m_d_skip, g_att_out=m_g_att_out,
               g_ssd_out=m_g_ssd_out, w_out=m_w_out, g_ffn=m_g_ffn, w_gate=m_w_gate, w_up=m_w_up, w_down=m_w_down,
               g_final=m_g_final)
    var = dict(w_ada=v_w_ada, b_ada=v_b_ada, g_mix=v_g_mix, w_in=v_w_in, rel_bias=v_rel_bias, conv_w=v_conv_w,
               conv_b=v_conv_b, dt_bias=v_dt_bias, a_log=v_a_log, d_skip=v_d_skip, g_att_out=v_g_att_out,
               g_ssd_out=v_g_ssd_out, w_out=v_w_out, g_ffn=v_g_ffn, w_gate=v_w_gate, w_up=v_w_up, w_down=v_w_down,
               g_final=v_g_final)
    order = ("w_ada", "b_ada", "g_mix", "w_in", "rel_bias", "conv_w", "conv_b", "dt_bias", "a_log", "d_skip",
             "g_att_out", "g_ssd_out", "w_out", "g_ffn", "w_gate", "w_up", "w_down", "g_final")

    c_all, rel_all, cw_all, _ = _exchange([c, rel_bias[0], conv_w[0]], [True] * 3, "gather_small")
    c_all = c_all.reshape(NDEV, D)
    rel_full = _blocks_to_cols(rel_all)
    cw_full = _blocks_to_cols(cw_all)

    ncol = w_ada.shape[2]
    b_sh = lax.dynamic_slice(b_ada, (0, me * ncol), (1, ncol))
    mods_part = _mods(jnp.pad(c_all, ((0, 8), (0, 0))), w_ada[0], b_sh)[:NDEV]
    mods_all, tok_m = _exchange([mods_part], [True], "gather_mods")
    mods_me = lax.dynamic_index_in_dim(mods_all, me, axis=1, keepdims=False).reshape(6, 1, D)
    mods = [mods_me[i] for i in range(6)]

    st_in = _xstart([(w_in[0] + tok_m[0, 0]).T.astype(BF)], [True], "gather_w_in_start")
    st_late = dict(out=_xstart([(w_out[0] + st_in[-1][0, 0]).astype(BF)], [True], "gather_w_out_start"))
    st_late["gate_up"] = _xstart([(w_gate[0] + st_late["out"][-1][0, 0]).T.astype(BF), w_up[0].T.astype(BF)],
                                 [True] * 2, "gather_w_gate_up_start")
    st_late["down"] = _xstart([(w_down[0] + st_late["gate_up"][-1][0, 0]).astype(BF)], [True],
                              "gather_w_down_start")
    st_rest = st_late["down"]

    def first_weights(after):
        (gi,) = _xwait(st_in, [True], after, "gather_w_in_wait")
        return jnp.pad(gi.reshape(INC, D), ((0, INP - INC), (0, 0)))

    def late_weights(which, after):
        got = _xwait(st_late[which], [True] * (2 if which == "gate_up" else 1), after, f"gather_w_{which}_wait")
        return [g_.reshape(-1, D) for g_ in got]

    started = {}

    def grads_ready(which, *g):
        if which == "in":
            g = (g[0][:INC],)
        srcs = [g_.reshape(NDEV, g_.shape[0] // NDEV, D) for g_ in g]
        started[which] = _xstart(srcs, [False] * len(srcs), f"exchange_{which}_grads_start")
        return started[which][-1]

    gx, small = _local_step(
        x[0], loss_target[0], mods, first_weights, late_weights, grads_ready,
        g_mix + st_rest[-1][0, 0], rel_full, cw_full, conv_b, dt_bias, a_log, d_skip, g_att_out, g_ssd_out, g_ffn,
        g_final.reshape(1, D))

    loss_row = jnp.pad(jnp.sum(small["loss"]).reshape(1), (0, LANE - 1))
    pack = jnp.concatenate([small[k] for k in ("dmods", "g_mix", "conv_b", "dt_bias", "a_log", "d_skip", "g_att",
                                               "g_ssd", "g_ffn", "g_final", "rel_bias", "conv_w")] + [loss_row])
    packs, _ = _exchange([pack.reshape(-1, LANE)], [True], "gather_small_grads")
    rg, ru, rd = _xwait(started["ffn"], [False] * 3, packs, "exchange_ffn_grads_wait")
    (ro,) = _xwait(started["out"], [False], rg, "exchange_out_grads_wait")
    (ri,) = _xwait(started["in"], [False], ro, "exchange_in_grads_wait")
    big = dict(
        w_gate=[o_.T for o_ in _adam(w_gate[0].T, m_w_gate[0].T, v_w_gate[0].T, recv=rg, tc=512,
                                     name="adam_w_gate")],
        w_up=[o_.T for o_ in _adam(w_up[0].T, m_w_up[0].T, v_w_up[0].T, recv=ru, tc=512, name="adam_w_up")],
        w_down=_adam(w_down[0], m_w_down[0], v_w_down[0], recv=rd, tr=176, name="adam_w_down"))
    tot = _sum8(packs, name="sum_small").reshape(-1)

    sizes = dict(dmods=6 * D, g_mix=D, conv_b=XBC, dt_bias=LANE, a_log=LANE, d_skip=LANE, g_att_out=AW, g_ssd_out=SW,
                 g_ffn=D, g_final=D, rel_bias=NH * NREL, conv_w=4 * XBC, loss=LANE)
    off, o = {}, 0
    for k_, n_ in sizes.items():
        off[k_] = o
        o += n_
    take = lambda k_, n_=None: tot[off[k_]:off[k_] + (n_ or sizes[k_])]
    loss = tot[off["loss"]]
    rel_w = NREL // NDEV
    cw_w = XBC // NDEV
    gsm = dict(
        b_ada=take("dmods"), g_mix=take("g_mix"),
        rel_bias=lax.dynamic_slice(take("rel_bias").reshape(NH, NREL), (0, me * rel_w), (NH, rel_w)),
        conv_w=lax.dynamic_slice(take("conv_w").reshape(4, XBC), (0, me * cw_w), (4, cw_w)),
        conv_b=take("conv_b"), dt_bias=take("dt_bias", NH), a_log=take("a_log", NH), d_skip=take("d_skip", NH),
        g_att_out=take("g_att_out"), g_ssd_out=take("g_ssd_out"), g_ffn=take("g_ffn"), g_final=take("g_final"))

    flat = lambda d_: jnp.concatenate([_pad_rows(d_[k_]) for k_ in _SMALL])
    gflat = flat(gsm)
    nrow = -(-gflat.shape[0] // (8 * LANE)) * 8
    to2d = lambda v_: jnp.pad(v_, (0, nrow * LANE - v_.shape[0])).reshape(nrow, LANE)
    res = _adam(to2d(flat(wts)), to2d(flat(mom)), to2d(flat(var)), grad=to2d(gflat), tr=nrow, name="adam_small")
    outs = {k_: {} for k_ in ("grad", "delta", "m", "v")}
    o = 0
    for k_ in _SMALL:
        n_ = wts[k_].size
        for kind, arr in zip(("grad", "delta", "m", "v"), res):
            outs[kind][k_] = arr.reshape(-1)[o:o + n_].reshape(wts[k_].shape)
        o += n_ + (-n_) % LANE

    dm_all = packs[:, :6 * D // LANE, :].reshape(NDEV, 6 * D)
    dm_sh = lax.dynamic_slice(dm_all, (0, me * ncol), (NDEV, ncol))
    gwa = _gw_ada(c_all.T, dm_sh)
    big.update(
        w_ada=_adam(w_ada[0], m_w_ada[0], v_w_ada[0], grad=gwa, tr=256, name="adam_w_ada"),
        w_in=[o_.T for o_ in _adam(w_in[0].T, m_w_in[0].T, v_w_in[0].T, recv=ri, tc=512, name="adam_w_in")],
        w_out=_adam(w_out[0], m_w_out[0], v_w_out[0], recv=ro, tr=128, name="adam_w_out"))
    for k_, r_ in big.items():
        for kind, arr in zip(("grad", "delta", "m", "v"), r_):
            outs[kind][k_] = arr.reshape(wts[k_].shape)

    return (loss, gx.reshape(x.shape), *[outs["grad"][k_] for k_ in order], *[outs["delta"][k_] for k_ in order],
            *[outs["m"][k_] for k_ in order], *[outs["v"][k_] for k_ in order])
```

```python
import functools

import jax
import jax.numpy as jnp
from jax import lax
from jax.experimental import pallas as pl
from jax.experimental.pallas import tpu as pltpu

F32 = jnp.float32
BF = jnp.bfloat16
HI = lax.Precision.HIGHEST

D = 2048
NH = 16
HD = 64
AW = 1024
SW = 1024
NG = 2
NS = 128
XBC = 1536
CH = 64
BAND = 576
NREL = 320
REL_CLIP = 256
FF = 5632
INC = 5648
INP = 5760
EPS = 1e-6
NDEV = 8
LANE = 128
VMEM_LIMIT = 56 * 1024 * 1024
LR, B1, B2, AEPS, WD, STEP = 0.001, 0.9, 0.999, 1e-08, 0.01, 10
MESH = pl.DeviceIdType.MESH


def _cp(*sem):
    return pltpu.CompilerParams(dimension_semantics=sem, vmem_limit_bytes=VMEM_LIMIT)


def _row(ts, w, col=0):
    return pl.BlockSpec((ts, w), lambda i, _c=col: (i, _c))


def _fix(shape, single=False):
    nd = len(shape)
    if single:
        return pl.BlockSpec(shape, lambda i, _n=nd: (0,) * _n, pipeline_mode=pl.Buffered(1))
    return pl.BlockSpec(shape, lambda i, _n=nd: (0,) * _n)


def _silu_grad(x, s):
    return s * (1.0 + x * (1.0 - s))


def _softplus(x):
    return jnp.maximum(x, 0.0) + jnp.log1p(jnp.exp(-jnp.abs(x)))


_DIMS = {"NN": (((1,), (0,)), ((), ())), "TN": (((0,), (0,)), ((), ())), "NT": (((1,), (1,)), ((), ()))}


def _mm(pairs, mode, out_dtype, name, tm, tn, tk, precision=None, after=None):
    a0, b0 = pairs[0]
    m, k = a0.shape[::-1] if mode == "TN" else a0.shape
    n = b0.shape[0] if mode == "NT" else b0.shape[1]
    tm, tn, tk = min(tm, m), min(tn, n), min(tk, k)
    nk1 = k // tk
    npair = len(pairs)
    nk = 1 if nk1 == 1 else nk1 * npair
    assert m % tm == 0 and n % tn == 0 and k % tk == 0, (name, a0.shape, b0.shape)
    dims = _DIMS[mode]

    def kloc(kk, p):
        return jnp.clip(kk - p * nk1, 0, nk1 - 1)

    def aspec(p):
        if mode == "TN":
            return pl.BlockSpec((tk, tm), lambda i, j, kk, _p=p: (kloc(kk, _p), i))
        return pl.BlockSpec((tm, tk), lambda i, j, kk, _p=p: (i, kloc(kk, _p)))

    def bspec(p):
        if mode == "NT":
            return pl.BlockSpec((tn, tk), lambda i, j, kk, _p=p: (j, kloc(kk, _p)))
        return pl.BlockSpec((tk, tn), lambda i, j, kk, _p=p: (kloc(kk, _p), j))

    nin = 2 * npair + (after is not None)

    def body(*refs):
        ab, o_ref = refs[:2 * npair], refs[nin]
        if nk == 1:
            tot = lax.dot_general(ab[0][...], ab[1][...], dims, preferred_element_type=F32, precision=precision)
            for p in range(1, npair):
                tot = tot + lax.dot_general(ab[2 * p][...], ab[2 * p + 1][...], dims, preferred_element_type=F32,
                                            precision=precision)
            o_ref[...] = tot.astype(o_ref.dtype)
            return
        acc = refs[nin + 1]
        kk = pl.program_id(2)

        @pl.when(kk == 0)
        def _():
            acc[...] = jnp.zeros_like(acc)

        for p in range(npair):
            @pl.when((kk >= p * nk1) & (kk < (p + 1) * nk1))
            def _(p=p):
                acc[...] += lax.dot_general(ab[2 * p][...], ab[2 * p + 1][...], dims, preferred_element_type=F32,
                                            precision=precision)

        @pl.when(kk == nk - 1)
        def _():
            o_ref[...] = acc[...].astype(o_ref.dtype)

    in_specs = []
    for p in range(npair):
        in_specs += [aspec(p), bspec(p)]
    operands = [t for ab_ in pairs for t in ab_]
    if after is not None:
        in_specs.append(pl.BlockSpec(memory_space=pl.ANY))
        operands.append(after)
    return pl.pallas_call(
        body, name=name, grid=(m // tm, n // tn, nk), in_specs=in_specs,
        out_specs=pl.BlockSpec((tm, tn), lambda i, j, kk: (i, j)),
        out_shape=jax.ShapeDtypeStruct((m, n), out_dtype),
        scratch_shapes=[] if nk == 1 else [pltpu.VMEM((tm, tn), F32)],
        compiler_params=_cp("parallel", "parallel", "arbitrary"),
    )(*operands)


def _norm_fwd(x, g, sc, sh, mix=None, gt=None, *, name, ts=256):
    s, d = x.shape
    has = mix is not None

    def body(*refs):
        if has:
            x_ref, g_ref, sc_ref, sh_ref, mix_ref, gt_ref, h_ref, x1_ref = refs
        else:
            x_ref, g_ref, sc_ref, sh_ref, h_ref = refs
        xv = x_ref[...]
        if has:
            xv = xv + gt_ref[...] * mix_ref[...]
            x1_ref[...] = xv
        r = lax.rsqrt(jnp.mean(xv * xv, axis=-1, keepdims=True) + EPS)
        h_ref[...] = ((xv * r) * g_ref[...] * (1.0 + sc_ref[...]) + sh_ref[...]).astype(BF)

    ins = [x, g, sc, sh] + ([mix, gt] if has else [])
    in_specs = [_row(ts, d), _fix((1, d)), _fix((1, d)), _fix((1, d))] + ([_row(ts, d), _fix((1, d))] if has else [])
    out_shape = [jax.ShapeDtypeStruct((s, d), BF)] + ([jax.ShapeDtypeStruct((s, d), F32)] if has else [])
    out_specs = [_row(ts, d)] + ([_row(ts, d)] if has else [])
    out = pl.pallas_call(body, name=name, grid=(s // ts,), in_specs=in_specs, out_specs=out_specs,
                         out_shape=out_shape, compiler_params=_cp("parallel"))(*ins)
    return out if has else out[0]


def _norm_bwd(x, g, sc, sh, dh, dres, mix=None, gt=None, *, name, ts=256):
    s, d = x.shape
    has = mix is not None

    def body(*refs):
        if has:
            x_ref, g_ref, sc_ref, sh_ref, dh_ref, dres_ref, mix_ref, gt_ref, dx_ref, dmix_ref, part_ref = refs
        else:
            x_ref, g_ref, sc_ref, sh_ref, dh_ref, dres_ref, dx_ref, part_ref = refs
        i = pl.program_id(0)
        xv = x_ref[...]
        gv = g_ref[...]
        dhv = dh_ref[...].astype(F32)
        r = lax.rsqrt(jnp.mean(xv * xv, axis=-1, keepdims=True) + EPS)
        xh = xv * r
        dyn = dhv * (1.0 + sc_ref[...])
        gy = dyn * gv
        dx = dres_ref[...] + r * (gy - xh * jnp.mean(gy * xh, axis=-1, keepdims=True))
        dx_ref[...] = dx

        @pl.when(i == 0)
        def _():
            part_ref[...] = jnp.zeros_like(part_ref)

        part_ref[0:1, :] += jnp.sum(dhv * (xh * gv), axis=0, keepdims=True)
        part_ref[1:2, :] += jnp.sum(dhv, axis=0, keepdims=True)
        part_ref[2:3, :] += jnp.sum(dyn * xh, axis=0, keepdims=True)
        if has:
            part_ref[3:4, :] += jnp.sum(dx * mix_ref[...], axis=0, keepdims=True)
            dmix_ref[...] = (gt_ref[...] * dx).astype(BF)

    ins = [x, g, sc, sh, dh, dres] + ([mix, gt] if has else [])
    in_specs = [_row(ts, d), _fix((1, d)), _fix((1, d)), _fix((1, d)), _row(ts, d), _row(ts, d)]
    in_specs += [_row(ts, d), _fix((1, d))] if has else []
    out_shape = [jax.ShapeDtypeStruct((s, d), F32)] + ([jax.ShapeDtypeStruct((s, d), BF)] if has else [])
    out_shape += [jax.ShapeDtypeStruct((8, d), F32)]
    out_specs = [_row(ts, d)] + ([_row(ts, d)] if has else []) + [_fix((8, d))]
    return pl.pallas_call(body, name=name, grid=(s // ts,), in_specs=in_specs, out_specs=out_specs,
                          out_shape=out_shape, compiler_params=_cp("arbitrary"))(*ins)


def _final(x1, ffn, gt2, gf, tgt, *, ts=256):
    s, d = x1.shape

    def body(x1_ref, ffn_ref, gt_ref, g_ref, t_ref, dx_ref, dffn_ref, part_ref):
        i = pl.program_id(0)
        fv = ffn_ref[...]
        gv = g_ref[...]
        xv = x1_ref[...] + gt_ref[...] * fv
        r = lax.rsqrt(jnp.mean(xv * xv, axis=-1, keepdims=True) + EPS)
        xh = xv * r
        e = xh * gv - t_ref[...]
        dy = e * (1.0 / d)
        gy = dy * gv
        dx = r * (gy - xh * jnp.mean(gy * xh, axis=-1, keepdims=True))
        dx_ref[...] = dx
        dffn_ref[...] = (gt_ref[...] * dx).astype(BF)

        @pl.when(i == 0)
        def _():
            part_ref[...] = jnp.zeros_like(part_ref)

        part_ref[0:1, :] += jnp.sum(dy * xh, axis=0, keepdims=True)
        part_ref[1:2, :] += jnp.sum(dx * fv, axis=0, keepdims=True)
        part_ref[2:3, :] += jnp.sum(e * e, axis=0, keepdims=True) * (0.5 / d)

    return pl.pallas_call(
        body, name="final_loss", grid=(s // ts,),
        in_specs=[_row(ts, d), _row(ts, d), _fix((1, d)), _fix((1, d)), _row(ts, d)],
        out_specs=[_row(ts, d), _row(ts, d), _fix((8, d))],
        out_shape=[jax.ShapeDtypeStruct((s, d), F32), jax.ShapeDtypeStruct((s, d), BF),
                   jax.ShapeDtypeStruct((8, d), F32)],
        compiler_params=_cp("arbitrary"))(x1, ffn, gt2, gf, tgt)


def _gate_up(h, wg_t, wu_t, *, tm=1024, tn=512, sub=512):
    s, k = h.shape
    n = wg_t.shape[0]
    tm = min(tm, s)

    def body(h_ref, wg_ref, wu_ref, g_ref, u_ref, a_ref):
        hv = h_ref[...]
        for c0 in range(0, tn, sub):
            cs_ = slice(c0, c0 + sub)
            gv = lax.dot_general(hv, wg_ref[cs_, :], _DIMS["NT"], preferred_element_type=F32)
            uv = lax.dot_general(hv, wu_ref[cs_, :], _DIMS["NT"], preferred_element_type=F32)
            g_ref[:, cs_] = gv.astype(BF)
            u_ref[:, cs_] = uv.astype(BF)
            a_ref[:, cs_] = (gv * jax.nn.sigmoid(gv) * uv).astype(BF)

    wspec = pl.BlockSpec((tn, k), lambda i, j: (j, 0))
    ospec = pl.BlockSpec((tm, tn), lambda i, j: (i, j))
    return pl.pallas_call(body, name="gate_up", grid=(s // tm, n // tn),
                          in_specs=[pl.BlockSpec((tm, k), lambda i, j: (i, 0)), wspec, wspec],
                          out_specs=[ospec] * 3, out_shape=[jax.ShapeDtypeStruct((s, n), BF)] * 3,
                          compiler_params=_cp("parallel", "parallel"))(h, wg_t, wu_t)


def _gate_up_bwd(dffn, w_down, gate, up, *, tm=1024, tn=512, sub=512):
    s, k = dffn.shape
    n = w_down.shape[0]
    tm = min(tm, s)

    def body(d_ref, w_ref, g_ref, u_ref, dg_ref, du_ref):
        dv_ = d_ref[...]
        for c0 in range(0, tn, sub):
            cs_ = slice(c0, c0 + sub)
            dav = lax.dot_general(dv_, w_ref[cs_, :], _DIMS["NT"], preferred_element_type=F32)
            gv = g_ref[:, cs_].astype(F32)
            uv = u_ref[:, cs_].astype(F32)
            sg = jax.nn.sigmoid(gv)
            dg_ref[:, cs_] = (dav * uv * _silu_grad(gv, sg)).astype(BF)
            du_ref[:, cs_] = (dav * gv * sg).astype(BF)

    tile = pl.BlockSpec((tm, tn), lambda i, j: (i, j))
    return pl.pallas_call(body, name="gate_up_bwd", grid=(s // tm, n // tn),
                          in_specs=[pl.BlockSpec((tm, k), lambda i, j: (i, 0)),
                                    pl.BlockSpec((tn, k), lambda i, j: (j, 0)), tile, tile],
                          out_specs=[tile] * 2, out_shape=[jax.ShapeDtypeStruct((s, n), BF)] * 2,
                          compiler_params=_cp("parallel", "parallel"))(dffn, w_down, gate, up)


def _mixout_fwd(att, y, proj, g_att, g_ssd, *, ts=256):
    s = att.shape[0]

    def body(a_ref, y_ref, z_ref, ga_ref, gs_ref, o_ref):
        av = a_ref[...]
        ra = lax.rsqrt(jnp.mean(av * av, axis=-1, keepdims=True) + EPS)
        o_ref[:, 0:AW] = (av * ra * ga_ref[...]).astype(BF)
        zv = z_ref[...]
        yz = y_ref[...] * (zv * jax.nn.sigmoid(zv))
        rs = lax.rsqrt(jnp.mean(yz * yz, axis=-1, keepdims=True) + EPS)
        o_ref[:, AW:AW + SW] = (yz * rs * gs_ref[...]).astype(BF)

    return pl.pallas_call(body, name="mixout_fwd", grid=(s // ts,),
                          in_specs=[_row(ts, AW), _row(ts, SW), _row(ts, SW, 3), _fix((1, AW)), _fix((1, SW))],
                          out_specs=_row(ts, AW + SW), out_shape=jax.ShapeDtypeStruct((s, AW + SW), BF),
                          compiler_params=_cp("parallel"))(att, y, proj, g_att, g_ssd)


def _mixout_bwd(dcat, att, y, proj, g_att, g_ssd, *, ts=256):
    s = att.shape[0]

    def body(dc_ref, a_ref, y_ref, z_ref, ga_ref, gs_ref, da_ref, dy_ref, dz_ref, part_ref):
        i = pl.program_id(0)
        av = a_ref[...]
        dca = dc_ref[:, 0:AW].astype(F32)
        ra = lax.rsqrt(jnp.mean(av * av, axis=-1, keepdims=True) + EPS)
        ah = av * ra
        gy = dca * ga_ref[...]
        da_ref[...] = ra * (gy - ah * jnp.mean(gy * ah, axis=-1, keepdims=True))
        zv = z_ref[...]
        yv = y_ref[...]
        sg = jax.nn.sigmoid(zv)
        sz = zv * sg
        yz = yv * sz
        dcs = dc_ref[:, AW:AW + SW].astype(F32)
        rs = lax.rsqrt(jnp.mean(yz * yz, axis=-1, keepdims=True) + EPS)
        yh = yz * rs
        gys = dcs * gs_ref[...]
        dyz = rs * (gys - yh * jnp.mean(gys * yh, axis=-1, keepdims=True))
        dy_ref[...] = dyz * sz
        dz_ref[...] = (dyz * yv * _silu_grad(zv, sg)).astype(BF)

        @pl.when(i == 0)
        def _():
            part_ref[...] = jnp.zeros_like(part_ref)

        part_ref[0:1, :] += jnp.sum(dca * ah, axis=0, keepdims=True)
        part_ref[1:2, :] += jnp.sum(dcs * yh, axis=0, keepdims=True)

    return pl.pallas_call(
        body, name="mixout_bwd", grid=(s // ts,),
        in_specs=[_row(ts, AW + SW), _row(ts, AW), _row(ts, SW), _row(ts, SW, 3), _fix((1, AW)), _fix((1, SW))],
        out_specs=[_row(ts, AW), _row(ts, SW), _row(ts, SW), _fix((8, AW))],
        out_shape=[jax.ShapeDtypeStruct((s, AW), F32), jax.ShapeDtypeStruct((s, SW), F32),
                   jax.ShapeDtypeStruct((s, SW), BF), jax.ShapeDtypeStruct((8, AW), F32)],
        compiler_params=_cp("arbitrary"))(dcat, att, y, proj, g_att, g_ssd)


SCALE = HD ** -0.5
QB = 256
NQC = QB // CH
WIN = BAND - CH + QB
NKB = WIN // QB


def _bias_window(bias):
    rows = [jnp.pad(bias, ((0, 0), (0, 0), (CH * a, WIN - BAND - CH * a)), constant_values=-jnp.inf)
            for a in range(NQC)]
    return jnp.concatenate(rows, axis=1)


def _win_specs(width, nb):
    def mk(col, back):
        return pl.BlockSpec((QB, width), lambda i, _c=col, _k=back: (jnp.maximum(jnp.minimum(i, nb - 1) - _k, 0), _c))
    return [mk(col, back) for col in (1, 2) for back in range(NKB - 1, -1, -1)]


def _attn2_fwd(qkv, bias_w):
    s = qkv.shape[0]
    nb = s // QB

    def body(q_ref, *refs):
        krefs, vrefs = refs[:NKB], refs[NKB:2 * NKB]
        b_ref, o_ref, lse_ref, kbuf, vbuf = refs[2 * NKB:]
        i = pl.program_id(0)
        for j in range(NKB):
            kbuf[QB * j:QB * j + QB, :] = krefs[j][...].astype(BF)
            vbuf[QB * j:QB * j + QB, :] = vrefs[j][...].astype(BF)
        valid = (i * QB - (NKB - 1) * QB + lax.broadcasted_iota(jnp.int32, (1, WIN), 1)) >= 0
        lane = lax.broadcasted_iota(jnp.int32, (QB, LANE), 1)

        lo = lane < HD

        def heads(masked):
            lse = jnp.zeros((QB, LANE), F32)
            for p in range(NH // 2):
                sl = slice(2 * HD * p, 2 * HD * p + 2 * HD)
                qp = q_ref[:, sl] * SCALE
                qbd = jnp.concatenate([jnp.where(lo, qp, 0.0), jnp.where(lo, 0.0, qp)], axis=0).astype(BF)
                sc = lax.dot_general(qbd, kbuf[:, sl], _DIMS["NT"], preferred_element_type=F32) + b_ref[p]
                if masked:
                    sc = jnp.where(valid, sc, -jnp.inf)
                mx = jnp.max(sc, axis=-1, keepdims=True)
                e = jnp.exp(sc - mx)
                tot = jnp.sum(e, axis=-1, keepdims=True)
                o2 = jnp.dot(e.astype(BF), vbuf[:, sl], preferred_element_type=F32) * (1.0 / tot)
                o_ref[:, sl] = jnp.where(lo, o2[0:QB], o2[QB:2 * QB])
                l2 = mx + jnp.log(tot)
                lse = jnp.where(lane == 2 * p, l2[0:QB], jnp.where(lane == 2 * p + 1, l2[QB:2 * QB], lse))
            lse_ref[...] = lse

        pl.when(i < NKB - 1)(functools.partial(heads, True))
        pl.when(i >= NKB - 1)(functools.partial(heads, False))

    return pl.pallas_call(
        body, name="attn_fwd", grid=(nb,),
        in_specs=[_row(QB, AW)] + _win_specs(AW, nb) + [_fix((NH // 2, 2 * QB, WIN), single=True)],
        out_specs=[_row(QB, AW), _row(QB, LANE)],
        out_shape=[jax.ShapeDtypeStruct((s, AW), F32), jax.ShapeDtypeStruct((s, LANE), F32)],
        scratch_shapes=[pltpu.VMEM((WIN, AW), BF), pltpu.VMEM((WIN, AW), BF)],
        compiler_params=_cp("parallel"))(*[qkv] * (1 + 2 * NKB), bias_w.reshape(NH // 2, 2 * QB, WIN))


def _attn2_bwd(qkv, datt, lse_t, bias_wt):
    s = qkv.shape[0]
    nb = s // QB
    cur = lambda i: jnp.minimum(i, nb - 1)
    late = lambda i: jnp.maximum(i - (NKB - 1), 0)

    def body(q_ref, *refs):
        krefs, vrefs = refs[:NKB], refs[NKB:2 * NKB]
        do_ref, lse_ref, b_ref, dq_ref, dk_ref, dv_ref, db_ref, kbuf, vbuf, dkacc, dvacc, dsbuf = refs[2 * NKB:]
        i = pl.program_id(0)

        @pl.when(i == 0)
        def _():
            dkacc[...] = jnp.zeros_like(dkacc)
            dvacc[...] = jnp.zeros_like(dvacc)
            db_ref[...] = jnp.zeros_like(db_ref)

        def heads(masked):
            for j in range(NKB):
                kbuf[QB * j:QB * j + QB, :] = krefs[j][...].astype(BF)
                vbuf[QB * j:QB * j + QB, :] = vrefs[j][...].astype(BF)
            valid = (i * QB - (NKB - 1) * QB + lax.broadcasted_iota(jnp.int32, (WIN, 1), 0)) >= 0
            lo = lax.broadcasted_iota(jnp.int32, (QB, LANE), 1) < HD

            def both(ref, sl):
                t = ref[:, sl]
                return jnp.concatenate([jnp.where(lo, t, 0.0), jnp.where(lo, 0.0, t)], axis=0)

            for p in range(NH // 2):
                sl = slice(2 * HD * p, 2 * HD * p + 2 * HD)
                qbd = (both(q_ref, sl) * SCALE).astype(BF)
                dobd = both(do_ref, sl).astype(BF)
                k = kbuf[:, sl]
                st = lax.dot_general(k, qbd, _DIMS["NT"], preferred_element_type=F32) + b_ref[p]
                if masked:
                    st = jnp.where(valid, st, -jnp.inf)
                lrow = jnp.concatenate([lse_ref[2 * p:2 * p + 1, :], lse_ref[2 * p + 1:2 * p + 2, :]], axis=1)
                pt = jnp.exp(st - lrow)
                dvacc[:, sl] += jnp.dot(pt.astype(BF), dobd, preferred_element_type=F32)
                dpt = lax.dot_general(vbuf[:, sl], dobd, _DIMS["NT"], preferred_element_type=F32)
                dst = pt * (dpt - jnp.sum(dpt * pt, axis=0, keepdims=True))
                dsbuf[...] = dst
                for hh in range(2):
                    fold = dsbuf[0:BAND, QB * hh:QB * hh + CH]
                    for a in range(1, NQC):
                        fold = fold + dsbuf[CH * a:CH * a + BAND, QB * hh + CH * a:QB * hh + CH * a + CH]
                    db_ref[2 * p + hh] += fold
                dsb = dst.astype(BF)
                dkacc[:, sl] += jnp.dot(dsb, qbd, preferred_element_type=F32)
                dq2 = lax.dot_general(dsb, k, _DIMS["TN"], preferred_element_type=F32) * SCALE
                dq_ref[:, sl] = jnp.where(lo, dq2[0:QB], dq2[QB:2 * QB]).astype(BF)

        pl.when(i < nb)(functools.partial(heads, True))

        dk_ref[...] = dkacc[0:QB, :].astype(BF)
        dv_ref[...] = dvacc[0:QB, :].astype(BF)
        for acc in (dkacc, dvacc):
            for j in range(NKB - 1):
                acc[QB * j:QB * j + QB, :] = acc[QB * j + QB:QB * j + 2 * QB, :]
            acc[WIN - QB:WIN, :] = jnp.zeros((QB, AW), F32)

    qspec = pl.BlockSpec((QB, AW), lambda i: (cur(i), 0))
    lspec = pl.BlockSpec((QB, AW), lambda i: (late(i), 0))
    return pl.pallas_call(
        body, name="attn_bwd", grid=(nb + NKB - 1,),
        in_specs=[qspec] + _win_specs(AW, nb) + [qspec, pl.BlockSpec((NH, QB), lambda i: (0, cur(i))),
                                                 _fix((NH // 2, WIN, 2 * QB), single=True)],
        out_specs=[qspec, lspec, lspec, _fix((NH, BAND, CH))],
        out_shape=[jax.ShapeDtypeStruct((s, AW), BF)] * 3 + [jax.ShapeDtypeStruct((NH, BAND, CH), F32)],
        scratch_shapes=[pltpu.VMEM((WIN, AW), BF), pltpu.VMEM((WIN, AW), BF),
                        pltpu.VMEM((WIN, AW), F32), pltpu.VMEM((WIN, AW), F32), pltpu.VMEM((WIN, 2 * QB), F32)],
        compiler_params=_cp("arbitrary"))(*[qkv] * (1 + 2 * NKB), datt, lse_t, bias_wt)


TB = 512
SCH = 512
NCB = TB // SCH
GW = 8 * HD


def _tri(lower):
    r = lax.broadcasted_iota(jnp.int32, (SCH, SCH), 0)
    c = lax.broadcasted_iota(jnp.int32, (SCH, SCH), 1)
    return r >= c if lower else r <= c


def _ssd_fwd(proj, conv_w, conv_b, dt_bias, a_log, d_skip):
    s = proj.shape[0]
    nb = s // TB

    def body(xs_ref, bc_ref, dt_ref, cw_ref, cb_ref, dtb_ref, al_ref, dsk_ref, y_ref, pre_ref, prev_ref,
             xpad, ubuf, dtbuf, csb, cstb, hst, xdec, yoffb, drow):
        i = pl.program_id(0)

        @pl.when(i == 0)
        def _():
            xpad[0:8, :] = jnp.zeros((8, XBC), F32)
            hst[...] = jnp.zeros_like(hst)

        xpad[8:8 + TB, 0:SW] = xs_ref[...]
        xpad[8:8 + TB, SW:XBC] = bc_ref[...]
        pre = cb_ref[...]
        for kk in range(4):
            pre = pre + cw_ref[kk:kk + 1, :] * xpad[5 + kk:5 + kk + TB, :]
        pre_ref[...] = pre
        ubuf[...] = pre * jax.nn.sigmoid(pre)
        xpad[0:8, :] = xpad[TB:TB + 8, :]
        dtbuf[...] = _softplus(dt_ref[...] + dtb_ref[...])
        a = -jnp.exp(al_ref[...])
        tri = _tri(True).astype(F32)
        causal = _tri(True)

        def chunk(c, carry):
            r0 = pl.multiple_of(c * SCH, SCH)
            rows = pl.ds(r0, SCH)
            cs = jnp.dot(tri, dtbuf[rows, :] * a, precision=HI, preferred_element_type=F32)
            csb[...] = cs
            cstb[...] = cs.T
            for g in range(NG):
                bg = ubuf[rows, SW + NS * g:SW + NS * g + NS]
                cg = ubuf[rows, SW + NG * NS + NS * g:SW + NG * NS + NS * g + NS].astype(BF)
                cb = lax.dot_general(cg, bg.astype(BF), (((1,), (1,)), ((), ())), preferred_element_type=F32)
                hg = hst[g]
                prev_ref[c, g] = hg
                yoffb[...] = jnp.dot(cg, hg.astype(BF), preferred_element_type=F32)
                for r in range(8):
                    h = 8 * g + r
                    sl = slice(HD * h, HD * h + HD)
                    rs = slice(HD * r, HD * r + HD)
                    cs_h = csb[:, h:h + 1]
                    cl = csb[SCH - 1:SCH, h:h + 1]
                    seg = jnp.exp(jnp.where(causal, cs_h - cstb[h:h + 1, :], -jnp.inf))
                    xh = ubuf[rows, sl]
                    xdt = xh * dtbuf[rows, h:h + 1]
                    yd = jnp.dot((cb * seg).astype(BF), xdt.astype(BF), preferred_element_type=F32)
                    y_ref[rows, sl] = yd + jnp.exp(cs_h) * yoffb[:, rs] + xh * dsk_ref[:, h:h + 1]
                    xdec[:, rs] = xdt * jnp.exp(cl - cs_h)
                    drow[:, rs] = jnp.broadcast_to(jnp.exp(cl), (1, HD))
                st = jnp.dot(bg.T.astype(BF), xdec[...].astype(BF), preferred_element_type=F32)
                hst[g] = hg * drow[...] + st
            return carry

        lax.fori_loop(0, NCB, chunk, 0)

    return pl.pallas_call(
        body, name="ssd_fwd", grid=(nb,),
        in_specs=[_row(TB, SW, 4), _row(TB, 512, 10), _row(TB, LANE, 44), _fix((4, XBC)), _fix((1, XBC)),
                  _fix((1, LANE)), _fix((1, LANE)), _fix((1, LANE))],
        out_specs=[_row(TB, SW), _row(TB, XBC), pl.BlockSpec((NCB, NG, NS, GW), lambda i: (i, 0, 0, 0))],
        out_shape=[jax.ShapeDtypeStruct((s, SW), F32), jax.ShapeDtypeStruct((s, XBC), F32),
                   jax.ShapeDtypeStruct((s // SCH, NG, NS, GW), F32)],
        scratch_shapes=[pltpu.VMEM((TB + 8, XBC), F32), pltpu.VMEM((TB, XBC), F32), pltpu.VMEM((TB, LANE), F32),
                        pltpu.VMEM((SCH, LANE), F32), pltpu.VMEM((LANE, SCH), F32), pltpu.VMEM((NG, NS, GW), F32),
                        pltpu.VMEM((SCH, GW), F32), pltpu.VMEM((SCH, GW), F32), pltpu.VMEM((1, GW), F32)],
        compiler_params=_cp("arbitrary"))(proj, proj, proj, conv_w, conv_b, dt_bias, a_log, d_skip)


def _ssd_bwd(dy, pre, proj, prev, conv_w, dt_bias, a_log, d_skip):
    s = dy.shape[0]
    nb = s // TB
    rev = lambda i: nb - 1 - i
    halo = lambda i: jnp.maximum((nb - 1 - i) * (TB // 8) - 1, 0)

    def body(dy_ref, pre_ref, dt_ref, prev_ref, xs_ref, bc_ref, xsh_ref, bch_ref, cw_ref, dtb_ref, al_ref, dsk_ref,
             dx_ref, ddt_ref, pw_ref, ph_ref,
             ubuf, dtbuf, dpad, xpad, csb, cstb, dhs, yoffb, ebuf, xdecb, wbuf, drow, dub):
        i = pl.program_id(0)

        @pl.when(i == 0)
        def _():
            dhs[...] = jnp.zeros_like(dhs)
            dpad[TB:TB + 8, :] = jnp.zeros((8, XBC), F32)
            pw_ref[...] = jnp.zeros_like(pw_ref)
            ph_ref[...] = jnp.zeros_like(ph_ref)

        pre = pre_ref[...]
        ubuf[...] = pre * jax.nn.sigmoid(pre)
        dtbuf[...] = _softplus(dt_ref[...] + dtb_ref[...])
        a = -jnp.exp(al_ref[...])
        tri = _tri(True).astype(F32)
        trit = _tri(False).astype(F32)
        causal = _tri(True)
        rowid = lax.broadcasted_iota(jnp.int32, (LANE, SCH), 0)
        lane = lax.broadcasted_iota(jnp.int32, (SCH, LANE), 1)
        lane1 = lax.broadcasted_iota(jnp.int32, (1, LANE), 1)
        lastrow = lax.broadcasted_iota(jnp.int32, (SCH, LANE), 0) == SCH - 1

        def chunk(cc, carry):
            c = NCB - 1 - cc
            r0 = pl.multiple_of(c * SCH, SCH)
            rows = pl.ds(r0, SCH)
            dtc = dtbuf[rows, :]
            cs = jnp.dot(tri, dtc * a, precision=HI, preferred_element_type=F32)
            csb[...] = cs
            cstb[...] = cs.T
            dcs = jnp.zeros((SCH, LANE), F32)
            dcst = jnp.zeros((LANE, SCH), F32)
            xr = jnp.zeros((SCH, LANE), F32)
            dlast = jnp.zeros((1, LANE), F32)
            dsk = jnp.zeros((1, LANE), F32)
            for g in range(NG):
                bg = ubuf[rows, SW + NS * g:SW + NS * g + NS]
                cg = ubuf[rows, SW + NG * NS + NS * g:SW + NG * NS + NS * g + NS]
                bb = bg.astype(BF)
                cbf = cg.astype(BF)
                cb = lax.dot_general(cbf, bb, (((1,), (1,)), ((), ())), preferred_element_type=F32)
                hp = prev_ref[c, g]
                hpb = hp.astype(BF)
                dhg = dhs[g]
                dhb = dhg.astype(BF)
                yoffb[...] = jnp.dot(cbf, hpb, preferred_element_type=F32)
                for r in range(8):
                    h = 8 * g + r
                    sl = slice(HD * h, HD * h + HD)
                    rs = slice(HD * r, HD * r + HD)
                    cs_h = csb[:, h:h + 1]
                    cl = csb[SCH - 1:SCH, h:h + 1]
                    dyh = dy_ref[rows, sl]
                    ecs = jnp.exp(cs_h)
                    ebuf[:, rs] = dyh * ecs
                    xdecb[:, rs] = ubuf[rows, sl] * dtbuf[rows, h:h + 1] * jnp.exp(cl - cs_h)
                    t1 = jnp.sum(dyh * ecs * yoffb[:, rs], axis=1, keepdims=True)
                    dcs = dcs + jnp.where(lane == h, t1, 0.0)
                    drow[:, rs] = jnp.broadcast_to(jnp.exp(cl), (1, HD))
                eb = ebuf[...].astype(BF)
                dc = lax.dot_general(eb, hpb, (((1,), (1,)), ((), ())), preferred_element_type=F32)
                dprev = jnp.dot(cg.T.astype(BF), eb, preferred_element_type=F32)
                wbuf[...] = jnp.dot(bb, dhb, preferred_element_type=F32)
                db = lax.dot_general(xdecb[...].astype(BF), dhb, (((1,), (1,)), ((), ())), preferred_element_type=F32)
                dcbs = jnp.zeros((SCH, SCH), F32)
                for r in range(8):
                    h = 8 * g + r
                    sl = slice(HD * h, HD * h + HD)
                    rs = slice(HD * r, HD * r + HD)
                    cs_h = csb[:, h:h + 1]
                    cst_h = cstb[h:h + 1, :]
                    cl = csb[SCH - 1:SCH, h:h + 1]
                    seg = jnp.exp(jnp.where(causal, cs_h - cst_h, -jnp.inf))
                    dyh = dy_ref[rows, sl]
                    xh = ubuf[rows, sl]
                    dt_h = dtbuf[rows, h:h + 1]
                    xdt = xh * dt_h
                    dyb = dyh.astype(BF)
                    dm = lax.dot_general(dyb, xdt.astype(BF), _DIMS["NT"], preferred_element_type=F32)
                    mm_ = cb * seg
                    dxdt = lax.dot_general(mm_.astype(BF), dyb, _DIMS["TN"], preferred_element_type=F32)
                    gg = dm * mm_
                    dcs_h = jnp.sum(gg, axis=1, keepdims=True)
                    dcst = dcst + jnp.where(rowid == h, jnp.sum(gg, axis=0, keepdims=True), 0.0)
                    dcbs = dcbs + dm * seg
                    dec = jnp.exp(cl - cs_h)
                    w_h = wbuf[:, rs]
                    dxdt = dxdt + dec * w_h
                    t = jnp.sum(w_h * xdt, axis=1, keepdims=True) * dec
                    dcs_h = dcs_h - t
                    dch = jnp.sum(jnp.sum(dhs[g, :, rs] * prev_ref[c, g, :, rs], axis=1, keepdims=True),
                                  axis=0, keepdims=True)
                    dl = jnp.sum(t, axis=0, keepdims=True) + dch * jnp.exp(cl)
                    dcs = dcs + jnp.where(lane == h, dcs_h, 0.0)
                    dlast = dlast + jnp.where(lane1 == h, dl, 0.0)
                    xr = xr + jnp.where(lane == h, jnp.sum(dxdt * xh, axis=1, keepdims=True), 0.0)
                    dskh = jnp.sum(jnp.sum(dyh * xh, axis=1, keepdims=True), axis=0, keepdims=True)
                    dsk = dsk + jnp.where(lane1 == h, dskh, 0.0)
                    dub[:, sl] = dyh * dsk_ref[:, h:h + 1] + dxdt * dt_h
                dc = dc + jnp.dot(dcbs.astype(BF), bb, preferred_element_type=F32)
                db = db + lax.dot_general(dcbs.astype(BF), cbf, _DIMS["TN"], preferred_element_type=F32)
                dub[:, SW + NS * g:SW + NS * g + NS] = db
                dub[:, SW + NG * NS + NS * g:SW + NG * NS + NS * g + NS] = dc
                dhs[g] = dprev + drow[...] * dhg
            dcs = dcs - dcst.T + jnp.where(lastrow, dlast, 0.0)
            dadt = jnp.dot(trit, dcs, precision=HI, preferred_element_type=F32)
            ddt = dadt * a + xr
            ddtr = ddt * jax.nn.sigmoid(dt_ref[rows, :] + dtb_ref[...])
            ddt_ref[rows, :] = ddtr.astype(BF)
            ph_ref[0:1, :] += jnp.sum(ddtr, axis=0, keepdims=True)
            ph_ref[1:2, :] += jnp.sum(dadt * dtc, axis=0, keepdims=True) * a
            ph_ref[2:3, :] += dsk
            pc = pre_ref[rows, :]
            sg = jax.nn.sigmoid(pc)
            dpad[rows, :] = dub[...] * _silu_grad(pc, sg)
            return carry

        lax.fori_loop(0, NCB, chunk, 0)

        dxb = cw_ref[0:1, :] * dpad[3:3 + TB, :]
        for kk in range(1, 4):
            dxb = dxb + cw_ref[kk:kk + 1, :] * dpad[3 - kk:3 - kk + TB, :]
        dx_ref[...] = dxb.astype(BF)
        keep = jnp.where(i < nb - 1, 1.0, 0.0)
        xpad[0:8, 0:SW] = xsh_ref[...] * keep
        xpad[0:8, SW:XBC] = bch_ref[...] * keep
        xpad[8:8 + TB, 0:SW] = xs_ref[...]
        xpad[8:8 + TB, SW:XBC] = bc_ref[...]
        dp = dpad[0:TB, :]
        for kk in range(4):
            pw_ref[kk:kk + 1, :] += jnp.sum(dp * xpad[5 + kk:5 + kk + TB, :], axis=0, keepdims=True)
        pw_ref[4:5, :] += jnp.sum(dp, axis=0, keepdims=True)
        dpad[TB:TB + 8, :] = dpad[0:8, :]

    rblk = lambda w, col: pl.BlockSpec((TB, w), lambda i, _c=col: (rev(i), _c))
    return pl.pallas_call(
        body, name="ssd_bwd", grid=(nb,),
        in_specs=[rblk(SW, 0), rblk(XBC, 0), rblk(LANE, 44),
                  pl.BlockSpec((NCB, NG, NS, GW), lambda i: (rev(i), 0, 0, 0)),
                  rblk(SW, 4), rblk(512, 10),
                  pl.BlockSpec((8, SW), lambda i: (halo(i), 4)), pl.BlockSpec((8, 512), lambda i: (halo(i), 10)),
                  _fix((4, XBC)), _fix((1, LANE)), _fix((1, LANE)), _fix((1, LANE))],
        out_specs=[rblk(XBC, 0), rblk(LANE, 0), _fix((8, XBC)), _fix((8, LANE))],
        out_shape=[jax.ShapeDtypeStruct((s, XBC), BF), jax.ShapeDtypeStruct((s, LANE), BF),
                   jax.ShapeDtypeStruct((8, XBC), F32), jax.ShapeDtypeStruct((8, LANE), F32)],
        scratch_shapes=[pltpu.VMEM((TB, XBC), F32), pltpu.VMEM((TB, LANE), F32), pltpu.VMEM((TB + 8, XBC), F32),
                        pltpu.VMEM((TB + 8, XBC), F32), pltpu.VMEM((SCH, LANE), F32), pltpu.VMEM((LANE, SCH), F32),
                        pltpu.VMEM((NG, NS, GW), F32), pltpu.VMEM((SCH, GW), F32), pltpu.VMEM((SCH, GW), F32),
                        pltpu.VMEM((SCH, GW), F32), pltpu.VMEM((SCH, GW), F32), pltpu.VMEM((1, GW), F32),
                        pltpu.VMEM((SCH, XBC), F32)],
        compiler_params=_cp("arbitrary"))(dy, pre, proj, prev, proj, proj, proj, proj, conv_w, dt_bias, a_log, d_skip)


def _rel_index():
    q = jnp.arange(CH)[:, None] + (BAND - CH)
    k = jnp.arange(BAND)[None, :]
    return jnp.clip(q - k, -(CH - 1), REL_CLIP) + (CH - 1)


def _local_step(x, tgt, mods, first_weights, late_weights, grads_ready,
                g_mix, rel_bias, conv_w, conv_b, dt_bias, a_log, d_skip, g_att, g_ssd, g_ffn, g_final):
    sh1, sc1, gt1, sh2, sc2, gt2 = mods
    pad16 = lambda v: jnp.pad(v, ((0, 0), (0, LANE - NH)))
    dtb_p, al_p, dsk_p = pad16(dt_bias), pad16(a_log), pad16(d_skip)
    ridx = _rel_index()
    ext = jnp.concatenate([jnp.broadcast_to(rel_bias[:, NREL - 1:], (NH, BAND + CH - 1 - NREL)),
                           rel_bias[:, ::-1]], axis=1)
    skew = jnp.tile(jnp.pad(ext, ((0, 0), (0, 1))), (1, CH + 1))[:, :CH * (BAND + CH + 1)]
    bias = skew.reshape(NH, CH, BAND + CH + 1)[:, ::-1, :BAND]
    bias_w = _bias_window(bias)
    bias_wt = jnp.transpose(bias_w.reshape(NH // 2, 2, QB, WIN), (0, 3, 1, 2)).reshape(NH // 2, WIN, 2 * QB)

    h1 = _norm_fwd(x, g_mix, sc1, sh1, name="norm1_fwd")
    onehot = (ridx.T.reshape(BAND * CH, 1) == jnp.arange(NREL)[None, :]).astype(F32)
    w_in_t = first_weights([h1, bias_w, bias_wt, onehot])
    proj = _mm([(h1, w_in_t)], "NT", F32, "in_proj", 1024, 1152, 2048)
    att, lse = _attn2_fwd(proj, bias_w)
    y, pre, prev = _ssd_fwd(proj, conv_w, conv_b, dtb_p, al_p, dsk_p)
    cat = _mixout_fwd(att, y, proj, g_att, g_ssd)
    (w_out,) = late_weights("out", cat)
    mix = _mm([(cat, w_out)], "NN", F32, "out_proj", 1024, 1024, 2048)
    h2, x1 = _norm_fwd(x, g_ffn, sc2, sh2, mix, gt1, name="norm2_fwd")
    w_gate_t, w_up_t = late_weights("gate_up", h2)
    gate, up, act = _gate_up(h2, w_gate_t, w_up_t)
    (w_down,) = late_weights("down", act)
    ffn = _mm([(act, w_down)], "NN", F32, "down_proj", 512, 1024, FF)
    dx2, dffn, pf = _final(x1, ffn, gt2, g_final, tgt)

    gw_down = _mm([(act, dffn)], "TN", BF, "gw_down", 1408, 2048, 1024)
    dgate, dup = _gate_up_bwd(dffn, w_down, gate, up)
    dh2 = _mm([(dgate, w_gate_t), (dup, w_up_t)], "NN", BF, "d_h2", 512, 512, FF)
    gw_gate_t = _mm([(dgate, h2)], "TN", BF, "gw_gate", 1408, 2048, 1024)
    gw_up_t = _mm([(dup, h2)], "TN", BF, "gw_up", 1408, 2048, 1024)
    tok = grads_ready("ffn", gw_gate_t, gw_up_t, gw_down)
    dx1, dmix, p2 = _norm_bwd(x1, g_ffn + tok[0, 0], sc2, sh2, dh2, dx2, mix, gt1, name="norm2_bwd")
    gw_out = _mm([(cat, dmix)], "TN", BF, "gw_out", 1024, 2048, 1024)
    tok = grads_ready("out", gw_out)
    dcat = _mm([(dmix, w_out)], "NT", BF, "d_cat", 1024, 1024, 2048, after=tok)
    datt, dy, dz, pm = _mixout_bwd(dcat, att, y, proj, g_att, g_ssd)
    dq, dk, dv, dbias_t = _attn2_bwd(proj, datt, lse[:, :NH].T, bias_wt)
    dxbc, ddt, pw, ph = _ssd_bwd(dy, pre, proj, prev, conv_w, dtb_p, al_p, dsk_p)
    dproj = jnp.concatenate([dq, dk, dv, dz, dxbc, ddt], axis=1)
    gw_in_t = _mm([(dproj, h1)], "TN", BF, "gw_in", 1152, 2048, 1024)
    tok = grads_ready("in", gw_in_t)
    dh1 = _mm([(dproj, w_in_t)], "NN", BF, "d_h1", 512, 1024, INP, after=tok)
    gx, p1 = _norm_bwd(x, g_mix, sc1, sh1, dh1, dx1, name="norm1_bwd")
    g_rel =_mm([(dbias_t.reshape(NH, BAND * CH), onehot)], "NN", F32, "g_rel_bias", NH, NREL, 4608, precision=HI)

    small = dict(
        dmods=jnp.concatenate([p1[1], p1[0], p2[3], p2[1], p2[0], pf[1]]),
        g_mix=p1[2], conv_b=pw[4], dt_bias=ph[0], a_log=ph[1], d_skip=ph[2], g_att=pm[0], g_ssd=pm[1],
        g_ffn=p2[2], g_final=pf[0], rel_bias=g_rel.reshape(-1), conv_w=pw[0:4].reshape(-1), loss=pf[2])
    return gx, small


def _exchange(srcs, gather, name):
    n = len(srcs)
    outs = [jax.ShapeDtypeStruct((NDEV,) + (s.shape if g else s.shape[1:]), s.dtype) for s, g in zip(srcs, gather)]
    outs.append(jax.ShapeDtypeStruct((8, LANE), F32))

    def body(*refs):
        src, dst, token = refs[:n], refs[n:2 * n], refs[2 * n]
        ssem, rsem, lsem = refs[2 * n + 1:]
        token[...] = jnp.zeros_like(token)
        x, y, c = lax.axis_index("x"), lax.axis_index("y"), lax.axis_index("c")
        me = 4 * x + 2 * y + c
        local = []
        for a in range(n):
            own = src[a] if gather[a] else src[a].at[me]
            cp = pltpu.make_async_copy(own, dst[a].at[me], lsem.at[a])
            cp.start()
            local.append(cp)
        remote = []
        for k in range(1, NDEV):
            px = 1 - x if k & 4 else x
            py = 1 - y if k & 2 else y
            pc = 1 - c if k & 1 else c
            pid = 4 * px + 2 * py + pc
            for a in range(n):
                s_ref = src[a] if gather[a] else src[a].at[pid]
                cp = pltpu.make_async_remote_copy(
                    src_ref=s_ref, dst_ref=dst[a].at[me], send_sem=ssem.at[a * (NDEV - 1) + k - 1],
                    recv_sem=rsem.at[a * (NDEV - 1) + k - 1], device_id=(px, py, pc), device_id_type=MESH)
                cp.start()
                remote.append(cp)
        for cp in remote:
            cp.wait()
        for cp in local:
            cp.wait()

    anyspec = pl.BlockSpec(memory_space=pl.ANY)
    return pl.pallas_call(
        body, name=name, out_shape=outs, in_specs=[anyspec] * n,
        out_specs=[anyspec] * n + [pl.BlockSpec(memory_space=pltpu.VMEM)],
        scratch_shapes=[pltpu.SemaphoreType.DMA((n * (NDEV - 1),)), pltpu.SemaphoreType.DMA((n * (NDEV - 1),)),
                        pltpu.SemaphoreType.DMA((n,))],
    )(*srcs)


_HBM = pl.BlockSpec(memory_space=pltpu.HBM)
_SEM = pl.BlockSpec(memory_space=pltpu.SEMAPHORE)
_EFFECT = pltpu.SideEffectType.DATAFLOW_SIDE_EFFECTING


def _peer_copies(src, land, ssem, rsem, gather):
    x, y, c = lax.axis_index("x"), lax.axis_index("y"), lax.axis_index("c")
    me = 4 * x + 2 * y + c
    copies = []
    for a in range(len(src)):
        for k in range(1, NDEV):
            px = 1 - x if k & 4 else x
            py = 1 - y if k & 2 else y
            pc = 1 - c if k & 1 else c
            s_ref = src[a] if gather[a] else src[a].at[4 * px + 2 * py + pc]
            idx = a * (NDEV - 1) + k - 1
            copies.append(pltpu.make_async_remote_copy(
                src_ref=s_ref, dst_ref=land[a].at[me], send_sem=ssem.at[idx], recv_sem=rsem.at[idx],
                device_id=(px, py, pc), device_id_type=MESH))
    return copies


def _xstart(srcs, gather, name):
    n = len(srcs)
    me = 4 * lax.axis_index("x") + 2 * lax.axis_index("y") + lax.axis_index("c")
    lands = []
    for s_, g_ in zip(srcs, gather):
        own = s_ if g_ else lax.dynamic_index_in_dim(s_, me, 0, keepdims=False)
        empty = lax.empty((NDEV,) + own.shape, own.dtype)
        lands.append(lax.dynamic_update_slice(empty, own[None], (me,) + (0,) * own.ndim))
    nsem = n * (NDEV - 1)

    def body(*refs):
        src, land, ssem, rsem, token = refs[:n], refs[n:2 * n], refs[2 * n], refs[2 * n + 1], refs[-1]
        for cp in _peer_copies(src, land, ssem, rsem, gather):
            cp.start()
        token[...] = jnp.zeros_like(token)

    ops = [pltpu.with_memory_space_constraint(a_, pltpu.HBM) for a_ in list(srcs) + lands]
    return pl.pallas_call(
        body, name=name,
        out_shape=(pltpu.SemaphoreType.DMA((nsem,)), pltpu.SemaphoreType.DMA((nsem,)),
                   *[pltpu.HBM(a_.shape, a_.dtype) for a_ in ops], jax.ShapeDtypeStruct((8, LANE), F32)),
        in_specs=[_HBM] * (2 * n),
        out_specs=(_SEM, _SEM, *[_HBM] * (2 * n), pl.BlockSpec(memory_space=pltpu.VMEM)),
        input_output_aliases={i: 2 + i for i in range(2 * n)},
        compiler_params=pltpu.CompilerParams(has_side_effects=_EFFECT),
    )(*ops)


def _xwait(started, gather, after, name):
    n = len(gather)
    ssem, rsem = started[0], started[1]
    bufs = started[2:2 + 2 * n]
    after = list(after) if isinstance(after, (list, tuple)) else [after]

    def body(*refs):
        src, land, ssem_, rsem_ = refs[:n], refs[n:2 * n], refs[2 * n], refs[2 * n + 1]
        for cp in _peer_copies(src, land, ssem_, rsem_, gather):
            cp.wait_send()
            cp.wait_recv()

    out = pl.pallas_call(
        body, name=name, out_shape=tuple(pltpu.HBM(a_.shape, a_.dtype) for a_ in bufs),
        in_specs=[_HBM] * (2 * n) + [_SEM, _SEM] + [pl.BlockSpec(memory_space=pl.ANY)] * len(after),
        out_specs=[_HBM] * (2 * n), input_output_aliases={i: i for i in range(2 * n)},
        compiler_params=pltpu.CompilerParams(has_side_effects=_EFFECT),
    )(*bufs, ssem, rsem, *after)
    return out[n:]


def _mods(c16, w_shard, b_shard, *, tn=512):
    n = w_shard.shape[1]

    def body(c_ref, w_ref, b_ref, o_ref):
        cv = c_ref[...]
        cond = (cv * jax.nn.sigmoid(cv)).astype(BF)
        o_ref[...] = jnp.dot(cond, w_ref[...].astype(BF), preferred_element_type=F32) + b_ref[...]

    return pl.pallas_call(body, name="ada_mods", grid=(n // tn,),
                          in_specs=[_fix((16, D)), pl.BlockSpec((D, tn), lambda j: (0, j)),
                                    pl.BlockSpec((1, tn), lambda j: (0, j))],
                          out_specs=pl.BlockSpec((16, tn), lambda j: (0, j)),
                          out_shape=jax.ShapeDtypeStruct((16, n), F32), compiler_params=_cp("parallel"))(c16, w_shard, b_shard)


def _gw_ada(c_t, dm, *, tr=256):
    n = dm.shape[1]

    def body(ct_ref, dm_ref, o_ref):
        acc = jnp.zeros((tr, n), F32)
        for b in range(NDEV):
            cv = ct_ref[:, b:b + 1]
            cond = (cv * jax.nn.sigmoid(cv)).astype(BF).astype(F32)
            acc = acc + cond * dm_ref[b:b + 1, :].astype(BF).astype(F32)
        o_ref[...] = acc

    return pl.pallas_call(body, name="gw_ada", grid=(D // tr,),
                          in_specs=[_row(tr, NDEV), _fix((NDEV, n))], out_specs=_row(tr, n),
                          out_shape=jax.ShapeDtypeStruct((D, n), F32), compiler_params=_cp("parallel"))(c_t, dm)


def _sum8(parts, *, name, tc=512):
    _, r, c = parts.shape
    tc = min(tc, c)

    def body(p_ref, o_ref):
        acc = p_ref[0].astype(F32)
        for d in range(1, NDEV):
            acc = acc + p_ref[d].astype(F32)
        o_ref[...] = acc

    return pl.pallas_call(body, name=name, grid=(c // tc,),
                          in_specs=[pl.BlockSpec((NDEV, r, tc), lambda j: (0, 0, j))],
                          out_specs=pl.BlockSpec((r, tc), lambda j: (0, j)),
                          out_shape=jax.ShapeDtypeStruct((r, c), F32), compiler_params=_cp("parallel"))(parts)


def _adam(w, m, v, *, recv=None, grad=None, tr=None, tc=None, name):
    lead = w.ndim == 3
    r, c = w.shape[-2:]
    summed = recv is not None
    assert tc is not None or r % tr == 0
    at = (0,) if lead else (Ellipsis,)

    def body(g_ref, w_ref, m_ref, v_ref, go_ref, do_ref, mo_ref, vo_ref):
        if summed:
            g = g_ref[0].astype(F32)
            for d in range(1, NDEV):
                g = g + g_ref[d].astype(F32)
        else:
            g = g_ref[...]
        go_ref[at] = g
        mn = B1 * m_ref[at] + (1.0 - B1) * g
        vn = B2 * v_ref[at] + (1.0 - B2) * (g * g)
        mo_ref[at] = mn
        vo_ref[at] = vn
        mh = mn / (1.0 - B1 ** STEP)
        vh = vn / (1.0 - B2 ** STEP)
        do_ref[at] = -LR * (mh / (jnp.sqrt(vh) + AEPS) + WD * w_ref[at])

    if tc is not None:
        assert not lead and c % tc == 0
        gspec = (pl.BlockSpec((NDEV, r, tc), lambda j: (0, 0, j)) if summed
                 else pl.BlockSpec((r, tc), lambda j: (0, j)))
        wspec = pl.BlockSpec((r, tc), lambda j: (0, j))
        return pl.pallas_call(body, name=name, grid=(c // tc,),
                              in_specs=[gspec, wspec, wspec, wspec], out_specs=[wspec] * 4,
                              out_shape=[jax.ShapeDtypeStruct(w.shape, F32)] * 4,
                              compiler_params=_cp("parallel"))(recv if summed else grad, w, m, v)
    gspec = pl.BlockSpec((NDEV, tr, c), lambda i: (0, i, 0)) if summed else _row(tr, c)
    wspec = pl.BlockSpec((1, tr, c), lambda i: (0, i, 0)) if lead else _row(tr, c)
    return pl.pallas_call(body, name=name, grid=(r // tr,),
                          in_specs=[gspec, wspec, wspec, wspec], out_specs=[wspec] * 4,
                          out_shape=[jax.ShapeDtypeStruct(w.shape, F32)] * 4,
                          compiler_params=_cp("parallel"))(recv if summed else grad, w, m, v)


def _blocks_to_cols(b):
    _, r, width = b.shape
    return jnp.transpose(b, (1, 0, 2)).reshape(r, NDEV * width)


_SMALL = ("b_ada", "g_mix", "rel_bias", "conv_w", "conv_b", "dt_bias", "a_log", "d_skip",
          "g_att_out", "g_ssd_out", "g_ffn", "g_final")


def _pad_rows(v):
    v = v.reshape(-1)
    return jnp.pad(v, (0, (-v.shape[0]) % LANE))


def kernel(x, c, w_ada, b_ada, g_mix, w_in, rel_bias, conv_w, conv_b, dt_bias, a_log, d_skip, g_att_out, g_ssd_out, w_out, g_ffn, w_gate, w_up, w_down, g_final, loss_target, m_w_ada, m_b_ada, m_g_mix, m_w_in, m_rel_bias, m_conv_w, m_conv_b, m_dt_bias, m_a_log, m_d_skip, m_g_att_out, m_g_ssd_out, m_w_out, m_g_ffn, m_w_gate, m_w_up, m_w_down, m_g_final, v_w_ada, v_b_ada, v_g_mix, v_w_in, v_rel_bias, v_conv_w, v_conv_b, v_dt_bias, v_a_log, v_d_skip, v_g_att_out, v_g_ssd_out, v_w_out, v_g_ffn, v_w_gate, v_w_up, v_w_down, v_g_final):
    me = 4 * lax.axis_index("x") + 2 * lax.axis_index("y") + lax.axis_index("c")
    wts = dict(w_ada=w_ada, b_ada=b_ada, g_mix=g_mix, w_in=w_in, rel_bias=rel_bias, conv_w=conv_w, conv_b=conv_b,
               dt_bias=dt_bias, a_log=a_log, d_skip=d_skip, g_att_out=g_att_out, g_ssd_out=g_ssd_out, w_out=w_out,
               g_ffn=g_ffn, w_gate=w_gate, w_up=w_up, w_down=w_down, g_final=g_final)
    mom = dict(w_ada=m_w_ada, b_ada=m_b_ada, g_mix=m_g_mix, w_in=m_w_in, rel_bias=m_rel_bias, conv_w=m_conv_w,
               conv_b=m_conv_b, dt_bias=m_dt_bias, a_log=m_a_log, d_skip=m_d_skip, g_att_out=m_g_att_out,
               g_ssd_out=m_g_ssd_out, w_out=m_w_out, g_ffn=m_g_ffn, w_gate=m_w_gate, w_up=m_w_up, w_down=m_w_down,
               g_final=m_g_final)
    var = dict(w_ada=v_w_ada, b_ada=v_b_ada, g_mix=v_g_mix, w_in=v_w_in, rel_bias=v_rel_bias, conv_w=v_conv_w,
               conv_b=v_conv_b, dt_bias=v_dt_bias, a_log=v_a_log, d_skip=v_d_skip, g_att_out=v_g_att_out,
               g_ssd_out=v_g_ssd_out, w_out=v_w_out, g_ffn=v_g_ffn, w_gate=v_w_gate, w_up=v_w_up, w_down=v_w_down,
               g_final=v_g_final)
    order = ("w_ada", "b_ada", "g_mix", "w_in", "rel_bias", "conv_w", "conv_b", "dt_bias", "a_log", "d_skip",
             "g_att_out", "g_ssd_out", "w_out", "g_ffn", "w_gate", "w_up", "w_down", "g_final")

    c_all, rel_all, cw_all, _ = _exchange([c, rel_bias[0], conv_w[0]], [True] * 3, "gather_small")
    c_all = c_all.reshape(NDEV, D)
    rel_full = _blocks_to_cols(rel_all)
    cw_full = _blocks_to_cols(cw_all)

    ncol = w_ada.shape[2]
    b_sh = lax.dynamic_slice(b_ada, (0, me * ncol), (1, ncol))
    mods_part = _mods(jnp.pad(c_all, ((0, 8), (0, 0))), w_ada[0], b_sh)[:NDEV]
    mods_all, tok_m = _exchange([mods_part], [True], "gather_mods")
    mods_me = lax.dynamic_index_in_dim(mods_all, me, axis=1, keepdims=False).reshape(6, 1, D)
    mods = [mods_me[i] for i in range(6)]

    st_in = _xstart([(w_in[0] + tok_m[0, 0]).T.astype(BF)], [True], "gather_w_in_start")
    st_late = dict(out=_xstart([(w_out[0] + st_in[-1][0, 0]).astype(BF)], [True], "gather_w_out_start"))
    st_late["gate_up"] = _xstart([(w_gate[0] + st_late["out"][-1][0, 0]).T.astype(BF), w_up[0].T.astype(BF)],
                                 [True] * 2, "gather_w_gate_up_start")
    st_late["down"] = _xstart([(w_down[0] + st_late["gate_up"][-1][0, 0]).astype(BF)], [True],
                              "gather_w_down_start")
    st_rest = st_late["down"]

    def first_weights(after):
        (gi,) = _xwait(st_in, [True], after, "gather_w_in_wait")
        return jnp.pad(gi.reshape(INC, D), ((0, INP - INC), (0, 0)))

    def late_weights(which, after):
        got = _xwait(st_late[which], [True] * (2 if which == "gate_up" else 1), after, f"gather_w_{which}_wait")
        return [g_.reshape(-1, D) for g_ in got]

    started = {}

    def grads_ready(which, *g):
        if which == "in":
            g = (g[0][:INC],)
        srcs = [g_.reshape(NDEV, g_.shape[0] // NDEV, D) for g_ in g]
        started[which] = _xstart(srcs, [False] * len(srcs), f"exchange_{which}_grads_start")
        return started[which][-1]

    gx, small = _local_step(
        x[0], loss_target[0], mods, first_weights, late_weights, grads_ready,
        g_mix + st_rest[-1][0, 0], rel_full, cw_full, conv_b, dt_bias, a_log, d_skip, g_att_out, g_ssd_out, g_ffn,
        g_final.reshape(1, D))

    loss_row = jnp.pad(jnp.sum(small["loss"]).reshape(1), (0, LANE - 1))
    pack = jnp.concatenate([small[k] for k in ("dmods", "g_mix", "conv_b", "dt_bias", "a_log", "d_skip", "g_att",
                                               "g_ssd", "g_ffn", "g_final", "rel_bias", "conv_w")] + [loss_row])
    packs, _ = _exchange([pack.reshape(-1, LANE)], [True], "gather_small_grads")
    rg, ru, rd = _xwait(started["ffn"], [False] * 3, packs, "exchange_ffn_grads_wait")
    (ro,) = _xwait(started["out"], [False], rg, "exchange_out_grads_wait")
    (ri,) = _xwait(started["in"], [False], ro, "exchange_in_grads_wait")
    big = dict(
        w_gate=[o_.T for o_ in _adam(w_gate[0].T, m_w_gate[0].T, v_w_gate[0].T, recv=rg, tc=512,
                                     name="adam_w_gate")],
        w_up=[o_.T for o_ in _adam(w_up[0].T, m_w_up[0].T, v_w_up[0].T, recv=ru, tc=512, name="adam_w_up")],
        w_down=_adam(w_down[0], m_w_down[0], v_w_down[0], recv=rd, tr=176, name="adam_w_down"))
    tot = _sum8(packs, name="sum_small").reshape(-1)

    sizes = dict(dmods=6 * D, g_mix=D, conv_b=XBC, dt_bias=LANE, a_log=LANE, d_skip=LANE, g_att_out=AW, g_ssd_out=SW,
                 g_ffn=D, g_final=D, rel_bias=NH * NREL, conv_w=4 * XBC, loss=LANE)
    off, o = {}, 0
    for k_, n_ in sizes.items():
        off[k_] = o
        o += n_
    take = lambda k_, n_=None: tot[off[k_]:off[k_] + (n_ or sizes[k_])]
    loss = tot[off["loss"]]
    rel_w = NREL // NDEV
    cw_w = XBC // NDEV
    gsm = dict(
        b_ada=take("dmods"), g_mix=take("g_mix"),
        rel_bias=lax.dynamic_slice(take("rel_bias").reshape(NH, NREL), (0, me * rel_w), (NH, rel_w)),
        conv_w=lax.dynamic_slice(take("conv_w").reshape(4, XBC), (0, me * cw_w), (4, cw_w)),
        conv_b=take("conv_b"), dt_bias=take("dt_bias", NH), a_log=take("a_log", NH), d_skip=take("d_skip", NH),
        g_att_out=take("g_att_out"), g_ssd_out=take("g_ssd_out"), g_ffn=take("g_ffn"), g_final=take("g_final"))

    flat = lambda d_: jnp.concatenate([_pad_rows(d_[k_]) for k_ in _SMALL])
    gflat = flat(gsm)
    nrow = -(-gflat.shape[0] // (8 * LANE)) * 8
    to2d = lambda v_: jnp.pad(v_, (0, nrow * LANE - v_.shape[0])).reshape(nrow, LANE)
    res = _adam(to2d(flat(wts)), to2d(flat(mom)), to2d(flat(var)), grad=to2d(gflat), tr=nrow, name="adam_small")
    outs = {k_: {} for k_ in ("grad", "delta", "m", "v")}
    o = 0
    for k_ in _SMALL:
        n_ = wts[k_].size
        for kind, arr in zip(("grad", "delta", "m", "v"), res):
            outs[kind][k_] = arr.reshape(-1)[o:o + n_].reshape(wts[k_].shape)
        o += n_ + (-n_) % LANE

    dm_all = packs[:, :6 * D // LANE, :].reshape(NDEV, 6 * D)
    dm_sh = lax.dynamic_slice(dm_all, (0, me * ncol), (NDEV, ncol))
    gwa = _gw_ada(c_all.T, dm_sh)
    big.update(
        w_ada=_adam(w_ada[0], m_w_ada[0], v_w_ada[0], grad=gwa, tr=256, name="adam_w_ada"),
        w_in=[o_.T for o_ in _adam(w_in[0].T, m_w_in[0].T, v_w_in[0].T, recv=ri, tc=512, name="adam_w_in")],
        w_out=_adam(w_out[0], m_w_out[0], v_w_out[0], recv=ro, tr=128, name="adam_w_out"))
    for k_, r_ in big.items():
        for kind, arr in zip(("grad", "delta", "m", "v"), r_):
            outs[kind][k_] = arr.reshape(wts[k_].shape)

    return (loss, gx.reshape(x.shape), *[outs["grad"][k_] for k_ in order], *[outs["delta"][k_] for k_ in order],
            *[outs["m"][k_] for k_ in order], *[outs["v"][k_] for k_ in order])
```

```python
import functools

import jax
import jax.numpy as jnp
from jax import lax
from jax.experimental import pallas as pl
from jax.experimental.pallas import tpu as pltpu

F32 = jnp.float32
BF = jnp.bfloat16
HI = lax.Precision.HIGHEST

D = 2048
NH = 16
HD = 64
AW = 1024
SW = 1024
NG = 2
NS = 128
XBC = 1536
CH = 64
BAND = 576
NREL = 320
REL_CLIP = 256
FF = 5632
INC = 5648
INP = 5760
EPS = 1e-6
NDEV = 8
LANE = 128
VMEM_LIMIT = 56 * 1024 * 1024
LR, B1, B2, AEPS, WD, STEP = 0.001, 0.9, 0.999, 1e-08, 0.01, 10
MESH = pl.DeviceIdType.MESH


def _cp(*sem):
    return pltpu.CompilerParams(dimension_semantics=sem, vmem_limit_bytes=VMEM_LIMIT)


def _row(ts, w, col=0):
    return pl.BlockSpec((ts, w), lambda i, _c=col: (i, _c))


def _fix(shape, single=False):
    nd = len(shape)
    if single:
        return pl.BlockSpec(shape, lambda i, _n=nd: (0,) * _n, pipeline_mode=pl.Buffered(1))
    return pl.BlockSpec(shape, lambda i, _n=nd: (0,) * _n)


def _silu_grad(x, s):
    return s * (1.0 + x * (1.0 - s))


def _softplus(x):
    return jnp.maximum(x, 0.0) + jnp.log1p(jnp.exp(-jnp.abs(x)))


_DIMS = {"NN": (((1,), (0,)), ((), ())), "TN": (((0,), (0,)), ((), ())), "NT": (((1,), (1,)), ((), ()))}


def _mm(pairs, mode, out_dtype, name, tm, tn, tk, precision=None, after=None):
    a0, b0 = pairs[0]
    m, k = a0.shape[::-1] if mode == "TN" else a0.shape
    n = b0.shape[0] if mode == "NT" else b0.shape[1]
    tm, tn, tk = min(tm, m), min(tn, n), min(tk, k)
    nk1 = k // tk
    npair = len(pairs)
    nk = 1 if nk1 == 1 else nk1 * npair
    assert m % tm == 0 and n % tn == 0 and k % tk == 0, (name, a0.shape, b0.shape)
    dims = _DIMS[mode]

    def kloc(kk, p):
        return jnp.clip(kk - p * nk1, 0, nk1 - 1)

    def aspec(p):
        if mode == "TN":
            return pl.BlockSpec((tk, tm), lambda i, j, kk, _p=p: (kloc(kk, _p), i))
        return pl.BlockSpec((tm, tk), lambda i, j, kk, _p=p: (i, kloc(kk, _p)))

    def bspec(p):
        if mode == "NT":
            return pl.BlockSpec((tn, tk), lambda i, j, kk, _p=p: (j, kloc(kk, _p)))
        return pl.BlockSpec((tk, tn), lambda i, j, kk, _p=p: (kloc(kk, _p), j))

    nin = 2 * npair + (after is not None)

    def body(*refs):
        ab, o_ref = refs[:2 * npair], refs[nin]
        if nk == 1:
            tot = lax.dot_general(ab[0][...], ab[1][...], dims, preferred_element_type=F32, precision=precision)
            for p in range(1, npair):
                tot = tot + lax.dot_general(ab[2 * p][...], ab[2 * p + 1][...], dims, preferred_element_type=F32,
                                            precision=precision)
            o_ref[...] = tot.astype(o_ref.dtype)
            return
        acc = refs[nin + 1]
        kk = pl.program_id(2)

        @pl.when(kk == 0)
        def _():
            acc[...] = jnp.zeros_like(acc)

        for p in range(npair):
            @pl.when((kk >= p * nk1) & (kk < (p + 1) * nk1))
            def _(p=p):
                acc[...] += lax.dot_general(ab[2 * p][...], ab[2 * p + 1][...], dims, preferred_element_type=F32,
                                            precision=precision)

        @pl.when(kk == nk - 1)
        def _():
            o_ref[...] = acc[...].astype(o_ref.dtype)

    in_specs = []
    for p in range(npair):
        in_specs += [aspec(p), bspec(p)]
    operands = [t for ab_ in pairs for t in ab_]
    if after is not None:
        in_specs.append(pl.BlockSpec(memory_space=pl.ANY))
        operands.append(after)
    return pl.pallas_call(
        body, name=name, grid=(m // tm, n // tn, nk), in_specs=in_specs,
        out_specs=pl.BlockSpec((tm, tn), lambda i, j, kk: (i, j)),
        out_shape=jax.ShapeDtypeStruct((m, n), out_dtype),
        scratch_shapes=[] if nk == 1 else [pltpu.VMEM((tm, tn), F32)],
        compiler_params=_cp("parallel", "parallel", "arbitrary"),
    )(*operands)


def _norm_fwd(x, g, sc, sh, mix=None, gt=None, *, name, ts=512):
    s, d = x.shape
    has = mix is not None

    def body(*refs):
        if has:
            x_ref, g_ref, sc_ref, sh_ref, mix_ref, gt_ref, h_ref, x1_ref = refs
        else:
            x_ref, g_ref, sc_ref, sh_ref, h_ref = refs
        xv = x_ref[...]
        if has:
            xv = xv + gt_ref[...] * mix_ref[...]
            x1_ref[...] = xv
        r = lax.rsqrt(jnp.mean(xv * xv, axis=-1, keepdims=True) + EPS)
        h_ref[...] = ((xv * r) * g_ref[...] * (1.0 + sc_ref[...]) + sh_ref[...]).astype(BF)

    ins = [x, g, sc, sh] + ([mix, gt] if has else [])
    in_specs = [_row(ts, d), _fix((1, d)), _fix((1, d)), _fix((1, d))] + ([_row(ts, d), _fix((1, d))] if has else [])
    out_shape = [jax.ShapeDtypeStruct((s, d), BF)] + ([jax.ShapeDtypeStruct((s, d), F32)] if has else [])
    out_specs = [_row(ts, d)] + ([_row(ts, d)] if has else [])
    out = pl.pallas_call(body, name=name, grid=(s // ts,), in_specs=in_specs, out_specs=out_specs,
                         out_shape=out_shape, compiler_params=_cp("parallel"))(*ins)
    return out if has else out[0]


def _norm_bwd(x, g, sc, sh, dh, dres, mix=None, gt=None, *, name, ts=256):
    s, d = x.shape
    has = mix is not None

    def body(*refs):
        if has:
            x_ref, g_ref, sc_ref, sh_ref, dh_ref, dres_ref, mix_ref, gt_ref, dx_ref, dmix_ref, part_ref = refs
        else:
            x_ref, g_ref, sc_ref, sh_ref, dh_ref, dres_ref, dx_ref, part_ref = refs
        i = pl.program_id(0)
        xv = x_ref[...]
        gv = g_ref[...]
        dhv = dh_ref[...].astype(F32)
        r = lax.rsqrt(jnp.mean(xv * xv, axis=-1, keepdims=True) + EPS)
        xh = xv * r
        dyn = dhv * (1.0 + sc_ref[...])
        gy = dyn * gv
        dx = dres_ref[...] + r * (gy - xh * jnp.mean(gy * xh, axis=-1, keepdims=True))
        dx_ref[...] = dx

        @pl.when(i == 0)
        def _():
            part_ref[...] = jnp.zeros_like(part_ref)

        part_ref[0:1, :] += jnp.sum(dhv * (xh * gv), axis=0, keepdims=True)
        part_ref[1:2, :] += jnp.sum(dhv, axis=0, keepdims=True)
        part_ref[2:3, :] += jnp.sum(dyn * xh, axis=0, keepdims=True)
        if has:
            part_ref[3:4, :] += jnp.sum(dx * mix_ref[...], axis=0, keepdims=True)
            dmix_ref[...] = (gt_ref[...] * dx).astype(BF)

    ins = [x, g, sc, sh, dh, dres] + ([mix, gt] if has else [])
    in_specs = [_row(ts, d), _fix((1, d)), _fix((1, d)), _fix((1, d)), _row(ts, d), _row(ts, d)]
    in_specs += [_row(ts, d), _fix((1, d))] if has else []
    out_shape = [jax.ShapeDtypeStruct((s, d), F32)] + ([jax.ShapeDtypeStruct((s, d), BF)] if has else [])
    out_shape += [jax.ShapeDtypeStruct((8, d), F32)]
    out_specs = [_row(ts, d)] + ([_row(ts, d)] if has else []) + [_fix((8, d))]
    return pl.pallas_call(body, name=name, grid=(s // ts,), in_specs=in_specs, out_specs=out_specs,
                          out_shape=out_shape, compiler_params=_cp("arbitrary"))(*ins)


def _final(x1, ffn, gt2, gf, tgt, *, ts=256):
    s, d = x1.shape

    def body(x1_ref, ffn_ref, gt_ref, g_ref, t_ref, dx_ref, dffn_ref, part_ref):
        i = pl.program_id(0)
        fv = ffn_ref[...]
        gv = g_ref[...]
        xv = x1_ref[...] + gt_ref[...] * fv
        r = lax.rsqrt(jnp.mean(xv * xv, axis=-1, keepdims=True) + EPS)
        xh = xv * r
        e = xh * gv - t_ref[...]
        dy = e * (1.0 / d)
        gy = dy * gv
        dx = r * (gy - xh * jnp.mean(gy * xh, axis=-1, keepdims=True))
        dx_ref[...] = dx
        dffn_ref[...] = (gt_ref[...] * dx).astype(BF)

        @pl.when(i == 0)
        def _():
            part_ref[...] = jnp.zeros_like(part_ref)

        part_ref[0:1, :] += jnp.sum(dy * xh, axis=0, keepdims=True)
        part_ref[1:2, :] += jnp.sum(dx * fv, axis=0, keepdims=True)
        part_ref[2:3, :] += jnp.sum(e * e, axis=0, keepdims=True) * (0.5 / d)

    return pl.pallas_call(
        body, name="final_loss", grid=(s // ts,),
        in_specs=[_row(ts, d), _row(ts, d), _fix((1, d)), _fix((1, d)), _row(ts, d)],
        out_specs=[_row(ts, d), _row(ts, d), _fix((8, d))],
        out_shape=[jax.ShapeDtypeStruct((s, d), F32), jax.ShapeDtypeStruct((s, d), BF),
                   jax.ShapeDtypeStruct((8, d), F32)],
        compiler_params=_cp("arbitrary"))(x1, ffn, gt2, gf, tgt)


def _gate_up(h, wg_t, wu_t, *, tm=1024, tn=512, sub=512):
    s, k = h.shape
    n = wg_t.shape[0]
    tm = min(tm, s)

    def body(h_ref, wg_ref, wu_ref, g_ref, u_ref, a_ref):
        hv = h_ref[...]
        for c0 in range(0, tn, sub):
            cs_ = slice(c0, c0 + sub)
            gv = lax.dot_general(hv, wg_ref[cs_, :], _DIMS["NT"], preferred_element_type=F32)
            uv = lax.dot_general(hv, wu_ref[cs_, :], _DIMS["NT"], preferred_element_type=F32)
            g_ref[:, cs_] = gv.astype(BF)
            u_ref[:, cs_] = uv.astype(BF)
            a_ref[:, cs_] = (gv * jax.nn.sigmoid(gv) * uv).astype(BF)

    wspec = pl.BlockSpec((tn, k), lambda i, j: (j, 0))
    ospec = pl.BlockSpec((tm, tn), lambda i, j: (i, j))
    return pl.pallas_call(body, name="gate_up", grid=(s // tm, n // tn),
                          in_specs=[pl.BlockSpec((tm, k), lambda i, j: (i, 0)), wspec, wspec],
                          out_specs=[ospec] * 3, out_shape=[jax.ShapeDtypeStruct((s, n), BF)] * 3,
                          compiler_params=_cp("parallel", "parallel"))(h, wg_t, wu_t)


def _gate_up_bwd(dffn, w_down, gate, up, *, tm=1024, tn=512, sub=512):
    s, k = dffn.shape
    n = w_down.shape[0]
    tm = min(tm, s)

    def body(d_ref, w_ref, g_ref, u_ref, dg_ref, du_ref):
        dv_ = d_ref[...]
        for c0 in range(0, tn, sub):
            cs_ = slice(c0, c0 + sub)
            dav = lax.dot_general(dv_, w_ref[cs_, :], _DIMS["NT"], preferred_element_type=F32)
            gv = g_ref[:, cs_].astype(F32)
            uv = u_ref[:, cs_].astype(F32)
            sg = jax.nn.sigmoid(gv)
            dg_ref[:, cs_] = (dav * uv * _silu_grad(gv, sg)).astype(BF)
            du_ref[:, cs_] = (dav * gv * sg).astype(BF)

    tile = pl.BlockSpec((tm, tn), lambda i, j: (i, j))
    return pl.pallas_call(body, name="gate_up_bwd", grid=(s // tm, n // tn),
                          in_specs=[pl.BlockSpec((tm, k), lambda i, j: (i, 0)),
                                    pl.BlockSpec((tn, k), lambda i, j: (j, 0)), tile, tile],
                          out_specs=[tile] * 2, out_shape=[jax.ShapeDtypeStruct((s, n), BF)] * 2,
                          compiler_params=_cp("parallel", "parallel"))(dffn, w_down, gate, up)


def _mixout_fwd(att, y, proj, g_att, g_ssd, *, ts=512):
    s = att.shape[0]

    def body(a_ref, y_ref, z_ref, ga_ref, gs_ref, o_ref):
        av = a_ref[...]
        ra = lax.rsqrt(jnp.mean(av * av, axis=-1, keepdims=True) + EPS)
        o_ref[:, 0:AW] = (av * ra * ga_ref[...]).astype(BF)
        zv = z_ref[...]
        yz = y_ref[...] * (zv * jax.nn.sigmoid(zv))
        rs = lax.rsqrt(jnp.mean(yz * yz, axis=-1, keepdims=True) + EPS)
        o_ref[:, AW:AW + SW] = (yz * rs * gs_ref[...]).astype(BF)

    return pl.pallas_call(body, name="mixout_fwd", grid=(s // ts,),
                          in_specs=[_row(ts, AW), _row(ts, SW), _row(ts, SW, 3), _fix((1, AW)), _fix((1, SW))],
                          out_specs=_row(ts, AW + SW), out_shape=jax.ShapeDtypeStruct((s, AW + SW), BF),
                          compiler_params=_cp("parallel"))(att, y, proj, g_att, g_ssd)


def _mixout_bwd(dcat, att, y, proj, g_att, g_ssd, *, ts=512):
    s = att.shape[0]

    def body(dc_ref, a_ref, y_ref, z_ref, ga_ref, gs_ref, da_ref, dy_ref, dz_ref, part_ref):
        i = pl.program_id(0)
        av = a_ref[...]
        dca = dc_ref[:, 0:AW].astype(F32)
        ra = lax.rsqrt(jnp.mean(av * av, axis=-1, keepdims=True) + EPS)
        ah = av * ra
        gy = dca * ga_ref[...]
        da_ref[...] = ra * (gy - ah * jnp.mean(gy * ah, axis=-1, keepdims=True))
        zv = z_ref[...]
        yv = y_ref[...]
        sg = jax.nn.sigmoid(zv)
        sz = zv * sg
        yz = yv * sz
        dcs = dc_ref[:, AW:AW + SW].astype(F32)
        rs = lax.rsqrt(jnp.mean(yz * yz, axis=-1, keepdims=True) + EPS)
        yh = yz * rs
        gys = dcs * gs_ref[...]
        dyz = rs * (gys - yh * jnp.mean(gys * yh, axis=-1, keepdims=True))
        dy_ref[...] = dyz * sz
        dz_ref[...] = (dyz * yv * _silu_grad(zv, sg)).astype(BF)

        @pl.when(i == 0)
        def _():
            part_ref[...] = jnp.zeros_like(part_ref)

        part_ref[0:1, :] += jnp.sum(dca * ah, axis=0, keepdims=True)
        part_ref[1:2, :] += jnp.sum(dcs * yh, axis=0, keepdims=True)

    return pl.pallas_call(
        body, name="mixout_bwd", grid=(s // ts,),
        in_specs=[_row(ts, AW + SW), _row(ts, AW), _row(ts, SW), _row(ts, SW, 3), _fix((1, AW)), _fix((1, SW))],
        out_specs=[_row(ts, AW), _row(ts, SW), _row(ts, SW), _fix((8, AW))],
        out_shape=[jax.ShapeDtypeStruct((s, AW), F32), jax.ShapeDtypeStruct((s, SW), F32),
                   jax.ShapeDtypeStruct((s, SW), BF), jax.ShapeDtypeStruct((8, AW), F32)],
        compiler_params=_cp("arbitrary"))(dcat, att, y, proj, g_att, g_ssd)


SCALE = HD ** -0.5
QB = 256
NQC = QB // CH
WIN = BAND - CH + QB
NKB = WIN // QB


def _bias_window(bias):
    rows = [jnp.pad(bias, ((0, 0), (0, 0), (CH * a, WIN - BAND - CH * a)), constant_values=-jnp.inf)
            for a in range(NQC)]
    return jnp.concatenate(rows, axis=1)


def _win_specs(width, nb):
    def mk(col, back):
        return pl.BlockSpec((QB, width), lambda i, _c=col, _k=back: (jnp.maximum(jnp.minimum(i, nb - 1) - _k, 0), _c))
    return [mk(col, back) for col in (1, 2) for back in range(NKB - 1, -1, -1)]


def _attn2_fwd(qkv, bias_w):
    s = qkv.shape[0]
    nb = s // QB

    def body(q_ref, *refs):
        krefs, vrefs = refs[:NKB], refs[NKB:2 * NKB]
        b_ref, o_ref, lse_ref, kbuf, vbuf = refs[2 * NKB:]
        i = pl.program_id(0)
        for j in range(NKB):
            kbuf[QB * j:QB * j + QB, :] = krefs[j][...].astype(BF)
            vbuf[QB * j:QB * j + QB, :] = vrefs[j][...].astype(BF)
        valid = (i * QB - (NKB - 1) * QB + lax.broadcasted_iota(jnp.int32, (1, WIN), 1)) >= 0
        lane = lax.broadcasted_iota(jnp.int32, (QB, LANE), 1)

        lo = lane < HD

        def heads(masked):
            lse = jnp.zeros((QB, LANE), F32)
            for p in range(NH // 2):
                sl = slice(2 * HD * p, 2 * HD * p + 2 * HD)
                qp = q_ref[:, sl] * SCALE
                qbd = jnp.concatenate([jnp.where(lo, qp, 0.0), jnp.where(lo, 0.0, qp)], axis=0).astype(BF)
                sc = lax.dot_general(qbd, kbuf[:, sl], _DIMS["NT"], preferred_element_type=F32) + b_ref[p]
                if masked:
                    sc = jnp.where(valid, sc, -jnp.inf)
                mx = jnp.max(sc, axis=-1, keepdims=True)
                e = jnp.exp(sc - mx)
                tot = jnp.sum(e, axis=-1, keepdims=True)
                o2 = jnp.dot(e.astype(BF), vbuf[:, sl], preferred_element_type=F32) * (1.0 / tot)
                o_ref[:, sl] = jnp.where(lo, o2[0:QB], o2[QB:2 * QB])
                l2 = mx + jnp.log(tot)
                lse = jnp.where(lane == 2 * p, l2[0:QB], jnp.where(lane == 2 * p + 1, l2[QB:2 * QB], lse))
            lse_ref[...] = lse

        pl.when(i < NKB - 1)(functools.partial(heads, True))
        pl.when(i >= NKB - 1)(functools.partial(heads, False))

    return pl.pallas_call(
        body, name="attn_fwd", grid=(nb,),
        in_specs=[_row(QB, AW)] + _win_specs(AW, nb) + [_fix((NH // 2, 2 * QB, WIN), single=True)],
        out_specs=[_row(QB, AW), _row(QB, LANE)],
        out_shape=[jax.ShapeDtypeStruct((s, AW), F32), jax.ShapeDtypeStruct((s, LANE), F32)],
        scratch_shapes=[pltpu.VMEM((WIN, AW), BF), pltpu.VMEM((WIN, AW), BF)],
        compiler_params=_cp("parallel"))(*[qkv] * (1 + 2 * NKB), bias_w.reshape(NH // 2, 2 * QB, WIN))


def _attn2_bwd(qkv, datt, lse_t, bias_wt):
    s = qkv.shape[0]
    nb = s // QB
    cur = lambda i: jnp.minimum(i, nb - 1)
    late = lambda i: jnp.maximum(i - (NKB - 1), 0)

    def body(q_ref, *refs):
        krefs, vrefs = refs[:NKB], refs[NKB:2 * NKB]
        do_ref, lse_ref, b_ref, dq_ref, dk_ref, dv_ref, db_ref, kbuf, vbuf, dkacc, dvacc, dsbuf = refs[2 * NKB:]
        i = pl.program_id(0)

        @pl.when(i == 0)
        def _():
            dkacc[...] = jnp.zeros_like(dkacc)
            dvacc[...] = jnp.zeros_like(dvacc)
            db_ref[...] = jnp.zeros_like(db_ref)

        def heads(masked):
            for j in range(NKB):
                kbuf[QB * j:QB * j + QB, :] = krefs[j][...].astype(BF)
                vbuf[QB * j:QB * j + QB, :] = vrefs[j][...].astype(BF)
            valid = (i * QB - (NKB - 1) * QB + lax.broadcasted_iota(jnp.int32, (WIN, 1), 0)) >= 0
            lo = lax.broadcasted_iota(jnp.int32, (QB, LANE), 1) < HD

            def both(ref, sl):
                t = ref[:, sl]
                return jnp.concatenate([jnp.where(lo, t, 0.0), jnp.where(lo, 0.0, t)], axis=0)

            for p in range(NH // 2):
                sl = slice(2 * HD * p, 2 * HD * p + 2 * HD)
                qbd = (both(q_ref, sl) * SCALE).astype(BF)
                dobd = both(do_ref, sl).astype(BF)
                k = kbuf[:, sl]
                st = lax.dot_general(k, qbd, _DIMS["NT"], preferred_element_type=F32) + b_ref[p]
                if masked:
                    st = jnp.where(valid, st, -jnp.inf)
                lrow = jnp.concatenate([lse_ref[2 * p:2 * p + 1, :], lse_ref[2 * p + 1:2 * p + 2, :]], axis=1)
                pt = jnp.exp(st - lrow)
                dvacc[:, sl] += jnp.dot(pt.astype(BF), dobd, preferred_element_type=F32)
                dpt = lax.dot_general(vbuf[:, sl], dobd, _DIMS["NT"], preferred_element_type=F32)
                dst = pt * (dpt - jnp.sum(dpt * pt, axis=0, keepdims=True))
                dsbuf[...] = dst
                for hh in range(2):
                    fold = dsbuf[0:BAND, QB * hh:QB * hh + CH]
                    for a in range(1, NQC):
                        fold = fold + dsbuf[CH * a:CH * a + BAND, QB * hh + CH * a:QB * hh + CH * a + CH]
                    db_ref[2 * p + hh] += fold
                dsb = dst.astype(BF)
                dkacc[:, sl] += jnp.dot(dsb, qbd, preferred_element_type=F32)
                dq2 = lax.dot_general(dsb, k, _DIMS["TN"], preferred_element_type=F32) * SCALE
                dq_ref[:, sl] = jnp.where(lo, dq2[0:QB], dq2[QB:2 * QB]).astype(BF)

        pl.when(i < nb)(functools.partial(heads, True))

        dk_ref[...] = dkacc[0:QB, :].astype(BF)
        dv_ref[...] = dvacc[0:QB, :].astype(BF)
        for acc in (dkacc, dvacc):
            for j in range(NKB - 1):
                acc[QB * j:QB * j + QB, :] = acc[QB * j + QB:QB * j + 2 * QB, :]
            acc[WIN - QB:WIN, :] = jnp.zeros((QB, AW), F32)

    qspec = pl.BlockSpec((QB, AW), lambda i: (cur(i), 0))
    lspec = pl.BlockSpec((QB, AW), lambda i: (late(i), 0))
    return pl.pallas_call(
        body, name="attn_bwd", grid=(nb + NKB - 1,),
        in_specs=[qspec] + _win_specs(AW, nb) + [qspec, pl.BlockSpec((NH, QB), lambda i: (0, cur(i))),
                                                 _fix((NH // 2, WIN, 2 * QB), single=True)],
        out_specs=[qspec, lspec, lspec, _fix((NH, BAND, CH))],
        out_shape=[jax.ShapeDtypeStruct((s, AW), BF)] * 3 + [jax.ShapeDtypeStruct((NH, BAND, CH), F32)],
        scratch_shapes=[pltpu.VMEM((WIN, AW), BF), pltpu.VMEM((WIN, AW), BF),
                        pltpu.VMEM((WIN, AW), F32), pltpu.VMEM((WIN, AW), F32), pltpu.VMEM((WIN, 2 * QB), F32)],
        compiler_params=_cp("arbitrary"))(*[qkv] * (1 + 2 * NKB), datt, lse_t, bias_wt)


TB = 512
SCH = 512
NCB = TB // SCH
GW = 8 * HD


def _tri(lower):
    r = lax.broadcasted_iota(jnp.int32, (SCH, SCH), 0)
    c = lax.broadcasted_iota(jnp.int32, (SCH, SCH), 1)
    return r >= c if lower else r <= c


def _ssd_fwd(proj, conv_w, conv_b, dt_bias, a_log, d_skip):
    s = proj.shape[0]
    nb = s // TB

    def body(xs_ref, bc_ref, dt_ref, cw_ref, cb_ref, dtb_ref, al_ref, dsk_ref, y_ref, pre_ref, prev_ref,
             xpad, ubuf, dtbuf, csb, cstb, hst, xdec, yoffb, drow):
        i = pl.program_id(0)

        @pl.when(i == 0)
        def _():
            xpad[0:8, :] = jnp.zeros((8, XBC), F32)
            hst[...] = jnp.zeros_like(hst)

        xpad[8:8 + TB, 0:SW] = xs_ref[...]
        xpad[8:8 + TB, SW:XBC] = bc_ref[...]
        pre = cb_ref[...]
        for kk in range(4):
            pre = pre + cw_ref[kk:kk + 1, :] * xpad[5 + kk:5 + kk + TB, :]
        pre_ref[...] = pre
        ubuf[...] = pre * jax.nn.sigmoid(pre)
        xpad[0:8, :] = xpad[TB:TB + 8, :]
        dtbuf[...] = _softplus(dt_ref[...] + dtb_ref[...])
        a = -jnp.exp(al_ref[...])
        tri = _tri(True).astype(F32)
        causal = _tri(True)

        def chunk(c, carry):
            r0 = pl.multiple_of(c * SCH, SCH)
            rows = pl.ds(r0, SCH)
            cs = jnp.dot(tri, dtbuf[rows, :] * a, precision=HI, preferred_element_type=F32)
            csb[...] = cs
            cstb[...] = cs.T
            for g in range(NG):
                bg = ubuf[rows, SW + NS * g:SW + NS * g + NS]
                cg = ubuf[rows, SW + NG * NS + NS * g:SW + NG * NS + NS * g + NS].astype(BF)
                cb = lax.dot_general(cg, bg.astype(BF), (((1,), (1,)), ((), ())), preferred_element_type=F32)
                hg = hst[g]
                prev_ref[c, g] = hg
                yoffb[...] = jnp.dot(cg, hg.astype(BF), preferred_element_type=F32)
                for r in range(8):
                    h = 8 * g + r
                    sl = slice(HD * h, HD * h + HD)
                    rs = slice(HD * r, HD * r + HD)
                    cs_h = csb[:, h:h + 1]
                    cl = csb[SCH - 1:SCH, h:h + 1]
                    seg = jnp.exp(jnp.where(causal, cs_h - cstb[h:h + 1, :], -jnp.inf))
                    xh = ubuf[rows, sl]
                    xdt = xh * dtbuf[rows, h:h + 1]
                    yd = jnp.dot((cb * seg).astype(BF), xdt.astype(BF), preferred_element_type=F32)
                    y_ref[rows, sl] = yd + jnp.exp(cs_h) * yoffb[:, rs] + xh * dsk_ref[:, h:h + 1]
                    xdec[:, rs] = xdt * jnp.exp(cl - cs_h)
                    drow[:, rs] = jnp.broadcast_to(jnp.exp(cl), (1, HD))
                st = jnp.dot(bg.T.astype(BF), xdec[...].astype(BF), preferred_element_type=F32)
                hst[g] = hg * drow[...] + st
            return carry

        lax.fori_loop(0, NCB, chunk, 0)

    return pl.pallas_call(
        body, name="ssd_fwd", grid=(nb,),
        in_specs=[_row(TB, SW, 4), _row(TB, 512, 10), _row(TB, LANE, 44), _fix((4, XBC)), _fix((1, XBC)),
                  _fix((1, LANE)), _fix((1, LANE)), _fix((1, LANE))],
        out_specs=[_row(TB, SW), _row(TB, XBC), pl.BlockSpec((NCB, NG, NS, GW), lambda i: (i, 0, 0, 0))],
        out_shape=[jax.ShapeDtypeStruct((s, SW), F32), jax.ShapeDtypeStruct((s, XBC), F32),
                   jax.ShapeDtypeStruct((s // SCH, NG, NS, GW), F32)],
        scratch_shapes=[pltpu.VMEM((TB + 8, XBC), F32), pltpu.VMEM((TB, XBC), F32), pltpu.VMEM((TB, LANE), F32),
                        pltpu.VMEM((SCH, LANE), F32), pltpu.VMEM((LANE, SCH), F32), pltpu.VMEM((NG, NS, GW), F32),
                        pltpu.VMEM((SCH, GW), F32), pltpu.VMEM((SCH, GW), F32), pltpu.VMEM((1, GW), F32)],
        compiler_params=_cp("arbitrary"))(proj, proj, proj, conv_w, conv_b, dt_bias, a_log, d_skip)


def _ssd_bwd(dy, pre, proj, prev, conv_w, dt_bias, a_log, d_skip):
    s = dy.shape[0]
    nb = s // TB
    rev = lambda i: nb - 1 - i
    halo = lambda i: jnp.maximum((nb - 1 - i) * (TB // 8) - 1, 0)

    def body(dy_ref, pre_ref, dt_ref, prev_ref, xs_ref, bc_ref, xsh_ref, bch_ref, cw_ref, dtb_ref, al_ref, dsk_ref,
             dx_ref, ddt_ref, pw_ref, ph_ref,
             ubuf, dtbuf, dpad, xpad, csb, cstb, dhs, yoffb, ebuf, xdecb, wbuf, drow, dub):
        i = pl.program_id(0)

        @pl.when(i == 0)
        def _():
            dhs[...] = jnp.zeros_like(dhs)
            dpad[TB:TB + 8, :] = jnp.zeros((8, XBC), F32)
            pw_ref[...] = jnp.zeros_like(pw_ref)
            ph_ref[...] = jnp.zeros_like(ph_ref)

        pre = pre_ref[...]
        ubuf[...] = pre * jax.nn.sigmoid(pre)
        dtbuf[...] = _softplus(dt_ref[...] + dtb_ref[...])
        a = -jnp.exp(al_ref[...])
        tri = _tri(True).astype(F32)
        trit = _tri(False).astype(F32)
        causal = _tri(True)
        rowid = lax.broadcasted_iota(jnp.int32, (LANE, SCH), 0)
        lane = lax.broadcasted_iota(jnp.int32, (SCH, LANE), 1)
        lane1 = lax.broadcasted_iota(jnp.int32, (1, LANE), 1)
        lastrow = lax.broadcasted_iota(jnp.int32, (SCH, LANE), 0) == SCH - 1

        def chunk(cc, carry):
            c = NCB - 1 - cc
            r0 = pl.multiple_of(c * SCH, SCH)
            rows = pl.ds(r0, SCH)
            dtc = dtbuf[rows, :]
            cs = jnp.dot(tri, dtc * a, precision=HI, preferred_element_type=F32)
            csb[...] = cs
            cstb[...] = cs.T
            dcs = jnp.zeros((SCH, LANE), F32)
            dcst = jnp.zeros((LANE, SCH), F32)
            xr = jnp.zeros((SCH, LANE), F32)
            dlast = jnp.zeros((1, LANE), F32)
            dsk = jnp.zeros((1, LANE), F32)
            for g in range(NG):
                bg = ubuf[rows, SW + NS * g:SW + NS * g + NS]
                cg = ubuf[rows, SW + NG * NS + NS * g:SW + NG * NS + NS * g + NS]
                bb = bg.astype(BF)
                cbf = cg.astype(BF)
                cb = lax.dot_general(cbf, bb, (((1,), (1,)), ((), ())), preferred_element_type=F32)
                hp = prev_ref[c, g]
                hpb = hp.astype(BF)
                dhg = dhs[g]
                dhb = dhg.astype(BF)
                yoffb[...] = jnp.dot(cbf, hpb, preferred_element_type=F32)
                for r in range(8):
                    h = 8 * g + r
                    sl = slice(HD * h, HD * h + HD)
                    rs = slice(HD * r, HD * r + HD)
                    cs_h = csb[:, h:h + 1]
                    cl = csb[SCH - 1:SCH, h:h + 1]
                    dyh = dy_ref[rows, sl]
                    ecs = jnp.exp(cs_h)
                    ebuf[:, rs] = dyh * ecs
                    xdecb[:, rs] = ubuf[rows, sl] * dtbuf[rows, h:h + 1] * jnp.exp(cl - cs_h)
                    t1 = jnp.sum(dyh * ecs * yoffb[:, rs], axis=1, keepdims=True)
                    dcs = dcs + jnp.where(lane == h, t1, 0.0)
                    drow[:, rs] = jnp.broadcast_to(jnp.exp(cl), (1, HD))
                eb = ebuf[...].astype(BF)
                dc = lax.dot_general(eb, hpb, (((1,), (1,)), ((), ())), preferred_element_type=F32)
                dprev = jnp.dot(cg.T.astype(BF), eb, preferred_element_type=F32)
                wbuf[...] = jnp.dot(bb, dhb, preferred_element_type=F32)
                db = lax.dot_general(xdecb[...].astype(BF), dhb, (((1,), (1,)), ((), ())), preferred_element_type=F32)
                dcbs = jnp.zeros((SCH, SCH), F32)
                for r in range(8):
                    h = 8 * g + r
                    sl = slice(HD * h, HD * h + HD)
                    rs = slice(HD * r, HD * r + HD)
                    cs_h = csb[:, h:h + 1]
                    cst_h = cstb[h:h + 1, :]
                    cl = csb[SCH - 1:SCH, h:h + 1]
                    seg = jnp.exp(jnp.where(causal, cs_h - cst_h, -jnp.inf))
                    dyh = dy_ref[rows, sl]
                    xh = ubuf[rows, sl]
                    dt_h = dtbuf[rows, h:h + 1]
                    xdt = xh * dt_h
                    dyb = dyh.astype(BF)
                    dm = lax.dot_general(dyb, xdt.astype(BF), _DIMS["NT"], preferred_element_type=F32)
                    mm_ = cb * seg
                    dxdt = lax.dot_general(mm_.astype(BF), dyb, _DIMS["TN"], preferred_element_type=F32)
                    gg = dm * mm_
                    dcs_h = jnp.sum(gg, axis=1, keepdims=True)
                    dcst = dcst + jnp.where(rowid == h, jnp.sum(gg, axis=0, keepdims=True), 0.0)
                    dcbs = dcbs + dm * seg
                    dec = jnp.exp(cl - cs_h)
                    w_h = wbuf[:, rs]
                    dxdt = dxdt + dec * w_h
                    t = jnp.sum(w_h * xdt, axis=1, keepdims=True) * dec
                    dcs_h = dcs_h - t
                    dch = jnp.sum(jnp.sum(dhs[g, :, rs] * prev_ref[c, g, :, rs], axis=1, keepdims=True),
                                  axis=0, keepdims=True)
                    dl = jnp.sum(t, axis=0, keepdims=True) + dch * jnp.exp(cl)
                    dcs = dcs + jnp.where(lane == h, dcs_h, 0.0)
                    dlast = dlast + jnp.where(lane1 == h, dl, 0.0)
                    xr = xr + jnp.where(lane == h, jnp.sum(dxdt * xh, axis=1, keepdims=True), 0.0)
                    dskh = jnp.sum(jnp.sum(dyh * xh, axis=1, keepdims=True), axis=0, keepdims=True)
                    dsk = dsk + jnp.where(lane1 == h, dskh, 0.0)
                    dub[:, sl] = dyh * dsk_ref[:, h:h + 1] + dxdt * dt_h
                dc = dc + jnp.dot(dcbs.astype(BF), bb, preferred_element_type=F32)
                db = db + lax.dot_general(dcbs.astype(BF), cbf, _DIMS["TN"], preferred_element_type=F32)
                dub[:, SW + NS * g:SW + NS * g + NS] = db
                dub[:, SW + NG * NS + NS * g:SW + NG * NS + NS * g + NS] = dc
                dhs[g] = dprev + drow[...] * dhg
            dcs = dcs - dcst.T + jnp.where(lastrow, dlast, 0.0)
            dadt = jnp.dot(trit, dcs, precision=HI, preferred_element_type=F32)
            ddt = dadt * a + xr
            ddtr = ddt * jax.nn.sigmoid(dt_ref[rows, :] + dtb_ref[...])
            ddt_ref[rows, :] = ddtr.astype(BF)
            ph_ref[0:1, :] += jnp.sum(ddtr, axis=0, keepdims=True)
            ph_ref[1:2, :] += jnp.sum(dadt * dtc, axis=0, keepdims=True) * a
            ph_ref[2:3, :] += dsk
            pc = pre_ref[rows, :]
            sg = jax.nn.sigmoid(pc)
            dpad[rows, :] = dub[...] * _silu_grad(pc, sg)
            return carry

        lax.fori_loop(0, NCB, chunk, 0)

        dxb = cw_ref[0:1, :] * dpad[3:3 + TB, :]
        for kk in range(1, 4):
            dxb = dxb + cw_ref[kk:kk + 1, :] * dpad[3 - kk:3 - kk + TB, :]
        dx_ref[...] = dxb.astype(BF)
        keep = jnp.where(i < nb - 1, 1.0, 0.0)
        xpad[0:8, 0:SW] = xsh_ref[...] * keep
        xpad[0:8, SW:XBC] = bch_ref[...] * keep
        xpad[8:8 + TB, 0:SW] = xs_ref[...]
        xpad[8:8 + TB, SW:XBC] = bc_ref[...]
        dp = dpad[0:TB, :]
        for kk in range(4):
            pw_ref[kk:kk + 1, :] += jnp.sum(dp * xpad[5 + kk:5 + kk + TB, :], axis=0, keepdims=True)
        pw_ref[4:5, :] += jnp.sum(dp, axis=0, keepdims=True)
        dpad[TB:TB + 8, :] = dpad[0:8, :]

    rblk = lambda w, col: pl.BlockSpec((TB, w), lambda i, _c=col: (rev(i), _c))
    return pl.pallas_call(
        body, name="ssd_bwd", grid=(nb,),
        in_specs=[rblk(SW, 0), rblk(XBC, 0), rblk(LANE, 44),
                  pl.BlockSpec((NCB, NG, NS, GW), lambda i: (rev(i), 0, 0, 0)),
                  rblk(SW, 4), rblk(512, 10),
                  pl.BlockSpec((8, SW), lambda i: (halo(i), 4)), pl.BlockSpec((8, 512), lambda i: (halo(i), 10)),
                  _fix((4, XBC)), _fix((1, LANE)), _fix((1, LANE)), _fix((1, LANE))],
        out_specs=[rblk(XBC, 0), rblk(LANE, 0), _fix((8, XBC)), _fix((8, LANE))],
        out_shape=[jax.ShapeDtypeStruct((s, XBC), BF), jax.ShapeDtypeStruct((s, LANE), BF),
                   jax.ShapeDtypeStruct((8, XBC), F32), jax.ShapeDtypeStruct((8, LANE), F32)],
        scratch_shapes=[pltpu.VMEM((TB, XBC), F32), pltpu.VMEM((TB, LANE), F32), pltpu.VMEM((TB + 8, XBC), F32),
                        pltpu.VMEM((TB + 8, XBC), F32), pltpu.VMEM((SCH, LANE), F32), pltpu.VMEM((LANE, SCH), F32),
                        pltpu.VMEM((NG, NS, GW), F32), pltpu.VMEM((SCH, GW), F32), pltpu.VMEM((SCH, GW), F32),
                        pltpu.VMEM((SCH, GW), F32), pltpu.VMEM((SCH, GW), F32), pltpu.VMEM((1, GW), F32),
                        pltpu.VMEM((SCH, XBC), F32)],
        compiler_params=_cp("arbitrary"))(dy, pre, proj, prev, proj, proj, proj, proj, conv_w, dt_bias, a_log, d_skip)


def _rel_index():
    q = jnp.arange(CH)[:, None] + (BAND - CH)
    k = jnp.arange(BAND)[None, :]
    return jnp.clip(q - k, -(CH - 1), REL_CLIP) + (CH - 1)


def _local_step(x, tgt, mods, first_weights, late_weights, grads_ready,
                g_mix, rel_bias, conv_w, conv_b, dt_bias, a_log, d_skip, g_att, g_ssd, g_ffn, g_final):
    sh1, sc1, gt1, sh2, sc2, gt2 = mods
    pad16 = lambda v: jnp.pad(v, ((0, 0), (0, LANE - NH)))
    dtb_p, al_p, dsk_p = pad16(dt_bias), pad16(a_log), pad16(d_skip)
    ridx = _rel_index()
    ext = jnp.concatenate([jnp.broadcast_to(rel_bias[:, NREL - 1:], (NH, BAND + CH - 1 - NREL)),
                           rel_bias[:, ::-1]], axis=1)
    skew = jnp.tile(jnp.pad(ext, ((0, 0), (0, 1))), (1, CH + 1))[:, :CH * (BAND + CH + 1)]
    bias = skew.reshape(NH, CH, BAND + CH + 1)[:, ::-1, :BAND]
    bias_w = _bias_window(bias)
    bias_wt = jnp.transpose(bias_w.reshape(NH // 2, 2, QB, WIN), (0, 3, 1, 2)).reshape(NH // 2, WIN, 2 * QB)

    h1 = _norm_fwd(x, g_mix, sc1, sh1, name="norm1_fwd")
    onehot = (ridx.T.reshape(BAND * CH, 1) == jnp.arange(NREL)[None, :]).astype(F32)
    w_in_t = first_weights([h1, bias_w, bias_wt, onehot])
    proj = _mm([(h1, w_in_t)], "NT", F32, "in_proj", 1024, 1920, 2048)
    att, lse = _attn2_fwd(proj, bias_w)
    y, pre, prev = _ssd_fwd(proj, conv_w, conv_b, dtb_p, al_p, dsk_p)
    cat = _mixout_fwd(att, y, proj, g_att, g_ssd)
    (w_out,) = late_weights("out", cat)
    mix = _mm([(cat, w_out)], "NN", F32, "out_proj", 1024, 1024, 2048)
    h2, x1 = _norm_fwd(x, g_ffn, sc2, sh2, mix, gt1, name="norm2_fwd")
    w_gate_t, w_up_t = late_weights("gate_up", h2)
    gate, up, act = _gate_up(h2, w_gate_t, w_up_t)
    (w_down,) = late_weights("down", act)
    ffn = _mm([(act, w_down)], "NN", F32, "down_proj", 512, 1024, FF)
    dx2, dffn, pf = _final(x1, ffn, gt2, g_final, tgt)

    gw_down = _mm([(act, dffn)], "TN", BF, "gw_down", 1408, 2048, 1024)
    dgate, dup = _gate_up_bwd(dffn, w_down, gate, up)
    dh2 = _mm([(dgate, w_gate_t), (dup, w_up_t)], "NN", BF, "d_h2", 512, 512, FF)
    gw_gate_t = _mm([(dgate, h2)], "TN", BF, "gw_gate", 1408, 2048, 1024)
    gw_up_t = _mm([(dup, h2)], "TN", BF, "gw_up", 1408, 2048, 1024)
    tok = grads_ready("ffn", gw_gate_t, gw_up_t, gw_down)
    dx1, dmix, p2 = _norm_bwd(x1, g_ffn + tok[0, 0], sc2, sh2, dh2, dx2, mix, gt1, name="norm2_bwd")
    gw_out = _mm([(cat, dmix)], "TN", BF, "gw_out", 1024, 2048, 1024)
    tok = grads_ready("out", gw_out)
    dcat = _mm([(dmix, w_out)], "NT", BF, "d_cat", 1024, 1024, 2048, after=tok)
    datt, dy, dz, pm = _mixout_bwd(dcat, att, y, proj, g_att, g_ssd)
    dq, dk, dv, dbias_t = _attn2_bwd(proj, datt, lse[:, :NH].T, bias_wt)
    dxbc, ddt, pw, ph = _ssd_bwd(dy, pre, proj, prev, conv_w, dtb_p, al_p, dsk_p)
    dproj = jnp.concatenate([dq, dk, dv, dz, dxbc, ddt], axis=1)
    gw_in_t = _mm([(dproj, h1)], "TN", BF, "gw_in", 1152, 2048, 1024)
    tok = grads_ready("in", gw_in_t)
    dh1 = _mm([(dproj, w_in_t)], "NN", BF, "d_h1", 512, 1024, INP, after=tok)
    gx, p1 = _norm_bwd(x, g_mix, sc1, sh1, dh1, dx1, name="norm1_bwd")
    g_rel =_mm([(dbias_t.reshape(NH, BAND * CH), onehot)], "NN", F32, "g_rel_bias", NH, NREL, 4608, precision=HI)

    small = dict(
        dmods=jnp.concatenate([p1[1], p1[0], p2[3], p2[1], p2[0], pf[1]]),
        g_mix=p1[2], conv_b=pw[4], dt_bias=ph[0], a_log=ph[1], d_skip=ph[2], g_att=pm[0], g_ssd=pm[1],
        g_ffn=p2[2], g_final=pf[0], rel_bias=g_rel.reshape(-1), conv_w=pw[0:4].reshape(-1), loss=pf[2])
    return gx, small


def _exchange(srcs, gather, name):
    n = len(srcs)
    outs = [jax.ShapeDtypeStruct((NDEV,) + (s.shape if g else s.shape[1:]), s.dtype) for s, g in zip(srcs, gather)]
    outs.append(jax.ShapeDtypeStruct((8, LANE), F32))

    def body(*refs):
        src, dst, token = refs[:n], refs[n:2 * n], refs[2 * n]
        ssem, rsem, lsem = refs[2 * n + 1:]
        token[...] = jnp.zeros_like(token)
        x, y, c = lax.axis_index("x"), lax.axis_index("y"), lax.axis_index("c")
        me = 4 * x + 2 * y + c
        local = []
        for a in range(n):
            own = src[a] if gather[a] else src[a].at[me]
            cp = pltpu.make_async_copy(own, dst[a].at[me], lsem.at[a])
            cp.start()
            local.append(cp)
        remote = []
        for k in range(1, NDEV):
            px = 1 - x if k & 4 else x
            py = 1 - y if k & 2 else y
            pc = 1 - c if k & 1 else c
            pid = 4 * px + 2 * py + pc
            for a in range(n):
                s_ref = src[a] if gather[a] else src[a].at[pid]
                cp = pltpu.make_async_remote_copy(
                    src_ref=s_ref, dst_ref=dst[a].at[me], send_sem=ssem.at[a * (NDEV - 1) + k - 1],
                    recv_sem=rsem.at[a * (NDEV - 1) + k - 1], device_id=(px, py, pc), device_id_type=MESH)
                cp.start()
                remote.append(cp)
        for cp in remote:
            cp.wait()
        for cp in local:
            cp.wait()

    anyspec = pl.BlockSpec(memory_space=pl.ANY)
    return pl.pallas_call(
        body, name=name, out_shape=outs, in_specs=[anyspec] * n,
        out_specs=[anyspec] * n + [pl.BlockSpec(memory_space=pltpu.VMEM)],
        scratch_shapes=[pltpu.SemaphoreType.DMA((n * (NDEV - 1),)), pltpu.SemaphoreType.DMA((n * (NDEV - 1),)),
                        pltpu.SemaphoreType.DMA((n,))],
    )(*srcs)


_HBM = pl.BlockSpec(memory_space=pltpu.HBM)
_SEM = pl.BlockSpec(memory_space=pltpu.SEMAPHORE)
_EFFECT = pltpu.SideEffectType.DATAFLOW_SIDE_EFFECTING


def _peer_copies(src, land, ssem, rsem, gather):
    x, y, c = lax.axis_index("x"), lax.axis_index("y"), lax.axis_index("c")
    me = 4 * x + 2 * y + c
    copies = []
    for a in range(len(src)):
        for k in range(1, NDEV):
            px = 1 - x if k & 4 else x
            py = 1 - y if k & 2 else y
            pc = 1 - c if k & 1 else c
            s_ref = src[a] if gather[a] else src[a].at[4 * px + 2 * py + pc]
            idx = a * (NDEV - 1) + k - 1
            copies.append(pltpu.make_async_remote_copy(
                src_ref=s_ref, dst_ref=land[a].at[me], send_sem=ssem.at[idx], recv_sem=rsem.at[idx],
                device_id=(px, py, pc), device_id_type=MESH))
    return copies


def _xstart(srcs, gather, name):
    n = len(srcs)
    me = 4 * lax.axis_index("x") + 2 * lax.axis_index("y") + lax.axis_index("c")
    lands = []
    for s_, g_ in zip(srcs, gather):
        own = s_ if g_ else lax.dynamic_index_in_dim(s_, me, 0, keepdims=False)
        empty = lax.empty((NDEV,) + own.shape, own.dtype)
        lands.append(lax.dynamic_update_slice(empty, own[None], (me,) + (0,) * own.ndim))
    nsem = n * (NDEV - 1)

    def body(*refs):
        src, land, ssem, rsem, token = refs[:n], refs[n:2 * n], refs[2 * n], refs[2 * n + 1], refs[-1]
        for cp in _peer_copies(src, land, ssem, rsem, gather):
            cp.start()
        token[...] = jnp.zeros_like(token)

    ops = [pltpu.with_memory_space_constraint(a_, pltpu.HBM) for a_ in list(srcs) + lands]
    return pl.pallas_call(
        body, name=name,
        out_shape=(pltpu.SemaphoreType.DMA((nsem,)), pltpu.SemaphoreType.DMA((nsem,)),
                   *[pltpu.HBM(a_.shape, a_.dtype) for a_ in ops], jax.ShapeDtypeStruct((8, LANE), F32)),
        in_specs=[_HBM] * (2 * n),
        out_specs=(_SEM, _SEM, *[_HBM] * (2 * n), pl.BlockSpec(memory_space=pltpu.VMEM)),
        input_output_aliases={i: 2 + i for i in range(2 * n)},
        compiler_params=pltpu.CompilerParams(has_side_effects=_EFFECT),
    )(*ops)


def _xwait(started, gather, after, name):
    n = len(gather)
    ssem, rsem = started[0], started[1]
    bufs = started[2:2 + 2 * n]
    after = list(after) if isinstance(after, (list, tuple)) else [after]

    def body(*refs):
        src, land, ssem_, rsem_ = refs[:n], refs[n:2 * n], refs[2 * n], refs[2 * n + 1]
        for cp in _peer_copies(src, land, ssem_, rsem_, gather):
            cp.wait_send()
            cp.wait_recv()

    out = pl.pallas_call(
        body, name=name, out_shape=tuple(pltpu.HBM(a_.shape, a_.dtype) for a_ in bufs),
        in_specs=[_HBM] * (2 * n) + [_SEM, _SEM] + [pl.BlockSpec(memory_space=pl.ANY)] * len(after),
        out_specs=[_HBM] * (2 * n), input_output_aliases={i: i for i in range(2 * n)},
        compiler_params=pltpu.CompilerParams(has_side_effects=_EFFECT),
    )(*bufs, ssem, rsem, *after)
    return out[n:]


def _mods(c16, w_shard, b_shard, *, tn=512):
    n = w_shard.shape[1]

    def body(c_ref, w_ref, b_ref, o_ref):
        cv = c_ref[...]
        cond = (cv * jax.nn.sigmoid(cv)).astype(BF)
        o_ref[...] = jnp.dot(cond, w_ref[...].astype(BF), preferred_element_type=F32) + b_ref[...]

    return pl.pallas_call(body, name="ada_mods", grid=(n // tn,),
                          in_specs=[_fix((16, D)), pl.BlockSpec((D, tn), lambda j: (0, j)),
                                    pl.BlockSpec((1, tn), lambda j: (0, j))],
                          out_specs=pl.BlockSpec((16, tn), lambda j: (0, j)),
                          out_shape=jax.ShapeDtypeStruct((16, n), F32), compiler_params=_cp("parallel"))(c16, w_shard, b_shard)


def _gw_ada(c_t, dm, *, tr=256):
    n = dm.shape[1]

    def body(ct_ref, dm_ref, o_ref):
        acc = jnp.zeros((tr, n), F32)
        for b in range(NDEV):
            cv = ct_ref[:, b:b + 1]
            cond = (cv * jax.nn.sigmoid(cv)).astype(BF).astype(F32)
            acc = acc + cond * dm_ref[b:b + 1, :].astype(BF).astype(F32)
        o_ref[...] = acc

    return pl.pallas_call(body, name="gw_ada", grid=(D // tr,),
                          in_specs=[_row(tr, NDEV), _fix((NDEV, n))], out_specs=_row(tr, n),
                          out_shape=jax.ShapeDtypeStruct((D, n), F32), compiler_params=_cp("parallel"))(c_t, dm)


def _sum8(parts, *, name, tc=512):
    _, r, c = parts.shape
    tc = min(tc, c)

    def body(p_ref, o_ref):
        acc = p_ref[0].astype(F32)
        for d in range(1, NDEV):
            acc = acc + p_ref[d].astype(F32)
        o_ref[...] = acc

    return pl.pallas_call(body, name=name, grid=(c // tc,),
                          in_specs=[pl.BlockSpec((NDEV, r, tc), lambda j: (0, 0, j))],
                          out_specs=pl.BlockSpec((r, tc), lambda j: (0, j)),
                          out_shape=jax.ShapeDtypeStruct((r, c), F32), compiler_params=_cp("parallel"))(parts)


def _adam(w, m, v, *, recv=None, grad=None, tr=None, tc=None, name):
    lead = w.ndim == 3
    r, c = w.shape[-2:]
    summed = recv is not None
    assert tc is not None or r % tr == 0
    at = (0,) if lead else (Ellipsis,)

    def body(g_ref, w_ref, m_ref, v_ref, go_ref, do_ref, mo_ref, vo_ref):
        if summed:
            g = g_ref[0].astype(F32)
            for d in range(1, NDEV):
                g = g + g_ref[d].astype(F32)
        else:
            g = g_ref[...]
        go_ref[at] = g
        mn = B1 * m_ref[at] + (1.0 - B1) * g
        vn = B2 * v_ref[at] + (1.0 - B2) * (g * g)
        mo_ref[at] = mn
        vo_ref[at] = vn
        mh = mn / (1.0 - B1 ** STEP)
        vh = vn / (1.0 - B2 ** STEP)
        do_ref[at] = -LR * (mh / (jnp.sqrt(vh) + AEPS) + WD * w_ref[at])

    if tc is not None:
        assert not lead and c % tc == 0
        gspec = (pl.BlockSpec((NDEV, r, tc), lambda j: (0, 0, j)) if summed
                 else pl.BlockSpec((r, tc), lambda j: (0, j)))
        wspec = pl.BlockSpec((r, tc), lambda j: (0, j))
        return pl.pallas_call(body, name=name, grid=(c // tc,),
                              in_specs=[gspec, wspec, wspec, wspec], out_specs=[wspec] * 4,
                              out_shape=[jax.ShapeDtypeStruct(w.shape, F32)] * 4,
                              compiler_params=_cp("parallel"))(recv if summed else grad, w, m, v)
    gspec = pl.BlockSpec((NDEV, tr, c), lambda i: (0, i, 0)) if summed else _row(tr, c)
    wspec = pl.BlockSpec((1, tr, c), lambda i: (0, i, 0)) if lead else _row(tr, c)
    return pl.pallas_call(body, name=name, grid=(r // tr,),
                          in_specs=[gspec, wspec, wspec, wspec], out_specs=[wspec] * 4,
                          out_shape=[jax.ShapeDtypeStruct(w.shape, F32)] * 4,
                          compiler_params=_cp("parallel"))(recv if summed else grad, w, m, v)


def _blocks_to_cols(b):
    _, r, width = b.shape
    return jnp.transpose(b, (1, 0, 2)).reshape(r, NDEV * width)


_SMALL = ("b_ada", "g_mix", "rel_bias", "conv_w", "conv_b", "dt_bias", "a_log", "d_skip",
          "g_att_out", "g_ssd_out", "g_ffn", "g_final")


def _pad_rows(v):
    v = v.reshape(-1)
    return jnp.pad(v, (0, (-v.shape[0]) % LANE))


def kernel(x, c, w_ada, b_ada, g_mix, w_in, rel_bias, conv_w, conv_b, dt_bias, a_log, d_skip, g_att_out, g_ssd_out, w_out, g_ffn, w_gate, w_up, w_down, g_final, loss_target, m_w_ada, m_b_ada, m_g_mix, m_w_in, m_rel_bias, m_conv_w, m_conv_b, m_dt_bias, m_a_log, m_d_skip, m_g_att_out, m_g_ssd_out, m_w_out, m_g_ffn, m_w_gate, m_w_up, m_w_down, m_g_final, v_w_ada, v_b_ada, v_g_mix, v_w_in, v_rel_bias, v_conv_w, v_conv_b, v_dt_bias, v_a_log, v_d_skip, v_g_att_out, v_g_ssd_out, v_w_out, v_g_ffn, v_w_gate, v_w_up, v_w_down, v_g_final):
    me = 4 * lax.axis_index("x") + 2 * lax.axis_index("y") + lax.axis_index("c")
    wts = dict(w_ada=w_ada, b_ada=b_ada, g_mix=g_mix, w_in=w_in, rel_bias=rel_bias, conv_w=conv_w, conv_b=conv_b,
               dt_bias=dt_bias, a_log=a_log, d_skip=d_skip, g_att_out=g_att_out, g_ssd_out=g_ssd_out, w_out=w_out,
               g_ffn=g_ffn, w_gate=w_gate, w_up=w_up, w_down=w_down, g_final=g_final)
    mom = dict(w_ada=m_w_ada, b_ada=m_b_ada, g_mix=m_g_mix, w_in=m_w_in, rel_bias=m_rel_bias, conv_w=m_conv_w,
               conv_b=m_conv_b, dt_bias=m_dt_bias, a_log=m_a_log, d_skip=m_d_skip, g_att_out=m_g_att_out,
               g_ssd_out=m_g_ssd_out, w_out=m_w_out, g_ffn=m_g_ffn, w_gate=m_w_gate, w_up=m_w_up, w_down=m_w_down,
               g_final=m_g_final)
    var = dict(w_ada=v_w_ada, b_ada=v_b_ada, g_mix=v_g_mix, w_in=v_w_in, rel_bias=v_rel_bias, conv_w=v_conv_w,
               conv_b=v_conv_b, dt_bias=v_dt_bias, a_log=v_a_log, d_skip=v_d_skip, g_att_out=v_g_att_out,
               g_ssd_out=v_g_ssd_out, w_out=v_w_out, g_ffn=v_g_ffn, w_gate=v_w_gate, w_up=v_w_up, w_down=v_w_down,
               g_final=v_g_final)
    order = ("w_ada", "b_ada", "g_mix", "w_in", "rel_bias", "conv_w", "conv_b", "dt_bias", "a_log", "d_skip",
             "g_att_out", "g_ssd_out", "w_out", "g_ffn", "w_gate", "w_up", "w_down", "g_final")

    c_all, rel_all, cw_all, _ = _exchange([c, rel_bias[0], conv_w[0]], [True] * 3, "gather_small")
    c_all = c_all.reshape(NDEV, D)
    rel_full = _blocks_to_cols(rel_all)
    cw_full = _blocks_to_cols(cw_all)

    ncol = w_ada.shape[2]
    b_sh = lax.dynamic_slice(b_ada, (0, me * ncol), (1, ncol))
    mods_part = _mods(jnp.pad(c_all, ((0, 8), (0, 0))), w_ada[0], b_sh)[:NDEV]
    mods_all, tok_m = _exchange([mods_part], [True], "gather_mods")
    mods_me = lax.dynamic_index_in_dim(mods_all, me, axis=1, keepdims=False).reshape(6, 1, D)
    mods = [mods_me[i] for i in range(6)]

    st_in = _xstart([(w_in[0] + tok_m[0, 0]).T.astype(BF)], [True], "gather_w_in_start")
    st_late = dict(out=_xstart([(w_out[0] + st_in[-1][0, 0]).astype(BF)], [True], "gather_w_out_start"))
    st_late["gate_up"] = _xstart([(w_gate[0] + st_late["out"][-1][0, 0]).T.astype(BF), w_up[0].T.astype(BF)],
                                 [True] * 2, "gather_w_gate_up_start")
    st_late["down"] = _xstart([(w_down[0] + st_late["gate_up"][-1][0, 0]).astype(BF)], [True],
                              "gather_w_down_start")
    st_rest = st_late["down"]

    def first_weights(after):
        (gi,) = _xwait(st_in, [True], after, "gather_w_in_wait")
        return jnp.pad(gi.reshape(INC, D), ((0, INP - INC), (0, 0)))

    def late_weights(which, after):
        got = _xwait(st_late[which], [True] * (2 if which == "gate_up" else 1), after, f"gather_w_{which}_wait")
        return [g_.reshape(-1, D) for g_ in got]

    started = {}

    def grads_ready(which, *g):
        if which == "in":
            g = (g[0][:INC],)
        srcs = [g_.reshape(NDEV, g_.shape[0] // NDEV, D) for g_ in g]
        started[which] = _xstart(srcs, [False] * len(srcs), f"exchange_{which}_grads_start")
        return started[which][-1]

    gx, small = _local_step(
        x[0], loss_target[0], mods, first_weights, late_weights, grads_ready,
        g_mix + st_rest[-1][0, 0], rel_full, cw_full, conv_b, dt_bias, a_log, d_skip, g_att_out, g_ssd_out, g_ffn,
        g_final.reshape(1, D))

    loss_row = jnp.pad(jnp.sum(small["loss"]).reshape(1), (0, LANE - 1))
    pack = jnp.concatenate([small[k] for k in ("dmods", "g_mix", "conv_b", "dt_bias", "a_log", "d_skip", "g_att",
                                               "g_ssd", "g_ffn", "g_final", "rel_bias", "conv_w")] + [loss_row])
    packs, _ = _exchange([pack.reshape(-1, LANE)], [True], "gather_small_grads")
    rg, ru, rd = _xwait(started["ffn"], [False] * 3, packs, "exchange_ffn_grads_wait")
    (ro,) = _xwait(started["out"], [False], rg, "exchange_out_grads_wait")
    (ri,) = _xwait(started["in"], [False], ro, "exchange_in_grads_wait")
    big = dict(
        w_gate=[o_.T for o_ in _adam(w_gate[0].T, m_w_gate[0].T, v_w_gate[0].T, recv=rg, tc=512,
                                     name="adam_w_gate")],
        w_up=[o_.T for o_ in _adam(w_up[0].T, m_w_up[0].T, v_w_up[0].T, recv=ru, tc=512, name="adam_w_up")],
        w_down=_adam(w_down[0], m_w_down[0], v_w_down[0], recv=rd, tr=176, name="adam_w_down"))
    tot = _sum8(packs, name="sum_small").reshape(-1)

    sizes = dict(dmods=6 * D, g_mix=D, conv_b=XBC, dt_bias=LANE, a_log=LANE, d_skip=LANE, g_att_out=AW, g_ssd_out=SW,
                 g_ffn=D, g_final=D, rel_bias=NH * NREL, conv_w=4 * XBC, loss=LANE)
    off, o = {}, 0
    for k_, n_ in sizes.items():
        off[k_] = o
        o += n_
    take = lambda k_, n_=None: tot[off[k_]:off[k_] + (n_ or sizes[k_])]
    loss = tot[off["loss"]]
    rel_w = NREL // NDEV
    cw_w = XBC // NDEV
    gsm = dict(
        b_ada=take("dmods"), g_mix=take("g_mix"),
        rel_bias=lax.dynamic_slice(take("rel_bias").reshape(NH, NREL), (0, me * rel_w), (NH, rel_w)),
        conv_w=lax.dynamic_slice(take("conv_w").reshape(4, XBC), (0, me * cw_w), (4, cw_w)),
        conv_b=take("conv_b"), dt_bias=take("dt_bias", NH), a_log=take("a_log", NH), d_skip=take("d_skip", NH),
        g_att_out=take("g_att_out"), g_ssd_out=take("g_ssd_out"), g_ffn=take("g_ffn"), g_final=take("g_final"))

    flat = lambda d_: jnp.concatenate([_pad_rows(d_[k_]) for k_ in _SMALL])
    gflat = flat(gsm)
    nrow = -(-gflat.shape[0] // (8 * LANE)) * 8
    to2d = lambda v_: jnp.pad(v_, (0, nrow * LANE - v_.shape[0])).reshape(nrow, LANE)
    res = _adam(to2d(flat(wts)), to2d(flat(mom)), to2d(flat(var)), grad=to2d(gflat), tr=nrow, name="adam_small")
    outs = {k_: {} for k_ in ("grad", "delta", "m", "v")}
    o = 0
    for k_ in _SMALL:
        n_ = wts[k_].size
        for kind, arr in zip(("grad", "delta", "m", "v"), res):
            outs[kind][k_] = arr.reshape(-1)[o:o + n_].reshape(wts[k_].shape)
        o += n_ + (-n_) % LANE

    dm_all = packs[:, :6 * D // LANE, :].reshape(NDEV, 6 * D)
    dm_sh = lax.dynamic_slice(dm_all, (0, me * ncol), (NDEV, ncol))
    gwa = _gw_ada(c_all.T, dm_sh)
    big.update(
        w_ada=_adam(w_ada[0], m_w_ada[0], v_w_ada[0], grad=gwa, tr=256, name="adam_w_ada"),
        w_in=[o_.T for o_ in _adam(w_in[0].T, m_w_in[0].T, v_w_in[0].T, recv=ri, tc=512, name="adam_w_in")],
        w_out=_adam(w_out[0], m_w_out[0], v_w_out[0], recv=ro, tr=128, name="adam_w_out"))
    for k_, r_ in big.items():
        for kind, arr in zip(("grad", "delta", "m", "v"), r_):
            outs[kind][k_] = arr.reshape(wts[k_].shape)

    return (loss, gx.reshape(x.shape), *[outs["grad"][k_] for k_ in order], *[outs["delta"][k_] for k_ in order],
            *[outs["m"][k_] for k_ in order], *[outs["v"][k_] for k_ in order])
```

```python
import functools

import jax
import jax.numpy as jnp
from jax import lax
from jax.experimental import pallas as pl
from jax.experimental.pallas import tpu as pltpu

F32 = jnp.float32
BF = jnp.bfloat16
HI = lax.Precision.HIGHEST

D = 2048
NH = 16
HD = 64
AW = 1024
SW = 1024
NG = 2
NS = 128
XBC = 1536
CH = 64
BAND = 576
NREL = 320
REL_CLIP = 256
FF = 5632
INC = 5648
INP = 5760
EPS = 1e-6
NDEV = 8
LANE = 128
VMEM_LIMIT = 56 * 1024 * 1024
LR, B1, B2, AEPS, WD, STEP = 0.001, 0.9, 0.999, 1e-08, 0.01, 10
MESH = pl.DeviceIdType.MESH


def _cp(*sem):
    return pltpu.CompilerParams(dimension_semantics=sem, vmem_limit_bytes=VMEM_LIMIT)


def _row(ts, w, col=0):
    return pl.BlockSpec((ts, w), lambda i, _c=col: (i, _c))


def _fix(shape, single=False):
    nd = len(shape)
    if single:
        return pl.BlockSpec(shape, lambda i, _n=nd: (0,) * _n, pipeline_mode=pl.Buffered(1))
    return pl.BlockSpec(shape, lambda i, _n=nd: (0,) * _n)


def _silu_grad(x, s):
    return s * (1.0 + x * (1.0 - s))


def _softplus(x):
    return jnp.maximum(x, 0.0) + jnp.log1p(jnp.exp(-jnp.abs(x)))


_DIMS = {"NN": (((1,), (0,)), ((), ())), "TN": (((0,), (0,)), ((), ())), "NT": (((1,), (1,)), ((), ()))}


def _mm(pairs, mode, out_dtype, name, tm, tn, tk, precision=None, after=None):
    a0, b0 = pairs[0]
    m, k = a0.shape[::-1] if mode == "TN" else a0.shape
    n = b0.shape[0] if mode == "NT" else b0.shape[1]
    tm, tn, tk = min(tm, m), min(tn, n), min(tk, k)
    nk1 = k // tk
    npair = len(pairs)
    nk = 1 if nk1 == 1 else nk1 * npair
    assert m % tm == 0 and n % tn == 0 and k % tk == 0, (name, a0.shape, b0.shape)
    dims = _DIMS[mode]

    def kloc(kk, p):
        return jnp.clip(kk - p * nk1, 0, nk1 - 1)

    def aspec(p):
        if mode == "TN":
            return pl.BlockSpec((tk, tm), lambda i, j, kk, _p=p: (kloc(kk, _p), i))
        return pl.BlockSpec((tm, tk), lambda i, j, kk, _p=p: (i, kloc(kk, _p)))

    def bspec(p):
        extra = dict(pipeline_mode=pl.Buffered(1)) if (nk == 1 and tn == n) else {}
        if mode == "NT":
            return pl.BlockSpec((tn, tk), lambda i, j, kk, _p=p: (j, kloc(kk, _p)), **extra)
        return pl.BlockSpec((tk, tn), lambda i, j, kk, _p=p: (kloc(kk, _p), j), **extra)

    nin = 2 * npair + (after is not None)

    def body(*refs):
        ab, o_ref = refs[:2 * npair], refs[nin]
        if nk == 1:
            tot = lax.dot_general(ab[0][...], ab[1][...], dims, preferred_element_type=F32, precision=precision)
            for p in range(1, npair):
                tot = tot + lax.dot_general(ab[2 * p][...], ab[2 * p + 1][...], dims, preferred_element_type=F32,
                                            precision=precision)
            o_ref[...] = tot.astype(o_ref.dtype)
            return
        acc = refs[nin + 1]
        kk = pl.program_id(2)

        @pl.when(kk == 0)
        def _():
            acc[...] = jnp.zeros_like(acc)

        for p in range(npair):
            @pl.when((kk >= p * nk1) & (kk < (p + 1) * nk1))
            def _(p=p):
                acc[...] += lax.dot_general(ab[2 * p][...], ab[2 * p + 1][...], dims, preferred_element_type=F32,
                                            precision=precision)

        @pl.when(kk == nk - 1)
        def _():
            o_ref[...] = acc[...].astype(o_ref.dtype)

    in_specs = []
    for p in range(npair):
        in_specs += [aspec(p), bspec(p)]
    operands = [t for ab_ in pairs for t in ab_]
    if after is not None:
        in_specs.append(pl.BlockSpec(memory_space=pl.ANY))
        operands.append(after)
    return pl.pallas_call(
        body, name=name, grid=(m // tm, n // tn, nk), in_specs=in_specs,
        out_specs=pl.BlockSpec((tm, tn), lambda i, j, kk: (i, j)),
        out_shape=jax.ShapeDtypeStruct((m, n), out_dtype),
        scratch_shapes=[] if nk == 1 else [pltpu.VMEM((tm, tn), F32)],
        compiler_params=_cp("parallel", "parallel", "arbitrary"),
    )(*operands)


def _norm_fwd(x, g, sc, sh, mix=None, gt=None, *, name, ts=512):
    s, d = x.shape
    has = mix is not None

    def body(*refs):
        if has:
            x_ref, g_ref, sc_ref, sh_ref, mix_ref, gt_ref, h_ref, x1_ref = refs
        else:
            x_ref, g_ref, sc_ref, sh_ref, h_ref = refs
        xv = x_ref[...]
        if has:
            xv = xv + gt_ref[...] * mix_ref[...]
            x1_ref[...] = xv
        r = lax.rsqrt(jnp.mean(xv * xv, axis=-1, keepdims=True) + EPS)
        h_ref[...] = ((xv * r) * g_ref[...] * (1.0 + sc_ref[...]) + sh_ref[...]).astype(BF)

    ins = [x, g, sc, sh] + ([mix, gt] if has else [])
    in_specs = [_row(ts, d), _fix((1, d)), _fix((1, d)), _fix((1, d))] + ([_row(ts, d), _fix((1, d))] if has else [])
    out_shape = [jax.ShapeDtypeStruct((s, d), BF)] + ([jax.ShapeDtypeStruct((s, d), F32)] if has else [])
    out_specs = [_row(ts, d)] + ([_row(ts, d)] if has else [])
    out = pl.pallas_call(body, name=name, grid=(s // ts,), in_specs=in_specs, out_specs=out_specs,
                         out_shape=out_shape, compiler_params=_cp("parallel"))(*ins)
    return out if has else out[0]


def _norm_bwd(x, g, sc, sh, dh, dres, mix=None, gt=None, *, name, ts=256):
    s, d = x.shape
    has = mix is not None

    def body(*refs):
        if has:
            x_ref, g_ref, sc_ref, sh_ref, dh_ref, dres_ref, mix_ref, gt_ref, dx_ref, dmix_ref, part_ref = refs
        else:
            x_ref, g_ref, sc_ref, sh_ref, dh_ref, dres_ref, dx_ref, part_ref = refs
        i = pl.program_id(0)
        xv = x_ref[...]
        gv = g_ref[...]
        dhv = dh_ref[...].astype(F32)
        r = lax.rsqrt(jnp.mean(xv * xv, axis=-1, keepdims=True) + EPS)
        xh = xv * r
        dyn = dhv * (1.0 + sc_ref[...])
        gy = dyn * gv
        dx = dres_ref[...] + r * (gy - xh * jnp.mean(gy * xh, axis=-1, keepdims=True))
        dx_ref[...] = dx

        @pl.when(i == 0)
        def _():
            part_ref[...] = jnp.zeros_like(part_ref)

        part_ref[0:1, :] += jnp.sum(dhv * (xh * gv), axis=0, keepdims=True)
        part_ref[1:2, :] += jnp.sum(dhv, axis=0, keepdims=True)
        part_ref[2:3, :] += jnp.sum(dyn * xh, axis=0, keepdims=True)
        if has:
            part_ref[3:4, :] += jnp.sum(dx * mix_ref[...], axis=0, keepdims=True)
            dmix_ref[...] = (gt_ref[...] * dx).astype(BF)

    ins = [x, g, sc, sh, dh, dres] + ([mix, gt] if has else [])
    in_specs = [_row(ts, d), _fix((1, d)), _fix((1, d)), _fix((1, d)), _row(ts, d), _row(ts, d)]
    in_specs += [_row(ts, d), _fix((1, d))] if has else []
    out_shape = [jax.ShapeDtypeStruct((s, d), F32)] + ([jax.ShapeDtypeStruct((s, d), BF)] if has else [])
    out_shape += [jax.ShapeDtypeStruct((8, d), F32)]
    out_specs = [_row(ts, d)] + ([_row(ts, d)] if has else []) + [_fix((8, d))]
    return pl.pallas_call(body, name=name, grid=(s // ts,), in_specs=in_specs, out_specs=out_specs,
                          out_shape=out_shape, compiler_params=_cp("arbitrary"))(*ins)


def _final(x1, ffn, gt2, gf, tgt, *, ts=256):
    s, d = x1.shape

    def body(x1_ref, ffn_ref, gt_ref, g_ref, t_ref, dx_ref, dffn_ref, part_ref):
        i = pl.program_id(0)
        fv = ffn_ref[...]
        gv = g_ref[...]
        xv = x1_ref[...] + gt_ref[...] * fv
        r = lax.rsqrt(jnp.mean(xv * xv, axis=-1, keepdims=True) + EPS)
        xh = xv * r
        e = xh * gv - t_ref[...]
        dy = e * (1.0 / d)
        gy = dy * gv
        dx = r * (gy - xh * jnp.mean(gy * xh, axis=-1, keepdims=True))
        dx_ref[...] = dx
        dffn_ref[...] = (gt_ref[...] * dx).astype(BF)

        @pl.when(i == 0)
        def _():
            part_ref[...] = jnp.zeros_like(part_ref)

        part_ref[0:1, :] += jnp.sum(dy * xh, axis=0, keepdims=True)
        part_ref[1:2, :] += jnp.sum(dx * fv, axis=0, keepdims=True)
        part_ref[2:3, :] += jnp.sum(e * e, axis=0, keepdims=True) * (0.5 / d)

    return pl.pallas_call(
        body, name="final_loss", grid=(s // ts,),
        in_specs=[_row(ts, d), _row(ts, d), _fix((1, d)), _fix((1, d)), _row(ts, d)],
        out_specs=[_row(ts, d), _row(ts, d), _fix((8, d))],
        out_shape=[jax.ShapeDtypeStruct((s, d), F32), jax.ShapeDtypeStruct((s, d), BF),
                   jax.ShapeDtypeStruct((8, d), F32)],
        compiler_params=_cp("arbitrary"))(x1, ffn, gt2, gf, tgt)


def _gate_up(h, wg_t, wu_t, *, tm=1024, tn=512, sub=512):
    s, k = h.shape
    n = wg_t.shape[0]
    tm = min(tm, s)

    def body(h_ref, wg_ref, wu_ref, g_ref, u_ref, a_ref):
        hv = h_ref[...]
        for c0 in range(0, tn, sub):
            cs_ = slice(c0, c0 + sub)
            gv = lax.dot_general(hv, wg_ref[cs_, :], _DIMS["NT"], preferred_element_type=F32)
            uv = lax.dot_general(hv, wu_ref[cs_, :], _DIMS["NT"], preferred_element_type=F32)
            g_ref[:, cs_] = gv.astype(BF)
            u_ref[:, cs_] = uv.astype(BF)
            a_ref[:, cs_] = (gv * jax.nn.sigmoid(gv) * uv).astype(BF)

    wspec = pl.BlockSpec((tn, k), lambda i, j: (j, 0))
    ospec = pl.BlockSpec((tm, tn), lambda i, j: (i, j))
    return pl.pallas_call(body, name="gate_up", grid=(s // tm, n // tn),
                          in_specs=[pl.BlockSpec((tm, k), lambda i, j: (i, 0)), wspec, wspec],
                          out_specs=[ospec] * 3, out_shape=[jax.ShapeDtypeStruct((s, n), BF)] * 3,
                          compiler_params=_cp("parallel", "parallel"))(h, wg_t, wu_t)


def _gate_up_bwd(dffn, w_down, gate, up, *, tm=1024, tn=512, sub=512):
    s, k = dffn.shape
    n = w_down.shape[0]
    tm = min(tm, s)

    def body(d_ref, w_ref, g_ref, u_ref, dg_ref, du_ref):
        dv_ = d_ref[...]
        for c0 in range(0, tn, sub):
            cs_ = slice(c0, c0 + sub)
            dav = lax.dot_general(dv_, w_ref[cs_, :], _DIMS["NT"], preferred_element_type=F32)
            gv = g_ref[:, cs_].astype(F32)
            uv = u_ref[:, cs_].astype(F32)
            sg = jax.nn.sigmoid(gv)
            dg_ref[:, cs_] = (dav * uv * _silu_grad(gv, sg)).astype(BF)
            du_ref[:, cs_] = (dav * gv * sg).astype(BF)

    tile = pl.BlockSpec((tm, tn), lambda i, j: (i, j))
    return pl.pallas_call(body, name="gate_up_bwd", grid=(s // tm, n // tn),
                          in_specs=[pl.BlockSpec((tm, k), lambda i, j: (i, 0)),
                                    pl.BlockSpec((tn, k), lambda i, j: (j, 0)), tile, tile],
                          out_specs=[tile] * 2, out_shape=[jax.ShapeDtypeStruct((s, n), BF)] * 2,
                          compiler_params=_cp("parallel", "parallel"))(dffn, w_down, gate, up)


def _mixout_fwd(att, y, proj, g_att, g_ssd, *, ts=512):
    s = att.shape[0]

    def body(a_ref, y_ref, z_ref, ga_ref, gs_ref, o_ref):
        av = a_ref[...]
        ra = lax.rsqrt(jnp.mean(av * av, axis=-1, keepdims=True) + EPS)
        o_ref[:, 0:AW] = (av * ra * ga_ref[...]).astype(BF)
        zv = z_ref[...]
        yz = y_ref[...] * (zv * jax.nn.sigmoid(zv))
        rs = lax.rsqrt(jnp.mean(yz * yz, axis=-1, keepdims=True) + EPS)
        o_ref[:, AW:AW + SW] = (yz * rs * gs_ref[...]).astype(BF)

    return pl.pallas_call(body, name="mixout_fwd", grid=(s // ts,),
                          in_specs=[_row(ts, AW), _row(ts, SW), _row(ts, SW, 3), _fix((1, AW)), _fix((1, SW))],
                          out_specs=_row(ts, AW + SW), out_shape=jax.ShapeDtypeStruct((s, AW + SW), BF),
                          compiler_params=_cp("parallel"))(att, y, proj, g_att, g_ssd)


def _mixout_bwd(dcat, att, y, proj, g_att, g_ssd, *, ts=512):
    s = att.shape[0]

    def body(dc_ref, a_ref, y_ref, z_ref, ga_ref, gs_ref, da_ref, dy_ref, dz_ref, part_ref):
        i = pl.program_id(0)
        av = a_ref[...]
        dca = dc_ref[:, 0:AW].astype(F32)
        ra = lax.rsqrt(jnp.mean(av * av, axis=-1, keepdims=True) + EPS)
        ah = av * ra
        gy = dca * ga_ref[...]
        da_ref[...] = ra * (gy - ah * jnp.mean(gy * ah, axis=-1, keepdims=True))
        zv = z_ref[...]
        yv = y_ref[...]
        sg = jax.nn.sigmoid(zv)
        sz = zv * sg
        yz = yv * sz
        dcs = dc_ref[:, AW:AW + SW].astype(F32)
        rs = lax.rsqrt(jnp.mean(yz * yz, axis=-1, keepdims=True) + EPS)
        yh = yz * rs
        gys = dcs * gs_ref[...]
        dyz = rs * (gys - yh * jnp.mean(gys * yh, axis=-1, keepdims=True))
        dy_ref[...] = dyz * sz
        dz_ref[...] = (dyz * yv * _silu_grad(zv, sg)).astype(BF)

        @pl.when(i == 0)
        def _():
            part_ref[...] = jnp.zeros_like(part_ref)

        part_ref[0:1, :] += jnp.sum(dca * ah, axis=0, keepdims=True)
        part_ref[1:2, :] += jnp.sum(dcs * yh, axis=0, keepdims=True)

    return pl.pallas_call(
        body, name="mixout_bwd", grid=(s // ts,),
        in_specs=[_row(ts, AW + SW), _row(ts, AW), _row(ts, SW), _row(ts, SW, 3), _fix((1, AW)), _fix((1, SW))],
        out_specs=[_row(ts, AW), _row(ts, SW), _row(ts, SW), _fix((8, AW))],
        out_shape=[jax.ShapeDtypeStruct((s, AW), F32), jax.ShapeDtypeStruct((s, SW), F32),
                   jax.ShapeDtypeStruct((s, SW), BF), jax.ShapeDtypeStruct((8, AW), F32)],
        compiler_params=_cp("arbitrary"))(dcat, att, y, proj, g_att, g_ssd)


SCALE = HD ** -0.5
QB = 256
NQC = QB // CH
WIN = BAND - CH + QB
NKB = WIN // QB


def _bias_window(bias):
    rows = [jnp.pad(bias, ((0, 0), (0, 0), (CH * a, WIN - BAND - CH * a)), constant_values=-jnp.inf)
            for a in range(NQC)]
    return jnp.concatenate(rows, axis=1)


def _win_specs(width, nb):
    def mk(col, back):
        return pl.BlockSpec((QB, width), lambda i, _c=col, _k=back: (jnp.maximum(jnp.minimum(i, nb - 1) - _k, 0), _c))
    return [mk(col, back) for col in (1, 2) for back in range(NKB - 1, -1, -1)]


def _attn2_fwd(qkv, bias_w):
    s = qkv.shape[0]
    nb = s // QB

    def body(q_ref, *refs):
        krefs, vrefs = refs[:NKB], refs[NKB:2 * NKB]
        b_ref, o_ref, lse_ref, kbuf, vbuf = refs[2 * NKB:]
        i = pl.program_id(0)
        for j in range(NKB):
            kbuf[QB * j:QB * j + QB, :] = krefs[j][...].astype(BF)
            vbuf[QB * j:QB * j + QB, :] = vrefs[j][...].astype(BF)
        valid = (i * QB - (NKB - 1) * QB + lax.broadcasted_iota(jnp.int32, (1, WIN), 1)) >= 0
        lane = lax.broadcasted_iota(jnp.int32, (QB, LANE), 1)

        lo = lane < HD

        def heads(masked):
            lse = jnp.zeros((QB, LANE), F32)
            for p in range(NH // 2):
                sl = slice(2 * HD * p, 2 * HD * p + 2 * HD)
                qp = q_ref[:, sl] * SCALE
                qbd = jnp.concatenate([jnp.where(lo, qp, 0.0), jnp.where(lo, 0.0, qp)], axis=0).astype(BF)
                sc = lax.dot_general(qbd, kbuf[:, sl], _DIMS["NT"], preferred_element_type=F32) + b_ref[p]
                if masked:
                    sc = jnp.where(valid, sc, -jnp.inf)
                mx = jnp.max(sc, axis=-1, keepdims=True)
                e = jnp.exp(sc - mx)
                tot = jnp.sum(e, axis=-1, keepdims=True)
                o2 = jnp.dot(e.astype(BF), vbuf[:, sl], preferred_element_type=F32) * (1.0 / tot)
                o_ref[:, sl] = jnp.where(lo, o2[0:QB], o2[QB:2 * QB])
                l2 = mx + jnp.log(tot)
                lse = jnp.where(lane == 2 * p, l2[0:QB], jnp.where(lane == 2 * p + 1, l2[QB:2 * QB], lse))
            lse_ref[...] = lse

        pl.when(i < NKB - 1)(functools.partial(heads, True))
        pl.when(i >= NKB - 1)(functools.partial(heads, False))

    return pl.pallas_call(
        body, name="attn_fwd", grid=(nb,),
        in_specs=[_row(QB, AW)] + _win_specs(AW, nb) + [_fix((NH // 2, 2 * QB, WIN), single=True)],
        out_specs=[_row(QB, AW), _row(QB, LANE)],
        out_shape=[jax.ShapeDtypeStruct((s, AW), F32), jax.ShapeDtypeStruct((s, LANE), F32)],
        scratch_shapes=[pltpu.VMEM((WIN, AW), BF), pltpu.VMEM((WIN, AW), BF)],
        compiler_params=_cp("parallel"))(*[qkv] * (1 + 2 * NKB), bias_w.reshape(NH // 2, 2 * QB, WIN))


def _attn2_bwd(qkv, datt, lse_t, bias_wt):
    s = qkv.shape[0]
    nb = s // QB
    cur = lambda i: jnp.minimum(i, nb - 1)
    late = lambda i: jnp.maximum(i - (NKB - 1), 0)

    def body(q_ref, *refs):
        krefs, vrefs = refs[:NKB], refs[NKB:2 * NKB]
        do_ref, lse_ref, b_ref, dq_ref, dk_ref, dv_ref, db_ref, kbuf, vbuf, dkacc, dvacc, dsbuf = refs[2 * NKB:]
        i = pl.program_id(0)

        @pl.when(i == 0)
        def _():
            dkacc[...] = jnp.zeros_like(dkacc)
            dvacc[...] = jnp.zeros_like(dvacc)
            db_ref[...] = jnp.zeros_like(db_ref)

        def heads(masked):
            for j in range(NKB):
                kbuf[QB * j:QB * j + QB, :] = krefs[j][...].astype(BF)
                vbuf[QB * j:QB * j + QB, :] = vrefs[j][...].astype(BF)
            valid = (i * QB - (NKB - 1) * QB + lax.broadcasted_iota(jnp.int32, (WIN, 1), 0)) >= 0
            lo = lax.broadcasted_iota(jnp.int32, (QB, LANE), 1) < HD

            def both(ref, sl):
                t = ref[:, sl]
                return jnp.concatenate([jnp.where(lo, t, 0.0), jnp.where(lo, 0.0, t)], axis=0)

            for p in range(NH // 2):
                sl = slice(2 * HD * p, 2 * HD * p + 2 * HD)
                qbd = (both(q_ref, sl) * SCALE).astype(BF)
                dobd = both(do_ref, sl).astype(BF)
                k = kbuf[:, sl]
                st = lax.dot_general(k, qbd, _DIMS["NT"], preferred_element_type=F32) + b_ref[p]
                if masked:
                    st = jnp.where(valid, st, -jnp.inf)
                lrow = jnp.concatenate([lse_ref[2 * p:2 * p + 1, :], lse_ref[2 * p + 1:2 * p + 2, :]], axis=1)
                pt = jnp.exp(st - lrow)
                dvacc[:, sl] += jnp.dot(pt.astype(BF), dobd, preferred_element_type=F32)
                dpt = lax.dot_general(vbuf[:, sl], dobd, _DIMS["NT"], preferred_element_type=F32)
                dst = pt * (dpt - jnp.sum(dpt * pt, axis=0, keepdims=True))
                dsbuf[...] = dst
                for hh in range(2):
                    fold = dsbuf[0:BAND, QB * hh:QB * hh + CH]
                    for a in range(1, NQC):
                        fold = fold + dsbuf[CH * a:CH * a + BAND, QB * hh + CH * a:QB * hh + CH * a + CH]
                    db_ref[2 * p + hh] += fold
                dsb = dst.astype(BF)
                dkacc[:, sl] += jnp.dot(dsb, qbd, preferred_element_type=F32)
                dq2 = lax.dot_general(dsb, k, _DIMS["TN"], preferred_element_type=F32) * SCALE
                dq_ref[:, sl] = jnp.where(lo, dq2[0:QB], dq2[QB:2 * QB]).astype(BF)

        pl.when(i < nb)(functools.partial(heads, True))

        dk_ref[...] = dkacc[0:QB, :].astype(BF)
        dv_ref[...] = dvacc[0:QB, :].astype(BF)
        for acc in (dkacc, dvacc):
            for j in range(NKB - 1):
                acc[QB * j:QB * j + QB, :] = acc[QB * j + QB:QB * j + 2 * QB, :]
            acc[WIN - QB:WIN, :] = jnp.zeros((QB, AW), F32)

    qspec = pl.BlockSpec((QB, AW), lambda i: (cur(i), 0))
    lspec = pl.BlockSpec((QB, AW), lambda i: (late(i), 0))
    return pl.pallas_call(
        body, name="attn_bwd", grid=(nb + NKB - 1,),
        in_specs=[qspec] + _win_specs(AW, nb) + [qspec, pl.BlockSpec((NH, QB), lambda i: (0, cur(i))),
                                                 _fix((NH // 2, WIN, 2 * QB), single=True)],
        out_specs=[qspec, lspec, lspec, _fix((NH, BAND, CH))],
        out_shape=[jax.ShapeDtypeStruct((s, AW), BF)] * 3 + [jax.ShapeDtypeStruct((NH, BAND, CH), F32)],
        scratch_shapes=[pltpu.VMEM((WIN, AW), BF), pltpu.VMEM((WIN, AW), BF),
                        pltpu.VMEM((WIN, AW), F32), pltpu.VMEM((WIN, AW), F32), pltpu.VMEM((WIN, 2 * QB), F32)],
        compiler_params=_cp("arbitrary"))(*[qkv] * (1 + 2 * NKB), datt, lse_t, bias_wt)


TB = 512
SCH = 512
NCB = TB // SCH
GW = 8 * HD


def _tri(lower):
    r = lax.broadcasted_iota(jnp.int32, (SCH, SCH), 0)
    c = lax.broadcasted_iota(jnp.int32, (SCH, SCH), 1)
    return r >= c if lower else r <= c


def _ssd_fwd(proj, conv_w, conv_b, dt_bias, a_log, d_skip):
    s = proj.shape[0]
    nb = s // TB

    def body(xs_ref, bc_ref, dt_ref, cw_ref, cb_ref, dtb_ref, al_ref, dsk_ref, y_ref, pre_ref, prev_ref,
             xpad, ubuf, dtbuf, csb, cstb, hst, xdec, yoffb, drow):
        i = pl.program_id(0)

        @pl.when(i == 0)
        def _():
            xpad[0:8, :] = jnp.zeros((8, XBC), F32)
            hst[...] = jnp.zeros_like(hst)

        xpad[8:8 + TB, 0:SW] = xs_ref[...]
        xpad[8:8 + TB, SW:XBC] = bc_ref[...]
        pre = cb_ref[...]
        for kk in range(4):
            pre = pre + cw_ref[kk:kk + 1, :] * xpad[5 + kk:5 + kk + TB, :]
        pre_ref[...] = pre
        ubuf[...] = pre * jax.nn.sigmoid(pre)
        xpad[0:8, :] = xpad[TB:TB + 8, :]
        dtbuf[...] = _softplus(dt_ref[...] + dtb_ref[...])
        a = -jnp.exp(al_ref[...])
        tri = _tri(True).astype(F32)
        causal = _tri(True)

        def chunk(c, carry):
            r0 = pl.multiple_of(c * SCH, SCH)
            rows = pl.ds(r0, SCH)
            cs = jnp.dot(tri, dtbuf[rows, :] * a, precision=HI, preferred_element_type=F32)
            csb[...] = cs
            cstb[...] = cs.T
            for g in range(NG):
                bg = ubuf[rows, SW + NS * g:SW + NS * g + NS]
                cg = ubuf[rows, SW + NG * NS + NS * g:SW + NG * NS + NS * g + NS].astype(BF)
                cb = lax.dot_general(cg, bg.astype(BF), (((1,), (1,)), ((), ())), preferred_element_type=F32)
                hg = hst[g]
                prev_ref[c, g] = hg
                yoffb[...] = jnp.dot(cg, hg.astype(BF), preferred_element_type=F32)
                for r in range(8):
                    h = 8 * g + r
                    sl = slice(HD * h, HD * h + HD)
                    rs = slice(HD * r, HD * r + HD)
                    cs_h = csb[:, h:h + 1]
                    cl = csb[SCH - 1:SCH, h:h + 1]
                    seg = jnp.exp(jnp.where(causal, cs_h - cstb[h:h + 1, :], -jnp.inf))
                    xh = ubuf[rows, sl]
                    xdt = xh * dtbuf[rows, h:h + 1]
                    yd = jnp.dot((cb * seg).astype(BF), xdt.astype(BF), preferred_element_type=F32)
                    y_ref[rows, sl] = yd + jnp.exp(cs_h) * yoffb[:, rs] + xh * dsk_ref[:, h:h + 1]
                    xdec[:, rs] = xdt * jnp.exp(cl - cs_h)
                    drow[:, rs] = jnp.broadcast_to(jnp.exp(cl), (1, HD))
                st = jnp.dot(bg.T.astype(BF), xdec[...].astype(BF), preferred_element_type=F32)
                hst[g] = hg * drow[...] + st
            return carry

        lax.fori_loop(0, NCB, chunk, 0)

    return pl.pallas_call(
        body, name="ssd_fwd", grid=(nb,),
        in_specs=[_row(TB, SW, 4), _row(TB, 512, 10), _row(TB, LANE, 44), _fix((4, XBC)), _fix((1, XBC)),
                  _fix((1, LANE)), _fix((1, LANE)), _fix((1, LANE))],
        out_specs=[_row(TB, SW), _row(TB, XBC), pl.BlockSpec((NCB, NG, NS, GW), lambda i: (i, 0, 0, 0))],
        out_shape=[jax.ShapeDtypeStruct((s, SW), F32), jax.ShapeDtypeStruct((s, XBC), F32),
                   jax.ShapeDtypeStruct((s // SCH, NG, NS, GW), F32)],
        scratch_shapes=[pltpu.VMEM((TB + 8, XBC), F32), pltpu.VMEM((TB, XBC), F32), pltpu.VMEM((TB, LANE), F32),
                        pltpu.VMEM((SCH, LANE), F32), pltpu.VMEM((LANE, SCH), F32), pltpu.VMEM((NG, NS, GW), F32),
                        pltpu.VMEM((SCH, GW), F32), pltpu.VMEM((SCH, GW), F32), pltpu.VMEM((1, GW), F32)],
        compiler_params=_cp("arbitrary"))(proj, proj, proj, conv_w, conv_b, dt_bias, a_log, d_skip)


def _ssd_bwd(dy, pre, proj, prev, conv_w, dt_bias, a_log, d_skip):
    s = dy.shape[0]
    nb = s // TB
    rev = lambda i: nb - 1 - i
    halo = lambda i: jnp.maximum((nb - 1 - i) * (TB // 8) - 1, 0)

    def body(dy_ref, pre_ref, dt_ref, prev_ref, xs_ref, bc_ref, xsh_ref, bch_ref, cw_ref, dtb_ref, al_ref, dsk_ref,
             dx_ref, ddt_ref, pw_ref, ph_ref,
             ubuf, dtbuf, dpad, xpad, csb, cstb, dhs, yoffb, ebuf, xdecb, wbuf, drow, dub):
        i = pl.program_id(0)

        @pl.when(i == 0)
        def _():
            dhs[...] = jnp.zeros_like(dhs)
            dpad[TB:TB + 8, :] = jnp.zeros((8, XBC), F32)
            pw_ref[...] = jnp.zeros_like(pw_ref)
            ph_ref[...] = jnp.zeros_like(ph_ref)

        pre = pre_ref[...]
        ubuf[...] = pre * jax.nn.sigmoid(pre)
        dtbuf[...] = _softplus(dt_ref[...] + dtb_ref[...])
        a = -jnp.exp(al_ref[...])
        tri = _tri(True).astype(F32)
        trit = _tri(False).astype(F32)
        causal = _tri(True)
        rowid = lax.broadcasted_iota(jnp.int32, (LANE, SCH), 0)
        lane = lax.broadcasted_iota(jnp.int32, (SCH, LANE), 1)
        lane1 = lax.broadcasted_iota(jnp.int32, (1, LANE), 1)
        lastrow = lax.broadcasted_iota(jnp.int32, (SCH, LANE), 0) == SCH - 1

        def chunk(cc, carry):
            c = NCB - 1 - cc
            r0 = pl.multiple_of(c * SCH, SCH)
            rows = pl.ds(r0, SCH)
            dtc = dtbuf[rows, :]
            cs = jnp.dot(tri, dtc * a, precision=HI, preferred_element_type=F32)
            csb[...] = cs
            cstb[...] = cs.T
            dcs = jnp.zeros((SCH, LANE), F32)
            dcst = jnp.zeros((LANE, SCH), F32)
            xr = jnp.zeros((SCH, LANE), F32)
            dlast = jnp.zeros((1, LANE), F32)
            dsk = jnp.zeros((1, LANE), F32)
            for g in range(NG):
                bg = ubuf[rows, SW + NS * g:SW + NS * g + NS]
                cg = ubuf[rows, SW + NG * NS + NS * g:SW + NG * NS + NS * g + NS]
                bb = bg.astype(BF)
                cbf = cg.astype(BF)
                cb = lax.dot_general(cbf, bb, (((1,), (1,)), ((), ())), preferred_element_type=F32)
                hp = prev_ref[c, g]
                hpb = hp.astype(BF)
                dhg = dhs[g]
                dhb = dhg.astype(BF)
                yoffb[...] = jnp.dot(cbf, hpb, preferred_element_type=F32)
                for r in range(8):
                    h = 8 * g + r
                    sl = slice(HD * h, HD * h + HD)
                    rs = slice(HD * r, HD * r + HD)
                    cs_h = csb[:, h:h + 1]
                    cl = csb[SCH - 1:SCH, h:h + 1]
                    dyh = dy_ref[rows, sl]
                    ecs = jnp.exp(cs_h)
                    ebuf[:, rs] = dyh * ecs
                    xdecb[:, rs] = ubuf[rows, sl] * dtbuf[rows, h:h + 1] * jnp.exp(cl - cs_h)
                    t1 = jnp.sum(dyh * ecs * yoffb[:, rs], axis=1, keepdims=True)
                    dcs = dcs + jnp.where(lane == h, t1, 0.0)
                    drow[:, rs] = jnp.broadcast_to(jnp.exp(cl), (1, HD))
                eb = ebuf[...].astype(BF)
                dc = lax.dot_general(eb, hpb, (((1,), (1,)), ((), ())), preferred_element_type=F32)
                dprev = jnp.dot(cg.T.astype(BF), eb, preferred_element_type=F32)
                wbuf[...] = jnp.dot(bb, dhb, preferred_element_type=F32)
                db = lax.dot_general(xdecb[...].astype(BF), dhb, (((1,), (1,)), ((), ())), preferred_element_type=F32)
                dcbs = jnp.zeros((SCH, SCH), F32)
                for r in range(8):
                    h = 8 * g + r
                    sl = slice(HD * h, HD * h + HD)
                    rs = slice(HD * r, HD * r + HD)
                    cs_h = csb[:, h:h + 1]
                    cst_h = cstb[h:h + 1, :]
                    cl = csb[SCH - 1:SCH, h:h + 1]
                    seg = jnp.exp(jnp.where(causal, cs_h - cst_h, -jnp.inf))
                    dyh = dy_ref[rows, sl]
                    xh = ubuf[rows, sl]
                    dt_h = dtbuf[rows, h:h + 1]
                    xdt = xh * dt_h
                    dyb = dyh.astype(BF)
                    dm = lax.dot_general(dyb, xdt.astype(BF), _DIMS["NT"], preferred_element_type=F32)
                    mm_ = cb * seg
                    dxdt = lax.dot_general(mm_.astype(BF), dyb, _DIMS["TN"], preferred_element_type=F32)
                    gg = dm * mm_
                    dcs_h = jnp.sum(gg, axis=1, keepdims=True)
                    dcst = dcst + jnp.where(rowid == h, jnp.sum(gg, axis=0, keepdims=True), 0.0)
                    dcbs = dcbs + dm * seg
                    dec = jnp.exp(cl - cs_h)
                    w_h = wbuf[:, rs]
                    dxdt = dxdt + dec * w_h
                    t = jnp.sum(w_h * xdt, axis=1, keepdims=True) * dec
                    dcs_h = dcs_h - t
                    dch = jnp.sum(jnp.sum(dhs[g, :, rs] * prev_ref[c, g, :, rs], axis=1, keepdims=True),
                                  axis=0, keepdims=True)
                    dl = jnp.sum(t, axis=0, keepdims=True) + dch * jnp.exp(cl)
                    dcs = dcs + jnp.where(lane == h, dcs_h, 0.0)
                    dlast = dlast + jnp.where(lane1 == h, dl, 0.0)
                    xr = xr + jnp.where(lane == h, jnp.sum(dxdt * xh, axis=1, keepdims=True), 0.0)
                    dskh = jnp.sum(jnp.sum(dyh * xh, axis=1, keepdims=True), axis=0, keepdims=True)
                    dsk = dsk + jnp.where(lane1 == h, dskh, 0.0)
                    dub[:, sl] = dyh * dsk_ref[:, h:h + 1] + dxdt * dt_h
                dc = dc + jnp.dot(dcbs.astype(BF), bb, preferred_element_type=F32)
                db = db + lax.dot_general(dcbs.astype(BF), cbf, _DIMS["TN"], preferred_element_type=F32)
                dub[:, SW + NS * g:SW + NS * g + NS] = db
                dub[:, SW + NG * NS + NS * g:SW + NG * NS + NS * g + NS] = dc
                dhs[g] = dprev + drow[...] * dhg
            dcs = dcs - dcst.T + jnp.where(lastrow, dlast, 0.0)
            dadt = jnp.dot(trit, dcs, precision=HI, preferred_element_type=F32)
            ddt = dadt * a + xr
            ddtr = ddt * jax.nn.sigmoid(dt_ref[rows, :] + dtb_ref[...])
            ddt_ref[rows, :] = ddtr.astype(BF)
            ph_ref[0:1, :] += jnp.sum(ddtr, axis=0, keepdims=True)
            ph_ref[1:2, :] += jnp.sum(dadt * dtc, axis=0, keepdims=True) * a
            ph_ref[2:3, :] += dsk
            pc = pre_ref[rows, :]
            sg = jax.nn.sigmoid(pc)
            dpad[rows, :] = dub[...] * _silu_grad(pc, sg)
            return carry

        lax.fori_loop(0, NCB, chunk, 0)

        dxb = cw_ref[0:1, :] * dpad[3:3 + TB, :]
        for kk in range(1, 4):
            dxb = dxb + cw_ref[kk:kk + 1, :] * dpad[3 - kk:3 - kk + TB, :]
        dx_ref[...] = dxb.astype(BF)
        keep = jnp.where(i < nb - 1, 1.0, 0.0)
        xpad[0:8, 0:SW] = xsh_ref[...] * keep
        xpad[0:8, SW:XBC] = bch_ref[...] * keep
        xpad[8:8 + TB, 0:SW] = xs_ref[...]
        xpad[8:8 + TB, SW:XBC] = bc_ref[...]
        dp = dpad[0:TB, :]
        for kk in range(4):
            pw_ref[kk:kk + 1, :] += jnp.sum(dp * xpad[5 + kk:5 + kk + TB, :], axis=0, keepdims=True)
        pw_ref[4:5, :] += jnp.sum(dp, axis=0, keepdims=True)
        dpad[TB:TB + 8, :] = dpad[0:8, :]

    rblk = lambda w, col: pl.BlockSpec((TB, w), lambda i, _c=col: (rev(i), _c))
    return pl.pallas_call(
        body, name="ssd_bwd", grid=(nb,),
        in_specs=[rblk(SW, 0), rblk(XBC, 0), rblk(LANE, 44),
                  pl.BlockSpec((NCB, NG, NS, GW), lambda i: (rev(i), 0, 0, 0)),
                  rblk(SW, 4), rblk(512, 10),
                  pl.BlockSpec((8, SW), lambda i: (halo(i), 4)), pl.BlockSpec((8, 512), lambda i: (halo(i), 10)),
                  _fix((4, XBC)), _fix((1, LANE)), _fix((1, LANE)), _fix((1, LANE))],
        out_specs=[rblk(XBC, 0), rblk(LANE, 0), _fix((8, XBC)), _fix((8, LANE))],
        out_shape=[jax.ShapeDtypeStruct((s, XBC), BF), jax.ShapeDtypeStruct((s, LANE), BF),
                   jax.ShapeDtypeStruct((8, XBC), F32), jax.ShapeDtypeStruct((8, LANE), F32)],
        scratch_shapes=[pltpu.VMEM((TB, XBC), F32), pltpu.VMEM((TB, LANE), F32), pltpu.VMEM((TB + 8, XBC), F32),
                        pltpu.VMEM((TB + 8, XBC), F32), pltpu.VMEM((SCH, LANE), F32), pltpu.VMEM((LANE, SCH), F32),
                        pltpu.VMEM((NG, NS, GW), F32), pltpu.VMEM((SCH, GW), F32), pltpu.VMEM((SCH, GW), F32),
                        pltpu.VMEM((SCH, GW), F32), pltpu.VMEM((SCH, GW), F32), pltpu.VMEM((1, GW), F32),
                        pltpu.VMEM((SCH, XBC), F32)],
        compiler_params=_cp("arbitrary"))(dy, pre, proj, prev, proj, proj, proj, proj, conv_w, dt_bias, a_log, d_skip)


def _rel_index():
    q = jnp.arange(CH)[:, None] + (BAND - CH)
    k = jnp.arange(BAND)[None, :]
    return jnp.clip(q - k, -(CH - 1), REL_CLIP) + (CH - 1)


def _local_step(x, tgt, mods, first_weights, late_weights, grads_ready,
                g_mix, rel_bias, conv_w, conv_b, dt_bias, a_log, d_skip, g_att, g_ssd, g_ffn, g_final):
    sh1, sc1, gt1, sh2, sc2, gt2 = mods
    pad16 = lambda v: jnp.pad(v, ((0, 0), (0, LANE - NH)))
    dtb_p, al_p, dsk_p = pad16(dt_bias), pad16(a_log), pad16(d_skip)
    ridx = _rel_index()
    ext = jnp.concatenate([jnp.broadcast_to(rel_bias[:, NREL - 1:], (NH, BAND + CH - 1 - NREL)),
                           rel_bias[:, ::-1]], axis=1)
    skew = jnp.tile(jnp.pad(ext, ((0, 0), (0, 1))), (1, CH + 1))[:, :CH * (BAND + CH + 1)]
    bias = skew.reshape(NH, CH, BAND + CH + 1)[:, ::-1, :BAND]
    bias_w = _bias_window(bias)
    bias_wt = jnp.transpose(bias_w.reshape(NH // 2, 2, QB, WIN), (0, 3, 1, 2)).reshape(NH // 2, WIN, 2 * QB)

    h1 = _norm_fwd(x, g_mix, sc1, sh1, name="norm1_fwd")
    onehot = (ridx.T.reshape(BAND * CH, 1) == jnp.arange(NREL)[None, :]).astype(F32)
    w_in_t = first_weights([h1, bias_w, bias_wt, onehot])
    proj = _mm([(h1, w_in_t)], "NT", F32, "in_proj", 1024, 1920, 2048)
    att, lse = _attn2_fwd(proj, bias_w)
    y, pre, prev = _ssd_fwd(proj, conv_w, conv_b, dtb_p, al_p, dsk_p)
    cat = _mixout_fwd(att, y, proj, g_att, g_ssd)
    (w_out,) = late_weights("out", cat)
    mix = _mm([(cat, w_out)], "NN", F32, "out_proj", 1024, 2048, 2048)
    h2, x1 = _norm_fwd(x, g_ffn, sc2, sh2, mix, gt1, name="norm2_fwd")
    w_gate_t, w_up_t = late_weights("gate_up", h2)
    gate, up, act = _gate_up(h2, w_gate_t, w_up_t)
    (w_down,) = late_weights("down", act)
    ffn = _mm([(act, w_down)], "NN", F32, "down_proj", 512, 2048, FF)
    dx2, dffn, pf = _final(x1, ffn, gt2, g_final, tgt)

    gw_down = _mm([(act, dffn)], "TN", BF, "gw_down", 1408, 2048, 1024)
    dgate, dup = _gate_up_bwd(dffn, w_down, gate, up)
    dh2 = _mm([(dgate, w_gate_t), (dup, w_up_t)], "NN", BF, "d_h2", 512, 512, FF)
    gw_gate_t = _mm([(dgate, h2)], "TN", BF, "gw_gate", 1408, 2048, 1024)
    gw_up_t = _mm([(dup, h2)], "TN", BF, "gw_up", 1408, 2048, 1024)
    tok = grads_ready("ffn", gw_gate_t, gw_up_t, gw_down)
    dx1, dmix, p2 = _norm_bwd(x1, g_ffn + tok[0, 0], sc2, sh2, dh2, dx2, mix, gt1, name="norm2_bwd")
    gw_out = _mm([(cat, dmix)], "TN", BF, "gw_out", 1024, 2048, 1024)
    tok = grads_ready("out", gw_out)
    dcat = _mm([(dmix, w_out)], "NT", BF, "d_cat", 1024, 2048, 2048, after=tok)
    datt, dy, dz, pm = _mixout_bwd(dcat, att, y, proj, g_att, g_ssd)
    dq, dk, dv, dbias_t = _attn2_bwd(proj, datt, lse[:, :NH].T, bias_wt)
    dxbc, ddt, pw, ph = _ssd_bwd(dy, pre, proj, prev, conv_w, dtb_p, al_p, dsk_p)
    dproj = jnp.concatenate([dq, dk, dv, dz, dxbc, ddt], axis=1)
    gw_in_t = _mm([(dproj, h1)], "TN", BF, "gw_in", 1152, 2048, 1024)
    tok = grads_ready("in", gw_in_t)
    dh1 = _mm([(dproj, w_in_t)], "NN", BF, "d_h1", 512, 2048, INP, after=tok)
    gx, p1 = _norm_bwd(x, g_mix, sc1, sh1, dh1, dx1, name="norm1_bwd")
    g_rel =_mm([(dbias_t.reshape(NH, BAND * CH), onehot)], "NN", F32, "g_rel_bias", NH, NREL, 4608, precision=HI)

    small = dict(
        dmods=jnp.concatenate([p1[1], p1[0], p2[3], p2[1], p2[0], pf[1]]),
        g_mix=p1[2], conv_b=pw[4], dt_bias=ph[0], a_log=ph[1], d_skip=ph[2], g_att=pm[0], g_ssd=pm[1],
        g_ffn=p2[2], g_final=pf[0], rel_bias=g_rel.reshape(-1), conv_w=pw[0:4].reshape(-1), loss=pf[2])
    return gx, small


def _exchange(srcs, gather, name):
    n = len(srcs)
    outs = [jax.ShapeDtypeStruct((NDEV,) + (s.shape if g else s.shape[1:]), s.dtype) for s, g in zip(srcs, gather)]
    outs.append(jax.ShapeDtypeStruct((8, LANE), F32))

    def body(*refs):
        src, dst, token = refs[:n], refs[n:2 * n], refs[2 * n]
        ssem, rsem, lsem = refs[2 * n + 1:]
        token[...] = jnp.zeros_like(token)
        x, y, c = lax.axis_index("x"), lax.axis_index("y"), lax.axis_index("c")
        me = 4 * x + 2 * y + c
        local = []
        for a in range(n):
            own = src[a] if gather[a] else src[a].at[me]
            cp = pltpu.make_async_copy(own, dst[a].at[me], lsem.at[a])
            cp.start()
            local.append(cp)
        remote = []
        for k in range(1, NDEV):
            px = 1 - x if k & 4 else x
            py = 1 - y if k & 2 else y
            pc = 1 - c if k & 1 else c
            pid = 4 * px + 2 * py + pc
            for a in range(n):
                s_ref = src[a] if gather[a] else src[a].at[pid]
                cp = pltpu.make_async_remote_copy(
                    src_ref=s_ref, dst_ref=dst[a].at[me], send_sem=ssem.at[a * (NDEV - 1) + k - 1],
                    recv_sem=rsem.at[a * (NDEV - 1) + k - 1], device_id=(px, py, pc), device_id_type=MESH)
                cp.start()
                remote.append(cp)
        for cp in remote:
            cp.wait()
        for cp in local:
            cp.wait()

    anyspec = pl.BlockSpec(memory_space=pl.ANY)
    return pl.pallas_call(
        body, name=name, out_shape=outs, in_specs=[anyspec] * n,
        out_specs=[anyspec] * n + [pl.BlockSpec(memory_space=pltpu.VMEM)],
        scratch_shapes=[pltpu.SemaphoreType.DMA((n * (NDEV - 1),)), pltpu.SemaphoreType.DMA((n * (NDEV - 1),)),
                        pltpu.SemaphoreType.DMA((n,))],
    )(*srcs)


_HBM = pl.BlockSpec(memory_space=pltpu.HBM)
_SEM = pl.BlockSpec(memory_space=pltpu.SEMAPHORE)
_EFFECT = pltpu.SideEffectType.DATAFLOW_SIDE_EFFECTING


def _peer_copies(src, land, ssem, rsem, gather):
    x, y, c = lax.axis_index("x"), lax.axis_index("y"), lax.axis_index("c")
    me = 4 * x + 2 * y + c
    copies = []
    for a in range(len(src)):
        for k in range(1, NDEV):
            px = 1 - x if k & 4 else x
            py = 1 - y if k & 2 else y
            pc = 1 - c if k & 1 else c
            s_ref = src[a] if gather[a] else src[a].at[4 * px + 2 * py + pc]
            idx = a * (NDEV - 1) + k - 1
            copies.append(pltpu.make_async_remote_copy(
                src_ref=s_ref, dst_ref=land[a].at[me], send_sem=ssem.at[idx], recv_sem=rsem.at[idx],
                device_id=(px, py, pc), device_id_type=MESH))
    return copies


def _xstart(srcs, gather, name):
    n = len(srcs)
    me = 4 * lax.axis_index("x") + 2 * lax.axis_index("y") + lax.axis_index("c")
    lands = []
    for s_, g_ in zip(srcs, gather):
        own = s_ if g_ else lax.dynamic_index_in_dim(s_, me, 0, keepdims=False)
        empty = lax.empty((NDEV,) + own.shape, own.dtype)
        lands.append(lax.dynamic_update_slice(empty, own[None], (me,) + (0,) * own.ndim))
    nsem = n * (NDEV - 1)

    def body(*refs):
        src, land, ssem, rsem, token = refs[:n], refs[n:2 * n], refs[2 * n], refs[2 * n + 1], refs[-1]
        for cp in _peer_copies(src, land, ssem, rsem, gather):
            cp.start()
        token[...] = jnp.zeros_like(token)

    ops = [pltpu.with_memory_space_constraint(a_, pltpu.HBM) for a_ in list(srcs) + lands]
    return pl.pallas_call(
        body, name=name,
        out_shape=(pltpu.SemaphoreType.DMA((nsem,)), pltpu.SemaphoreType.DMA((nsem,)),
                   *[pltpu.HBM(a_.shape, a_.dtype) for a_ in ops], jax.ShapeDtypeStruct((8, LANE), F32)),
        in_specs=[_HBM] * (2 * n),
        out_specs=(_SEM, _SEM, *[_HBM] * (2 * n), pl.BlockSpec(memory_space=pltpu.VMEM)),
        input_output_aliases={i: 2 + i for i in range(2 * n)},
        compiler_params=pltpu.CompilerParams(has_side_effects=_EFFECT),
    )(*ops)


def _xwait(started, gather, after, name):
    n = len(gather)
    ssem, rsem = started[0], started[1]
    bufs = started[2:2 + 2 * n]
    after = list(after) if isinstance(after, (list, tuple)) else [after]

    def body(*refs):
        src, land, ssem_, rsem_ = refs[:n], refs[n:2 * n], refs[2 * n], refs[2 * n + 1]
        for cp in _peer_copies(src, land, ssem_, rsem_, gather):
            cp.wait_send()
            cp.wait_recv()

    out = pl.pallas_call(
        body, name=name, out_shape=tuple(pltpu.HBM(a_.shape, a_.dtype) for a_ in bufs),
        in_specs=[_HBM] * (2 * n) + [_SEM, _SEM] + [pl.BlockSpec(memory_space=pl.ANY)] * len(after),
        out_specs=[_HBM] * (2 * n), input_output_aliases={i: i for i in range(2 * n)},
        compiler_params=pltpu.CompilerParams(has_side_effects=_EFFECT),
    )(*bufs, ssem, rsem, *after)
    return out[n:]


def _mods(c16, w_shard, b_shard, *, tn=512):
    n = w_shard.shape[1]

    def body(c_ref, w_ref, b_ref, o_ref):
        cv = c_ref[...]
        cond = (cv * jax.nn.sigmoid(cv)).astype(BF)
        o_ref[...] = jnp.dot(cond, w_ref[...].astype(BF), preferred_element_type=F32) + b_ref[...]

    return pl.pallas_call(body, name="ada_mods", grid=(n // tn,),
                          in_specs=[_fix((16, D)), pl.BlockSpec((D, tn), lambda j: (0, j)),
                                    pl.BlockSpec((1, tn), lambda j: (0, j))],
                          out_specs=pl.BlockSpec((16, tn), lambda j: (0, j)),
                          out_shape=jax.ShapeDtypeStruct((16, n), F32), compiler_params=_cp("parallel"))(c16, w_shard, b_shard)


def _gw_ada(c_t, dm, *, tr=256):
    n = dm.shape[1]

    def body(ct_ref, dm_ref, o_ref):
        acc = jnp.zeros((tr, n), F32)
        for b in range(NDEV):
            cv = ct_ref[:, b:b + 1]
            cond = (cv * jax.nn.sigmoid(cv)).astype(BF).astype(F32)
            acc = acc + cond * dm_ref[b:b + 1, :].astype(BF).astype(F32)
        o_ref[...] = acc

    return pl.pallas_call(body, name="gw_ada", grid=(D // tr,),
                          in_specs=[_row(tr, NDEV), _fix((NDEV, n))], out_specs=_row(tr, n),
                          out_shape=jax.ShapeDtypeStruct((D, n), F32), compiler_params=_cp("parallel"))(c_t, dm)


def _sum8(parts, *, name, tc=512):
    _, r, c = parts.shape
    tc = min(tc, c)

    def body(p_ref, o_ref):
        acc = p_ref[0].astype(F32)
        for d in range(1, NDEV):
            acc = acc + p_ref[d].astype(F32)
        o_ref[...] = acc

    return pl.pallas_call(body, name=name, grid=(c // tc,),
                          in_specs=[pl.BlockSpec((NDEV, r, tc), lambda j: (0, 0, j))],
                          out_specs=pl.BlockSpec((r, tc), lambda j: (0, j)),
                          out_shape=jax.ShapeDtypeStruct((r, c), F32), compiler_params=_cp("parallel"))(parts)


def _adam(w, m, v, *, recv=None, grad=None, tr=None, tc=None, name):
    lead = w.ndim == 3
    r, c = w.shape[-2:]
    summed = recv is not None
    assert tc is not None or r % tr == 0
    at = (0,) if lead else (Ellipsis,)

    def body(g_ref, w_ref, m_ref, v_ref, go_ref, do_ref, mo_ref, vo_ref):
        if summed:
            g = g_ref[0].astype(F32)
            for d in range(1, NDEV):
                g = g + g_ref[d].astype(F32)
        else:
            g = g_ref[...]
        go_ref[at] = g
        mn = B1 * m_ref[at] + (1.0 - B1) * g
        vn = B2 * v_ref[at] + (1.0 - B2) * (g * g)
        mo_ref[at] = mn
        vo_ref[at] = vn
        mh = mn / (1.0 - B1 ** STEP)
        vh = vn / (1.0 - B2 ** STEP)
        do_ref[at] = -LR * (mh / (jnp.sqrt(vh) + AEPS) + WD * w_ref[at])

    if tc is not None:
        assert not lead and c % tc == 0
        gspec = (pl.BlockSpec((NDEV, r, tc), lambda j: (0, 0, j)) if summed
                 else pl.BlockSpec((r, tc), lambda j: (0, j)))
        wspec = pl.BlockSpec((r, tc), lambda j: (0, j))
        return pl.pallas_call(body, name=name, grid=(c // tc,),
                              in_specs=[gspec, wspec, wspec, wspec], out_specs=[wspec] * 4,
                              out_shape=[jax.ShapeDtypeStruct(w.shape, F32)] * 4,
                              compiler_params=_cp("parallel"))(recv if summed else grad, w, m, v)
    gspec = pl.BlockSpec((NDEV, tr, c), lambda i: (0, i, 0)) if summed else _row(tr, c)
    wspec = pl.BlockSpec((1, tr, c), lambda i: (0, i, 0)) if lead else _row(tr, c)
    return pl.pallas_call(body, name=name, grid=(r // tr,),
                          in_specs=[gspec, wspec, wspec, wspec], out_specs=[wspec] * 4,
                          out_shape=[jax.ShapeDtypeStruct(w.shape, F32)] * 4,
                          compiler_params=_cp("parallel"))(recv if summed else grad, w, m, v)


def _blocks_to_cols(b):
    _, r, width = b.shape
    return jnp.transpose(b, (1, 0, 2)).reshape(r, NDEV * width)


_SMALL = ("b_ada", "g_mix", "rel_bias", "conv_w", "conv_b", "dt_bias", "a_log", "d_skip",
          "g_att_out", "g_ssd_out", "g_ffn", "g_final")


def _pad_rows(v):
    v = v.reshape(-1)
    return jnp.pad(v, (0, (-v.shape[0]) % LANE))


def kernel(x, c, w_ada, b_ada, g_mix, w_in, rel_bias, conv_w, conv_b, dt_bias, a_log, d_skip, g_att_out, g_ssd_out, w_out, g_ffn, w_gate, w_up, w_down, g_final, loss_target, m_w_ada, m_b_ada, m_g_mix, m_w_in, m_rel_bias, m_conv_w, m_conv_b, m_dt_bias, m_a_log, m_d_skip, m_g_att_out, m_g_ssd_out, m_w_out, m_g_ffn, m_w_gate, m_w_up, m_w_down, m_g_final, v_w_ada, v_b_ada, v_g_mix, v_w_in, v_rel_bias, v_conv_w, v_conv_b, v_dt_bias, v_a_log, v_d_skip, v_g_att_out, v_g_ssd_out, v_w_out, v_g_ffn, v_w_gate, v_w_up, v_w_down, v_g_final):
    me = 4 * lax.axis_index("x") + 2 * lax.axis_index("y") + lax.axis_index("c")
    wts = dict(w_ada=w_ada, b_ada=b_ada, g_mix=g_mix, w_in=w_in, rel_bias=rel_bias, conv_w=conv_w, conv_b=conv_b,
               dt_bias=dt_bias, a_log=a_log, d_skip=d_skip, g_att_out=g_att_out, g_ssd_out=g_ssd_out, w_out=w_out,
               g_ffn=g_ffn, w_gate=w_gate, w_up=w_up, w_down=w_down, g_final=g_final)
    mom = dict(w_ada=m_w_ada, b_ada=m_b_ada, g_mix=m_g_mix, w_in=m_w_in, rel_bias=m_rel_bias, conv_w=m_conv_w,
               conv_b=m_conv_b, dt_bias=m_dt_bias, a_log=m_a_log, d_skip=m_d_skip, g_att_out=m_g_att_out,
               g_ssd_out=m_g_ssd_out, w_out=m_w_out, g_ffn=m_g_ffn, w_gate=m_w_gate, w_up=m_w_up, w_down=m_w_down,
               g_final=m_g_final)
    var = dict(w_ada=v_w_ada, b_ada=v_b_ada, g_mix=v_g_mix, w_in=v_w_in, rel_bias=v_rel_bias, conv_w=v_conv_w,
               conv_b=v_conv_b, dt_bias=v_dt_bias, a_log=v_a_log, d_skip=v_d_skip, g_att_out=v_g_att_out,
               g_ssd_out=v_g_ssd_out, w_out=v_w_out, g_ffn=v_g_ffn, w_gate=v_w_gate, w_up=v_w_up, w_down=v_w_down,
               g_final=v_g_final)
    order = ("w_ada", "b_ada", "g_mix", "w_in", "rel_bias", "conv_w", "conv_b", "dt_bias", "a_log", "d_skip",
             "g_att_out", "g_ssd_out", "w_out", "g_ffn", "w_gate", "w_up", "w_down", "g_final")

    c_all, rel_all, cw_all, _ = _exchange([c, rel_bias[0], conv_w[0]], [True] * 3, "gather_small")
    c_all = c_all.reshape(NDEV, D)
    rel_full = _blocks_to_cols(rel_all)
    cw_full = _blocks_to_cols(cw_all)

    ncol = w_ada.shape[2]
    b_sh = lax.dynamic_slice(b_ada, (0, me * ncol), (1, ncol))
    mods_part = _mods(jnp.pad(c_all, ((0, 8), (0, 0))), w_ada[0], b_sh)[:NDEV]
    mods_all, tok_m = _exchange([mods_part], [True], "gather_mods")
    mods_me = lax.dynamic_index_in_dim(mods_all, me, axis=1, keepdims=False).reshape(6, 1, D)
    mods = [mods_me[i] for i in range(6)]

    st_in = _xstart([(w_in[0] + tok_m[0, 0]).T.astype(BF)], [True], "gather_w_in_start")
    st_late = dict(out=_xstart([(w_out[0] + st_in[-1][0, 0]).astype(BF)], [True], "gather_w_out_start"))
    st_late["gate_up"] = _xstart([(w_gate[0] + st_late["out"][-1][0, 0]).T.astype(BF), w_up[0].T.astype(BF)],
                                 [True] * 2, "gather_w_gate_up_start")
    st_late["down"] = _xstart([(w_down[0] + st_late["gate_up"][-1][0, 0]).astype(BF)], [True],
                              "gather_w_down_start")
    st_rest = st_late["down"]

    def first_weights(after):
        (gi,) = _xwait(st_in, [True], after, "gather_w_in_wait")
        return jnp.pad(gi.reshape(INC, D), ((0, INP - INC), (0, 0)))

    def late_weights(which, after):
        got = _xwait(st_late[which], [True] * (2 if which == "gate_up" else 1), after, f"gather_w_{which}_wait")
        return [g_.reshape(-1, D) for g_ in got]

    started = {}

    def grads_ready(which, *g):
        if which == "in":
            g = (g[0][:INC],)
        srcs = [g_.reshape(NDEV, g_.shape[0] // NDEV, D) for g_ in g]
        started[which] = _xstart(srcs, [False] * len(srcs), f"exchange_{which}_grads_start")
        return started[which][-1]

    gx, small = _local_step(
        x[0], loss_target[0], mods, first_weights, late_weights, grads_ready,
        g_mix + st_rest[-1][0, 0], rel_full, cw_full, conv_b, dt_bias, a_log, d_skip, g_att_out, g_ssd_out, g_ffn,
        g_final.reshape(1, D))

    loss_row = jnp.pad(jnp.sum(small["loss"]).reshape(1), (0, LANE - 1))
    pack = jnp.concatenate([small[k] for k in ("dmods", "g_mix", "conv_b", "dt_bias", "a_log", "d_skip", "g_att",
                                               "g_ssd", "g_ffn", "g_final", "rel_bias", "conv_w")] + [loss_row])
    packs, _ = _exchange([pack.reshape(-1, LANE)], [True], "gather_small_grads")
    rg, ru, rd = _xwait(started["ffn"], [False] * 3, packs, "exchange_ffn_grads_wait")
    (ro,) = _xwait(started["out"], [False], rg, "exchange_out_grads_wait")
    (ri,) = _xwait(started["in"], [False], ro, "exchange_in_grads_wait")
    big = dict(
        w_gate=[o_.T for o_ in _adam(w_gate[0].T, m_w_gate[0].T, v_w_gate[0].T, recv=rg, tc=512,
                                     name="adam_w_gate")],
        w_up=[o_.T for o_ in _adam(w_up[0].T, m_w_up[0].T, v_w_up[0].T, recv=ru, tc=512, name="adam_w_up")],
        w_down=_adam(w_down[0], m_w_down[0], v_w_down[0], recv=rd, tr=176, name="adam_w_down"))
    tot = _sum8(packs, name="sum_small").reshape(-1)

    sizes = dict(dmods=6 * D, g_mix=D, conv_b=XBC, dt_bias=LANE, a_log=LANE, d_skip=LANE, g_att_out=AW, g_ssd_out=SW,
                 g_ffn=D, g_final=D, rel_bias=NH * NREL, conv_w=4 * XBC, loss=LANE)
    off, o = {}, 0
    for k_, n_ in sizes.items():
        off[k_] = o
        o += n_
    take = lambda k_, n_=None: tot[off[k_]:off[k_] + (n_ or sizes[k_])]
    loss = tot[off["loss"]]
    rel_w = NREL // NDEV
    cw_w = XBC // NDEV
    gsm = dict(
        b_ada=take("dmods"), g_mix=take("g_mix"),
        rel_bias=lax.dynamic_slice(take("rel_bias").reshape(NH, NREL), (0, me * rel_w), (NH, rel_w)),
        conv_w=lax.dynamic_slice(take("conv_w").reshape(4, XBC), (0, me * cw_w), (4, cw_w)),
        conv_b=take("conv_b"), dt_bias=take("dt_bias", NH), a_log=take("a_log", NH), d_skip=take("d_skip", NH),
        g_att_out=take("g_att_out"), g_ssd_out=take("g_ssd_out"), g_ffn=take("g_ffn"), g_final=take("g_final"))

    flat = lambda d_: jnp.concatenate([_pad_rows(d_[k_]) for k_ in _SMALL])
    gflat = flat(gsm)
    nrow = -(-gflat.shape[0] // (8 * LANE)) * 8
    to2d = lambda v_: jnp.pad(v_, (0, nrow * LANE - v_.shape[0])).reshape(nrow, LANE)
    res = _adam(to2d(flat(wts)), to2d(flat(mom)), to2d(flat(var)), grad=to2d(gflat), tr=nrow, name="adam_small")
    outs = {k_: {} for k_ in ("grad", "delta", "m", "v")}
    o = 0
    for k_ in _SMALL:
        n_ = wts[k_].size
        for kind, arr in zip(("grad", "delta", "m", "v"), res):
            outs[kind][k_] = arr.reshape(-1)[o:o + n_].reshape(wts[k_].shape)
        o += n_ + (-n_) % LANE

    dm_all = packs[:, :6 * D // LANE, :].reshape(NDEV, 6 * D)
    dm_sh = lax.dynamic_slice(dm_all, (0, me * ncol), (NDEV, ncol))
    gwa = _gw_ada(c_all.T, dm_sh)
    big.update(
        w_ada=_adam(w_ada[0], m_w_ada[0], v_w_ada[0], grad=gwa, tr=256, name="adam_w_ada"),
        w_in=[o_.T for o_ in _adam(w_in[0].T, m_w_in[0].T, v_w_in[0].T, recv=ri, tc=512, name="adam_w_in")],
        w_out=_adam(w_out[0], m_w_out[0], v_w_out[0], recv=ro, tr=128, name="adam_w_out"))
    for k_, r_ in big.items():
        for kind, arr in zip(("grad", "delta", "m", "v"), r_):
            outs[kind][k_] = arr.reshape(wts[k_].shape)

    return (loss, gx.reshape(x.shape), *[outs["grad"][k_] for k_ in order], *[outs["delta"][k_] for k_ in order],
            *[outs["m"][k_] for k_ in order], *[outs["v"][k_] for k_ in order])
```

```python
import functools

import jax
import jax.numpy as jnp
from jax import lax
from jax.experimental import pallas as pl
from jax.experimental.pallas import tpu as pltpu

F32 = jnp.float32
BF = jnp.bfloat16
HI = lax.Precision.HIGHEST

D = 2048
NH = 16
HD = 64
AW = 1024
SW = 1024
NG = 2
NS = 128
XBC = 1536
CH = 64
BAND = 576
NREL = 320
REL_CLIP = 256
FF = 5632
INC = 5648
INP = 5760
EPS = 1e-6
NDEV = 8
LANE = 128
VMEM_LIMIT = 56 * 1024 * 1024
LR, B1, B2, AEPS, WD, STEP = 0.001, 0.9, 0.999, 1e-08, 0.01, 10
MESH = pl.DeviceIdType.MESH


def _cp(*sem):
    return pltpu.CompilerParams(dimension_semantics=sem, vmem_limit_bytes=VMEM_LIMIT)


def _row(ts, w, col=0):
    return pl.BlockSpec((ts, w), lambda i, _c=col: (i, _c))


def _fix(shape, single=False):
    nd = len(shape)
    if single:
        return pl.BlockSpec(shape, lambda i, _n=nd: (0,) * _n, pipeline_mode=pl.Buffered(1))
    return pl.BlockSpec(shape, lambda i, _n=nd: (0,) * _n)


def _silu_grad(x, s):
    return s * (1.0 + x * (1.0 - s))


def _softplus(x):
    return jnp.maximum(x, 0.0) + jnp.log1p(jnp.exp(-jnp.abs(x)))


_DIMS = {"NN": (((1,), (0,)), ((), ())), "TN": (((0,), (0,)), ((), ())), "NT": (((1,), (1,)), ((), ()))}


def _mm(pairs, mode, out_dtype, name, tm, tn, tk, precision=None, after=None):
    a0, b0 = pairs[0]
    m, k = a0.shape[::-1] if mode == "TN" else a0.shape
    n = b0.shape[0] if mode == "NT" else b0.shape[1]
    tm, tn, tk = min(tm, m), min(tn, n), min(tk, k)
    nk1 = k // tk
    npair = len(pairs)
    nk = 1 if nk1 == 1 else nk1 * npair
    assert m % tm == 0 and n % tn == 0 and k % tk == 0, (name, a0.shape, b0.shape)
    dims = _DIMS[mode]

    def kloc(kk, p):
        return jnp.clip(kk - p * nk1, 0, nk1 - 1)

    def aspec(p):
        if mode == "TN":
            return pl.BlockSpec((tk, tm), lambda i, j, kk, _p=p: (kloc(kk, _p), i))
        return pl.BlockSpec((tm, tk), lambda i, j, kk, _p=p: (i, kloc(kk, _p)))

    def bspec(p):
        extra = dict(pipeline_mode=pl.Buffered(1)) if (nk == 1 and tn == n) else {}
        if mode == "NT":
            return pl.BlockSpec((tn, tk), lambda i, j, kk, _p=p: (j, kloc(kk, _p)), **extra)
        return pl.BlockSpec((tk, tn), lambda i, j, kk, _p=p: (kloc(kk, _p), j), **extra)

    nin = 2 * npair + (after is not None)

    def body(*refs):
        ab, o_ref = refs[:2 * npair], refs[nin]
        if nk == 1:
            tot = lax.dot_general(ab[0][...], ab[1][...], dims, preferred_element_type=F32, precision=precision)
            for p in range(1, npair):
                tot = tot + lax.dot_general(ab[2 * p][...], ab[2 * p + 1][...], dims, preferred_element_type=F32,
                                            precision=precision)
            o_ref[...] = tot.astype(o_ref.dtype)
            return
        acc = refs[nin + 1]
        kk = pl.program_id(2)

        @pl.when(kk == 0)
        def _():
            acc[...] = jnp.zeros_like(acc)

        for p in range(npair):
            @pl.when((kk >= p * nk1) & (kk < (p + 1) * nk1))
            def _(p=p):
                acc[...] += lax.dot_general(ab[2 * p][...], ab[2 * p + 1][...], dims, preferred_element_type=F32,
                                            precision=precision)

        @pl.when(kk == nk - 1)
        def _():
            o_ref[...] = acc[...].astype(o_ref.dtype)

    in_specs = []
    for p in range(npair):
        in_specs += [aspec(p), bspec(p)]
    operands = [t for ab_ in pairs for t in ab_]
    if after is not None:
        in_specs.append(pl.BlockSpec(memory_space=pl.ANY))
        operands.append(after)
    return pl.pallas_call(
        body, name=name, grid=(m // tm, n // tn, nk), in_specs=in_specs,
        out_specs=pl.BlockSpec((tm, tn), lambda i, j, kk: (i, j)),
        out_shape=jax.ShapeDtypeStruct((m, n), out_dtype),
        scratch_shapes=[] if nk == 1 else [pltpu.VMEM((tm, tn), F32)],
        compiler_params=_cp("parallel", "parallel", "arbitrary"),
    )(*operands)


def _norm_fwd(x, g, sc, sh, mix=None, gt=None, *, name, ts=512):
    s, d = x.shape
    has = mix is not None

    def body(*refs):
        if has:
            x_ref, g_ref, sc_ref, sh_ref, mix_ref, gt_ref, h_ref, x1_ref = refs
        else:
            x_ref, g_ref, sc_ref, sh_ref, h_ref = refs
        xv = x_ref[...]
        if has:
            xv = xv + gt_ref[...] * mix_ref[...]
            x1_ref[...] = xv
        r = lax.rsqrt(jnp.mean(xv * xv, axis=-1, keepdims=True) + EPS)
        h_ref[...] = ((xv * r) * g_ref[...] * (1.0 + sc_ref[...]) + sh_ref[...]).astype(BF)

    ins = [x, g, sc, sh] + ([mix, gt] if has else [])
    in_specs = [_row(ts, d), _fix((1, d)), _fix((1, d)), _fix((1, d))] + ([_row(ts, d), _fix((1, d))] if has else [])
    out_shape = [jax.ShapeDtypeStruct((s, d), BF)] + ([jax.ShapeDtypeStruct((s, d), F32)] if has else [])
    out_specs = [_row(ts, d)] + ([_row(ts, d)] if has else [])
    out = pl.pallas_call(body, name=name, grid=(s // ts,), in_specs=in_specs, out_specs=out_specs,
                         out_shape=out_shape, compiler_params=_cp("parallel"))(*ins)
    return out if has else out[0]


def _norm_bwd(x, g, sc, sh, dh, dres, mix=None, gt=None, *, name, ts=256):
    s, d = x.shape
    has = mix is not None

    def body(*refs):
        if has:
            x_ref, g_ref, sc_ref, sh_ref, dh_ref, dres_ref, mix_ref, gt_ref, dx_ref, dmix_ref, part_ref = refs
        else:
            x_ref, g_ref, sc_ref, sh_ref, dh_ref, dres_ref, dx_ref, part_ref = refs
        i = pl.program_id(0)
        xv = x_ref[...]
        gv = g_ref[...]
        dhv = dh_ref[...].astype(F32)
        r = lax.rsqrt(jnp.mean(xv * xv, axis=-1, keepdims=True) + EPS)
        xh = xv * r
        dyn = dhv * (1.0 + sc_ref[...])
        gy = dyn * gv
        dx = dres_ref[...] + r * (gy - xh * jnp.mean(gy * xh, axis=-1, keepdims=True))
        dx_ref[...] = dx

        @pl.when(i == 0)
        def _():
            part_ref[...] = jnp.zeros_like(part_ref)

        part_ref[0:1, :] += jnp.sum(dhv * (xh * gv), axis=0, keepdims=True)
        part_ref[1:2, :] += jnp.sum(dhv, axis=0, keepdims=True)
        part_ref[2:3, :] += jnp.sum(dyn * xh, axis=0, keepdims=True)
        if has:
            part_ref[3:4, :] += jnp.sum(dx * mix_ref[...], axis=0, keepdims=True)
            dmix_ref[...] = (gt_ref[...] * dx).astype(BF)

    ins = [x, g, sc, sh, dh, dres] + ([mix, gt] if has else [])
    in_specs = [_row(ts, d), _fix((1, d)), _fix((1, d)), _fix((1, d)), _row(ts, d), _row(ts, d)]
    in_specs += [_row(ts, d), _fix((1, d))] if has else []
    out_shape = [jax.ShapeDtypeStruct((s, d), F32)] + ([jax.ShapeDtypeStruct((s, d), BF)] if has else [])
    out_shape += [jax.ShapeDtypeStruct((8, d), F32)]
    out_specs = [_row(ts, d)] + ([_row(ts, d)] if has else []) + [_fix((8, d))]
    return pl.pallas_call(body, name=name, grid=(s // ts,), in_specs=in_specs, out_specs=out_specs,
                          out_shape=out_shape, compiler_params=_cp("arbitrary"))(*ins)


def _final(x1, ffn, gt2, gf, tgt, *, ts=256):
    s, d = x1.shape

    def body(x1_ref, ffn_ref, gt_ref, g_ref, t_ref, dx_ref, dffn_ref, part_ref):
        i = pl.program_id(0)
        fv = ffn_ref[...]
        gv = g_ref[...]
        xv = x1_ref[...] + gt_ref[...] * fv
        r = lax.rsqrt(jnp.mean(xv * xv, axis=-1, keepdims=True) + EPS)
        xh = xv * r
        e = xh * gv - t_ref[...]
        dy = e * (1.0 / d)
        gy = dy * gv
        dx = r * (gy - xh * jnp.mean(gy * xh, axis=-1, keepdims=True))
        dx_ref[...] = dx
        dffn_ref[...] = (gt_ref[...] * dx).astype(BF)

        @pl.when(i == 0)
        def _():
            part_ref[...] = jnp.zeros_like(part_ref)

        part_ref[0:1, :] += jnp.sum(dy * xh, axis=0, keepdims=True)
        part_ref[1:2, :] += jnp.sum(dx * fv, axis=0, keepdims=True)
        part_ref[2:3, :] += jnp.sum(e * e, axis=0, keepdims=True) * (0.5 / d)

    return pl.pallas_call(
        body, name="final_loss", grid=(s // ts,),
        in_specs=[_row(ts, d), _row(ts, d), _fix((1, d)), _fix((1, d)), _row(ts, d)],
        out_specs=[_row(ts, d), _row(ts, d), _fix((8, d))],
        out_shape=[jax.ShapeDtypeStruct((s, d), F32), jax.ShapeDtypeStruct((s, d), BF),
                   jax.ShapeDtypeStruct((8, d), F32)],
        compiler_params=_cp("arbitrary"))(x1, ffn, gt2, gf, tgt)


def _gate_up(h, wg_t, wu_t, *, tm=2048, tn=512, sub=512):
    s, k = h.shape
    n = wg_t.shape[0]
    tm = min(tm, s)

    def body(h_ref, wg_ref, wu_ref, g_ref, u_ref, a_ref):
        hv = h_ref[...]
        for c0 in range(0, tn, sub):
            cs_ = slice(c0, c0 + sub)
            gv = lax.dot_general(hv, wg_ref[cs_, :], _DIMS["NT"], preferred_element_type=F32)
            uv = lax.dot_general(hv, wu_ref[cs_, :], _DIMS["NT"], preferred_element_type=F32)
            g_ref[:, cs_] = gv.astype(BF)
            u_ref[:, cs_] = uv.astype(BF)
            a_ref[:, cs_] = (gv * jax.nn.sigmoid(gv) * uv).astype(BF)

    wspec = pl.BlockSpec((tn, k), lambda i, j: (j, 0))
    ospec = pl.BlockSpec((tm, tn), lambda i, j: (i, j))
    return pl.pallas_call(body, name="gate_up", grid=(s // tm, n // tn),
                          in_specs=[pl.BlockSpec((tm, k), lambda i, j: (i, 0)), wspec, wspec],
                          out_specs=[ospec] * 3, out_shape=[jax.ShapeDtypeStruct((s, n), BF)] * 3,
                          compiler_params=_cp("parallel", "parallel"))(h, wg_t, wu_t)


def _gate_up_bwd(dffn, w_down, gate, up, *, tm=2048, tn=512, sub=512):
    s, k = dffn.shape
    n = w_down.shape[0]
    tm = min(tm, s)

    def body(d_ref, w_ref, g_ref, u_ref, dg_ref, du_ref):
        dv_ = d_ref[...]
        for c0 in range(0, tn, sub):
            cs_ = slice(c0, c0 + sub)
            dav = lax.dot_general(dv_, w_ref[cs_, :], _DIMS["NT"], preferred_element_type=F32)
            gv = g_ref[:, cs_].astype(F32)
            uv = u_ref[:, cs_].astype(F32)
            sg = jax.nn.sigmoid(gv)
            dg_ref[:, cs_] = (dav * uv * _silu_grad(gv, sg)).astype(BF)
            du_ref[:, cs_] = (dav * gv * sg).astype(BF)

    tile = pl.BlockSpec((tm, tn), lambda i, j: (i, j))
    return pl.pallas_call(body, name="gate_up_bwd", grid=(s // tm, n // tn),
                          in_specs=[pl.BlockSpec((tm, k), lambda i, j: (i, 0)),
                                    pl.BlockSpec((tn, k), lambda i, j: (j, 0)), tile, tile],
                          out_specs=[tile] * 2, out_shape=[jax.ShapeDtypeStruct((s, n), BF)] * 2,
                          compiler_params=_cp("parallel", "parallel"))(dffn, w_down, gate, up)


def _mixout_fwd(att, y, proj, g_att, g_ssd, *, ts=512):
    s = att.shape[0]

    def body(a_ref, y_ref, z_ref, ga_ref, gs_ref, o_ref):
        av = a_ref[...]
        ra = lax.rsqrt(jnp.mean(av * av, axis=-1, keepdims=True) + EPS)
        o_ref[:, 0:AW] = (av * ra * ga_ref[...]).astype(BF)
        zv = z_ref[...]
        yz = y_ref[...] * (zv * jax.nn.sigmoid(zv))
        rs = lax.rsqrt(jnp.mean(yz * yz, axis=-1, keepdims=True) + EPS)
        o_ref[:, AW:AW + SW] = (yz * rs * gs_ref[...]).astype(BF)

    return pl.pallas_call(body, name="mixout_fwd", grid=(s // ts,),
                          in_specs=[_row(ts, AW), _row(ts, SW), _row(ts, SW, 3), _fix((1, AW)), _fix((1, SW))],
                          out_specs=_row(ts, AW + SW), out_shape=jax.ShapeDtypeStruct((s, AW + SW), BF),
                          compiler_params=_cp("parallel"))(att, y, proj, g_att, g_ssd)


def _mixout_bwd(dcat, att, y, proj, g_att, g_ssd, *, ts=512):
    s = att.shape[0]

    def body(dc_ref, a_ref, y_ref, z_ref, ga_ref, gs_ref, da_ref, dy_ref, dz_ref, part_ref):
        i = pl.program_id(0)
        av = a_ref[...]
        dca = dc_ref[:, 0:AW].astype(F32)
        ra = lax.rsqrt(jnp.mean(av * av, axis=-1, keepdims=True) + EPS)
        ah = av * ra
        gy = dca * ga_ref[...]
        da_ref[...] = ra * (gy - ah * jnp.mean(gy * ah, axis=-1, keepdims=True))
        zv = z_ref[...]
        yv = y_ref[...]
        sg = jax.nn.sigmoid(zv)
        sz = zv * sg
        yz = yv * sz
        dcs = dc_ref[:, AW:AW + SW].astype(F32)
        rs = lax.rsqrt(jnp.mean(yz * yz, axis=-1, keepdims=True) + EPS)
        yh = yz * rs
        gys = dcs * gs_ref[...]
        dyz = rs * (gys - yh * jnp.mean(gys * yh, axis=-1, keepdims=True))
        dy_ref[...] = dyz * sz
        dz_ref[...] = (dyz * yv * _silu_grad(zv, sg)).astype(BF)

        @pl.when(i == 0)
        def _():
            part_ref[...] = jnp.zeros_like(part_ref)

        part_ref[0:1, :] += jnp.sum(dca * ah, axis=0, keepdims=True)
        part_ref[1:2, :] += jnp.sum(dcs * yh, axis=0, keepdims=True)

    return pl.pallas_call(
        body, name="mixout_bwd", grid=(s // ts,),
        in_specs=[_row(ts, AW + SW), _row(ts, AW), _row(ts, SW), _row(ts, SW, 3), _fix((1, AW)), _fix((1, SW))],
        out_specs=[_row(ts, AW), _row(ts, SW), _row(ts, SW), _fix((8, AW))],
        out_shape=[jax.ShapeDtypeStruct((s, AW), F32), jax.ShapeDtypeStruct((s, SW), F32),
                   jax.ShapeDtypeStruct((s, SW), BF), jax.ShapeDtypeStruct((8, AW), F32)],
        compiler_params=_cp("arbitrary"))(dcat, att, y, proj, g_att, g_ssd)


SCALE = HD ** -0.5
QB = 256
NQC = QB // CH
WIN = BAND - CH + QB
NKB = WIN // QB


def _bias_window(bias):
    rows = [jnp.pad(bias, ((0, 0), (0, 0), (CH * a, WIN - BAND - CH * a)), constant_values=-jnp.inf)
            for a in range(NQC)]
    return jnp.concatenate(rows, axis=1)


def _win_specs(width, nb):
    def mk(col, back):
        return pl.BlockSpec((QB, width), lambda i, _c=col, _k=back: (jnp.maximum(jnp.minimum(i, nb - 1) - _k, 0), _c))
    return [mk(col, back) for col in (1, 2) for back in range(NKB - 1, -1, -1)]


def _attn2_fwd(qkv, bias_w):
    s = qkv.shape[0]
    nb = s // QB

    def body(q_ref, *refs):
        krefs, vrefs = refs[:NKB], refs[NKB:2 * NKB]
        b_ref, o_ref, lse_ref, kbuf, vbuf = refs[2 * NKB:]
        i = pl.program_id(0)
        for j in range(NKB):
            kbuf[QB * j:QB * j + QB, :] = krefs[j][...].astype(BF)
            vbuf[QB * j:QB * j + QB, :] = vrefs[j][...].astype(BF)
        valid = (i * QB - (NKB - 1) * QB + lax.broadcasted_iota(jnp.int32, (1, WIN), 1)) >= 0
        lane = lax.broadcasted_iota(jnp.int32, (QB, LANE), 1)

        lo = lane < HD

        def heads(masked):
            lse = jnp.zeros((QB, LANE), F32)
            for p in range(NH // 2):
                sl = slice(2 * HD * p, 2 * HD * p + 2 * HD)
                qp = q_ref[:, sl] * SCALE
                qbd = jnp.concatenate([jnp.where(lo, qp, 0.0), jnp.where(lo, 0.0, qp)], axis=0).astype(BF)
                sc = lax.dot_general(qbd, kbuf[:, sl], _DIMS["NT"], preferred_element_type=F32) + b_ref[p]
                if masked:
                    sc = jnp.where(valid, sc, -jnp.inf)
                mx = jnp.max(sc, axis=-1, keepdims=True)
                e = jnp.exp(sc - mx)
                tot = jnp.sum(e, axis=-1, keepdims=True)
                o2 = jnp.dot(e.astype(BF), vbuf[:, sl], preferred_element_type=F32) * (1.0 / tot)
                o_ref[:, sl] = jnp.where(lo, o2[0:QB], o2[QB:2 * QB])
                l2 = mx + jnp.log(tot)
                lse = jnp.where(lane == 2 * p, l2[0:QB], jnp.where(lane == 2 * p + 1, l2[QB:2 * QB], lse))
            lse_ref[...] = lse

        pl.when(i < NKB - 1)(functools.partial(heads, True))
        pl.when(i >= NKB - 1)(functools.partial(heads, False))

    return pl.pallas_call(
        body, name="attn_fwd", grid=(nb,),
        in_specs=[_row(QB, AW)] + _win_specs(AW, nb) + [_fix((NH // 2, 2 * QB, WIN), single=True)],
        out_specs=[_row(QB, AW), _row(QB, LANE)],
        out_shape=[jax.ShapeDtypeStruct((s, AW), F32), jax.ShapeDtypeStruct((s, LANE), F32)],
        scratch_shapes=[pltpu.VMEM((WIN, AW), BF), pltpu.VMEM((WIN, AW), BF)],
        compiler_params=_cp("parallel"))(*[qkv] * (1 + 2 * NKB), bias_w.reshape(NH // 2, 2 * QB, WIN))


def _attn2_bwd(qkv, datt, lse_t, bias_wt):
    s = qkv.shape[0]
    nb = s // QB
    cur = lambda i: jnp.minimum(i, nb - 1)
    late = lambda i: jnp.maximum(i - (NKB - 1), 0)

    def body(q_ref, *refs):
        krefs, vrefs = refs[:NKB], refs[NKB:2 * NKB]
        do_ref, lse_ref, b_ref, dq_ref, dk_ref, dv_ref, db_ref, kbuf, vbuf, dkacc, dvacc, dsbuf = refs[2 * NKB:]
        i = pl.program_id(0)

        @pl.when(i == 0)
        def _():
            dkacc[...] = jnp.zeros_like(dkacc)
            dvacc[...] = jnp.zeros_like(dvacc)
            db_ref[...] = jnp.zeros_like(db_ref)

        def heads(masked):
            for j in range(NKB):
                kbuf[QB * j:QB * j + QB, :] = krefs[j][...].astype(BF)
                vbuf[QB * j:QB * j + QB, :] = vrefs[j][...].astype(BF)
            valid = (i * QB - (NKB - 1) * QB + lax.broadcasted_iota(jnp.int32, (WIN, 1), 0)) >= 0
            lo = lax.broadcasted_iota(jnp.int32, (QB, LANE), 1) < HD

            def both(ref, sl):
                t = ref[:, sl]
                return jnp.concatenate([jnp.where(lo, t, 0.0), jnp.where(lo, 0.0, t)], axis=0)

            for p in range(NH // 2):
                sl = slice(2 * HD * p, 2 * HD * p + 2 * HD)
                qbd = (both(q_ref, sl) * SCALE).astype(BF)
                dobd = both(do_ref, sl).astype(BF)
                k = kbuf[:, sl]
                st = lax.dot_general(k, qbd, _DIMS["NT"], preferred_element_type=F32) + b_ref[p]
                if masked:
                    st = jnp.where(valid, st, -jnp.inf)
                lrow = jnp.concatenate([lse_ref[2 * p:2 * p + 1, :], lse_ref[2 * p + 1:2 * p + 2, :]], axis=1)
                pt = jnp.exp(st - lrow)
                dvacc[:, sl] += jnp.dot(pt.astype(BF), dobd, preferred_element_type=F32)
                dpt = lax.dot_general(vbuf[:, sl], dobd, _DIMS["NT"], preferred_element_type=F32)
                dst = pt * (dpt - jnp.sum(dpt * pt, axis=0, keepdims=True))
                dsbuf[...] = dst
                for hh in range(2):
                    fold = dsbuf[0:BAND, QB * hh:QB * hh + CH]
                    for a in range(1, NQC):
                        fold = fold + dsbuf[CH * a:CH * a + BAND, QB * hh + CH * a:QB * hh + CH * a + CH]
                    db_ref[2 * p + hh] += fold
                dsb = dst.astype(BF)
                dkacc[:, sl] += jnp.dot(dsb, qbd, preferred_element_type=F32)
                dq2 = lax.dot_general(dsb, k, _DIMS["TN"], preferred_element_type=F32) * SCALE
                dq_ref[:, sl] = jnp.where(lo, dq2[0:QB], dq2[QB:2 * QB]).astype(BF)

        pl.when(i < nb)(functools.partial(heads, True))

        dk_ref[...] = dkacc[0:QB, :].astype(BF)
        dv_ref[...] = dvacc[0:QB, :].astype(BF)
        for acc in (dkacc, dvacc):
            for j in range(NKB - 1):
                acc[QB * j:QB * j + QB, :] = acc[QB * j + QB:QB * j + 2 * QB, :]
            acc[WIN - QB:WIN, :] = jnp.zeros((QB, AW), F32)

    qspec = pl.BlockSpec((QB, AW), lambda i: (cur(i), 0))
    lspec = pl.BlockSpec((QB, AW), lambda i: (late(i), 0))
    return pl.pallas_call(
        body, name="attn_bwd", grid=(nb + NKB - 1,),
        in_specs=[qspec] + _win_specs(AW, nb) + [qspec, pl.BlockSpec((NH, QB), lambda i: (0, cur(i))),
                                                 _fix((NH // 2, WIN, 2 * QB), single=True)],
        out_specs=[qspec, lspec, lspec, _fix((NH, BAND, CH))],
        out_shape=[jax.ShapeDtypeStruct((s, AW), BF)] * 3 + [jax.ShapeDtypeStruct((NH, BAND, CH), F32)],
        scratch_shapes=[pltpu.VMEM((WIN, AW), BF), pltpu.VMEM((WIN, AW), BF),
                        pltpu.VMEM((WIN, AW), F32), pltpu.VMEM((WIN, AW), F32), pltpu.VMEM((WIN, 2 * QB), F32)],
        compiler_params=_cp("arbitrary"))(*[qkv] * (1 + 2 * NKB), datt, lse_t, bias_wt)


TB = 512
SCH = 512
NCB = TB // SCH
GW = 8 * HD


def _tri(lower):
    r = lax.broadcasted_iota(jnp.int32, (SCH, SCH), 0)
    c = lax.broadcasted_iota(jnp.int32, (SCH, SCH), 1)
    return r >= c if lower else r <= c


def _ssd_fwd(proj, conv_w, conv_b, dt_bias, a_log, d_skip):
    s = proj.shape[0]
    nb = s // TB

    def body(xs_ref, bc_ref, dt_ref, cw_ref, cb_ref, dtb_ref, al_ref, dsk_ref, y_ref, pre_ref, prev_ref,
             xpad, ubuf, dtbuf, csb, cstb, hst, xdec, yoffb, drow):
        i = pl.program_id(0)

        @pl.when(i == 0)
        def _():
            xpad[0:8, :] = jnp.zeros((8, XBC), F32)
            hst[...] = jnp.zeros_like(hst)

        xpad[8:8 + TB, 0:SW] = xs_ref[...]
        xpad[8:8 + TB, SW:XBC] = bc_ref[...]
        pre = cb_ref[...]
        for kk in range(4):
            pre = pre + cw_ref[kk:kk + 1, :] * xpad[5 + kk:5 + kk + TB, :]
        pre_ref[...] = pre
        ubuf[...] = pre * jax.nn.sigmoid(pre)
        xpad[0:8, :] = xpad[TB:TB + 8, :]
        dtbuf[...] = _softplus(dt_ref[...] + dtb_ref[...])
        a = -jnp.exp(al_ref[...])
        tri = _tri(True).astype(F32)
        causal = _tri(True)

        def chunk(c, carry):
            r0 = pl.multiple_of(c * SCH, SCH)
            rows = pl.ds(r0, SCH)
            cs = jnp.dot(tri, dtbuf[rows, :] * a, precision=HI, preferred_element_type=F32)
            csb[...] = cs
            cstb[...] = cs.T
            for g in range(NG):
                bg = ubuf[rows, SW + NS * g:SW + NS * g + NS]
                cg = ubuf[rows, SW + NG * NS + NS * g:SW + NG * NS + NS * g + NS].astype(BF)
                cb = lax.dot_general(cg, bg.astype(BF), (((1,), (1,)), ((), ())), preferred_element_type=F32)
                hg = hst[g]
                prev_ref[c, g] = hg
                yoffb[...] = jnp.dot(cg, hg.astype(BF), preferred_element_type=F32)
                for r in range(8):
                    h = 8 * g + r
                    sl = slice(HD * h, HD * h + HD)
                    rs = slice(HD * r, HD * r + HD)
                    cs_h = csb[:, h:h + 1]
                    cl = csb[SCH - 1:SCH, h:h + 1]
                    seg = jnp.exp(jnp.where(causal, cs_h - cstb[h:h + 1, :], -jnp.inf))
                    xh = ubuf[rows, sl]
                    xdt = xh * dtbuf[rows, h:h + 1]
                    yd = jnp.dot((cb * seg).astype(BF), xdt.astype(BF), preferred_element_type=F32)
                    y_ref[rows, sl] = yd + jnp.exp(cs_h) * yoffb[:, rs] + xh * dsk_ref[:, h:h + 1]
                    xdec[:, rs] = xdt * jnp.exp(cl - cs_h)
                    drow[:, rs] = jnp.broadcast_to(jnp.exp(cl), (1, HD))
                st = jnp.dot(bg.T.astype(BF), xdec[...].astype(BF), preferred_element_type=F32)
                hst[g] = hg * drow[...] + st
            return carry

        lax.fori_loop(0, NCB, chunk, 0)

    return pl.pallas_call(
        body, name="ssd_fwd", grid=(nb,),
        in_specs=[_row(TB, SW, 4), _row(TB, 512, 10), _row(TB, LANE, 44), _fix((4, XBC)), _fix((1, XBC)),
                  _fix((1, LANE)), _fix((1, LANE)), _fix((1, LANE))],
        out_specs=[_row(TB, SW), _row(TB, XBC), pl.BlockSpec((NCB, NG, NS, GW), lambda i: (i, 0, 0, 0))],
        out_shape=[jax.ShapeDtypeStruct((s, SW), F32), jax.ShapeDtypeStruct((s, XBC), F32),
                   jax.ShapeDtypeStruct((s // SCH, NG, NS, GW), F32)],
        scratch_shapes=[pltpu.VMEM((TB + 8, XBC), F32), pltpu.VMEM((TB, XBC), F32), pltpu.VMEM((TB, LANE), F32),
                        pltpu.VMEM((SCH, LANE), F32), pltpu.VMEM((LANE, SCH), F32), pltpu.VMEM((NG, NS, GW), F32),
                        pltpu.VMEM((SCH, GW), F32), pltpu.VMEM((SCH, GW), F32), pltpu.VMEM((1, GW), F32)],
        compiler_params=_cp("arbitrary"))(proj, proj, proj, conv_w, conv_b, dt_bias, a_log, d_skip)


def _ssd_bwd(dy, pre, proj, prev, conv_w, dt_bias, a_log, d_skip):
    s = dy.shape[0]
    nb = s // TB
    rev = lambda i: nb - 1 - i
    halo = lambda i: jnp.maximum((nb - 1 - i) * (TB // 8) - 1, 0)

    def body(dy_ref, pre_ref, dt_ref, prev_ref, xs_ref, bc_ref, xsh_ref, bch_ref, cw_ref, dtb_ref, al_ref, dsk_ref,
             dx_ref, ddt_ref, pw_ref, ph_ref,
             ubuf, dtbuf, dpad, xpad, csb, cstb, dhs, yoffb, ebuf, xdecb, wbuf, drow, dub):
        i = pl.program_id(0)

        @pl.when(i == 0)
        def _():
            dhs[...] = jnp.zeros_like(dhs)
            dpad[TB:TB + 8, :] = jnp.zeros((8, XBC), F32)
            pw_ref[...] = jnp.zeros_like(pw_ref)
            ph_ref[...] = jnp.zeros_like(ph_ref)

        pre = pre_ref[...]
        ubuf[...] = pre * jax.nn.sigmoid(pre)
        dtbuf[...] = _softplus(dt_ref[...] + dtb_ref[...])
        a = -jnp.exp(al_ref[...])
        tri = _tri(True).astype(F32)
        trit = _tri(False).astype(F32)
        causal = _tri(True)
        rowid = lax.broadcasted_iota(jnp.int32, (LANE, SCH), 0)
        lane = lax.broadcasted_iota(jnp.int32, (SCH, LANE), 1)
        lane1 = lax.broadcasted_iota(jnp.int32, (1, LANE), 1)
        lastrow = lax.broadcasted_iota(jnp.int32, (SCH, LANE), 0) == SCH - 1

        def chunk(cc, carry):
            c = NCB - 1 - cc
            r0 = pl.multiple_of(c * SCH, SCH)
            rows = pl.ds(r0, SCH)
            dtc = dtbuf[rows, :]
            cs = jnp.dot(tri, dtc * a, precision=HI, preferred_element_type=F32)
            csb[...] = cs
            cstb[...] = cs.T
            dcs = jnp.zeros((SCH, LANE), F32)
            dcst = jnp.zeros((LANE, SCH), F32)
            xr = jnp.zeros((SCH, LANE), F32)
            dlast = jnp.zeros((1, LANE), F32)
            dsk = jnp.zeros((1, LANE), F32)
            for g in range(NG):
                bg = ubuf[rows, SW + NS * g:SW + NS * g + NS]
                cg = ubuf[rows, SW + NG * NS + NS * g:SW + NG * NS + NS * g + NS]
                bb = bg.astype(BF)
                cbf = cg.astype(BF)
                cb = lax.dot_general(cbf, bb, (((1,), (1,)), ((), ())), preferred_element_type=F32)
                hp = prev_ref[c, g]
                hpb = hp.astype(BF)
                dhg = dhs[g]
                dhb = dhg.astype(BF)
                yoffb[...] = jnp.dot(cbf, hpb, preferred_element_type=F32)
                for r in range(8):
                    h = 8 * g + r
                    sl = slice(HD * h, HD * h + HD)
                    rs = slice(HD * r, HD * r + HD)
                    cs_h = csb[:, h:h + 1]
                    cl = csb[SCH - 1:SCH, h:h + 1]
                    dyh = dy_ref[rows, sl]
                    ecs = jnp.exp(cs_h)
                    ebuf[:, rs] = dyh * ecs
                    xdecb[:, rs] = ubuf[rows, sl] * dtbuf[rows, h:h + 1] * jnp.exp(cl - cs_h)
                    t1 = jnp.sum(dyh * ecs * yoffb[:, rs], axis=1, keepdims=True)
                    dcs = dcs + jnp.where(lane == h, t1, 0.0)
                    drow[:, rs] = jnp.broadcast_to(jnp.exp(cl), (1, HD))
                eb = ebuf[...].astype(BF)
                dc = lax.dot_general(eb, hpb, (((1,), (1,)), ((), ())), preferred_element_type=F32)
                dprev = jnp.dot(cg.T.astype(BF), eb, preferred_element_type=F32)
                wbuf[...] = jnp.dot(bb, dhb, preferred_element_type=F32)
                db = lax.dot_general(xdecb[...].astype(BF), dhb, (((1,), (1,)), ((), ())), preferred_element_type=F32)
                dcbs = jnp.zeros((SCH, SCH), F32)
                for r in range(8):
                    h = 8 * g + r
                    sl = slice(HD * h, HD * h + HD)
                    rs = slice(HD * r, HD * r + HD)
                    cs_h = csb[:, h:h + 1]
                    cst_h = cstb[h:h + 1, :]
                    cl = csb[SCH - 1:SCH, h:h + 1]
                    seg = jnp.exp(jnp.where(causal, cs_h - cst_h, -jnp.inf))
                    dyh = dy_ref[rows, sl]
                    xh = ubuf[rows, sl]
                    dt_h = dtbuf[rows, h:h + 1]
                    xdt = xh * dt_h
                    dyb = dyh.astype(BF)
                    dm = lax.dot_general(dyb, xdt.astype(BF), _DIMS["NT"], preferred_element_type=F32)
                    mm_ = cb * seg
                    dxdt = lax.dot_general(mm_.astype(BF), dyb, _DIMS["TN"], preferred_element_type=F32)
                    gg = dm * mm_
                    dcs_h = jnp.sum(gg, axis=1, keepdims=True)
                    dcst = dcst + jnp.where(rowid == h, jnp.sum(gg, axis=0, keepdims=True), 0.0)
                    dcbs = dcbs + dm * seg
                    dec = jnp.exp(cl - cs_h)
                    w_h = wbuf[:, rs]
                    dxdt = dxdt + dec * w_h
                    t = jnp.sum(w_h * xdt, axis=1, keepdims=True) * dec
                    dcs_h = dcs_h - t
                    dch = jnp.sum(jnp.sum(dhs[g, :, rs] * prev_ref[c, g, :, rs], axis=1, keepdims=True),
                                  axis=0, keepdims=True)
                    dl = jnp.sum(t, axis=0, keepdims=True) + dch * jnp.exp(cl)
                    dcs = dcs + jnp.where(lane == h, dcs_h, 0.0)
                    dlast = dlast + jnp.where(lane1 == h, dl, 0.0)
                    xr = xr + jnp.where(lane == h, jnp.sum(dxdt * xh, axis=1, keepdims=True), 0.0)
                    dskh = jnp.sum(jnp.sum(dyh * xh, axis=1, keepdims=True), axis=0, keepdims=True)
                    dsk = dsk + jnp.where(lane1 == h, dskh, 0.0)
                    dub[:, sl] = dyh * dsk_ref[:, h:h + 1] + dxdt * dt_h
                dc = dc + jnp.dot(dcbs.astype(BF), bb, preferred_element_type=F32)
                db = db + lax.dot_general(dcbs.astype(BF), cbf, _DIMS["TN"], preferred_element_type=F32)
                dub[:, SW + NS * g:SW + NS * g + NS] = db
                dub[:, SW + NG * NS + NS * g:SW + NG * NS + NS * g + NS] = dc
                dhs[g] = dprev + drow[...] * dhg
            dcs = dcs - dcst.T + jnp.where(lastrow, dlast, 0.0)
            dadt = jnp.dot(trit, dcs, precision=HI, preferred_element_type=F32)
            ddt = dadt * a + xr
            ddtr = ddt * jax.nn.sigmoid(dt_ref[rows, :] + dtb_ref[...])
            ddt_ref[rows, :] = ddtr.astype(BF)
            ph_ref[0:1, :] += jnp.sum(ddtr, axis=0, keepdims=True)
            ph_ref[1:2, :] += jnp.sum(dadt * dtc, axis=0, keepdims=True) * a
            ph_ref[2:3, :] += dsk
            pc = pre_ref[rows, :]
            sg = jax.nn.sigmoid(pc)
            dpad[rows, :] = dub[...] * _silu_grad(pc, sg)
            return carry

        lax.fori_loop(0, NCB, chunk, 0)

        dxb = cw_ref[0:1, :] * dpad[3:3 + TB, :]
        for kk in range(1, 4):
            dxb = dxb + cw_ref[kk:kk + 1, :] * dpad[3 - kk:3 - kk + TB, :]
        dx_ref[...] = dxb.astype(BF)
        keep = jnp.where(i < nb - 1, 1.0, 0.0)
        xpad[0:8, 0:SW] = xsh_ref[...] * keep
        xpad[0:8, SW:XBC] = bch_ref[...] * keep
        xpad[8:8 + TB, 0:SW] = xs_ref[...]
        xpad[8:8 + TB, SW:XBC] = bc_ref[...]
        dp = dpad[0:TB, :]
        for kk in range(4):
            pw_ref[kk:kk + 1, :] += jnp.sum(dp * xpad[5 + kk:5 + kk + TB, :], axis=0, keepdims=True)
        pw_ref[4:5, :] += jnp.sum(dp, axis=0, keepdims=True)
        dpad[TB:TB + 8, :] = dpad[0:8, :]

    rblk = lambda w, col: pl.BlockSpec((TB, w), lambda i, _c=col: (rev(i), _c))
    return pl.pallas_call(
        body, name="ssd_bwd", grid=(nb,),
        in_specs=[rblk(SW, 0), rblk(XBC, 0), rblk(LANE, 44),
                  pl.BlockSpec((NCB, NG, NS, GW), lambda i: (rev(i), 0, 0, 0)),
                  rblk(SW, 4), rblk(512, 10),
                  pl.BlockSpec((8, SW), lambda i: (halo(i), 4)), pl.BlockSpec((8, 512), lambda i: (halo(i), 10)),
                  _fix((4, XBC)), _fix((1, LANE)), _fix((1, LANE)), _fix((1, LANE))],
        out_specs=[rblk(XBC, 0), rblk(LANE, 0), _fix((8, XBC)), _fix((8, LANE))],
        out_shape=[jax.ShapeDtypeStruct((s, XBC), BF), jax.ShapeDtypeStruct((s, LANE), BF),
                   jax.ShapeDtypeStruct((8, XBC), F32), jax.ShapeDtypeStruct((8, LANE), F32)],
        scratch_shapes=[pltpu.VMEM((TB, XBC), F32), pltpu.VMEM((TB, LANE), F32), pltpu.VMEM((TB + 8, XBC), F32),
                        pltpu.VMEM((TB + 8, XBC), F32), pltpu.VMEM((SCH, LANE), F32), pltpu.VMEM((LANE, SCH), F32),
                        pltpu.VMEM((NG, NS, GW), F32), pltpu.VMEM((SCH, GW), F32), pltpu.VMEM((SCH, GW), F32),
                        pltpu.VMEM((SCH, GW), F32), pltpu.VMEM((SCH, GW), F32), pltpu.VMEM((1, GW), F32),
                        pltpu.VMEM((SCH, XBC), F32)],
        compiler_params=_cp("arbitrary"))(dy, pre, proj, prev, proj, proj, proj, proj, conv_w, dt_bias, a_log, d_skip)


def _rel_index():
    q = jnp.arange(CH)[:, None] + (BAND - CH)
    k = jnp.arange(BAND)[None, :]
    return jnp.clip(q - k, -(CH - 1), REL_CLIP) + (CH - 1)


def _local_step(x, tgt, mods, first_weights, late_weights, grads_ready,
                g_mix, rel_bias, conv_w, conv_b, dt_bias, a_log, d_skip, g_att, g_ssd, g_ffn, g_final):
    sh1, sc1, gt1, sh2, sc2, gt2 = mods
    pad16 = lambda v: jnp.pad(v, ((0, 0), (0, LANE - NH)))
    dtb_p, al_p, dsk_p = pad16(dt_bias), pad16(a_log), pad16(d_skip)
    ridx = _rel_index()
    ext = jnp.concatenate([jnp.broadcast_to(rel_bias[:, NREL - 1:], (NH, BAND + CH - 1 - NREL)),
                           rel_bias[:, ::-1]], axis=1)
    skew = jnp.tile(jnp.pad(ext, ((0, 0), (0, 1))), (1, CH + 1))[:, :CH * (BAND + CH + 1)]
    bias = skew.reshape(NH, CH, BAND + CH + 1)[:, ::-1, :BAND]
    bias_w = _bias_window(bias)
    bias_wt = jnp.transpose(bias_w.reshape(NH // 2, 2, QB, WIN), (0, 3, 1, 2)).reshape(NH // 2, WIN, 2 * QB)

    h1 = _norm_fwd(x, g_mix, sc1, sh1, name="norm1_fwd")
    onehot = (ridx.T.reshape(BAND * CH, 1) == jnp.arange(NREL)[None, :]).astype(F32)
    w_in_t = first_weights([h1, bias_w, bias_wt, onehot])
    proj = _mm([(h1, w_in_t)], "NT", F32, "in_proj", 1024, 1920, 2048)
    att, lse = _attn2_fwd(proj, bias_w)
    y, pre, prev = _ssd_fwd(proj, conv_w, conv_b, dtb_p, al_p, dsk_p)
    cat = _mixout_fwd(att, y, proj, g_att, g_ssd)
    (w_out,) = late_weights("out", cat)
    mix = _mm([(cat, w_out)], "NN", F32, "out_proj", 1024, 2048, 2048)
    h2, x1 = _norm_fwd(x, g_ffn, sc2, sh2, mix, gt1, name="norm2_fwd")
    w_gate_t, w_up_t = late_weights("gate_up", h2)
    gate, up, act = _gate_up(h2, w_gate_t, w_up_t)
    (w_down,) = late_weights("down", act)
    ffn = _mm([(act, w_down)], "NN", F32, "down_proj", 512, 2048, FF)
    dx2, dffn, pf = _final(x1, ffn, gt2, g_final, tgt)

    gw_down = _mm([(act, dffn)], "TN", BF, "gw_down", 1408, 2048, 1024)
    dgate, dup = _gate_up_bwd(dffn, w_down, gate, up)
    dh2 = _mm([(dgate, w_gate_t), (dup, w_up_t)], "NN", BF, "d_h2", 512, 512, FF)
    gw_gate_t = _mm([(dgate, h2)], "TN", BF, "gw_gate", 1408, 2048, 1024)
    gw_up_t = _mm([(dup, h2)], "TN", BF, "gw_up", 1408, 2048, 1024)
    tok = grads_ready("ffn", gw_gate_t, gw_up_t, gw_down)
    dx1, dmix, p2 = _norm_bwd(x1, g_ffn + tok[0, 0], sc2, sh2, dh2, dx2, mix, gt1, name="norm2_bwd")
    gw_out = _mm([(cat, dmix)], "TN", BF, "gw_out", 1024, 2048, 1024)
    tok = grads_ready("out", gw_out)
    dcat = _mm([(dmix, w_out)], "NT", BF, "d_cat", 1024, 2048, 2048, after=tok)
    datt, dy, dz, pm = _mixout_bwd(dcat, att, y, proj, g_att, g_ssd)
    dq, dk, dv, dbias_t = _attn2_bwd(proj, datt, lse[:, :NH].T, bias_wt)
    dxbc, ddt, pw, ph = _ssd_bwd(dy, pre, proj, prev, conv_w, dtb_p, al_p, dsk_p)
    dproj = jnp.concatenate([dq, dk, dv, dz, dxbc, ddt], axis=1)
    gw_in_t = _mm([(dproj, h1)], "TN", BF, "gw_in", 1152, 2048, 1024)
    tok = grads_ready("in", gw_in_t)
    dh1 = _mm([(dproj, w_in_t)], "NN", BF, "d_h1", 512, 2048, INP, after=tok)
    gx, p1 = _norm_bwd(x, g_mix, sc1, sh1, dh1, dx1, name="norm1_bwd")
    g_rel =_mm([(dbias_t.reshape(NH, BAND * CH), onehot)], "NN", F32, "g_rel_bias", NH, NREL, 4608, precision=HI)

    small = dict(
        dmods=jnp.concatenate([p1[1], p1[0], p2[3], p2[1], p2[0], pf[1]]),
        g_mix=p1[2], conv_b=pw[4], dt_bias=ph[0], a_log=ph[1], d_skip=ph[2], g_att=pm[0], g_ssd=pm[1],
        g_ffn=p2[2], g_final=pf[0], rel_bias=g_rel.reshape(-1), conv_w=pw[0:4].reshape(-1), loss=pf[2])
    return gx, small


def _exchange(srcs, gather, name):
    n = len(srcs)
    outs = [jax.ShapeDtypeStruct((NDEV,) + (s.shape if g else s.shape[1:]), s.dtype) for s, g in zip(srcs, gather)]
    outs.append(jax.ShapeDtypeStruct((8, LANE), F32))

    def body(*refs):
        src, dst, token = refs[:n], refs[n:2 * n], refs[2 * n]
        ssem, rsem, lsem = refs[2 * n + 1:]
        token[...] = jnp.zeros_like(token)
        x, y, c = lax.axis_index("x"), lax.axis_index("y"), lax.axis_index("c")
        me = 4 * x + 2 * y + c
        local = []
        for a in range(n):
            own = src[a] if gather[a] else src[a].at[me]
            cp = pltpu.make_async_copy(own, dst[a].at[me], lsem.at[a])
            cp.start()
            local.append(cp)
        remote = []
        for k in range(1, NDEV):
            px = 1 - x if k & 4 else x
            py = 1 - y if k & 2 else y
            pc = 1 - c if k & 1 else c
            pid = 4 * px + 2 * py + pc
            for a in range(n):
                s_ref = src[a] if gather[a] else src[a].at[pid]
                cp = pltpu.make_async_remote_copy(
                    src_ref=s_ref, dst_ref=dst[a].at[me], send_sem=ssem.at[a * (NDEV - 1) + k - 1],
                    recv_sem=rsem.at[a * (NDEV - 1) + k - 1], device_id=(px, py, pc), device_id_type=MESH)
                cp.start()
                remote.append(cp)
        for cp in remote:
            cp.wait()
        for cp in local:
            cp.wait()

    anyspec = pl.BlockSpec(memory_space=pl.ANY)
    return pl.pallas_call(
        body, name=name, out_shape=outs, in_specs=[anyspec] * n,
        out_specs=[anyspec] * n + [pl.BlockSpec(memory_space=pltpu.VMEM)],
        scratch_shapes=[pltpu.SemaphoreType.DMA((n * (NDEV - 1),)), pltpu.SemaphoreType.DMA((n * (NDEV - 1),)),
                        pltpu.SemaphoreType.DMA((n,))],
    )(*srcs)


_HBM = pl.BlockSpec(memory_space=pltpu.HBM)
_SEM = pl.BlockSpec(memory_space=pltpu.SEMAPHORE)
_EFFECT = pltpu.SideEffectType.DATAFLOW_SIDE_EFFECTING


def _peer_copies(src, land, ssem, rsem, gather):
    x, y, c = lax.axis_index("x"), lax.axis_index("y"), lax.axis_index("c")
    me = 4 * x + 2 * y + c
    copies = []
    for a in range(len(src)):
        for k in range(1, NDEV):
            px = 1 - x if k & 4 else x
            py = 1 - y if k & 2 else y
            pc = 1 - c if k & 1 else c
            s_ref = src[a] if gather[a] else src[a].at[4 * px + 2 * py + pc]
            idx = a * (NDEV - 1) + k - 1
            copies.append(pltpu.make_async_remote_copy(
                src_ref=s_ref, dst_ref=land[a].at[me], send_sem=ssem.at[idx], recv_sem=rsem.at[idx],
                device_id=(px, py, pc), device_id_type=MESH))
    return copies


def _xstart(srcs, gather, name):
    n = len(srcs)
    me = 4 * lax.axis_index("x") + 2 * lax.axis_index("y") + lax.axis_index("c")
    lands = []
    for s_, g_ in zip(srcs, gather):
        own = s_ if g_ else lax.dynamic_index_in_dim(s_, me, 0, keepdims=False)
        empty = lax.empty((NDEV,) + own.shape, own.dtype)
        lands.append(lax.dynamic_update_slice(empty, own[None], (me,) + (0,) * own.ndim))
    nsem = n * (NDEV - 1)

    def body(*refs):
        src, land, ssem, rsem, token = refs[:n], refs[n:2 * n], refs[2 * n], refs[2 * n + 1], refs[-1]
        for cp in _peer_copies(src, land, ssem, rsem, gather):
            cp.start()
        token[...] = jnp.zeros_like(token)

    ops = [pltpu.with_memory_space_constraint(a_, pltpu.HBM) for a_ in list(srcs) + lands]
    return pl.pallas_call(
        body, name=name,
        out_shape=(pltpu.SemaphoreType.DMA((nsem,)), pltpu.SemaphoreType.DMA((nsem,)),
                   *[pltpu.HBM(a_.shape, a_.dtype) for a_ in ops], jax.ShapeDtypeStruct((8, LANE), F32)),
        in_specs=[_HBM] * (2 * n),
        out_specs=(_SEM, _SEM, *[_HBM] * (2 * n), pl.BlockSpec(memory_space=pltpu.VMEM)),
        input_output_aliases={i: 2 + i for i in range(2 * n)},
        compiler_params=pltpu.CompilerParams(has_side_effects=_EFFECT),
    )(*ops)


def _xwait(started, gather, after, name):
    n = len(gather)
    ssem, rsem = started[0], started[1]
    bufs = started[2:2 + 2 * n]
    after = list(after) if isinstance(after, (list, tuple)) else [after]

    def body(*refs):
        src, land, ssem_, rsem_ = refs[:n], refs[n:2 * n], refs[2 * n], refs[2 * n + 1]
        for cp in _peer_copies(src, land, ssem_, rsem_, gather):
            cp.wait_send()
            cp.wait_recv()

    out = pl.pallas_call(
        body, name=name, out_shape=tuple(pltpu.HBM(a_.shape, a_.dtype) for a_ in bufs),
        in_specs=[_HBM] * (2 * n) + [_SEM, _SEM] + [pl.BlockSpec(memory_space=pl.ANY)] * len(after),
        out_specs=[_HBM] * (2 * n), input_output_aliases={i: i for i in range(2 * n)},
        compiler_params=pltpu.CompilerParams(has_side_effects=_EFFECT),
    )(*bufs, ssem, rsem, *after)
    return out[n:]


def _mods(c16, w_shard, b_shard, *, tn=512):
    n = w_shard.shape[1]

    def body(c_ref, w_ref, b_ref, o_ref):
        cv = c_ref[...]
        cond = (cv * jax.nn.sigmoid(cv)).astype(BF)
        o_ref[...] = jnp.dot(cond, w_ref[...].astype(BF), preferred_element_type=F32) + b_ref[...]

    return pl.pallas_call(body, name="ada_mods", grid=(n // tn,),
                          in_specs=[_fix((16, D)), pl.BlockSpec((D, tn), lambda j: (0, j)),
                                    pl.BlockSpec((1, tn), lambda j: (0, j))],
                          out_specs=pl.BlockSpec((16, tn), lambda j: (0, j)),
                          out_shape=jax.ShapeDtypeStruct((16, n), F32), compiler_params=_cp("parallel"))(c16, w_shard, b_shard)


def _gw_ada(c_t, dm, *, tr=256):
    n = dm.shape[1]

    def body(ct_ref, dm_ref, o_ref):
        acc = jnp.zeros((tr, n), F32)
        for b in range(NDEV):
            cv = ct_ref[:, b:b + 1]
            cond = (cv * jax.nn.sigmoid(cv)).astype(BF).astype(F32)
            acc = acc + cond * dm_ref[b:b + 1, :].astype(BF).astype(F32)
        o_ref[...] = acc

    return pl.pallas_call(body, name="gw_ada", grid=(D // tr,),
                          in_specs=[_row(tr, NDEV), _fix((NDEV, n))], out_specs=_row(tr, n),
                          out_shape=jax.ShapeDtypeStruct((D, n), F32), compiler_params=_cp("parallel"))(c_t, dm)


def _sum8(parts, *, name, tc=512):
    _, r, c = parts.shape
    tc = min(tc, c)

    def body(p_ref, o_ref):
        acc = p_ref[0].astype(F32)
        for d in range(1, NDEV):
            acc = acc + p_ref[d].astype(F32)
        o_ref[...] = acc

    return pl.pallas_call(body, name=name, grid=(c // tc,),
                          in_specs=[pl.BlockSpec((NDEV, r, tc), lambda j: (0, 0, j))],
                          out_specs=pl.BlockSpec((r, tc), lambda j: (0, j)),
                          out_shape=jax.ShapeDtypeStruct((r, c), F32), compiler_params=_cp("parallel"))(parts)


def _adam(w, m, v, *, recv=None, grad=None, tr=None, tc=None, name):
    lead = w.ndim == 3
    r, c = w.shape[-2:]
    summed = recv is not None
    assert tc is not None or r % tr == 0
    at = (0,) if lead else (Ellipsis,)

    def body(g_ref, w_ref, m_ref, v_ref, go_ref, do_ref, mo_ref, vo_ref):
        if summed:
            g = g_ref[0].astype(F32)
            for d in range(1, NDEV):
                g = g + g_ref[d].astype(F32)
        else:
            g = g_ref[...]
        go_ref[at] = g
        mn = B1 * m_ref[at] + (1.0 - B1) * g
        vn = B2 * v_ref[at] + (1.0 - B2) * (g * g)
        mo_ref[at] = mn
        vo_ref[at] = vn
        mh = mn / (1.0 - B1 ** STEP)
        vh = vn / (1.0 - B2 ** STEP)
        do_ref[at] = -LR * (mh / (jnp.sqrt(vh) + AEPS) + WD * w_ref[at])

    if tc is not None:
        assert not lead and c % tc == 0
        gspec = (pl.BlockSpec((NDEV, r, tc), lambda j: (0, 0, j)) if summed
                 else pl.BlockSpec((r, tc), lambda j: (0, j)))
        wspec = pl.BlockSpec((r, tc), lambda j: (0, j))
        return pl.pallas_call(body, name=name, grid=(c // tc,),
                              in_specs=[gspec, wspec, wspec, wspec], out_specs=[wspec] * 4,
                              out_shape=[jax.ShapeDtypeStruct(w.shape, F32)] * 4,
                              compiler_params=_cp("parallel"))(recv if summed else grad, w, m, v)
    gspec = pl.BlockSpec((NDEV, tr, c), lambda i: (0, i, 0)) if summed else _row(tr, c)
    wspec = pl.BlockSpec((1, tr, c), lambda i: (0, i, 0)) if lead else _row(tr, c)
    return pl.pallas_call(body, name=name, grid=(r // tr,),
                          in_specs=[gspec, wspec, wspec, wspec], out_specs=[wspec] * 4,
                          out_shape=[jax.ShapeDtypeStruct(w.shape, F32)] * 4,
                          compiler_params=_cp("parallel"))(recv if summed else grad, w, m, v)


def _blocks_to_cols(b):
    _, r, width = b.shape
    return jnp.transpose(b, (1, 0, 2)).reshape(r, NDEV * width)


_SMALL = ("b_ada", "g_mix", "rel_bias", "conv_w", "conv_b", "dt_bias", "a_log", "d_skip",
          "g_att_out", "g_ssd_out", "g_ffn", "g_final")


def _pad_rows(v):
    v = v.reshape(-1)
    return jnp.pad(v, (0, (-v.shape[0]) % LANE))


def kernel(x, c, w_ada, b_ada, g_mix, w_in, rel_bias, conv_w, conv_b, dt_bias, a_log, d_skip, g_att_out, g_ssd_out, w_out, g_ffn, w_gate, w_up, w_down, g_final, loss_target, m_w_ada, m_b_ada, m_g_mix, m_w_in, m_rel_bias, m_conv_w, m_conv_b, m_dt_bias, m_a_log, m_d_skip, m_g_att_out, m_g_ssd_out, m_w_out, m_g_ffn, m_w_gate, m_w_up, m_w_down, m_g_final, v_w_ada, v_b_ada, v_g_mix, v_w_in, v_rel_bias, v_conv_w, v_conv_b, v_dt_bias, v_a_log, v_d_skip, v_g_att_out, v_g_ssd_out, v_w_out, v_g_ffn, v_w_gate, v_w_up, v_w_down, v_g_final):
    me = 4 * lax.axis_index("x") + 2 * lax.axis_index("y") + lax.axis_index("c")
    wts = dict(w_ada=w_ada, b_ada=b_ada, g_mix=g_mix, w_in=w_in, rel_bias=rel_bias, conv_w=conv_w, conv_b=conv_b,
               dt_bias=dt_bias, a_log=a_log, d_skip=d_skip, g_att_out=g_att_out, g_ssd_out=g_ssd_out, w_out=w_out,
               g_ffn=g_ffn, w_gate=w_gate, w_up=w_up, w_down=w_down, g_final=g_final)
    mom = dict(w_ada=m_w_ada, b_ada=m_b_ada, g_mix=m_g_mix, w_in=m_w_in, rel_bias=m_rel_bias, conv_w=m_conv_w,
               conv_b=m_conv_b, dt_bias=m_dt_bias, a_log=m_a_log, d_skip=m_d_skip, g_att_out=m_g_att_out,
               g_ssd_out=m_g_ssd_out, w_out=m_w_out, g_ffn=m_g_ffn, w_gate=m_w_gate, w_up=m_w_up, w_down=m_w_down,
               g_final=m_g_final)
    var = dict(w_ada=v_w_ada, b_ada=v_b_ada, g_mix=v_g_mix, w_in=v_w_in, rel_bias=v_rel_bias, conv_w=v_conv_w,
               conv_b=v_conv_b, dt_bias=v_dt_bias, a_log=v_a_log, d_skip=v_d_skip, g_att_out=v_g_att_out,
               g_ssd_out=v_g_ssd_out, w_out=v_w_out, g_ffn=v_g_ffn, w_gate=v_w_gate, w_up=v_w_up, w_down=v_w_down,
               g_final=v_g_final)
    order = ("w_ada", "b_ada", "g_mix", "w_in", "rel_bias", "conv_w", "conv_b", "dt_bias", "a_log", "d_skip",
             "g_att_out", "g_ssd_out", "w_out", "g_ffn", "w_gate", "w_up", "w_down", "g_final")

    c_all, rel_all, cw_all, _ = _exchange([c, rel_bias[0], conv_w[0]], [True] * 3, "gather_small")
    c_all = c_all.reshape(NDEV, D)
    rel_full = _blocks_to_cols(rel_all)
    cw_full = _blocks_to_cols(cw_all)

    ncol = w_ada.shape[2]
    b_sh = lax.dynamic_slice(b_ada, (0, me * ncol), (1, ncol))
    mods_part = _mods(jnp.pad(c_all, ((0, 8), (0, 0))), w_ada[0], b_sh)[:NDEV]
    mods_all, tok_m = _exchange([mods_part], [True], "gather_mods")
    mods_me = lax.dynamic_index_in_dim(mods_all, me, axis=1, keepdims=False).reshape(6, 1, D)
    mods = [mods_me[i] for i in range(6)]

    st_in = _xstart([(w_in[0] + tok_m[0, 0]).T.astype(BF)], [True], "gather_w_in_start")
    st_late = dict(out=_xstart([(w_out[0] + st_in[-1][0, 0]).astype(BF)], [True], "gather_w_out_start"))
    st_late["gate_up"] = _xstart([(w_gate[0] + st_late["out"][-1][0, 0]).T.astype(BF), w_up[0].T.astype(BF)],
                                 [True] * 2, "gather_w_gate_up_start")
    st_late["down"] = _xstart([(w_down[0] + st_late["gate_up"][-1][0, 0]).astype(BF)], [True],
                              "gather_w_down_start")
    st_rest = st_late["down"]

    def first_weights(after):
        (gi,) = _xwait(st_in, [True], after, "gather_w_in_wait")
        return jnp.pad(gi.reshape(INC, D), ((0, INP - INC), (0, 0)))

    def late_weights(which, after):
        got = _xwait(st_late[which], [True] * (2 if which == "gate_up" else 1), after, f"gather_w_{which}_wait")
        return [g_.reshape(-1, D) for g_ in got]

    started = {}

    def grads_ready(which, *g):
        if which == "in":
            g = (g[0][:INC],)
        srcs = [g_.reshape(NDEV, g_.shape[0] // NDEV, D) for g_ in g]
        started[which] = _xstart(srcs, [False] * len(srcs), f"exchange_{which}_grads_start")
        return started[which][-1]

    gx, small = _local_step(
        x[0], loss_target[0], mods, first_weights, late_weights, grads_ready,
        g_mix + st_rest[-1][0, 0], rel_full, cw_full, conv_b, dt_bias, a_log, d_skip, g_att_out, g_ssd_out, g_ffn,
        g_final.reshape(1, D))

    loss_row = jnp.pad(jnp.sum(small["loss"]).reshape(1), (0, LANE - 1))
    pack = jnp.concatenate([small[k] for k in ("dmods", "g_mix", "conv_b", "dt_bias", "a_log", "d_skip", "g_att",
                                               "g_ssd", "g_ffn", "g_final", "rel_bias", "conv_w")] + [loss_row])
    packs, _ = _exchange([pack.reshape(-1, LANE)], [True], "gather_small_grads")
    rg, ru, rd = _xwait(started["ffn"], [False] * 3, packs, "exchange_ffn_grads_wait")
    (ro,) = _xwait(started["out"], [False], rg, "exchange_out_grads_wait")
    (ri,) = _xwait(started["in"], [False], ro, "exchange_in_grads_wait")
    big = dict(
        w_gate=[o_.T for o_ in _adam(w_gate[0].T, m_w_gate[0].T, v_w_gate[0].T, recv=rg, tc=512,
                                     name="adam_w_gate")],
        w_up=[o_.T for o_ in _adam(w_up[0].T, m_w_up[0].T, v_w_up[0].T, recv=ru, tc=512, name="adam_w_up")],
        w_down=_adam(w_down[0], m_w_down[0], v_w_down[0], recv=rd, tr=176, name="adam_w_down"))
    tot = _sum8(packs, name="sum_small").reshape(-1)

    sizes = dict(dmods=6 * D, g_mix=D, conv_b=XBC, dt_bias=LANE, a_log=LANE, d_skip=LANE, g_att_out=AW, g_ssd_out=SW,
                 g_ffn=D, g_final=D, rel_bias=NH * NREL, conv_w=4 * XBC, loss=LANE)
    off, o = {}, 0
    for k_, n_ in sizes.items():
        off[k_] = o
        o += n_
    take = lambda k_, n_=None: tot[off[k_]:off[k_] + (n_ or sizes[k_])]
    loss = tot[off["loss"]]
    rel_w = NREL // NDEV
    cw_w = XBC // NDEV
    gsm = dict(
        b_ada=take("dmods"), g_mix=take("g_mix"),
        rel_bias=lax.dynamic_slice(take("rel_bias").reshape(NH, NREL), (0, me * rel_w), (NH, rel_w)),
        conv_w=lax.dynamic_slice(take("conv_w").reshape(4, XBC), (0, me * cw_w), (4, cw_w)),
        conv_b=take("conv_b"), dt_bias=take("dt_bias", NH), a_log=take("a_log", NH), d_skip=take("d_skip", NH),
        g_att_out=take("g_att_out"), g_ssd_out=take("g_ssd_out"), g_ffn=take("g_ffn"), g_final=take("g_final"))

    flat = lambda d_: jnp.concatenate([_pad_rows(d_[k_]) for k_ in _SMALL])
    gflat = flat(gsm)
    nrow = -(-gflat.shape[0] // (8 * LANE)) * 8
    to2d = lambda v_: jnp.pad(v_, (0, nrow * LANE - v_.shape[0])).reshape(nrow, LANE)
    res = _adam(to2d(flat(wts)), to2d(flat(mom)), to2d(flat(var)), grad=to2d(gflat), tr=nrow, name="adam_small")
    outs = {k_: {} for k_ in ("grad", "delta", "m", "v")}
    o = 0
    for k_ in _SMALL:
        n_ = wts[k_].size
        for kind, arr in zip(("grad", "delta", "m", "v"), res):
            outs[kind][k_] = arr.reshape(-1)[o:o + n_].reshape(wts[k_].shape)
        o += n_ + (-n_) % LANE

    dm_all = packs[:, :6 * D // LANE, :].reshape(NDEV, 6 * D)
    dm_sh = lax.dynamic_slice(dm_all, (0, me * ncol), (NDEV, ncol))
    gwa = _gw_ada(c_all.T, dm_sh)
    big.update(
        w_ada=_adam(w_ada[0], m_w_ada[0], v_w_ada[0], grad=gwa, tr=256, name="adam_w_ada"),
        w_in=[o_.T for o_ in _adam(w_in[0].T, m_w_in[0].T, v_w_in[0].T, recv=ri, tc=512, name="adam_w_in")],
        w_out=_adam(w_out[0], m_w_out[0], v_w_out[0], recv=ro, tr=128, name="adam_w_out"))
    for k_, r_ in big.items():
        for kind, arr in zip(("grad", "delta", "m", "v"), r_):
            outs[kind][k_] = arr.reshape(wts[k_].shape)

    return (loss, gx.reshape(x.shape), *[outs["grad"][k_] for k_ in order], *[outs["delta"][k_] for k_ in order],
            *[outs["m"][k_] for k_ in order], *[outs["v"][k_] for k_ in order])
```

```python
import functools

import jax
import jax.numpy as jnp
from jax import lax
from jax.experimental import pallas as pl
from jax.experimental.pallas import tpu as pltpu

F32 = jnp.float32
BF = jnp.bfloat16
HI = lax.Precision.HIGHEST

D = 2048
NH = 16
HD = 64
AW = 1024
SW = 1024
NG = 2
NS = 128
XBC = 1536
CH = 64
BAND = 576
NREL = 320
REL_CLIP = 256
FF = 5632
INC = 5648
INP = 5760
EPS = 1e-6
NDEV = 8
LANE = 128
VMEM_LIMIT = 56 * 1024 * 1024
LR, B1, B2, AEPS, WD, STEP = 0.001, 0.9, 0.999, 1e-08, 0.01, 10
MESH = pl.DeviceIdType.MESH


def _cp(*sem):
    return pltpu.CompilerParams(dimension_semantics=sem, vmem_limit_bytes=VMEM_LIMIT)


def _row(ts, w, col=0):
    return pl.BlockSpec((ts, w), lambda i, _c=col: (i, _c))


def _fix(shape, single=False):
    nd = len(shape)
    if single:
        return pl.BlockSpec(shape, lambda i, _n=nd: (0,) * _n, pipeline_mode=pl.Buffered(1))
    return pl.BlockSpec(shape, lambda i, _n=nd: (0,) * _n)


def _silu_grad(x, s):
    return s * (1.0 + x * (1.0 - s))


def _softplus(x):
    return jnp.maximum(x, 0.0) + jnp.log1p(jnp.exp(-jnp.abs(x)))


_DIMS = {"NN": (((1,), (0,)), ((), ())), "TN": (((0,), (0,)), ((), ())), "NT": (((1,), (1,)), ((), ()))}


def _mm(pairs, mode, out_dtype, name, tm, tn, tk, precision=None, after=None):
    a0, b0 = pairs[0]
    m, k = a0.shape[::-1] if mode == "TN" else a0.shape
    n = b0.shape[0] if mode == "NT" else b0.shape[1]
    tm, tn, tk = min(tm, m), min(tn, n), min(tk, k)
    nk1 = k // tk
    npair = len(pairs)
    nk = 1 if nk1 == 1 else nk1 * npair
    assert m % tm == 0 and n % tn == 0 and k % tk == 0, (name, a0.shape, b0.shape)
    dims = _DIMS[mode]

    def kloc(kk, p):
        return jnp.clip(kk - p * nk1, 0, nk1 - 1)

    def aspec(p):
        if mode == "TN":
            return pl.BlockSpec((tk, tm), lambda i, j, kk, _p=p: (kloc(kk, _p), i))
        return pl.BlockSpec((tm, tk), lambda i, j, kk, _p=p: (i, kloc(kk, _p)))

    def bspec(p):
        extra = dict(pipeline_mode=pl.Buffered(1)) if (nk == 1 and tn == n) else {}
        if mode == "NT":
            return pl.BlockSpec((tn, tk), lambda i, j, kk, _p=p: (j, kloc(kk, _p)), **extra)
        return pl.BlockSpec((tk, tn), lambda i, j, kk, _p=p: (kloc(kk, _p), j), **extra)

    nin = 2 * npair + (after is not None)

    def body(*refs):
        ab, o_ref = refs[:2 * npair], refs[nin]
        if nk == 1:
            tot = lax.dot_general(ab[0][...], ab[1][...], dims, preferred_element_type=F32, precision=precision)
            for p in range(1, npair):
                tot = tot + lax.dot_general(ab[2 * p][...], ab[2 * p + 1][...], dims, preferred_element_type=F32,
                                            precision=precision)
            o_ref[...] = tot.astype(o_ref.dtype)
            return
        acc = refs[nin + 1]
        kk = pl.program_id(2)

        @pl.when(kk == 0)
        def _():
            acc[...] = jnp.zeros_like(acc)

        for p in range(npair):
            @pl.when((kk >= p * nk1) & (kk < (p + 1) * nk1))
            def _(p=p):
                acc[...] += lax.dot_general(ab[2 * p][...], ab[2 * p + 1][...], dims, preferred_element_type=F32,
                                            precision=precision)

        @pl.when(kk == nk - 1)
        def _():
            o_ref[...] = acc[...].astype(o_ref.dtype)

    in_specs = []
    for p in range(npair):
        in_specs += [aspec(p), bspec(p)]
    operands = [t for ab_ in pairs for t in ab_]
    if after is not None:
        in_specs.append(pl.BlockSpec(memory_space=pl.ANY))
        operands.append(after)
    return pl.pallas_call(
        body, name=name, grid=(m // tm, n // tn, nk), in_specs=in_specs,
        out_specs=pl.BlockSpec((tm, tn), lambda i, j, kk: (i, j)),
        out_shape=jax.ShapeDtypeStruct((m, n), out_dtype),
        scratch_shapes=[] if nk == 1 else [pltpu.VMEM((tm, tn), F32)],
        compiler_params=_cp("parallel", "parallel", "arbitrary"),
    )(*operands)


def _norm_fwd(x, g, sc, sh, mix=None, gt=None, *, name, ts=512):
    s, d = x.shape
    has = mix is not None

    def body(*refs):
        if has:
            x_ref, g_ref, sc_ref, sh_ref, mix_ref, gt_ref, h_ref, x1_ref = refs
        else:
            x_ref, g_ref, sc_ref, sh_ref, h_ref = refs
        xv = x_ref[...]
        if has:
            xv = xv + gt_ref[...] * mix_ref[...]
            x1_ref[...] = xv
        r = lax.rsqrt(jnp.mean(xv * xv, axis=-1, keepdims=True) + EPS)
        h_ref[...] = ((xv * r) * g_ref[...] * (1.0 + sc_ref[...]) + sh_ref[...]).astype(BF)

    ins = [x, g, sc, sh] + ([mix, gt] if has else [])
    in_specs = [_row(ts, d), _fix((1, d)), _fix((1, d)), _fix((1, d))] + ([_row(ts, d), _fix((1, d))] if has else [])
    out_shape = [jax.ShapeDtypeStruct((s, d), BF)] + ([jax.ShapeDtypeStruct((s, d), F32)] if has else [])
    out_specs = [_row(ts, d)] + ([_row(ts, d)] if has else [])
    out = pl.pallas_call(body, name=name, grid=(s // ts,), in_specs=in_specs, out_specs=out_specs,
                         out_shape=out_shape, compiler_params=_cp("parallel"))(*ins)
    return out if has else out[0]


def _norm_bwd(x, g, sc, sh, dh, dres, mix=None, gt=None, *, name, ts=256):
    s, d = x.shape
    has = mix is not None

    def body(*refs):
        if has:
            x_ref, g_ref, sc_ref, sh_ref, dh_ref, dres_ref, mix_ref, gt_ref, dx_ref, dmix_ref, part_ref = refs
        else:
            x_ref, g_ref, sc_ref, sh_ref, dh_ref, dres_ref, dx_ref, part_ref = refs
        i = pl.program_id(0)
        xv = x_ref[...]
        gv = g_ref[...]
        dhv = dh_ref[...].astype(F32)
        r = lax.rsqrt(jnp.mean(xv * xv, axis=-1, keepdims=True) + EPS)
        xh = xv * r
        dyn = dhv * (1.0 + sc_ref[...])
        gy = dyn * gv
        dx = dres_ref[...] + r * (gy - xh * jnp.mean(gy * xh, axis=-1, keepdims=True))
        dx_ref[...] = dx

        @pl.when(i == 0)
        def _():
            part_ref[...] = jnp.zeros_like(part_ref)

        part_ref[0:1, :] += jnp.sum(dhv * (xh * gv), axis=0, keepdims=True)
        part_ref[1:2, :] += jnp.sum(dhv, axis=0, keepdims=True)
        part_ref[2:3, :] += jnp.sum(dyn * xh, axis=0, keepdims=True)
        if has:
            part_ref[3:4, :] += jnp.sum(dx * mix_ref[...], axis=0, keepdims=True)
            dmix_ref[...] = (gt_ref[...] * dx).astype(BF)

    ins = [x, g, sc, sh, dh, dres] + ([mix, gt] if has else [])
    in_specs = [_row(ts, d), _fix((1, d)), _fix((1, d)), _fix((1, d)), _row(ts, d), _row(ts, d)]
    in_specs += [_row(ts, d), _fix((1, d))] if has else []
    out_shape = [jax.ShapeDtypeStruct((s, d), F32)] + ([jax.ShapeDtypeStruct((s, d), BF)] if has else [])
    out_shape += [jax.ShapeDtypeStruct((8, d), F32)]
    out_specs = [_row(ts, d)] + ([_row(ts, d)] if has else []) + [_fix((8, d))]
    return pl.pallas_call(body, name=name, grid=(s // ts,), in_specs=in_specs, out_specs=out_specs,
                          out_shape=out_shape, compiler_params=_cp("arbitrary"))(*ins)


def _final(x1, ffn, gt2, gf, tgt, *, ts=256):
    s, d = x1.shape

    def body(x1_ref, ffn_ref, gt_ref, g_ref, t_ref, dx_ref, dffn_ref, part_ref):
        i = pl.program_id(0)
        fv = ffn_ref[...]
        gv = g_ref[...]
        xv = x1_ref[...] + gt_ref[...] * fv
        r = lax.rsqrt(jnp.mean(xv * xv, axis=-1, keepdims=True) + EPS)
        xh = xv * r
        e = xh * gv - t_ref[...]
        dy = e * (1.0 / d)
        gy = dy * gv
        dx = r * (gy - xh * jnp.mean(gy * xh, axis=-1, keepdims=True))
        dx_ref[...] = dx
        dffn_ref[...] = (gt_ref[...] * dx).astype(BF)

        @pl.when(i == 0)
        def _():
            part_ref[...] = jnp.zeros_like(part_ref)

        part_ref[0:1, :] += jnp.sum(dy * xh, axis=0, keepdims=True)
        part_ref[1:2, :] += jnp.sum(dx * fv, axis=0, keepdims=True)
        part_ref[2:3, :] += jnp.sum(e * e, axis=0, keepdims=True) * (0.5 / d)

    return pl.pallas_call(
        body, name="final_loss", grid=(s // ts,),
        in_specs=[_row(ts, d), _row(ts, d), _fix((1, d)), _fix((1, d)), _row(ts, d)],
        out_specs=[_row(ts, d), _row(ts, d), _fix((8, d))],
        out_shape=[jax.ShapeDtypeStruct((s, d), F32), jax.ShapeDtypeStruct((s, d), BF),
                   jax.ShapeDtypeStruct((8, d), F32)],
        compiler_params=_cp("arbitrary"))(x1, ffn, gt2, gf, tgt)


def _gate_up(h, wg_t, wu_t, *, tm=2048, tn=512, sub=512):
    s, k = h.shape
    n = wg_t.shape[0]
    tm = min(tm, s)

    def body(h_ref, wg_ref, wu_ref, g_ref, u_ref, a_ref):
        hv = h_ref[...]
        for c0 in range(0, tn, sub):
            cs_ = slice(c0, c0 + sub)
            gv = lax.dot_general(hv, wg_ref[cs_, :], _DIMS["NT"], preferred_element_type=F32)
            uv = lax.dot_general(hv, wu_ref[cs_, :], _DIMS["NT"], preferred_element_type=F32)
            g_ref[:, cs_] = gv.astype(BF)
            u_ref[:, cs_] = uv.astype(BF)
            a_ref[:, cs_] = (gv * jax.nn.sigmoid(gv) * uv).astype(BF)

    wspec = pl.BlockSpec((tn, k), lambda i, j: (j, 0))
    ospec = pl.BlockSpec((tm, tn), lambda i, j: (i, j))
    return pl.pallas_call(body, name="gate_up", grid=(s // tm, n // tn),
                          in_specs=[pl.BlockSpec((tm, k), lambda i, j: (i, 0)), wspec, wspec],
                          out_specs=[ospec] * 3, out_shape=[jax.ShapeDtypeStruct((s, n), BF)] * 3,
                          compiler_params=_cp("parallel", "parallel"))(h, wg_t, wu_t)


def _gate_up_bwd(dffn, w_down, gate, up, *, tm=2048, tn=512, sub=512):
    s, k = dffn.shape
    n = w_down.shape[0]
    tm = min(tm, s)

    def body(d_ref, w_ref, g_ref, u_ref, dg_ref, du_ref):
        dv_ = d_ref[...]
        for c0 in range(0, tn, sub):
            cs_ = slice(c0, c0 + sub)
            dav = lax.dot_general(dv_, w_ref[cs_, :], _DIMS["NT"], preferred_element_type=F32)
            gv = g_ref[:, cs_].astype(F32)
            uv = u_ref[:, cs_].astype(F32)
            sg = jax.nn.sigmoid(gv)
            dg_ref[:, cs_] = (dav * uv * _silu_grad(gv, sg)).astype(BF)
            du_ref[:, cs_] = (dav * gv * sg).astype(BF)

    tile = pl.BlockSpec((tm, tn), lambda i, j: (i, j))
    return pl.pallas_call(body, name="gate_up_bwd", grid=(s // tm, n // tn),
                          in_specs=[pl.BlockSpec((tm, k), lambda i, j: (i, 0)),
                                    pl.BlockSpec((tn, k), lambda i, j: (j, 0)), tile, tile],
                          out_specs=[tile] * 2, out_shape=[jax.ShapeDtypeStruct((s, n), BF)] * 2,
                          compiler_params=_cp("parallel", "parallel"))(dffn, w_down, gate, up)


def _mixout_fwd(att, y, proj, g_att, g_ssd, *, ts=512):
    s = att.shape[0]

    def body(a_ref, y_ref, z_ref, ga_ref, gs_ref, o_ref):
        av = a_ref[...]
        ra = lax.rsqrt(jnp.mean(av * av, axis=-1, keepdims=True) + EPS)
        o_ref[:, 0:AW] = (av * ra * ga_ref[...]).astype(BF)
        zv = z_ref[...]
        yz = y_ref[...] * (zv * jax.nn.sigmoid(zv))
        rs = lax.rsqrt(jnp.mean(yz * yz, axis=-1, keepdims=True) + EPS)
        o_ref[:, AW:AW + SW] = (yz * rs * gs_ref[...]).astype(BF)

    return pl.pallas_call(body, name="mixout_fwd", grid=(s // ts,),
                          in_specs=[_row(ts, AW), _row(ts, SW), _row(ts, SW, 3), _fix((1, AW)), _fix((1, SW))],
                          out_specs=_row(ts, AW + SW), out_shape=jax.ShapeDtypeStruct((s, AW + SW), BF),
                          compiler_params=_cp("parallel"))(att, y, proj, g_att, g_ssd)


def _mixout_bwd(dcat, att, y, proj, g_att, g_ssd, *, ts=512):
    s = att.shape[0]

    def body(dc_ref, a_ref, y_ref, z_ref, ga_ref, gs_ref, da_ref, dy_ref, dz_ref, part_ref):
        i = pl.program_id(0)
        av = a_ref[...]
        dca = dc_ref[:, 0:AW].astype(F32)
        ra = lax.rsqrt(jnp.mean(av * av, axis=-1, keepdims=True) + EPS)
        ah = av * ra
        gy = dca * ga_ref[...]
        da_ref[...] = ra * (gy - ah * jnp.mean(gy * ah, axis=-1, keepdims=True))
        zv = z_ref[...]
        yv = y_ref[...]
        sg = jax.nn.sigmoid(zv)
        sz = zv * sg
        yz = yv * sz
        dcs = dc_ref[:, AW:AW + SW].astype(F32)
        rs = lax.rsqrt(jnp.mean(yz * yz, axis=-1, keepdims=True) + EPS)
        yh = yz * rs
        gys = dcs * gs_ref[...]
        dyz = rs * (gys - yh * jnp.mean(gys * yh, axis=-1, keepdims=True))
        dy_ref[...] = dyz * sz
        dz_ref[...] = (dyz * yv * _silu_grad(zv, sg)).astype(BF)

        @pl.when(i == 0)
        def _():
            part_ref[...] = jnp.zeros_like(part_ref)

        part_ref[0:1, :] += jnp.sum(dca * ah, axis=0, keepdims=True)
        part_ref[1:2, :] += jnp.sum(dcs * yh, axis=0, keepdims=True)

    return pl.pallas_call(
        body, name="mixout_bwd", grid=(s // ts,),
        in_specs=[_row(ts, AW + SW), _row(ts, AW), _row(ts, SW), _row(ts, SW, 3), _fix((1, AW)), _fix((1, SW))],
        out_specs=[_row(ts, AW), _row(ts, SW), _row(ts, SW), _fix((8, AW))],
        out_shape=[jax.ShapeDtypeStruct((s, AW), F32), jax.ShapeDtypeStruct((s, SW), F32),
                   jax.ShapeDtypeStruct((s, SW), BF), jax.ShapeDtypeStruct((8, AW), F32)],
        compiler_params=_cp("arbitrary"))(dcat, att, y, proj, g_att, g_ssd)


SCALE = HD ** -0.5
QB = 256
NQC = QB // CH
WIN = BAND - CH + QB
NKB = WIN // QB


def _bias_window(bias):
    rows = [jnp.pad(bias, ((0, 0), (0, 0), (CH * a, WIN - BAND - CH * a)), constant_values=-jnp.inf)
            for a in range(NQC)]
    return jnp.concatenate(rows, axis=1)


def _win_specs(width, nb):
    def mk(col, back):
        return pl.BlockSpec((QB, width), lambda i, _c=col, _k=back: (jnp.maximum(jnp.minimum(i, nb - 1) - _k, 0), _c))
    return [mk(col, back) for col in (1, 2) for back in range(NKB - 1, -1, -1)]


def _attn2_fwd(qkv, bias_w):
    s = qkv.shape[0]
    nb = s // QB

    def body(q_ref, *refs):
        krefs, vrefs = refs[:NKB], refs[NKB:2 * NKB]
        b_ref, o_ref, lse_ref, kbuf, vbuf = refs[2 * NKB:]
        i = pl.program_id(0)
        for j in range(NKB):
            kbuf[QB * j:QB * j + QB, :] = krefs[j][...].astype(BF)
            vbuf[QB * j:QB * j + QB, :] = vrefs[j][...].astype(BF)
        valid = (i * QB - (NKB - 1) * QB + lax.broadcasted_iota(jnp.int32, (1, WIN), 1)) >= 0
        lane = lax.broadcasted_iota(jnp.int32, (QB, LANE), 1)

        lo = lane < HD

        def heads(masked):
            lse = jnp.zeros((QB, LANE), F32)
            for p in range(NH // 2):
                sl = slice(2 * HD * p, 2 * HD * p + 2 * HD)
                qp = q_ref[:, sl] * SCALE
                qbd = jnp.concatenate([jnp.where(lo, qp, 0.0), jnp.where(lo, 0.0, qp)], axis=0).astype(BF)
                sc = lax.dot_general(qbd, kbuf[:, sl], _DIMS["NT"], preferred_element_type=F32) + b_ref[p]
                if masked:
                    sc = jnp.where(valid, sc, -jnp.inf)
                mx = jnp.max(sc, axis=-1, keepdims=True)
                e = jnp.exp(sc - mx)
                tot = jnp.sum(e, axis=-1, keepdims=True)
                o2 = jnp.dot(e.astype(BF), vbuf[:, sl], preferred_element_type=F32) * (1.0 / tot)
                o_ref[:, sl] = jnp.where(lo, o2[0:QB], o2[QB:2 * QB])
                l2 = mx + jnp.log(tot)
                lse = jnp.where(lane == 2 * p, l2[0:QB], jnp.where(lane == 2 * p + 1, l2[QB:2 * QB], lse))
            lse_ref[...] = lse

        pl.when(i < NKB - 1)(functools.partial(heads, True))
        pl.when(i >= NKB - 1)(functools.partial(heads, False))

    return pl.pallas_call(
        body, name="attn_fwd", grid=(nb,),
        in_specs=[_row(QB, AW)] + _win_specs(AW, nb) + [_fix((NH // 2, 2 * QB, WIN), single=True)],
        out_specs=[_row(QB, AW), _row(QB, LANE)],
        out_shape=[jax.ShapeDtypeStruct((s, AW), F32), jax.ShapeDtypeStruct((s, LANE), F32)],
        scratch_shapes=[pltpu.VMEM((WIN, AW), BF), pltpu.VMEM((WIN, AW), BF)],
        compiler_params=_cp("parallel"))(*[qkv] * (1 + 2 * NKB), bias_w.reshape(NH // 2, 2 * QB, WIN))


def _attn2_bwd(qkv, datt, lse_t, bias_wt):
    s = qkv.shape[0]
    nb = s // QB
    cur = lambda i: jnp.minimum(i, nb - 1)
    late = lambda i: jnp.maximum(i - (NKB - 1), 0)

    def body(q_ref, *refs):
        krefs, vrefs = refs[:NKB], refs[NKB:2 * NKB]
        do_ref, lse_ref, b_ref, dq_ref, dk_ref, dv_ref, db_ref, kbuf, vbuf, dkacc, dvacc, dsbuf = refs[2 * NKB:]
        i = pl.program_id(0)

        @pl.when(i == 0)
        def _():
            dkacc[...] = jnp.zeros_like(dkacc)
            dvacc[...] = jnp.zeros_like(dvacc)
            db_ref[...] = jnp.zeros_like(db_ref)

        def heads(masked):
            for j in range(NKB):
                kbuf[QB * j:QB * j + QB, :] = krefs[j][...].astype(BF)
                vbuf[QB * j:QB * j + QB, :] = vrefs[j][...].astype(BF)
            valid = (i * QB - (NKB - 1) * QB + lax.broadcasted_iota(jnp.int32, (WIN, 1), 0)) >= 0
            lo = lax.broadcasted_iota(jnp.int32, (QB, LANE), 1) < HD

            def both(ref, sl):
                t = ref[:, sl]
                return jnp.concatenate([jnp.where(lo, t, 0.0), jnp.where(lo, 0.0, t)], axis=0)

            for p in range(NH // 2):
                sl = slice(2 * HD * p, 2 * HD * p + 2 * HD)
                qbd = (both(q_ref, sl) * SCALE).astype(BF)
                dobd = both(do_ref, sl).astype(BF)
                k = kbuf[:, sl]
                st = lax.dot_general(k, qbd, _DIMS["NT"], preferred_element_type=F32) + b_ref[p]
                if masked:
                    st = jnp.where(valid, st, -jnp.inf)
                lrow = jnp.concatenate([lse_ref[2 * p:2 * p + 1, :], lse_ref[2 * p + 1:2 * p + 2, :]], axis=1)
                pt = jnp.exp(st - lrow)
                dvacc[:, sl] += jnp.dot(pt.astype(BF), dobd, preferred_element_type=F32)
                dpt = lax.dot_general(vbuf[:, sl], dobd, _DIMS["NT"], preferred_element_type=F32)
                dst = pt * (dpt - jnp.sum(dpt * pt, axis=0, keepdims=True))
                dsbuf[...] = dst
                for hh in range(2):
                    fold = dsbuf[0:BAND, QB * hh:QB * hh + CH]
                    for a in range(1, NQC):
                        fold = fold + dsbuf[CH * a:CH * a + BAND, QB * hh + CH * a:QB * hh + CH * a + CH]
                    db_ref[2 * p + hh] += fold
                dsb = dst.astype(BF)
                dkacc[:, sl] += jnp.dot(dsb, qbd, preferred_element_type=F32)
                dq2 = lax.dot_general(dsb, k, _DIMS["TN"], preferred_element_type=F32) * SCALE
                dq_ref[:, sl] = jnp.where(lo, dq2[0:QB], dq2[QB:2 * QB]).astype(BF)

        pl.when(i < nb)(functools.partial(heads, True))

        dk_ref[...] = dkacc[0:QB, :].astype(BF)
        dv_ref[...] = dvacc[0:QB, :].astype(BF)
        for acc in (dkacc, dvacc):
            for j in range(NKB - 1):
                acc[QB * j:QB * j + QB, :] = acc[QB * j + QB:QB * j + 2 * QB, :]
            acc[WIN - QB:WIN, :] = jnp.zeros((QB, AW), F32)

    qspec = pl.BlockSpec((QB, AW), lambda i: (cur(i), 0))
    lspec = pl.BlockSpec((QB, AW), lambda i: (late(i), 0))
    return pl.pallas_call(
        body, name="attn_bwd", grid=(nb + NKB - 1,),
        in_specs=[qspec] + _win_specs(AW, nb) + [qspec, pl.BlockSpec((NH, QB), lambda i: (0, cur(i))),
                                                 _fix((NH // 2, WIN, 2 * QB), single=True)],
        out_specs=[qspec, lspec, lspec, _fix((NH, BAND, CH))],
        out_shape=[jax.ShapeDtypeStruct((s, AW), BF)] * 3 + [jax.ShapeDtypeStruct((NH, BAND, CH), F32)],
        scratch_shapes=[pltpu.VMEM((WIN, AW), BF), pltpu.VMEM((WIN, AW), BF),
                        pltpu.VMEM((WIN, AW), F32), pltpu.VMEM((WIN, AW), F32), pltpu.VMEM((WIN, 2 * QB), F32)],
        compiler_params=_cp("arbitrary"))(*[qkv] * (1 + 2 * NKB), datt, lse_t, bias_wt)


TB = 512
SCH = 512
NCB = TB // SCH
GW = 8 * HD


def _tri(lower):
    r = lax.broadcasted_iota(jnp.int32, (SCH, SCH), 0)
    c = lax.broadcasted_iota(jnp.int32, (SCH, SCH), 1)
    return r >= c if lower else r <= c


def _ssd_fwd(proj, conv_w, conv_b, dt_bias, a_log, d_skip):
    s = proj.shape[0]
    nb = s // TB

    def body(xs_ref, bc_ref, dt_ref, cw_ref, cb_ref, dtb_ref, al_ref, dsk_ref, y_ref, pre_ref, prev_ref,
             xpad, ubuf, dtbuf, csb, cstb, hst, xdec, yoffb, drow):
        i = pl.program_id(0)

        @pl.when(i == 0)
        def _():
            xpad[0:8, :] = jnp.zeros((8, XBC), F32)
            hst[...] = jnp.zeros_like(hst)

        xpad[8:8 + TB, 0:SW] = xs_ref[...]
        xpad[8:8 + TB, SW:XBC] = bc_ref[...]
        pre = cb_ref[...]
        for kk in range(4):
            pre = pre + cw_ref[kk:kk + 1, :] * xpad[5 + kk:5 + kk + TB, :]
        pre_ref[...] = pre
        ubuf[...] = pre * jax.nn.sigmoid(pre)
        xpad[0:8, :] = xpad[TB:TB + 8, :]
        dtbuf[...] = _softplus(dt_ref[...] + dtb_ref[...])
        a = -jnp.exp(al_ref[...])
        tri = _tri(True).astype(F32)
        causal = _tri(True)

        def chunk(c, carry):
            r0 = pl.multiple_of(c * SCH, SCH)
            rows = pl.ds(r0, SCH)
            cs = jnp.dot(tri, dtbuf[rows, :] * a, precision=HI, preferred_element_type=F32)
            csb[...] = cs
            cstb[...] = cs.T
            for g in range(NG):
                bg = ubuf[rows, SW + NS * g:SW + NS * g + NS]
                cg = ubuf[rows, SW + NG * NS + NS * g:SW + NG * NS + NS * g + NS].astype(BF)
                cb = lax.dot_general(cg, bg.astype(BF), (((1,), (1,)), ((), ())), preferred_element_type=F32)
                hg = hst[g]
                prev_ref[c, g] = hg
                yoffb[...] = jnp.dot(cg, hg.astype(BF), preferred_element_type=F32)
                for r in range(8):
                    h = 8 * g + r
                    sl = slice(HD * h, HD * h + HD)
                    rs = slice(HD * r, HD * r + HD)
                    cs_h = csb[:, h:h + 1]
                    cl = csb[SCH - 1:SCH, h:h + 1]
                    seg = jnp.exp(jnp.where(causal, cs_h - cstb[h:h + 1, :], -jnp.inf))
                    xh = ubuf[rows, sl]
                    xdt = xh * dtbuf[rows, h:h + 1]
                    yd = jnp.dot((cb * seg).astype(BF), xdt.astype(BF), preferred_element_type=F32)
                    y_ref[rows, sl] = yd + jnp.exp(cs_h) * yoffb[:, rs] + xh * dsk_ref[:, h:h + 1]
                    xdec[:, rs] = xdt * jnp.exp(cl - cs_h)
                    drow[:, rs] = jnp.broadcast_to(jnp.exp(cl), (1, HD))
                st = jnp.dot(bg.T.astype(BF), xdec[...].astype(BF), preferred_element_type=F32)
                hst[g] = hg * drow[...] + st
            return carry

        lax.fori_loop(0, NCB, chunk, 0)

    return pl.pallas_call(
        body, name="ssd_fwd", grid=(nb,),
        in_specs=[_row(TB, SW, 4), _row(TB, 512, 10), _row(TB, LANE, 44), _fix((4, XBC)), _fix((1, XBC)),
                  _fix((1, LANE)), _fix((1, LANE)), _fix((1, LANE))],
        out_specs=[_row(TB, SW), _row(TB, XBC), pl.BlockSpec((NCB, NG, NS, GW), lambda i: (i, 0, 0, 0))],
        out_shape=[jax.ShapeDtypeStruct((s, SW), F32), jax.ShapeDtypeStruct((s, XBC), F32),
                   jax.ShapeDtypeStruct((s // SCH, NG, NS, GW), F32)],
        scratch_shapes=[pltpu.VMEM((TB + 8, XBC), F32), pltpu.VMEM((TB, XBC), F32), pltpu.VMEM((TB, LANE), F32),
                        pltpu.VMEM((SCH, LANE), F32), pltpu.VMEM((LANE, SCH), F32), pltpu.VMEM((NG, NS, GW), F32),
                        pltpu.VMEM((SCH, GW), F32), pltpu.VMEM((SCH, GW), F32), pltpu.VMEM((1, GW), F32)],
        compiler_params=_cp("arbitrary"))(proj, proj, proj, conv_w, conv_b, dt_bias, a_log, d_skip)


def _ssd_bwd(dy, pre, proj, prev, conv_w, dt_bias, a_log, d_skip):
    s = dy.shape[0]
    nb = s // TB
    rev = lambda i: nb - 1 - i
    halo = lambda i: jnp.maximum((nb - 1 - i) * (TB // 8) - 1, 0)

    def body(dy_ref, pre_ref, dt_ref, prev_ref, xs_ref, bc_ref, xsh_ref, bch_ref, cw_ref, dtb_ref, al_ref, dsk_ref,
             dx_ref, ddt_ref, pw_ref, ph_ref,
             ubuf, dtbuf, dpad, xpad, csb, cstb, dhs, yoffb, ebuf, xdecb, wbuf, drow, dub):
        i = pl.program_id(0)

        @pl.when(i == 0)
        def _():
            dhs[...] = jnp.zeros_like(dhs)
            dpad[TB:TB + 8, :] = jnp.zeros((8, XBC), F32)
            pw_ref[...] = jnp.zeros_like(pw_ref)
            ph_ref[...] = jnp.zeros_like(ph_ref)

        pre = pre_ref[...]
        ubuf[...] = pre * jax.nn.sigmoid(pre)
        dtbuf[...] = _softplus(dt_ref[...] + dtb_ref[...])
        a = -jnp.exp(al_ref[...])
        tri = _tri(True).astype(F32)
        trit = _tri(False).astype(F32)
        causal = _tri(True)
        rowid = lax.broadcasted_iota(jnp.int32, (LANE, SCH), 0)
        lane = lax.broadcasted_iota(jnp.int32, (SCH, LANE), 1)
        lane1 = lax.broadcasted_iota(jnp.int32, (1, LANE), 1)
        lastrow = lax.broadcasted_iota(jnp.int32, (SCH, LANE), 0) == SCH - 1

        def chunk(cc, carry):
            c = NCB - 1 - cc
            r0 = pl.multiple_of(c * SCH, SCH)
            rows = pl.ds(r0, SCH)
            dtc = dtbuf[rows, :]
            cs = jnp.dot(tri, dtc * a, precision=HI, preferred_element_type=F32)
            csb[...] = cs
            cstb[...] = cs.T
            dcs = jnp.zeros((SCH, LANE), F32)
            dcst = jnp.zeros((LANE, SCH), F32)
            xr = jnp.zeros((SCH, LANE), F32)
            dlast = jnp.zeros((1, LANE), F32)
            dsk = jnp.zeros((1, LANE), F32)
            for g in range(NG):
                bg = ubuf[rows, SW + NS * g:SW + NS * g + NS]
                cg = ubuf[rows, SW + NG * NS + NS * g:SW + NG * NS + NS * g + NS]
                bb = bg.astype(BF)
                cbf = cg.astype(BF)
                cb = lax.dot_general(cbf, bb, (((1,), (1,)), ((), ())), preferred_element_type=F32)
                hp = prev_ref[c, g]
                hpb = hp.astype(BF)
                dhg = dhs[g]
                dhb = dhg.astype(BF)
                yoffb[...] = jnp.dot(cbf, hpb, preferred_element_type=F32)
                for r in range(8):
                    h = 8 * g + r
                    sl = slice(HD * h, HD * h + HD)
                    rs = slice(HD * r, HD * r + HD)
                    cs_h = csb[:, h:h + 1]
                    cl = csb[SCH - 1:SCH, h:h + 1]
                    dyh = dy_ref[rows, sl]
                    ecs = jnp.exp(cs_h)
                    ebuf[:, rs] = dyh * ecs
                    xdecb[:, rs] = ubuf[rows, sl] * dtbuf[rows, h:h + 1] * jnp.exp(cl - cs_h)
                    t1 = jnp.sum(dyh * ecs * yoffb[:, rs], axis=1, keepdims=True)
                    dcs = dcs + jnp.where(lane == h, t1, 0.0)
                    drow[:, rs] = jnp.broadcast_to(jnp.exp(cl), (1, HD))
                eb = ebuf[...].astype(BF)
                dc = lax.dot_general(eb, hpb, (((1,), (1,)), ((), ())), preferred_element_type=F32)
                dprev = jnp.dot(cg.T.astype(BF), eb, preferred_element_type=F32)
                wbuf[...] = jnp.dot(bb, dhb, preferred_element_type=F32)
                db = lax.dot_general(xdecb[...].astype(BF), dhb, (((1,), (1,)), ((), ())), preferred_element_type=F32)
                dcbs = jnp.zeros((SCH, SCH), F32)
                for r in range(8):
                    h = 8 * g + r
                    sl = slice(HD * h, HD * h + HD)
                    rs = slice(HD * r, HD * r + HD)
                    cs_h = csb[:, h:h + 1]
                    cst_h = cstb[h:h + 1, :]
                    cl = csb[SCH - 1:SCH, h:h + 1]
                    seg = jnp.exp(jnp.where(causal, cs_h - cst_h, -jnp.inf))
                    dyh = dy_ref[rows, sl]
                    xh = ubuf[rows, sl]
                    dt_h = dtbuf[rows, h:h + 1]
                    xdt = xh * dt_h
                    dyb = dyh.astype(BF)
                    dm = lax.dot_general(dyb, xdt.astype(BF), _DIMS["NT"], preferred_element_type=F32)
                    mm_ = cb * seg
                    dxdt = lax.dot_general(mm_.astype(BF), dyb, _DIMS["TN"], preferred_element_type=F32)
                    gg = dm * mm_
                    dcs_h = jnp.sum(gg, axis=1, keepdims=True)
                    dcst = dcst + jnp.where(rowid == h, jnp.sum(gg, axis=0, keepdims=True), 0.0)
                    dcbs = dcbs + dm * seg
                    dec = jnp.exp(cl - cs_h)
                    w_h = wbuf[:, rs]
                    dxdt = dxdt + dec * w_h
                    t = jnp.sum(w_h * xdt, axis=1, keepdims=True) * dec
                    dcs_h = dcs_h - t
                    dch = jnp.sum(jnp.sum(dhs[g, :, rs] * prev_ref[c, g, :, rs], axis=1, keepdims=True),
                                  axis=0, keepdims=True)
                    dl = jnp.sum(t, axis=0, keepdims=True) + dch * jnp.exp(cl)
                    dcs = dcs + jnp.where(lane == h, dcs_h, 0.0)
                    dlast = dlast + jnp.where(lane1 == h, dl, 0.0)
                    xr = xr + jnp.where(lane == h, jnp.sum(dxdt * xh, axis=1, keepdims=True), 0.0)
                    dskh = jnp.sum(jnp.sum(dyh * xh, axis=1, keepdims=True), axis=0, keepdims=True)
                    dsk = dsk + jnp.where(lane1 == h, dskh, 0.0)
                    dub[:, sl] = dyh * dsk_ref[:, h:h + 1] + dxdt * dt_h
                dc = dc + jnp.dot(dcbs.astype(BF), bb, preferred_element_type=F32)
                db = db + lax.dot_general(dcbs.astype(BF), cbf, _DIMS["TN"], preferred_element_type=F32)
                dub[:, SW + NS * g:SW + NS * g + NS] = db
                dub[:, SW + NG * NS + NS * g:SW + NG * NS + NS * g + NS] = dc
                dhs[g] = dprev + drow[...] * dhg
            dcs = dcs - dcst.T + jnp.where(lastrow, dlast, 0.0)
            dadt = jnp.dot(trit, dcs, precision=HI, preferred_element_type=F32)
            ddt = dadt * a + xr
            ddtr = ddt * jax.nn.sigmoid(dt_ref[rows, :] + dtb_ref[...])
            ddt_ref[rows, :] = ddtr.astype(BF)
            ph_ref[0:1, :] += jnp.sum(ddtr, axis=0, keepdims=True)
            ph_ref[1:2, :] += jnp.sum(dadt * dtc, axis=0, keepdims=True) * a
            ph_ref[2:3, :] += dsk
            pc = pre_ref[rows, :]
            sg = jax.nn.sigmoid(pc)
            dpad[rows, :] = dub[...] * _silu_grad(pc, sg)
            return carry

        lax.fori_loop(0, NCB, chunk, 0)

        dxb = cw_ref[0:1, :] * dpad[3:3 + TB, :]
        for kk in range(1, 4):
            dxb = dxb + cw_ref[kk:kk + 1, :] * dpad[3 - kk:3 - kk + TB, :]
        dx_ref[...] = dxb.astype(BF)
        keep = jnp.where(i < nb - 1, 1.0, 0.0)
        xpad[0:8, 0:SW] = xsh_ref[...] * keep
        xpad[0:8, SW:XBC] = bch_ref[...] * keep
        xpad[8:8 + TB, 0:SW] = xs_ref[...]
        xpad[8:8 + TB, SW:XBC] = bc_ref[...]
        dp = dpad[0:TB, :]
        for kk in range(4):
            pw_ref[kk:kk + 1, :] += jnp.sum(dp * xpad[5 + kk:5 + kk + TB, :], axis=0, keepdims=True)
        pw_ref[4:5, :] += jnp.sum(dp, axis=0, keepdims=True)
        dpad[TB:TB + 8, :] = dpad[0:8, :]

    rblk = lambda w, col: pl.BlockSpec((TB, w), lambda i, _c=col: (rev(i), _c))
    return pl.pallas_call(
        body, name="ssd_bwd", grid=(nb,),
        in_specs=[rblk(SW, 0), rblk(XBC, 0), rblk(LANE, 44),
                  pl.BlockSpec((NCB, NG, NS, GW), lambda i: (rev(i), 0, 0, 0)),
                  rblk(SW, 4), rblk(512, 10),
                  pl.BlockSpec((8, SW), lambda i: (halo(i), 4)), pl.BlockSpec((8, 512), lambda i: (halo(i), 10)),
                  _fix((4, XBC)), _fix((1, LANE)), _fix((1, LANE)), _fix((1, LANE))],
        out_specs=[rblk(XBC, 0), rblk(LANE, 0), _fix((8, XBC)), _fix((8, LANE))],
        out_shape=[jax.ShapeDtypeStruct((s, XBC), BF), jax.ShapeDtypeStruct((s, LANE), BF),
                   jax.ShapeDtypeStruct((8, XBC), F32), jax.ShapeDtypeStruct((8, LANE), F32)],
        scratch_shapes=[pltpu.VMEM((TB, XBC), F32), pltpu.VMEM((TB, LANE), F32), pltpu.VMEM((TB + 8, XBC), F32),
                        pltpu.VMEM((TB + 8, XBC), F32), pltpu.VMEM((SCH, LANE), F32), pltpu.VMEM((LANE, SCH), F32),
                        pltpu.VMEM((NG, NS, GW), F32), pltpu.VMEM((SCH, GW), F32), pltpu.VMEM((SCH, GW), F32),
                        pltpu.VMEM((SCH, GW), F32), pltpu.VMEM((SCH, GW), F32), pltpu.VMEM((1, GW), F32),
                        pltpu.VMEM((SCH, XBC), F32)],
        compiler_params=_cp("arbitrary"))(dy, pre, proj, prev, proj, proj, proj, proj, conv_w, dt_bias, a_log, d_skip)


def _rel_index():
    q = jnp.arange(CH)[:, None] + (BAND - CH)
    k = jnp.arange(BAND)[None, :]
    return jnp.clip(q - k, -(CH - 1), REL_CLIP) + (CH - 1)


def _local_step(x, tgt, mods, first_weights, late_weights, grads_ready,
                g_mix, rel_bias, conv_w, conv_b, dt_bias, a_log, d_skip, g_att, g_ssd, g_ffn, g_final):
    sh1, sc1, gt1, sh2, sc2, gt2 = mods
    pad16 = lambda v: jnp.pad(v, ((0, 0), (0, LANE - NH)))
    dtb_p, al_p, dsk_p = pad16(dt_bias), pad16(a_log), pad16(d_skip)
    ridx = _rel_index()
    ext = jnp.concatenate([jnp.broadcast_to(rel_bias[:, NREL - 1:], (NH, BAND + CH - 1 - NREL)),
                           rel_bias[:, ::-1]], axis=1)
    skew = jnp.tile(jnp.pad(ext, ((0, 0), (0, 1))), (1, CH + 1))[:, :CH * (BAND + CH + 1)]
    bias = skew.reshape(NH, CH, BAND + CH + 1)[:, ::-1, :BAND]
    bias_w = _bias_window(bias)
    bias_wt = jnp.transpose(bias_w.reshape(NH // 2, 2, QB, WIN), (0, 3, 1, 2)).reshape(NH // 2, WIN, 2 * QB)

    h1 = _norm_fwd(x, g_mix, sc1, sh1, name="norm1_fwd")
    onehot = (ridx.T.reshape(BAND * CH, 1) == jnp.arange(NREL)[None, :]).astype(F32)
    w_in_t = first_weights([h1, bias_w, bias_wt, onehot])
    proj = _mm([(h1, w_in_t)], "NT", F32, "in_proj", 256, INP, 2048)
    att, lse = _attn2_fwd(proj, bias_w)
    y, pre, prev = _ssd_fwd(proj, conv_w, conv_b, dtb_p, al_p, dsk_p)
    cat = _mixout_fwd(att, y, proj, g_att, g_ssd)
    (w_out,) = late_weights("out", cat)
    mix = _mm([(cat, w_out)], "NN", F32, "out_proj", 1024, 2048, 2048)
    h2, x1 = _norm_fwd(x, g_ffn, sc2, sh2, mix, gt1, name="norm2_fwd")
    w_gate_t, w_up_t = late_weights("gate_up", h2)
    gate, up, act = _gate_up(h2, w_gate_t, w_up_t)
    (w_down,) = late_weights("down", act)
    ffn = _mm([(act, w_down)], "NN", F32, "down_proj", 512, 2048, FF)
    dx2, dffn, pf = _final(x1, ffn, gt2, g_final, tgt)

    gw_down = _mm([(act, dffn)], "TN", BF, "gw_down", 1408, 2048, 1024)
    dgate, dup = _gate_up_bwd(dffn, w_down, gate, up)
    dh2 = _mm([(dgate, w_gate_t), (dup, w_up_t)], "NN", BF, "d_h2", 512, 512, FF)
    gw_gate_t = _mm([(dgate, h2)], "TN", BF, "gw_gate", 1408, 2048, 1024)
    gw_up_t = _mm([(dup, h2)], "TN", BF, "gw_up", 1408, 2048, 1024)
    tok = grads_ready("ffn", gw_gate_t, gw_up_t, gw_down)
    dx1, dmix, p2 = _norm_bwd(x1, g_ffn + tok[0, 0], sc2, sh2, dh2, dx2, mix, gt1, name="norm2_bwd")
    gw_out = _mm([(cat, dmix)], "TN", BF, "gw_out", 1024, 2048, 1024)
    tok = grads_ready("out", gw_out)
    dcat = _mm([(dmix, w_out)], "NT", BF, "d_cat", 1024, 2048, 2048, after=tok)
    datt, dy, dz, pm = _mixout_bwd(dcat, att, y, proj, g_att, g_ssd)
    dq, dk, dv, dbias_t = _attn2_bwd(proj, datt, lse[:, :NH].T, bias_wt)
    dxbc, ddt, pw, ph = _ssd_bwd(dy, pre, proj, prev, conv_w, dtb_p, al_p, dsk_p)
    dproj = jnp.concatenate([dq, dk, dv, dz, dxbc, ddt], axis=1)
    gw_in_t = _mm([(dproj, h1)], "TN", BF, "gw_in", 1152, 2048, 1024)
    tok = grads_ready("in", gw_in_t)
    dh1 = _mm([(dproj, w_in_t)], "NN", BF, "d_h1", 512, 2048, INP, after=tok)
    gx, p1 = _norm_bwd(x, g_mix, sc1, sh1, dh1, dx1, name="norm1_bwd")
    g_rel =_mm([(dbias_t.reshape(NH, BAND * CH), onehot)], "NN", F32, "g_rel_bias", NH, NREL, 4608, precision=HI)

    small = dict(
        dmods=jnp.concatenate([p1[1], p1[0], p2[3], p2[1], p2[0], pf[1]]),
        g_mix=p1[2], conv_b=pw[4], dt_bias=ph[0], a_log=ph[1], d_skip=ph[2], g_att=pm[0], g_ssd=pm[1],
        g_ffn=p2[2], g_final=pf[0], rel_bias=g_rel.reshape(-1), conv_w=pw[0:4].reshape(-1), loss=pf[2])
    return gx, small


def _exchange(srcs, gather, name):
    n = len(srcs)
    outs = [jax.ShapeDtypeStruct((NDEV,) + (s.shape if g else s.shape[1:]), s.dtype) for s, g in zip(srcs, gather)]
    outs.append(jax.ShapeDtypeStruct((8, LANE), F32))

    def body(*refs):
        src, dst, token = refs[:n], refs[n:2 * n], refs[2 * n]
        ssem, rsem, lsem = refs[2 * n + 1:]
        token[...] = jnp.zeros_like(token)
        x, y, c = lax.axis_index("x"), lax.axis_index("y"), lax.axis_index("c")
        me = 4 * x + 2 * y + c
        local = []
        for a in range(n):
            own = src[a] if gather[a] else src[a].at[me]
            cp = pltpu.make_async_copy(own, dst[a].at[me], lsem.at[a])
            cp.start()
            local.append(cp)
        remote = []
        for k in range(1, NDEV):
            px = 1 - x if k & 4 else x
            py = 1 - y if k & 2 else y
            pc = 1 - c if k & 1 else c
            pid = 4 * px + 2 * py + pc
            for a in range(n):
                s_ref = src[a] if gather[a] else src[a].at[pid]
                cp = pltpu.make_async_remote_copy(
                    src_ref=s_ref, dst_ref=dst[a].at[me], send_sem=ssem.at[a * (NDEV - 1) + k - 1],
                    recv_sem=rsem.at[a * (NDEV - 1) + k - 1], device_id=(px, py, pc), device_id_type=MESH)
                cp.start()
                remote.append(cp)
        for cp in remote:
            cp.wait()
        for cp in local:
            cp.wait()

    anyspec = pl.BlockSpec(memory_space=pl.ANY)
    return pl.pallas_call(
        body, name=name, out_shape=outs, in_specs=[anyspec] * n,
        out_specs=[anyspec] * n + [pl.BlockSpec(memory_space=pltpu.VMEM)],
        scratch_shapes=[pltpu.SemaphoreType.DMA((n * (NDEV - 1),)), pltpu.SemaphoreType.DMA((n * (NDEV - 1),)),
                        pltpu.SemaphoreType.DMA((n,))],
    )(*srcs)


_HBM = pl.BlockSpec(memory_space=pltpu.HBM)
_SEM = pl.BlockSpec(memory_space=pltpu.SEMAPHORE)
_EFFECT = pltpu.SideEffectType.DATAFLOW_SIDE_EFFECTING


def _peer_copies(src, land, ssem, rsem, gather):
    x, y, c = lax.axis_index("x"), lax.axis_index("y"), lax.axis_index("c")
    me = 4 * x + 2 * y + c
    copies = []
    for a in range(len(src)):
        for k in range(1, NDEV):
            px = 1 - x if k & 4 else x
            py = 1 - y if k & 2 else y
            pc = 1 - c if k & 1 else c
            s_ref = src[a] if gather[a] else src[a].at[4 * px + 2 * py + pc]
            idx = a * (NDEV - 1) + k - 1
            copies.append(pltpu.make_async_remote_copy(
                src_ref=s_ref, dst_ref=land[a].at[me], send_sem=ssem.at[idx], recv_sem=rsem.at[idx],
                device_id=(px, py, pc), device_id_type=MESH))
    return copies


def _xstart(srcs, gather, name):
    n = len(srcs)
    me = 4 * lax.axis_index("x") + 2 * lax.axis_index("y") + lax.axis_index("c")
    lands = []
    for s_, g_ in zip(srcs, gather):
        own = s_ if g_ else lax.dynamic_index_in_dim(s_, me, 0, keepdims=False)
        empty = lax.empty((NDEV,) + own.shape, own.dtype)
        lands.append(lax.dynamic_update_slice(empty, own[None], (me,) + (0,) * own.ndim))
    nsem = n * (NDEV - 1)

    def body(*refs):
        src, land, ssem, rsem, token = refs[:n], refs[n:2 * n], refs[2 * n], refs[2 * n + 1], refs[-1]
        for cp in _peer_copies(src, land, ssem, rsem, gather):
            cp.start()
        token[...] = jnp.zeros_like(token)

    ops = [pltpu.with_memory_space_constraint(a_, pltpu.HBM) for a_ in list(srcs) + lands]
    return pl.pallas_call(
        body, name=name,
        out_shape=(pltpu.SemaphoreType.DMA((nsem,)), pltpu.SemaphoreType.DMA((nsem,)),
                   *[pltpu.HBM(a_.shape, a_.dtype) for a_ in ops], jax.ShapeDtypeStruct((8, LANE), F32)),
        in_specs=[_HBM] * (2 * n),
        out_specs=(_SEM, _SEM, *[_HBM] * (2 * n), pl.BlockSpec(memory_space=pltpu.VMEM)),
        input_output_aliases={i: 2 + i for i in range(2 * n)},
        compiler_params=pltpu.CompilerParams(has_side_effects=_EFFECT),
    )(*ops)


def _xwait(started, gather, after, name):
    n = len(gather)
    ssem, rsem = started[0], started[1]
    bufs = started[2:2 + 2 * n]
    after = list(after) if isinstance(after, (list, tuple)) else [after]

    def body(*refs):
        src, land, ssem_, rsem_ = refs[:n], refs[n:2 * n], refs[2 * n], refs[2 * n + 1]
        for cp in _peer_copies(src, land, ssem_, rsem_, gather):
            cp.wait_send()
            cp.wait_recv()

    out = pl.pallas_call(
        body, name=name, out_shape=tuple(pltpu.HBM(a_.shape, a_.dtype) for a_ in bufs),
        in_specs=[_HBM] * (2 * n) + [_SEM, _SEM] + [pl.BlockSpec(memory_space=pl.ANY)] * len(after),
        out_specs=[_HBM] * (2 * n), input_output_aliases={i: i for i in range(2 * n)},
        compiler_params=pltpu.CompilerParams(has_side_effects=_EFFECT),
    )(*bufs, ssem, rsem, *after)
    return out[n:]


def _mods(c16, w_shard, b_shard, *, tn=512):
    n = w_shard.shape[1]

    def body(c_ref, w_ref, b_ref, o_ref):
        cv = c_ref[...]
        cond = (cv * jax.nn.sigmoid(cv)).astype(BF)
        o_ref[...] = jnp.dot(cond, w_ref[...].astype(BF), preferred_element_type=F32) + b_ref[...]

    return pl.pallas_call(body, name="ada_mods", grid=(n // tn,),
                          in_specs=[_fix((16, D)), pl.BlockSpec((D, tn), lambda j: (0, j)),
                                    pl.BlockSpec((1, tn), lambda j: (0, j))],
                          out_specs=pl.BlockSpec((16, tn), lambda j: (0, j)),
                          out_shape=jax.ShapeDtypeStruct((16, n), F32), compiler_params=_cp("parallel"))(c16, w_shard, b_shard)


def _gw_ada(c_t, dm, *, tr=256):
    n = dm.shape[1]

    def body(ct_ref, dm_ref, o_ref):
        acc = jnp.zeros((tr, n), F32)
        for b in range(NDEV):
            cv = ct_ref[:, b:b + 1]
            cond = (cv * jax.nn.sigmoid(cv)).astype(BF).astype(F32)
            acc = acc + cond * dm_ref[b:b + 1, :].astype(BF).astype(F32)
        o_ref[...] = acc

    return pl.pallas_call(body, name="gw_ada", grid=(D // tr,),
                          in_specs=[_row(tr, NDEV), _fix((NDEV, n))], out_specs=_row(tr, n),
                          out_shape=jax.ShapeDtypeStruct((D, n), F32), compiler_params=_cp("parallel"))(c_t, dm)


def _sum8(parts, *, name, tc=512):
    _, r, c = parts.shape
    tc = min(tc, c)

    def body(p_ref, o_ref):
        acc = p_ref[0].astype(F32)
        for d in range(1, NDEV):
            acc = acc + p_ref[d].astype(F32)
        o_ref[...] = acc

    return pl.pallas_call(body, name=name, grid=(c // tc,),
                          in_specs=[pl.BlockSpec((NDEV, r, tc), lambda j: (0, 0, j))],
                          out_specs=pl.BlockSpec((r, tc), lambda j: (0, j)),
                          out_shape=jax.ShapeDtypeStruct((r, c), F32), compiler_params=_cp("parallel"))(parts)


def _adam(w, m, v, *, recv=None, grad=None, tr=None, tc=None, name):
    lead = w.ndim == 3
    r, c = w.shape[-2:]
    summed = recv is not None
    assert tc is not None or r % tr == 0
    at = (0,) if lead else (Ellipsis,)

    def body(g_ref, w_ref, m_ref, v_ref, go_ref, do_ref, mo_ref, vo_ref):
        if summed:
            g = g_ref[0].astype(F32)
            for d in range(1, NDEV):
                g = g + g_ref[d].astype(F32)
        else:
            g = g_ref[...]
        go_ref[at] = g
        mn = B1 * m_ref[at] + (1.0 - B1) * g
        vn = B2 * v_ref[at] + (1.0 - B2) * (g * g)
        mo_ref[at] = mn
        vo_ref[at] = vn
        mh = mn / (1.0 - B1 ** STEP)
        vh = vn / (1.0 - B2 ** STEP)
        do_ref[at] = -LR * (mh / (jnp.sqrt(vh) + AEPS) + WD * w_ref[at])

    if tc is not None:
        assert not lead and c % tc == 0
        gspec = (pl.BlockSpec((NDEV, r, tc), lambda j: (0, 0, j)) if summed
                 else pl.BlockSpec((r, tc), lambda j: (0, j)))
        wspec = pl.BlockSpec((r, tc), lambda j: (0, j))
        return pl.pallas_call(body, name=name, grid=(c // tc,),
                              in_specs=[gspec, wspec, wspec, wspec], out_specs=[wspec] * 4,
                              out_shape=[jax.ShapeDtypeStruct(w.shape, F32)] * 4,
                              compiler_params=_cp("parallel"))(recv if summed else grad, w, m, v)
    gspec = pl.BlockSpec((NDEV, tr, c), lambda i: (0, i, 0)) if summed else _row(tr, c)
    wspec = pl.BlockSpec((1, tr, c), lambda i: (0, i, 0)) if lead else _row(tr, c)
    return pl.pallas_call(body, name=name, grid=(r // tr,),
                          in_specs=[gspec, wspec, wspec, wspec], out_specs=[wspec] * 4,
                          out_shape=[jax.ShapeDtypeStruct(w.shape, F32)] * 4,
                          compiler_params=_cp("parallel"))(recv if summed else grad, w, m, v)


def _blocks_to_cols(b):
    _, r, width = b.shape
    return jnp.transpose(b, (1, 0, 2)).reshape(r, NDEV * width)


_SMALL = ("b_ada", "g_mix", "rel_bias", "conv_w", "conv_b", "dt_bias", "a_log", "d_skip",
          "g_att_out", "g_ssd_out", "g_ffn", "g_final")


def _pad_rows(v):
    v = v.reshape(-1)
    return jnp.pad(v, (0, (-v.shape[0]) % LANE))


def kernel(x, c, w_ada, b_ada, g_mix, w_in, rel_bias, conv_w, conv_b, dt_bias, a_log, d_skip, g_att_out, g_ssd_out, w_out, g_ffn, w_gate, w_up, w_down, g_final, loss_target, m_w_ada, m_b_ada, m_g_mix, m_w_in, m_rel_bias, m_conv_w, m_conv_b, m_dt_bias, m_a_log, m_d_skip, m_g_att_out, m_g_ssd_out, m_w_out, m_g_ffn, m_w_gate, m_w_up, m_w_down, m_g_final, v_w_ada, v_b_ada, v_g_mix, v_w_in, v_rel_bias, v_conv_w, v_conv_b, v_dt_bias, v_a_log, v_d_skip, v_g_att_out, v_g_ssd_out, v_w_out, v_g_ffn, v_w_gate, v_w_up, v_w_down, v_g_final):
    me = 4 * lax.axis_index("x") + 2 * lax.axis_index("y") + lax.axis_index("c")
    wts = dict(w_ada=w_ada, b_ada=b_ada, g_mix=g_mix, w_in=w_in, rel_bias=rel_bias, conv_w=conv_w, conv_b=conv_b,
               dt_bias=dt_bias, a_log=a_log, d_skip=d_skip, g_att_out=g_att_out, g_ssd_out=g_ssd_out, w_out=w_out,
               g_ffn=g_ffn, w_gate=w_gate, w_up=w_up, w_down=w_down, g_final=g_final)
    mom = dict(w_ada=m_w_ada, b_ada=m_b_ada, g_mix=m_g_mix, w_in=m_w_in, rel_bias=m_rel_bias, conv_w=m_conv_w,
               conv_b=m_conv_b, dt_bias=m_dt_bias, a_log=m_a_log, d_skip=m_d_skip, g_att_out=m_g_att_out,
               g_ssd_out=m_g_ssd_out, w_out=m_w_out, g_ffn=m_g_ffn, w_gate=m_w_gate, w_up=m_w_up, w_down=m_w_down,
               g_final=m_g_final)
    var = dict(w_ada=v_w_ada, b_ada=v_b_ada, g_mix=v_g_mix, w_in=v_w_in, rel_bias=v_rel_bias, conv_w=v_conv_w,
               conv_b=v_conv_b, dt_bias=v_dt_bias, a_log=v_a_log, d_skip=v_d_skip, g_att_out=v_g_att_out,
               g_ssd_out=v_g_ssd_out, w_out=v_w_out, g_ffn=v_g_ffn, w_gate=v_w_gate, w_up=v_w_up, w_down=v_w_down,
               g_final=v_g_final)
    order = ("w_ada", "b_ada", "g_mix", "w_in", "rel_bias", "conv_w", "conv_b", "dt_bias", "a_log", "d_skip",
             "g_att_out", "g_ssd_out", "w_out", "g_ffn", "w_gate", "w_up", "w_down", "g_final")

    c_all, rel_all, cw_all, _ = _exchange([c, rel_bias[0], conv_w[0]], [True] * 3, "gather_small")
    c_all = c_all.reshape(NDEV, D)
    rel_full = _blocks_to_cols(rel_all)
    cw_full = _blocks_to_cols(cw_all)

    ncol = w_ada.shape[2]
    b_sh = lax.dynamic_slice(b_ada, (0, me * ncol), (1, ncol))
    mods_part = _mods(jnp.pad(c_all, ((0, 8), (0, 0))), w_ada[0], b_sh)[:NDEV]
    mods_all, tok_m = _exchange([mods_part], [True], "gather_mods")
    mods_me = lax.dynamic_index_in_dim(mods_all, me, axis=1, keepdims=False).reshape(6, 1, D)
    mods = [mods_me[i] for i in range(6)]

    st_in = _xstart([(w_in[0] + tok_m[0, 0]).T.astype(BF)], [True], "gather_w_in_start")
    st_late = dict(out=_xstart([(w_out[0] + st_in[-1][0, 0]).astype(BF)], [True], "gather_w_out_start"))
    st_late["gate_up"] = _xstart([(w_gate[0] + st_late["out"][-1][0, 0]).T.astype(BF), w_up[0].T.astype(BF)],
                                 [True] * 2, "gather_w_gate_up_start")
    st_late["down"] = _xstart([(w_down[0] + st_late["gate_up"][-1][0, 0]).astype(BF)], [True],
                              "gather_w_down_start")
    st_rest = st_late["down"]

    def first_weights(after):
        (gi,) = _xwait(st_in, [True], after, "gather_w_in_wait")
        return jnp.pad(gi.reshape(INC, D), ((0, INP - INC), (0, 0)))

    def late_weights(which, after):
        got = _xwait(st_late[which], [True] * (2 if which == "gate_up" else 1), after, f"gather_w_{which}_wait")
        return [g_.reshape(-1, D) for g_ in got]

    started = {}

    def grads_ready(which, *g):
        if which == "in":
            g = (g[0][:INC],)
        srcs = [g_.reshape(NDEV, g_.shape[0] // NDEV, D) for g_ in g]
        started[which] = _xstart(srcs, [False] * len(srcs), f"exchange_{which}_grads_start")
        return started[which][-1]

    gx, small = _local_step(
        x[0], loss_target[0], mods, first_weights, late_weights, grads_ready,
        g_mix + st_rest[-1][0, 0], rel_full, cw_full, conv_b, dt_bias, a_log, d_skip, g_att_out, g_ssd_out, g_ffn,
        g_final.reshape(1, D))

    loss_row = jnp.pad(jnp.sum(small["loss"]).reshape(1), (0, LANE - 1))
    pack = jnp.concatenate([small[k] for k in ("dmods", "g_mix", "conv_b", "dt_bias", "a_log", "d_skip", "g_att",
                                               "g_ssd", "g_ffn", "g_final", "rel_bias", "conv_w")] + [loss_row])
    packs, _ = _exchange([pack.reshape(-1, LANE)], [True], "gather_small_grads")
    rg, ru, rd = _xwait(started["ffn"], [False] * 3, packs, "exchange_ffn_grads_wait")
    (ro,) = _xwait(started["out"], [False], rg, "exchange_out_grads_wait")
    (ri,) = _xwait(started["in"], [False], ro, "exchange_in_grads_wait")
    big = dict(
        w_gate=[o_.T for o_ in _adam(w_gate[0].T, m_w_gate[0].T, v_w_gate[0].T, recv=rg, tc=512,
                                     name="adam_w_gate")],
        w_up=[o_.T for o_ in _adam(w_up[0].T, m_w_up[0].T, v_w_up[0].T, recv=ru, tc=512, name="adam_w_up")],
        w_down=_adam(w_down[0], m_w_down[0], v_w_down[0], recv=rd, tr=176, name="adam_w_down"))
    tot = _sum8(packs, name="sum_small").reshape(-1)

    sizes = dict(dmods=6 * D, g_mix=D, conv_b=XBC, dt_bias=LANE, a_log=LANE, d_skip=LANE, g_att_out=AW, g_ssd_out=SW,
                 g_ffn=D, g_final=D, rel_bias=NH * NREL, conv_w=4 * XBC, loss=LANE)
    off, o = {}, 0
    for k_, n_ in sizes.items():
        off[k_] = o
        o += n_
    take = lambda k_, n_=None: tot[off[k_]:off[k_] + (n_ or sizes[k_])]
    loss = tot[off["loss"]]
    rel_w = NREL // NDEV
    cw_w = XBC // NDEV
    gsm = dict(
        b_ada=take("dmods"), g_mix=take("g_mix"),
        rel_bias=lax.dynamic_slice(take("rel_bias").reshape(NH, NREL), (0, me * rel_w), (NH, rel_w)),
        conv_w=lax.dynamic_slice(take("conv_w").reshape(4, XBC), (0, me * cw_w), (4, cw_w)),
        conv_b=take("conv_b"), dt_bias=take("dt_bias", NH), a_log=take("a_log", NH), d_skip=take("d_skip", NH),
        g_att_out=take("g_att_out"), g_ssd_out=take("g_ssd_out"), g_ffn=take("g_ffn"), g_final=take("g_final"))

    flat = lambda d_: jnp.concatenate([_pad_rows(d_[k_]) for k_ in _SMALL])
    gflat = flat(gsm)
    nrow = -(-gflat.shape[0] // (8 * LANE)) * 8
    to2d = lambda v_: jnp.pad(v_, (0, nrow * LANE - v_.shape[0])).reshape(nrow, LANE)
    res = _adam(to2d(flat(wts)), to2d(flat(mom)), to2d(flat(var)), grad=to2d(gflat), tr=nrow, name="adam_small")
    outs = {k_: {} for k_ in ("grad", "delta", "m", "v")}
    o = 0
    for k_ in _SMALL:
        n_ = wts[k_].size
        for kind, arr in zip(("grad", "delta", "m", "v"), res):
            outs[kind][k_] = arr.reshape(-1)[o:o + n_].reshape(wts[k_].shape)
        o += n_ + (-n_) % LANE

    dm_all = packs[:, :6 * D // LANE, :].reshape(NDEV, 6 * D)
    dm_sh = lax.dynamic_slice(dm_all, (0, me * ncol), (NDEV, ncol))
    gwa = _gw_ada(c_all.T, dm_sh)
    big.update(
        w_ada=_adam(w_ada[0], m_w_ada[0], v_w_ada[0], grad=gwa, tr=256, name="adam_w_ada"),
        w_in=[o_.T for o_ in _adam(w_in[0].T, m_w_in[0].T, v_w_in[0].T, recv=ri, tc=512, name="adam_w_in")],
        w_out=_adam(w_out[0], m_w_out[0], v_w_out[0], recv=ro, tr=128, name="adam_w_out"))
    for k_, r_ in big.items():
        for kind, arr in zip(("grad", "delta", "m", "v"), r_):
            outs[kind][k_] = arr.reshape(wts[k_].shape)

    return (loss, gx.reshape(x.shape), *[outs["grad"][k_] for k_ in order], *[outs["delta"][k_] for k_ in order],
            *[outs["m"][k_] for k_ in order], *[outs["v"][k_] for k_ in order])
```
